```python
import jax, jax.numpy as jnp
from jax import lax
import numpy as np

D_MODEL = 2048
BATCH = 4
SEQ = 2048
DEPTH = 1
DEC_BATCH = 128
DEC_SEQ = 8
PAST_LEN = 16384
PAGE_SIZE = 128

N_META = 16
G_RWKV = D_MODEL // 2
HEAD_DIM = 64
N_HEADS = G_RWKV // HEAD_DIM
G_CONV = D_MODEL - G_RWKV
CONV_WIDTH = 3
W_LORA = max(32, int(round(1.8 * D_MODEL ** 0.5 / 32)) * 32)
A_LORA = max(32, int(round(1.8 * D_MODEL ** 0.5 / 32)) * 32)
G_LORA = max(32, int(round(0.6 * D_MODEL ** 0.8 / 32)) * 32)
RWKV_PROJ = 3 * G_RWKV + W_LORA + A_LORA + G_LORA
P_TOTAL = RWKV_PROJ + 3 * G_CONV
RWKV_SPLITS = [int(s) for s in np.cumsum([G_RWKV, W_LORA, G_RWKV, G_RWKV, A_LORA])]
D_FF = 5632
RMS_EPS = 1e-6
GN_EPS = 64e-5
F32 = jnp.float32

kernel_name = "hymba_rwkv7_shortconv_macaron_step"

LAYER_KEYS = ("g_ffn1", "ffn1_gate", "ffn1_up", "ffn1_down", "g_mix", "w_in", "mu_shift",
              "w0", "w_lora_w", "a0", "w_lora_a", "w_lora_g", "k_k", "k_a", "r_k",
              "ln_x_w", "ln_x_b", "conv_w", "w_out", "g_ffn2", "ffn2_gate", "ffn2_up", "ffn2_down")


def _rms_norm(x, g):
    xf = x.astype(F32)
    y = xf * lax.rsqrt(jnp.mean(xf * xf, axis=-1, keepdims=True) + RMS_EPS)
    return (y * g.astype(F32)).astype(x.dtype)


def _swiglu(h, w_gate, w_up, w_down):
    return (jax.nn.silu(h @ w_gate) * (h @ w_up)) @ w_down


def _wkv7_scan(S0, r, decay, k, v, kk, a):
    xs = tuple(jnp.moveaxis(t, 1, 0) for t in (r, decay, k, v, kk, a))

    def step(S, inp):
        r_t, w_t, k_t, v_t, kk_t, a_t = inp
        sa = jnp.einsum("bhij,bhj->bhi", S, -kk_t)
        S = (S * w_t[:, :, None, :] + sa[..., None] * (kk_t * a_t)[:, :, None, :]
             + v_t[..., None] * k_t[:, :, None, :])
        return S, jnp.einsum("bhij,bhj->bhi", S, r_t)

    S_T, y = lax.scan(step, S0, xs)
    return S_T, jnp.moveaxis(y, 0, 1)


def _mixer(h, wkv0, prev0, buf0, p):
    B, T, _ = h.shape
    proj = h @ p["w_in"]
    p_rwkv, p_conv = proj[..., :RWKV_PROJ], proj[..., RWKV_PROJ:]

    shifted = jnp.concatenate([prev0[:, None, :].astype(proj.dtype), p_rwkv[:, :-1]], axis=1)
    q = (p_rwkv + (shifted - p_rwkv) * p["mu_shift"]).astype(F32)
    new_prev = p_rwkv[:, -1]
    r, wd, k, v, ad, gd = jnp.split(q, RWKV_SPLITS, axis=-1)
    w_log = -jax.nn.softplus(-(p["w0"].astype(F32) + jnp.tanh(wd) @ p["w_lora_w"].astype(F32))) - 0.5
    decay = jnp.exp(-jnp.exp(w_log))
    a = jax.nn.sigmoid(p["a0"].astype(F32) + ad @ p["w_lora_a"].astype(F32))
    g = jax.nn.sigmoid(gd) @ p["w_lora_g"].astype(F32)
    heads = lambda t: t.reshape(B, T, N_HEADS, HEAD_DIM)
    kk = heads(k * p["k_k"].astype(F32))
    kk = kk / jnp.maximum(jnp.sqrt(jnp.sum(kk * kk, axis=-1, keepdims=True)), 1e-12)
    k = k * (1.0 + (a - 1.0) * p["k_a"].astype(F32))
    r_h, k_h, v_h = heads(r), heads(k), heads(v)
    S_T, y = _wkv7_scan(wkv0.astype(F32), r_h, heads(decay), k_h, v_h, kk, heads(a))
    mu = jnp.mean(y, axis=-1, keepdims=True)
    var = jnp.mean(jnp.square(y - mu), axis=-1, keepdims=True)
    y_n = ((y - mu) * lax.rsqrt(var + GN_EPS)).reshape(B, T, G_RWKV)
    y_n = y_n * p["ln_x_w"].astype(F32) + p["ln_x_b"].astype(F32)
    bonus = (jnp.sum(r_h * k_h * p["r_k"].astype(F32), axis=-1, keepdims=True) * v_h).reshape(B, T, G_RWKV)
    rwkv_out = ((y_n + bonus) * g).astype(h.dtype)

    b_gate, c_gate, x_in = jnp.split(p_conv, 3, axis=-1)
    u = c_gate * x_in
    full = jnp.concatenate([buf0.astype(u.dtype), u], axis=1)
    cw = p["conv_w"]
    conv = sum(cw[j] * full[:, j:j + T] for j in range(CONV_WIDTH))
    conv_out = (b_gate * conv).astype(h.dtype)
    new_buf = full[:, -(CONV_WIDTH - 1):]

    mix = jnp.concatenate([rwkv_out, conv_out], axis=-1) @ p["w_out"]
    return mix, S_T, new_prev, new_buf


def _layer(x, wkv0, prev0, buf0, p):
    x = x + 0.5 * _swiglu(_rms_norm(x, p["g_ffn1"]), p["ffn1_gate"], p["ffn1_up"], p["ffn1_down"])
    mix, wkv1, prev1, buf1 = _mixer(_rms_norm(x, p["g_mix"]), wkv0, prev0, buf0, p)
    x = x + mix
    x = x + 0.5 * _swiglu(_rms_norm(x, p["g_ffn2"]), p["ffn2_gate"], p["ffn2_up"], p["ffn2_down"])
    return x, wkv1, prev1, buf1


def _trunk(x, wkv, prev, buf, layers, g_final):
    new_wkv, new_prev, new_buf = [], [], []
    for i in range(DEPTH):
        x, s, pr, bf = _layer(x, wkv[i], prev[i], buf[i], layers[i])
        new_wkv.append(s); new_prev.append(pr); new_buf.append(bf)
    return _rms_norm(x, g_final), jnp.stack(new_wkv), jnp.stack(new_prev), jnp.stack(new_buf)


def setup_inputs(seed: int = 0) -> dict:
    key = jax.random.key(seed)
    ks = iter(jax.random.split(key, 40))
    nrm = lambda shape, s: jax.random.normal(next(ks), shape, F32) * s
    L = DEPTH
    return {
        "x_prompt": nrm((BATCH, SEQ, D_MODEL), 1.0),
        "x_sample": nrm((DEC_BATCH, DEC_SEQ, D_MODEL), 1.0),
        "state_wkv": nrm((L, DEC_BATCH, N_HEADS, HEAD_DIM, HEAD_DIM), 0.5),
        "state_shift": nrm((L, DEC_BATCH, RWKV_PROJ), 1.0),
        "state_conv": nrm((L, DEC_BATCH, CONV_WIDTH - 1, G_CONV), 1.0),
        "meta_tokens": nrm((N_META, D_MODEL), 1.0),
        "g_ffn1": 1.0 + nrm((L, D_MODEL), 0.05),
        "ffn1_gate": nrm((L, D_MODEL, D_FF), D_MODEL ** -0.5),
        "ffn1_up": nrm((L, D_MODEL, D_FF), D_MODEL ** -0.5),
        "ffn1_down": nrm((L, D_FF, D_MODEL), D_FF ** -0.5),
        "g_mix": 1.0 + nrm((L, D_MODEL), 0.05),
        "w_in": nrm((L, D_MODEL, P_TOTAL), D_MODEL ** -0.5),
        "mu_shift": jax.random.uniform(next(ks), (L, RWKV_PROJ), F32),
        "w0": -2.0 + nrm((L, G_RWKV), 0.5),
        "w_lora_w": nrm((L, W_LORA, G_RWKV), 0.5 * W_LORA ** -0.5),
        "a0": nrm((L, G_RWKV), 0.1),
        "w_lora_a": nrm((L, A_LORA, G_RWKV), 0.5 * A_LORA ** -0.5),
        "w_lora_g": nrm((L, G_LORA, G_RWKV), G_LORA ** -0.5),
        "k_k": 0.85 + nrm((L, G_RWKV), 0.05),
        "k_a": 1.0 + nrm((L, G_RWKV), 0.05),
        "r_k": nrm((L, N_HEADS, HEAD_DIM), 0.1),
        "ln_x_w": 1.0 + nrm((L, G_RWKV), 0.05),
        "ln_x_b": nrm((L, G_RWKV), 0.01),
        "conv_w": nrm((L, CONV_WIDTH, G_CONV), CONV_WIDTH ** -0.5),
        "w_out": nrm((L, D_MODEL, D_MODEL), D_MODEL ** -0.5),
        "g_ffn2": 1.0 + nrm((L, D_MODEL), 0.05),
        "ffn2_gate": nrm((L, D_MODEL, D_FF), D_MODEL ** -0.5),
        "ffn2_up": nrm((L, D_MODEL, D_FF), D_MODEL ** -0.5),
        "ffn2_down": nrm((L, D_FF, D_MODEL), D_FF ** -0.5),
        "g_final": 1.0 + nrm((D_MODEL,), 0.05),
    }


def reference(x_prompt, x_sample, state_wkv, state_shift, state_conv, meta_tokens,
              g_ffn1, ffn1_gate, ffn1_up, ffn1_down, g_mix, w_in, mu_shift, w0, w_lora_w,
              a0, w_lora_a, w_lora_g, k_k, k_a, r_k, ln_x_w, ln_x_b, conv_w, w_out,
              g_ffn2, ffn2_gate, ffn2_up, ffn2_down, g_final):
    layers = [
        {"g_ffn1": g_ffn1[i], "ffn1_gate": ffn1_gate[i], "ffn1_up": ffn1_up[i],
         "ffn1_down": ffn1_down[i], "g_mix": g_mix[i], "w_in": w_in[i], "mu_shift": mu_shift[i],
         "w0": w0[i], "w_lora_w": w_lora_w[i], "a0": a0[i], "w_lora_a": w_lora_a[i],
         "w_lora_g": w_lora_g[i], "k_k": k_k[i], "k_a": k_a[i], "r_k": r_k[i],
         "ln_x_w": ln_x_w[i], "ln_x_b": ln_x_b[i], "conv_w": conv_w[i], "w_out": w_out[i],
         "g_ffn2": g_ffn2[i], "ffn2_gate": ffn2_gate[i], "ffn2_up": ffn2_up[i],
         "ffn2_down": ffn2_down[i]}
        for i in range(DEPTH)]

    B = x_prompt.shape[0]
    meta = jnp.broadcast_to(meta_tokens.astype(x_prompt.dtype)[None], (B, N_META, D_MODEL))
    xp = jnp.concatenate([meta, x_prompt], axis=1)
    wkv_p0 = jnp.zeros((DEPTH, B, N_HEADS, HEAD_DIM, HEAD_DIM), F32)
    prev_p0 = jnp.zeros((DEPTH, B, RWKV_PROJ), F32)
    buf_p0 = jnp.zeros((DEPTH, B, CONV_WIDTH - 1, G_CONV), F32)
    yp, wkv_p, prev_p, buf_p = _trunk(xp, wkv_p0, prev_p0, buf_p0, layers, g_final)
    y_prompt = yp[:, N_META:]

    y_sample, wkv_s, prev_s, buf_s = _trunk(x_sample, state_wkv, state_shift, state_conv, layers, g_final)

    return (y_prompt, y_sample,
            wkv_p.astype(state_wkv.dtype), prev_p.astype(state_shift.dtype), buf_p.astype(state_conv.dtype),
            wkv_s.astype(state_wkv.dtype), prev_s.astype(state_shift.dtype), buf_s.astype(state_conv.dtype))
```

```python
import functools

import jax
import jax.numpy as jnp
from jax import lax
from jax.experimental import pallas as pl
from jax.experimental.pallas import tpu as pltpu

F32 = jnp.float32
BF16 = jnp.bfloat16

D_MODEL = 2048
D_FF = 5632
N_META = 16
G = 1024
HEAD_DIM = 64
N_HEADS = G // HEAD_DIM
W_LORA = 96
A_LORA = 96
G_LORA = 256
LORA_PAD = 128
RWKV_PROJ = 3 * G + W_LORA + A_LORA + G_LORA
RWKV_SPLITS = (G, G + W_LORA, 2 * G + W_LORA, 3 * G + W_LORA, 3 * G + W_LORA + A_LORA)
RW = 3 * G + 2 * LORA_PAD + G_LORA
PW = RW + 3 * G
OFF_WD, OFF_AD, OFF_GD = 3 * G, 3 * G + LORA_PAD, 3 * G + 2 * LORA_PAD
RMS_EPS = 1e-6
GN_EPS = 64e-5
EXP_M05 = 0.6065306597126334

C_PROMPT = 64
C_SAMPLE = 8
VMEM_LIMIT = 56 * 1024 * 1024


def _dot(a, b):
    return jnp.dot(a, b, preferred_element_type=F32)


def _dot_nt(a, b):
    return lax.dot_general(a, b, (((1,), (1,)), ((), ())), preferred_element_type=F32)


def _dot_tn(a, b):
    return lax.dot_general(a, b, (((0,), (0,)), ((), ())), preferred_element_type=F32)


def _split(a):
    hi = a.astype(BF16)
    lo = (a - hi.astype(F32)).astype(BF16)
    return hi, lo


def _dot_split_lhs(a, b_bf16):
    hi, lo = _split(a)
    return _dot(hi, b_bf16) + _dot(lo, b_bf16)


def _dot_split_rhs(a_bf16, b):
    hi, lo = _split(b)
    return _dot(a_bf16, hi) + _dot(a_bf16, lo)


def _rms(x, g):
    return x * lax.rsqrt(jnp.mean(x * x, axis=-1, keepdims=True) + RMS_EPS) * g


def _ffn_kernel(x_ref, g_ref, wg_ref, wu_ref, wd_ref, *rest, final_norm):
    if final_norm:
        gf_ref, o_ref, h_ref, acc_ref = rest
    else:
        o_ref, h_ref, acc_ref = rest
    j = pl.program_id(1)

    @pl.when(j == 0)
    def _():
        h_ref[...] = _rms(x_ref[...], g_ref[...]).astype(BF16)

    h = h_ref[...]
    gate = _dot(h, wg_ref[...])
    up = _dot(h, wu_ref[...])
    act = (gate * jax.nn.sigmoid(gate) * up).astype(BF16)
    part = _dot(act, wd_ref[...])

    @pl.when(j == 0)
    def _():
        acc_ref[...] = part

    @pl.when(j > 0)
    def _():
        acc_ref[...] += part

    @pl.when(j == pl.num_programs(1) - 1)
    def _():
        xo = x_ref[...] + 0.5 * acc_ref[...]
        if final_norm:
            xo = _rms(xo, gf_ref[...])
        o_ref[...] = xo


def _ffn(x, g, wg, wu, wd, g_final, *, tm, tf, name):
    t = x.shape[0]
    final_norm = g_final is not None
    in_specs = [
        pl.BlockSpec((tm, D_MODEL), lambda i, j: (i, 0)),
        pl.BlockSpec((1, D_MODEL), lambda i, j: (0, 0)),
        pl.BlockSpec((D_MODEL, tf), lambda i, j: (0, j)),
        pl.BlockSpec((D_MODEL, tf), lambda i, j: (0, j)),
        pl.BlockSpec((tf, D_MODEL), lambda i, j: (j, 0)),
    ]
    args = [x, g, wg, wu, wd]
    if final_norm:
        in_specs.append(pl.BlockSpec((1, D_MODEL), lambda i, j: (0, 0)))
        args.append(g_final)
    return pl.pallas_call(
        functools.partial(_ffn_kernel, final_norm=final_norm),
        grid=(t // tm, D_FF // tf),
        in_specs=in_specs,
        out_specs=pl.BlockSpec((tm, D_MODEL), lambda i, j: (i, 0)),
        out_shape=jax.ShapeDtypeStruct((t, D_MODEL), F32),
        scratch_shapes=[pltpu.VMEM((tm, D_MODEL), BF16), pltpu.VMEM((tm, D_MODEL), F32)],
        compiler_params=pltpu.CompilerParams(
            dimension_semantics=("parallel", "arbitrary"), vmem_limit_bytes=VMEM_LIMIT),
        name=name,
    )(*args)


def _proj_kernel(x_ref, g_ref, w_ref, o_ref, h_ref):
    @pl.when(pl.program_id(1) == 0)
    def _():
        h_ref[...] = _rms(x_ref[...], g_ref[...]).astype(BF16)

    o_ref[...] = _dot(h_ref[...], w_ref[...])


def _proj(x, g, w, *, tm, tn):
    t = x.shape[0]
    return pl.pallas_call(
        _proj_kernel,
        grid=(t // tm, PW // tn),
        in_specs=[
            pl.BlockSpec((tm, D_MODEL), lambda i, j: (i, 0)),
            pl.BlockSpec((1, D_MODEL), lambda i, j: (0, 0)),
            pl.BlockSpec((D_MODEL, tn), lambda i, j: (0, j)),
        ],
        out_specs=pl.BlockSpec((tm, tn), lambda i, j: (i, j)),
        out_shape=jax.ShapeDtypeStruct((t, PW), F32),
        scratch_shapes=[pltpu.VMEM((tm, D_MODEL), BF16)],
        compiler_params=pltpu.CompilerParams(
            dimension_semantics=("parallel", "arbitrary"), vmem_limit_bytes=VMEM_LIMIT),
        name="proj",
    )(x, g, w)


def _prep_kernel(*refs, rows, chunk, sample):
    if sample:
        (p_ref, prev_ref, up1_ref, up2_ref, mu_ref, w0_ref, ww_ref, a0_ref, wa_ref, wgl_ref,
         kk_ref, ka_ref, rk_ref, cw_ref, bd_ref, tri_ref,
         ah_ref, rh_ref, bh_ref, kh_ref, v_ref, wc_ref, g_ref, bonus_ref, conv_ref, u_ref) = refs
    else:
        (p_ref, mu_ref, w0_ref, ww_ref, a0_ref, wa_ref, wgl_ref,
         kk_ref, ka_ref, rk_ref, cw_ref, bd_ref, tri_ref,
         ah_ref, rh_ref, bh_ref, kh_ref, v_ref, wc_ref, g_ref, bonus_ref, conv_ref, u_ref,
         carry_p, carry_u) = refs

        @pl.when(pl.program_id(1) == 0)
        def _():
            carry_p[...] = jnp.zeros_like(carry_p)
            carry_u[...] = jnp.zeros_like(carry_u)

    def row_ids(width):
        r = lax.broadcasted_iota(jnp.int32, (rows, width), 0)
        return (r & 7) if sample else r

    def shifted_mix(lo, hi):
        p = p_ref[:, lo:hi]
        rolled = pltpu.roll(p, 1, 0)
        if sample:
            sh = jnp.where(row_ids(hi - lo) == 0, prev_ref[:, lo:hi], rolled)
        else:
            sh = jnp.where(row_ids(hi - lo) == 0, carry_p[0:1, lo:hi], rolled)
        return p + (sh - p) * mu_ref[:, lo:hi]

    r = shifted_mix(0, G)
    k = shifted_mix(G, 2 * G)
    v = shifted_mix(2 * G, 3 * G)
    wd = shifted_mix(OFF_WD, OFF_WD + LORA_PAD)
    ad = shifted_mix(OFF_AD, OFF_AD + LORA_PAD)
    gd = shifted_mix(OFF_GD, OFF_GD + G_LORA)

    z = w0_ref[...] + _dot(jnp.tanh(wd).astype(BF16), ww_ref[...])
    lw = -EXP_M05 * jax.nn.sigmoid(z)
    a = jax.nn.sigmoid(a0_ref[...] + _dot(ad.astype(BF16), wa_ref[...]))
    g_ref[...] = _dot(jax.nn.sigmoid(gd).astype(BF16), wgl_ref[...])

    bd = bd_ref[...]
    kk = k * kk_ref[...]
    norm = jnp.sqrt(_dot_split_lhs(kk * kk, bd))
    kk = kk / jnp.maximum(norm, 1e-12)
    km = k * (1.0 + (a - 1.0) * ka_ref[...])
    bonus_ref[...] = _dot_split_lhs(r * km * rk_ref[...], bd) * v

    cum = _dot_split_rhs(tri_ref[...], lw)
    e_cum = jnp.exp(cum)
    e_inv = jnp.exp(-cum)
    ah_ref[...] = -kk * jnp.exp(cum - lw)
    rh_ref[...] = r * e_cum
    bh_ref[...] = kk * a * e_inv
    kh_ref[...] = km * e_inv
    v_ref[...] = v
    for c in range(rows // chunk):
        wc_ref[c] = e_cum[(c + 1) * chunk - 1:(c + 1) * chunk, :]

    bg = p_ref[:, RW:RW + G]
    u = p_ref[:, RW + G:RW + 2 * G] * p_ref[:, RW + 2 * G:RW + 3 * G]
    u1 = pltpu.roll(u, 1, 0)
    u2 = pltpu.roll(u, 2, 0)
    rid = row_ids(G)
    if sample:
        um1 = jnp.where(rid == 0, up1_ref[...], u1)
        um2 = jnp.where(rid < 2, up2_ref[...], u2)
        u_ref[...] = u
    else:
        um1 = jnp.where(rid == 0, carry_u[1:2, :], u1)
        um2 = jnp.where(rid == 0, carry_u[0:1, :], jnp.where(rid == 1, carry_u[1:2, :], u2))
        u_ref[0] = u[rows - 8:, :]
    conv = cw_ref[0:1, :] * um2 + cw_ref[1:2, :] * um1 + cw_ref[2:3, :] * u
    conv_ref[...] = (bg * conv).astype(BF16)

    if not sample:
        carry_p[0:1, :] = p_ref[rows - 1:rows, 0:RW]
        carry_u[0:2, :] = u[rows - 2:, :]


def _prep(proj, weights, tri, *, n_seq, seq_len, rows, chunk, row_block_offset, sample, extra=()):
    t = n_seq * seq_len
    n_chunks = t // chunk
    cpt = rows // chunk
    if sample:
        grid = (t // rows,)
        rmap = lambda i: (i + row_block_offset, 0)
        omap = lambda i: (i, 0)
        omap3 = lambda i: (i, 0, 0)
        cmap = lambda i: (0, 0)
        sem = ("parallel",)
    else:
        tiles = seq_len // rows
        grid = (n_seq, tiles)
        rmap = lambda b, j: (b * tiles + j + row_block_offset, 0)
        omap = lambda b, j: (b * tiles + j, 0)
        omap3 = lambda b, j: (b * tiles + j, 0, 0)
        cmap = lambda b, j: (0, 0)
        sem = ("parallel", "arbitrary")

    const = lambda arr: pl.BlockSpec(arr.shape, cmap)
    in_specs = [pl.BlockSpec((rows, PW), rmap)]
    args = [proj]
    if sample:
        prev, up1, up2 = extra
        in_specs += [pl.BlockSpec((rows, RW), omap), pl.BlockSpec((rows, G), omap),
                     pl.BlockSpec((rows, G), omap)]
        args += [prev, up1, up2]
    in_specs += [const(w) for w in weights] + [const(tri)]
    args += list(weights) + [tri]

    row_out = jax.ShapeDtypeStruct((t, G), F32)
    row_spec = pl.BlockSpec((rows, G), omap)
    out_shape = [row_out] * 5 + [jax.ShapeDtypeStruct((n_chunks, 1, G), F32), row_out, row_out,
                                 jax.ShapeDtypeStruct((t, G), BF16)]
    out_specs = [row_spec] * 5 + [pl.BlockSpec((cpt, 1, G), omap3), row_spec, row_spec, row_spec]
    if sample:
        out_shape.append(row_out)
        out_specs.append(row_spec)
        scratch = []
    else:
        out_shape.append(jax.ShapeDtypeStruct((t // rows, 8, G), F32))
        out_specs.append(pl.BlockSpec((1, 8, G), omap3))
        scratch = [pltpu.VMEM((8, RW), F32), pltpu.VMEM((8, G), F32)]
    return pl.pallas_call(
        functools.partial(_prep_kernel, rows=rows, chunk=chunk, sample=sample),
        grid=grid, in_specs=in_specs, out_specs=out_specs, out_shape=out_shape,
        scratch_shapes=scratch,
        compiler_params=pltpu.CompilerParams(dimension_semantics=sem, vmem_limit_bytes=VMEM_LIMIT),
        name="prep_sample" if sample else "prep_prompt",
    )(*args)


def _scan_kernel(*refs, chunk, sample):
    if sample:
        ah_ref, rh_ref, bh_ref, kh_ref, v_ref, wc_ref, s0_ref, y_ref, s_ref = refs
        s_ref[...] = s0_ref[...]
    else:
        ah_ref, rh_ref, bh_ref, kh_ref, v_ref, wc_ref, y_ref, s_ref = refs

        @pl.when(pl.program_id(1) == 0)
        def _():
            s_ref[...] = jnp.zeros_like(s_ref)

    op = (lambda x: x.astype(BF16)) if chunk >= 16 else (lambda x: x)
    ri = lax.broadcasted_iota(jnp.int32, (chunk, chunk), 0)
    ci = lax.broadcasted_iota(jnp.int32, (chunk, chunk), 1)
    strict = ri > ci
    incl = ri >= ci
    eye = (ri == ci).astype(F32)
    wc = wc_ref[0]

    for h in range(N_HEADS):
        sl = slice(h * HEAD_DIM, (h + 1) * HEAD_DIM)
        a_h = op(ah_ref[:, sl])
        r_h = op(rh_ref[:, sl])
        b_h = op(bh_ref[:, sl])
        k_h = op(kh_ref[:, sl])
        v_h = op(v_ref[:, sl])
        s = s_ref[0, h]
        s_op = op(s)

        l_ab = jnp.where(strict, _dot_nt(a_h, b_h), 0.0)
        l_ak = jnp.where(strict, _dot_nt(a_h, k_h), 0.0)
        l_rb = jnp.where(incl, _dot_nt(r_h, b_h), 0.0)
        l_rk = jnp.where(incl, _dot_nt(r_h, k_h), 0.0)

        t_inv = eye + l_ab
        pw = l_ab
        n = 1
        while 2 * n < chunk:
            pw = _dot(op(pw), op(pw))
            t_inv = t_inv + _dot(op(t_inv), op(pw))
            n *= 2

        x = _dot_nt(a_h, s_op) + _dot(op(l_ak), v_h)
        u = _dot(op(t_inv), op(x))
        u_op = op(u)
        y = _dot_nt(r_h, s_op) + _dot(op(l_rb), u_op) + _dot(op(l_rk), v_h)
        y_ref[:, sl] = y
        s_ref[0, h] = (s + _dot_tn(u_op, b_h) + _dot_tn(v_h, k_h)) * wc[:, sl]


def _scan(ah, rh, bh, kh, v, wc, s0, *, n_seq, seq_len, chunk, sample):
    t = n_seq * seq_len
    n_chunks = seq_len // chunk
    if sample:
        grid = (n_seq,)
        rmap = lambda b: (b, 0)
        cmap = lambda b: (b, 0, 0)
        smap = lambda b: (b, 0, 0, 0)
        sem = ("parallel",)
    else:
        grid = (n_seq, n_chunks)
        rmap = lambda b, c: (b * n_chunks + c, 0)
        cmap = lambda b, c: (b * n_chunks + c, 0, 0)
        smap = lambda b, c: (b, 0, 0, 0)
        sem = ("parallel", "arbitrary")
    row_spec = pl.BlockSpec((chunk, G), rmap)
    state_spec = pl.BlockSpec((1, N_HEADS, HEAD_DIM, HEAD_DIM), smap)
    in_specs = [row_spec] * 5 + [pl.BlockSpec((1, 1, G), cmap)]
    args = [ah, rh, bh, kh, v, wc]
    if sample:
        in_specs.append(state_spec)
        args.append(s0)
    return pl.pallas_call(
        functools.partial(_scan_kernel, chunk=chunk, sample=sample),
        grid=grid, in_specs=in_specs,
        out_specs=[row_spec, state_spec],
        out_shape=[jax.ShapeDtypeStruct((t, G), F32),
                   jax.ShapeDtypeStruct((n_seq, N_HEADS, HEAD_DIM, HEAD_DIM), F32)],
        compiler_params=pltpu.CompilerParams(dimension_semantics=sem, vmem_limit_bytes=VMEM_LIMIT),
        name="scan_sample" if sample else "scan_prompt",
    )(*args)


def _mix_kernel(yp_ref, bp_ref, gp_ref, cp_ref, ys_ref, bs_ref, gs_ref, cs_ref,
                x_ref, lnw_ref, lnb_ref, bd_ref, wo_ref, o_ref, *, prompt_tiles):
    i = pl.program_id(0)

    def body(y_ref, bonus_ref, g_ref, conv_ref):
        bd = bd_ref[...]
        y = y_ref[...]
        mu = _dot_split_lhs(y, bd) * (1.0 / HEAD_DIM)
        d = y - mu
        var = _dot_split_lhs(d * d, bd) * (1.0 / HEAD_DIM)
        yn = d * lax.rsqrt(var + GN_EPS) * lnw_ref[...] + lnb_ref[...]
        rw = ((yn + bonus_ref[...]) * g_ref[...]).astype(BF16)
        mix = _dot(rw, wo_ref[0:G, :]) + _dot(conv_ref[...], wo_ref[G:2 * G, :])
        o_ref[...] = x_ref[...] + mix

    @pl.when(i < prompt_tiles)
    def _():
        body(yp_ref, bp_ref, gp_ref, cp_ref)

    @pl.when(i >= prompt_tiles)
    def _():
        body(ys_ref, bs_ref, gs_ref, cs_ref)


def _mix(prompt_rows, sample_rows, x1, lnw, lnb, bd, wo, *, tm):
    t = x1.shape[0]
    tp = prompt_rows[0].shape[0] // tm
    ts = sample_rows[0].shape[0] // tm
    pmap = lambda i: (jnp.minimum(i, tp - 1), 0)
    smap = lambda i: (jnp.maximum(i - tp, 0), 0)
    cmap = lambda i: (0, 0)
    in_specs = ([pl.BlockSpec((tm, G), pmap)] * 4 + [pl.BlockSpec((tm, G), smap)] * 4 + [
        pl.BlockSpec((tm, D_MODEL), lambda i: (i, 0)),
        pl.BlockSpec((1, G), cmap), pl.BlockSpec((1, G), cmap),
        pl.BlockSpec((G, G), cmap), pl.BlockSpec((D_MODEL, D_MODEL), cmap)])
    return pl.pallas_call(
        functools.partial(_mix_kernel, prompt_tiles=tp),
        grid=(tp + ts,), in_specs=in_specs,
        out_specs=pl.BlockSpec((tm, D_MODEL), lambda i: (i, 0)),
        out_shape=jax.ShapeDtypeStruct((t, D_MODEL), F32),
        compiler_params=pltpu.CompilerParams(
            dimension_semantics=("parallel",), vmem_limit_bytes=VMEM_LIMIT),
        name="mix",
    )(*prompt_rows, *sample_rows, x1, lnw, lnb, bd, wo)


def _pack_rwkv(a):
    r, wd, k, v, ad, gd = jnp.split(a, RWKV_SPLITS, axis=-1)
    zw = jnp.zeros(a.shape[:-1] + (LORA_PAD - W_LORA,), a.dtype)
    za = jnp.zeros(a.shape[:-1] + (LORA_PAD - A_LORA,), a.dtype)
    return jnp.concatenate([r, k, v, wd, zw, ad, za, gd], axis=-1)


def _unpack_rwkv(p):
    return jnp.concatenate([
        p[..., 0:G], p[..., OFF_WD:OFF_WD + W_LORA], p[..., G:2 * G], p[..., 2 * G:3 * G],
        p[..., OFF_AD:OFF_AD + A_LORA], p[..., OFF_GD:OFF_GD + G_LORA]], axis=-1)


def _block_tri(n, chunk):
    i = jnp.arange(n)
    return ((i[:, None] // chunk == i[None, :] // chunk) & (i[:, None] >= i[None, :])).astype(BF16)


def kernel(x_prompt, x_sample, state_wkv, state_shift, state_conv, meta_tokens, g_ffn1, ffn1_gate, ffn1_up, ffn1_down, g_mix, w_in, mu_shift, w0, w_lora_w, a0, w_lora_a, w_lora_g, k_k, k_a, r_k, ln_x_w, ln_x_b, conv_w, w_out, g_ffn2, ffn2_gate, ffn2_up, ffn2_down, g_final):
    assert g_ffn1.shape[0] == 1, "single layer"
    nb, seq, _ = x_prompt.shape
    db, dseq, _ = x_sample.shape
    assert dseq == C_SAMPLE
    lp = N_META + seq
    pad = (-lp) % C_PROMPT
    lpp = lp + pad
    tp, ts = nb * lpp, db * dseq
    t = tp + ts

    row = lambda a: a.reshape(1, -1).astype(F32)
    w_in_p = jnp.concatenate([_pack_rwkv(w_in[0][:, :RWKV_PROJ]), w_in[0][:, RWKV_PROJ:]], axis=-1).astype(BF16)
    mu_p = _pack_rwkv(mu_shift[0])[None]
    pad_rows = lambda w, n: jnp.concatenate([w, jnp.zeros((n - w.shape[0], w.shape[1]), w.dtype)], axis=0)
    prep_w = (mu_p, row(w0[0]), pad_rows(w_lora_w[0], LORA_PAD).astype(BF16), row(a0[0]),
              pad_rows(w_lora_a[0], LORA_PAD).astype(BF16), w_lora_g[0].astype(BF16),
              row(k_k[0]), row(k_a[0]), row(r_k[0]), conv_w[0].astype(F32))
    hid = jnp.arange(G) // HEAD_DIM
    bd = (hid[:, None] == hid[None, :]).astype(BF16)
    prep_w = prep_w + (bd,)

    meta = jnp.broadcast_to(meta_tokens.astype(F32)[None], (nb, N_META, D_MODEL))
    xp = jnp.concatenate([jnp.zeros((nb, pad, D_MODEL), F32), meta, x_prompt], axis=1)
    x_all = jnp.concatenate([xp.reshape(tp, D_MODEL), x_sample.reshape(ts, D_MODEL)], axis=0)

    tm = t // 16
    x1 = _ffn(x_all, row(g_ffn1[0]), ffn1_gate[0].astype(BF16), ffn1_up[0].astype(BF16),
              ffn1_down[0].astype(BF16), None, tm=tm, tf=512, name="ffn1")
    proj = _proj(x1, row(g_mix[0]), w_in_p, tm=tm, tn=512)

    rows_p = 3 * C_PROMPT
    (ah, rh, bh, kh, vv, wc, g_p, bonus_p, conv_p, utail_p) = _prep(
        proj, prep_w, _block_tri(rows_p, C_PROMPT), n_seq=nb, seq_len=lpp, rows=rows_p,
        chunk=C_PROMPT, row_block_offset=0, sample=False)
    y_p, wkv_p = _scan(ah, rh, bh, kh, vv, wc, None, n_seq=nb, seq_len=lpp, chunk=C_PROMPT, sample=False)

    rows_s = 256
    first = lambda a: jnp.zeros((db, dseq) + a.shape[1:], F32).at[:, 0].set(a)
    prev = first(_pack_rwkv(state_shift[0])).reshape(ts, RW)
    up1 = first(state_conv[0][:, 1]).reshape(ts, G)
    up2 = first(state_conv[0][:, 0]).at[:, 1].set(state_conv[0][:, 1]).reshape(ts, G)
    (ah, rh, bh, kh, vv, wc, g_s, bonus_s, conv_s, u_s) = _prep(
        proj, prep_w, _block_tri(rows_s, C_SAMPLE), n_seq=db, seq_len=dseq, rows=rows_s,
        chunk=C_SAMPLE, row_block_offset=tp // rows_s, sample=True, extra=(prev, up1, up2))
    y_s, wkv_s = _scan(ah, rh, bh, kh, vv, wc, state_wkv[0], n_seq=db, seq_len=dseq, chunk=C_SAMPLE, sample=True)

    x2 = _mix((y_p, bonus_p, g_p, conv_p), (y_s, bonus_s, g_s, conv_s), x1,
              row(ln_x_w[0]), row(ln_x_b[0]), bd, w_out[0].astype(BF16), tm=256)
    y_all = _ffn(x2, row(g_ffn2[0]), ffn2_gate[0].astype(BF16), ffn2_up[0].astype(BF16),
                 ffn2_down[0].astype(BF16), row(g_final), tm=tm, tf=512, name="ffn2")

    y_prompt = y_all[:tp].reshape(nb, lpp, D_MODEL)[:, pad + N_META:]
    y_sample = y_all[tp:].reshape(db, dseq, D_MODEL)
    proj_p = proj[:tp].reshape(nb, lpp, PW)
    shift_p = _unpack_rwkv(proj_p[:, -1, :RW])
    conv_state_p = utail_p.reshape(nb, lpp // rows_p, 8, G)[:, -1, 6:, :]
    proj_s = proj[tp:].reshape(db, dseq, PW)
    shift_s = _unpack_rwkv(proj_s[:, -1, :RW])
    conv_state_s = u_s.reshape(db, dseq, G)[:, -2:, :]
    return (y_prompt, y_sample,
            wkv_p[None].astype(state_wkv.dtype), shift_p[None].astype(state_shift.dtype),
            conv_state_p[None].astype(state_conv.dtype),
            wkv_s[None].astype(state_wkv.dtype), shift_s[None].astype(state_shift.dtype),
            conv_state_s[None].astype(state_conv.dtype))
```

```python
import functools

import jax
import jax.numpy as jnp
from jax import lax
from jax.experimental import pallas as pl
from jax.experimental.pallas import tpu as pltpu

F32 = jnp.float32
BF16 = jnp.bfloat16

D_MODEL = 2048
D_FF = 5632
N_META = 16
G = 1024
HEAD_DIM = 64
N_HEADS = G // HEAD_DIM
W_LORA = 96
A_LORA = 96
G_LORA = 256
LORA_PAD = 128
RWKV_PROJ = 3 * G + W_LORA + A_LORA + G_LORA
RWKV_SPLITS = (G, G + W_LORA, 2 * G + W_LORA, 3 * G + W_LORA, 3 * G + W_LORA + A_LORA)
RW = 3 * G + 2 * LORA_PAD + G_LORA
PW = RW + 3 * G
OFF_WD, OFF_AD, OFF_GD = 3 * G, 3 * G + LORA_PAD, 3 * G + 2 * LORA_PAD
RMS_EPS = 1e-6
GN_EPS = 64e-5
EXP_M05 = 0.6065306597126334

C_PROMPT = 64
C_SAMPLE = 8
VMEM_LIMIT = 56 * 1024 * 1024


def _dot(a, b):
    return jnp.dot(a, b, preferred_element_type=F32)


def _dot_nt(a, b):
    return lax.dot_general(a, b, (((1,), (1,)), ((), ())), preferred_element_type=F32)


def _dot_tn(a, b):
    return lax.dot_general(a, b, (((0,), (0,)), ((), ())), preferred_element_type=F32)


def _split(a):
    hi = a.astype(BF16)
    lo = (a - hi.astype(F32)).astype(BF16)
    return hi, lo


def _dot_split_lhs(a, b_bf16):
    hi, lo = _split(a)
    return _dot(hi, b_bf16) + _dot(lo, b_bf16)


def _dot_split_rhs(a_bf16, b):
    hi, lo = _split(b)
    return _dot(a_bf16, hi) + _dot(a_bf16, lo)


def _rms(x, g):
    return x * lax.rsqrt(jnp.mean(x * x, axis=-1, keepdims=True) + RMS_EPS) * g


def _ffn_kernel(x_ref, g_ref, wg_ref, wu_ref, wd_ref, *rest, final_norm):
    if final_norm:
        gf_ref, o_ref, h_ref, acc_ref = rest
    else:
        o_ref, h_ref, acc_ref = rest
    j = pl.program_id(1)

    @pl.when(j == 0)
    def _():
        h_ref[...] = _rms(x_ref[...], g_ref[...]).astype(BF16)

    h = h_ref[...]
    gate = _dot(h, wg_ref[...])
    up = _dot(h, wu_ref[...])
    act = (gate * jax.nn.sigmoid(gate) * up).astype(BF16)
    part = _dot(act, wd_ref[...])

    @pl.when(j == 0)
    def _():
        acc_ref[...] = part

    @pl.when(j > 0)
    def _():
        acc_ref[...] += part

    @pl.when(j == pl.num_programs(1) - 1)
    def _():
        xo = x_ref[...] + 0.5 * acc_ref[...]
        if final_norm:
            xo = _rms(xo, gf_ref[...])
        o_ref[...] = xo


def _ffn(x, g, wg, wu, wd, g_final, *, tm, tf, name):
    t = x.shape[0]
    final_norm = g_final is not None
    in_specs = [
        pl.BlockSpec((tm, D_MODEL), lambda i, j: (i, 0)),
        pl.BlockSpec((1, D_MODEL), lambda i, j: (0, 0)),
        pl.BlockSpec((D_MODEL, tf), lambda i, j: (0, j)),
        pl.BlockSpec((D_MODEL, tf), lambda i, j: (0, j)),
        pl.BlockSpec((tf, D_MODEL), lambda i, j: (j, 0)),
    ]
    args = [x, g, wg, wu, wd]
    if final_norm:
        in_specs.append(pl.BlockSpec((1, D_MODEL), lambda i, j: (0, 0)))
        args.append(g_final)
    return pl.pallas_call(
        functools.partial(_ffn_kernel, final_norm=final_norm),
        grid=(t // tm, D_FF // tf),
        in_specs=in_specs,
        out_specs=pl.BlockSpec((tm, D_MODEL), lambda i, j: (i, 0)),
        out_shape=jax.ShapeDtypeStruct((t, D_MODEL), F32),
        scratch_shapes=[pltpu.VMEM((tm, D_MODEL), BF16), pltpu.VMEM((tm, D_MODEL), F32)],
        compiler_params=pltpu.CompilerParams(
            dimension_semantics=("parallel", "arbitrary"), vmem_limit_bytes=VMEM_LIMIT),
        name=name,
    )(*args)


def _proj_kernel(x_ref, g_ref, w_ref, o_ref, h_ref):
    @pl.when(pl.program_id(1) == 0)
    def _():
        h_ref[...] = _rms(x_ref[...], g_ref[...]).astype(BF16)

    o_ref[...] = _dot(h_ref[...], w_ref[...])


def _proj(x, g, w, *, tm, tn):
    t = x.shape[0]
    return pl.pallas_call(
        _proj_kernel,
        grid=(t // tm, PW // tn),
        in_specs=[
            pl.BlockSpec((tm, D_MODEL), lambda i, j: (i, 0)),
            pl.BlockSpec((1, D_MODEL), lambda i, j: (0, 0)),
            pl.BlockSpec((D_MODEL, tn), lambda i, j: (0, j)),
        ],
        out_specs=pl.BlockSpec((tm, tn), lambda i, j: (i, j)),
        out_shape=jax.ShapeDtypeStruct((t, PW), F32),
        scratch_shapes=[pltpu.VMEM((tm, D_MODEL), BF16)],
        compiler_params=pltpu.CompilerParams(
            dimension_semantics=("parallel", "arbitrary"), vmem_limit_bytes=VMEM_LIMIT),
        name="proj",
    )(x, g, w)


def _prep_kernel(*refs, rows, chunk, sample):
    if sample:
        (p_ref, prev_ref, up1_ref, up2_ref, mu_ref, w0_ref, ww_ref, a0_ref, wa_ref, wgl_ref,
         kk_ref, ka_ref, rk_ref, cw_ref, bd_ref, tri_ref,
         ah_ref, rh_ref, bh_ref, kh_ref, v_ref, wc_ref, g_ref, bonus_ref, conv_ref, u_ref) = refs
    else:
        (p_ref, mu_ref, w0_ref, ww_ref, a0_ref, wa_ref, wgl_ref,
         kk_ref, ka_ref, rk_ref, cw_ref, bd_ref, tri_ref,
         ah_ref, rh_ref, bh_ref, kh_ref, v_ref, wc_ref, g_ref, bonus_ref, conv_ref, u_ref,
         carry_p, carry_u) = refs

        @pl.when(pl.program_id(1) == 0)
        def _():
            carry_p[...] = jnp.zeros_like(carry_p)
            carry_u[...] = jnp.zeros_like(carry_u)

    def row_ids(width):
        r = lax.broadcasted_iota(jnp.int32, (rows, width), 0)
        return (r & 7) if sample else r

    def shifted_mix(lo, hi):
        p = p_ref[:, lo:hi]
        rolled = pltpu.roll(p, 1, 0)
        if sample:
            sh = jnp.where(row_ids(hi - lo) == 0, prev_ref[:, lo:hi], rolled)
        else:
            sh = jnp.where(row_ids(hi - lo) == 0, carry_p[0:1, lo:hi], rolled)
        return p + (sh - p) * mu_ref[:, lo:hi]

    r = shifted_mix(0, G)
    k = shifted_mix(G, 2 * G)
    v = shifted_mix(2 * G, 3 * G)
    wd = shifted_mix(OFF_WD, OFF_WD + LORA_PAD)
    ad = shifted_mix(OFF_AD, OFF_AD + LORA_PAD)
    gd = shifted_mix(OFF_GD, OFF_GD + G_LORA)

    z = w0_ref[...] + _dot(jnp.tanh(wd).astype(BF16), ww_ref[...])
    lw = -EXP_M05 * jax.nn.sigmoid(z)
    a = jax.nn.sigmoid(a0_ref[...] + _dot(ad.astype(BF16), wa_ref[...]))
    g_ref[...] = _dot(jax.nn.sigmoid(gd).astype(BF16), wgl_ref[...])

    bd = bd_ref[...]
    kk = k * kk_ref[...]
    norm = jnp.sqrt(_dot_split_lhs(kk * kk, bd))
    kk = kk / jnp.maximum(norm, 1e-12)
    km = k * (1.0 + (a - 1.0) * ka_ref[...])
    bonus_ref[...] = _dot_split_lhs(r * km * rk_ref[...], bd) * v

    cum = _dot_split_rhs(tri_ref[...], lw)
    e_cum = jnp.exp(cum)
    e_inv = jnp.exp(-cum)
    ah_ref[...] = -kk * jnp.exp(cum - lw)
    rh_ref[...] = r * e_cum
    bh_ref[...] = kk * a * e_inv
    kh_ref[...] = km * e_inv
    v_ref[...] = v
    for c in range(rows // chunk):
        wc_ref[c] = e_cum[(c + 1) * chunk - 1:(c + 1) * chunk, :]

    bg = p_ref[:, RW:RW + G]
    u = p_ref[:, RW + G:RW + 2 * G] * p_ref[:, RW + 2 * G:RW + 3 * G]
    u1 = pltpu.roll(u, 1, 0)
    u2 = pltpu.roll(u, 2, 0)
    rid = row_ids(G)
    if sample:
        um1 = jnp.where(rid == 0, up1_ref[...], u1)
        um2 = jnp.where(rid < 2, up2_ref[...], u2)
        u_ref[...] = u
    else:
        um1 = jnp.where(rid == 0, carry_u[1:2, :], u1)
        um2 = jnp.where(rid == 0, carry_u[0:1, :], jnp.where(rid == 1, carry_u[1:2, :], u2))
        u_ref[0] = u[rows - 8:, :]
    conv = cw_ref[0:1, :] * um2 + cw_ref[1:2, :] * um1 + cw_ref[2:3, :] * u
    conv_ref[...] = (bg * conv).astype(BF16)

    if not sample:
        carry_p[0:1, :] = p_ref[rows - 1:rows, 0:RW]
        carry_u[0:2, :] = u[rows - 2:, :]


def _prep(proj, weights, tri, *, n_seq, seq_len, rows, chunk, row_block_offset, sample, extra=()):
    t = n_seq * seq_len
    n_chunks = t // chunk
    cpt = rows // chunk
    if sample:
        grid = (t // rows,)
        rmap = lambda i: (i + row_block_offset, 0)
        omap = lambda i: (i, 0)
        omap3 = lambda i: (i, 0, 0)
        cmap = lambda i: (0, 0)
        sem = ("parallel",)
    else:
        tiles = seq_len // rows
        grid = (n_seq, tiles)
        rmap = lambda b, j: (b * tiles + j + row_block_offset, 0)
        omap = lambda b, j: (b * tiles + j, 0)
        omap3 = lambda b, j: (b * tiles + j, 0, 0)
        cmap = lambda b, j: (0, 0)
        sem = ("parallel", "arbitrary")

    const = lambda arr: pl.BlockSpec(arr.shape, cmap)
    in_specs = [pl.BlockSpec((rows, PW), rmap)]
    args = [proj]
    if sample:
        prev, up1, up2 = extra
        in_specs += [pl.BlockSpec((rows, RW), omap), pl.BlockSpec((rows, G), omap),
                     pl.BlockSpec((rows, G), omap)]
        args += [prev, up1, up2]
    in_specs += [const(w) for w in weights] + [const(tri)]
    args += list(weights) + [tri]

    row_out = jax.ShapeDtypeStruct((t, G), F32)
    row_spec = pl.BlockSpec((rows, G), omap)
    out_shape = [row_out] * 5 + [jax.ShapeDtypeStruct((n_chunks, 1, G), F32), row_out, row_out,
                                 jax.ShapeDtypeStruct((t, G), BF16)]
    out_specs = [row_spec] * 5 + [pl.BlockSpec((cpt, 1, G), omap3), row_spec, row_spec, row_spec]
    if sample:
        out_shape.append(row_out)
        out_specs.append(row_spec)
        scratch = []
    else:
        out_shape.append(jax.ShapeDtypeStruct((t // rows, 8, G), F32))
        out_specs.append(pl.BlockSpec((1, 8, G), omap3))
        scratch = [pltpu.VMEM((8, RW), F32), pltpu.VMEM((8, G), F32)]
    return pl.pallas_call(
        functools.partial(_prep_kernel, rows=rows, chunk=chunk, sample=sample),
        grid=grid, in_specs=in_specs, out_specs=out_specs, out_shape=out_shape,
        scratch_shapes=scratch,
        compiler_params=pltpu.CompilerParams(dimension_semantics=sem, vmem_limit_bytes=VMEM_LIMIT),
        name="prep_sample" if sample else "prep_prompt",
    )(*args)


def _scan_kernel(*refs, rows, chunk, carried, group):
    if carried:
        ah_ref, rh_ref, bh_ref, kh_ref, v_ref, wc_ref, s0_ref, y_ref, s_ref = refs
        s_in = s0_ref
    else:
        ah_ref, rh_ref, bh_ref, kh_ref, v_ref, wc_ref, y_ref, s_ref = refs
        s_in = s_ref

        @pl.when(pl.program_id(1) == 0)
        def _():
            s_ref[...] = jnp.zeros_like(s_ref)

    n_blk = rows // chunk
    lanes = [slice(h * HEAD_DIM, (h + 1) * HEAD_DIM) for h in range(N_HEADS)]
    bf = lambda x: x.astype(BF16)

    ri = lax.broadcasted_iota(jnp.int32, (rows, 2 * rows), 0)
    ci = lax.broadcasted_iota(jnp.int32, (rows, 2 * rows), 1)
    right = ci >= rows
    cj = jnp.where(right, ci - rows, ci)
    shift = chunk.bit_length() - 1
    same = (ri >> shift) == (cj >> shift)
    mask_ak = same & (ri > cj) & right
    mask_r = same & (ri >= cj)
    rs = lax.broadcasted_iota(jnp.int32, (rows, rows), 0)
    cs = lax.broadcasted_iota(jnp.int32, (rows, rows), 1)
    mask_ab = ((rs >> shift) == (cs >> shift)) & (rs > cs)
    eye = (rs == cs).astype(F32)

    def chunk_rows(x, c):
        return x[c * chunk:(c + 1) * chunk, :]

    def head_group(heads):
        per_head = lambda f: {h: f(h) for h in heads}
        a = per_head(lambda h: ah_ref[:, lanes[h]])
        r = per_head(lambda h: rh_ref[:, lanes[h]])
        b = per_head(lambda h: bh_ref[:, lanes[h]])
        k = per_head(lambda h: kh_ref[:, lanes[h]])
        v = per_head(lambda h: v_ref[:, lanes[h]])
        ar = per_head(lambda h: jnp.concatenate([a[h], r[h]], axis=0))
        bk = per_head(lambda h: jnp.concatenate([b[h], k[h]], axis=0))

        gram = per_head(lambda h: _dot_nt(bf(ar[h]), bf(bk[h])))
        l_ab = per_head(lambda h: jnp.where(mask_ab, gram[h][:rows, :rows], 0.0))
        l_ak = per_head(lambda h: bf(jnp.where(mask_ak, gram[h][:rows, :], 0.0)))
        l_r = per_head(lambda h: bf(jnp.where(mask_r, gram[h][rows:, :], 0.0)))

        xs, ys = {}, {}
        for h in heads:
            if n_blk == 1:
                st = _dot_nt(bf(ar[h]), bf(s_in[0, h]))
                xs[h], ys[h] = st[:rows], st[rows:]
            else:
                parts = [_dot_nt(jnp.concatenate([chunk_rows(a[h], c), chunk_rows(r[h], c)], axis=0),
                                 s_in[c, h]) for c in range(n_blk)]
                xs[h] = jnp.concatenate([p[:chunk] for p in parts], axis=0)
                ys[h] = jnp.concatenate([p[chunk:] for p in parts], axis=0)

        vv = per_head(lambda h: bf(jnp.concatenate([v[h], v[h]], axis=0)))
        x = per_head(lambda h: xs[h] + _dot(l_ak[h], vv[h]))

        t_inv = per_head(lambda h: eye + l_ab[h])
        pw = per_head(lambda h: bf(l_ab[h]))
        n = 1
        while 2 * n < chunk:
            pw = per_head(lambda h: bf(_dot(pw[h], pw[h])))
            t_inv = per_head(lambda h: t_inv[h] + _dot(bf(t_inv[h]), pw[h]))
            n *= 2

        u = per_head(lambda h: _dot(bf(t_inv[h]), bf(x[h])))
        uv = per_head(lambda h: jnp.concatenate([u[h], v[h]], axis=0))
        for h in heads:
            y_ref[:, lanes[h]] = ys[h] + _dot(l_r[h], bf(uv[h]))

        for h in heads:
            if n_blk == 1:
                upd = _dot_tn(bf(uv[h]), bf(bk[h]))
                s_ref[0, h] = (s_in[0, h] + upd) * wc_ref[0][:, lanes[h]]
            else:
                for c in range(n_blk):
                    uv_c = jnp.concatenate([chunk_rows(u[h], c), chunk_rows(v[h], c)], axis=0)
                    bk_c = jnp.concatenate([chunk_rows(b[h], c), chunk_rows(k[h], c)], axis=0)
                    s_ref[c, h] = (s_in[c, h] + _dot_tn(uv_c, bk_c)) * wc_ref[c][:, lanes[h]]

    for g0 in range(0, N_HEADS, group):
        head_group(range(g0, g0 + group))


def _scan(ah, rh, bh, kh, v, wc, s0, *, n_seq, seq_len, rows, chunk, group):
    carried = s0 is not None
    t = n_seq * seq_len
    n_blk = rows // chunk
    if carried:
        assert seq_len == chunk
        grid = (t // rows,)
        rmap = lambda i: (i, 0)
        cmap = lambda i: (i, 0, 0)
        smap = lambda i: (i, 0, 0, 0)
        sem = ("parallel",)
    else:
        assert rows == chunk
        n_chunks = seq_len // chunk
        grid = (n_seq, n_chunks)
        rmap = lambda s, c: (s * n_chunks + c, 0)
        cmap = lambda s, c: (s * n_chunks + c, 0, 0)
        smap = lambda s, c: (s, 0, 0, 0)
        sem = ("parallel", "arbitrary")
    row_spec = pl.BlockSpec((rows, G), rmap)
    state_spec = pl.BlockSpec((n_blk, N_HEADS, HEAD_DIM, HEAD_DIM), smap)
    in_specs = [row_spec] * 5 + [pl.BlockSpec((n_blk, 1, G), cmap)]
    args = [ah, rh, bh, kh, v, wc]
    if carried:
        in_specs.append(state_spec)
        args.append(s0)
    return pl.pallas_call(
        functools.partial(_scan_kernel, rows=rows, chunk=chunk, carried=carried, group=group),
        grid=grid, in_specs=in_specs,
        out_specs=[row_spec, state_spec],
        out_shape=[jax.ShapeDtypeStruct((t, G), F32),
                   jax.ShapeDtypeStruct((n_seq, N_HEADS, HEAD_DIM, HEAD_DIM), F32)],
        compiler_params=pltpu.CompilerParams(dimension_semantics=sem, vmem_limit_bytes=VMEM_LIMIT),
        name="scan_sample" if carried else "scan_prompt",
    )(*args)


def _mix_kernel(yp_ref, bp_ref, gp_ref, cp_ref, ys_ref, bs_ref, gs_ref, cs_ref,
                x_ref, lnw_ref, lnb_ref, bd_ref, wo_ref, o_ref, *, prompt_tiles):
    i = pl.program_id(0)

    def body(y_ref, bonus_ref, g_ref, conv_ref):
        bd = bd_ref[...]
        y = y_ref[...]
        mu = _dot_split_lhs(y, bd) * (1.0 / HEAD_DIM)
        d = y - mu
        var = _dot_split_lhs(d * d, bd) * (1.0 / HEAD_DIM)
        yn = d * lax.rsqrt(var + GN_EPS) * lnw_ref[...] + lnb_ref[...]
        rw = ((yn + bonus_ref[...]) * g_ref[...]).astype(BF16)
        mix = _dot(rw, wo_ref[0:G, :]) + _dot(conv_ref[...], wo_ref[G:2 * G, :])
        o_ref[...] = x_ref[...] + mix

    @pl.when(i < prompt_tiles)
    def _():
        body(yp_ref, bp_ref, gp_ref, cp_ref)

    @pl.when(i >= prompt_tiles)
    def _():
        body(ys_ref, bs_ref, gs_ref, cs_ref)


def _mix(prompt_rows, sample_rows, x1, lnw, lnb, bd, wo, *, tm):
    t = x1.shape[0]
    tp = prompt_rows[0].shape[0] // tm
    ts = sample_rows[0].shape[0] // tm
    pmap = lambda i: (jnp.minimum(i, tp - 1), 0)
    smap = lambda i: (jnp.maximum(i - tp, 0), 0)
    cmap = lambda i: (0, 0)
    in_specs = ([pl.BlockSpec((tm, G), pmap)] * 4 + [pl.BlockSpec((tm, G), smap)] * 4 + [
        pl.BlockSpec((tm, D_MODEL), lambda i: (i, 0)),
        pl.BlockSpec((1, G), cmap), pl.BlockSpec((1, G), cmap),
        pl.BlockSpec((G, G), cmap), pl.BlockSpec((D_MODEL, D_MODEL), cmap)])
    return pl.pallas_call(
        functools.partial(_mix_kernel, prompt_tiles=tp),
        grid=(tp + ts,), in_specs=in_specs,
        out_specs=pl.BlockSpec((tm, D_MODEL), lambda i: (i, 0)),
        out_shape=jax.ShapeDtypeStruct((t, D_MODEL), F32),
        compiler_params=pltpu.CompilerParams(
            dimension_semantics=("parallel",), vmem_limit_bytes=VMEM_LIMIT),
        name="mix",
    )(*prompt_rows, *sample_rows, x1, lnw, lnb, bd, wo)


def _pack_rwkv(a):
    r, wd, k, v, ad, gd = jnp.split(a, RWKV_SPLITS, axis=-1)
    zw = jnp.zeros(a.shape[:-1] + (LORA_PAD - W_LORA,), a.dtype)
    za = jnp.zeros(a.shape[:-1] + (LORA_PAD - A_LORA,), a.dtype)
    return jnp.concatenate([r, k, v, wd, zw, ad, za, gd], axis=-1)


def _unpack_rwkv(p):
    return jnp.concatenate([
        p[..., 0:G], p[..., OFF_WD:OFF_WD + W_LORA], p[..., G:2 * G], p[..., 2 * G:3 * G],
        p[..., OFF_AD:OFF_AD + A_LORA], p[..., OFF_GD:OFF_GD + G_LORA]], axis=-1)


def _block_tri(n, chunk):
    i = jnp.arange(n)
    return ((i[:, None] // chunk == i[None, :] // chunk) & (i[:, None] >= i[None, :])).astype(BF16)


def kernel(x_prompt, x_sample, state_wkv, state_shift, state_conv, meta_tokens, g_ffn1, ffn1_gate, ffn1_up, ffn1_down, g_mix, w_in, mu_shift, w0, w_lora_w, a0, w_lora_a, w_lora_g, k_k, k_a, r_k, ln_x_w, ln_x_b, conv_w, w_out, g_ffn2, ffn2_gate, ffn2_up, ffn2_down, g_final):
    assert g_ffn1.shape[0] == 1, "single layer"
    nb, seq, _ = x_prompt.shape
    db, dseq, _ = x_sample.shape
    assert dseq == C_SAMPLE
    lp = N_META + seq
    pad = (-lp) % C_PROMPT
    lpp = lp + pad
    tp, ts = nb * lpp, db * dseq
    t = tp + ts

    row = lambda a: a.reshape(1, -1).astype(F32)
    w_in_p = jnp.concatenate([_pack_rwkv(w_in[0][:, :RWKV_PROJ]), w_in[0][:, RWKV_PROJ:]], axis=-1).astype(BF16)
    mu_p = _pack_rwkv(mu_shift[0])[None]
    pad_rows = lambda w, n: jnp.concatenate([w, jnp.zeros((n - w.shape[0], w.shape[1]), w.dtype)], axis=0)
    prep_w = (mu_p, row(w0[0]), pad_rows(w_lora_w[0], LORA_PAD).astype(BF16), row(a0[0]),
              pad_rows(w_lora_a[0], LORA_PAD).astype(BF16), w_lora_g[0].astype(BF16),
              row(k_k[0]), row(k_a[0]), row(r_k[0]), conv_w[0].astype(F32))
    hid = jnp.arange(G) // HEAD_DIM
    bd = (hid[:, None] == hid[None, :]).astype(BF16)
    prep_w = prep_w + (bd,)

    meta = jnp.broadcast_to(meta_tokens.astype(F32)[None], (nb, N_META, D_MODEL))
    xp = jnp.concatenate([jnp.zeros((nb, pad, D_MODEL), F32), meta, x_prompt], axis=1)
    x_all = jnp.concatenate([xp.reshape(tp, D_MODEL), x_sample.reshape(ts, D_MODEL)], axis=0)

    tm = t // 16
    x1 = _ffn(x_all, row(g_ffn1[0]), ffn1_gate[0].astype(BF16), ffn1_up[0].astype(BF16),
              ffn1_down[0].astype(BF16), None, tm=tm, tf=512, name="ffn1")
    proj = _proj(x1, row(g_mix[0]), w_in_p, tm=tm, tn=512)

    rows_p = 3 * C_PROMPT
    (ah, rh, bh, kh, vv, wc, g_p, bonus_p, conv_p, utail_p) = _prep(
        proj, prep_w, _block_tri(rows_p, C_PROMPT), n_seq=nb, seq_len=lpp, rows=rows_p,
        chunk=C_PROMPT, row_block_offset=0, sample=False)
    y_p, wkv_p = _scan(ah, rh, bh, kh, vv, wc, None, n_seq=nb, seq_len=lpp, rows=C_PROMPT, chunk=C_PROMPT,
                       group=N_HEADS)

    rows_s = 256
    first = lambda a: jnp.zeros((db, dseq) + a.shape[1:], F32).at[:, 0].set(a)
    prev = first(_pack_rwkv(state_shift[0])).reshape(ts, RW)
    up1 = first(state_conv[0][:, 1]).reshape(ts, G)
    up2 = first(state_conv[0][:, 0]).at[:, 1].set(state_conv[0][:, 1]).reshape(ts, G)
    (ah, rh, bh, kh, vv, wc, g_s, bonus_s, conv_s, u_s) = _prep(
        proj, prep_w, _block_tri(rows_s, C_SAMPLE), n_seq=db, seq_len=dseq, rows=rows_s,
        chunk=C_SAMPLE, row_block_offset=tp // rows_s, sample=True, extra=(prev, up1, up2))
    y_s, wkv_s = _scan(ah, rh, bh, kh, vv, wc, state_wkv[0], n_seq=db, seq_len=dseq, rows=128,
                       chunk=C_SAMPLE, group=4)

    x2 = _mix((y_p, bonus_p, g_p, conv_p), (y_s, bonus_s, g_s, conv_s), x1,
              row(ln_x_w[0]), row(ln_x_b[0]), bd, w_out[0].astype(BF16), tm=256)
    y_all = _ffn(x2, row(g_ffn2[0]), ffn2_gate[0].astype(BF16), ffn2_up[0].astype(BF16),
                 ffn2_down[0].astype(BF16), row(g_final), tm=tm, tf=512, name="ffn2")

    y_prompt = y_all[:tp].reshape(nb, lpp, D_MODEL)[:, pad + N_META:]
    y_sample = y_all[tp:].reshape(db, dseq, D_MODEL)
    proj_p = proj[:tp].reshape(nb, lpp, PW)
    shift_p = _unpack_rwkv(proj_p[:, -1, :RW])
    conv_state_p = utail_p.reshape(nb, lpp // rows_p, 8, G)[:, -1, 6:, :]
    proj_s = proj[tp:].reshape(db, dseq, PW)
    shift_s = _unpack_rwkv(proj_s[:, -1, :RW])
    conv_state_s = u_s.reshape(db, dseq, G)[:, -2:, :]
    return (y_prompt, y_sample,
            wkv_p[None].astype(state_wkv.dtype), shift_p[None].astype(state_shift.dtype),
            conv_state_p[None].astype(state_conv.dtype),
            wkv_s[None].astype(state_wkv.dtype), shift_s[None].astype(state_shift.dtype),
            conv_state_s[None].astype(state_conv.dtype))
```

```python
import functools

import jax
import jax.numpy as jnp
from jax import lax
from jax.experimental import pallas as pl
from jax.experimental.pallas import tpu as pltpu

F32 = jnp.float32
BF16 = jnp.bfloat16

D_MODEL = 2048
D_FF = 5632
N_META = 16
G = 1024
HEAD_DIM = 64
N_HEADS = G // HEAD_DIM
W_LORA = 96
A_LORA = 96
G_LORA = 256
LORA_PAD = 128
RWKV_PROJ = 3 * G + W_LORA + A_LORA + G_LORA
RWKV_SPLITS = (G, G + W_LORA, 2 * G + W_LORA, 3 * G + W_LORA, 3 * G + W_LORA + A_LORA)
RW = 3 * G + 2 * LORA_PAD + G_LORA
PW = RW + 3 * G
OFF_WD, OFF_AD, OFF_GD = 3 * G, 3 * G + LORA_PAD, 3 * G + 2 * LORA_PAD
RMS_EPS = 1e-6
GN_EPS = 64e-5
EXP_M05 = 0.6065306597126334

C_PROMPT = 64
C_SAMPLE = 8
TM = 512
TF = 512
TN = 512
TM_MIX = 256
ROWS_PREP = 256
ROWS_SCAN_SAMPLE = 128
VMEM_LIMIT = 56 * 1024 * 1024


def _dot(a, b):
    return jnp.dot(a, b, preferred_element_type=F32)


def _dot_nt(a, b):
    return lax.dot_general(a, b, (((1,), (1,)), ((), ())), preferred_element_type=F32)


def _dot_tn(a, b):
    return lax.dot_general(a, b, (((0,), (0,)), ((), ())), preferred_element_type=F32)


def _split(a):
    hi = a.astype(BF16)
    lo = (a - hi.astype(F32)).astype(BF16)
    return hi, lo


def _dot_split_lhs(a, b_bf16):
    hi, lo = _split(a)
    return _dot(hi, b_bf16) + _dot(lo, b_bf16)


def _dot_split_rhs(a_bf16, b):
    hi, lo = _split(b)
    return _dot(a_bf16, hi) + _dot(a_bf16, lo)


def _rms(x, g):
    return x * lax.rsqrt(jnp.mean(x * x, axis=-1, keepdims=True) + RMS_EPS) * g


def _segments(arrays, tile):
    segs, off = [], 0
    for arr in arrays:
        n = arr.shape[0] // tile
        assert n * tile == arr.shape[0]
        segs.append((off, n))
        off += n
    return segs, off


def _seg_spec(tile, width, seg):
    off, n = seg
    return pl.BlockSpec((tile, width), lambda i, j: (jnp.clip(i - off, 0, n - 1), 0))


def _overlaps(in_segs, out_segs):
    for a, (ao, an) in enumerate(in_segs):
        for b, (bo, bn) in enumerate(out_segs):
            lo, hi = max(ao, bo), min(ao + an, bo + bn)
            if lo < hi:
                yield a, b, lo, hi


def _ffn_kernel(*refs, in_segs, out_segs, final_norm):
    n_in, n_out = len(in_segs), len(out_segs)
    x_refs = refs[:n_in]
    g_ref, wg_ref, wu_ref, wd_ref = refs[n_in:n_in + 4]
    rest = refs[n_in + 4:]
    if final_norm:
        gf_ref, rest = rest[0], rest[1:]
    o_refs = rest[:n_out]
    h_ref, acc_ref = rest[n_out:]
    i = pl.program_id(0)
    j = pl.program_id(1)

    for x_ref, (off, n) in zip(x_refs, in_segs):
        @pl.when((j == 0) & (i >= off) & (i < off + n))
        def _(x_ref=x_ref):
            h_ref[...] = _rms(x_ref[...], g_ref[...]).astype(BF16)
            acc_ref[...] = jnp.zeros_like(acc_ref)

    h = h_ref[...]
    gate = _dot(h, wg_ref[...])
    up = _dot(h, wu_ref[...])
    act = (gate * jax.nn.sigmoid(gate) * up).astype(BF16)
    acc_ref[...] += _dot(act, wd_ref[...])

    for a, b, lo, hi in _overlaps(in_segs, out_segs):
        @pl.when((j == pl.num_programs(1) - 1) & (i >= lo) & (i < hi))
        def _(x_ref=x_refs[a], o_ref=o_refs[b]):
            xo = x_ref[...] + 0.5 * acc_ref[...]
            if final_norm:
                xo = _rms(xo, gf_ref[...])
            o_ref[...] = xo


def _ffn(xs, g, wg, wu, wd, g_final, out_rows, *, name):
    final_norm = g_final is not None
    in_segs, n_tiles_in = _segments(xs, TM)
    out_shape = [jax.ShapeDtypeStruct((n, D_MODEL), F32) for n in out_rows]
    out_segs, n_tiles = _segments(out_shape, TM)
    assert n_tiles <= n_tiles_in
    cmap = lambda i, j: (0, 0)
    in_specs = [_seg_spec(TM, D_MODEL, s) for s in in_segs] + [
        pl.BlockSpec((1, D_MODEL), cmap),
        pl.BlockSpec((D_MODEL, TF), lambda i, j: (0, j)),
        pl.BlockSpec((D_MODEL, TF), lambda i, j: (0, j)),
        pl.BlockSpec((TF, D_MODEL), lambda i, j: (j, 0)),
    ]
    args = list(xs) + [g, wg, wu, wd]
    if final_norm:
        in_specs.append(pl.BlockSpec((1, D_MODEL), cmap))
        args.append(g_final)
    return pl.pallas_call(
        functools.partial(_ffn_kernel, in_segs=in_segs, out_segs=out_segs, final_norm=final_norm),
        grid=(n_tiles, D_FF // TF),
        in_specs=in_specs,
        out_specs=[_seg_spec(TM, D_MODEL, s) for s in out_segs],
        out_shape=out_shape,
        scratch_shapes=[pltpu.VMEM((TM, D_MODEL), BF16), pltpu.VMEM((TM, D_MODEL), F32)],
        compiler_params=pltpu.CompilerParams(
            dimension_semantics=("arbitrary", "arbitrary"), vmem_limit_bytes=VMEM_LIMIT),
        name=name,
    )(*args)


def _proj_kernel(x_ref, g_ref, w_ref, o_ref, h_ref):
    @pl.when(pl.program_id(1) == 0)
    def _():
        h_ref[...] = _rms(x_ref[...], g_ref[...]).astype(BF16)

    o_ref[...] = _dot(h_ref[...], w_ref[...])


def _proj(x, g, w):
    t = x.shape[0]
    return pl.pallas_call(
        _proj_kernel,
        grid=(t // TM, PW // TN),
        in_specs=[
            pl.BlockSpec((TM, D_MODEL), lambda i, j: (i, 0)),
            pl.BlockSpec((1, D_MODEL), lambda i, j: (0, 0)),
            pl.BlockSpec((D_MODEL, TN), lambda i, j: (0, j)),
        ],
        out_specs=pl.BlockSpec((TM, TN), lambda i, j: (i, j)),
        out_shape=jax.ShapeDtypeStruct((t, PW), F32),
        scratch_shapes=[pltpu.VMEM((TM, D_MODEL), BF16)],
        compiler_params=pltpu.CompilerParams(
            dimension_semantics=("parallel", "arbitrary"), vmem_limit_bytes=VMEM_LIMIT),
        name="proj",
    )(x, g, w)


def _prep_kernel(*refs, rows, chunk, sample):
    if sample:
        (p_ref, prev_ref, up1_ref, up2_ref, mu_ref, w0_ref, ww_ref, a0_ref, wa_ref, wgl_ref,
         kk_ref, ka_ref, rk_ref, cw_ref, bd_ref, tri_ref,
         ah_ref, rh_ref, bh_ref, kh_ref, v_ref, wc_ref, g_ref, bonus_ref, conv_ref, u_ref) = refs
    else:
        (p_ref, cp0_ref, cu0_ref, mu_ref, w0_ref, ww_ref, a0_ref, wa_ref, wgl_ref,
         kk_ref, ka_ref, rk_ref, cw_ref, bd_ref, tri_ref,
         ah_ref, rh_ref, bh_ref, kh_ref, v_ref, wc_ref, g_ref, bonus_ref, conv_ref, u_ref,
         carry_p, carry_u) = refs

        @pl.when(pl.program_id(1) == 0)
        def _():
            carry_p[...] = cp0_ref[...]
            carry_u[...] = cu0_ref[...]

    def row_ids(width):
        r = lax.broadcasted_iota(jnp.int32, (rows, width), 0)
        return (r & 7) if sample else r

    def shifted_mix(lo, hi):
        p = p_ref[:, lo:hi]
        rolled = pltpu.roll(p, 1, 0)
        if sample:
            sh = jnp.where(row_ids(hi - lo) == 0, prev_ref[:, lo:hi], rolled)
        else:
            sh = jnp.where(row_ids(hi - lo) == 0, carry_p[7:8, lo:hi], rolled)
        return p + (sh - p) * mu_ref[:, lo:hi]

    r = shifted_mix(0, G)
    k = shifted_mix(G, 2 * G)
    v = shifted_mix(2 * G, 3 * G)
    wd = shifted_mix(OFF_WD, OFF_WD + LORA_PAD)
    ad = shifted_mix(OFF_AD, OFF_AD + LORA_PAD)
    gd = shifted_mix(OFF_GD, OFF_GD + G_LORA)

    z = w0_ref[...] + _dot(jnp.tanh(wd).astype(BF16), ww_ref[...])
    lw = -EXP_M05 * jax.nn.sigmoid(z)
    a = jax.nn.sigmoid(a0_ref[...] + _dot(ad.astype(BF16), wa_ref[...]))
    g_ref[...] = _dot(jax.nn.sigmoid(gd).astype(BF16), wgl_ref[...])

    bd = bd_ref[...]
    kk = k * kk_ref[...]
    norm = jnp.sqrt(_dot_split_lhs(kk * kk, bd))
    kk = kk / jnp.maximum(norm, 1e-12)
    km = k * (1.0 + (a - 1.0) * ka_ref[...])
    bonus_ref[...] = _dot_split_lhs(r * km * rk_ref[...], bd) * v

    cum = _dot_split_rhs(tri_ref[...], lw)
    e_cum = jnp.exp(cum)
    e_inv = jnp.exp(-cum)
    ah_ref[...] = -kk * jnp.exp(cum - lw)
    rh_ref[...] = r * e_cum
    bh_ref[...] = kk * a * e_inv
    kh_ref[...] = km * e_inv
    v_ref[...] = v
    for c in range(rows // chunk):
        wc_ref[c] = e_cum[(c + 1) * chunk - 1:(c + 1) * chunk, :]

    bg = p_ref[:, RW:RW + G]
    u = p_ref[:, RW + G:RW + 2 * G] * p_ref[:, RW + 2 * G:RW + 3 * G]
    u1 = pltpu.roll(u, 1, 0)
    u2 = pltpu.roll(u, 2, 0)
    rid = row_ids(G)
    if sample:
        um1 = jnp.where(rid == 0, up1_ref[...], u1)
        um2 = jnp.where(rid < 2, up2_ref[...], u2)
        u_ref[...] = u
    else:
        um1 = jnp.where(rid == 0, carry_u[7:8, :], u1)
        um2 = jnp.where(rid == 0, carry_u[6:7, :], jnp.where(rid == 1, carry_u[7:8, :], u2))
        u_ref[0] = u[rows - 8:, :]
    conv = cw_ref[0:1, :] * um2 + cw_ref[1:2, :] * um1 + cw_ref[2:3, :] * u
    conv_ref[...] = (bg * conv).astype(BF16)

    if not sample:
        carry_p[...] = p_ref[rows - 8:rows, 0:RW]
        carry_u[...] = u[rows - 8:, :]


def _prep(proj, weights, tri, *, n_seq, seq_len, rows, chunk, row_block_offset, extra, sample, name):
    t = n_seq * seq_len
    n_chunks = t // chunk
    cpt = rows // chunk
    if sample:
        grid = (t // rows,)
        rmap = lambda i: (i + row_block_offset, 0)
        omap = lambda i: (i, 0)
        omap3 = lambda i: (i, 0, 0)
        cmap = lambda i: (0, 0)
        sem = ("parallel",)
    else:
        tiles = seq_len // rows
        grid = (n_seq, tiles)
        rmap = lambda b, j: (b * tiles + j + row_block_offset, 0)
        omap = lambda b, j: (b * tiles + j, 0)
        omap3 = lambda b, j: (b * tiles + j, 0, 0)
        cmap = lambda b, j: (0, 0)
        sem = ("parallel", "arbitrary")

    const = lambda arr: pl.BlockSpec(arr.shape, cmap)
    in_specs = [pl.BlockSpec((rows, PW), rmap)]
    if sample:
        in_specs += [pl.BlockSpec((rows, RW), omap), pl.BlockSpec((rows, G), omap),
                     pl.BlockSpec((rows, G), omap)]
    else:
        in_specs += [const(e) for e in extra]
    in_specs += [const(w) for w in weights] + [const(tri)]
    args = [proj] + list(extra) + list(weights) + [tri]

    row_out = jax.ShapeDtypeStruct((t, G), F32)
    row_spec = pl.BlockSpec((rows, G), omap)
    out_shape = [row_out] * 5 + [jax.ShapeDtypeStruct((n_chunks, 1, G), F32), row_out, row_out,
                                 jax.ShapeDtypeStruct((t, G), BF16)]
    out_specs = [row_spec] * 5 + [pl.BlockSpec((cpt, 1, G), omap3), row_spec, row_spec, row_spec]
    if sample:
        out_shape.append(row_out)
        out_specs.append(row_spec)
        scratch = []
    else:
        out_shape.append(jax.ShapeDtypeStruct((t // rows, 8, G), F32))
        out_specs.append(pl.BlockSpec((1, 8, G), omap3))
        scratch = [pltpu.VMEM((8, RW), F32), pltpu.VMEM((8, G), F32)]
    return pl.pallas_call(
        functools.partial(_prep_kernel, rows=rows, chunk=chunk, sample=sample),
        grid=grid, in_specs=in_specs, out_specs=out_specs, out_shape=out_shape,
        scratch_shapes=scratch,
        compiler_params=pltpu.CompilerParams(dimension_semantics=sem, vmem_limit_bytes=VMEM_LIMIT),
        name=name,
    )(*args)


def _scan_kernel(ah_ref, rh_ref, bh_ref, kh_ref, v_ref, wc_ref, s0_ref, y_ref, s_ref,
                 *, rows, chunk, per_chunk_state, group):
    if per_chunk_state:
        s_in = s0_ref
    else:
        s_in = s_ref

        @pl.when(pl.program_id(1) == 0)
        def _():
            s_ref[...] = s0_ref[...]

    n_blk = rows // chunk
    lanes = [slice(h * HEAD_DIM, (h + 1) * HEAD_DIM) for h in range(N_HEADS)]
    bf = lambda x: x.astype(BF16)

    ri = lax.broadcasted_iota(jnp.int32, (rows, 2 * rows), 0)
    ci = lax.broadcasted_iota(jnp.int32, (rows, 2 * rows), 1)
    right = ci >= rows
    cj = jnp.where(right, ci - rows, ci)
    shift = chunk.bit_length() - 1
    same = (ri >> shift) == (cj >> shift)
    mask_ak = same & (ri > cj) & right
    mask_r = same & (ri >= cj)
    rs = lax.broadcasted_iota(jnp.int32, (rows, rows), 0)
    cs = lax.broadcasted_iota(jnp.int32, (rows, rows), 1)
    mask_ab = ((rs >> shift) == (cs >> shift)) & (rs > cs)
    eye = (rs == cs).astype(F32)

    def chunk_rows(x, c):
        return x[c * chunk:(c + 1) * chunk, :]

    def head_group(heads):
        per_head = lambda f: {h: f(h) for h in heads}
        a = per_head(lambda h: ah_ref[:, lanes[h]])
        r = per_head(lambda h: rh_ref[:, lanes[h]])
        b = per_head(lambda h: bh_ref[:, lanes[h]])
        k = per_head(lambda h: kh_ref[:, lanes[h]])
        v = per_head(lambda h: v_ref[:, lanes[h]])
        ar = per_head(lambda h: jnp.concatenate([a[h], r[h]], axis=0))
        bk = per_head(lambda h: jnp.concatenate([b[h], k[h]], axis=0))

        gram = per_head(lambda h: _dot_nt(bf(ar[h]), bf(bk[h])))
        l_ab = per_head(lambda h: jnp.where(mask_ab, gram[h][:rows, :rows], 0.0))
        l_ak = per_head(lambda h: bf(jnp.where(mask_ak, gram[h][:rows, :], 0.0)))
        l_r = per_head(lambda h: bf(jnp.where(mask_r, gram[h][rows:, :], 0.0)))

        xs, ys = {}, {}
        for h in heads:
            if n_blk == 1:
                st = _dot_nt(bf(ar[h]), bf(s_in[0, h]))
                xs[h], ys[h] = st[:rows], st[rows:]
            else:
                parts = [_dot_nt(jnp.concatenate([chunk_rows(a[h], c), chunk_rows(r[h], c)], axis=0),
                                 s_in[c, h]) for c in range(n_blk)]
                xs[h] = jnp.concatenate([p[:chunk] for p in parts], axis=0)
                ys[h] = jnp.concatenate([p[chunk:] for p in parts], axis=0)

        vv = per_head(lambda h: bf(jnp.concatenate([v[h], v[h]], axis=0)))
        x = per_head(lambda h: xs[h] + _dot(l_ak[h], vv[h]))

        t_inv = per_head(lambda h: eye + l_ab[h])
        pw = per_head(lambda h: bf(l_ab[h]))
        n = 1
        while 2 * n < chunk:
            pw = per_head(lambda h: bf(_dot(pw[h], pw[h])))
            t_inv = per_head(lambda h: t_inv[h] + _dot(bf(t_inv[h]), pw[h]))
            n *= 2

        u = per_head(lambda h: _dot(bf(t_inv[h]), bf(x[h])))
        uv = per_head(lambda h: jnp.concatenate([u[h], v[h]], axis=0))
        for h in heads:
            y_ref[:, lanes[h]] = ys[h] + _dot(l_r[h], bf(uv[h]))

        for h in heads:
            if n_blk == 1:
                upd = _dot_tn(bf(uv[h]), bf(bk[h]))
                s_ref[0, h] = (s_in[0, h] + upd) * wc_ref[0][:, lanes[h]]
            else:
                for c in range(n_blk):
                    uv_c = jnp.concatenate([chunk_rows(u[h], c), chunk_rows(v[h], c)], axis=0)
                    bk_c = jnp.concatenate([chunk_rows(b[h], c), chunk_rows(k[h], c)], axis=0)
                    s_ref[c, h] = (s_in[c, h] + _dot_tn(uv_c, bk_c)) * wc_ref[c][:, lanes[h]]

    for g0 in range(0, N_HEADS, group):
        head_group(range(g0, g0 + group))


def _scan(ah, rh, bh, kh, v, wc, s0, *, n_seq, seq_len, rows, chunk, per_chunk_state, group, name):
    t = n_seq * seq_len
    n_blk = rows // chunk
    if per_chunk_state:
        assert seq_len == chunk
        grid = (t // rows,)
        rmap = lambda i: (i, 0)
        cmap = lambda i: (i, 0, 0)
        smap = lambda i: (i, 0, 0, 0)
        s0map = smap
        sem = ("parallel",)
    else:
        assert rows == chunk and s0.shape[0] == 1
        n_chunks = seq_len // chunk
        grid = (n_seq, n_chunks)
        rmap = lambda s, c: (s * n_chunks + c, 0)
        cmap = lambda s, c: (s * n_chunks + c, 0, 0)
        smap = lambda s, c: (s, 0, 0, 0)
        s0map = lambda s, c: (0, 0, 0, 0)
        sem = ("parallel", "arbitrary")
    row_spec = pl.BlockSpec((rows, G), rmap)
    state_block = (n_blk, N_HEADS, HEAD_DIM, HEAD_DIM)
    return pl.pallas_call(
        functools.partial(_scan_kernel, rows=rows, chunk=chunk, per_chunk_state=per_chunk_state,
                          group=group),
        grid=grid,
        in_specs=[row_spec] * 5 + [pl.BlockSpec((n_blk, 1, G), cmap), pl.BlockSpec(state_block, s0map)],
        out_specs=[row_spec, pl.BlockSpec(state_block, smap)],
        out_shape=[jax.ShapeDtypeStruct((t, G), F32),
                   jax.ShapeDtypeStruct((n_seq, N_HEADS, HEAD_DIM, HEAD_DIM), F32)],
        compiler_params=pltpu.CompilerParams(dimension_semantics=sem, vmem_limit_bytes=VMEM_LIMIT),
        name=name,
    )(ah, rh, bh, kh, v, wc, s0)


def _mix_kernel(*refs, segs):
    n = len(segs)
    row_refs = [refs[4 * s:4 * s + 4] for s in range(n)]
    x_ref, lnw_ref, lnb_ref, bd_ref, wo_ref, o_ref = refs[4 * n:]
    i = pl.program_id(0)

    def body(y_ref, bonus_ref, g_ref, conv_ref):
        bd = bd_ref[...]
        y = y_ref[...]
        mu = _dot_split_lhs(y, bd) * (1.0 / HEAD_DIM)
        d = y - mu
        var = _dot_split_lhs(d * d, bd) * (1.0 / HEAD_DIM)
        yn = d * lax.rsqrt(var + GN_EPS) * lnw_ref[...] + lnb_ref[...]
        rw = ((yn + bonus_ref[...]) * g_ref[...]).astype(BF16)
        mix = _dot(rw, wo_ref[0:G, :]) + _dot(conv_ref[...], wo_ref[G:2 * G, :])
        o_ref[...] = x_ref[...] + mix

    for rr, (off, cnt) in zip(row_refs, segs):
        @pl.when((i >= off) & (i < off + cnt))
        def _(rr=rr):
            body(*rr)


def _mix(row_groups, x1, lnw, lnb, bd, wo):
    segs, n_tiles = _segments([grp[0] for grp in row_groups], TM_MIX)
    cmap = lambda i: (0, 0)
    in_specs, args = [], []
    for grp, (off, cnt) in zip(row_groups, segs):
        smap = lambda i, off=off, cnt=cnt: (jnp.clip(i - off, 0, cnt - 1), 0)
        in_specs += [pl.BlockSpec((TM_MIX, G), smap)] * 4
        args += list(grp)
    in_specs += [pl.BlockSpec((TM_MIX, D_MODEL), lambda i: (i, 0)),
                 pl.BlockSpec((1, G), cmap), pl.BlockSpec((1, G), cmap),
                 pl.BlockSpec((G, G), cmap), pl.BlockSpec((D_MODEL, D_MODEL), cmap)]
    return pl.pallas_call(
        functools.partial(_mix_kernel, segs=segs),
        grid=(n_tiles,), in_specs=in_specs,
        out_specs=pl.BlockSpec((TM_MIX, D_MODEL), lambda i: (i, 0)),
        out_shape=jax.ShapeDtypeStruct((n_tiles * TM_MIX, D_MODEL), F32),
        compiler_params=pltpu.CompilerParams(
            dimension_semantics=("parallel",), vmem_limit_bytes=VMEM_LIMIT),
        name="mix",
    )(*args, x1, lnw, lnb, bd, wo)


def _pack_rwkv(a):
    r, wd, k, v, ad, gd = jnp.split(a, RWKV_SPLITS, axis=-1)
    zw = jnp.zeros(a.shape[:-1] + (LORA_PAD - W_LORA,), a.dtype)
    za = jnp.zeros(a.shape[:-1] + (LORA_PAD - A_LORA,), a.dtype)
    return jnp.concatenate([r, k, v, wd, zw, ad, za, gd], axis=-1)


def _unpack_rwkv(p):
    return jnp.concatenate([
        p[..., 0:G], p[..., OFF_WD:OFF_WD + W_LORA], p[..., G:2 * G], p[..., 2 * G:3 * G],
        p[..., OFF_AD:OFF_AD + A_LORA], p[..., OFF_GD:OFF_GD + G_LORA]], axis=-1)


def _block_tri(n, chunk):
    i = jnp.arange(n)
    return ((i[:, None] // chunk == i[None, :] // chunk) & (i[:, None] >= i[None, :])).astype(BF16)


def kernel(x_prompt, x_sample, state_wkv, state_shift, state_conv, meta_tokens, g_ffn1, ffn1_gate, ffn1_up, ffn1_down, g_mix, w_in, mu_shift, w0, w_lora_w, a0, w_lora_a, w_lora_g, k_k, k_a, r_k, ln_x_w, ln_x_b, conv_w, w_out, g_ffn2, ffn2_gate, ffn2_up, ffn2_down, g_final):
    assert g_ffn1.shape[0] == 1, "single layer"
    nb, seq, _ = x_prompt.shape
    db, dseq, _ = x_sample.shape
    assert dseq == C_SAMPLE and N_META <= C_PROMPT and seq % ROWS_PREP == 0
    tp, ts = nb * seq, db * dseq
    assert tp % TM == 0 and ts % TM == 0

    row = lambda a: a.reshape(1, -1).astype(F32)
    w_in_p = jnp.concatenate([_pack_rwkv(w_in[0][:, :RWKV_PROJ]), w_in[0][:, RWKV_PROJ:]], axis=-1).astype(BF16)
    pad_rows = lambda w, n: jnp.concatenate([w, jnp.zeros((n - w.shape[0], w.shape[1]), w.dtype)], axis=0)
    hid = jnp.arange(G) // HEAD_DIM
    bd = (hid[:, None] == hid[None, :]).astype(BF16)
    prep_w = (_pack_rwkv(mu_shift[0])[None], row(w0[0]), pad_rows(w_lora_w[0], LORA_PAD).astype(BF16),
              row(a0[0]), pad_rows(w_lora_a[0], LORA_PAD).astype(BF16), w_lora_g[0].astype(BF16),
              row(k_k[0]), row(k_a[0]), row(r_k[0]), conv_w[0].astype(F32), bd)

    tail_rows = ts + C_PROMPT
    tail_rows += (-tail_rows) % TM
    x_tail = jnp.concatenate([
        x_sample.reshape(ts, D_MODEL), jnp.zeros((C_PROMPT - N_META, D_MODEL), F32),
        meta_tokens.astype(F32), jnp.zeros((tail_rows - ts - C_PROMPT, D_MODEL), F32)], axis=0)
    meta_row0 = tp + ts

    x1 = _ffn([x_prompt.reshape(tp, D_MODEL), x_tail], row(g_ffn1[0]), ffn1_gate[0].astype(BF16),
              ffn1_up[0].astype(BF16), ffn1_down[0].astype(BF16), None, [tp + tail_rows], name="ffn1")[0]
    proj = _proj(x1, row(g_mix[0]), w_in_p)

    tri_p = _block_tri(ROWS_PREP, C_PROMPT)
    zeros_state = jnp.zeros((1, N_HEADS, HEAD_DIM, HEAD_DIM), F32)
    (ah, rh, bh, kh, vv, wc, _, _, _, utail_m) = _prep(
        proj, prep_w, tri_p[:C_PROMPT, :C_PROMPT], n_seq=1, seq_len=C_PROMPT, rows=C_PROMPT, chunk=C_PROMPT,
        row_block_offset=meta_row0 // C_PROMPT, extra=(jnp.zeros((8, RW), F32), jnp.zeros((8, G), F32)),
        sample=False, name="prep_meta")
    _, wkv_m = _scan(ah, rh, bh, kh, vv, wc, zeros_state, n_seq=1, seq_len=C_PROMPT, rows=C_PROMPT,
                     chunk=C_PROMPT, per_chunk_state=False, group=N_HEADS, name="scan_meta")
    ptail_m = proj[meta_row0 + C_PROMPT - 8:meta_row0 + C_PROMPT, :RW]

    (ah, rh, bh, kh, vv, wc, g_p, bonus_p, conv_p, utail_p) = _prep(
        proj, prep_w, tri_p, n_seq=nb, seq_len=seq, rows=ROWS_PREP, chunk=C_PROMPT,
        row_block_offset=0, extra=(ptail_m, utail_m[0]), sample=False, name="prep_prompt")
    y_p, wkv_p = _scan(ah, rh, bh, kh, vv, wc, wkv_m, n_seq=nb, seq_len=seq, rows=C_PROMPT,
                       chunk=C_PROMPT, per_chunk_state=False, group=N_HEADS, name="scan_prompt")

    first = lambda a: jnp.zeros((db, dseq) + a.shape[1:], F32).at[:, 0].set(a)
    prev = first(_pack_rwkv(state_shift[0])).reshape(ts, RW)
    up1 = first(state_conv[0][:, 1]).reshape(ts, G)
    up2 = first(state_conv[0][:, 0]).at[:, 1].set(state_conv[0][:, 1]).reshape(ts, G)
    (ah, rh, bh, kh, vv, wc, g_s, bonus_s, conv_s, u_s) = _prep(
        proj, prep_w, _block_tri(ROWS_PREP, C_SAMPLE), n_seq=db, seq_len=dseq, rows=ROWS_PREP,
        chunk=C_SAMPLE, row_block_offset=tp // ROWS_PREP, extra=(prev, up1, up2), sample=True,
        name="prep_sample")
    y_s, wkv_s = _scan(ah, rh, bh, kh, vv, wc, state_wkv[0], n_seq=db, seq_len=dseq,
                       rows=ROWS_SCAN_SAMPLE, chunk=C_SAMPLE, per_chunk_state=True, group=4,
                       name="scan_sample")

    x2 = _mix([(y_p, bonus_p, g_p, conv_p), (y_s, bonus_s, g_s, conv_s)], x1,
              row(ln_x_w[0]), row(ln_x_b[0]), bd, w_out[0].astype(BF16))
    y_prompt, y_sample = _ffn([x2], row(g_ffn2[0]), ffn2_gate[0].astype(BF16), ffn2_up[0].astype(BF16),
                              ffn2_down[0].astype(BF16), row(g_final), [tp, ts], name="ffn2")

    shift_p = _unpack_rwkv(proj[seq - 1:tp:seq, :RW])
    conv_state_p = utail_p.reshape(nb, seq // ROWS_PREP, 8, G)[:, -1, 6:, :]
    shift_s = _unpack_rwkv(proj[tp + dseq - 1:tp + ts:dseq, :RW])
    conv_state_s = u_s.reshape(db, dseq, G)[:, -2:, :]
    return (y_prompt.reshape(nb, seq, D_MODEL), y_sample.reshape(db, dseq, D_MODEL),
            wkv_p[None].astype(state_wkv.dtype), shift_p[None].astype(state_shift.dtype),
            conv_state_p[None].astype(state_conv.dtype),
            wkv_s[None].astype(state_wkv.dtype), shift_s[None].astype(state_shift.dtype),
            conv_state_s[None].astype(state_conv.dtype))
```

```python
import functools

import jax
import jax.numpy as jnp
from jax import lax
from jax.experimental import pallas as pl
from jax.experimental.pallas import tpu as pltpu

F32 = jnp.float32
BF16 = jnp.bfloat16

D_MODEL = 2048
D_FF = 5632
N_META = 16
G = 1024
HEAD_DIM = 64
N_HEADS = G // HEAD_DIM
W_LORA = 96
A_LORA = 96
G_LORA = 256
LORA_PAD = 128
RWKV_PROJ = 3 * G + W_LORA + A_LORA + G_LORA
RWKV_SPLITS = (G, G + W_LORA, 2 * G + W_LORA, 3 * G + W_LORA, 3 * G + W_LORA + A_LORA)
RW = 3 * G + 2 * LORA_PAD + G_LORA
PW = RW + 3 * G
OFF_WD, OFF_AD, OFF_GD = 3 * G, 3 * G + LORA_PAD, 3 * G + 2 * LORA_PAD
RMS_EPS = 1e-6
GN_EPS = 64e-5
EXP_M05 = 0.6065306597126334

C_PROMPT = 64
C_SAMPLE = 8
TM = 512
TF = 512
TN = 1664
TM_MIX = 256
ROWS_PREP = 256
ROWS_SCAN_SAMPLE = 128
SCAN_PAR = 2
VMEM_LIMIT = 56 * 1024 * 1024


def _dot(a, b):
    return jnp.dot(a, b, preferred_element_type=F32)


def _dot_nt(a, b):
    return lax.dot_general(a, b, (((1,), (1,)), ((), ())), preferred_element_type=F32)


def _dot_tn(a, b):
    return lax.dot_general(a, b, (((0,), (0,)), ((), ())), preferred_element_type=F32)


def _split(a):
    hi = a.astype(BF16)
    lo = (a - hi.astype(F32)).astype(BF16)
    return hi, lo


def _head_sum(x, bd):
    return _dot(x.astype(BF16), bd)


def _dot_split_rhs(a_bf16, b):
    hi, lo = _split(b)
    return _dot(a_bf16, hi) + _dot(a_bf16, lo)


def _rms(x, g):
    return x * lax.rsqrt(jnp.mean(x * x, axis=-1, keepdims=True) + RMS_EPS) * g


def _segments(arrays, tile):
    segs, off = [], 0
    for arr in arrays:
        n = arr.shape[0] // tile
        assert n * tile == arr.shape[0]
        segs.append((off, n))
        off += n
    return segs, off


def _seg_spec(tile, width, seg):
    off, n = seg
    return pl.BlockSpec((tile, width), lambda i, j: (jnp.clip(i - off, 0, n - 1), 0))


def _overlaps(in_segs, out_segs):
    for a, (ao, an) in enumerate(in_segs):
        for b, (bo, bn) in enumerate(out_segs):
            lo, hi = max(ao, bo), min(ao + an, bo + bn)
            if lo < hi:
                yield a, b, lo, hi


def _ffn_kernel(*refs, in_segs, out_segs, final_norm):
    n_in, n_out = len(in_segs), len(out_segs)
    x_refs = refs[:n_in]
    g_ref, wg_ref, wu_ref, wd_ref = refs[n_in:n_in + 4]
    rest = refs[n_in + 4:]
    if final_norm:
        gf_ref, rest = rest[0], rest[1:]
    o_refs = rest[:n_out]
    h_ref, acc_ref = rest[n_out:]
    i = pl.program_id(0)
    j = pl.program_id(1)

    for x_ref, (off, n) in zip(x_refs, in_segs):
        @pl.when((j == 0) & (i >= off) & (i < off + n))
        def _(x_ref=x_ref):
            h_ref[...] = _rms(x_ref[...], g_ref[...]).astype(BF16)
            acc_ref[...] = jnp.zeros_like(acc_ref)

    h = h_ref[...]
    gate = _dot(h, wg_ref[...])
    up = _dot(h, wu_ref[...])
    act = (gate * jax.nn.sigmoid(gate) * up).astype(BF16)
    acc_ref[...] += _dot(act, wd_ref[...])

    for a, b, lo, hi in _overlaps(in_segs, out_segs):
        @pl.when((j == pl.num_programs(1) - 1) & (i >= lo) & (i < hi))
        def _(x_ref=x_refs[a], o_ref=o_refs[b]):
            xo = x_ref[...] + 0.5 * acc_ref[...]
            if final_norm:
                xo = _rms(xo, gf_ref[...])
            o_ref[...] = xo


def _ffn(xs, g, wg, wu, wd, g_final, out_rows, *, name):
    final_norm = g_final is not None
    in_segs, n_tiles_in = _segments(xs, TM)
    out_shape = [jax.ShapeDtypeStruct((n, D_MODEL), F32) for n in out_rows]
    out_segs, n_tiles = _segments(out_shape, TM)
    assert n_tiles <= n_tiles_in
    cmap = lambda i, j: (0, 0)
    in_specs = [_seg_spec(TM, D_MODEL, s) for s in in_segs] + [
        pl.BlockSpec((1, D_MODEL), cmap),
        pl.BlockSpec((D_MODEL, TF), lambda i, j: (0, j)),
        pl.BlockSpec((D_MODEL, TF), lambda i, j: (0, j)),
        pl.BlockSpec((TF, D_MODEL), lambda i, j: (j, 0)),
    ]
    args = list(xs) + [g, wg, wu, wd]
    if final_norm:
        in_specs.append(pl.BlockSpec((1, D_MODEL), cmap))
        args.append(g_final)
    return pl.pallas_call(
        functools.partial(_ffn_kernel, in_segs=in_segs, out_segs=out_segs, final_norm=final_norm),
        grid=(n_tiles, D_FF // TF),
        in_specs=in_specs,
        out_specs=[_seg_spec(TM, D_MODEL, s) for s in out_segs],
        out_shape=out_shape,
        scratch_shapes=[pltpu.VMEM((TM, D_MODEL), BF16), pltpu.VMEM((TM, D_MODEL), F32)],
        compiler_params=pltpu.CompilerParams(
            dimension_semantics=("arbitrary", "arbitrary"), vmem_limit_bytes=VMEM_LIMIT),
        name=name,
    )(*args)


_PACK_MOVES = (
    (0, 0, G),
    (G, RWKV_SPLITS[1], G),
    (2 * G, RWKV_SPLITS[2], G),
    (OFF_WD, RWKV_SPLITS[0], W_LORA),
    (OFF_AD, RWKV_SPLITS[3], A_LORA),
    (OFF_GD, RWKV_SPLITS[4], G_LORA),
    (RW, RWKV_PROJ, 3 * G),
)


def _pack_w_in_kernel(w_ref, o_ref):
    o_ref[...] = jnp.zeros_like(o_ref)
    for dst, src, n in _PACK_MOVES:
        o_ref[:, dst:dst + n] = w_ref[:, src:src + n].astype(BF16)


def _pack_w_in(w):
    rows = 256
    return pl.pallas_call(
        _pack_w_in_kernel,
        grid=(D_MODEL // rows,),
        in_specs=[pl.BlockSpec((rows, w.shape[1]), lambda i: (i, 0))],
        out_specs=pl.BlockSpec((rows, PW), lambda i: (i, 0)),
        out_shape=jax.ShapeDtypeStruct((D_MODEL, PW), BF16),
        compiler_params=pltpu.CompilerParams(
            dimension_semantics=("parallel",), vmem_limit_bytes=VMEM_LIMIT),
        name="pack_w_in",
    )(w)


def _proj_kernel(x_ref, g_ref, w_ref, o_ref, h_ref):
    @pl.when(pl.program_id(1) == 0)
    def _():
        h_ref[...] = _rms(x_ref[...], g_ref[...]).astype(BF16)

    o_ref[...] = _dot(h_ref[...], w_ref[...])


def _proj(x, g, w):
    t = x.shape[0]
    return pl.pallas_call(
        _proj_kernel,
        grid=(t // TM, PW // TN),
        in_specs=[
            pl.BlockSpec((TM, D_MODEL), lambda i, j: (i, 0)),
            pl.BlockSpec((1, D_MODEL), lambda i, j: (0, 0)),
            pl.BlockSpec((D_MODEL, TN), lambda i, j: (0, j)),
        ],
        out_specs=pl.BlockSpec((TM, TN), lambda i, j: (i, j)),
        out_shape=jax.ShapeDtypeStruct((t, PW), F32),
        scratch_shapes=[pltpu.VMEM((TM, D_MODEL), BF16)],
        compiler_params=pltpu.CompilerParams(
            dimension_semantics=("parallel", "arbitrary"), vmem_limit_bytes=VMEM_LIMIT),
        name="proj",
    )(x, g, w)


def _prep_kernel(*refs, rows, chunk, sample):
    if sample:
        (p_ref, prev_ref, up1_ref, up2_ref, mu_ref, w0_ref, ww_ref, a0_ref, wa_ref, wgl_ref,
         kk_ref, ka_ref, rk_ref, cw_ref, bd_ref, tri_ref,
         ah_ref, rh_ref, bh_ref, kh_ref, v_ref, wc_ref, g_ref, bonus_ref, conv_ref, u_ref) = refs
    else:
        (p_ref, cp0_ref, cu0_ref, mu_ref, w0_ref, ww_ref, a0_ref, wa_ref, wgl_ref,
         kk_ref, ka_ref, rk_ref, cw_ref, bd_ref, tri_ref,
         ah_ref, rh_ref, bh_ref, kh_ref, v_ref, wc_ref, g_ref, bonus_ref, conv_ref, u_ref,
         carry_p, carry_u) = refs

        @pl.when(pl.program_id(1) == 0)
        def _():
            carry_p[...] = cp0_ref[...]
            carry_u[...] = cu0_ref[...]

    def row_ids(width):
        r = lax.broadcasted_iota(jnp.int32, (rows, width), 0)
        return (r & 7) if sample else r

    def shifted_mix(lo, hi):
        p = p_ref[:, lo:hi]
        rolled = pltpu.roll(p, 1, 0)
        if sample:
            sh = jnp.where(row_ids(hi - lo) == 0, prev_ref[:, lo:hi], rolled)
        else:
            sh = jnp.where(row_ids(hi - lo) == 0, carry_p[7:8, lo:hi], rolled)
        return p + (sh - p) * mu_ref[:, lo:hi]

    r = shifted_mix(0, G)
    k = shifted_mix(G, 2 * G)
    v = shifted_mix(2 * G, 3 * G)
    wd = shifted_mix(OFF_WD, OFF_WD + LORA_PAD)
    ad = shifted_mix(OFF_AD, OFF_AD + LORA_PAD)
    gd = shifted_mix(OFF_GD, OFF_GD + G_LORA)

    z = w0_ref[...] + _dot(jnp.tanh(wd).astype(BF16), ww_ref[...])
    lw = -EXP_M05 * jax.nn.sigmoid(z)
    a = jax.nn.sigmoid(a0_ref[...] + _dot(ad.astype(BF16), wa_ref[...]))
    g_ref[...] = _dot(jax.nn.sigmoid(gd).astype(BF16), wgl_ref[...])

    bd = bd_ref[...]
    kk = k * kk_ref[...]
    norm = jnp.sqrt(_head_sum(kk * kk, bd))
    kk = kk / jnp.maximum(norm, 1e-12)
    km = k * (1.0 + (a - 1.0) * ka_ref[...])
    bonus_ref[...] = _head_sum(r * km * rk_ref[...], bd) * v

    cum = _dot_split_rhs(tri_ref[...], lw)
    e_cum = jnp.exp(cum)
    e_inv = jnp.exp(-cum)
    ah_ref[...] = -kk * jnp.exp(cum - lw)
    rh_ref[...] = r * e_cum
    bh_ref[...] = kk * a * e_inv
    kh_ref[...] = km * e_inv
    v_ref[...] = v
    for c in range(rows // chunk):
        wc_ref[c] = e_cum[(c + 1) * chunk - 1:(c + 1) * chunk, :]

    bg = p_ref[:, RW:RW + G]
    u = p_ref[:, RW + G:RW + 2 * G] * p_ref[:, RW + 2 * G:RW + 3 * G]
    u1 = pltpu.roll(u, 1, 0)
    u2 = pltpu.roll(u, 2, 0)
    rid = row_ids(G)
    if sample:
        um1 = jnp.where(rid == 0, up1_ref[...], u1)
        um2 = jnp.where(rid < 2, up2_ref[...], u2)
        u_ref[...] = u
    else:
        um1 = jnp.where(rid == 0, carry_u[7:8, :], u1)
        um2 = jnp.where(rid == 0, carry_u[6:7, :], jnp.where(rid == 1, carry_u[7:8, :], u2))
        u_ref[0] = u[rows - 8:, :]
    conv = cw_ref[0:1, :] * um2 + cw_ref[1:2, :] * um1 + cw_ref[2:3, :] * u
    conv_ref[...] = (bg * conv).astype(BF16)

    if not sample:
        carry_p[...] = p_ref[rows - 8:rows, 0:RW]
        carry_u[...] = u[rows - 8:, :]


def _prep(proj, weights, tri, *, n_seq, seq_len, rows, chunk, row_block_offset, extra, sample, name):
    t = n_seq * seq_len
    n_chunks = t // chunk
    cpt = rows // chunk
    if sample:
        grid = (t // rows,)
        rmap = lambda i: (i + row_block_offset, 0)
        omap = lambda i: (i, 0)
        omap3 = lambda i: (i, 0, 0)
        cmap = lambda i: (0, 0)
        sem = ("parallel",)
    else:
        tiles = seq_len // rows
        grid = (n_seq, tiles)
        rmap = lambda b, j: (b * tiles + j + row_block_offset, 0)
        omap = lambda b, j: (b * tiles + j, 0)
        omap3 = lambda b, j: (b * tiles + j, 0, 0)
        cmap = lambda b, j: (0, 0)
        sem = ("parallel", "arbitrary")

    const = lambda arr: pl.BlockSpec(arr.shape, cmap)
    in_specs = [pl.BlockSpec((rows, PW), rmap)]
    if sample:
        in_specs += [pl.BlockSpec((rows, RW), omap), pl.BlockSpec((rows, G), omap),
                     pl.BlockSpec((rows, G), omap)]
    else:
        in_specs += [const(e) for e in extra]
    in_specs += [const(w) for w in weights] + [const(tri)]
    args = [proj] + list(extra) + list(weights) + [tri]

    row_out = jax.ShapeDtypeStruct((t, G), F32)
    row_spec = pl.BlockSpec((rows, G), omap)
    out_shape = [row_out] * 5 + [jax.ShapeDtypeStruct((n_chunks, 1, G), F32), row_out, row_out,
                                 jax.ShapeDtypeStruct((t, G), BF16)]
    out_specs = [row_spec] * 5 + [pl.BlockSpec((cpt, 1, G), omap3), row_spec, row_spec, row_spec]
    if sample:
        out_shape.append(row_out)
        out_specs.append(row_spec)
        scratch = []
    else:
        out_shape.append(jax.ShapeDtypeStruct((t // rows, 8, G), F32))
        out_specs.append(pl.BlockSpec((1, 8, G), omap3))
        scratch = [pltpu.VMEM((8, RW), F32), pltpu.VMEM((8, G), F32)]
    return pl.pallas_call(
        functools.partial(_prep_kernel, rows=rows, chunk=chunk, sample=sample),
        grid=grid, in_specs=in_specs, out_specs=out_specs, out_shape=out_shape,
        scratch_shapes=scratch,
        compiler_params=pltpu.CompilerParams(dimension_semantics=sem, vmem_limit_bytes=VMEM_LIMIT),
        name=name,
    )(*args)


def _scan_kernel(ah_ref, rh_ref, bh_ref, kh_ref, v_ref, wc_ref, s0_ref, y_ref, s_ref,
                 *, n_par, rows, chunk, per_chunk_state, group):
    if per_chunk_state:
        s_in = s0_ref
    else:
        s_in = s_ref

        @pl.when(pl.program_id(1) == 0)
        def _():
            for q in range(n_par):
                s_ref[q] = s0_ref[0]

    n_blk = rows // chunk
    lanes = [slice(h * HEAD_DIM, (h + 1) * HEAD_DIM) for h in range(N_HEADS)]
    bf = lambda x: x.astype(BF16)

    ri = lax.broadcasted_iota(jnp.int32, (rows, 2 * rows), 0)
    ci = lax.broadcasted_iota(jnp.int32, (rows, 2 * rows), 1)
    right = ci >= rows
    cj = jnp.where(right, ci - rows, ci)
    shift = chunk.bit_length() - 1
    same = (ri >> shift) == (cj >> shift)
    mask_ak = same & (ri > cj) & right
    mask_r = same & (ri >= cj)
    rs = lax.broadcasted_iota(jnp.int32, (rows, rows), 0)
    cs = lax.broadcasted_iota(jnp.int32, (rows, rows), 1)
    mask_ab = ((rs >> shift) == (cs >> shift)) & (rs > cs)
    eye = (rs == cs).astype(F32)

    def chunk_rows(x, c):
        return x[c * chunk:(c + 1) * chunk, :]

    def unit_group(units):
        per_unit = lambda f: {u: f(*u) for u in units}
        a = per_unit(lambda q, h: ah_ref[q, :, lanes[h]])
        r = per_unit(lambda q, h: rh_ref[q, :, lanes[h]])
        b = per_unit(lambda q, h: bh_ref[q, :, lanes[h]])
        k = per_unit(lambda q, h: kh_ref[q, :, lanes[h]])
        v = per_unit(lambda q, h: v_ref[q, :, lanes[h]])
        ar = {u: jnp.concatenate([a[u], r[u]], axis=0) for u in units}
        bk = {u: jnp.concatenate([b[u], k[u]], axis=0) for u in units}

        gram = {u: _dot_nt(bf(ar[u]), bf(bk[u])) for u in units}
        l_ab = {u: jnp.where(mask_ab, gram[u][:rows, :rows], 0.0) for u in units}
        l_ak = {u: bf(jnp.where(mask_ak, gram[u][:rows, :], 0.0)) for u in units}
        l_r = {u: bf(jnp.where(mask_r, gram[u][rows:, :], 0.0)) for u in units}

        xs, ys = {}, {}
        for u in units:
            q, h = u
            if n_blk == 1:
                st = _dot_nt(bf(ar[u]), bf(s_in[q, h]))
                xs[u], ys[u] = st[:rows], st[rows:]
            else:
                parts = [_dot_nt(jnp.concatenate([chunk_rows(a[u], c), chunk_rows(r[u], c)], axis=0),
                                 s_in[c, h]) for c in range(n_blk)]
                xs[u] = jnp.concatenate([p[:chunk] for p in parts], axis=0)
                ys[u] = jnp.concatenate([p[chunk:] for p in parts], axis=0)

        vv = {u: bf(jnp.concatenate([v[u], v[u]], axis=0)) for u in units}
        x = {u: xs[u] + _dot(l_ak[u], vv[u]) for u in units}

        t_inv = {u: eye + l_ab[u] for u in units}
        pw = {u: bf(l_ab[u]) for u in units}
        n = 1
        while 2 * n < chunk:
            pw = {u: bf(_dot(pw[u], pw[u])) for u in units}
            t_inv = {u: t_inv[u] + _dot(bf(t_inv[u]), pw[u]) for u in units}
            n *= 2

        uu = {u: _dot(bf(t_inv[u]), bf(x[u])) for u in units}
        uv = {u: jnp.concatenate([uu[u], v[u]], axis=0) for u in units}
        for u in units:
            q, h = u
            y_ref[q, :, lanes[h]] = ys[u] + _dot(l_r[u], bf(uv[u]))

        for u in units:
            q, h = u
            if n_blk == 1:
                upd = _dot_tn(bf(uv[u]), bf(bk[u]))
                s_ref[q, h] = (s_in[q, h] + upd) * wc_ref[q, 0][:, lanes[h]]
            else:
                for c in range(n_blk):
                    uv_c = jnp.concatenate([chunk_rows(uu[u], c), chunk_rows(v[u], c)], axis=0)
                    bk_c = jnp.concatenate([chunk_rows(b[u], c), chunk_rows(k[u], c)], axis=0)
                    s_ref[c, h] = (s_in[c, h] + _dot_tn(uv_c, bk_c)) * wc_ref[q, c][:, lanes[h]]

    all_units = [(q, h) for h in range(N_HEADS) for q in range(n_par)]
    for g0 in range(0, len(all_units), group):
        unit_group(all_units[g0:g0 + group])


def _scan(ah, rh, bh, kh, v, wc, s0, *, n_seq, seq_len, n_par, rows, chunk, per_chunk_state, group,
          name):
    t = n_seq * seq_len
    n_blk = rows // chunk
    if per_chunk_state:
        assert seq_len == chunk and n_par == 1
        lead = 1
        grid = (t // rows,)
        rmap = lambda i: (0, i, 0)
        cmap = lambda i: (0, i, 0, 0)
        smap = lambda i: (i, 0, 0, 0)
        s0map = smap
        state_block = (n_blk, N_HEADS, HEAD_DIM, HEAD_DIM)
        s0_block = state_block
        wc_block = (1, n_blk, 1, G)
        sem = ("parallel",)
    else:
        assert rows == chunk and s0.shape[0] == 1 and n_seq % n_par == 0
        lead = n_seq
        grid = (n_seq // n_par, seq_len // chunk)
        rmap = lambda s, c: (s, c, 0)
        cmap = lambda s, c: (s, c, 0, 0)
        smap = lambda s, c: (s, 0, 0, 0)
        s0map = lambda s, c: (0, 0, 0, 0)
        state_block = (n_par, N_HEADS, HEAD_DIM, HEAD_DIM)
        s0_block = (1, N_HEADS, HEAD_DIM, HEAD_DIM)
        wc_block = (n_par, 1, 1, G)
        sem = ("parallel", "arbitrary")
    rows3 = lambda x: x.reshape(lead, t // lead, G)
    row_spec = pl.BlockSpec((n_par, rows, G), rmap)
    y, s_out = pl.pallas_call(
        functools.partial(_scan_kernel, n_par=n_par, rows=rows, chunk=chunk,
                          per_chunk_state=per_chunk_state, group=group),
        grid=grid,
        in_specs=[row_spec] * 5 + [pl.BlockSpec(wc_block, cmap), pl.BlockSpec(s0_block, s0map)],
        out_specs=[row_spec, pl.BlockSpec(state_block, smap)],
        out_shape=[jax.ShapeDtypeStruct((lead, t // lead, G), F32),
                   jax.ShapeDtypeStruct((n_seq, N_HEADS, HEAD_DIM, HEAD_DIM), F32)],
        compiler_params=pltpu.CompilerParams(dimension_semantics=sem, vmem_limit_bytes=VMEM_LIMIT),
        name=name,
    )(rows3(ah), rows3(rh), rows3(bh), rows3(kh), rows3(v), wc.reshape(lead, -1, 1, G), s0)
    return y.reshape(t, G), s_out


def _mix_kernel(*refs, segs):
    n = len(segs)
    row_refs = [refs[4 * s:4 * s + 4] for s in range(n)]
    x_ref, lnw_ref, lnb_ref, bd_ref, wo_ref, o_ref = refs[4 * n:]
    i = pl.program_id(0)

    def body(y_ref, bonus_ref, g_ref, conv_ref):
        bd = bd_ref[...]
        y = y_ref[...]
        mu = _head_sum(y, bd) * (1.0 / HEAD_DIM)
        d = y - mu
        var = _head_sum(d * d, bd) * (1.0 / HEAD_DIM)
        yn = d * lax.rsqrt(var + GN_EPS) * lnw_ref[...] + lnb_ref[...]
        rw = ((yn + bonus_ref[...]) * g_ref[...]).astype(BF16)
        mix = _dot(rw, wo_ref[0:G, :]) + _dot(conv_ref[...], wo_ref[G:2 * G, :])
        o_ref[...] = x_ref[...] + mix

    for rr, (off, cnt) in zip(row_refs, segs):
        @pl.when((i >= off) & (i < off + cnt))
        def _(rr=rr):
            body(*rr)


def _mix(row_groups, x1, lnw, lnb, bd, wo):
    segs, n_tiles = _segments([grp[0] for grp in row_groups], TM_MIX)
    cmap = lambda i: (0, 0)
    in_specs, args = [], []
    for grp, (off, cnt) in zip(row_groups, segs):
        smap = lambda i, off=off, cnt=cnt: (jnp.clip(i - off, 0, cnt - 1), 0)
        in_specs += [pl.BlockSpec((TM_MIX, G), smap)] * 4
        args += list(grp)
    in_specs += [pl.BlockSpec((TM_MIX, D_MODEL), lambda i: (i, 0)),
                 pl.BlockSpec((1, G), cmap), pl.BlockSpec((1, G), cmap),
                 pl.BlockSpec((G, G), cmap), pl.BlockSpec((D_MODEL, D_MODEL), cmap)]
    return pl.pallas_call(
        functools.partial(_mix_kernel, segs=segs),
        grid=(n_tiles,), in_specs=in_specs,
        out_specs=pl.BlockSpec((TM_MIX, D_MODEL), lambda i: (i, 0)),
        out_shape=jax.ShapeDtypeStruct((n_tiles * TM_MIX, D_MODEL), F32),
        compiler_params=pltpu.CompilerParams(
            dimension_semantics=("parallel",), vmem_limit_bytes=VMEM_LIMIT),
        name="mix",
    )(*args, x1, lnw, lnb, bd, wo)


def _pack_rwkv(a):
    r, wd, k, v, ad, gd = jnp.split(a, RWKV_SPLITS, axis=-1)
    zw = jnp.zeros(a.shape[:-1] + (LORA_PAD - W_LORA,), a.dtype)
    za = jnp.zeros(a.shape[:-1] + (LORA_PAD - A_LORA,), a.dtype)
    return jnp.concatenate([r, k, v, wd, zw, ad, za, gd], axis=-1)


def _unpack_rwkv(p):
    return jnp.concatenate([
        p[..., 0:G], p[..., OFF_WD:OFF_WD + W_LORA], p[..., G:2 * G], p[..., 2 * G:3 * G],
        p[..., OFF_AD:OFF_AD + A_LORA], p[..., OFF_GD:OFF_GD + G_LORA]], axis=-1)


def _block_tri(n, chunk):
    i = jnp.arange(n)
    return ((i[:, None] // chunk == i[None, :] // chunk) & (i[:, None] >= i[None, :])).astype(BF16)


def kernel(x_prompt, x_sample, state_wkv, state_shift, state_conv, meta_tokens, g_ffn1, ffn1_gate, ffn1_up, ffn1_down, g_mix, w_in, mu_shift, w0, w_lora_w, a0, w_lora_a, w_lora_g, k_k, k_a, r_k, ln_x_w, ln_x_b, conv_w, w_out, g_ffn2, ffn2_gate, ffn2_up, ffn2_down, g_final):
    assert g_ffn1.shape[0] == 1, "single layer"
    nb, seq, _ = x_prompt.shape
    db, dseq, _ = x_sample.shape
    assert dseq == C_SAMPLE and N_META <= C_PROMPT and seq % ROWS_PREP == 0
    tp, ts = nb * seq, db * dseq
    assert tp % TM == 0 and ts % TM == 0

    row = lambda a: a.reshape(1, -1).astype(F32)
    w_in_p = _pack_w_in(w_in[0])
    pad_rows = lambda w, n: jnp.concatenate([w, jnp.zeros((n - w.shape[0], w.shape[1]), w.dtype)], axis=0)
    hid = jnp.arange(G) // HEAD_DIM
    bd = (hid[:, None] == hid[None, :]).astype(BF16)
    prep_w = (_pack_rwkv(mu_shift[0])[None], row(w0[0]), pad_rows(w_lora_w[0], LORA_PAD).astype(BF16),
              row(a0[0]), pad_rows(w_lora_a[0], LORA_PAD).astype(BF16), w_lora_g[0].astype(BF16),
              row(k_k[0]), row(k_a[0]), row(r_k[0]), conv_w[0].astype(F32), bd)

    tail_rows = ts + C_PROMPT
    tail_rows += (-tail_rows) % TM
    x_tail = jnp.concatenate([
        x_sample.reshape(ts, D_MODEL), jnp.zeros((C_PROMPT - N_META, D_MODEL), F32),
        meta_tokens.astype(F32), jnp.zeros((tail_rows - ts - C_PROMPT, D_MODEL), F32)], axis=0)
    meta_row0 = tp + ts

    x1 = _ffn([x_prompt.reshape(tp, D_MODEL), x_tail], row(g_ffn1[0]), ffn1_gate[0].astype(BF16),
              ffn1_up[0].astype(BF16), ffn1_down[0].astype(BF16), None, [tp + tail_rows], name="ffn1")[0]
    proj = _proj(x1, row(g_mix[0]), w_in_p)

    tri_p = _block_tri(ROWS_PREP, C_PROMPT)
    zeros_state = jnp.zeros((1, N_HEADS, HEAD_DIM, HEAD_DIM), F32)
    (ah, rh, bh, kh, vv, wc, _, _, _, utail_m) = _prep(
        proj, prep_w, tri_p[:C_PROMPT, :C_PROMPT], n_seq=1, seq_len=C_PROMPT, rows=C_PROMPT, chunk=C_PROMPT,
        row_block_offset=meta_row0 // C_PROMPT, extra=(jnp.zeros((8, RW), F32), jnp.zeros((8, G), F32)),
        sample=False, name="prep_meta")
    _, wkv_m = _scan(ah, rh, bh, kh, vv, wc, zeros_state, n_seq=1, seq_len=C_PROMPT, n_par=1,
                     rows=C_PROMPT, chunk=C_PROMPT, per_chunk_state=False, group=N_HEADS,
                     name="scan_meta")
    ptail_m = proj[meta_row0 + C_PROMPT - 8:meta_row0 + C_PROMPT, :RW]

    (ah, rh, bh, kh, vv, wc, g_p, bonus_p, conv_p, utail_p) = _prep(
        proj, prep_w, tri_p, n_seq=nb, seq_len=seq, rows=ROWS_PREP, chunk=C_PROMPT,
        row_block_offset=0, extra=(ptail_m, utail_m[0]), sample=False, name="prep_prompt")
    y_p, wkv_p = _scan(ah, rh, bh, kh, vv, wc, wkv_m, n_seq=nb, seq_len=seq, n_par=SCAN_PAR,
                       rows=C_PROMPT, chunk=C_PROMPT, per_chunk_state=False,
                       group=SCAN_PAR * N_HEADS, name="scan_prompt")

    first = lambda a: jnp.zeros((db, dseq) + a.shape[1:], F32).at[:, 0].set(a)
    prev = first(_pack_rwkv(state_shift[0])).reshape(ts, RW)
    up1 = first(state_conv[0][:, 1]).reshape(ts, G)
    up2 = first(state_conv[0][:, 0]).at[:, 1].set(state_conv[0][:, 1]).reshape(ts, G)
    (ah, rh, bh, kh, vv, wc, g_s, bonus_s, conv_s, u_s) = _prep(
        proj, prep_w, _block_tri(ROWS_PREP, C_SAMPLE), n_seq=db, seq_len=dseq, rows=ROWS_PREP,
        chunk=C_SAMPLE, row_block_offset=tp // ROWS_PREP, extra=(prev, up1, up2), sample=True,
        name="prep_sample")
    y_s, wkv_s = _scan(ah, rh, bh, kh, vv, wc, state_wkv[0], n_seq=db, seq_len=dseq, n_par=1,
                       rows=ROWS_SCAN_SAMPLE, chunk=C_SAMPLE, per_chunk_state=True, group=4,
                       name="scan_sample")

    x2 = _mix([(y_p, bonus_p, g_p, conv_p), (y_s, bonus_s, g_s, conv_s)], x1,
              row(ln_x_w[0]), row(ln_x_b[0]), bd, w_out[0].astype(BF16))
    y_prompt, y_sample = _ffn([x2], row(g_ffn2[0]), ffn2_gate[0].astype(BF16), ffn2_up[0].astype(BF16),
                              ffn2_down[0].astype(BF16), row(g_final), [tp, ts], name="ffn2")

    shift_p = _unpack_rwkv(proj[seq - 1:tp:seq, :RW])
    conv_state_p = utail_p.reshape(nb, seq // ROWS_PREP, 8, G)[:, -1, 6:, :]
    shift_s = _unpack_rwkv(proj[tp + dseq - 1:tp + ts:dseq, :RW])
    conv_state_s = u_s.reshape(db, dseq, G)[:, -2:, :]
    return (y_prompt.reshape(nb, seq, D_MODEL), y_sample.reshape(db, dseq, D_MODEL),
            wkv_p[None].astype(state_wkv.dtype), shift_p[None].astype(state_shift.dtype),
            conv_state_p[None].astype(state_conv.dtype),
            wkv_s[None].astype(state_wkv.dtype), shift_s[None].astype(state_shift.dtype),
            conv_state_s[None].astype(state_conv.dtype))
```

```python
import functools

import jax
import jax.numpy as jnp
from jax import lax
from jax.experimental import pallas as pl
from jax.experimental.pallas import tpu as pltpu

F32 = jnp.float32
BF16 = jnp.bfloat16

D_MODEL = 2048
D_FF = 5632
N_META = 16
G = 1024
HEAD_DIM = 64
N_HEADS = G // HEAD_DIM
W_LORA = 96
A_LORA = 96
G_LORA = 256
LORA_PAD = 128
RWKV_PROJ = 3 * G + W_LORA + A_LORA + G_LORA
RWKV_SPLITS = (G, G + W_LORA, 2 * G + W_LORA, 3 * G + W_LORA, 3 * G + W_LORA + A_LORA)
RW = 3 * G + 2 * LORA_PAD + G_LORA
PW = RW + 3 * G
OFF_WD, OFF_AD, OFF_GD = 3 * G, 3 * G + LORA_PAD, 3 * G + 2 * LORA_PAD
RMS_EPS = 1e-6
GN_EPS = 64e-5
EXP_M05 = 0.6065306597126334

C_PROMPT = 64
C_SAMPLE = 8
TM = 512
TF = 512
TN = 1664
TM_MIX = 256
ROWS_PREP = 256
ROWS_SCAN_SAMPLE = 128
SCAN_PAR = 4
VMEM_LIMIT = 56 * 1024 * 1024


def _dot(a, b):
    return jnp.dot(a, b, preferred_element_type=F32)


def _dot_nt(a, b):
    return lax.dot_general(a, b, (((1,), (1,)), ((), ())), preferred_element_type=F32)


def _dot_tn(a, b):
    return lax.dot_general(a, b, (((0,), (0,)), ((), ())), preferred_element_type=F32)


def _split(a):
    hi = a.astype(BF16)
    lo = (a - hi.astype(F32)).astype(BF16)
    return hi, lo


def _head_sum(x, bd):
    return _dot(x.astype(BF16), bd)


def _dot_split_rhs(a_bf16, b):
    hi, lo = _split(b)
    return _dot(a_bf16, hi) + _dot(a_bf16, lo)


def _rms(x, g):
    return x * lax.rsqrt(jnp.mean(x * x, axis=-1, keepdims=True) + RMS_EPS) * g


def _segments(arrays, tile):
    segs, off = [], 0
    for arr in arrays:
        n = arr.shape[0] // tile
        assert n * tile == arr.shape[0]
        segs.append((off, n))
        off += n
    return segs, off


def _seg_spec(tile, width, seg):
    off, n = seg
    return pl.BlockSpec((tile, width), lambda i, j: (jnp.clip(i - off, 0, n - 1), 0))


def _overlaps(in_segs, out_segs):
    for a, (ao, an) in enumerate(in_segs):
        for b, (bo, bn) in enumerate(out_segs):
            lo, hi = max(ao, bo), min(ao + an, bo + bn)
            if lo < hi:
                yield a, b, lo, hi


def _ffn_kernel(*refs, in_segs, out_segs, final_norm):
    n_in, n_out = len(in_segs), len(out_segs)
    x_refs = refs[:n_in]
    g_ref, wg_ref, wu_ref, wd_ref = refs[n_in:n_in + 4]
    rest = refs[n_in + 4:]
    if final_norm:
        gf_ref, rest = rest[0], rest[1:]
    o_refs = rest[:n_out]
    h_ref, acc_ref = rest[n_out:]
    i = pl.program_id(0)
    j = pl.program_id(1)

    for x_ref, (off, n) in zip(x_refs, in_segs):
        @pl.when((j == 0) & (i >= off) & (i < off + n))
        def _(x_ref=x_ref):
            h_ref[...] = _rms(x_ref[...], g_ref[...]).astype(BF16)
            acc_ref[...] = jnp.zeros_like(acc_ref)

    h = h_ref[...]
    gate = _dot(h, wg_ref[...])
    up = _dot(h, wu_ref[...])
    act = (gate * jax.nn.sigmoid(gate) * up).astype(BF16)
    acc_ref[...] += _dot(act, wd_ref[...])

    for a, b, lo, hi in _overlaps(in_segs, out_segs):
        @pl.when((j == pl.num_programs(1) - 1) & (i >= lo) & (i < hi))
        def _(x_ref=x_refs[a], o_ref=o_refs[b]):
            xo = x_ref[...] + 0.5 * acc_ref[...]
            if final_norm:
                xo = _rms(xo, gf_ref[...])
            o_ref[...] = xo


def _ffn(xs, g, wg, wu, wd, g_final, out_rows, *, name):
    final_norm = g_final is not None
    in_segs, n_tiles_in = _segments(xs, TM)
    out_shape = [jax.ShapeDtypeStruct((n, D_MODEL), F32) for n in out_rows]
    out_segs, n_tiles = _segments(out_shape, TM)
    assert n_tiles <= n_tiles_in
    cmap = lambda i, j: (0, 0)
    in_specs = [_seg_spec(TM, D_MODEL, s) for s in in_segs] + [
        pl.BlockSpec((1, D_MODEL), cmap),
        pl.BlockSpec((D_MODEL, TF), lambda i, j: (0, j)),
        pl.BlockSpec((D_MODEL, TF), lambda i, j: (0, j)),
        pl.BlockSpec((TF, D_MODEL), lambda i, j: (j, 0)),
    ]
    args = list(xs) + [g, wg, wu, wd]
    if final_norm:
        in_specs.append(pl.BlockSpec((1, D_MODEL), cmap))
        args.append(g_final)
    return pl.pallas_call(
        functools.partial(_ffn_kernel, in_segs=in_segs, out_segs=out_segs, final_norm=final_norm),
        grid=(n_tiles, D_FF // TF),
        in_specs=in_specs,
        out_specs=[_seg_spec(TM, D_MODEL, s) for s in out_segs],
        out_shape=out_shape,
        scratch_shapes=[pltpu.VMEM((TM, D_MODEL), BF16), pltpu.VMEM((TM, D_MODEL), F32)],
        compiler_params=pltpu.CompilerParams(
            dimension_semantics=("arbitrary", "arbitrary"), vmem_limit_bytes=VMEM_LIMIT),
        name=name,
    )(*args)


_PACK_MOVES = (
    (0, 0, G),
    (G, RWKV_SPLITS[1], G),
    (2 * G, RWKV_SPLITS[2], G),
    (OFF_WD, RWKV_SPLITS[0], W_LORA),
    (OFF_AD, RWKV_SPLITS[3], A_LORA),
    (OFF_GD, RWKV_SPLITS[4], G_LORA),
    (RW, RWKV_PROJ, 3 * G),
)


def _pack_w_in_kernel(w_ref, o_ref):
    for dst, src, n in _PACK_MOVES:
        o_ref[dst:dst + n, :] = w_ref[src:src + n, :].astype(BF16)
    for lo, hi in ((OFF_WD + W_LORA, OFF_AD), (OFF_AD + A_LORA, OFF_GD)):
        o_ref[lo:hi, :] = jnp.zeros((hi - lo, o_ref.shape[1]), BF16)


def _pack_w_in(w_t):
    cols = 512
    return pl.pallas_call(
        _pack_w_in_kernel,
        grid=(D_MODEL // cols,),
        in_specs=[pl.BlockSpec((w_t.shape[0], cols), lambda i: (0, i))],
        out_specs=pl.BlockSpec((PW, cols), lambda i: (0, i)),
        out_shape=jax.ShapeDtypeStruct((PW, D_MODEL), BF16),
        compiler_params=pltpu.CompilerParams(
            dimension_semantics=("parallel",), vmem_limit_bytes=VMEM_LIMIT),
        name="pack_w_in",
    )(w_t)


def _proj_kernel(x_ref, g_ref, w_ref, o_ref, h_ref):
    @pl.when(pl.program_id(1) == 0)
    def _():
        h_ref[...] = _rms(x_ref[...], g_ref[...]).astype(BF16)

    o_ref[...] = _dot_nt(h_ref[...], w_ref[...])


def _proj(x, g, w):
    t = x.shape[0]
    return pl.pallas_call(
        _proj_kernel,
        grid=(t // TM, PW // TN),
        in_specs=[
            pl.BlockSpec((TM, D_MODEL), lambda i, j: (i, 0)),
            pl.BlockSpec((1, D_MODEL), lambda i, j: (0, 0)),
            pl.BlockSpec((TN, D_MODEL), lambda i, j: (j, 0)),
        ],
        out_specs=pl.BlockSpec((TM, TN), lambda i, j: (i, j)),
        out_shape=jax.ShapeDtypeStruct((t, PW), F32),
        scratch_shapes=[pltpu.VMEM((TM, D_MODEL), BF16)],
        compiler_params=pltpu.CompilerParams(
            dimension_semantics=("parallel", "arbitrary"), vmem_limit_bytes=VMEM_LIMIT),
        name="proj",
    )(x, g, w)


def _prep_kernel(*refs, rows, chunk, sample):
    if sample:
        (p_ref, prev_ref, up1_ref, up2_ref, mu_ref, w0_ref, ww_ref, a0_ref, wa_ref, wgl_ref,
         kk_ref, ka_ref, rk_ref, cw_ref, bd_ref, tri_ref,
         ah_ref, rh_ref, bh_ref, kh_ref, v_ref, wc_ref, g_ref, bonus_ref, conv_ref, u_ref) = refs
    else:
        (p_ref, cp0_ref, cu0_ref, mu_ref, w0_ref, ww_ref, a0_ref, wa_ref, wgl_ref,
         kk_ref, ka_ref, rk_ref, cw_ref, bd_ref, tri_ref,
         ah_ref, rh_ref, bh_ref, kh_ref, v_ref, wc_ref, g_ref, bonus_ref, conv_ref, u_ref,
         ptail_ref, carry_p, carry_u) = refs

        @pl.when(pl.program_id(1) == 0)
        def _():
            carry_p[...] = cp0_ref[...]
            carry_u[...] = cu0_ref[...]

    def row_ids(width):
        r = lax.broadcasted_iota(jnp.int32, (rows, width), 0)
        return (r & 7) if sample else r

    def shifted_mix(lo, hi):
        p = p_ref[:, lo:hi]
        rolled = pltpu.roll(p, 1, 0)
        if sample:
            sh = jnp.where(row_ids(hi - lo) == 0, prev_ref[:, lo:hi], rolled)
        else:
            sh = jnp.where(row_ids(hi - lo) == 0, carry_p[7:8, lo:hi], rolled)
        return p + (sh - p) * mu_ref[:, lo:hi]

    r = shifted_mix(0, G)
    k = shifted_mix(G, 2 * G)
    v = shifted_mix(2 * G, 3 * G)
    wd = shifted_mix(OFF_WD, OFF_WD + LORA_PAD)
    ad = shifted_mix(OFF_AD, OFF_AD + LORA_PAD)
    gd = shifted_mix(OFF_GD, OFF_GD + G_LORA)

    z = w0_ref[...] + _dot(jnp.tanh(wd).astype(BF16), ww_ref[...])
    lw = -EXP_M05 * jax.nn.sigmoid(z)
    a = jax.nn.sigmoid(a0_ref[...] + _dot(ad.astype(BF16), wa_ref[...]))
    g_ref[...] = _dot(jax.nn.sigmoid(gd).astype(BF16), wgl_ref[...])

    bd = bd_ref[...]
    kk = k * kk_ref[...]
    norm = jnp.sqrt(_head_sum(kk * kk, bd))
    kk = kk / jnp.maximum(norm, 1e-12)
    km = k * (1.0 + (a - 1.0) * ka_ref[...])
    bonus_ref[...] = _head_sum(r * km * rk_ref[...], bd) * v

    cum = _dot_split_rhs(tri_ref[...], lw)
    e_cum = jnp.exp(cum)
    e_inv = jnp.exp(-cum)
    ah_ref[...] = -kk * jnp.exp(cum - lw)
    rh_ref[...] = r * e_cum
    bh_ref[...] = kk * a * e_inv
    kh_ref[...] = km * e_inv
    v_ref[...] = v
    for c in range(rows // chunk):
        wc_ref[c] = e_cum[(c + 1) * chunk - 1:(c + 1) * chunk, :]

    bg = p_ref[:, RW:RW + G]
    u = p_ref[:, RW + G:RW + 2 * G] * p_ref[:, RW + 2 * G:RW + 3 * G]
    u1 = pltpu.roll(u, 1, 0)
    u2 = pltpu.roll(u, 2, 0)
    rid = row_ids(G)
    if sample:
        um1 = jnp.where(rid == 0, up1_ref[...], u1)
        um2 = jnp.where(rid < 2, up2_ref[...], u2)
        u_ref[...] = u
    else:
        um1 = jnp.where(rid == 0, carry_u[7:8, :], u1)
        um2 = jnp.where(rid == 0, carry_u[6:7, :], jnp.where(rid == 1, carry_u[7:8, :], u2))
        u_ref[0] = u[rows - 8:, :]
        ptail_ref[0] = p_ref[rows - 8:rows, 0:RW]
    conv = cw_ref[0:1, :] * um2 + cw_ref[1:2, :] * um1 + cw_ref[2:3, :] * u
    conv_ref[...] = (bg * conv).astype(BF16)

    if not sample:
        carry_p[...] = p_ref[rows - 8:rows, 0:RW]
        carry_u[...] = u[rows - 8:, :]


def _prep(proj, weights, tri, *, n_seq, seq_len, rows, chunk, row_block_offset, extra, sample, name):
    t = n_seq * seq_len
    n_chunks = t // chunk
    cpt = rows // chunk
    if sample:
        grid = (t // rows,)
        rmap = lambda i: (i + row_block_offset, 0)
        omap = lambda i: (i, 0)
        omap3 = lambda i: (i, 0, 0)
        cmap = lambda i: (0, 0)
        sem = ("parallel",)
    else:
        tiles = seq_len // rows
        grid = (n_seq, tiles)
        rmap = lambda b, j: (b * tiles + j + row_block_offset, 0)
        omap = lambda b, j: (b * tiles + j, 0)
        omap3 = lambda b, j: (b * tiles + j, 0, 0)
        cmap = lambda b, j: (0, 0)
        sem = ("parallel", "arbitrary")

    const = lambda arr: pl.BlockSpec(arr.shape, cmap)
    in_specs = [pl.BlockSpec((rows, PW), rmap)]
    if sample:
        in_specs += [pl.BlockSpec((rows, RW), omap), pl.BlockSpec((rows, G), omap),
                     pl.BlockSpec((rows, G), omap)]
    else:
        in_specs += [const(e) for e in extra]
    in_specs += [const(w) for w in weights] + [const(tri)]
    args = [proj] + list(extra) + list(weights) + [tri]

    row_out = jax.ShapeDtypeStruct((t, G), F32)
    row_spec = pl.BlockSpec((rows, G), omap)
    out_shape = [row_out] * 5 + [jax.ShapeDtypeStruct((n_chunks, 1, G), F32), row_out, row_out,
                                 jax.ShapeDtypeStruct((t, G), BF16)]
    out_specs = [row_spec] * 5 + [pl.BlockSpec((cpt, 1, G), omap3), row_spec, row_spec, row_spec]
    if sample:
        out_shape.append(row_out)
        out_specs.append(row_spec)
        scratch = []
    else:
        out_shape += [jax.ShapeDtypeStruct((t // rows, 8, G), F32),
                      jax.ShapeDtypeStruct((t // rows, 8, RW), F32)]
        out_specs += [pl.BlockSpec((1, 8, G), omap3), pl.BlockSpec((1, 8, RW), omap3)]
        scratch = [pltpu.VMEM((8, RW), F32), pltpu.VMEM((8, G), F32)]
    return pl.pallas_call(
        functools.partial(_prep_kernel, rows=rows, chunk=chunk, sample=sample),
        grid=grid, in_specs=in_specs, out_specs=out_specs, out_shape=out_shape,
        scratch_shapes=scratch,
        compiler_params=pltpu.CompilerParams(dimension_semantics=sem, vmem_limit_bytes=VMEM_LIMIT),
        name=name,
    )(*args)


def _scan_kernel(ah_ref, rh_ref, bh_ref, kh_ref, v_ref, wc_ref, s0_ref, y_ref, s_ref,
                 *, n_par, rows, chunk, per_chunk_state, group):
    if per_chunk_state:
        s_in = s0_ref
    else:
        s_in = s_ref

        @pl.when(pl.program_id(1) == 0)
        def _():
            for q in range(n_par):
                s_ref[q] = s0_ref[0]

    n_blk = rows // chunk
    lanes = [slice(h * HEAD_DIM, (h + 1) * HEAD_DIM) for h in range(N_HEADS)]
    bf = lambda x: x.astype(BF16)

    ri = lax.broadcasted_iota(jnp.int32, (rows, 2 * rows), 0)
    ci = lax.broadcasted_iota(jnp.int32, (rows, 2 * rows), 1)
    right = ci >= rows
    cj = jnp.where(right, ci - rows, ci)
    shift = chunk.bit_length() - 1
    same = (ri >> shift) == (cj >> shift)
    mask_ak = same & (ri > cj) & right
    mask_r = same & (ri >= cj)
    rs = lax.broadcasted_iota(jnp.int32, (rows, rows), 0)
    cs = lax.broadcasted_iota(jnp.int32, (rows, rows), 1)
    mask_ab = ((rs >> shift) == (cs >> shift)) & (rs > cs)
    eye = (rs == cs).astype(F32)

    def chunk_rows(x, c):
        return x[c * chunk:(c + 1) * chunk, :]

    def unit_group(units):
        per_unit = lambda f: {u: f(*u) for u in units}
        a = per_unit(lambda q, h: ah_ref[q, :, lanes[h]])
        r = per_unit(lambda q, h: rh_ref[q, :, lanes[h]])
        b = per_unit(lambda q, h: bh_ref[q, :, lanes[h]])
        k = per_unit(lambda q, h: kh_ref[q, :, lanes[h]])
        v = per_unit(lambda q, h: v_ref[q, :, lanes[h]])
        ar = {u: jnp.concatenate([a[u], r[u]], axis=0) for u in units}
        bk = {u: jnp.concatenate([b[u], k[u]], axis=0) for u in units}

        gram = {u: _dot_nt(bf(ar[u]), bf(bk[u])) for u in units}
        l_ab = {u: jnp.where(mask_ab, gram[u][:rows, :rows], 0.0) for u in units}
        l_ak = {u: bf(jnp.where(mask_ak, gram[u][:rows, :], 0.0)) for u in units}
        l_r = {u: bf(jnp.where(mask_r, gram[u][rows:, :], 0.0)) for u in units}

        xs, ys = {}, {}
        for u in units:
            q, h = u
            if n_blk == 1:
                st = _dot_nt(bf(ar[u]), bf(s_in[q, h]))
                xs[u], ys[u] = st[:rows], st[rows:]
            else:
                parts = [_dot_nt(jnp.concatenate([chunk_rows(a[u], c), chunk_rows(r[u], c)], axis=0),
                                 s_in[c, h]) for c in range(n_blk)]
                xs[u] = jnp.concatenate([p[:chunk] for p in parts], axis=0)
                ys[u] = jnp.concatenate([p[chunk:] for p in parts], axis=0)

        vv = {u: bf(jnp.concatenate([v[u], v[u]], axis=0)) for u in units}
        x = {u: xs[u] + _dot(l_ak[u], vv[u]) for u in units}

        t_inv = {u: eye + l_ab[u] for u in units}
        pw = {u: bf(l_ab[u]) for u in units}
        n = 1
        while 2 * n < chunk:
            pw = {u: bf(_dot(pw[u], pw[u])) for u in units}
            t_inv = {u: t_inv[u] + _dot(bf(t_inv[u]), pw[u]) for u in units}
            n *= 2

        uu = {u: _dot(bf(t_inv[u]), bf(x[u])) for u in units}
        uv = {u: jnp.concatenate([uu[u], v[u]], axis=0) for u in units}
        for u in units:
            q, h = u
            y_ref[q, :, lanes[h]] = ys[u] + _dot(l_r[u], bf(uv[u]))

        for u in units:
            q, h = u
            if n_blk == 1:
                upd = _dot_tn(bf(uv[u]), bf(bk[u]))
                s_ref[q, h] = (s_in[q, h] + upd) * wc_ref[q, 0][:, lanes[h]]
            else:
                for c in range(n_blk):
                    uv_c = jnp.concatenate([chunk_rows(uu[u], c), chunk_rows(v[u], c)], axis=0)
                    bk_c = jnp.concatenate([chunk_rows(b[u], c), chunk_rows(k[u], c)], axis=0)
                    s_ref[c, h] = (s_in[c, h] + _dot_tn(uv_c, bk_c)) * wc_ref[q, c][:, lanes[h]]

    all_units = [(q, h) for h in range(N_HEADS) for q in range(n_par)]
    for g0 in range(0, len(all_units), group):
        unit_group(all_units[g0:g0 + group])


def _scan(ah, rh, bh, kh, v, wc, s0, *, n_seq, seq_len, n_par, rows, chunk, per_chunk_state, group,
          name):
    t = n_seq * seq_len
    n_blk = rows // chunk
    if per_chunk_state:
        assert seq_len == chunk and n_par == 1
        lead = 1
        grid = (t // rows,)
        rmap = lambda i: (0, i, 0)
        cmap = lambda i: (0, i, 0, 0)
        smap = lambda i: (i, 0, 0, 0)
        s0map = smap
        state_block = (n_blk, N_HEADS, HEAD_DIM, HEAD_DIM)
        s0_block = state_block
        wc_block = (1, n_blk, 1, G)
        sem = ("parallel",)
    else:
        assert rows == chunk and s0.shape[0] == 1 and n_seq % n_par == 0
        lead = n_seq
        grid = (n_seq // n_par, seq_len // chunk)
        rmap = lambda s, c: (s, c, 0)
        cmap = lambda s, c: (s, c, 0, 0)
        smap = lambda s, c: (s, 0, 0, 0)
        s0map = lambda s, c: (0, 0, 0, 0)
        state_block = (n_par, N_HEADS, HEAD_DIM, HEAD_DIM)
        s0_block = (1, N_HEADS, HEAD_DIM, HEAD_DIM)
        wc_block = (n_par, 1, 1, G)
        sem = ("parallel", "arbitrary")
    rows3 = lambda x: x.reshape(lead, t // lead, G)
    row_spec = pl.BlockSpec((n_par, rows, G), rmap)
    y, s_out = pl.pallas_call(
        functools.partial(_scan_kernel, n_par=n_par, rows=rows, chunk=chunk,
                          per_chunk_state=per_chunk_state, group=group),
        grid=grid,
        in_specs=[row_spec] * 5 + [pl.BlockSpec(wc_block, cmap), pl.BlockSpec(s0_block, s0map)],
        out_specs=[row_spec, pl.BlockSpec(state_block, smap)],
        out_shape=[jax.ShapeDtypeStruct((lead, t // lead, G), F32),
                   jax.ShapeDtypeStruct((n_seq, N_HEADS, HEAD_DIM, HEAD_DIM), F32)],
        compiler_params=pltpu.CompilerParams(dimension_semantics=sem, vmem_limit_bytes=VMEM_LIMIT),
        name=name,
    )(rows3(ah), rows3(rh), rows3(bh), rows3(kh), rows3(v), wc.reshape(lead, -1, 1, G), s0)
    return y.reshape(t, G), s_out


def _mix_kernel(*refs, segs):
    n = len(segs)
    row_refs = [refs[4 * s:4 * s + 4] for s in range(n)]
    x_ref, lnw_ref, lnb_ref, bd_ref, wo_ref, o_ref = refs[4 * n:]
    i = pl.program_id(0)

    def body(y_ref, bonus_ref, g_ref, conv_ref):
        bd = bd_ref[...]
        y = y_ref[...]
        mu = _head_sum(y, bd) * (1.0 / HEAD_DIM)
        d = y - mu
        var = _head_sum(d * d, bd) * (1.0 / HEAD_DIM)
        yn = d * lax.rsqrt(var + GN_EPS) * lnw_ref[...] + lnb_ref[...]
        rw = ((yn + bonus_ref[...]) * g_ref[...]).astype(BF16)
        mix = _dot(rw, wo_ref[0:G, :]) + _dot(conv_ref[...], wo_ref[G:2 * G, :])
        o_ref[...] = x_ref[...] + mix

    for rr, (off, cnt) in zip(row_refs, segs):
        @pl.when((i >= off) & (i < off + cnt))
        def _(rr=rr):
            body(*rr)


def _mix(row_groups, x1, lnw, lnb, bd, wo):
    segs, n_tiles = _segments([grp[0] for grp in row_groups], TM_MIX)
    cmap = lambda i: (0, 0)
    in_specs, args = [], []
    for grp, (off, cnt) in zip(row_groups, segs):
        smap = lambda i, off=off, cnt=cnt: (jnp.clip(i - off, 0, cnt - 1), 0)
        in_specs += [pl.BlockSpec((TM_MIX, G), smap)] * 4
        args += list(grp)
    in_specs += [pl.BlockSpec((TM_MIX, D_MODEL), lambda i: (i, 0)),
                 pl.BlockSpec((1, G), cmap), pl.BlockSpec((1, G), cmap),
                 pl.BlockSpec((G, G), cmap), pl.BlockSpec((D_MODEL, D_MODEL), cmap)]
    return pl.pallas_call(
        functools.partial(_mix_kernel, segs=segs),
        grid=(n_tiles,), in_specs=in_specs,
        out_specs=pl.BlockSpec((TM_MIX, D_MODEL), lambda i: (i, 0)),
        out_shape=jax.ShapeDtypeStruct((n_tiles * TM_MIX, D_MODEL), F32),
        compiler_params=pltpu.CompilerParams(
            dimension_semantics=("parallel",), vmem_limit_bytes=VMEM_LIMIT),
        name="mix",
    )(*args, x1, lnw, lnb, bd, wo)


def _pack_rwkv(a):
    r, wd, k, v, ad, gd = jnp.split(a, RWKV_SPLITS, axis=-1)
    zw = jnp.zeros(a.shape[:-1] + (LORA_PAD - W_LORA,), a.dtype)
    za = jnp.zeros(a.shape[:-1] + (LORA_PAD - A_LORA,), a.dtype)
    return jnp.concatenate([r, k, v, wd, zw, ad, za, gd], axis=-1)


def _unpack_rwkv(p):
    return jnp.concatenate([
        p[..., 0:G], p[..., OFF_WD:OFF_WD + W_LORA], p[..., G:2 * G], p[..., 2 * G:3 * G],
        p[..., OFF_AD:OFF_AD + A_LORA], p[..., OFF_GD:OFF_GD + G_LORA]], axis=-1)


def _block_tri(n, chunk):
    i = jnp.arange(n)
    return ((i[:, None] // chunk == i[None, :] // chunk) & (i[:, None] >= i[None, :])).astype(BF16)


def kernel(x_prompt, x_sample, state_wkv, state_shift, state_conv, meta_tokens, g_ffn1, ffn1_gate, ffn1_up, ffn1_down, g_mix, w_in, mu_shift, w0, w_lora_w, a0, w_lora_a, w_lora_g, k_k, k_a, r_k, ln_x_w, ln_x_b, conv_w, w_out, g_ffn2, ffn2_gate, ffn2_up, ffn2_down, g_final):
    assert g_ffn1.shape[0] == 1, "single layer"
    nb, seq, _ = x_prompt.shape
    db, dseq, _ = x_sample.shape
    assert dseq == C_SAMPLE and N_META <= C_PROMPT and seq % ROWS_PREP == 0
    tp, ts = nb * seq, db * dseq
    assert tp % TM == 0 and ts % TM == 0

    row = lambda a: a.reshape(1, -1).astype(F32)
    w_in_p = _pack_w_in(jnp.transpose(w_in[0]))
    pad_rows = lambda w, n: jnp.concatenate([w, jnp.zeros((n - w.shape[0], w.shape[1]), w.dtype)], axis=0)
    hid = jnp.arange(G) // HEAD_DIM
    bd = (hid[:, None] == hid[None, :]).astype(BF16)
    prep_w = (_pack_rwkv(mu_shift[0])[None], row(w0[0]), pad_rows(w_lora_w[0], LORA_PAD).astype(BF16),
              row(a0[0]), pad_rows(w_lora_a[0], LORA_PAD).astype(BF16), w_lora_g[0].astype(BF16),
              row(k_k[0]), row(k_a[0]), row(r_k[0]), conv_w[0].astype(F32), bd)

    tail_rows = ts + C_PROMPT
    tail_rows += (-tail_rows) % TM
    x_tail = jnp.concatenate([
        x_sample.reshape(ts, D_MODEL), jnp.zeros((C_PROMPT - N_META, D_MODEL), F32),
        meta_tokens.astype(F32), jnp.zeros((tail_rows - ts - C_PROMPT, D_MODEL), F32)], axis=0)
    meta_row0 = tp + ts

    x1 = _ffn([x_prompt.reshape(tp, D_MODEL), x_tail], row(g_ffn1[0]), ffn1_gate[0].astype(BF16),
              ffn1_up[0].astype(BF16), ffn1_down[0].astype(BF16), None, [tp + tail_rows], name="ffn1")[0]
    proj = _proj(x1, row(g_mix[0]), w_in_p)

    tri_p = _block_tri(ROWS_PREP, C_PROMPT)
    zeros_state = jnp.zeros((1, N_HEADS, HEAD_DIM, HEAD_DIM), F32)
    (ah, rh, bh, kh, vv, wc, _, _, _, utail_m, ptail_m) = _prep(
        proj, prep_w, tri_p[:C_PROMPT, :C_PROMPT], n_seq=1, seq_len=C_PROMPT, rows=C_PROMPT, chunk=C_PROMPT,
        row_block_offset=meta_row0 // C_PROMPT, extra=(jnp.zeros((8, RW), F32), jnp.zeros((8, G), F32)),
        sample=False, name="prep_meta")
    _, wkv_m = _scan(ah, rh, bh, kh, vv, wc, zeros_state, n_seq=1, seq_len=C_PROMPT, n_par=1,
                     rows=C_PROMPT, chunk=C_PROMPT, per_chunk_state=False, group=N_HEADS,
                     name="scan_meta")

    (ah, rh, bh, kh, vv, wc, g_p, bonus_p, conv_p, utail_p, ptail_p) = _prep(
        proj, prep_w, tri_p, n_seq=nb, seq_len=seq, rows=ROWS_PREP, chunk=C_PROMPT,
        row_block_offset=0, extra=(ptail_m[0], utail_m[0]), sample=False, name="prep_prompt")
    y_p, wkv_p = _scan(ah, rh, bh, kh, vv, wc, wkv_m, n_seq=nb, seq_len=seq, n_par=SCAN_PAR,
                       rows=C_PROMPT, chunk=C_PROMPT, per_chunk_state=False,
                       group=SCAN_PAR * N_HEADS, name="scan_prompt")

    first = lambda a: jnp.zeros((db, dseq) + a.shape[1:], F32).at[:, 0].set(a)
    prev = first(_pack_rwkv(state_shift[0])).reshape(ts, RW)
    up1 = first(state_conv[0][:, 1]).reshape(ts, G)
    up2 = first(state_conv[0][:, 0]).at[:, 1].set(state_conv[0][:, 1]).reshape(ts, G)
    (ah, rh, bh, kh, vv, wc, g_s, bonus_s, conv_s, u_s) = _prep(
        proj, prep_w, _block_tri(ROWS_PREP, C_SAMPLE), n_seq=db, seq_len=dseq, rows=ROWS_PREP,
        chunk=C_SAMPLE, row_block_offset=tp // ROWS_PREP, extra=(prev, up1, up2), sample=True,
        name="prep_sample")
    y_s, wkv_s = _scan(ah, rh, bh, kh, vv, wc, state_wkv[0], n_seq=db, seq_len=dseq, n_par=1,
                       rows=ROWS_SCAN_SAMPLE, chunk=C_SAMPLE, per_chunk_state=True, group=4,
                       name="scan_sample")

    x2 = _mix([(y_p, bonus_p, g_p, conv_p), (y_s, bonus_s, g_s, conv_s)], x1,
              row(ln_x_w[0]), row(ln_x_b[0]), bd, w_out[0].astype(BF16))
    y_prompt, y_sample = _ffn([x2], row(g_ffn2[0]), ffn2_gate[0].astype(BF16), ffn2_up[0].astype(BF16),
                              ffn2_down[0].astype(BF16), row(g_final), [tp, ts], name="ffn2")

    shift_p = _unpack_rwkv(ptail_p.reshape(nb, seq // ROWS_PREP, 8, RW)[:, -1, 7, :])
    conv_state_p = utail_p.reshape(nb, seq // ROWS_PREP, 8, G)[:, -1, 6:, :]
    shift_s = _unpack_rwkv(proj[tp + dseq - 1:tp + ts:dseq, :RW])
    conv_state_s = u_s.reshape(db, dseq, G)[:, -2:, :]
    return (y_prompt.reshape(nb, seq, D_MODEL), y_sample.reshape(db, dseq, D_MODEL),
            wkv_p[None].astype(state_wkv.dtype), shift_p[None].astype(state_shift.dtype),
            conv_state_p[None].astype(state_conv.dtype),
            wkv_s[None].astype(state_wkv.dtype), shift_s[None].astype(state_shift.dtype),
            conv_state_s[None].astype(state_conv.dtype))
```

```python
import functools

import jax
import jax.numpy as jnp
from jax import lax
from jax.experimental import pallas as pl
from jax.experimental.pallas import tpu as pltpu

F32 = jnp.float32
BF16 = jnp.bfloat16

D_MODEL = 2048
D_FF = 5632
N_META = 16
G = 1024
HEAD_DIM = 64
N_HEADS = G // HEAD_DIM
W_LORA = 96
A_LORA = 96
G_LORA = 256
LORA_PAD = 128
RWKV_PROJ = 3 * G + W_LORA + A_LORA + G_LORA
RWKV_SPLITS = (G, G + W_LORA, 2 * G + W_LORA, 3 * G + W_LORA, 3 * G + W_LORA + A_LORA)
RW = 3 * G + 2 * LORA_PAD + G_LORA
PW = RW + 3 * G
OFF_WD, OFF_AD, OFF_GD = 3 * G, 3 * G + LORA_PAD, 3 * G + 2 * LORA_PAD
RMS_EPS = 1e-6
GN_EPS = 64e-5
EXP_M05 = 0.6065306597126334

C_PROMPT = 64
C_SAMPLE = 8
TM = 512
TF = 512
TN = 3328
TM_MIX = 256
ROWS_PREP = 256
ROWS_SCAN_SAMPLE = 128
SCAN_PAR = 4
VMEM_LIMIT = 56 * 1024 * 1024


def _dot(a, b):
    return jnp.dot(a, b, preferred_element_type=F32)


def _dot_nt(a, b):
    return lax.dot_general(a, b, (((1,), (1,)), ((), ())), preferred_element_type=F32)


def _dot_tn(a, b):
    return lax.dot_general(a, b, (((0,), (0,)), ((), ())), preferred_element_type=F32)


def _split(a):
    hi = a.astype(BF16)
    lo = (a - hi.astype(F32)).astype(BF16)
    return hi, lo


def _head_sum(x, bd):
    return _dot(x.astype(BF16), bd)


def _dot_split_rhs(a_bf16, b):
    hi, lo = _split(b)
    return _dot(a_bf16, hi) + _dot(a_bf16, lo)


def _rms(x, g):
    return x * lax.rsqrt(jnp.mean(x * x, axis=-1, keepdims=True) + RMS_EPS) * g


def _segments(arrays, tile):
    segs, off = [], 0
    for arr in arrays:
        n = arr.shape[0] // tile
        assert n * tile == arr.shape[0]
        segs.append((off, n))
        off += n
    return segs, off


def _seg_spec(tile, width, seg):
    off, n = seg
    return pl.BlockSpec((tile, width), lambda i, j: (jnp.clip(i - off, 0, n - 1), 0))


def _overlaps(in_segs, out_segs):
    for a, (ao, an) in enumerate(in_segs):
        for b, (bo, bn) in enumerate(out_segs):
            lo, hi = max(ao, bo), min(ao + an, bo + bn)
            if lo < hi:
                yield a, b, lo, hi


def _ffn_kernel(*refs, in_segs, out_segs, final_norm):
    n_in, n_out = len(in_segs), len(out_segs)
    x_refs = refs[:n_in]
    g_ref, wg_ref, wu_ref, wd_ref = refs[n_in:n_in + 4]
    rest = refs[n_in + 4:]
    if final_norm:
        gf_ref, rest = rest[0], rest[1:]
    o_refs = rest[:n_out]
    h_ref, acc_ref = rest[n_out:]
    i = pl.program_id(0)
    j = pl.program_id(1)

    for x_ref, (off, n) in zip(x_refs, in_segs):
        @pl.when((j == 0) & (i >= off) & (i < off + n))
        def _(x_ref=x_ref):
            h_ref[...] = _rms(x_ref[...], g_ref[...]).astype(BF16)
            acc_ref[...] = jnp.zeros_like(acc_ref)

    h = h_ref[...]
    gate = _dot(h, wg_ref[...])
    up = _dot(h, wu_ref[...])
    act = (gate * jax.nn.sigmoid(gate) * up).astype(BF16)
    acc_ref[...] += _dot(act, wd_ref[...])

    for a, b, lo, hi in _overlaps(in_segs, out_segs):
        @pl.when((j == pl.num_programs(1) - 1) & (i >= lo) & (i < hi))
        def _(x_ref=x_refs[a], o_ref=o_refs[b]):
            xo = x_ref[...] + 0.5 * acc_ref[...]
            if final_norm:
                xo = _rms(xo, gf_ref[...])
            o_ref[...] = xo


def _ffn(xs, g, wg, wu, wd, g_final, out_rows, *, name):
    final_norm = g_final is not None
    in_segs, n_tiles_in = _segments(xs, TM)
    out_shape = [jax.ShapeDtypeStruct((n, D_MODEL), F32) for n in out_rows]
    out_segs, n_tiles = _segments(out_shape, TM)
    assert n_tiles <= n_tiles_in
    cmap = lambda i, j: (0, 0)
    in_specs = [_seg_spec(TM, D_MODEL, s) for s in in_segs] + [
        pl.BlockSpec((1, D_MODEL), cmap),
        pl.BlockSpec((D_MODEL, TF), lambda i, j: (0, j)),
        pl.BlockSpec((D_MODEL, TF), lambda i, j: (0, j)),
        pl.BlockSpec((TF, D_MODEL), lambda i, j: (j, 0)),
    ]
    args = list(xs) + [g, wg, wu, wd]
    if final_norm:
        in_specs.append(pl.BlockSpec((1, D_MODEL), cmap))
        args.append(g_final)
    return pl.pallas_call(
        functools.partial(_ffn_kernel, in_segs=in_segs, out_segs=out_segs, final_norm=final_norm),
        grid=(n_tiles, D_FF // TF),
        in_specs=in_specs,
        out_specs=[_seg_spec(TM, D_MODEL, s) for s in out_segs],
        out_shape=out_shape,
        scratch_shapes=[pltpu.VMEM((TM, D_MODEL), BF16), pltpu.VMEM((TM, D_MODEL), F32)],
        compiler_params=pltpu.CompilerParams(
            dimension_semantics=("arbitrary", "arbitrary"), vmem_limit_bytes=VMEM_LIMIT),
        name=name,
    )(*args)


_PACK_MOVES = (
    (0, 0, G),
    (G, RWKV_SPLITS[1], G),
    (2 * G, RWKV_SPLITS[2], G),
    (OFF_WD, RWKV_SPLITS[0], W_LORA),
    (OFF_AD, RWKV_SPLITS[3], A_LORA),
    (OFF_GD, RWKV_SPLITS[4], G_LORA),
    (RW, RWKV_PROJ, 3 * G),
)


def _pack_w_in_kernel(w_ref, o_ref):
    for dst, src, n in _PACK_MOVES:
        o_ref[dst:dst + n, :] = w_ref[src:src + n, :].astype(BF16)
    for lo, hi in ((OFF_WD + W_LORA, OFF_AD), (OFF_AD + A_LORA, OFF_GD)):
        o_ref[lo:hi, :] = jnp.zeros((hi - lo, o_ref.shape[1]), BF16)


def _pack_w_in(w_t):
    cols = 512
    return pl.pallas_call(
        _pack_w_in_kernel,
        grid=(D_MODEL // cols,),
        in_specs=[pl.BlockSpec((w_t.shape[0], cols), lambda i: (0, i))],
        out_specs=pl.BlockSpec((PW, cols), lambda i: (0, i)),
        out_shape=jax.ShapeDtypeStruct((PW, D_MODEL), BF16),
        compiler_params=pltpu.CompilerParams(
            dimension_semantics=("parallel",), vmem_limit_bytes=VMEM_LIMIT),
        name="pack_w_in",
    )(w_t)


def _proj_kernel(x_ref, g_ref, w_ref, o_ref, h_ref):
    @pl.when(pl.program_id(1) == 0)
    def _():
        h_ref[...] = _rms(x_ref[...], g_ref[...]).astype(BF16)

    o_ref[...] = _dot_nt(h_ref[...], w_ref[...])


def _proj(x, g, w):
    t = x.shape[0]
    return pl.pallas_call(
        _proj_kernel,
        grid=(t // TM, PW // TN),
        in_specs=[
            pl.BlockSpec((TM, D_MODEL), lambda i, j: (i, 0)),
            pl.BlockSpec((1, D_MODEL), lambda i, j: (0, 0)),
            pl.BlockSpec((TN, D_MODEL), lambda i, j: (j, 0)),
        ],
        out_specs=pl.BlockSpec((TM, TN), lambda i, j: (i, j)),
        out_shape=jax.ShapeDtypeStruct((t, PW), F32),
        scratch_shapes=[pltpu.VMEM((TM, D_MODEL), BF16)],
        compiler_params=pltpu.CompilerParams(
            dimension_semantics=("parallel", "arbitrary"), vmem_limit_bytes=VMEM_LIMIT),
        name="proj",
    )(x, g, w)


def _prep_kernel(*refs, rows, chunk, sample):
    if sample:
        (p_ref, prev_ref, up1_ref, up2_ref, mu_ref, w0_ref, ww_ref, a0_ref, wa_ref, wgl_ref,
         kk_ref, ka_ref, rk_ref, cw_ref, bd_ref, tri_ref,
         ah_ref, rh_ref, bh_ref, kh_ref, v_ref, wc_ref, g_ref, bonus_ref, conv_ref, u_ref) = refs
    else:
        (p_ref, cp0_ref, cu0_ref, mu_ref, w0_ref, ww_ref, a0_ref, wa_ref, wgl_ref,
         kk_ref, ka_ref, rk_ref, cw_ref, bd_ref, tri_ref,
         ah_ref, rh_ref, bh_ref, kh_ref, v_ref, wc_ref, g_ref, bonus_ref, conv_ref, u_ref,
         ptail_ref, carry_p, carry_u) = refs

        @pl.when(pl.program_id(1) == 0)
        def _():
            carry_p[...] = cp0_ref[...]
            carry_u[...] = cu0_ref[...]

    def row_ids(width):
        r = lax.broadcasted_iota(jnp.int32, (rows, width), 0)
        return (r & 7) if sample else r

    def shifted_mix(lo, hi):
        p = p_ref[:, lo:hi]
        rolled = pltpu.roll(p, 1, 0)
        if sample:
            sh = jnp.where(row_ids(hi - lo) == 0, prev_ref[:, lo:hi], rolled)
        else:
            sh = jnp.where(row_ids(hi - lo) == 0, carry_p[7:8, lo:hi], rolled)
        return p + (sh - p) * mu_ref[:, lo:hi]

    r = shifted_mix(0, G)
    k = shifted_mix(G, 2 * G)
    v = shifted_mix(2 * G, 3 * G)
    wd = shifted_mix(OFF_WD, OFF_WD + LORA_PAD)
    ad = shifted_mix(OFF_AD, OFF_AD + LORA_PAD)
    gd = shifted_mix(OFF_GD, OFF_GD + G_LORA)

    z = w0_ref[...] + _dot(jnp.tanh(wd).astype(BF16), ww_ref[...])
    lw = -EXP_M05 * jax.nn.sigmoid(z)
    a = jax.nn.sigmoid(a0_ref[...] + _dot(ad.astype(BF16), wa_ref[...]))
    g_ref[...] = _dot(jax.nn.sigmoid(gd).astype(BF16), wgl_ref[...]).astype(BF16)

    bd = bd_ref[...]
    kk = k * kk_ref[...]
    norm = jnp.sqrt(_head_sum(kk * kk, bd))
    kk = kk / jnp.maximum(norm, 1e-12)
    km = k * (1.0 + (a - 1.0) * ka_ref[...])
    bonus_ref[...] = _head_sum(r * km * rk_ref[...], bd) * v

    cum = _dot_split_rhs(tri_ref[...], lw)
    e_cum = jnp.exp(cum)
    e_inv = jnp.exp(-cum)
    ah_ref[...] = (-kk * jnp.exp(cum - lw)).astype(BF16)
    rh_ref[...] = (r * e_cum).astype(BF16)
    bh_ref[...] = (kk * a * e_inv).astype(BF16)
    kh_ref[...] = (km * e_inv).astype(BF16)
    v_ref[...] = v.astype(BF16)
    for c in range(rows // chunk):
        wc_ref[c] = e_cum[(c + 1) * chunk - 1:(c + 1) * chunk, :]

    bg = p_ref[:, RW:RW + G]
    u = p_ref[:, RW + G:RW + 2 * G] * p_ref[:, RW + 2 * G:RW + 3 * G]
    u1 = pltpu.roll(u, 1, 0)
    u2 = pltpu.roll(u, 2, 0)
    rid = row_ids(G)
    if sample:
        um1 = jnp.where(rid == 0, up1_ref[...], u1)
        um2 = jnp.where(rid < 2, up2_ref[...], u2)
        u_ref[...] = u
    else:
        um1 = jnp.where(rid == 0, carry_u[7:8, :], u1)
        um2 = jnp.where(rid == 0, carry_u[6:7, :], jnp.where(rid == 1, carry_u[7:8, :], u2))
        u_ref[0] = u[rows - 8:, :]
        ptail_ref[0] = p_ref[rows - 8:rows, 0:RW]
    conv = cw_ref[0:1, :] * um2 + cw_ref[1:2, :] * um1 + cw_ref[2:3, :] * u
    conv_ref[...] = (bg * conv).astype(BF16)

    if not sample:
        carry_p[...] = p_ref[rows - 8:rows, 0:RW]
        carry_u[...] = u[rows - 8:, :]


def _prep(proj, weights, tri, *, n_seq, seq_len, rows, chunk, row_block_offset, extra, sample, name):
    t = n_seq * seq_len
    n_chunks = t // chunk
    cpt = rows // chunk
    if sample:
        grid = (t // rows,)
        rmap = lambda i: (i + row_block_offset, 0)
        omap = lambda i: (i, 0)
        omap3 = lambda i: (i, 0, 0)
        cmap = lambda i: (0, 0)
        sem = ("parallel",)
    else:
        tiles = seq_len // rows
        grid = (n_seq, tiles)
        rmap = lambda b, j: (b * tiles + j + row_block_offset, 0)
        omap = lambda b, j: (b * tiles + j, 0)
        omap3 = lambda b, j: (b * tiles + j, 0, 0)
        cmap = lambda b, j: (0, 0)
        sem = ("parallel", "arbitrary")

    const = lambda arr: pl.BlockSpec(arr.shape, cmap)
    in_specs = [pl.BlockSpec((rows, PW), rmap)]
    if sample:
        in_specs += [pl.BlockSpec((rows, RW), omap), pl.BlockSpec((rows, G), omap),
                     pl.BlockSpec((rows, G), omap)]
    else:
        in_specs += [const(e) for e in extra]
    in_specs += [const(w) for w in weights] + [const(tri)]
    args = [proj] + list(extra) + list(weights) + [tri]

    row_out = jax.ShapeDtypeStruct((t, G), F32)
    row_spec = pl.BlockSpec((rows, G), omap)
    row_bf16 = jax.ShapeDtypeStruct((t, G), BF16)
    out_shape = [row_bf16] * 5 + [jax.ShapeDtypeStruct((n_chunks, 1, G), F32), row_bf16, row_out,
                                  row_bf16]
    out_specs = [row_spec] * 5 + [pl.BlockSpec((cpt, 1, G), omap3), row_spec, row_spec, row_spec]
    if sample:
        out_shape.append(row_out)
        out_specs.append(row_spec)
        scratch = []
    else:
        out_shape += [jax.ShapeDtypeStruct((t // rows, 8, G), F32),
                      jax.ShapeDtypeStruct((t // rows, 8, RW), F32)]
        out_specs += [pl.BlockSpec((1, 8, G), omap3), pl.BlockSpec((1, 8, RW), omap3)]
        scratch = [pltpu.VMEM((8, RW), F32), pltpu.VMEM((8, G), F32)]
    return pl.pallas_call(
        functools.partial(_prep_kernel, rows=rows, chunk=chunk, sample=sample),
        grid=grid, in_specs=in_specs, out_specs=out_specs, out_shape=out_shape,
        scratch_shapes=scratch,
        compiler_params=pltpu.CompilerParams(dimension_semantics=sem, vmem_limit_bytes=VMEM_LIMIT),
        name=name,
    )(*args)


def _scan_kernel(ah_ref, rh_ref, bh_ref, kh_ref, v_ref, wc_ref, s0_ref, y_ref, s_ref,
                 *, n_par, rows, chunk, per_chunk_state, group):
    if per_chunk_state:
        s_in = s0_ref
    else:
        s_in = s_ref

        @pl.when(pl.program_id(1) == 0)
        def _():
            for q in range(n_par):
                s_ref[q] = s0_ref[0]

    n_blk = rows // chunk
    lanes = [slice(h * HEAD_DIM, (h + 1) * HEAD_DIM) for h in range(N_HEADS)]
    bf = lambda x: x.astype(BF16)

    ri = lax.broadcasted_iota(jnp.int32, (rows, 2 * rows), 0)
    ci = lax.broadcasted_iota(jnp.int32, (rows, 2 * rows), 1)
    right = ci >= rows
    cj = jnp.where(right, ci - rows, ci)
    shift = chunk.bit_length() - 1
    same = (ri >> shift) == (cj >> shift)
    mask_ak = same & (ri > cj) & right
    mask_r = same & (ri >= cj)
    rs = lax.broadcasted_iota(jnp.int32, (rows, rows), 0)
    cs = lax.broadcasted_iota(jnp.int32, (rows, rows), 1)
    mask_ab = ((rs >> shift) == (cs >> shift)) & (rs > cs)
    eye = (rs == cs).astype(F32)

    def chunk_rows(x, c):
        return x[c * chunk:(c + 1) * chunk, :]

    def unit_group(units):
        per_unit = lambda f: {u: f(*u) for u in units}
        a = per_unit(lambda q, h: ah_ref[q, :, lanes[h]])
        r = per_unit(lambda q, h: rh_ref[q, :, lanes[h]])
        b = per_unit(lambda q, h: bh_ref[q, :, lanes[h]])
        k = per_unit(lambda q, h: kh_ref[q, :, lanes[h]])
        v = per_unit(lambda q, h: v_ref[q, :, lanes[h]])
        ar = {u: jnp.concatenate([a[u], r[u]], axis=0) for u in units}
        bk = {u: jnp.concatenate([b[u], k[u]], axis=0) for u in units}
        if n_blk > 1:
            a, r, b, k, v32 = ({u: t[u].astype(F32) for u in units} for t in (a, r, b, k, v))

        gram = {u: _dot_nt(ar[u], bk[u]) for u in units}
        l_ab = {u: jnp.where(mask_ab, gram[u][:rows, :rows], 0.0) for u in units}
        l_ak = {u: bf(jnp.where(mask_ak, gram[u][:rows, :], 0.0)) for u in units}
        l_r = {u: bf(jnp.where(mask_r, gram[u][rows:, :], 0.0)) for u in units}

        xs, ys = {}, {}
        for u in units:
            q, h = u
            if n_blk == 1:
                st = _dot_nt(ar[u], bf(s_in[q, h]))
                xs[u], ys[u] = st[:rows], st[rows:]
            else:
                parts = [_dot_nt(jnp.concatenate([chunk_rows(a[u], c), chunk_rows(r[u], c)], axis=0),
                                 s_in[c, h]) for c in range(n_blk)]
                xs[u] = jnp.concatenate([p[:chunk] for p in parts], axis=0)
                ys[u] = jnp.concatenate([p[chunk:] for p in parts], axis=0)

        vv = {u: jnp.concatenate([v[u], v[u]], axis=0) for u in units}
        x = {u: xs[u] + _dot(l_ak[u], vv[u]) for u in units}

        t_inv = {u: eye + l_ab[u] for u in units}
        pw = {u: bf(l_ab[u]) for u in units}
        n = 1
        while 2 * n < chunk:
            pw = {u: bf(_dot(pw[u], pw[u])) for u in units}
            t_inv = {u: t_inv[u] + _dot(bf(t_inv[u]), pw[u]) for u in units}
            n *= 2

        uu = {u: _dot(bf(t_inv[u]), bf(x[u])) for u in units}
        uv = {u: jnp.concatenate([bf(uu[u]), v[u]], axis=0) for u in units}
        for u in units:
            q, h = u
            y_ref[q, :, lanes[h]] = ys[u] + _dot(l_r[u], uv[u])

        for u in units:
            q, h = u
            if n_blk == 1:
                upd = _dot_tn(uv[u], bk[u])
                s_ref[q, h] = (s_in[q, h] + upd) * wc_ref[q, 0][:, lanes[h]]
            else:
                for c in range(n_blk):
                    uv_c = jnp.concatenate([chunk_rows(uu[u], c), chunk_rows(v32[u], c)], axis=0)
                    bk_c = jnp.concatenate([chunk_rows(b[u], c), chunk_rows(k[u], c)], axis=0)
                    s_ref[c, h] = (s_in[c, h] + _dot_tn(uv_c, bk_c)) * wc_ref[q, c][:, lanes[h]]

    all_units = [(q, h) for h in range(N_HEADS) for q in range(n_par)]
    for g0 in range(0, len(all_units), group):
        unit_group(all_units[g0:g0 + group])


def _scan(ah, rh, bh, kh, v, wc, s0, *, n_seq, seq_len, n_par, rows, chunk, per_chunk_state, group,
          name):
    t = n_seq * seq_len
    n_blk = rows // chunk
    if per_chunk_state:
        assert seq_len == chunk and n_par == 1
        lead = 1
        grid = (t // rows,)
        rmap = lambda i: (0, i, 0)
        cmap = lambda i: (0, i, 0, 0)
        smap = lambda i: (i, 0, 0, 0)
        s0map = smap
        state_block = (n_blk, N_HEADS, HEAD_DIM, HEAD_DIM)
        s0_block = state_block
        wc_block = (1, n_blk, 1, G)
        sem = ("parallel",)
    else:
        assert rows == chunk and s0.shape[0] == 1 and n_seq % n_par == 0
        lead = n_seq
        grid = (n_seq // n_par, seq_len // chunk)
        rmap = lambda s, c: (s, c, 0)
        cmap = lambda s, c: (s, c, 0, 0)
        smap = lambda s, c: (s, 0, 0, 0)
        s0map = lambda s, c: (0, 0, 0, 0)
        state_block = (n_par, N_HEADS, HEAD_DIM, HEAD_DIM)
        s0_block = (1, N_HEADS, HEAD_DIM, HEAD_DIM)
        wc_block = (n_par, 1, 1, G)
        sem = ("parallel", "arbitrary")
    rows3 = lambda x: x.reshape(lead, t // lead, G)
    row_spec = pl.BlockSpec((n_par, rows, G), rmap)
    y, s_out = pl.pallas_call(
        functools.partial(_scan_kernel, n_par=n_par, rows=rows, chunk=chunk,
                          per_chunk_state=per_chunk_state, group=group),
        grid=grid,
        in_specs=[row_spec] * 5 + [pl.BlockSpec(wc_block, cmap), pl.BlockSpec(s0_block, s0map)],
        out_specs=[row_spec, pl.BlockSpec(state_block, smap)],
        out_shape=[jax.ShapeDtypeStruct((lead, t // lead, G), F32),
                   jax.ShapeDtypeStruct((n_seq, N_HEADS, HEAD_DIM, HEAD_DIM), F32)],
        compiler_params=pltpu.CompilerParams(dimension_semantics=sem, vmem_limit_bytes=VMEM_LIMIT),
        name=name,
    )(rows3(ah), rows3(rh), rows3(bh), rows3(kh), rows3(v), wc.reshape(lead, -1, 1, G), s0)
    return y.reshape(t, G), s_out


def _mix_kernel(*refs, segs):
    n = len(segs)
    row_refs = [refs[4 * s:4 * s + 4] for s in range(n)]
    x_ref, lnw_ref, lnb_ref, bd_ref, wo_ref, o_ref = refs[4 * n:]
    i = pl.program_id(0)

    def body(y_ref, bonus_ref, g_ref, conv_ref):
        bd = bd_ref[...]
        y = y_ref[...]
        mu = _head_sum(y, bd) * (1.0 / HEAD_DIM)
        d = y - mu
        var = _head_sum(d * d, bd) * (1.0 / HEAD_DIM)
        yn = d * lax.rsqrt(var + GN_EPS) * lnw_ref[...] + lnb_ref[...]
        rw = ((yn + bonus_ref[...]) * g_ref[...]).astype(BF16)
        mix = _dot(rw, wo_ref[0:G, :]) + _dot(conv_ref[...], wo_ref[G:2 * G, :])
        o_ref[...] = x_ref[...] + mix

    for rr, (off, cnt) in zip(row_refs, segs):
        @pl.when((i >= off) & (i < off + cnt))
        def _(rr=rr):
            body(*rr)


def _mix(row_groups, x1, lnw, lnb, bd, wo):
    segs, n_tiles = _segments([grp[0] for grp in row_groups], TM_MIX)
    cmap = lambda i: (0, 0)
    in_specs, args = [], []
    for grp, (off, cnt) in zip(row_groups, segs):
        smap = lambda i, off=off, cnt=cnt: (jnp.clip(i - off, 0, cnt - 1), 0)
        in_specs += [pl.BlockSpec((TM_MIX, G), smap)] * 4
        args += list(grp)
    in_specs += [pl.BlockSpec((TM_MIX, D_MODEL), lambda i: (i, 0)),
                 pl.BlockSpec((1, G), cmap), pl.BlockSpec((1, G), cmap),
                 pl.BlockSpec((G, G), cmap), pl.BlockSpec((D_MODEL, D_MODEL), cmap)]
    return pl.pallas_call(
        functools.partial(_mix_kernel, segs=segs),
        grid=(n_tiles,), in_specs=in_specs,
        out_specs=pl.BlockSpec((TM_MIX, D_MODEL), lambda i: (i, 0)),
        out_shape=jax.ShapeDtypeStruct((n_tiles * TM_MIX, D_MODEL), F32),
        compiler_params=pltpu.CompilerParams(
            dimension_semantics=("parallel",), vmem_limit_bytes=VMEM_LIMIT),
        name="mix",
    )(*args, x1, lnw, lnb, bd, wo)


def _pack_rwkv(a):
    r, wd, k, v, ad, gd = jnp.split(a, RWKV_SPLITS, axis=-1)
    zw = jnp.zeros(a.shape[:-1] + (LORA_PAD - W_LORA,), a.dtype)
    za = jnp.zeros(a.shape[:-1] + (LORA_PAD - A_LORA,), a.dtype)
    return jnp.concatenate([r, k, v, wd, zw, ad, za, gd], axis=-1)


def _unpack_rwkv(p):
    return jnp.concatenate([
        p[..., 0:G], p[..., OFF_WD:OFF_WD + W_LORA], p[..., G:2 * G], p[..., 2 * G:3 * G],
        p[..., OFF_AD:OFF_AD + A_LORA], p[..., OFF_GD:OFF_GD + G_LORA]], axis=-1)


def _block_tri(n, chunk):
    i = jnp.arange(n)
    return ((i[:, None] // chunk == i[None, :] // chunk) & (i[:, None] >= i[None, :])).astype(BF16)


def kernel(x_prompt, x_sample, state_wkv, state_shift, state_conv, meta_tokens, g_ffn1, ffn1_gate, ffn1_up, ffn1_down, g_mix, w_in, mu_shift, w0, w_lora_w, a0, w_lora_a, w_lora_g, k_k, k_a, r_k, ln_x_w, ln_x_b, conv_w, w_out, g_ffn2, ffn2_gate, ffn2_up, ffn2_down, g_final):
    assert g_ffn1.shape[0] == 1, "single layer"
    nb, seq, _ = x_prompt.shape
    db, dseq, _ = x_sample.shape
    assert dseq == C_SAMPLE and N_META <= C_PROMPT and seq % ROWS_PREP == 0
    tp, ts = nb * seq, db * dseq
    assert tp % TM == 0 and ts % TM == 0

    row = lambda a: a.reshape(1, -1).astype(F32)
    w_in_p = _pack_w_in(jnp.transpose(w_in[0]))
    pad_rows = lambda w, n: jnp.concatenate([w, jnp.zeros((n - w.shape[0], w.shape[1]), w.dtype)], axis=0)
    hid = jnp.arange(G) // HEAD_DIM
    bd = (hid[:, None] == hid[None, :]).astype(BF16)
    prep_w = (_pack_rwkv(mu_shift[0])[None], row(w0[0]), pad_rows(w_lora_w[0], LORA_PAD).astype(BF16),
              row(a0[0]), pad_rows(w_lora_a[0], LORA_PAD).astype(BF16), w_lora_g[0].astype(BF16),
              row(k_k[0]), row(k_a[0]), row(r_k[0]), conv_w[0].astype(F32), bd)

    tail_rows = ts + C_PROMPT
    tail_rows += (-tail_rows) % TM
    x_tail = jnp.concatenate([
        x_sample.reshape(ts, D_MODEL), jnp.zeros((C_PROMPT - N_META, D_MODEL), F32),
        meta_tokens.astype(F32), jnp.zeros((tail_rows - ts - C_PROMPT, D_MODEL), F32)], axis=0)
    meta_row0 = tp + ts

    x1 = _ffn([x_prompt.reshape(tp, D_MODEL), x_tail], row(g_ffn1[0]), ffn1_gate[0].astype(BF16),
              ffn1_up[0].astype(BF16), ffn1_down[0].astype(BF16), None, [tp + tail_rows], name="ffn1")[0]
    proj = _proj(x1, row(g_mix[0]), w_in_p)

    tri_p = _block_tri(ROWS_PREP, C_PROMPT)
    zeros_state = jnp.zeros((1, N_HEADS, HEAD_DIM, HEAD_DIM), F32)
    (ah, rh, bh, kh, vv, wc, _, _, _, utail_m, ptail_m) = _prep(
        proj, prep_w, tri_p[:C_PROMPT, :C_PROMPT], n_seq=1, seq_len=C_PROMPT, rows=C_PROMPT, chunk=C_PROMPT,
        row_block_offset=meta_row0 // C_PROMPT, extra=(jnp.zeros((8, RW), F32), jnp.zeros((8, G), F32)),
        sample=False, name="prep_meta")
    _, wkv_m = _scan(ah, rh, bh, kh, vv, wc, zeros_state, n_seq=1, seq_len=C_PROMPT, n_par=1,
                     rows=C_PROMPT, chunk=C_PROMPT, per_chunk_state=False, group=N_HEADS,
                     name="scan_meta")

    (ah, rh, bh, kh, vv, wc, g_p, bonus_p, conv_p, utail_p, ptail_p) = _prep(
        proj, prep_w, tri_p, n_seq=nb, seq_len=seq, rows=ROWS_PREP, chunk=C_PROMPT,
        row_block_offset=0, extra=(ptail_m[0], utail_m[0]), sample=False, name="prep_prompt")
    y_p, wkv_p = _scan(ah, rh, bh, kh, vv, wc, wkv_m, n_seq=nb, seq_len=seq, n_par=SCAN_PAR,
                       rows=C_PROMPT, chunk=C_PROMPT, per_chunk_state=False,
                       group=SCAN_PAR * N_HEADS, name="scan_prompt")

    first = lambda a: jnp.zeros((db, dseq) + a.shape[1:], F32).at[:, 0].set(a)
    prev = first(_pack_rwkv(state_shift[0])).reshape(ts, RW)
    up1 = first(state_conv[0][:, 1]).reshape(ts, G)
    up2 = first(state_conv[0][:, 0]).at[:, 1].set(state_conv[0][:, 1]).reshape(ts, G)
    (ah, rh, bh, kh, vv, wc, g_s, bonus_s, conv_s, u_s) = _prep(
        proj, prep_w, _block_tri(ROWS_PREP, C_SAMPLE), n_seq=db, seq_len=dseq, rows=ROWS_PREP,
        chunk=C_SAMPLE, row_block_offset=tp // ROWS_PREP, extra=(prev, up1, up2), sample=True,
        name="prep_sample")
    y_s, wkv_s = _scan(ah, rh, bh, kh, vv, wc, state_wkv[0], n_seq=db, seq_len=dseq, n_par=1,
                       rows=ROWS_SCAN_SAMPLE, chunk=C_SAMPLE, per_chunk_state=True, group=4,
                       name="scan_sample")

    x2 = _mix([(y_p, bonus_p, g_p, conv_p), (y_s, bonus_s, g_s, conv_s)], x1,
              row(ln_x_w[0]), row(ln_x_b[0]), bd, w_out[0].astype(BF16))
    y_prompt, y_sample = _ffn([x2], row(g_ffn2[0]), ffn2_gate[0].astype(BF16), ffn2_up[0].astype(BF16),
                              ffn2_down[0].astype(BF16), row(g_final), [tp, ts], name="ffn2")

    shift_p = _unpack_rwkv(ptail_p.reshape(nb, seq // ROWS_PREP, 8, RW)[:, -1, 7, :])
    conv_state_p = utail_p.reshape(nb, seq // ROWS_PREP, 8, G)[:, -1, 6:, :]
    shift_s = _unpack_rwkv(proj[tp + dseq - 1:tp + ts:dseq, :RW])
    conv_state_s = u_s.reshape(db, dseq, G)[:, -2:, :]
    return (y_prompt.reshape(nb, seq, D_MODEL), y_sample.reshape(db, dseq, D_MODEL),
            wkv_p[None].astype(state_wkv.dtype), shift_p[None].astype(state_shift.dtype),
            conv_state_p[None].astype(state_conv.dtype),
            wkv_s[None].astype(state_wkv.dtype), shift_s[None].astype(state_shift.dtype),
            conv_state_s[None].astype(state_conv.dtype))
```

```python
import functools

import jax
import jax.numpy as jnp
from jax import lax
from jax.experimental import pallas as pl
from jax.experimental.pallas import tpu as pltpu

F32 = jnp.float32
BF16 = jnp.bfloat16

D_MODEL = 2048
D_FF = 5632
N_META = 16
G = 1024
HEAD_DIM = 64
N_HEADS = G // HEAD_DIM
W_LORA = 96
A_LORA = 96
G_LORA = 256
LORA_PAD = 128
RWKV_PROJ = 3 * G + W_LORA + A_LORA + G_LORA
RWKV_SPLITS = (G, G + W_LORA, 2 * G + W_LORA, 3 * G + W_LORA, 3 * G + W_LORA + A_LORA)
RW = 3 * G + 2 * LORA_PAD + G_LORA
PW = RW + 3 * G
OFF_WD, OFF_AD, OFF_GD = 3 * G, 3 * G + LORA_PAD, 3 * G + 2 * LORA_PAD
RMS_EPS = 1e-6
GN_EPS = 64e-5
EXP_M05_LOG2E = 0.6065306597126334 * 1.4426950408889634

C_PROMPT = 64
C_SAMPLE = 8
TM = 512
TF = 512
TN = 3328
TM_MIX = 256
ROWS_PREP = 256
ROWS_SCAN_SAMPLE = 128
SCAN_PAR = 4
VMEM_LIMIT = 56 * 1024 * 1024


def _dot(a, b):
    return jnp.dot(a, b, preferred_element_type=F32)


def _dot_nt(a, b):
    return lax.dot_general(a, b, (((1,), (1,)), ((), ())), preferred_element_type=F32)


def _dot_tn(a, b):
    return lax.dot_general(a, b, (((0,), (0,)), ((), ())), preferred_element_type=F32)


def _split(a):
    hi = a.astype(BF16)
    lo = (a - hi.astype(F32)).astype(BF16)
    return hi, lo


def _head_sum(x, bd):
    return _dot(x.astype(BF16), bd)


def _dot_split_rhs(a_bf16, b):
    hi, lo = _split(b)
    return _dot(a_bf16, hi) + _dot(a_bf16, lo)


def _rms(x, g):
    return x * lax.rsqrt(jnp.mean(x * x, axis=-1, keepdims=True) + RMS_EPS) * g


def _segments(arrays, tile):
    segs, off = [], 0
    for arr in arrays:
        n = arr.shape[0] // tile
        assert n * tile == arr.shape[0]
        segs.append((off, n))
        off += n
    return segs, off


def _seg_spec(tile, width, seg):
    off, n = seg
    return pl.BlockSpec((tile, width), lambda i, j: (jnp.clip(i - off, 0, n - 1), 0))


def _overlaps(in_segs, out_segs):
    for a, (ao, an) in enumerate(in_segs):
        for b, (bo, bn) in enumerate(out_segs):
            lo, hi = max(ao, bo), min(ao + an, bo + bn)
            if lo < hi:
                yield a, b, lo, hi


def _ffn_kernel(*refs, in_segs, out_segs, final_norm):
    n_in, n_out = len(in_segs), len(out_segs)
    x_refs = refs[:n_in]
    g_ref, wg_ref, wu_ref, wd_ref = refs[n_in:n_in + 4]
    rest = refs[n_in + 4:]
    if final_norm:
        gf_ref, rest = rest[0], rest[1:]
    o_refs = rest[:n_out]
    h_ref, acc_ref = rest[n_out:]
    i = pl.program_id(0)
    j = pl.program_id(1)

    for x_ref, (off, n) in zip(x_refs, in_segs):
        @pl.when((j == 0) & (i >= off) & (i < off + n))
        def _(x_ref=x_ref):
            h_ref[...] = _rms(x_ref[...], g_ref[...]).astype(BF16)
            acc_ref[...] = jnp.zeros_like(acc_ref)

    h = h_ref[...]
    gate = _dot(h, wg_ref[...])
    up = _dot(h, wu_ref[...])
    act = (gate * jax.nn.sigmoid(gate) * up).astype(BF16)
    acc_ref[...] += _dot(act, wd_ref[...])

    for a, b, lo, hi in _overlaps(in_segs, out_segs):
        @pl.when((j == pl.num_programs(1) - 1) & (i >= lo) & (i < hi))
        def _(x_ref=x_refs[a], o_ref=o_refs[b]):
            xo = x_ref[...] + 0.5 * acc_ref[...]
            if final_norm:
                xo = _rms(xo, gf_ref[...])
            o_ref[...] = xo


def _ffn(xs, g, wg, wu, wd, g_final, out_rows, *, name):
    final_norm = g_final is not None
    in_segs, n_tiles_in = _segments(xs, TM)
    out_shape = [jax.ShapeDtypeStruct((n, D_MODEL), F32) for n in out_rows]
    out_segs, n_tiles = _segments(out_shape, TM)
    assert n_tiles <= n_tiles_in
    cmap = lambda i, j: (0, 0)
    in_specs = [_seg_spec(TM, D_MODEL, s) for s in in_segs] + [
        pl.BlockSpec((1, D_MODEL), cmap),
        pl.BlockSpec((D_MODEL, TF), lambda i, j: (0, j)),
        pl.BlockSpec((D_MODEL, TF), lambda i, j: (0, j)),
        pl.BlockSpec((TF, D_MODEL), lambda i, j: (j, 0)),
    ]
    args = list(xs) + [g, wg, wu, wd]
    if final_norm:
        in_specs.append(pl.BlockSpec((1, D_MODEL), cmap))
        args.append(g_final)
    return pl.pallas_call(
        functools.partial(_ffn_kernel, in_segs=in_segs, out_segs=out_segs, final_norm=final_norm),
        grid=(n_tiles, D_FF // TF),
        in_specs=in_specs,
        out_specs=[_seg_spec(TM, D_MODEL, s) for s in out_segs],
        out_shape=out_shape,
        scratch_shapes=[pltpu.VMEM((TM, D_MODEL), BF16), pltpu.VMEM((TM, D_MODEL), F32)],
        compiler_params=pltpu.CompilerParams(
            dimension_semantics=("arbitrary", "arbitrary"), vmem_limit_bytes=VMEM_LIMIT),
        name=name,
    )(*args)


_PACK_MOVES = (
    (0, 0, G),
    (G, RWKV_SPLITS[1], G),
    (2 * G, RWKV_SPLITS[2], G),
    (OFF_WD, RWKV_SPLITS[0], W_LORA),
    (OFF_AD, RWKV_SPLITS[3], A_LORA),
    (OFF_GD, RWKV_SPLITS[4], G_LORA),
    (RW, RWKV_PROJ, 3 * G),
)


def _pack_w_in_kernel(w_ref, o_ref):
    for dst, src, n in _PACK_MOVES:
        o_ref[dst:dst + n, :] = w_ref[src:src + n, :].astype(BF16)
    for lo, hi in ((OFF_WD + W_LORA, OFF_AD), (OFF_AD + A_LORA, OFF_GD)):
        o_ref[lo:hi, :] = jnp.zeros((hi - lo, o_ref.shape[1]), BF16)


def _pack_w_in(w_t):
    cols = 512
    return pl.pallas_call(
        _pack_w_in_kernel,
        grid=(D_MODEL // cols,),
        in_specs=[pl.BlockSpec((w_t.shape[0], cols), lambda i: (0, i))],
        out_specs=pl.BlockSpec((PW, cols), lambda i: (0, i)),
        out_shape=jax.ShapeDtypeStruct((PW, D_MODEL), BF16),
        compiler_params=pltpu.CompilerParams(
            dimension_semantics=("parallel",), vmem_limit_bytes=VMEM_LIMIT),
        name="pack_w_in",
    )(w_t)


def _proj_kernel(x_ref, g_ref, w_ref, o_ref):
    h = _rms(x_ref[...], g_ref[...]).astype(BF16)
    o_ref[...] = _dot_nt(h, w_ref[...])


def _proj(x, g, w):
    t = x.shape[0]
    return pl.pallas_call(
        _proj_kernel,
        grid=(PW // TN, t // TM),
        in_specs=[
            pl.BlockSpec((TM, D_MODEL), lambda j, i: (i, 0)),
            pl.BlockSpec((1, D_MODEL), lambda j, i: (0, 0)),
            pl.BlockSpec((TN, D_MODEL), lambda j, i: (j, 0)),
        ],
        out_specs=pl.BlockSpec((TM, TN), lambda j, i: (i, j)),
        out_shape=jax.ShapeDtypeStruct((t, PW), F32),
        compiler_params=pltpu.CompilerParams(
            dimension_semantics=("parallel", "parallel"), vmem_limit_bytes=VMEM_LIMIT),
        name="proj",
    )(x, g, w)


def _prep_kernel(*refs, rows, chunk, sample):
    if sample:
        (p_ref, prev_ref, up1_ref, up2_ref, mu_ref, w0_ref, ww_ref, a0_ref, wa_ref, wgl_ref,
         kk_ref, ka_ref, rk_ref, cw_ref, bd_ref, tri_ref,
         ah_ref, rh_ref, bh_ref, kh_ref, v_ref, wc_ref, g_ref, bonus_ref, conv_ref, u_ref) = refs
    else:
        (p_ref, cp0_ref, cu0_ref, mu_ref, w0_ref, ww_ref, a0_ref, wa_ref, wgl_ref,
         kk_ref, ka_ref, rk_ref, cw_ref, bd_ref, tri_ref,
         ah_ref, rh_ref, bh_ref, kh_ref, v_ref, wc_ref, g_ref, bonus_ref, conv_ref, u_ref,
         ptail_ref, carry_p, carry_u) = refs

        @pl.when(pl.program_id(1) == 0)
        def _():
            carry_p[...] = cp0_ref[...]
            carry_u[...] = cu0_ref[...]

    def row_ids(width):
        r = lax.broadcasted_iota(jnp.int32, (rows, width), 0)
        return (r & 7) if sample else r

    def shifted_mix(lo, hi):
        p = p_ref[:, lo:hi]
        rolled = pltpu.roll(p, 1, 0)
        if sample:
            sh = jnp.where(row_ids(hi - lo) == 0, prev_ref[:, lo:hi], rolled)
        else:
            sh = jnp.where(row_ids(hi - lo) == 0, carry_p[7:8, lo:hi], rolled)
        return p + (sh - p) * mu_ref[:, lo:hi]

    r = shifted_mix(0, G)
    k = shifted_mix(G, 2 * G)
    v = shifted_mix(2 * G, 3 * G)
    wd = shifted_mix(OFF_WD, OFF_WD + LORA_PAD)
    ad = shifted_mix(OFF_AD, OFF_AD + LORA_PAD)
    gd = shifted_mix(OFF_GD, OFF_GD + G_LORA)

    z = w0_ref[...] + _dot(jnp.tanh(wd).astype(BF16), ww_ref[...])
    lw = -EXP_M05_LOG2E * jax.nn.sigmoid(z)
    a = jax.nn.sigmoid(a0_ref[...] + _dot(ad.astype(BF16), wa_ref[...]))
    g_ref[...] = _dot(jax.nn.sigmoid(gd).astype(BF16), wgl_ref[...]).astype(BF16)

    bd = bd_ref[...]
    kk = k * kk_ref[...]
    norm = jnp.sqrt(_head_sum(kk * kk, bd))
    kk = kk / jnp.maximum(norm, 1e-12)
    km = k * (1.0 + (a - 1.0) * ka_ref[...])
    bonus_ref[...] = _head_sum(r * km * rk_ref[...], bd) * v

    cum = _dot_split_rhs(tri_ref[...], lw)
    e_cum = jnp.exp2(cum)
    e_inv = jnp.exp2(-cum)
    ah_ref[...] = (-kk * jnp.exp2(cum - lw)).astype(BF16)
    rh_ref[...] = (r * e_cum).astype(BF16)
    bh_ref[...] = (kk * a * e_inv).astype(BF16)
    kh_ref[...] = (km * e_inv).astype(BF16)
    v_ref[...] = v.astype(BF16)
    for c in range(rows // chunk):
        wc_ref[c] = e_cum[(c + 1) * chunk - 1:(c + 1) * chunk, :]

    bg = p_ref[:, RW:RW + G]
    u = p_ref[:, RW + G:RW + 2 * G] * p_ref[:, RW + 2 * G:RW + 3 * G]
    u1 = pltpu.roll(u, 1, 0)
    u2 = pltpu.roll(u, 2, 0)
    rid = row_ids(G)
    if sample:
        um1 = jnp.where(rid == 0, up1_ref[...], u1)
        um2 = jnp.where(rid < 2, up2_ref[...], u2)
        u_ref[...] = u
    else:
        um1 = jnp.where(rid == 0, carry_u[7:8, :], u1)
        um2 = jnp.where(rid == 0, carry_u[6:7, :], jnp.where(rid == 1, carry_u[7:8, :], u2))
        u_ref[0] = u[rows - 8:, :]
        ptail_ref[0] = p_ref[rows - 8:rows, 0:RW]
    conv = cw_ref[0:1, :] * um2 + cw_ref[1:2, :] * um1 + cw_ref[2:3, :] * u
    conv_ref[...] = (bg * conv).astype(BF16)

    if not sample:
        carry_p[...] = p_ref[rows - 8:rows, 0:RW]
        carry_u[...] = u[rows - 8:, :]


def _prep(proj, weights, tri, *, n_seq, seq_len, rows, chunk, row_block_offset, extra, sample, name):
    t = n_seq * seq_len
    n_chunks = t // chunk
    cpt = rows // chunk
    if sample:
        grid = (t // rows,)
        rmap = lambda i: (i + row_block_offset, 0)
        omap = lambda i: (i, 0)
        omap3 = lambda i: (i, 0, 0)
        cmap = lambda i: (0, 0)
        sem = ("parallel",)
    else:
        tiles = seq_len // rows
        grid = (n_seq, tiles)
        rmap = lambda b, j: (b * tiles + j + row_block_offset, 0)
        omap = lambda b, j: (b * tiles + j, 0)
        omap3 = lambda b, j: (b * tiles + j, 0, 0)
        cmap = lambda b, j: (0, 0)
        sem = ("parallel", "arbitrary")

    const = lambda arr: pl.BlockSpec(arr.shape, cmap)
    in_specs = [pl.BlockSpec((rows, PW), rmap)]
    if sample:
        in_specs += [pl.BlockSpec((rows, RW), omap), pl.BlockSpec((rows, G), omap),
                     pl.BlockSpec((rows, G), omap)]
    else:
        in_specs += [const(e) for e in extra]
    in_specs += [const(w) for w in weights] + [const(tri)]
    args = [proj] + list(extra) + list(weights) + [tri]

    row_out = jax.ShapeDtypeStruct((t, G), F32)
    row_spec = pl.BlockSpec((rows, G), omap)
    row_bf16 = jax.ShapeDtypeStruct((t, G), BF16)
    out_shape = [row_bf16] * 5 + [jax.ShapeDtypeStruct((n_chunks, 1, G), F32), row_bf16, row_out,
                                  row_bf16]
    out_specs = [row_spec] * 5 + [pl.BlockSpec((cpt, 1, G), omap3), row_spec, row_spec, row_spec]
    if sample:
        out_shape.append(row_out)
        out_specs.append(row_spec)
        scratch = []
    else:
        out_shape += [jax.ShapeDtypeStruct((t // rows, 8, G), F32),
                      jax.ShapeDtypeStruct((t // rows, 8, RW), F32)]
        out_specs += [pl.BlockSpec((1, 8, G), omap3), pl.BlockSpec((1, 8, RW), omap3)]
        scratch = [pltpu.VMEM((8, RW), F32), pltpu.VMEM((8, G), F32)]
    return pl.pallas_call(
        functools.partial(_prep_kernel, rows=rows, chunk=chunk, sample=sample),
        grid=grid, in_specs=in_specs, out_specs=out_specs, out_shape=out_shape,
        scratch_shapes=scratch,
        compiler_params=pltpu.CompilerParams(dimension_semantics=sem, vmem_limit_bytes=VMEM_LIMIT),
        name=name,
    )(*args)


def _scan_kernel(ah_ref, rh_ref, bh_ref, kh_ref, v_ref, wc_ref, s0_ref, y_ref, s_ref,
                 *, n_par, rows, chunk, per_chunk_state, group):
    if per_chunk_state:
        s_in = s0_ref
    else:
        s_in = s_ref

        @pl.when(pl.program_id(1) == 0)
        def _():
            for q in range(n_par):
                s_ref[q] = s0_ref[0]

    n_blk = rows // chunk
    lanes = [slice(h * HEAD_DIM, (h + 1) * HEAD_DIM) for h in range(N_HEADS)]
    bf = lambda x: x.astype(BF16)

    ri = lax.broadcasted_iota(jnp.int32, (rows, 2 * rows), 0)
    ci = lax.broadcasted_iota(jnp.int32, (rows, 2 * rows), 1)
    right = ci >= rows
    cj = jnp.where(right, ci - rows, ci)
    shift = chunk.bit_length() - 1
    same = (ri >> shift) == (cj >> shift)
    mask_ak = same & (ri > cj) & right
    mask_r = same & (ri >= cj)
    rs = lax.broadcasted_iota(jnp.int32, (rows, rows), 0)
    cs = lax.broadcasted_iota(jnp.int32, (rows, rows), 1)
    mask_ab = ((rs >> shift) == (cs >> shift)) & (rs > cs)
    eye = (rs == cs).astype(F32)

    def chunk_rows(x, c):
        return x[c * chunk:(c + 1) * chunk, :]

    def unit_group(units):
        per_unit = lambda f: {u: f(*u) for u in units}
        a = per_unit(lambda q, h: ah_ref[q, :, lanes[h]])
        r = per_unit(lambda q, h: rh_ref[q, :, lanes[h]])
        b = per_unit(lambda q, h: bh_ref[q, :, lanes[h]])
        k = per_unit(lambda q, h: kh_ref[q, :, lanes[h]])
        v = per_unit(lambda q, h: v_ref[q, :, lanes[h]])
        ar = {u: jnp.concatenate([a[u], r[u]], axis=0) for u in units}
        bk = {u: jnp.concatenate([b[u], k[u]], axis=0) for u in units}
        if n_blk > 1:
            a, r, b, k, v32 = ({u: t[u].astype(F32) for u in units} for t in (a, r, b, k, v))

        gram = {u: _dot_nt(ar[u], bk[u]) for u in units}
        l_ab = {u: jnp.where(mask_ab, gram[u][:rows, :rows], 0.0) for u in units}
        l_ak = {u: bf(jnp.where(mask_ak, gram[u][:rows, :], 0.0)) for u in units}
        l_r = {u: bf(jnp.where(mask_r, gram[u][rows:, :], 0.0)) for u in units}

        xs, ys = {}, {}
        for u in units:
            q, h = u
            if n_blk == 1:
                st = _dot_nt(ar[u], bf(s_in[q, h]))
                xs[u], ys[u] = st[:rows], st[rows:]
            else:
                parts = [_dot_nt(jnp.concatenate([chunk_rows(a[u], c), chunk_rows(r[u], c)], axis=0),
                                 s_in[c, h]) for c in range(n_blk)]
                xs[u] = jnp.concatenate([p[:chunk] for p in parts], axis=0)
                ys[u] = jnp.concatenate([p[chunk:] for p in parts], axis=0)

        vv = {u: jnp.concatenate([v[u], v[u]], axis=0) for u in units}
        x = {u: xs[u] + _dot(l_ak[u], vv[u]) for u in units}

        t_inv = {u: eye + l_ab[u] for u in units}
        pw = {u: bf(l_ab[u]) for u in units}
        n = 1
        while 2 * n < chunk:
            pw = {u: bf(_dot(pw[u], pw[u])) for u in units}
            t_inv = {u: t_inv[u] + _dot(bf(t_inv[u]), pw[u]) for u in units}
            n *= 2

        uu = {u: _dot(bf(t_inv[u]), bf(x[u])) for u in units}
        uv = {u: jnp.concatenate([bf(uu[u]), v[u]], axis=0) for u in units}
        for u in units:
            q, h = u
            y_ref[q, :, lanes[h]] = ys[u] + _dot(l_r[u], uv[u])

        for u in units:
            q, h = u
            if n_blk == 1:
                upd = _dot_tn(uv[u], bk[u])
                s_ref[q, h] = (s_in[q, h] + upd) * wc_ref[q, 0][:, lanes[h]]
            else:
                for c in range(n_blk):
                    uv_c = jnp.concatenate([chunk_rows(uu[u], c), chunk_rows(v32[u], c)], axis=0)
                    bk_c = jnp.concatenate([chunk_rows(b[u], c), chunk_rows(k[u], c)], axis=0)
                    s_ref[c, h] = (s_in[c, h] + _dot_tn(uv_c, bk_c)) * wc_ref[q, c][:, lanes[h]]

    all_units = [(q, h) for h in range(N_HEADS) for q in range(n_par)]
    for g0 in range(0, len(all_units), group):
        unit_group(all_units[g0:g0 + group])


def _scan(ah, rh, bh, kh, v, wc, s0, *, n_seq, seq_len, n_par, rows, chunk, per_chunk_state, group,
          name):
    t = n_seq * seq_len
    n_blk = rows // chunk
    if per_chunk_state:
        assert seq_len == chunk and n_par == 1
        lead = 1
        grid = (t // rows,)
        rmap = lambda i: (0, i, 0)
        cmap = lambda i: (0, i, 0, 0)
        smap = lambda i: (i, 0, 0, 0)
        s0map = smap
        state_block = (n_blk, N_HEADS, HEAD_DIM, HEAD_DIM)
        s0_block = state_block
        wc_block = (1, n_blk, 1, G)
        sem = ("parallel",)
    else:
        assert rows == chunk and s0.shape[0] == 1 and n_seq % n_par == 0
        lead = n_seq
        grid = (n_seq // n_par, seq_len // chunk)
        rmap = lambda s, c: (s, c, 0)
        cmap = lambda s, c: (s, c, 0, 0)
        smap = lambda s, c: (s, 0, 0, 0)
        s0map = lambda s, c: (0, 0, 0, 0)
        state_block = (n_par, N_HEADS, HEAD_DIM, HEAD_DIM)
        s0_block = (1, N_HEADS, HEAD_DIM, HEAD_DIM)
        wc_block = (n_par, 1, 1, G)
        sem = ("parallel", "arbitrary")
    rows3 = lambda x: x.reshape(lead, t // lead, G)
    row_spec = pl.BlockSpec((n_par, rows, G), rmap)
    y, s_out = pl.pallas_call(
        functools.partial(_scan_kernel, n_par=n_par, rows=rows, chunk=chunk,
                          per_chunk_state=per_chunk_state, group=group),
        grid=grid,
        in_specs=[row_spec] * 5 + [pl.BlockSpec(wc_block, cmap), pl.BlockSpec(s0_block, s0map)],
        out_specs=[row_spec, pl.BlockSpec(state_block, smap)],
        out_shape=[jax.ShapeDtypeStruct((lead, t // lead, G), F32),
                   jax.ShapeDtypeStruct((n_seq, N_HEADS, HEAD_DIM, HEAD_DIM), F32)],
        compiler_params=pltpu.CompilerParams(dimension_semantics=sem, vmem_limit_bytes=VMEM_LIMIT),
        name=name,
    )(rows3(ah), rows3(rh), rows3(bh), rows3(kh), rows3(v), wc.reshape(lead, -1, 1, G), s0)
    return y.reshape(t, G), s_out


def _mix_kernel(*refs, segs):
    n = len(segs)
    row_refs = [refs[4 * s:4 * s + 4] for s in range(n)]
    x_ref, lnw_ref, lnb_ref, bd_ref, wo_ref, o_ref = refs[4 * n:]
    i = pl.program_id(0)

    def body(y_ref, bonus_ref, g_ref, conv_ref):
        bd = bd_ref[...]
        y = y_ref[...]
        mu = _head_sum(y, bd) * (1.0 / HEAD_DIM)
        d = y - mu
        var = _head_sum(d * d, bd) * (1.0 / HEAD_DIM)
        yn = d * lax.rsqrt(var + GN_EPS) * lnw_ref[...] + lnb_ref[...]
        rw = ((yn + bonus_ref[...]) * g_ref[...]).astype(BF16)
        mix = _dot(rw, wo_ref[0:G, :]) + _dot(conv_ref[...], wo_ref[G:2 * G, :])
        o_ref[...] = x_ref[...] + mix

    for rr, (off, cnt) in zip(row_refs, segs):
        @pl.when((i >= off) & (i < off + cnt))
        def _(rr=rr):
            body(*rr)


def _mix(row_groups, x1, lnw, lnb, bd, wo):
    segs, n_tiles = _segments([grp[0] for grp in row_groups], TM_MIX)
    cmap = lambda i: (0, 0)
    in_specs, args = [], []
    for grp, (off, cnt) in zip(row_groups, segs):
        smap = lambda i, off=off, cnt=cnt: (jnp.clip(i - off, 0, cnt - 1), 0)
        in_specs += [pl.BlockSpec((TM_MIX, G), smap)] * 4
        args += list(grp)
    in_specs += [pl.BlockSpec((TM_MIX, D_MODEL), lambda i: (i, 0)),
                 pl.BlockSpec((1, G), cmap), pl.BlockSpec((1, G), cmap),
                 pl.BlockSpec((G, G), cmap), pl.BlockSpec((D_MODEL, D_MODEL), cmap)]
    return pl.pallas_call(
        functools.partial(_mix_kernel, segs=segs),
        grid=(n_tiles,), in_specs=in_specs,
        out_specs=pl.BlockSpec((TM_MIX, D_MODEL), lambda i: (i, 0)),
        out_shape=jax.ShapeDtypeStruct((n_tiles * TM_MIX, D_MODEL), F32),
        compiler_params=pltpu.CompilerParams(
            dimension_semantics=("parallel",), vmem_limit_bytes=VMEM_LIMIT),
        name="mix",
    )(*args, x1, lnw, lnb, bd, wo)


def _pack_rwkv(a):
    r, wd, k, v, ad, gd = jnp.split(a, RWKV_SPLITS, axis=-1)
    zw = jnp.zeros(a.shape[:-1] + (LORA_PAD - W_LORA,), a.dtype)
    za = jnp.zeros(a.shape[:-1] + (LORA_PAD - A_LORA,), a.dtype)
    return jnp.concatenate([r, k, v, wd, zw, ad, za, gd], axis=-1)


def _unpack_rwkv(p):
    return jnp.concatenate([
        p[..., 0:G], p[..., OFF_WD:OFF_WD + W_LORA], p[..., G:2 * G], p[..., 2 * G:3 * G],
        p[..., OFF_AD:OFF_AD + A_LORA], p[..., OFF_GD:OFF_GD + G_LORA]], axis=-1)


def _block_tri(n, chunk):
    i = jnp.arange(n)
    return ((i[:, None] // chunk == i[None, :] // chunk) & (i[:, None] >= i[None, :])).astype(BF16)


def kernel(x_prompt, x_sample, state_wkv, state_shift, state_conv, meta_tokens, g_ffn1, ffn1_gate, ffn1_up, ffn1_down, g_mix, w_in, mu_shift, w0, w_lora_w, a0, w_lora_a, w_lora_g, k_k, k_a, r_k, ln_x_w, ln_x_b, conv_w, w_out, g_ffn2, ffn2_gate, ffn2_up, ffn2_down, g_final):
    assert g_ffn1.shape[0] == 1, "single layer"
    nb, seq, _ = x_prompt.shape
    db, dseq, _ = x_sample.shape
    assert dseq == C_SAMPLE and N_META <= C_PROMPT and seq % ROWS_PREP == 0
    tp, ts = nb * seq, db * dseq
    assert tp % TM == 0 and ts % TM == 0

    row = lambda a: a.reshape(1, -1).astype(F32)
    w_in_p = _pack_w_in(jnp.transpose(w_in[0]))
    pad_rows = lambda w, n: jnp.concatenate([w, jnp.zeros((n - w.shape[0], w.shape[1]), w.dtype)], axis=0)
    hid = jnp.arange(G) // HEAD_DIM
    bd = (hid[:, None] == hid[None, :]).astype(BF16)
    prep_w = (_pack_rwkv(mu_shift[0])[None], row(w0[0]), pad_rows(w_lora_w[0], LORA_PAD).astype(BF16),
              row(a0[0]), pad_rows(w_lora_a[0], LORA_PAD).astype(BF16), w_lora_g[0].astype(BF16),
              row(k_k[0]), row(k_a[0]), row(r_k[0]), conv_w[0].astype(F32), bd)

    tail_rows = ts + C_PROMPT
    tail_rows += (-tail_rows) % TM
    x_tail = jnp.concatenate([
        x_sample.reshape(ts, D_MODEL), jnp.zeros((C_PROMPT - N_META, D_MODEL), F32),
        meta_tokens.astype(F32), jnp.zeros((tail_rows - ts - C_PROMPT, D_MODEL), F32)], axis=0)
    meta_row0 = tp + ts

    x1 = _ffn([x_prompt.reshape(tp, D_MODEL), x_tail], row(g_ffn1[0]), ffn1_gate[0].astype(BF16),
              ffn1_up[0].astype(BF16), ffn1_down[0].astype(BF16), None, [tp + tail_rows], name="ffn1")[0]
    proj = _proj(x1, row(g_mix[0]), w_in_p)

    tri_p = _block_tri(ROWS_PREP, C_PROMPT)
    zeros_state = jnp.zeros((1, N_HEADS, HEAD_DIM, HEAD_DIM), F32)
    (ah, rh, bh, kh, vv, wc, _, _, _, utail_m, ptail_m) = _prep(
        proj, prep_w, tri_p[:C_PROMPT, :C_PROMPT], n_seq=1, seq_len=C_PROMPT, rows=C_PROMPT, chunk=C_PROMPT,
        row_block_offset=meta_row0 // C_PROMPT, extra=(jnp.zeros((8, RW), F32), jnp.zeros((8, G), F32)),
        sample=False, name="prep_meta")
    _, wkv_m = _scan(ah, rh, bh, kh, vv, wc, zeros_state, n_seq=1, seq_len=C_PROMPT, n_par=1,
                     rows=C_PROMPT, chunk=C_PROMPT, per_chunk_state=False, group=N_HEADS,
                     name="scan_meta")

    (ah, rh, bh, kh, vv, wc, g_p, bonus_p, conv_p, utail_p, ptail_p) = _prep(
        proj, prep_w, tri_p, n_seq=nb, seq_len=seq, rows=ROWS_PREP, chunk=C_PROMPT,
        row_block_offset=0, extra=(ptail_m[0], utail_m[0]), sample=False, name="prep_prompt")
    y_p, wkv_p = _scan(ah, rh, bh, kh, vv, wc, wkv_m, n_seq=nb, seq_len=seq, n_par=SCAN_PAR,
                       rows=C_PROMPT, chunk=C_PROMPT, per_chunk_state=False,
                       group=SCAN_PAR * N_HEADS, name="scan_prompt")

    first = lambda a: jnp.zeros((db, dseq) + a.shape[1:], F32).at[:, 0].set(a)
    prev = first(_pack_rwkv(state_shift[0])).reshape(ts, RW)
    up1 = first(state_conv[0][:, 1]).reshape(ts, G)
    up2 = first(state_conv[0][:, 0]).at[:, 1].set(state_conv[0][:, 1]).reshape(ts, G)
    (ah, rh, bh, kh, vv, wc, g_s, bonus_s, conv_s, u_s) = _prep(
        proj, prep_w, _block_tri(ROWS_PREP, C_SAMPLE), n_seq=db, seq_len=dseq, rows=ROWS_PREP,
        chunk=C_SAMPLE, row_block_offset=tp // ROWS_PREP, extra=(prev, up1, up2), sample=True,
        name="prep_sample")
    y_s, wkv_s = _scan(ah, rh, bh, kh, vv, wc, state_wkv[0], n_seq=db, seq_len=dseq, n_par=1,
                       rows=ROWS_SCAN_SAMPLE, chunk=C_SAMPLE, per_chunk_state=True, group=4,
                       name="scan_sample")

    x2 = _mix([(y_p, bonus_p, g_p, conv_p), (y_s, bonus_s, g_s, conv_s)], x1,
              row(ln_x_w[0]), row(ln_x_b[0]), bd, w_out[0].astype(BF16))
    y_prompt, y_sample = _ffn([x2], row(g_ffn2[0]), ffn2_gate[0].astype(BF16), ffn2_up[0].astype(BF16),
                              ffn2_down[0].astype(BF16), row(g_final), [tp, ts], name="ffn2")

    shift_p = _unpack_rwkv(ptail_p.reshape(nb, seq // ROWS_PREP, 8, RW)[:, -1, 7, :])
    conv_state_p = utail_p.reshape(nb, seq // ROWS_PREP, 8, G)[:, -1, 6:, :]
    shift_s = _unpack_rwkv(proj[tp + dseq - 1:tp + ts:dseq, :RW])
    conv_state_s = u_s.reshape(db, dseq, G)[:, -2:, :]
    return (y_prompt.reshape(nb, seq, D_MODEL), y_sample.reshape(db, dseq, D_MODEL),
            wkv_p[None].astype(state_wkv.dtype), shift_p[None].astype(state_shift.dtype),
            conv_state_p[None].astype(state_conv.dtype),
            wkv_s[None].astype(state_wkv.dtype), shift_s[None].astype(state_shift.dtype),
            conv_state_s[None].astype(state_conv.dtype))
```

```python
import functools

import jax
import jax.numpy as jnp
from jax import lax
from jax.experimental import pallas as pl
from jax.experimental.pallas import tpu as pltpu

F32 = jnp.float32
BF16 = jnp.bfloat16

D_MODEL = 2048
D_FF = 5632
N_META = 16
G = 1024
HEAD_DIM = 64
N_HEADS = G // HEAD_DIM
W_LORA = 96
A_LORA = 96
G_LORA = 256
LORA_PAD = 128
RWKV_PROJ = 3 * G + W_LORA + A_LORA + G_LORA
RWKV_SPLITS = (G, G + W_LORA, 2 * G + W_LORA, 3 * G + W_LORA, 3 * G + W_LORA + A_LORA)
RW = 3 * G + 2 * LORA_PAD + G_LORA
PW = RW + 3 * G
OFF_WD, OFF_AD, OFF_GD = 3 * G, 3 * G + LORA_PAD, 3 * G + 2 * LORA_PAD
RMS_EPS = 1e-6
GN_EPS = 64e-5
EXP_M05_LOG2E = 0.6065306597126334 * 1.4426950408889634

C_PROMPT = 64
C_SAMPLE = 8
TM = 512
TF = 512
TN = 3328
TM_MIX = 256
ROWS_PREP = 256
ROWS_SCAN_SAMPLE = 128
SCAN_PAR = 4
VMEM_LIMIT = 56 * 1024 * 1024


def _dot(a, b):
    return jnp.dot(a, b, preferred_element_type=F32)


def _dot_nt(a, b):
    return lax.dot_general(a, b, (((1,), (1,)), ((), ())), preferred_element_type=F32)


def _dot_tn(a, b):
    return lax.dot_general(a, b, (((0,), (0,)), ((), ())), preferred_element_type=F32)


def _split(a):
    hi = a.astype(BF16)
    lo = (a - hi.astype(F32)).astype(BF16)
    return hi, lo


def _head_sum(x, bd):
    return _dot(x.astype(BF16), bd)


def _dot_split_rhs(a_bf16, b):
    hi, lo = _split(b)
    return _dot(a_bf16, hi) + _dot(a_bf16, lo)


def _rms(x, g):
    return x * lax.rsqrt(jnp.mean(x * x, axis=-1, keepdims=True) + RMS_EPS) * g


def _segments(arrays, tile):
    segs, off = [], 0
    for arr in arrays:
        n = arr.shape[0] // tile
        assert n * tile == arr.shape[0]
        segs.append((off, n))
        off += n
    return segs, off


def _seg_spec(tile, width, seg):
    off, n = seg
    return pl.BlockSpec((tile, width), lambda i, j: (jnp.clip(i - off, 0, n - 1), 0))


def _overlaps(in_segs, out_segs):
    for a, (ao, an) in enumerate(in_segs):
        for b, (bo, bn) in enumerate(out_segs):
            lo, hi = max(ao, bo), min(ao + an, bo + bn)
            if lo < hi:
                yield a, b, lo, hi


def _ffn_kernel(*refs, in_segs, out_segs, final_norm):
    n_in, n_out = len(in_segs), len(out_segs)
    x_refs = refs[:n_in]
    g_ref, wg_ref, wu_ref, wd_ref = refs[n_in:n_in + 4]
    rest = refs[n_in + 4:]
    if final_norm:
        gf_ref, rest = rest[0], rest[1:]
    o_refs = rest[:n_out]
    h_ref, acc_ref = rest[n_out:]
    i = pl.program_id(0)
    j = pl.program_id(1)

    for x_ref, (off, n) in zip(x_refs, in_segs):
        @pl.when((j == 0) & (i >= off) & (i < off + n))
        def _(x_ref=x_ref):
            h_ref[...] = _rms(x_ref[...], g_ref[...]).astype(BF16)
            acc_ref[...] = jnp.zeros_like(acc_ref)

    h = h_ref[...]
    gate = _dot(h, wg_ref[...])
    up = _dot(h, wu_ref[...])
    act = (gate * jax.nn.sigmoid(gate) * up).astype(BF16)
    acc_ref[...] += _dot(act, wd_ref[...])

    for a, b, lo, hi in _overlaps(in_segs, out_segs):
        @pl.when((j == pl.num_programs(1) - 1) & (i >= lo) & (i < hi))
        def _(x_ref=x_refs[a], o_ref=o_refs[b]):
            xo = x_ref[...] + 0.5 * acc_ref[...]
            if final_norm:
                xo = _rms(xo, gf_ref[...])
            o_ref[...] = xo


def _ffn(xs, g, wg, wu, wd, g_final, out_rows, *, name, tm=TM):
    final_norm = g_final is not None
    in_segs, n_tiles_in = _segments(xs, tm)
    out_shape = [jax.ShapeDtypeStruct((n, D_MODEL), F32) for n in out_rows]
    out_segs, n_tiles = _segments(out_shape, tm)
    assert n_tiles <= n_tiles_in
    cmap = lambda i, j: (0, 0)
    in_specs = [_seg_spec(tm, D_MODEL, s) for s in in_segs] + [
        pl.BlockSpec((1, D_MODEL), cmap),
        pl.BlockSpec((D_MODEL, TF), lambda i, j: (0, j)),
        pl.BlockSpec((D_MODEL, TF), lambda i, j: (0, j)),
        pl.BlockSpec((TF, D_MODEL), lambda i, j: (j, 0)),
    ]
    args = list(xs) + [g, wg, wu, wd]
    if final_norm:
        in_specs.append(pl.BlockSpec((1, D_MODEL), cmap))
        args.append(g_final)
    return pl.pallas_call(
        functools.partial(_ffn_kernel, in_segs=in_segs, out_segs=out_segs, final_norm=final_norm),
        grid=(n_tiles, D_FF // TF),
        in_specs=in_specs,
        out_specs=[_seg_spec(tm, D_MODEL, s) for s in out_segs],
        out_shape=out_shape,
        scratch_shapes=[pltpu.VMEM((tm, D_MODEL), BF16), pltpu.VMEM((tm, D_MODEL), F32)],
        compiler_params=pltpu.CompilerParams(
            dimension_semantics=("arbitrary", "arbitrary"), vmem_limit_bytes=VMEM_LIMIT),
        name=name,
    )(*args)


_PACK_MOVES = (
    (0, 0, G),
    (G, RWKV_SPLITS[1], G),
    (2 * G, RWKV_SPLITS[2], G),
    (OFF_WD, RWKV_SPLITS[0], W_LORA),
    (OFF_AD, RWKV_SPLITS[3], A_LORA),
    (OFF_GD, RWKV_SPLITS[4], G_LORA),
    (RW, RWKV_PROJ, 3 * G),
)


def _pack_w_in_kernel(w_ref, o_ref):
    for dst, src, n in _PACK_MOVES:
        o_ref[dst:dst + n, :] = w_ref[src:src + n, :].astype(BF16)
    for lo, hi in ((OFF_WD + W_LORA, OFF_AD), (OFF_AD + A_LORA, OFF_GD)):
        o_ref[lo:hi, :] = jnp.zeros((hi - lo, o_ref.shape[1]), BF16)


def _pack_w_in(w_t):
    cols = 512
    return pl.pallas_call(
        _pack_w_in_kernel,
        grid=(D_MODEL // cols,),
        in_specs=[pl.BlockSpec((w_t.shape[0], cols), lambda i: (0, i))],
        out_specs=pl.BlockSpec((PW, cols), lambda i: (0, i)),
        out_shape=jax.ShapeDtypeStruct((PW, D_MODEL), BF16),
        compiler_params=pltpu.CompilerParams(
            dimension_semantics=("parallel",), vmem_limit_bytes=VMEM_LIMIT),
        name="pack_w_in",
    )(w_t)


def _proj_kernel(x_ref, g_ref, w_ref, o_ref):
    h = _rms(x_ref[...], g_ref[...]).astype(BF16)
    o_ref[...] = _dot_nt(h, w_ref[...])


def _proj(x, g, w, *, name, tm=TM):
    t = x.shape[0]
    return pl.pallas_call(
        _proj_kernel,
        grid=(PW // TN, t // tm),
        in_specs=[
            pl.BlockSpec((tm, D_MODEL), lambda j, i: (i, 0)),
            pl.BlockSpec((1, D_MODEL), lambda j, i: (0, 0)),
            pl.BlockSpec((TN, D_MODEL), lambda j, i: (j, 0)),
        ],
        out_specs=pl.BlockSpec((tm, TN), lambda j, i: (i, j)),
        out_shape=jax.ShapeDtypeStruct((t, PW), F32),
        compiler_params=pltpu.CompilerParams(
            dimension_semantics=("parallel", "parallel"), vmem_limit_bytes=VMEM_LIMIT),
        name=name,
    )(x, g, w)


def _prep_kernel(*refs, rows, chunk, sample):
    if sample:
        (p_ref, prev_ref, up1_ref, up2_ref, mu_ref, w0_ref, ww_ref, a0_ref, wa_ref, wgl_ref,
         kk_ref, ka_ref, rk_ref, cw_ref, bd_ref, tri_ref,
         ah_ref, rh_ref, bh_ref, kh_ref, v_ref, wc_ref, g_ref, bonus_ref, conv_ref, u_ref) = refs
    else:
        (p_ref, cp0_ref, cu0_ref, mu_ref, w0_ref, ww_ref, a0_ref, wa_ref, wgl_ref,
         kk_ref, ka_ref, rk_ref, cw_ref, bd_ref, tri_ref,
         ah_ref, rh_ref, bh_ref, kh_ref, v_ref, wc_ref, g_ref, bonus_ref, conv_ref, u_ref,
         ptail_ref, carry_p, carry_u) = refs

        @pl.when(pl.program_id(1) == 0)
        def _():
            carry_p[...] = cp0_ref[...]
            carry_u[...] = cu0_ref[...]

    def row_ids(width):
        r = lax.broadcasted_iota(jnp.int32, (rows, width), 0)
        return (r & 7) if sample else r

    def shifted_mix(lo, hi):
        p = p_ref[:, lo:hi]
        rolled = pltpu.roll(p, 1, 0)
        if sample:
            sh = jnp.where(row_ids(hi - lo) == 0, prev_ref[:, lo:hi], rolled)
        else:
            sh = jnp.where(row_ids(hi - lo) == 0, carry_p[7:8, lo:hi], rolled)
        return p + (sh - p) * mu_ref[:, lo:hi]

    r = shifted_mix(0, G)
    k = shifted_mix(G, 2 * G)
    v = shifted_mix(2 * G, 3 * G)
    wd = shifted_mix(OFF_WD, OFF_WD + LORA_PAD)
    ad = shifted_mix(OFF_AD, OFF_AD + LORA_PAD)
    gd = shifted_mix(OFF_GD, OFF_GD + G_LORA)

    z = w0_ref[...] + _dot(jnp.tanh(wd).astype(BF16), ww_ref[...])
    lw = -EXP_M05_LOG2E * jax.nn.sigmoid(z)
    a = jax.nn.sigmoid(a0_ref[...] + _dot(ad.astype(BF16), wa_ref[...]))
    g_ref[...] = _dot(jax.nn.sigmoid(gd).astype(BF16), wgl_ref[...]).astype(BF16)

    bd = bd_ref[...]
    kk = k * kk_ref[...]
    norm = jnp.sqrt(_head_sum(kk * kk, bd))
    kk = kk / jnp.maximum(norm, 1e-12)
    km = k * (1.0 + (a - 1.0) * ka_ref[...])
    bonus_ref[...] = _head_sum(r * km * rk_ref[...], bd) * v

    cum = _dot_split_rhs(tri_ref[...], lw)
    e_cum = jnp.exp2(cum)
    e_inv = jnp.exp2(-cum)
    ah_ref[...] = (-kk * jnp.exp2(cum - lw)).astype(BF16)
    rh_ref[...] = (r * e_cum).astype(BF16)
    bh_ref[...] = (kk * a * e_inv).astype(BF16)
    kh_ref[...] = (km * e_inv).astype(BF16)
    v_ref[...] = v.astype(BF16)
    for c in range(rows // chunk):
        wc_ref[c] = e_cum[(c + 1) * chunk - 1:(c + 1) * chunk, :]

    bg = p_ref[:, RW:RW + G]
    u = p_ref[:, RW + G:RW + 2 * G] * p_ref[:, RW + 2 * G:RW + 3 * G]
    u1 = pltpu.roll(u, 1, 0)
    u2 = pltpu.roll(u, 2, 0)
    rid = row_ids(G)
    if sample:
        um1 = jnp.where(rid == 0, up1_ref[...], u1)
        um2 = jnp.where(rid < 2, up2_ref[...], u2)
        u_ref[...] = u
    else:
        um1 = jnp.where(rid == 0, carry_u[7:8, :], u1)
        um2 = jnp.where(rid == 0, carry_u[6:7, :], jnp.where(rid == 1, carry_u[7:8, :], u2))
        u_ref[0] = u[rows - 8:, :]
        ptail_ref[0] = p_ref[rows - 8:rows, 0:RW]
    conv = cw_ref[0:1, :] * um2 + cw_ref[1:2, :] * um1 + cw_ref[2:3, :] * u
    conv_ref[...] = (bg * conv).astype(BF16)

    if not sample:
        carry_p[...] = p_ref[rows - 8:rows, 0:RW]
        carry_u[...] = u[rows - 8:, :]


def _prep(proj, weights, tri, *, n_seq, seq_len, rows, chunk, row_block_offset, extra, sample, name):
    t = n_seq * seq_len
    n_chunks = t // chunk
    cpt = rows // chunk
    if sample:
        grid = (t // rows,)
        rmap = lambda i: (i + row_block_offset, 0)
        omap = lambda i: (i, 0)
        omap3 = lambda i: (i, 0, 0)
        cmap = lambda i: (0, 0)
        sem = ("parallel",)
    else:
        tiles = seq_len // rows
        grid = (n_seq, tiles)
        rmap = lambda b, j: (b * tiles + j + row_block_offset, 0)
        omap = lambda b, j: (b * tiles + j, 0)
        omap3 = lambda b, j: (b * tiles + j, 0, 0)
        cmap = lambda b, j: (0, 0)
        sem = ("parallel", "arbitrary")

    const = lambda arr: pl.BlockSpec(arr.shape, cmap)
    in_specs = [pl.BlockSpec((rows, PW), rmap)]
    if sample:
        in_specs += [pl.BlockSpec((rows, RW), omap), pl.BlockSpec((rows, G), omap),
                     pl.BlockSpec((rows, G), omap)]
    else:
        in_specs += [const(e) for e in extra]
    in_specs += [const(w) for w in weights] + [const(tri)]
    args = [proj] + list(extra) + list(weights) + [tri]

    row_out = jax.ShapeDtypeStruct((t, G), F32)
    row_spec = pl.BlockSpec((rows, G), omap)
    row_bf16 = jax.ShapeDtypeStruct((t, G), BF16)
    out_shape = [row_bf16] * 5 + [jax.ShapeDtypeStruct((n_chunks, 1, G), F32), row_bf16, row_out,
                                  row_bf16]
    out_specs = [row_spec] * 5 + [pl.BlockSpec((cpt, 1, G), omap3), row_spec, row_spec, row_spec]
    if sample:
        out_shape.append(row_out)
        out_specs.append(row_spec)
        scratch = []
    else:
        out_shape += [jax.ShapeDtypeStruct((t // rows, 8, G), F32),
                      jax.ShapeDtypeStruct((t // rows, 8, RW), F32)]
        out_specs += [pl.BlockSpec((1, 8, G), omap3), pl.BlockSpec((1, 8, RW), omap3)]
        scratch = [pltpu.VMEM((8, RW), F32), pltpu.VMEM((8, G), F32)]
    return pl.pallas_call(
        functools.partial(_prep_kernel, rows=rows, chunk=chunk, sample=sample),
        grid=grid, in_specs=in_specs, out_specs=out_specs, out_shape=out_shape,
        scratch_shapes=scratch,
        compiler_params=pltpu.CompilerParams(dimension_semantics=sem, vmem_limit_bytes=VMEM_LIMIT),
        name=name,
    )(*args)


def _scan_kernel(ah_ref, rh_ref, bh_ref, kh_ref, v_ref, wc_ref, s0_ref, y_ref, s_ref,
                 *, n_par, rows, chunk, per_chunk_state, group):
    if per_chunk_state:
        s_in = s0_ref
    else:
        s_in = s_ref

        @pl.when(pl.program_id(1) == 0)
        def _():
            for q in range(n_par):
                s_ref[q] = s0_ref[0]

    n_blk = rows // chunk
    lanes = [slice(h * HEAD_DIM, (h + 1) * HEAD_DIM) for h in range(N_HEADS)]
    bf = lambda x: x.astype(BF16)

    ri = lax.broadcasted_iota(jnp.int32, (rows, 2 * rows), 0)
    ci = lax.broadcasted_iota(jnp.int32, (rows, 2 * rows), 1)
    right = ci >= rows
    cj = jnp.where(right, ci - rows, ci)
    shift = chunk.bit_length() - 1
    same = (ri >> shift) == (cj >> shift)
    mask_ak = same & (ri > cj) & right
    mask_r = same & (ri >= cj)
    rs = lax.broadcasted_iota(jnp.int32, (rows, rows), 0)
    cs = lax.broadcasted_iota(jnp.int32, (rows, rows), 1)
    mask_ab = ((rs >> shift) == (cs >> shift)) & (rs > cs)
    eye = (rs == cs).astype(F32)

    def chunk_rows(x, c):
        return x[c * chunk:(c + 1) * chunk, :]

    pair_lane = lax.broadcasted_iota(jnp.int32, (rows, 2 * HEAD_DIM), 1) >> (HEAD_DIM.bit_length() - 1)

    def pair_masked(ref, q, h):
        hp, hh = divmod(h, 2)
        x2 = ref[q, :, 2 * HEAD_DIM * hp:2 * HEAD_DIM * (hp + 1)].astype(F32)
        return jnp.where(pair_lane == hh, x2, 0.0)

    def unit_group(units):
        per_unit = lambda f: {u: f(*u) for u in units}
        v = per_unit(lambda q, h: v_ref[q, :, lanes[h]])
        if n_blk == 1:
            a = per_unit(lambda q, h: ah_ref[q, :, lanes[h]])
            r = per_unit(lambda q, h: rh_ref[q, :, lanes[h]])
            b = per_unit(lambda q, h: bh_ref[q, :, lanes[h]])
            k = per_unit(lambda q, h: kh_ref[q, :, lanes[h]])
        else:
            a = per_unit(lambda q, h: pair_masked(ah_ref, q, h))
            r = per_unit(lambda q, h: pair_masked(rh_ref, q, h))
            b = per_unit(lambda q, h: pair_masked(bh_ref, q, h))
            k = per_unit(lambda q, h: pair_masked(kh_ref, q, h))
            v32 = {u: v[u].astype(F32) for u in units}
        ar = {u: bf(jnp.concatenate([a[u], r[u]], axis=0)) for u in units}
        bk = {u: bf(jnp.concatenate([b[u], k[u]], axis=0)) for u in units}

        gram = {u: _dot_nt(ar[u], bk[u]) for u in units}
        l_ab = {u: jnp.where(mask_ab, gram[u][:rows, :rows], 0.0) for u in units}
        l_ak = {u: bf(jnp.where(mask_ak, gram[u][:rows, :], 0.0)) for u in units}
        l_r = {u: bf(jnp.where(mask_r, gram[u][rows:, :], 0.0)) for u in units}

        xs, ys = {}, {}
        for u in units:
            q, h = u
            if n_blk == 1:
                st = _dot_nt(ar[u], bf(s_in[q, h]))
                xs[u], ys[u] = st[:rows], st[rows:]
            else:
                parts = [_dot_nt(jnp.concatenate([chunk_rows(a[u], c), chunk_rows(r[u], c)], axis=0),
                                 s_in[c, h // 2]) for c in range(n_blk)]
                xs[u] = jnp.concatenate([p[:chunk] for p in parts], axis=0)
                ys[u] = jnp.concatenate([p[chunk:] for p in parts], axis=0)

        vv = {u: jnp.concatenate([v[u], v[u]], axis=0) for u in units}
        x = {u: xs[u] + _dot(l_ak[u], vv[u]) for u in units}

        t_inv = {u: eye + l_ab[u] for u in units}
        pw = {u: bf(l_ab[u]) for u in units}
        n = 1
        while 2 * n < chunk:
            pw = {u: bf(_dot(pw[u], pw[u])) for u in units}
            t_inv = {u: t_inv[u] + _dot(bf(t_inv[u]), pw[u]) for u in units}
            n *= 2

        uu = {u: _dot(bf(t_inv[u]), bf(x[u])) for u in units}
        uv = {u: jnp.concatenate([bf(uu[u]), v[u]], axis=0) for u in units}
        for u in units:
            q, h = u
            y_ref[q, :, lanes[h]] = ys[u] + _dot(l_r[u], uv[u])

        if n_blk == 1:
            for u in units:
                q, h = u
                upd = _dot_tn(uv[u], bk[u])
                s_ref[q, h] = (s_in[q, h] + upd) * wc_ref[q, 0][:, lanes[h]]
        else:
            def update(u, c):
                uv_c = jnp.concatenate([chunk_rows(uu[u], c), chunk_rows(v32[u], c)], axis=0)
                bk_c = jnp.concatenate([chunk_rows(b[u], c), chunk_rows(k[u], c)], axis=0)
                return _dot_tn(uv_c, bk_c)

            for q, h in units:
                if h % 2 == 0:
                    assert (q, h + 1) in units
                    hp = h // 2
                    for c in range(n_blk):
                        s_new = s_in[c, hp] + update((q, h), c) + update((q, h + 1), c)
                        s_ref[c, hp] = s_new * wc_ref[q, c][:, 2 * HEAD_DIM * hp:2 * HEAD_DIM * (hp + 1)]

    all_units = [(q, h) for h in range(N_HEADS) for q in range(n_par)]
    for g0 in range(0, len(all_units), group):
        unit_group(all_units[g0:g0 + group])


def _scan(ah, rh, bh, kh, v, wc, s0, *, n_seq, seq_len, n_par, rows, chunk, per_chunk_state, group,
          name):
    t = n_seq * seq_len
    n_blk = rows // chunk
    if per_chunk_state:
        assert seq_len == chunk and n_par == 1
        lead = 1
        grid = (t // rows,)
        rmap = lambda i: (0, i, 0)
        cmap = lambda i: (0, i, 0, 0)
        smap = lambda i: (i, 0, 0, 0)
        s0map = smap
        state_block = (n_blk, N_HEADS // 2, HEAD_DIM, 2 * HEAD_DIM)
        s0_block = state_block
        wc_block = (1, n_blk, 1, G)
        sem = ("parallel",)
    else:
        assert rows == chunk and s0.shape[0] == 1 and n_seq % n_par == 0
        lead = n_seq
        grid = (n_seq // n_par, seq_len // chunk)
        rmap = lambda s, c: (s, c, 0)
        cmap = lambda s, c: (s, c, 0, 0)
        smap = lambda s, c: (s, 0, 0, 0)
        s0map = lambda s, c: (0, 0, 0, 0)
        state_block = (n_par, N_HEADS, HEAD_DIM, HEAD_DIM)
        s0_block = (1, N_HEADS, HEAD_DIM, HEAD_DIM)
        wc_block = (n_par, 1, 1, G)
        sem = ("parallel", "arbitrary")
    rows3 = lambda x: x.reshape(lead, t // lead, G)
    row_spec = pl.BlockSpec((n_par, rows, G), rmap)
    y, s_out = pl.pallas_call(
        functools.partial(_scan_kernel, n_par=n_par, rows=rows, chunk=chunk,
                          per_chunk_state=per_chunk_state, group=group),
        grid=grid,
        in_specs=[row_spec] * 5 + [pl.BlockSpec(wc_block, cmap), pl.BlockSpec(s0_block, s0map)],
        out_specs=[row_spec, pl.BlockSpec(state_block, smap)],
        out_shape=[jax.ShapeDtypeStruct((lead, t // lead, G), F32),
                   jax.ShapeDtypeStruct((n_seq,) + state_block[1:], F32)],
        compiler_params=pltpu.CompilerParams(dimension_semantics=sem, vmem_limit_bytes=VMEM_LIMIT),
        name=name,
    )(rows3(ah), rows3(rh), rows3(bh), rows3(kh), rows3(v), wc.reshape(lead, -1, 1, G), s0)
    return y.reshape(t, G), s_out


def _mix_kernel(*refs, segs):
    n = len(segs)
    row_refs = [refs[4 * s:4 * s + 4] for s in range(n)]
    x_ref, lnw_ref, lnb_ref, bd_ref, wo_ref, o_ref = refs[4 * n:]
    i = pl.program_id(0)

    def body(y_ref, bonus_ref, g_ref, conv_ref):
        bd = bd_ref[...]
        y = y_ref[...]
        mu = _head_sum(y, bd) * (1.0 / HEAD_DIM)
        d = y - mu
        var = _head_sum(d * d, bd) * (1.0 / HEAD_DIM)
        yn = d * lax.rsqrt(var + GN_EPS) * lnw_ref[...] + lnb_ref[...]
        rw = ((yn + bonus_ref[...]) * g_ref[...]).astype(BF16)
        mix = _dot(rw, wo_ref[0:G, :]) + _dot(conv_ref[...], wo_ref[G:2 * G, :])
        o_ref[...] = x_ref[...] + mix

    for rr, (off, cnt) in zip(row_refs, segs):
        @pl.when((i >= off) & (i < off + cnt))
        def _(rr=rr):
            body(*rr)


def _mix(row_groups, x1, lnw, lnb, bd, wo):
    segs, n_tiles = _segments([grp[0] for grp in row_groups], TM_MIX)
    cmap = lambda i: (0, 0)
    in_specs, args = [], []
    for grp, (off, cnt) in zip(row_groups, segs):
        smap = lambda i, off=off, cnt=cnt: (jnp.clip(i - off, 0, cnt - 1), 0)
        in_specs += [pl.BlockSpec((TM_MIX, G), smap)] * 4
        args += list(grp)
    in_specs += [pl.BlockSpec((TM_MIX, D_MODEL), lambda i: (i, 0)),
                 pl.BlockSpec((1, G), cmap), pl.BlockSpec((1, G), cmap),
                 pl.BlockSpec((G, G), cmap), pl.BlockSpec((D_MODEL, D_MODEL), cmap)]
    return pl.pallas_call(
        functools.partial(_mix_kernel, segs=segs),
        grid=(n_tiles,), in_specs=in_specs,
        out_specs=pl.BlockSpec((TM_MIX, D_MODEL), lambda i: (i, 0)),
        out_shape=jax.ShapeDtypeStruct((n_tiles * TM_MIX, D_MODEL), F32),
        compiler_params=pltpu.CompilerParams(
            dimension_semantics=("parallel",), vmem_limit_bytes=VMEM_LIMIT),
        name="mix",
    )(*args, x1, lnw, lnb, bd, wo)


def _pack_rwkv(a):
    r, wd, k, v, ad, gd = jnp.split(a, RWKV_SPLITS, axis=-1)
    zw = jnp.zeros(a.shape[:-1] + (LORA_PAD - W_LORA,), a.dtype)
    za = jnp.zeros(a.shape[:-1] + (LORA_PAD - A_LORA,), a.dtype)
    return jnp.concatenate([r, k, v, wd, zw, ad, za, gd], axis=-1)


def _unpack_rwkv(p):
    return jnp.concatenate([
        p[..., 0:G], p[..., OFF_WD:OFF_WD + W_LORA], p[..., G:2 * G], p[..., 2 * G:3 * G],
        p[..., OFF_AD:OFF_AD + A_LORA], p[..., OFF_GD:OFF_GD + G_LORA]], axis=-1)


def _block_tri(n, chunk):
    i = jnp.arange(n)
    return ((i[:, None] // chunk == i[None, :] // chunk) & (i[:, None] >= i[None, :])).astype(BF16)


def kernel(x_prompt, x_sample, state_wkv, state_shift, state_conv, meta_tokens, g_ffn1, ffn1_gate, ffn1_up, ffn1_down, g_mix, w_in, mu_shift, w0, w_lora_w, a0, w_lora_a, w_lora_g, k_k, k_a, r_k, ln_x_w, ln_x_b, conv_w, w_out, g_ffn2, ffn2_gate, ffn2_up, ffn2_down, g_final):
    assert g_ffn1.shape[0] == 1, "single layer"
    nb, seq, _ = x_prompt.shape
    db, dseq, _ = x_sample.shape
    assert dseq == C_SAMPLE and N_META <= C_PROMPT and seq % ROWS_PREP == 0
    tp, ts = nb * seq, db * dseq
    assert tp % TM == 0 and ts % TM == 0

    row = lambda a: a.reshape(1, -1).astype(F32)
    w_in_p = _pack_w_in(jnp.transpose(w_in[0]))
    pad_rows = lambda w, n: jnp.concatenate([w, jnp.zeros((n - w.shape[0], w.shape[1]), w.dtype)], axis=0)
    hid = jnp.arange(G) // HEAD_DIM
    bd = (hid[:, None] == hid[None, :]).astype(BF16)
    prep_w = (_pack_rwkv(mu_shift[0])[None], row(w0[0]), pad_rows(w_lora_w[0], LORA_PAD).astype(BF16),
              row(a0[0]), pad_rows(w_lora_a[0], LORA_PAD).astype(BF16), w_lora_g[0].astype(BF16),
              row(k_k[0]), row(k_a[0]), row(r_k[0]), conv_w[0].astype(F32), bd)

    ffn1_w = (row(g_ffn1[0]), ffn1_gate[0].astype(BF16), ffn1_up[0].astype(BF16), ffn1_down[0].astype(BF16))
    x1 = _ffn([x_prompt.reshape(tp, D_MODEL), x_sample.reshape(ts, D_MODEL)], *ffn1_w, None, [tp + ts],
              name="ffn1")[0]
    proj = _proj(x1, row(g_mix[0]), w_in_p, name="proj")
    x_meta = jnp.concatenate([jnp.zeros((C_PROMPT - N_META, D_MODEL), F32), meta_tokens.astype(F32)], axis=0)
    x1_meta = _ffn([x_meta], *ffn1_w, None, [C_PROMPT], name="ffn1_meta", tm=C_PROMPT)[0]
    proj_meta = _proj(x1_meta, row(g_mix[0]), w_in_p, name="proj_meta", tm=C_PROMPT)

    tri_p = _block_tri(ROWS_PREP, C_PROMPT)
    zeros_state = jnp.zeros((1, N_HEADS, HEAD_DIM, HEAD_DIM), F32)
    (ah, rh, bh, kh, vv, wc, _, _, _, utail_m, ptail_m) = _prep(
        proj_meta, prep_w, tri_p[:C_PROMPT, :C_PROMPT], n_seq=1, seq_len=C_PROMPT, rows=C_PROMPT,
        chunk=C_PROMPT, row_block_offset=0, extra=(jnp.zeros((8, RW), F32), jnp.zeros((8, G), F32)),
        sample=False, name="prep_meta")
    _, wkv_m = _scan(ah, rh, bh, kh, vv, wc, zeros_state, n_seq=1, seq_len=C_PROMPT, n_par=1,
                     rows=C_PROMPT, chunk=C_PROMPT, per_chunk_state=False, group=N_HEADS,
                     name="scan_meta")

    (ah, rh, bh, kh, vv, wc, g_p, bonus_p, conv_p, utail_p, ptail_p) = _prep(
        proj, prep_w, tri_p, n_seq=nb, seq_len=seq, rows=ROWS_PREP, chunk=C_PROMPT,
        row_block_offset=0, extra=(ptail_m[0], utail_m[0]), sample=False, name="prep_prompt")
    y_p, wkv_p = _scan(ah, rh, bh, kh, vv, wc, wkv_m, n_seq=nb, seq_len=seq, n_par=SCAN_PAR,
                       rows=C_PROMPT, chunk=C_PROMPT, per_chunk_state=False,
                       group=SCAN_PAR * N_HEADS, name="scan_prompt")

    first = lambda a: jnp.zeros((db, dseq) + a.shape[1:], F32).at[:, 0].set(a)
    prev = first(_pack_rwkv(state_shift[0])).reshape(ts, RW)
    up1 = first(state_conv[0][:, 1]).reshape(ts, G)
    up2 = first(state_conv[0][:, 0]).at[:, 1].set(state_conv[0][:, 1]).reshape(ts, G)
    (ah, rh, bh, kh, vv, wc, g_s, bonus_s, conv_s, u_s) = _prep(
        proj, prep_w, _block_tri(ROWS_PREP, C_SAMPLE), n_seq=db, seq_len=dseq, rows=ROWS_PREP,
        chunk=C_SAMPLE, row_block_offset=tp // ROWS_PREP, extra=(prev, up1, up2), sample=True,
        name="prep_sample")
    pair = (db, N_HEADS // 2, 2, HEAD_DIM, HEAD_DIM)
    s0_pairs = jnp.transpose(state_wkv[0].reshape(pair), (0, 1, 3, 2, 4)).reshape(
        db, N_HEADS // 2, HEAD_DIM, 2 * HEAD_DIM)
    y_s, wkv_s = _scan(ah, rh, bh, kh, vv, wc, s0_pairs, n_seq=db, seq_len=dseq, n_par=1,
                       rows=ROWS_SCAN_SAMPLE, chunk=C_SAMPLE, per_chunk_state=True, group=4,
                       name="scan_sample")

    x2 = _mix([(y_p, bonus_p, g_p, conv_p), (y_s, bonus_s, g_s, conv_s)], x1,
              row(ln_x_w[0]), row(ln_x_b[0]), bd, w_out[0].astype(BF16))
    y_prompt, y_sample = _ffn([x2], row(g_ffn2[0]), ffn2_gate[0].astype(BF16), ffn2_up[0].astype(BF16),
                              ffn2_down[0].astype(BF16), row(g_final), [tp, ts], name="ffn2")

    shift_p = _unpack_rwkv(ptail_p.reshape(nb, seq // ROWS_PREP, 8, RW)[:, -1, 7, :])
    conv_state_p = utail_p.reshape(nb, seq // ROWS_PREP, 8, G)[:, -1, 6:, :]
    shift_s = _unpack_rwkv(proj[tp + dseq - 1:tp + ts:dseq, :RW])
    conv_state_s = u_s.reshape(db, dseq, G)[:, -2:, :]
    wkv_s_out = jnp.transpose(wkv_s.reshape(db, N_HEADS // 2, HEAD_DIM, 2, HEAD_DIM),
                              (0, 1, 3, 2, 4)).reshape(db, N_HEADS, HEAD_DIM, HEAD_DIM)
    return (y_prompt.reshape(nb, seq, D_MODEL), y_sample.reshape(db, dseq, D_MODEL),
            wkv_p[None].astype(state_wkv.dtype), shift_p[None].astype(state_shift.dtype),
            conv_state_p[None].astype(state_conv.dtype),
            wkv_s_out[None].astype(state_wkv.dtype), shift_s[None].astype(state_shift.dtype),
            conv_state_s[None].astype(state_conv.dtype))
```

```python
import functools

import jax
import jax.numpy as jnp
from jax import lax
from jax.experimental import pallas as pl
from jax.experimental.pallas import tpu as pltpu

F32 = jnp.float32
BF16 = jnp.bfloat16

D_MODEL = 2048
D_FF = 5632
N_META = 16
G = 1024
HEAD_DIM = 64
N_HEADS = G // HEAD_DIM
W_LORA = 96
A_LORA = 96
G_LORA = 256
LORA_PAD = 128
RWKV_PROJ = 3 * G + W_LORA + A_LORA + G_LORA
RWKV_SPLITS = (G, G + W_LORA, 2 * G + W_LORA, 3 * G + W_LORA, 3 * G + W_LORA + A_LORA)
RW = 3 * G + 2 * LORA_PAD + G_LORA
PW = RW + 3 * G
OFF_WD, OFF_AD, OFF_GD = 3 * G, 3 * G + LORA_PAD, 3 * G + 2 * LORA_PAD
RMS_EPS = 1e-6
GN_EPS = 64e-5
EXP_M05_LOG2E = 0.6065306597126334 * 1.4426950408889634

C_PROMPT = 64
C_SAMPLE = 8
TM = 512
TF = 512
TN = 3328
TM_MIX = 256
ROWS_PREP = 256
ROWS_SCAN_SAMPLE = 128
SCAN_PAR = 4
VMEM_LIMIT = 56 * 1024 * 1024


def _dot(a, b):
    return jnp.dot(a, b, preferred_element_type=F32)


def _dot_nt(a, b):
    return lax.dot_general(a, b, (((1,), (1,)), ((), ())), preferred_element_type=F32)


def _dot_tn(a, b):
    return lax.dot_general(a, b, (((0,), (0,)), ((), ())), preferred_element_type=F32)


def _split(a):
    hi = a.astype(BF16)
    lo = (a - hi.astype(F32)).astype(BF16)
    return hi, lo


def _head_sum(x, bd):
    return _dot(x.astype(BF16), bd)


def _dot_split_rhs(a_bf16, b):
    hi, lo = _split(b)
    return _dot(a_bf16, hi) + _dot(a_bf16, lo)


def _rms(x, g):
    return x * lax.rsqrt(jnp.mean(x * x, axis=-1, keepdims=True) + RMS_EPS) * g


def _segments(arrays, tile):
    segs, off = [], 0
    for arr in arrays:
        n = arr.shape[0] // tile
        assert n * tile == arr.shape[0]
        segs.append((off, n))
        off += n
    return segs, off


def _seg_spec(tile, width, seg):
    off, n = seg
    return pl.BlockSpec((tile, width), lambda i, j: (jnp.clip(i - off, 0, n - 1), 0))


def _overlaps(in_segs, out_segs):
    for a, (ao, an) in enumerate(in_segs):
        for b, (bo, bn) in enumerate(out_segs):
            lo, hi = max(ao, bo), min(ao + an, bo + bn)
            if lo < hi:
                yield a, b, lo, hi


def _ffn_kernel(*refs, in_segs, out_segs, final_norm):
    n_in, n_out = len(in_segs), len(out_segs)
    x_refs = refs[:n_in]
    g_ref, wg_ref, wu_ref, wd_ref = refs[n_in:n_in + 4]
    rest = refs[n_in + 4:]
    if final_norm:
        gf_ref, rest = rest[0], rest[1:]
    o_refs = rest[:n_out]
    h_ref, acc_ref = rest[n_out:]
    i = pl.program_id(0)
    j = pl.program_id(1)

    for x_ref, (off, n) in zip(x_refs, in_segs):
        @pl.when((j == 0) & (i >= off) & (i < off + n))
        def _(x_ref=x_ref):
            h_ref[...] = _rms(x_ref[...], g_ref[...]).astype(BF16)
            acc_ref[...] = jnp.zeros_like(acc_ref)

    h = h_ref[...]
    gate = _dot(h, wg_ref[...])
    up = _dot(h, wu_ref[...])
    act = (gate * jax.nn.sigmoid(gate) * up).astype(BF16)
    acc_ref[...] += _dot(act, wd_ref[...])

    for a, b, lo, hi in _overlaps(in_segs, out_segs):
        @pl.when((j == pl.num_programs(1) - 1) & (i >= lo) & (i < hi))
        def _(x_ref=x_refs[a], o_ref=o_refs[b]):
            xo = x_ref[...] + 0.5 * acc_ref[...]
            if final_norm:
                xo = _rms(xo, gf_ref[...])
            o_ref[...] = xo


def _ffn(xs, g, wg, wu, wd, g_final, out_rows, *, name, tm=TM, tf=TF):
    final_norm = g_final is not None
    in_segs, n_tiles_in = _segments(xs, tm)
    out_shape = [jax.ShapeDtypeStruct((n, D_MODEL), F32) for n in out_rows]
    out_segs, n_tiles = _segments(out_shape, tm)
    assert n_tiles <= n_tiles_in
    cmap = lambda i, j: (0, 0)
    in_specs = [_seg_spec(tm, D_MODEL, s) for s in in_segs] + [
        pl.BlockSpec((1, D_MODEL), cmap),
        pl.BlockSpec((D_MODEL, tf), lambda i, j: (0, j)),
        pl.BlockSpec((D_MODEL, tf), lambda i, j: (0, j)),
        pl.BlockSpec((tf, D_MODEL), lambda i, j: (j, 0)),
    ]
    args = list(xs) + [g, wg, wu, wd]
    if final_norm:
        in_specs.append(pl.BlockSpec((1, D_MODEL), cmap))
        args.append(g_final)
    return pl.pallas_call(
        functools.partial(_ffn_kernel, in_segs=in_segs, out_segs=out_segs, final_norm=final_norm),
        grid=(n_tiles, D_FF // tf),
        in_specs=in_specs,
        out_specs=[_seg_spec(tm, D_MODEL, s) for s in out_segs],
        out_shape=out_shape,
        scratch_shapes=[pltpu.VMEM((tm, D_MODEL), BF16), pltpu.VMEM((tm, D_MODEL), F32)],
        compiler_params=pltpu.CompilerParams(
            dimension_semantics=("arbitrary", "arbitrary"), vmem_limit_bytes=VMEM_LIMIT),
        name=name,
    )(*args)


_PACK_MOVES = (
    (0, 0, G),
    (G, RWKV_SPLITS[1], G),
    (2 * G, RWKV_SPLITS[2], G),
    (OFF_WD, RWKV_SPLITS[0], W_LORA),
    (OFF_AD, RWKV_SPLITS[3], A_LORA),
    (OFF_GD, RWKV_SPLITS[4], G_LORA),
    (RW, RWKV_PROJ, 3 * G),
)


def _pack_w_in_kernel(w_ref, o_ref):
    for dst, src, n in _PACK_MOVES:
        o_ref[dst:dst + n, :] = w_ref[src:src + n, :].astype(BF16)
    for lo, hi in ((OFF_WD + W_LORA, OFF_AD), (OFF_AD + A_LORA, OFF_GD)):
        o_ref[lo:hi, :] = jnp.zeros((hi - lo, o_ref.shape[1]), BF16)


def _pack_w_in(w_t):
    cols = 512
    return pl.pallas_call(
        _pack_w_in_kernel,
        grid=(D_MODEL // cols,),
        in_specs=[pl.BlockSpec((w_t.shape[0], cols), lambda i: (0, i))],
        out_specs=pl.BlockSpec((PW, cols), lambda i: (0, i)),
        out_shape=jax.ShapeDtypeStruct((PW, D_MODEL), BF16),
        compiler_params=pltpu.CompilerParams(
            dimension_semantics=("parallel",), vmem_limit_bytes=VMEM_LIMIT),
        name="pack_w_in",
    )(w_t)


def _proj_kernel(x_ref, g_ref, w_ref, o_ref):
    h = _rms(x_ref[...], g_ref[...]).astype(BF16)
    o_ref[...] = _dot_nt(h, w_ref[...])


def _proj(x, g, w, *, name, tm=TM):
    t = x.shape[0]
    return pl.pallas_call(
        _proj_kernel,
        grid=(PW // TN, t // tm),
        in_specs=[
            pl.BlockSpec((tm, D_MODEL), lambda j, i: (i, 0)),
            pl.BlockSpec((1, D_MODEL), lambda j, i: (0, 0)),
            pl.BlockSpec((TN, D_MODEL), lambda j, i: (j, 0)),
        ],
        out_specs=pl.BlockSpec((tm, TN), lambda j, i: (i, j)),
        out_shape=jax.ShapeDtypeStruct((t, PW), F32),
        compiler_params=pltpu.CompilerParams(
            dimension_semantics=("parallel", "parallel"), vmem_limit_bytes=VMEM_LIMIT),
        name=name,
    )(x, g, w)


def _prep_kernel(*refs, rows, chunk, sample):
    if sample:
        (p_ref, prev_ref, up1_ref, up2_ref, mu_ref, w0_ref, ww_ref, a0_ref, wa_ref, wgl_ref,
         kk_ref, ka_ref, rk_ref, cw_ref, bd_ref, tri_ref,
         ah_ref, rh_ref, bh_ref, kh_ref, v_ref, wc_ref, g_ref, bonus_ref, conv_ref, u_ref,
         plast_ref) = refs
    else:
        (p_ref, cp0_ref, cu0_ref, mu_ref, w0_ref, ww_ref, a0_ref, wa_ref, wgl_ref,
         kk_ref, ka_ref, rk_ref, cw_ref, bd_ref, tri_ref,
         ah_ref, rh_ref, bh_ref, kh_ref, v_ref, wc_ref, g_ref, bonus_ref, conv_ref, u_ref,
         ptail_ref, carry_p, carry_u) = refs

        @pl.when(pl.program_id(1) == 0)
        def _():
            carry_p[...] = cp0_ref[...]
            carry_u[...] = cu0_ref[...]

    def row_ids(width):
        r = lax.broadcasted_iota(jnp.int32, (rows, width), 0)
        return (r & 7) if sample else r

    def per_sequence(x):
        n, w = x.shape
        return jnp.broadcast_to(x[:, None, :], (n, 8, w)).reshape(n * 8, w)

    def shifted_mix(lo, hi):
        p = p_ref[:, lo:hi]
        rolled = pltpu.roll(p, 1, 0)
        if sample:
            sh = jnp.where(row_ids(hi - lo) == 0, per_sequence(prev_ref[:, lo:hi]), rolled)
        else:
            sh = jnp.where(row_ids(hi - lo) == 0, carry_p[7:8, lo:hi], rolled)
        return p + (sh - p) * mu_ref[:, lo:hi]

    r = shifted_mix(0, G)
    k = shifted_mix(G, 2 * G)
    v = shifted_mix(2 * G, 3 * G)
    wd = shifted_mix(OFF_WD, OFF_WD + LORA_PAD)
    ad = shifted_mix(OFF_AD, OFF_AD + LORA_PAD)
    gd = shifted_mix(OFF_GD, OFF_GD + G_LORA)

    z = w0_ref[...] + _dot(jnp.tanh(wd).astype(BF16), ww_ref[...])
    lw = -EXP_M05_LOG2E * jax.nn.sigmoid(z)
    a = jax.nn.sigmoid(a0_ref[...] + _dot(ad.astype(BF16), wa_ref[...]))
    g_ref[...] = _dot(jax.nn.sigmoid(gd).astype(BF16), wgl_ref[...]).astype(BF16)

    bd = bd_ref[...]
    kk = k * kk_ref[...]
    norm = jnp.sqrt(_head_sum(kk * kk, bd))
    kk = kk / jnp.maximum(norm, 1e-12)
    km = k * (1.0 + (a - 1.0) * ka_ref[...])
    bonus_ref[...] = _head_sum(r * km * rk_ref[...], bd) * v

    cum = _dot_split_rhs(tri_ref[...], lw)
    e_cum = jnp.exp2(cum)
    e_inv = jnp.exp2(-cum)
    ah_ref[...] = (-kk * jnp.exp2(cum - lw)).astype(BF16)
    rh_ref[...] = (r * e_cum).astype(BF16)
    bh_ref[...] = (kk * a * e_inv).astype(BF16)
    kh_ref[...] = (km * e_inv).astype(BF16)
    v_ref[...] = v.astype(BF16)
    for c in range(rows // chunk):
        wc_ref[c] = e_cum[(c + 1) * chunk - 1:(c + 1) * chunk, :]

    bg = p_ref[:, RW:RW + G]
    u = p_ref[:, RW + G:RW + 2 * G] * p_ref[:, RW + 2 * G:RW + 3 * G]
    u1 = pltpu.roll(u, 1, 0)
    u2 = pltpu.roll(u, 2, 0)
    rid = row_ids(G)
    if sample:
        s1 = per_sequence(up1_ref[...])
        um1 = jnp.where(rid == 0, s1, u1)
        um2 = jnp.where(rid == 0, per_sequence(up2_ref[...]), jnp.where(rid == 1, s1, u2))
        u_ref[...] = u
        plast_ref[...] = p_ref[:, 0:RW].reshape(rows // 8, 8, RW)[:, 7, :]
    else:
        um1 = jnp.where(rid == 0, carry_u[7:8, :], u1)
        um2 = jnp.where(rid == 0, carry_u[6:7, :], jnp.where(rid == 1, carry_u[7:8, :], u2))
        u_ref[0] = u[rows - 8:, :]
        ptail_ref[0] = p_ref[rows - 8:rows, 0:RW]
    conv = cw_ref[0:1, :] * um2 + cw_ref[1:2, :] * um1 + cw_ref[2:3, :] * u
    conv_ref[...] = (bg * conv).astype(BF16)

    if not sample:
        carry_p[...] = p_ref[rows - 8:rows, 0:RW]
        carry_u[...] = u[rows - 8:, :]


def _prep(proj, weights, tri, *, n_seq, seq_len, rows, chunk, row_block_offset, extra, sample, name):
    t = n_seq * seq_len
    n_chunks = t // chunk
    cpt = rows // chunk
    if sample:
        grid = (t // rows,)
        rmap = lambda i: (i + row_block_offset, 0)
        omap = lambda i: (i, 0)
        omap3 = lambda i: (i, 0, 0)
        cmap = lambda i: (0, 0)
        sem = ("parallel",)
    else:
        tiles = seq_len // rows
        grid = (n_seq, tiles)
        rmap = lambda b, j: (b * tiles + j + row_block_offset, 0)
        omap = lambda b, j: (b * tiles + j, 0)
        omap3 = lambda b, j: (b * tiles + j, 0, 0)
        cmap = lambda b, j: (0, 0)
        sem = ("parallel", "arbitrary")

    const = lambda arr: pl.BlockSpec(arr.shape, cmap)
    in_specs = [pl.BlockSpec((rows, PW), rmap)]
    if sample:
        in_specs += [pl.BlockSpec((rows // 8, RW), omap), pl.BlockSpec((rows // 8, G), omap),
                     pl.BlockSpec((rows // 8, G), omap)]
    else:
        in_specs += [const(e) for e in extra]
    in_specs += [const(w) for w in weights] + [const(tri)]
    args = [proj] + list(extra) + list(weights) + [tri]

    row_out = jax.ShapeDtypeStruct((t, G), F32)
    row_spec = pl.BlockSpec((rows, G), omap)
    row_bf16 = jax.ShapeDtypeStruct((t, G), BF16)
    out_shape = [row_bf16] * 5 + [jax.ShapeDtypeStruct((n_chunks, 1, G), F32), row_bf16, row_out,
                                  row_bf16]
    out_specs = [row_spec] * 5 + [pl.BlockSpec((cpt, 1, G), omap3), row_spec, row_spec, row_spec]
    if sample:
        out_shape += [row_out, jax.ShapeDtypeStruct((t // 8, RW), F32)]
        out_specs += [row_spec, pl.BlockSpec((rows // 8, RW), omap)]
        scratch = []
    else:
        out_shape += [jax.ShapeDtypeStruct((t // rows, 8, G), F32),
                      jax.ShapeDtypeStruct((t // rows, 8, RW), F32)]
        out_specs += [pl.BlockSpec((1, 8, G), omap3), pl.BlockSpec((1, 8, RW), omap3)]
        scratch = [pltpu.VMEM((8, RW), F32), pltpu.VMEM((8, G), F32)]
    return pl.pallas_call(
        functools.partial(_prep_kernel, rows=rows, chunk=chunk, sample=sample),
        grid=grid, in_specs=in_specs, out_specs=out_specs, out_shape=out_shape,
        scratch_shapes=scratch,
        compiler_params=pltpu.CompilerParams(dimension_semantics=sem, vmem_limit_bytes=VMEM_LIMIT),
        name=name,
    )(*args)


def _scan_kernel(ah_ref, rh_ref, bh_ref, kh_ref, v_ref, wc_ref, s0_ref, y_ref, s_ref,
                 *, n_par, rows, chunk, per_chunk_state, group):
    if per_chunk_state:
        s_in = s0_ref
    else:
        s_in = s_ref

        @pl.when(pl.program_id(1) == 0)
        def _():
            for q in range(n_par):
                s_ref[q] = s0_ref[0]

    n_blk = rows // chunk
    lanes = [slice(h * HEAD_DIM, (h + 1) * HEAD_DIM) for h in range(N_HEADS)]
    bf = lambda x: x.astype(BF16)

    shift = chunk.bit_length() - 1
    rs = lax.broadcasted_iota(jnp.int32, (rows, rows), 0)
    cs = lax.broadcasted_iota(jnp.int32, (rows, rows), 1)
    same = (rs >> shift) == (cs >> shift)
    mask_strict = same & (rs > cs)
    ri = lax.broadcasted_iota(jnp.int32, (rows, 2 * rows), 0)
    ci = lax.broadcasted_iota(jnp.int32, (rows, 2 * rows), 1)
    cj = jnp.where(ci >= rows, ci - rows, ci)
    mask_r = ((ri >> shift) == (cj >> shift)) & (ri >= cj)
    eye = (rs == cs).astype(F32)

    def chunk_rows(x, c):
        return x[c * chunk:(c + 1) * chunk, :]

    pair_lane = lax.broadcasted_iota(jnp.int32, (rows, 2 * HEAD_DIM), 1) >> (HEAD_DIM.bit_length() - 1)

    def pair_masked(ref, q, h):
        hp, hh = divmod(h, 2)
        x2 = ref[q, :, 2 * HEAD_DIM * hp:2 * HEAD_DIM * (hp + 1)].astype(F32)
        return jnp.where(pair_lane == hh, x2, 0.0)

    def unit_group(units):
        per_unit = lambda f: {u: f(*u) for u in units}
        v = per_unit(lambda q, h: v_ref[q, :, lanes[h]])
        if n_blk == 1:
            a = per_unit(lambda q, h: ah_ref[q, :, lanes[h]])
            r = per_unit(lambda q, h: rh_ref[q, :, lanes[h]])
            b = per_unit(lambda q, h: bh_ref[q, :, lanes[h]])
            k = per_unit(lambda q, h: kh_ref[q, :, lanes[h]])
        else:
            a = per_unit(lambda q, h: pair_masked(ah_ref, q, h))
            r = per_unit(lambda q, h: pair_masked(rh_ref, q, h))
            b = per_unit(lambda q, h: pair_masked(bh_ref, q, h))
            k = per_unit(lambda q, h: pair_masked(kh_ref, q, h))
            v32 = {u: v[u].astype(F32) for u in units}
        ar = {u: bf(jnp.concatenate([a[u], r[u]], axis=0)) for u in units}
        bk = {u: bf(jnp.concatenate([b[u], k[u]], axis=0)) for u in units}

        gram = {u: _dot(ar[u], jnp.transpose(bk[u])) for u in units}
        l_ab = {u: jnp.where(mask_strict, gram[u][:rows, :rows], 0.0) for u in units}
        l_ak = {u: bf(jnp.where(mask_strict, gram[u][:rows, rows:], 0.0)) for u in units}
        l_r = {u: bf(jnp.where(mask_r, gram[u][rows:, :], 0.0)) for u in units}

        xs, ys = {}, {}
        for u in units:
            q, h = u
            if n_blk == 1:
                st = _dot_nt(ar[u], bf(s_in[q, h]))
                xs[u], ys[u] = st[:rows], st[rows:]
            else:
                parts = [_dot_nt(jnp.concatenate([chunk_rows(a[u], c), chunk_rows(r[u], c)], axis=0),
                                 s_in[c, h // 2]) for c in range(n_blk)]
                xs[u] = jnp.concatenate([p[:chunk] for p in parts], axis=0)
                ys[u] = jnp.concatenate([p[chunk:] for p in parts], axis=0)

        x = {u: xs[u] + _dot(l_ak[u], v[u]) for u in units}

        t_inv = {u: eye + l_ab[u] for u in units}
        n = 2
        if n < chunk:
            pw = {u: bf(l_ab[u]) for u in units}
            pw = {u: bf(_dot(pw[u], pw[u])) for u in units}
        while n < chunk:
            if 2 * n < chunk:
                both = {u: _dot(jnp.concatenate([bf(t_inv[u]), pw[u]], axis=0), pw[u]) for u in units}
                t_inv = {u: t_inv[u] + both[u][:rows] for u in units}
                pw = {u: bf(both[u][rows:]) for u in units}
            else:
                t_inv = {u: t_inv[u] + _dot(bf(t_inv[u]), pw[u]) for u in units}
            n *= 2

        uu = {u: _dot(bf(t_inv[u]), bf(x[u])) for u in units}
        uv = {u: jnp.concatenate([bf(uu[u]), v[u]], axis=0) for u in units}
        for u in units:
            q, h = u
            y_ref[q, :, lanes[h]] = ys[u] + _dot(l_r[u], uv[u])

        if n_blk == 1:
            for u in units:
                q, h = u
                upd = _dot_tn(uv[u], bk[u])
                s_ref[q, h] = (s_in[q, h] + upd) * wc_ref[q, 0][:, lanes[h]]
        else:
            def update(u, c):
                uv_c = jnp.concatenate([chunk_rows(uu[u], c), chunk_rows(v32[u], c)], axis=0)
                bk_c = jnp.concatenate([chunk_rows(b[u], c), chunk_rows(k[u], c)], axis=0)
                return _dot_tn(uv_c, bk_c)

            for q, h in units:
                if h % 2 == 0:
                    assert (q, h + 1) in units
                    hp = h // 2
                    for c in range(n_blk):
                        s_new = s_in[c, hp] + update((q, h), c) + update((q, h + 1), c)
                        s_ref[c, hp] = s_new * wc_ref[q, c][:, 2 * HEAD_DIM * hp:2 * HEAD_DIM * (hp + 1)]

    all_units = [(q, h) for h in range(N_HEADS) for q in range(n_par)]
    for g0 in range(0, len(all_units), group):
        unit_group(all_units[g0:g0 + group])


def _scan(ah, rh, bh, kh, v, wc, s0, *, n_seq, seq_len, n_par, rows, chunk, per_chunk_state, group,
          name):
    t = n_seq * seq_len
    n_blk = rows // chunk
    if per_chunk_state:
        assert seq_len == chunk and n_par == 1
        lead = 1
        grid = (t // rows,)
        rmap = lambda i: (0, i, 0)
        cmap = lambda i: (0, i, 0, 0)
        smap = lambda i: (i, 0, 0, 0)
        s0map = smap
        state_block = (n_blk, N_HEADS // 2, HEAD_DIM, 2 * HEAD_DIM)
        s0_block = state_block
        wc_block = (1, n_blk, 1, G)
        sem = ("parallel",)
    else:
        assert rows == chunk and s0.shape[0] == 1 and n_seq % n_par == 0
        lead = n_seq
        grid = (n_seq // n_par, seq_len // chunk)
        rmap = lambda s, c: (s, c, 0)
        cmap = lambda s, c: (s, c, 0, 0)
        smap = lambda s, c: (s, 0, 0, 0)
        s0map = lambda s, c: (0, 0, 0, 0)
        state_block = (n_par, N_HEADS, HEAD_DIM, HEAD_DIM)
        s0_block = (1, N_HEADS, HEAD_DIM, HEAD_DIM)
        wc_block = (n_par, 1, 1, G)
        sem = ("parallel", "arbitrary")
    rows3 = lambda x: x.reshape(lead, t // lead, G)
    row_spec = pl.BlockSpec((n_par, rows, G), rmap)
    y, s_out = pl.pallas_call(
        functools.partial(_scan_kernel, n_par=n_par, rows=rows, chunk=chunk,
                          per_chunk_state=per_chunk_state, group=group),
        grid=grid,
        in_specs=[row_spec] * 5 + [pl.BlockSpec(wc_block, cmap), pl.BlockSpec(s0_block, s0map)],
        out_specs=[row_spec, pl.BlockSpec(state_block, smap)],
        out_shape=[jax.ShapeDtypeStruct((lead, t // lead, G), F32),
                   jax.ShapeDtypeStruct((n_seq,) + state_block[1:], F32)],
        compiler_params=pltpu.CompilerParams(dimension_semantics=sem, vmem_limit_bytes=VMEM_LIMIT),
        name=name,
    )(rows3(ah), rows3(rh), rows3(bh), rows3(kh), rows3(v), wc.reshape(lead, -1, 1, G), s0)
    return y.reshape(t, G), s_out


def _mix_kernel(*refs, segs):
    n = len(segs)
    row_refs = [refs[4 * s:4 * s + 4] for s in range(n)]
    x_ref, lnw_ref, lnb_ref, bd_ref, wo_ref, o_ref = refs[4 * n:]
    i = pl.program_id(0)

    def body(y_ref, bonus_ref, g_ref, conv_ref):
        bd = bd_ref[...]
        y = y_ref[...]
        mu = _head_sum(y, bd) * (1.0 / HEAD_DIM)
        d = y - mu
        var = _head_sum(d * d, bd) * (1.0 / HEAD_DIM)
        yn = d * lax.rsqrt(var + GN_EPS) * lnw_ref[...] + lnb_ref[...]
        rw = ((yn + bonus_ref[...]) * g_ref[...]).astype(BF16)
        mix = _dot(rw, wo_ref[0:G, :]) + _dot(conv_ref[...], wo_ref[G:2 * G, :])
        o_ref[...] = x_ref[...] + mix

    for rr, (off, cnt) in zip(row_refs, segs):
        @pl.when((i >= off) & (i < off + cnt))
        def _(rr=rr):
            body(*rr)


def _mix(row_groups, x1, lnw, lnb, bd, wo):
    segs, n_tiles = _segments([grp[0] for grp in row_groups], TM_MIX)
    cmap = lambda i: (0, 0)
    in_specs, args = [], []
    for grp, (off, cnt) in zip(row_groups, segs):
        smap = lambda i, off=off, cnt=cnt: (jnp.clip(i - off, 0, cnt - 1), 0)
        in_specs += [pl.BlockSpec((TM_MIX, G), smap)] * 4
        args += list(grp)
    in_specs += [pl.BlockSpec((TM_MIX, D_MODEL), lambda i: (i, 0)),
                 pl.BlockSpec((1, G), cmap), pl.BlockSpec((1, G), cmap),
                 pl.BlockSpec((G, G), cmap), pl.BlockSpec((D_MODEL, D_MODEL), cmap)]
    return pl.pallas_call(
        functools.partial(_mix_kernel, segs=segs),
        grid=(n_tiles,), in_specs=in_specs,
        out_specs=pl.BlockSpec((TM_MIX, D_MODEL), lambda i: (i, 0)),
        out_shape=jax.ShapeDtypeStruct((n_tiles * TM_MIX, D_MODEL), F32),
        compiler_params=pltpu.CompilerParams(
            dimension_semantics=("parallel",), vmem_limit_bytes=VMEM_LIMIT),
        name="mix",
    )(*args, x1, lnw, lnb, bd, wo)


def _pack_rwkv(a):
    r, wd, k, v, ad, gd = jnp.split(a, RWKV_SPLITS, axis=-1)
    zw = jnp.zeros(a.shape[:-1] + (LORA_PAD - W_LORA,), a.dtype)
    za = jnp.zeros(a.shape[:-1] + (LORA_PAD - A_LORA,), a.dtype)
    return jnp.concatenate([r, k, v, wd, zw, ad, za, gd], axis=-1)


def _unpack_rwkv(p):
    return jnp.concatenate([
        p[..., 0:G], p[..., OFF_WD:OFF_WD + W_LORA], p[..., G:2 * G], p[..., 2 * G:3 * G],
        p[..., OFF_AD:OFF_AD + A_LORA], p[..., OFF_GD:OFF_GD + G_LORA]], axis=-1)


def _block_tri(n, chunk):
    i = jnp.arange(n)
    return ((i[:, None] // chunk == i[None, :] // chunk) & (i[:, None] >= i[None, :])).astype(BF16)


def kernel(x_prompt, x_sample, state_wkv, state_shift, state_conv, meta_tokens, g_ffn1, ffn1_gate, ffn1_up, ffn1_down, g_mix, w_in, mu_shift, w0, w_lora_w, a0, w_lora_a, w_lora_g, k_k, k_a, r_k, ln_x_w, ln_x_b, conv_w, w_out, g_ffn2, ffn2_gate, ffn2_up, ffn2_down, g_final):
    assert g_ffn1.shape[0] == 1, "single layer"
    nb, seq, _ = x_prompt.shape
    db, dseq, _ = x_sample.shape
    assert dseq == C_SAMPLE and N_META <= C_PROMPT and seq % ROWS_PREP == 0
    tp, ts = nb * seq, db * dseq
    assert tp % TM == 0 and ts % TM == 0

    row = lambda a: a.reshape(1, -1).astype(F32)
    w_in_p = _pack_w_in(jnp.transpose(w_in[0]))
    pad_rows = lambda w, n: jnp.concatenate([w, jnp.zeros((n - w.shape[0], w.shape[1]), w.dtype)], axis=0)
    hid = jnp.arange(G) // HEAD_DIM
    bd = (hid[:, None] == hid[None, :]).astype(BF16)
    prep_w = (_pack_rwkv(mu_shift[0])[None], row(w0[0]), pad_rows(w_lora_w[0], LORA_PAD).astype(BF16),
              row(a0[0]), pad_rows(w_lora_a[0], LORA_PAD).astype(BF16), w_lora_g[0].astype(BF16),
              row(k_k[0]), row(k_a[0]), row(r_k[0]), conv_w[0].astype(F32), bd)

    ffn1_w = (row(g_ffn1[0]), ffn1_gate[0].astype(BF16), ffn1_up[0].astype(BF16), ffn1_down[0].astype(BF16))
    x1 = _ffn([x_prompt.reshape(tp, D_MODEL), x_sample.reshape(ts, D_MODEL)], *ffn1_w, None, [tp + ts],
              name="ffn1")[0]
    proj = _proj(x1, row(g_mix[0]), w_in_p, name="proj")
    x_meta = jnp.concatenate([jnp.zeros((C_PROMPT - N_META, D_MODEL), F32), meta_tokens.astype(F32)], axis=0)
    x1_meta = _ffn([x_meta], *ffn1_w, None, [C_PROMPT], name="ffn1_meta", tm=C_PROMPT, tf=D_FF // 4)[0]
    proj_meta = _proj(x1_meta, row(g_mix[0]), w_in_p, name="proj_meta", tm=C_PROMPT)

    tri_p = _block_tri(ROWS_PREP, C_PROMPT)
    zeros_state = jnp.zeros((1, N_HEADS, HEAD_DIM, HEAD_DIM), F32)
    (ah, rh, bh, kh, vv, wc, _, _, _, utail_m, ptail_m) = _prep(
        proj_meta, prep_w, tri_p[:C_PROMPT, :C_PROMPT], n_seq=1, seq_len=C_PROMPT, rows=C_PROMPT,
        chunk=C_PROMPT, row_block_offset=0, extra=(jnp.zeros((8, RW), F32), jnp.zeros((8, G), F32)),
        sample=False, name="prep_meta")
    _, wkv_m = _scan(ah, rh, bh, kh, vv, wc, zeros_state, n_seq=1, seq_len=C_PROMPT, n_par=1,
                     rows=C_PROMPT, chunk=C_PROMPT, per_chunk_state=False, group=N_HEADS,
                     name="scan_meta")

    (ah, rh, bh, kh, vv, wc, g_p, bonus_p, conv_p, utail_p, ptail_p) = _prep(
        proj, prep_w, tri_p, n_seq=nb, seq_len=seq, rows=ROWS_PREP, chunk=C_PROMPT,
        row_block_offset=0, extra=(ptail_m[0], utail_m[0]), sample=False, name="prep_prompt")
    y_p, wkv_p = _scan(ah, rh, bh, kh, vv, wc, wkv_m, n_seq=nb, seq_len=seq, n_par=SCAN_PAR,
                       rows=C_PROMPT, chunk=C_PROMPT, per_chunk_state=False,
                       group=SCAN_PAR * N_HEADS // 2, name="scan_prompt")

    prev = _pack_rwkv(state_shift[0])
    up1 = state_conv[0][:, 1]
    up2 = state_conv[0][:, 0]
    (ah, rh, bh, kh, vv, wc, g_s, bonus_s, conv_s, u_s, plast_s) = _prep(
        proj, prep_w, _block_tri(ROWS_PREP, C_SAMPLE), n_seq=db, seq_len=dseq, rows=ROWS_PREP,
        chunk=C_SAMPLE, row_block_offset=tp // ROWS_PREP, extra=(prev, up1, up2), sample=True,
        name="prep_sample")
    pair = (db, N_HEADS // 2, 2, HEAD_DIM, HEAD_DIM)
    s0_pairs = jnp.transpose(state_wkv[0].reshape(pair), (0, 1, 3, 2, 4)).reshape(
        db, N_HEADS // 2, HEAD_DIM, 2 * HEAD_DIM)
    y_s, wkv_s = _scan(ah, rh, bh, kh, vv, wc, s0_pairs, n_seq=db, seq_len=dseq, n_par=1,
                       rows=ROWS_SCAN_SAMPLE, chunk=C_SAMPLE, per_chunk_state=True, group=4,
                       name="scan_sample")

    x2 = _mix([(y_p, bonus_p, g_p, conv_p), (y_s, bonus_s, g_s, conv_s)], x1,
              row(ln_x_w[0]), row(ln_x_b[0]), bd, w_out[0].astype(BF16))
    y_prompt, y_sample = _ffn([x2], row(g_ffn2[0]), ffn2_gate[0].astype(BF16), ffn2_up[0].astype(BF16),
                              ffn2_down[0].astype(BF16), row(g_final), [tp, ts], name="ffn2")

    shift_p = _unpack_rwkv(ptail_p.reshape(nb, seq // ROWS_PREP, 8, RW)[:, -1, 7, :])
    conv_state_p = utail_p.reshape(nb, seq // ROWS_PREP, 8, G)[:, -1, 6:, :]
    shift_s = _unpack_rwkv(plast_s)
    conv_state_s = u_s.reshape(db, dseq, G)[:, -2:, :]
    wkv_s_out = jnp.transpose(wkv_s.reshape(db, N_HEADS // 2, HEAD_DIM, 2, HEAD_DIM),
                              (0, 1, 3, 2, 4)).reshape(db, N_HEADS, HEAD_DIM, HEAD_DIM)
    return (y_prompt.reshape(nb, seq, D_MODEL), y_sample.reshape(db, dseq, D_MODEL),
            wkv_p[None].astype(state_wkv.dtype), shift_p[None].astype(state_shift.dtype),
            conv_state_p[None].astype(state_conv.dtype),
            wkv_s_out[None].astype(state_wkv.dtype), shift_s[None].astype(state_shift.dtype),
            conv_state_s[None].astype(state_conv.dtype))
```

```python
import functools

import jax
import jax.numpy as jnp
from jax import lax
from jax.experimental import pallas as pl
from jax.experimental.pallas import tpu as pltpu

F32 = jnp.float32
BF16 = jnp.bfloat16

D_MODEL = 2048
D_FF = 5632
N_META = 16
G = 1024
HEAD_DIM = 64
N_HEADS = G // HEAD_DIM
W_LORA = 96
A_LORA = 96
G_LORA = 256
LORA_PAD = 128
RWKV_PROJ = 3 * G + W_LORA + A_LORA + G_LORA
RWKV_SPLITS = (G, G + W_LORA, 2 * G + W_LORA, 3 * G + W_LORA, 3 * G + W_LORA + A_LORA)
RW = 3 * G + 2 * LORA_PAD + G_LORA
PW = RW + 3 * G
OFF_WD, OFF_AD, OFF_GD = 3 * G, 3 * G + LORA_PAD, 3 * G + 2 * LORA_PAD
RMS_EPS = 1e-6
GN_EPS = 64e-5
EXP_M05_LOG2E = 0.6065306597126334 * 1.4426950408889634

C_PROMPT = 64
C_SAMPLE = 8
TM = 512
TF = 512
TN = 3328
TM_MIX = 256
ROWS_PREP = 256
ROWS_SCAN_SAMPLE = 128
SCAN_PAR = 4
VMEM_LIMIT = 56 * 1024 * 1024


def _dot(a, b):
    return jnp.dot(a, b, preferred_element_type=F32)


def _dot_nt(a, b):
    return lax.dot_general(a, b, (((1,), (1,)), ((), ())), preferred_element_type=F32)


def _dot_tn(a, b):
    return lax.dot_general(a, b, (((0,), (0,)), ((), ())), preferred_element_type=F32)


def _split(a):
    hi = a.astype(BF16)
    lo = (a - hi.astype(F32)).astype(BF16)
    return hi, lo


def _head_sum(x, bd):
    return _dot(x.astype(BF16), bd)


def _dot_split_rhs(a_bf16, b):
    hi, lo = _split(b)
    return _dot(a_bf16, hi) + _dot(a_bf16, lo)


def _rms(x, g):
    return x * lax.rsqrt(jnp.mean(x * x, axis=-1, keepdims=True) + RMS_EPS) * g


def _segments(arrays, tile):
    segs, off = [], 0
    for arr in arrays:
        n = arr.shape[0] // tile
        assert n * tile == arr.shape[0]
        segs.append((off, n))
        off += n
    return segs, off


def _seg_spec(tile, width, seg):
    off, n = seg
    return pl.BlockSpec((tile, width), lambda i: (jnp.clip(i - off, 0, n - 1), 0))


def _overlaps(in_segs, out_segs):
    for a, (ao, an) in enumerate(in_segs):
        for b, (bo, bn) in enumerate(out_segs):
            lo, hi = max(ao, bo), min(ao + an, bo + bn)
            if lo < hi:
                yield a, b, lo, hi


def _ffn_kernel(*refs, in_segs, out_segs, final_norm, tf):
    n_in, n_out = len(in_segs), len(out_segs)
    x_refs = refs[:n_in]
    g_ref, wg_hbm, wu_hbm, wd_hbm = refs[n_in:n_in + 4]
    rest = refs[n_in + 4:]
    if final_norm:
        gf_ref, rest = rest[0], rest[1:]
    o_refs = rest[:n_out]
    wg_buf, wu_buf, wd_buf, sem, h_ref, acc_ref = rest[n_out:]
    n_ff = wd_hbm.shape[0] // tf
    i = pl.program_id(0)
    last_tile = i == pl.num_programs(0) - 1

    def weight_copies(j, slot):
        cols = pl.ds(pl.multiple_of(j * tf, tf), tf)
        return (pltpu.make_async_copy(wg_hbm.at[:, cols], wg_buf.at[slot], sem.at[0, slot]),
                pltpu.make_async_copy(wu_hbm.at[:, cols], wu_buf.at[slot], sem.at[1, slot]),
                pltpu.make_async_copy(wd_hbm.at[cols, :], wd_buf.at[slot], sem.at[2, slot]))

    @pl.when(i == 0)
    def _():
        for c in weight_copies(0, 0):
            c.start()

    for x_ref, (off, n) in zip(x_refs, in_segs):
        @pl.when((i >= off) & (i < off + n))
        def _(x_ref=x_ref):
            h_ref[...] = _rms(x_ref[...], g_ref[...]).astype(BF16)
            acc_ref[...] = jnp.zeros_like(acc_ref)

    def sweep(j, carry):
        slot = lax.rem(i * n_ff + j, 2)
        for c in weight_copies(j, slot):
            c.wait()
        wraps = j + 1 == n_ff

        @pl.when(jnp.logical_not(wraps & last_tile))
        def _():
            for c in weight_copies(jnp.where(wraps, 0, j + 1), 1 - slot):
                c.start()

        h = h_ref[...]
        gate = _dot(h, wg_buf[slot])
        up = _dot(h, wu_buf[slot])
        act = (gate * jax.nn.sigmoid(gate) * up).astype(BF16)
        acc_ref[...] += _dot(act, wd_buf[slot])
        return carry

    lax.fori_loop(0, n_ff, sweep, 0)

    for a, b, lo, hi in _overlaps(in_segs, out_segs):
        @pl.when((i >= lo) & (i < hi))
        def _(x_ref=x_refs[a], o_ref=o_refs[b]):
            xo = x_ref[...] + 0.5 * acc_ref[...]
            if final_norm:
                xo = _rms(xo, gf_ref[...])
            o_ref[...] = xo


def _ffn(xs, g, wg, wu, wd, g_final, out_rows, *, name, tm=TM, tf=TF):
    final_norm = g_final is not None
    in_segs, n_tiles_in = _segments(xs, tm)
    out_shape = [jax.ShapeDtypeStruct((n, D_MODEL), F32) for n in out_rows]
    out_segs, n_tiles = _segments(out_shape, tm)
    assert n_tiles <= n_tiles_in and D_FF % tf == 0
    hbm = pl.BlockSpec(memory_space=pl.ANY)
    in_specs = [_seg_spec(tm, D_MODEL, s) for s in in_segs] + [
        pl.BlockSpec((1, D_MODEL), lambda i: (0, 0)), hbm, hbm, hbm]
    args = list(xs) + [g, wg, wu, wd]
    if final_norm:
        in_specs.append(pl.BlockSpec((1, D_MODEL), lambda i: (0, 0)))
        args.append(g_final)
    return pl.pallas_call(
        functools.partial(_ffn_kernel, in_segs=in_segs, out_segs=out_segs, final_norm=final_norm, tf=tf),
        grid=(n_tiles,),
        in_specs=in_specs,
        out_specs=[_seg_spec(tm, D_MODEL, s) for s in out_segs],
        out_shape=out_shape,
        scratch_shapes=[pltpu.VMEM((2, D_MODEL, tf), BF16), pltpu.VMEM((2, D_MODEL, tf), BF16),
                        pltpu.VMEM((2, tf, D_MODEL), BF16), pltpu.SemaphoreType.DMA((3, 2)),
                        pltpu.VMEM((tm, D_MODEL), BF16), pltpu.VMEM((tm, D_MODEL), F32)],
        compiler_params=pltpu.CompilerParams(
            dimension_semantics=("arbitrary",), vmem_limit_bytes=VMEM_LIMIT),
        name=name,
    )(*args)


_PACK_MOVES = (
    (0, 0, G),
    (G, RWKV_SPLITS[1], G),
    (2 * G, RWKV_SPLITS[2], G),
    (OFF_WD, RWKV_SPLITS[0], W_LORA),
    (OFF_AD, RWKV_SPLITS[3], A_LORA),
    (OFF_GD, RWKV_SPLITS[4], G_LORA),
    (RW, RWKV_PROJ, 3 * G),
)


def _pack_w_in_kernel(w_ref, o_ref):
    for dst, src, n in _PACK_MOVES:
        o_ref[dst:dst + n, :] = w_ref[src:src + n, :].astype(BF16)
    for lo, hi in ((OFF_WD + W_LORA, OFF_AD), (OFF_AD + A_LORA, OFF_GD)):
        o_ref[lo:hi, :] = jnp.zeros((hi - lo, o_ref.shape[1]), BF16)


def _pack_w_in(w_t):
    cols = 512
    return pl.pallas_call(
        _pack_w_in_kernel,
        grid=(D_MODEL // cols,),
        in_specs=[pl.BlockSpec((w_t.shape[0], cols), lambda i: (0, i))],
        out_specs=pl.BlockSpec((PW, cols), lambda i: (0, i)),
        out_shape=jax.ShapeDtypeStruct((PW, D_MODEL), BF16),
        compiler_params=pltpu.CompilerParams(
            dimension_semantics=("parallel",), vmem_limit_bytes=VMEM_LIMIT),
        name="pack_w_in",
    )(w_t)


def _proj_kernel(x_ref, g_ref, w_ref, o_ref):
    h = _rms(x_ref[...], g_ref[...]).astype(BF16)
    o_ref[...] = _dot_nt(h, w_ref[...])


def _proj(x, g, w, *, name, tm=TM):
    t = x.shape[0]
    return pl.pallas_call(
        _proj_kernel,
        grid=(PW // TN, t // tm),
        in_specs=[
            pl.BlockSpec((tm, D_MODEL), lambda j, i: (i, 0)),
            pl.BlockSpec((1, D_MODEL), lambda j, i: (0, 0)),
            pl.BlockSpec((TN, D_MODEL), lambda j, i: (j, 0)),
        ],
        out_specs=pl.BlockSpec((tm, TN), lambda j, i: (i, j)),
        out_shape=jax.ShapeDtypeStruct((t, PW), F32),
        compiler_params=pltpu.CompilerParams(
            dimension_semantics=("parallel", "parallel"), vmem_limit_bytes=VMEM_LIMIT),
        name=name,
    )(x, g, w)


def _prep_kernel(*refs, rows, chunk, sample):
    if sample:
        (p_ref, prev_ref, up1_ref, up2_ref, mu_ref, w0_ref, ww_ref, a0_ref, wa_ref, wgl_ref,
         kk_ref, ka_ref, rk_ref, cw_ref, bd_ref, tri_ref,
         ah_ref, rh_ref, bh_ref, kh_ref, v_ref, wc_ref, g_ref, bonus_ref, conv_ref, u_ref,
         plast_ref) = refs
    else:
        (p_ref, cp0_ref, cu0_ref, mu_ref, w0_ref, ww_ref, a0_ref, wa_ref, wgl_ref,
         kk_ref, ka_ref, rk_ref, cw_ref, bd_ref, tri_ref,
         ah_ref, rh_ref, bh_ref, kh_ref, v_ref, wc_ref, g_ref, bonus_ref, conv_ref, u_ref,
         ptail_ref, carry_p, carry_u) = refs

        @pl.when(pl.program_id(1) == 0)
        def _():
            carry_p[...] = cp0_ref[...]
            carry_u[...] = cu0_ref[...]

    def row_ids(width):
        r = lax.broadcasted_iota(jnp.int32, (rows, width), 0)
        return (r & 7) if sample else r

    def per_sequence(x):
        n, w = x.shape
        return jnp.broadcast_to(x[:, None, :], (n, 8, w)).reshape(n * 8, w)

    def shifted_mix(lo, hi):
        p = p_ref[:, lo:hi]
        rolled = pltpu.roll(p, 1, 0)
        if sample:
            sh = jnp.where(row_ids(hi - lo) == 0, per_sequence(prev_ref[:, lo:hi]), rolled)
        else:
            sh = jnp.where(row_ids(hi - lo) == 0, carry_p[7:8, lo:hi], rolled)
        return p + (sh - p) * mu_ref[:, lo:hi]

    r = shifted_mix(0, G)
    k = shifted_mix(G, 2 * G)
    v = shifted_mix(2 * G, 3 * G)
    wd = shifted_mix(OFF_WD, OFF_WD + LORA_PAD)
    ad = shifted_mix(OFF_AD, OFF_AD + LORA_PAD)
    gd = shifted_mix(OFF_GD, OFF_GD + G_LORA)

    z = w0_ref[...] + _dot(jnp.tanh(wd).astype(BF16), ww_ref[...])
    lw = -EXP_M05_LOG2E * jax.nn.sigmoid(z)
    a = jax.nn.sigmoid(a0_ref[...] + _dot(ad.astype(BF16), wa_ref[...]))
    g_ref[...] = _dot(jax.nn.sigmoid(gd).astype(BF16), wgl_ref[...]).astype(BF16)

    bd = bd_ref[...]
    kk = k * kk_ref[...]
    norm = jnp.sqrt(_head_sum(kk * kk, bd))
    kk = kk / jnp.maximum(norm, 1e-12)
    km = k * (1.0 + (a - 1.0) * ka_ref[...])
    bonus_ref[...] = _head_sum(r * km * rk_ref[...], bd) * v

    cum = _dot_split_rhs(tri_ref[...], lw)
    e_cum = jnp.exp2(cum)
    e_inv = jnp.exp2(-cum)
    ah_ref[...] = (-kk * jnp.exp2(cum - lw)).astype(BF16)
    rh_ref[...] = (r * e_cum).astype(BF16)
    bh_ref[...] = (kk * a * e_inv).astype(BF16)
    kh_ref[...] = (km * e_inv).astype(BF16)
    v_ref[...] = v.astype(BF16)
    for c in range(rows // chunk):
        wc_ref[c] = e_cum[(c + 1) * chunk - 1:(c + 1) * chunk, :]

    bg = p_ref[:, RW:RW + G]
    u = p_ref[:, RW + G:RW + 2 * G] * p_ref[:, RW + 2 * G:RW + 3 * G]
    u1 = pltpu.roll(u, 1, 0)
    u2 = pltpu.roll(u, 2, 0)
    rid = row_ids(G)
    if sample:
        s1 = per_sequence(up1_ref[...])
        um1 = jnp.where(rid == 0, s1, u1)
        um2 = jnp.where(rid == 0, per_sequence(up2_ref[...]), jnp.where(rid == 1, s1, u2))
        u_ref[...] = u
        plast_ref[...] = p_ref[:, 0:RW].reshape(rows // 8, 8, RW)[:, 7, :]
    else:
        um1 = jnp.where(rid == 0, carry_u[7:8, :], u1)
        um2 = jnp.where(rid == 0, carry_u[6:7, :], jnp.where(rid == 1, carry_u[7:8, :], u2))
        u_ref[0] = u[rows - 8:, :]
        ptail_ref[0] = p_ref[rows - 8:rows, 0:RW]
    conv = cw_ref[0:1, :] * um2 + cw_ref[1:2, :] * um1 + cw_ref[2:3, :] * u
    conv_ref[...] = (bg * conv).astype(BF16)

    if not sample:
        carry_p[...] = p_ref[rows - 8:rows, 0:RW]
        carry_u[...] = u[rows - 8:, :]


def _prep(proj, weights, tri, *, n_seq, seq_len, rows, chunk, row_block_offset, extra, sample, name):
    t = n_seq * seq_len
    n_chunks = t // chunk
    cpt = rows // chunk
    if sample:
        grid = (t // rows,)
        rmap = lambda i: (i + row_block_offset, 0)
        omap = lambda i: (i, 0)
        omap3 = lambda i: (i, 0, 0)
        cmap = lambda i: (0, 0)
        sem = ("parallel",)
    else:
        tiles = seq_len // rows
        grid = (n_seq, tiles)
        rmap = lambda b, j: (b * tiles + j + row_block_offset, 0)
        omap = lambda b, j: (b * tiles + j, 0)
        omap3 = lambda b, j: (b * tiles + j, 0, 0)
        cmap = lambda b, j: (0, 0)
        sem = ("parallel", "arbitrary")

    const = lambda arr: pl.BlockSpec(arr.shape, cmap)
    in_specs = [pl.BlockSpec((rows, PW), rmap)]
    if sample:
        in_specs += [pl.BlockSpec((rows // 8, RW), omap), pl.BlockSpec((rows // 8, G), omap),
                     pl.BlockSpec((rows // 8, G), omap)]
    else:
        in_specs += [const(e) for e in extra]
    in_specs += [const(w) for w in weights] + [const(tri)]
    args = [proj] + list(extra) + list(weights) + [tri]

    row_out = jax.ShapeDtypeStruct((t, G), F32)
    row_spec = pl.BlockSpec((rows, G), omap)
    row_bf16 = jax.ShapeDtypeStruct((t, G), BF16)
    out_shape = [row_bf16] * 5 + [jax.ShapeDtypeStruct((n_chunks, 1, G), F32), row_bf16, row_out,
                                  row_bf16]
    out_specs = [row_spec] * 5 + [pl.BlockSpec((cpt, 1, G), omap3), row_spec, row_spec, row_spec]
    if sample:
        out_shape += [row_out, jax.ShapeDtypeStruct((t // 8, RW), F32)]
        out_specs += [row_spec, pl.BlockSpec((rows // 8, RW), omap)]
        scratch = []
    else:
        out_shape += [jax.ShapeDtypeStruct((t // rows, 8, G), F32),
                      jax.ShapeDtypeStruct((t // rows, 8, RW), F32)]
        out_specs += [pl.BlockSpec((1, 8, G), omap3), pl.BlockSpec((1, 8, RW), omap3)]
        scratch = [pltpu.VMEM((8, RW), F32), pltpu.VMEM((8, G), F32)]
    return pl.pallas_call(
        functools.partial(_prep_kernel, rows=rows, chunk=chunk, sample=sample),
        grid=grid, in_specs=in_specs, out_specs=out_specs, out_shape=out_shape,
        scratch_shapes=scratch,
        compiler_params=pltpu.CompilerParams(dimension_semantics=sem, vmem_limit_bytes=VMEM_LIMIT),
        name=name,
    )(*args)


def _scan_kernel(ah_ref, rh_ref, bh_ref, kh_ref, v_ref, wc_ref, s0_ref, y_ref, s_ref,
                 *, n_par, rows, chunk, per_chunk_state, group):
    if per_chunk_state:
        s_in = s0_ref
    else:
        s_in = s_ref

        @pl.when(pl.program_id(1) == 0)
        def _():
            for q in range(n_par):
                s_ref[q] = s0_ref[0]

    n_blk = rows // chunk
    lanes = [slice(h * HEAD_DIM, (h + 1) * HEAD_DIM) for h in range(N_HEADS)]
    bf = lambda x: x.astype(BF16)

    shift = chunk.bit_length() - 1
    rs = lax.broadcasted_iota(jnp.int32, (rows, rows), 0)
    cs = lax.broadcasted_iota(jnp.int32, (rows, rows), 1)
    same = (rs >> shift) == (cs >> shift)
    mask_strict = same & (rs > cs)
    ri = lax.broadcasted_iota(jnp.int32, (rows, 2 * rows), 0)
    ci = lax.broadcasted_iota(jnp.int32, (rows, 2 * rows), 1)
    cj = jnp.where(ci >= rows, ci - rows, ci)
    mask_r = ((ri >> shift) == (cj >> shift)) & (ri >= cj)
    eye = (rs == cs).astype(F32)

    def chunk_rows(x, c):
        return x[c * chunk:(c + 1) * chunk, :]

    pair_lane = lax.broadcasted_iota(jnp.int32, (rows, 2 * HEAD_DIM), 1) >> (HEAD_DIM.bit_length() - 1)

    def pair_masked(ref, q, h):
        hp, hh = divmod(h, 2)
        x2 = ref[q, :, 2 * HEAD_DIM * hp:2 * HEAD_DIM * (hp + 1)].astype(F32)
        return jnp.where(pair_lane == hh, x2, 0.0)

    def unit_group(units):
        per_unit = lambda f: {u: f(*u) for u in units}
        v = per_unit(lambda q, h: v_ref[q, :, lanes[h]])
        if n_blk == 1:
            a = per_unit(lambda q, h: ah_ref[q, :, lanes[h]])
            r = per_unit(lambda q, h: rh_ref[q, :, lanes[h]])
            b = per_unit(lambda q, h: bh_ref[q, :, lanes[h]])
            k = per_unit(lambda q, h: kh_ref[q, :, lanes[h]])
        else:
            a = per_unit(lambda q, h: pair_masked(ah_ref, q, h))
            r = per_unit(lambda q, h: pair_masked(rh_ref, q, h))
            b = per_unit(lambda q, h: pair_masked(bh_ref, q, h))
            k = per_unit(lambda q, h: pair_masked(kh_ref, q, h))
            v32 = {u: v[u].astype(F32) for u in units}
        ar = {u: bf(jnp.concatenate([a[u], r[u]], axis=0)) for u in units}
        bk = {u: bf(jnp.concatenate([b[u], k[u]], axis=0)) for u in units}

        gram = {u: _dot(ar[u], jnp.transpose(bk[u])) for u in units}
        l_ab = {u: jnp.where(mask_strict, gram[u][:rows, :rows], 0.0) for u in units}
        l_ak = {u: bf(jnp.where(mask_strict, gram[u][:rows, rows:], 0.0)) for u in units}
        l_r = {u: bf(jnp.where(mask_r, gram[u][rows:, :], 0.0)) for u in units}

        xs, ys = {}, {}
        for u in units:
            q, h = u
            if n_blk == 1:
                st = _dot_nt(ar[u], bf(s_in[q, h]))
                xs[u], ys[u] = st[:rows], st[rows:]
            else:
                parts = [_dot_nt(jnp.concatenate([chunk_rows(a[u], c), chunk_rows(r[u], c)], axis=0),
                                 s_in[c, h // 2]) for c in range(n_blk)]
                xs[u] = jnp.concatenate([p[:chunk] for p in parts], axis=0)
                ys[u] = jnp.concatenate([p[chunk:] for p in parts], axis=0)

        x = {u: xs[u] + _dot(l_ak[u], v[u]) for u in units}

        t_inv = {u: eye + l_ab[u] for u in units}
        n = 2
        if n < chunk:
            pw = {u: bf(l_ab[u]) for u in units}
            pw = {u: bf(_dot(pw[u], pw[u])) for u in units}
        while n < chunk:
            if 2 * n < chunk:
                both = {u: _dot(jnp.concatenate([bf(t_inv[u]), pw[u]], axis=0), pw[u]) for u in units}
                t_inv = {u: t_inv[u] + both[u][:rows] for u in units}
                pw = {u: bf(both[u][rows:]) for u in units}
            else:
                t_inv = {u: t_inv[u] + _dot(bf(t_inv[u]), pw[u]) for u in units}
            n *= 2

        uu = {u: _dot(bf(t_inv[u]), bf(x[u])) for u in units}
        uv = {u: jnp.concatenate([bf(uu[u]), v[u]], axis=0) for u in units}
        for u in units:
            q, h = u
            y_ref[q, :, lanes[h]] = ys[u] + _dot(l_r[u], uv[u])

        if n_blk == 1:
            for u in units:
                q, h = u
                upd = _dot_tn(uv[u], bk[u])
                s_ref[q, h] = (s_in[q, h] + upd) * wc_ref[q, 0][:, lanes[h]]
        else:
            def update(u, c):
                uv_c = jnp.concatenate([chunk_rows(uu[u], c), chunk_rows(v32[u], c)], axis=0)
                bk_c = jnp.concatenate([chunk_rows(b[u], c), chunk_rows(k[u], c)], axis=0)
                return _dot_tn(uv_c, bk_c)

            for q, h in units:
                if h % 2 == 0:
                    assert (q, h + 1) in units
                    hp = h // 2
                    for c in range(n_blk):
                        s_new = s_in[c, hp] + update((q, h), c) + update((q, h + 1), c)
                        s_ref[c, hp] = s_new * wc_ref[q, c][:, 2 * HEAD_DIM * hp:2 * HEAD_DIM * (hp + 1)]

    all_units = [(q, h) for h in range(N_HEADS) for q in range(n_par)]
    for g0 in range(0, len(all_units), group):
        unit_group(all_units[g0:g0 + group])


def _scan(ah, rh, bh, kh, v, wc, s0, *, n_seq, seq_len, n_par, rows, chunk, per_chunk_state, group,
          name):
    t = n_seq * seq_len
    n_blk = rows // chunk
    if per_chunk_state:
        assert seq_len == chunk and n_par == 1
        lead = 1
        grid = (t // rows,)
        rmap = lambda i: (0, i, 0)
        cmap = lambda i: (0, i, 0, 0)
        smap = lambda i: (i, 0, 0, 0)
        s0map = smap
        state_block = (n_blk, N_HEADS // 2, HEAD_DIM, 2 * HEAD_DIM)
        s0_block = state_block
        wc_block = (1, n_blk, 1, G)
        sem = ("parallel",)
    else:
        assert rows == chunk and s0.shape[0] == 1 and n_seq % n_par == 0
        lead = n_seq
        grid = (n_seq // n_par, seq_len // chunk)
        rmap = lambda s, c: (s, c, 0)
        cmap = lambda s, c: (s, c, 0, 0)
        smap = lambda s, c: (s, 0, 0, 0)
        s0map = lambda s, c: (0, 0, 0, 0)
        state_block = (n_par, N_HEADS, HEAD_DIM, HEAD_DIM)
        s0_block = (1, N_HEADS, HEAD_DIM, HEAD_DIM)
        wc_block = (n_par, 1, 1, G)
        sem = ("parallel", "arbitrary")
    rows3 = lambda x: x.reshape(lead, t // lead, G)
    row_spec = pl.BlockSpec((n_par, rows, G), rmap)
    y, s_out = pl.pallas_call(
        functools.partial(_scan_kernel, n_par=n_par, rows=rows, chunk=chunk,
                          per_chunk_state=per_chunk_state, group=group),
        grid=grid,
        in_specs=[row_spec] * 5 + [pl.BlockSpec(wc_block, cmap), pl.BlockSpec(s0_block, s0map)],
        out_specs=[row_spec, pl.BlockSpec(state_block, smap)],
        out_shape=[jax.ShapeDtypeStruct((lead, t // lead, G), F32),
                   jax.ShapeDtypeStruct((n_seq,) + state_block[1:], F32)],
        compiler_params=pltpu.CompilerParams(dimension_semantics=sem, vmem_limit_bytes=VMEM_LIMIT),
        name=name,
    )(rows3(ah), rows3(rh), rows3(bh), rows3(kh), rows3(v), wc.reshape(lead, -1, 1, G), s0)
    return y.reshape(t, G), s_out


def _mix_kernel(*refs, segs):
    n = len(segs)
    row_refs = [refs[4 * s:4 * s + 4] for s in range(n)]
    x_ref, lnw_ref, lnb_ref, bd_ref, wo_ref, o_ref = refs[4 * n:]
    i = pl.program_id(0)

    def body(y_ref, bonus_ref, g_ref, conv_ref):
        bd = bd_ref[...]
        y = y_ref[...]
        mu = _head_sum(y, bd) * (1.0 / HEAD_DIM)
        d = y - mu
        var = _head_sum(d * d, bd) * (1.0 / HEAD_DIM)
        yn = d * lax.rsqrt(var + GN_EPS) * lnw_ref[...] + lnb_ref[...]
        rw = ((yn + bonus_ref[...]) * g_ref[...]).astype(BF16)
        mix = _dot(rw, wo_ref[0:G, :]) + _dot(conv_ref[...], wo_ref[G:2 * G, :])
        o_ref[...] = x_ref[...] + mix

    for rr, (off, cnt) in zip(row_refs, segs):
        @pl.when((i >= off) & (i < off + cnt))
        def _(rr=rr):
            body(*rr)


def _mix(row_groups, x1, lnw, lnb, bd, wo):
    segs, n_tiles = _segments([grp[0] for grp in row_groups], TM_MIX)
    cmap = lambda i: (0, 0)
    in_specs, args = [], []
    for grp, (off, cnt) in zip(row_groups, segs):
        smap = lambda i, off=off, cnt=cnt: (jnp.clip(i - off, 0, cnt - 1), 0)
        in_specs += [pl.BlockSpec((TM_MIX, G), smap)] * 4
        args += list(grp)
    in_specs += [pl.BlockSpec((TM_MIX, D_MODEL), lambda i: (i, 0)),
                 pl.BlockSpec((1, G), cmap), pl.BlockSpec((1, G), cmap),
                 pl.BlockSpec((G, G), cmap), pl.BlockSpec((D_MODEL, D_MODEL), cmap)]
    return pl.pallas_call(
        functools.partial(_mix_kernel, segs=segs),
        grid=(n_tiles,), in_specs=in_specs,
        out_specs=pl.BlockSpec((TM_MIX, D_MODEL), lambda i: (i, 0)),
        out_shape=jax.ShapeDtypeStruct((n_tiles * TM_MIX, D_MODEL), F32),
        compiler_params=pltpu.CompilerParams(
            dimension_semantics=("parallel",), vmem_limit_bytes=VMEM_LIMIT),
        name="mix",
    )(*args, x1, lnw, lnb, bd, wo)


def _pack_rwkv(a):
    r, wd, k, v, ad, gd = jnp.split(a, RWKV_SPLITS, axis=-1)
    zw = jnp.zeros(a.shape[:-1] + (LORA_PAD - W_LORA,), a.dtype)
    za = jnp.zeros(a.shape[:-1] + (LORA_PAD - A_LORA,), a.dtype)
    return jnp.concatenate([r, k, v, wd, zw, ad, za, gd], axis=-1)


def _unpack_rwkv(p):
    return jnp.concatenate([
        p[..., 0:G], p[..., OFF_WD:OFF_WD + W_LORA], p[..., G:2 * G], p[..., 2 * G:3 * G],
        p[..., OFF_AD:OFF_AD + A_LORA], p[..., OFF_GD:OFF_GD + G_LORA]], axis=-1)


def _block_tri(n, chunk):
    i = jnp.arange(n)
    return ((i[:, None] // chunk == i[None, :] // chunk) & (i[:, None] >= i[None, :])).astype(BF16)


def kernel(x_prompt, x_sample, state_wkv, state_shift, state_conv, meta_tokens, g_ffn1, ffn1_gate, ffn1_up, ffn1_down, g_mix, w_in, mu_shift, w0, w_lora_w, a0, w_lora_a, w_lora_g, k_k, k_a, r_k, ln_x_w, ln_x_b, conv_w, w_out, g_ffn2, ffn2_gate, ffn2_up, ffn2_down, g_final):
    assert g_ffn1.shape[0] == 1, "single layer"
    nb, seq, _ = x_prompt.shape
    db, dseq, _ = x_sample.shape
    assert dseq == C_SAMPLE and N_META <= C_PROMPT and seq % ROWS_PREP == 0
    tp, ts = nb * seq, db * dseq
    assert tp % TM == 0 and ts % TM == 0

    row = lambda a: a.reshape(1, -1).astype(F32)
    w_in_p = _pack_w_in(jnp.transpose(w_in[0]))
    pad_rows = lambda w, n: jnp.concatenate([w, jnp.zeros((n - w.shape[0], w.shape[1]), w.dtype)], axis=0)
    hid = jnp.arange(G) // HEAD_DIM
    bd = (hid[:, None] == hid[None, :]).astype(BF16)
    prep_w = (_pack_rwkv(mu_shift[0])[None], row(w0[0]), pad_rows(w_lora_w[0], LORA_PAD).astype(BF16),
              row(a0[0]), pad_rows(w_lora_a[0], LORA_PAD).astype(BF16), w_lora_g[0].astype(BF16),
              row(k_k[0]), row(k_a[0]), row(r_k[0]), conv_w[0].astype(F32), bd)

    ffn1_w = (row(g_ffn1[0]), ffn1_gate[0].astype(BF16), ffn1_up[0].astype(BF16), ffn1_down[0].astype(BF16))
    x1 = _ffn([x_prompt.reshape(tp, D_MODEL), x_sample.reshape(ts, D_MODEL)], *ffn1_w, None, [tp + ts],
              name="ffn1")[0]
    proj = _proj(x1, row(g_mix[0]), w_in_p, name="proj")
    x_meta = jnp.concatenate([jnp.zeros((C_PROMPT - N_META, D_MODEL), F32), meta_tokens.astype(F32)], axis=0)
    x1_meta = _ffn([x_meta], *ffn1_w, None, [C_PROMPT], name="ffn1_meta", tm=C_PROMPT, tf=D_FF // 4)[0]
    proj_meta = _proj(x1_meta, row(g_mix[0]), w_in_p, name="proj_meta", tm=C_PROMPT)

    tri_p = _block_tri(ROWS_PREP, C_PROMPT)
    zeros_state = jnp.zeros((1, N_HEADS, HEAD_DIM, HEAD_DIM), F32)
    (ah, rh, bh, kh, vv, wc, _, _, _, utail_m, ptail_m) = _prep(
        proj_meta, prep_w, tri_p[:C_PROMPT, :C_PROMPT], n_seq=1, seq_len=C_PROMPT, rows=C_PROMPT,
        chunk=C_PROMPT, row_block_offset=0, extra=(jnp.zeros((8, RW), F32), jnp.zeros((8, G), F32)),
        sample=False, name="prep_meta")
    _, wkv_m = _scan(ah, rh, bh, kh, vv, wc, zeros_state, n_seq=1, seq_len=C_PROMPT, n_par=1,
                     rows=C_PROMPT, chunk=C_PROMPT, per_chunk_state=False, group=N_HEADS,
                     name="scan_meta")

    (ah, rh, bh, kh, vv, wc, g_p, bonus_p, conv_p, utail_p, ptail_p) = _prep(
        proj, prep_w, tri_p, n_seq=nb, seq_len=seq, rows=ROWS_PREP, chunk=C_PROMPT,
        row_block_offset=0, extra=(ptail_m[0], utail_m[0]), sample=False, name="prep_prompt")
    y_p, wkv_p = _scan(ah, rh, bh, kh, vv, wc, wkv_m, n_seq=nb, seq_len=seq, n_par=SCAN_PAR,
                       rows=C_PROMPT, chunk=C_PROMPT, per_chunk_state=False,
                       group=SCAN_PAR * N_HEADS // 2, name="scan_prompt")

    prev = _pack_rwkv(state_shift[0])
    up1 = state_conv[0][:, 1]
    up2 = state_conv[0][:, 0]
    (ah, rh, bh, kh, vv, wc, g_s, bonus_s, conv_s, u_s, plast_s) = _prep(
        proj, prep_w, _block_tri(ROWS_PREP, C_SAMPLE), n_seq=db, seq_len=dseq, rows=ROWS_PREP,
        chunk=C_SAMPLE, row_block_offset=tp // ROWS_PREP, extra=(prev, up1, up2), sample=True,
        name="prep_sample")
    pair = (db, N_HEADS // 2, 2, HEAD_DIM, HEAD_DIM)
    s0_pairs = jnp.transpose(state_wkv[0].reshape(pair), (0, 1, 3, 2, 4)).reshape(
        db, N_HEADS // 2, HEAD_DIM, 2 * HEAD_DIM)
    y_s, wkv_s = _scan(ah, rh, bh, kh, vv, wc, s0_pairs, n_seq=db, seq_len=dseq, n_par=1,
                       rows=ROWS_SCAN_SAMPLE, chunk=C_SAMPLE, per_chunk_state=True, group=4,
                       name="scan_sample")

    x2 = _mix([(y_p, bonus_p, g_p, conv_p), (y_s, bonus_s, g_s, conv_s)], x1,
              row(ln_x_w[0]), row(ln_x_b[0]), bd, w_out[0].astype(BF16))
    y_prompt, y_sample = _ffn([x2], row(g_ffn2[0]), ffn2_gate[0].astype(BF16), ffn2_up[0].astype(BF16),
                              ffn2_down[0].astype(BF16), row(g_final), [tp, ts], name="ffn2")

    shift_p = _unpack_rwkv(ptail_p.reshape(nb, seq // ROWS_PREP, 8, RW)[:, -1, 7, :])
    conv_state_p = utail_p.reshape(nb, seq // ROWS_PREP, 8, G)[:, -1, 6:, :]
    shift_s = _unpack_rwkv(plast_s)
    conv_state_s = u_s.reshape(db, dseq, G)[:, -2:, :]
    wkv_s_out = jnp.transpose(wkv_s.reshape(db, N_HEADS // 2, HEAD_DIM, 2, HEAD_DIM),
                              (0, 1, 3, 2, 4)).reshape(db, N_HEADS, HEAD_DIM, HEAD_DIM)
    return (y_prompt.reshape(nb, seq, D_MODEL), y_sample.reshape(db, dseq, D_MODEL),
            wkv_p[None].astype(state_wkv.dtype), shift_p[None].astype(state_shift.dtype),
            conv_state_p[None].astype(state_conv.dtype),
            wkv_s_out[None].astype(state_wkv.dtype), shift_s[None].astype(state_shift.dtype),
            conv_state_s[None].astype(state_conv.dtype))
```

```python
import functools

import jax
import jax.numpy as jnp
from jax import lax
from jax.experimental import pallas as pl
from jax.experimental.pallas import tpu as pltpu

F32 = jnp.float32
BF16 = jnp.bfloat16

D_MODEL = 2048
D_FF = 5632
N_META = 16
G = 1024
HEAD_DIM = 64
N_HEADS = G // HEAD_DIM
W_LORA = 96
A_LORA = 96
G_LORA = 256
LORA_PAD = 128
RWKV_PROJ = 3 * G + W_LORA + A_LORA + G_LORA
RWKV_SPLITS = (G, G + W_LORA, 2 * G + W_LORA, 3 * G + W_LORA, 3 * G + W_LORA + A_LORA)
RW = 3 * G + 2 * LORA_PAD + G_LORA
PW = RW + 3 * G
OFF_WD, OFF_AD, OFF_GD = 3 * G, 3 * G + LORA_PAD, 3 * G + 2 * LORA_PAD
RMS_EPS = 1e-6
GN_EPS = 64e-5
EXP_M05_LOG2E = 0.6065306597126334 * 1.4426950408889634

C_PROMPT = 64
C_SAMPLE = 8
TM = 512
TF = 512
TN = 3328
TM_MIX = 256
ROWS_PREP = 256
ROWS_SCAN_SAMPLE = 128
SCAN_PAR = 4
VMEM_LIMIT = 56 * 1024 * 1024


def _dot(a, b):
    return jnp.dot(a, b, preferred_element_type=F32)


def _dot_nt(a, b):
    return lax.dot_general(a, b, (((1,), (1,)), ((), ())), preferred_element_type=F32)


def _dot_tn(a, b):
    return lax.dot_general(a, b, (((0,), (0,)), ((), ())), preferred_element_type=F32)


def _split(a):
    hi = a.astype(BF16)
    lo = (a - hi.astype(F32)).astype(BF16)
    return hi, lo


def _head_sum(x, bd):
    return _dot(x.astype(BF16), bd)


def _dot_split_rhs(a_bf16, b):
    hi, lo = _split(b)
    return _dot(a_bf16, hi) + _dot(a_bf16, lo)


def _rms(x, g):
    return x * lax.rsqrt(jnp.mean(x * x, axis=-1, keepdims=True) + RMS_EPS) * g


def _segments(arrays, tile):
    segs, off = [], 0
    for arr in arrays:
        n = arr.shape[0] // tile
        assert n * tile == arr.shape[0]
        segs.append((off, n))
        off += n
    return segs, off


def _seg_spec(tile, width, seg):
    off, n = seg
    return pl.BlockSpec((tile, width), lambda i, j: (jnp.clip(i - off, 0, n - 1), 0))


def _overlaps(in_segs, out_segs):
    for a, (ao, an) in enumerate(in_segs):
        for b, (bo, bn) in enumerate(out_segs):
            lo, hi = max(ao, bo), min(ao + an, bo + bn)
            if lo < hi:
                yield a, b, lo, hi


def _ffn_kernel(*refs, in_segs, out_segs, final_norm):
    n_in, n_out = len(in_segs), len(out_segs)
    x_refs = refs[:n_in]
    g_ref, wg_ref, wu_ref, wd_ref = refs[n_in:n_in + 4]
    rest = refs[n_in + 4:]
    if final_norm:
        gf_ref, rest = rest[0], rest[1:]
    o_refs = rest[:n_out]
    h_ref, acc_ref = rest[n_out:]
    i = pl.program_id(0)
    j = pl.program_id(1)

    for x_ref, (off, n) in zip(x_refs, in_segs):
        @pl.when((j == 0) & (i >= off) & (i < off + n))
        def _(x_ref=x_ref):
            h_ref[...] = _rms(x_ref[...], g_ref[...]).astype(BF16)
            acc_ref[...] = jnp.zeros_like(acc_ref)

    h = h_ref[...]
    gate = _dot(h, wg_ref[...])
    up = _dot(h, wu_ref[...])
    act = (gate * jax.nn.sigmoid(gate) * up).astype(BF16)
    acc_ref[...] += _dot(act, wd_ref[...])

    for a, b, lo, hi in _overlaps(in_segs, out_segs):
        @pl.when((j == pl.num_programs(1) - 1) & (i >= lo) & (i < hi))
        def _(x_ref=x_refs[a], o_ref=o_refs[b]):
            xo = x_ref[...] + 0.5 * acc_ref[...]
            if final_norm:
                xo = _rms(xo, gf_ref[...])
            o_ref[...] = xo


def _ffn(xs, g, wg, wu, wd, g_final, out_rows, *, name, tm=TM, tf=TF):
    final_norm = g_final is not None
    in_segs, n_tiles_in = _segments(xs, tm)
    out_shape = [jax.ShapeDtypeStruct((n, D_MODEL), F32) for n in out_rows]
    out_segs, n_tiles = _segments(out_shape, tm)
    assert n_tiles <= n_tiles_in
    cmap = lambda i, j: (0, 0)
    in_specs = [_seg_spec(tm, D_MODEL, s) for s in in_segs] + [
        pl.BlockSpec((1, D_MODEL), cmap),
        pl.BlockSpec((D_MODEL, tf), lambda i, j: (0, j)),
        pl.BlockSpec((D_MODEL, tf), lambda i, j: (0, j)),
        pl.BlockSpec((tf, D_MODEL), lambda i, j: (j, 0)),
    ]
    args = list(xs) + [g, wg, wu, wd]
    if final_norm:
        in_specs.append(pl.BlockSpec((1, D_MODEL), cmap))
        args.append(g_final)
    return pl.pallas_call(
        functools.partial(_ffn_kernel, in_segs=in_segs, out_segs=out_segs, final_norm=final_norm),
        grid=(n_tiles, D_FF // tf),
        in_specs=in_specs,
        out_specs=[_seg_spec(tm, D_MODEL, s) for s in out_segs],
        out_shape=out_shape,
        scratch_shapes=[pltpu.VMEM((tm, D_MODEL), BF16), pltpu.VMEM((tm, D_MODEL), F32)],
        compiler_params=pltpu.CompilerParams(
            dimension_semantics=("arbitrary", "arbitrary"), vmem_limit_bytes=VMEM_LIMIT),
        name=name,
    )(*args)


_PACK_MOVES = (
    (0, 0, G),
    (G, RWKV_SPLITS[1], G),
    (2 * G, RWKV_SPLITS[2], G),
    (OFF_WD, RWKV_SPLITS[0], W_LORA),
    (OFF_AD, RWKV_SPLITS[3], A_LORA),
    (OFF_GD, RWKV_SPLITS[4], G_LORA),
    (RW, RWKV_PROJ, 3 * G),
)


def _pack_w_in_kernel(w_ref, o_ref):
    for dst, src, n in _PACK_MOVES:
        o_ref[dst:dst + n, :] = w_ref[src:src + n, :].astype(BF16)
    for lo, hi in ((OFF_WD + W_LORA, OFF_AD), (OFF_AD + A_LORA, OFF_GD)):
        o_ref[lo:hi, :] = jnp.zeros((hi - lo, o_ref.shape[1]), BF16)


def _pack_w_in(w_t):
    cols = 512
    return pl.pallas_call(
        _pack_w_in_kernel,
        grid=(D_MODEL // cols,),
        in_specs=[pl.BlockSpec((w_t.shape[0], cols), lambda i: (0, i))],
        out_specs=pl.BlockSpec((PW, cols), lambda i: (0, i)),
        out_shape=jax.ShapeDtypeStruct((PW, D_MODEL), BF16),
        compiler_params=pltpu.CompilerParams(
            dimension_semantics=("parallel",), vmem_limit_bytes=VMEM_LIMIT),
        name="pack_w_in",
    )(w_t)


def _proj_kernel(x_ref, g_ref, w_ref, o_ref):
    h = _rms(x_ref[...], g_ref[...]).astype(BF16)
    o_ref[...] = _dot_nt(h, w_ref[...])


def _proj(x, g, w, *, name, tm=TM):
    t = x.shape[0]
    return pl.pallas_call(
        _proj_kernel,
        grid=(PW // TN, t // tm),
        in_specs=[
            pl.BlockSpec((tm, D_MODEL), lambda j, i: (i, 0)),
            pl.BlockSpec((1, D_MODEL), lambda j, i: (0, 0)),
            pl.BlockSpec((TN, D_MODEL), lambda j, i: (j, 0)),
        ],
        out_specs=pl.BlockSpec((tm, TN), lambda j, i: (i, j)),
        out_shape=jax.ShapeDtypeStruct((t, PW), F32),
        compiler_params=pltpu.CompilerParams(
            dimension_semantics=("parallel", "parallel"), vmem_limit_bytes=VMEM_LIMIT),
        name=name,
    )(x, g, w)


def _prep_kernel(*refs, rows, chunk, sample):
    if sample:
        (p_ref, prev_ref, up1_ref, up2_ref, mu_ref, w0_ref, ww_ref, a0_ref, wa_ref, wgl_ref,
         kk_ref, ka_ref, rk_ref, cw_ref, bd_ref, tri_ref,
         ah_ref, rh_ref, bh_ref, kh_ref, v_ref, wc_ref, g_ref, bonus_ref, conv_ref, u_ref,
         plast_ref) = refs
    else:
        (p_ref, cp0_ref, cu0_ref, mu_ref, w0_ref, ww_ref, a0_ref, wa_ref, wgl_ref,
         kk_ref, ka_ref, rk_ref, cw_ref, bd_ref, tri_ref,
         ah_ref, rh_ref, bh_ref, kh_ref, v_ref, wc_ref, g_ref, bonus_ref, conv_ref, u_ref,
         ptail_ref, carry_p, carry_u) = refs

        @pl.when(pl.program_id(1) == 0)
        def _():
            carry_p[...] = cp0_ref[...]
            carry_u[...] = cu0_ref[...]

    def row_ids(width):
        r = lax.broadcasted_iota(jnp.int32, (rows, width), 0)
        return (r & 7) if sample else r

    def per_sequence(x):
        n, w = x.shape
        return jnp.broadcast_to(x[:, None, :], (n, 8, w)).reshape(n * 8, w)

    def shifted_mix(lo, hi):
        p = p_ref[:, lo:hi]
        rolled = pltpu.roll(p, 1, 0)
        if sample:
            sh = jnp.where(row_ids(hi - lo) == 0, per_sequence(prev_ref[:, lo:hi]), rolled)
        else:
            sh = jnp.where(row_ids(hi - lo) == 0, carry_p[7:8, lo:hi], rolled)
        return p + (sh - p) * mu_ref[:, lo:hi]

    r = shifted_mix(0, G)
    k = shifted_mix(G, 2 * G)
    v = shifted_mix(2 * G, 3 * G)
    wd = shifted_mix(OFF_WD, OFF_WD + LORA_PAD)
    ad = shifted_mix(OFF_AD, OFF_AD + LORA_PAD)
    gd = shifted_mix(OFF_GD, OFF_GD + G_LORA)

    z = w0_ref[...] + _dot(jnp.tanh(wd).astype(BF16), ww_ref[...])
    lw = -EXP_M05_LOG2E * jax.nn.sigmoid(z)
    a = jax.nn.sigmoid(a0_ref[...] + _dot(ad.astype(BF16), wa_ref[...]))
    g_ref[...] = _dot(jax.nn.sigmoid(gd).astype(BF16), wgl_ref[...]).astype(BF16)

    bd = bd_ref[...]
    kk = k * kk_ref[...]
    norm = jnp.sqrt(_head_sum(kk * kk, bd))
    kk = kk / jnp.maximum(norm, 1e-12)
    km = k * (1.0 + (a - 1.0) * ka_ref[...])
    bonus_ref[...] = _head_sum(r * km * rk_ref[...], bd) * v

    cum = _dot_split_rhs(tri_ref[...], lw)
    e_cum = jnp.exp2(cum)
    e_inv = jnp.exp2(-cum)
    ah_ref[...] = (-kk * jnp.exp2(cum - lw)).astype(BF16)
    rh_ref[...] = (r * e_cum).astype(BF16)
    bh_ref[...] = (kk * a * e_inv).astype(BF16)
    kh_ref[...] = (km * e_inv).astype(BF16)
    v_ref[...] = v.astype(BF16)
    for c in range(rows // chunk):
        wc_ref[c] = e_cum[(c + 1) * chunk - 1:(c + 1) * chunk, :]

    bg = p_ref[:, RW:RW + G]
    u = p_ref[:, RW + G:RW + 2 * G] * p_ref[:, RW + 2 * G:RW + 3 * G]
    u1 = pltpu.roll(u, 1, 0)
    u2 = pltpu.roll(u, 2, 0)
    rid = row_ids(G)
    if sample:
        s1 = per_sequence(up1_ref[...])
        um1 = jnp.where(rid == 0, s1, u1)
        um2 = jnp.where(rid == 0, per_sequence(up2_ref[...]), jnp.where(rid == 1, s1, u2))
        u_ref[...] = u
        plast_ref[...] = p_ref[:, 0:RW].reshape(rows // 8, 8, RW)[:, 7, :]
    else:
        um1 = jnp.where(rid == 0, carry_u[7:8, :], u1)
        um2 = jnp.where(rid == 0, carry_u[6:7, :], jnp.where(rid == 1, carry_u[7:8, :], u2))
        u_ref[0] = u[rows - 8:, :]
        ptail_ref[0] = p_ref[rows - 8:rows, 0:RW]
    conv = cw_ref[0:1, :] * um2 + cw_ref[1:2, :] * um1 + cw_ref[2:3, :] * u
    conv_ref[...] = (bg * conv).astype(BF16)

    if not sample:
        carry_p[...] = p_ref[rows - 8:rows, 0:RW]
        carry_u[...] = u[rows - 8:, :]


def _prep(proj, weights, tri, *, n_seq, seq_len, rows, chunk, row_block_offset, extra, sample, name):
    t = n_seq * seq_len
    n_chunks = t // chunk
    cpt = rows // chunk
    if sample:
        grid = (t // rows,)
        rmap = lambda i: (i + row_block_offset, 0)
        omap = lambda i: (i, 0)
        omap3 = lambda i: (i, 0, 0)
        cmap = lambda i: (0, 0)
        sem = ("parallel",)
    else:
        tiles = seq_len // rows
        grid = (n_seq, tiles)
        rmap = lambda b, j: (b * tiles + j + row_block_offset, 0)
        omap = lambda b, j: (b * tiles + j, 0)
        omap3 = lambda b, j: (b * tiles + j, 0, 0)
        cmap = lambda b, j: (0, 0)
        sem = ("parallel", "arbitrary")

    const = lambda arr: pl.BlockSpec(arr.shape, cmap)
    in_specs = [pl.BlockSpec((rows, PW), rmap)]
    if sample:
        in_specs += [pl.BlockSpec((rows // 8, RW), omap), pl.BlockSpec((rows // 8, G), omap),
                     pl.BlockSpec((rows // 8, G), omap)]
    else:
        in_specs += [const(e) for e in extra]
    in_specs += [const(w) for w in weights] + [const(tri)]
    args = [proj] + list(extra) + list(weights) + [tri]

    row_out = jax.ShapeDtypeStruct((t, G), F32)
    row_spec = pl.BlockSpec((rows, G), omap)
    row_bf16 = jax.ShapeDtypeStruct((t, G), BF16)
    out_shape = [row_bf16] * 5 + [jax.ShapeDtypeStruct((n_chunks, 1, G), F32), row_bf16, row_out,
                                  row_bf16]
    out_specs = [row_spec] * 5 + [pl.BlockSpec((cpt, 1, G), omap3), row_spec, row_spec, row_spec]
    if sample:
        out_shape += [row_out, jax.ShapeDtypeStruct((t // 8, RW), F32)]
        out_specs += [row_spec, pl.BlockSpec((rows // 8, RW), omap)]
        scratch = []
    else:
        out_shape += [jax.ShapeDtypeStruct((t // rows, 8, G), F32),
                      jax.ShapeDtypeStruct((t // rows, 8, RW), F32)]
        out_specs += [pl.BlockSpec((1, 8, G), omap3), pl.BlockSpec((1, 8, RW), omap3)]
        scratch = [pltpu.VMEM((8, RW), F32), pltpu.VMEM((8, G), F32)]
    return pl.pallas_call(
        functools.partial(_prep_kernel, rows=rows, chunk=chunk, sample=sample),
        grid=grid, in_specs=in_specs, out_specs=out_specs, out_shape=out_shape,
        scratch_shapes=scratch,
        compiler_params=pltpu.CompilerParams(dimension_semantics=sem, vmem_limit_bytes=VMEM_LIMIT),
        name=name,
    )(*args)


def _scan_kernel(*refs, n_par, rows, chunk, per_chunk_state, group, n_cast):
    ah_ref, rh_ref, bh_ref, kh_ref, v_ref, wc_ref, s0_ref = refs[:7]
    y_ref, s_ref = refs[7 + n_cast:9 + n_cast]
    for w_ref, o_ref in zip(refs[7:7 + n_cast], refs[9 + n_cast:]):
        o_ref[...] = w_ref[...].astype(BF16)

    if per_chunk_state:
        s_in = s0_ref
    else:
        s_in = s_ref

        @pl.when(pl.program_id(1) == 0)
        def _():
            for q in range(n_par):
                s_ref[q] = s0_ref[0]

    n_blk = rows // chunk
    lanes = [slice(h * HEAD_DIM, (h + 1) * HEAD_DIM) for h in range(N_HEADS)]
    bf = lambda x: x.astype(BF16)

    shift = chunk.bit_length() - 1
    rs = lax.broadcasted_iota(jnp.int32, (rows, rows), 0)
    cs = lax.broadcasted_iota(jnp.int32, (rows, rows), 1)
    same = (rs >> shift) == (cs >> shift)
    mask_strict = same & (rs > cs)
    ri = lax.broadcasted_iota(jnp.int32, (rows, 2 * rows), 0)
    ci = lax.broadcasted_iota(jnp.int32, (rows, 2 * rows), 1)
    cj = jnp.where(ci >= rows, ci - rows, ci)
    mask_r = ((ri >> shift) == (cj >> shift)) & (ri >= cj)
    eye = (rs == cs).astype(F32)

    def chunk_rows(x, c):
        return x[c * chunk:(c + 1) * chunk, :]

    pair_lane = lax.broadcasted_iota(jnp.int32, (rows, 2 * HEAD_DIM), 1) >> (HEAD_DIM.bit_length() - 1)

    def pair_masked(ref, q, h):
        hp, hh = divmod(h, 2)
        x2 = ref[q, :, 2 * HEAD_DIM * hp:2 * HEAD_DIM * (hp + 1)].astype(F32)
        return jnp.where(pair_lane == hh, x2, 0.0)

    def unit_group(units):
        per_unit = lambda f: {u: f(*u) for u in units}
        v = per_unit(lambda q, h: v_ref[q, :, lanes[h]])
        if n_blk == 1:
            a = per_unit(lambda q, h: ah_ref[q, :, lanes[h]])
            r = per_unit(lambda q, h: rh_ref[q, :, lanes[h]])
            b = per_unit(lambda q, h: bh_ref[q, :, lanes[h]])
            k = per_unit(lambda q, h: kh_ref[q, :, lanes[h]])
        else:
            a = per_unit(lambda q, h: pair_masked(ah_ref, q, h))
            r = per_unit(lambda q, h: pair_masked(rh_ref, q, h))
            b = per_unit(lambda q, h: pair_masked(bh_ref, q, h))
            k = per_unit(lambda q, h: pair_masked(kh_ref, q, h))
            v32 = {u: v[u].astype(F32) for u in units}
        ar = {u: bf(jnp.concatenate([a[u], r[u]], axis=0)) for u in units}
        bk = {u: bf(jnp.concatenate([b[u], k[u]], axis=0)) for u in units}

        gram = {u: _dot(ar[u], jnp.transpose(bk[u])) for u in units}
        l_ab = {u: jnp.where(mask_strict, gram[u][:rows, :rows], 0.0) for u in units}
        l_ak = {u: bf(jnp.where(mask_strict, gram[u][:rows, rows:], 0.0)) for u in units}
        l_r = {u: bf(jnp.where(mask_r, gram[u][rows:, :], 0.0)) for u in units}

        xs, ys = {}, {}
        for u in units:
            q, h = u
            if n_blk == 1:
                st = _dot_nt(ar[u], bf(s_in[q, h]))
                xs[u], ys[u] = st[:rows], st[rows:]
            else:
                parts = [_dot_nt(jnp.concatenate([chunk_rows(a[u], c), chunk_rows(r[u], c)], axis=0),
                                 s_in[c, h // 2]) for c in range(n_blk)]
                xs[u] = jnp.concatenate([p[:chunk] for p in parts], axis=0)
                ys[u] = jnp.concatenate([p[chunk:] for p in parts], axis=0)

        x = {u: xs[u] + _dot(l_ak[u], v[u]) for u in units}

        t_inv = {u: eye + l_ab[u] for u in units}
        n = 2
        if n < chunk:
            pw = {u: bf(l_ab[u]) for u in units}
            pw = {u: bf(_dot(pw[u], pw[u])) for u in units}
        while n < chunk:
            if 2 * n < chunk:
                both = {u: _dot(jnp.concatenate([bf(t_inv[u]), pw[u]], axis=0), pw[u]) for u in units}
                t_inv = {u: t_inv[u] + both[u][:rows] for u in units}
                pw = {u: bf(both[u][rows:]) for u in units}
            else:
                t_inv = {u: t_inv[u] + _dot(bf(t_inv[u]), pw[u]) for u in units}
            n *= 2

        uu = {u: _dot(bf(t_inv[u]), bf(x[u])) for u in units}
        uv = {u: jnp.concatenate([bf(uu[u]), v[u]], axis=0) for u in units}
        for u in units:
            q, h = u
            y_ref[q, :, lanes[h]] = ys[u] + _dot(l_r[u], uv[u])

        if n_blk == 1:
            for u in units:
                q, h = u
                upd = _dot_tn(uv[u], bk[u])
                s_ref[q, h] = (s_in[q, h] + upd) * wc_ref[q, 0][:, lanes[h]]
        else:
            def update(u, c):
                uv_c = jnp.concatenate([chunk_rows(uu[u], c), chunk_rows(v32[u], c)], axis=0)
                bk_c = jnp.concatenate([chunk_rows(b[u], c), chunk_rows(k[u], c)], axis=0)
                return _dot_tn(uv_c, bk_c)

            for q, h in units:
                if h % 2 == 0:
                    assert (q, h + 1) in units
                    hp = h // 2
                    for c in range(n_blk):
                        s_new = s_in[c, hp] + update((q, h), c) + update((q, h + 1), c)
                        s_ref[c, hp] = s_new * wc_ref[q, c][:, 2 * HEAD_DIM * hp:2 * HEAD_DIM * (hp + 1)]

    all_units = [(q, h) for h in range(N_HEADS) for q in range(n_par)]
    for g0 in range(0, len(all_units), group):
        unit_group(all_units[g0:g0 + group])


def _scan(ah, rh, bh, kh, v, wc, s0, *, n_seq, seq_len, n_par, rows, chunk, per_chunk_state, group,
          name, cast=()):
    t = n_seq * seq_len
    n_blk = rows // chunk
    if per_chunk_state:
        assert seq_len == chunk and n_par == 1
        lead = 1
        grid = (t // rows,)
        rmap = lambda i: (0, i, 0)
        cmap = lambda i: (0, i, 0, 0)
        smap = lambda i: (i, 0, 0, 0)
        s0map = smap
        state_block = (n_blk, N_HEADS // 2, HEAD_DIM, 2 * HEAD_DIM)
        s0_block = state_block
        wc_block = (1, n_blk, 1, G)
        sem = ("parallel",)
    else:
        assert rows == chunk and s0.shape[0] == 1 and n_seq % n_par == 0
        lead = n_seq
        grid = (n_seq // n_par, seq_len // chunk)
        rmap = lambda s, c: (s, c, 0)
        cmap = lambda s, c: (s, c, 0, 0)
        smap = lambda s, c: (s, 0, 0, 0)
        s0map = lambda s, c: (0, 0, 0, 0)
        state_block = (n_par, N_HEADS, HEAD_DIM, HEAD_DIM)
        s0_block = (1, N_HEADS, HEAD_DIM, HEAD_DIM)
        wc_block = (n_par, 1, 1, G)
        sem = ("parallel", "arbitrary")
    rows3 = lambda x: x.reshape(lead, t // lead, G)
    row_spec = pl.BlockSpec((n_par, rows, G), rmap)
    n_steps = 1
    for extent in grid:
        n_steps *= extent
    step = (lambda i: i) if len(grid) == 1 else (lambda s, c: s * grid[1] + c)
    cast_specs = []
    for w in cast:
        assert w.shape[0] % (16 * n_steps) == 0
        cast_specs.append(pl.BlockSpec((w.shape[0] // n_steps, w.shape[1]), lambda *g: (step(*g), 0)))
    y, s_out, *narrowed = pl.pallas_call(
        functools.partial(_scan_kernel, n_par=n_par, rows=rows, chunk=chunk,
                          per_chunk_state=per_chunk_state, group=group, n_cast=len(cast)),
        grid=grid,
        in_specs=([row_spec] * 5 + [pl.BlockSpec(wc_block, cmap), pl.BlockSpec(s0_block, s0map)]
                  + cast_specs),
        out_specs=[row_spec, pl.BlockSpec(state_block, smap)] + cast_specs,
        out_shape=[jax.ShapeDtypeStruct((lead, t // lead, G), F32),
                   jax.ShapeDtypeStruct((n_seq,) + state_block[1:], F32)]
                  + [jax.ShapeDtypeStruct(w.shape, BF16) for w in cast],
        compiler_params=pltpu.CompilerParams(dimension_semantics=sem, vmem_limit_bytes=VMEM_LIMIT),
        name=name,
    )(rows3(ah), rows3(rh), rows3(bh), rows3(kh), rows3(v), wc.reshape(lead, -1, 1, G), s0, *cast)
    return (y.reshape(t, G), s_out, *narrowed)


def _mix_kernel(*refs, segs):
    n = len(segs)
    row_refs = [refs[4 * s:4 * s + 4] for s in range(n)]
    x_ref, lnw_ref, lnb_ref, bd_ref, wo_ref, o_ref = refs[4 * n:]
    i = pl.program_id(0)

    def body(y_ref, bonus_ref, g_ref, conv_ref):
        bd = bd_ref[...]
        y = y_ref[...]
        mu = _head_sum(y, bd) * (1.0 / HEAD_DIM)
        d = y - mu
        var = _head_sum(d * d, bd) * (1.0 / HEAD_DIM)
        yn = d * lax.rsqrt(var + GN_EPS) * lnw_ref[...] + lnb_ref[...]
        rw = ((yn + bonus_ref[...]) * g_ref[...]).astype(BF16)
        mix = _dot(rw, wo_ref[0:G, :]) + _dot(conv_ref[...], wo_ref[G:2 * G, :])
        o_ref[...] = x_ref[...] + mix

    for rr, (off, cnt) in zip(row_refs, segs):
        @pl.when((i >= off) & (i < off + cnt))
        def _(rr=rr):
            body(*rr)


def _mix(row_groups, x1, lnw, lnb, bd, wo):
    segs, n_tiles = _segments([grp[0] for grp in row_groups], TM_MIX)
    cmap = lambda i: (0, 0)
    in_specs, args = [], []
    for grp, (off, cnt) in zip(row_groups, segs):
        smap = lambda i, off=off, cnt=cnt: (jnp.clip(i - off, 0, cnt - 1), 0)
        in_specs += [pl.BlockSpec((TM_MIX, G), smap)] * 4
        args += list(grp)
    in_specs += [pl.BlockSpec((TM_MIX, D_MODEL), lambda i: (i, 0)),
                 pl.BlockSpec((1, G), cmap), pl.BlockSpec((1, G), cmap),
                 pl.BlockSpec((G, G), cmap), pl.BlockSpec((D_MODEL, D_MODEL), cmap)]
    return pl.pallas_call(
        functools.partial(_mix_kernel, segs=segs),
        grid=(n_tiles,), in_specs=in_specs,
        out_specs=pl.BlockSpec((TM_MIX, D_MODEL), lambda i: (i, 0)),
        out_shape=jax.ShapeDtypeStruct((n_tiles * TM_MIX, D_MODEL), F32),
        compiler_params=pltpu.CompilerParams(
            dimension_semantics=("parallel",), vmem_limit_bytes=VMEM_LIMIT),
        name="mix",
    )(*args, x1, lnw, lnb, bd, wo)


def _pack_rwkv(a):
    r, wd, k, v, ad, gd = jnp.split(a, RWKV_SPLITS, axis=-1)
    zw = jnp.zeros(a.shape[:-1] + (LORA_PAD - W_LORA,), a.dtype)
    za = jnp.zeros(a.shape[:-1] + (LORA_PAD - A_LORA,), a.dtype)
    return jnp.concatenate([r, k, v, wd, zw, ad, za, gd], axis=-1)


def _unpack_rwkv(p):
    return jnp.concatenate([
        p[..., 0:G], p[..., OFF_WD:OFF_WD + W_LORA], p[..., G:2 * G], p[..., 2 * G:3 * G],
        p[..., OFF_AD:OFF_AD + A_LORA], p[..., OFF_GD:OFF_GD + G_LORA]], axis=-1)


def _block_tri(n, chunk):
    i = jnp.arange(n)
    return ((i[:, None] // chunk == i[None, :] // chunk) & (i[:, None] >= i[None, :])).astype(BF16)


def kernel(x_prompt, x_sample, state_wkv, state_shift, state_conv, meta_tokens, g_ffn1, ffn1_gate, ffn1_up, ffn1_down, g_mix, w_in, mu_shift, w0, w_lora_w, a0, w_lora_a, w_lora_g, k_k, k_a, r_k, ln_x_w, ln_x_b, conv_w, w_out, g_ffn2, ffn2_gate, ffn2_up, ffn2_down, g_final):
    assert g_ffn1.shape[0] == 1, "single layer"
    nb, seq, _ = x_prompt.shape
    db, dseq, _ = x_sample.shape
    assert dseq == C_SAMPLE and N_META <= C_PROMPT and seq % ROWS_PREP == 0
    tp, ts = nb * seq, db * dseq
    assert tp % TM == 0 and ts % TM == 0

    row = lambda a: a.reshape(1, -1).astype(F32)
    w_in_p = _pack_w_in(jnp.transpose(w_in[0]))
    pad_rows = lambda w, n: jnp.concatenate([w, jnp.zeros((n - w.shape[0], w.shape[1]), w.dtype)], axis=0)
    hid = jnp.arange(G) // HEAD_DIM
    bd = (hid[:, None] == hid[None, :]).astype(BF16)
    prep_w = (_pack_rwkv(mu_shift[0])[None], row(w0[0]), pad_rows(w_lora_w[0], LORA_PAD).astype(BF16),
              row(a0[0]), pad_rows(w_lora_a[0], LORA_PAD).astype(BF16), w_lora_g[0].astype(BF16),
              row(k_k[0]), row(k_a[0]), row(r_k[0]), conv_w[0].astype(F32), bd)

    ffn1_w = (row(g_ffn1[0]), ffn1_gate[0].astype(BF16), ffn1_up[0].astype(BF16), ffn1_down[0].astype(BF16))
    x1 = _ffn([x_prompt.reshape(tp, D_MODEL), x_sample.reshape(ts, D_MODEL)], *ffn1_w, None, [tp + ts],
              name="ffn1")[0]
    proj = _proj(x1, row(g_mix[0]), w_in_p, name="proj")
    x_meta = jnp.concatenate([jnp.zeros((C_PROMPT - N_META, D_MODEL), F32), meta_tokens.astype(F32)], axis=0)
    x1_meta = _ffn([x_meta], *ffn1_w, None, [C_PROMPT], name="ffn1_meta", tm=C_PROMPT, tf=D_FF // 4)[0]
    proj_meta = _proj(x1_meta, row(g_mix[0]), w_in_p, name="proj_meta", tm=C_PROMPT)

    tri_p = _block_tri(ROWS_PREP, C_PROMPT)
    zeros_state = jnp.zeros((1, N_HEADS, HEAD_DIM, HEAD_DIM), F32)
    (ah, rh, bh, kh, vv, wc, _, _, _, utail_m, ptail_m) = _prep(
        proj_meta, prep_w, tri_p[:C_PROMPT, :C_PROMPT], n_seq=1, seq_len=C_PROMPT, rows=C_PROMPT,
        chunk=C_PROMPT, row_block_offset=0, extra=(jnp.zeros((8, RW), F32), jnp.zeros((8, G), F32)),
        sample=False, name="prep_meta")
    _, wkv_m = _scan(ah, rh, bh, kh, vv, wc, zeros_state, n_seq=1, seq_len=C_PROMPT, n_par=1,
                     rows=C_PROMPT, chunk=C_PROMPT, per_chunk_state=False, group=N_HEADS,
                     name="scan_meta")

    (ah, rh, bh, kh, vv, wc, g_p, bonus_p, conv_p, utail_p, ptail_p) = _prep(
        proj, prep_w, tri_p, n_seq=nb, seq_len=seq, rows=ROWS_PREP, chunk=C_PROMPT,
        row_block_offset=0, extra=(ptail_m[0], utail_m[0]), sample=False, name="prep_prompt")
    y_p, wkv_p, wg2, wu2, wd2, wo = _scan(
        ah, rh, bh, kh, vv, wc, wkv_m, n_seq=nb, seq_len=seq, n_par=SCAN_PAR, rows=C_PROMPT,
        chunk=C_PROMPT, per_chunk_state=False, group=SCAN_PAR * N_HEADS // 2, name="scan_prompt",
        cast=(ffn2_gate[0], ffn2_up[0], ffn2_down[0], w_out[0]))

    prev = _pack_rwkv(state_shift[0])
    up1 = state_conv[0][:, 1]
    up2 = state_conv[0][:, 0]
    (ah, rh, bh, kh, vv, wc, g_s, bonus_s, conv_s, u_s, plast_s) = _prep(
        proj, prep_w, _block_tri(ROWS_PREP, C_SAMPLE), n_seq=db, seq_len=dseq, rows=ROWS_PREP,
        chunk=C_SAMPLE, row_block_offset=tp // ROWS_PREP, extra=(prev, up1, up2), sample=True,
        name="prep_sample")
    pair = (db, N_HEADS // 2, 2, HEAD_DIM, HEAD_DIM)
    s0_pairs = jnp.transpose(state_wkv[0].reshape(pair), (0, 1, 3, 2, 4)).reshape(
        db, N_HEADS // 2, HEAD_DIM, 2 * HEAD_DIM)
    y_s, wkv_s = _scan(ah, rh, bh, kh, vv, wc, s0_pairs, n_seq=db, seq_len=dseq, n_par=1,
                       rows=ROWS_SCAN_SAMPLE, chunk=C_SAMPLE, per_chunk_state=True, group=4,
                       name="scan_sample")

    x2 = _mix([(y_p, bonus_p, g_p, conv_p), (y_s, bonus_s, g_s, conv_s)], x1,
              row(ln_x_w[0]), row(ln_x_b[0]), bd, wo)
    y_prompt, y_sample = _ffn([x2], row(g_ffn2[0]), wg2, wu2, wd2, row(g_final), [tp, ts], name="ffn2")

    shift_p = _unpack_rwkv(ptail_p.reshape(nb, seq // ROWS_PREP, 8, RW)[:, -1, 7, :])
    conv_state_p = utail_p.reshape(nb, seq // ROWS_PREP, 8, G)[:, -1, 6:, :]
    shift_s = _unpack_rwkv(plast_s)
    conv_state_s = u_s.reshape(db, dseq, G)[:, -2:, :]
    wkv_s_out = jnp.transpose(wkv_s.reshape(db, N_HEADS // 2, HEAD_DIM, 2, HEAD_DIM),
                              (0, 1, 3, 2, 4)).reshape(db, N_HEADS, HEAD_DIM, HEAD_DIM)
    return (y_prompt.reshape(nb, seq, D_MODEL), y_sample.reshape(db, dseq, D_MODEL),
            wkv_p[None].astype(state_wkv.dtype), shift_p[None].astype(state_shift.dtype),
            conv_state_p[None].astype(state_conv.dtype),
            wkv_s_out[None].astype(state_wkv.dtype), shift_s[None].astype(state_shift.dtype),
            conv_state_s[None].astype(state_conv.dtype))
```

```python
import functools

import jax
import jax.numpy as jnp
from jax import lax
from jax.experimental import pallas as pl
from jax.experimental.pallas import tpu as pltpu

F32 = jnp.float32
BF16 = jnp.bfloat16

D_MODEL = 2048
D_FF = 5632
N_META = 16
G = 1024
HEAD_DIM = 64
N_HEADS = G // HEAD_DIM
W_LORA = 96
A_LORA = 96
G_LORA = 256
LORA_PAD = 128
RWKV_PROJ = 3 * G + W_LORA + A_LORA + G_LORA
RWKV_SPLITS = (G, G + W_LORA, 2 * G + W_LORA, 3 * G + W_LORA, 3 * G + W_LORA + A_LORA)
RW = 3 * G + 2 * LORA_PAD + G_LORA
PW = RW + 3 * G
OFF_WD, OFF_AD, OFF_GD = 3 * G, 3 * G + LORA_PAD, 3 * G + 2 * LORA_PAD
RMS_EPS = 1e-6
GN_EPS = 64e-5
EXP_M05_LOG2E = 0.6065306597126334 * 1.4426950408889634

C_PROMPT = 64
C_SAMPLE = 8
TM = 512
TF = 512
TN = 3328
TM_MIX = 256
ROWS_PREP = 256
ROWS_SCAN_SAMPLE = 128
SCAN_PAR = 4
VMEM_LIMIT = 56 * 1024 * 1024


def _dot(a, b):
    return jnp.dot(a, b, preferred_element_type=F32)


def _dot_nt(a, b):
    return lax.dot_general(a, b, (((1,), (1,)), ((), ())), preferred_element_type=F32)


def _dot_tn(a, b):
    return lax.dot_general(a, b, (((0,), (0,)), ((), ())), preferred_element_type=F32)


def _split(a):
    hi = a.astype(BF16)
    lo = (a - hi.astype(F32)).astype(BF16)
    return hi, lo


def _head_sum(x, bd):
    return _dot(x.astype(BF16), bd)


def _dot_split_rhs(a_bf16, b):
    hi, lo = _split(b)
    return _dot(a_bf16, hi) + _dot(a_bf16, lo)


def _rms(x, g):
    return x * lax.rsqrt(jnp.mean(x * x, axis=-1, keepdims=True) + RMS_EPS) * g


def _segments(arrays, tile):
    segs, off = [], 0
    for arr in arrays:
        n = arr.shape[0] // tile
        assert n * tile == arr.shape[0]
        segs.append((off, n))
        off += n
    return segs, off


def _seg_spec(tile, width, seg):
    off, n = seg
    return pl.BlockSpec((tile, width), lambda i, j: (jnp.clip(i - off, 0, n - 1), 0))


def _overlaps(in_segs, out_segs):
    for a, (ao, an) in enumerate(in_segs):
        for b, (bo, bn) in enumerate(out_segs):
            lo, hi = max(ao, bo), min(ao + an, bo + bn)
            if lo < hi:
                yield a, b, lo, hi


def _ffn_kernel(*refs, in_segs, out_segs, final_norm, narrow):
    n_in, n_out = len(in_segs), len(out_segs)
    x_refs = refs[:n_in]
    g_ref, wg_ref, wu_ref, wd_ref = refs[n_in:n_in + 4]
    rest = refs[n_in + 4:]
    if final_norm:
        gf_ref, rest = rest[0], rest[1:]
    o_refs = rest[:n_out]
    rest = rest[n_out:]
    if narrow:
        w_out_refs, rest = rest[:3], rest[3:]
    h_ref, acc_ref = rest
    i = pl.program_id(0)
    j = pl.program_id(1)

    for x_ref, (off, n) in zip(x_refs, in_segs):
        @pl.when((j == 0) & (i >= off) & (i < off + n))
        def _(x_ref=x_ref):
            h_ref[...] = _rms(x_ref[...], g_ref[...]).astype(BF16)
            acc_ref[...] = jnp.zeros_like(acc_ref)

    wg, wu, wd = wg_ref[...], wu_ref[...], wd_ref[...]
    if narrow:
        wg, wu, wd = wg.astype(BF16), wu.astype(BF16), wd.astype(BF16)
        for o_ref, w in zip(w_out_refs, (wg, wu, wd)):
            o_ref[...] = w
    h = h_ref[...]
    gate = _dot(h, wg)
    up = _dot(h, wu)
    act = (gate * jax.nn.sigmoid(gate) * up).astype(BF16)
    acc_ref[...] += _dot(act, wd)

    for a, b, lo, hi in _overlaps(in_segs, out_segs):
        @pl.when((j == pl.num_programs(1) - 1) & (i >= lo) & (i < hi))
        def _(x_ref=x_refs[a], o_ref=o_refs[b]):
            xo = x_ref[...] + 0.5 * acc_ref[...]
            if final_norm:
                xo = _rms(xo, gf_ref[...])
            o_ref[...] = xo


def _ffn(xs, g, wg, wu, wd, g_final, out_rows, *, name, tm=TM, tf=TF):
    final_norm = g_final is not None
    narrow = wg.dtype == F32
    in_segs, n_tiles_in = _segments(xs, tm)
    out_shape = [jax.ShapeDtypeStruct((n, D_MODEL), F32) for n in out_rows]
    out_segs, n_tiles = _segments(out_shape, tm)
    assert n_tiles <= n_tiles_in and (n_tiles == 1 or not narrow)
    cmap = lambda i, j: (0, 0)
    w_specs = [pl.BlockSpec((D_MODEL, tf), lambda i, j: (0, j)),
               pl.BlockSpec((D_MODEL, tf), lambda i, j: (0, j)),
               pl.BlockSpec((tf, D_MODEL), lambda i, j: (j, 0))]
    in_specs = [_seg_spec(tm, D_MODEL, s) for s in in_segs] + [pl.BlockSpec((1, D_MODEL), cmap)] + w_specs
    args = list(xs) + [g, wg, wu, wd]
    if final_norm:
        in_specs.append(pl.BlockSpec((1, D_MODEL), cmap))
        args.append(g_final)
    out_specs = [_seg_spec(tm, D_MODEL, s) for s in out_segs]
    if narrow:
        out_specs += w_specs
        out_shape = out_shape + [jax.ShapeDtypeStruct(w.shape, BF16) for w in (wg, wu, wd)]
    return pl.pallas_call(
        functools.partial(_ffn_kernel, in_segs=in_segs, out_segs=out_segs, final_norm=final_norm,
                          narrow=narrow),
        grid=(n_tiles, D_FF // tf),
        in_specs=in_specs,
        out_specs=out_specs,
        out_shape=out_shape,
        scratch_shapes=[pltpu.VMEM((tm, D_MODEL), BF16), pltpu.VMEM((tm, D_MODEL), F32)],
        compiler_params=pltpu.CompilerParams(
            dimension_semantics=("arbitrary", "arbitrary"), vmem_limit_bytes=VMEM_LIMIT),
        name=name,
    )(*args)


_PACK_MOVES = (
    (0, 0, G),
    (G, RWKV_SPLITS[1], G),
    (2 * G, RWKV_SPLITS[2], G),
    (OFF_WD, RWKV_SPLITS[0], W_LORA),
    (OFF_AD, RWKV_SPLITS[3], A_LORA),
    (OFF_GD, RWKV_SPLITS[4], G_LORA),
    (RW, RWKV_PROJ, 3 * G),
)


def _pack_w_in_kernel(w_ref, o_ref):
    for dst, src, n in _PACK_MOVES:
        o_ref[dst:dst + n, :] = w_ref[src:src + n, :].astype(BF16)
    for lo, hi in ((OFF_WD + W_LORA, OFF_AD), (OFF_AD + A_LORA, OFF_GD)):
        o_ref[lo:hi, :] = jnp.zeros((hi - lo, o_ref.shape[1]), BF16)


def _pack_w_in(w_t):
    cols = 512
    return pl.pallas_call(
        _pack_w_in_kernel,
        grid=(D_MODEL // cols,),
        in_specs=[pl.BlockSpec((w_t.shape[0], cols), lambda i: (0, i))],
        out_specs=pl.BlockSpec((PW, cols), lambda i: (0, i)),
        out_shape=jax.ShapeDtypeStruct((PW, D_MODEL), BF16),
        compiler_params=pltpu.CompilerParams(
            dimension_semantics=("parallel",), vmem_limit_bytes=VMEM_LIMIT),
        name="pack_w_in",
    )(w_t)


def _proj_kernel(x_ref, g_ref, w_ref, o_ref):
    h = _rms(x_ref[...], g_ref[...]).astype(BF16)
    o_ref[...] = _dot_nt(h, w_ref[...])


def _proj(x, g, w, *, name, tm=TM):
    t = x.shape[0]
    return pl.pallas_call(
        _proj_kernel,
        grid=(PW // TN, t // tm),
        in_specs=[
            pl.BlockSpec((tm, D_MODEL), lambda j, i: (i, 0)),
            pl.BlockSpec((1, D_MODEL), lambda j, i: (0, 0)),
            pl.BlockSpec((TN, D_MODEL), lambda j, i: (j, 0)),
        ],
        out_specs=pl.BlockSpec((tm, TN), lambda j, i: (i, j)),
        out_shape=jax.ShapeDtypeStruct((t, PW), F32),
        compiler_params=pltpu.CompilerParams(
            dimension_semantics=("parallel", "parallel"), vmem_limit_bytes=VMEM_LIMIT),
        name=name,
    )(x, g, w)


def _prep_kernel(*refs, rows, chunk, sample):
    if sample:
        (p_ref, prev_ref, up1_ref, up2_ref, mu_ref, w0_ref, ww_ref, a0_ref, wa_ref, wgl_ref,
         kk_ref, ka_ref, rk_ref, cw_ref, bd_ref, tri_ref,
         ah_ref, rh_ref, bh_ref, kh_ref, v_ref, wc_ref, g_ref, bonus_ref, conv_ref, u_ref,
         plast_ref) = refs
    else:
        (p_ref, cp0_ref, cu0_ref, mu_ref, w0_ref, ww_ref, a0_ref, wa_ref, wgl_ref,
         kk_ref, ka_ref, rk_ref, cw_ref, bd_ref, tri_ref,
         ah_ref, rh_ref, bh_ref, kh_ref, v_ref, wc_ref, g_ref, bonus_ref, conv_ref, u_ref,
         ptail_ref, carry_p, carry_u) = refs

        @pl.when(pl.program_id(1) == 0)
        def _():
            carry_p[...] = cp0_ref[...]
            carry_u[...] = cu0_ref[...]

    def row_ids(width):
        r = lax.broadcasted_iota(jnp.int32, (rows, width), 0)
        return (r & 7) if sample else r

    def per_sequence(x):
        n, w = x.shape
        return jnp.broadcast_to(x[:, None, :], (n, 8, w)).reshape(n * 8, w)

    def shifted_mix(lo, hi):
        p = p_ref[:, lo:hi]
        rolled = pltpu.roll(p, 1, 0)
        if sample:
            sh = jnp.where(row_ids(hi - lo) == 0, per_sequence(prev_ref[:, lo:hi]), rolled)
        else:
            sh = jnp.where(row_ids(hi - lo) == 0, carry_p[7:8, lo:hi], rolled)
        return p + (sh - p) * mu_ref[:, lo:hi]

    r = shifted_mix(0, G)
    k = shifted_mix(G, 2 * G)
    v = shifted_mix(2 * G, 3 * G)
    wd = shifted_mix(OFF_WD, OFF_WD + LORA_PAD)
    ad = shifted_mix(OFF_AD, OFF_AD + LORA_PAD)
    gd = shifted_mix(OFF_GD, OFF_GD + G_LORA)

    z = w0_ref[...] + _dot(jnp.tanh(wd).astype(BF16), ww_ref[...])
    lw = -EXP_M05_LOG2E * jax.nn.sigmoid(z)
    a = jax.nn.sigmoid(a0_ref[...] + _dot(ad.astype(BF16), wa_ref[...]))
    g_ref[...] = _dot(jax.nn.sigmoid(gd).astype(BF16), wgl_ref[...]).astype(BF16)

    bd = bd_ref[...]
    kk = k * kk_ref[...]
    norm = jnp.sqrt(_head_sum(kk * kk, bd))
    kk = kk / jnp.maximum(norm, 1e-12)
    km = k * (1.0 + (a - 1.0) * ka_ref[...])
    bonus_ref[...] = _head_sum(r * km * rk_ref[...], bd) * v

    cum = _dot_split_rhs(tri_ref[...], lw)
    e_cum = jnp.exp2(cum)
    e_inv = jnp.exp2(-cum)
    ah_ref[...] = (-kk * jnp.exp2(cum - lw)).astype(BF16)
    rh_ref[...] = (r * e_cum).astype(BF16)
    bh_ref[...] = (kk * a * e_inv).astype(BF16)
    kh_ref[...] = (km * e_inv).astype(BF16)
    v_ref[...] = v.astype(BF16)
    for c in range(rows // chunk):
        wc_ref[c] = e_cum[(c + 1) * chunk - 1:(c + 1) * chunk, :]

    bg = p_ref[:, RW:RW + G]
    u = p_ref[:, RW + G:RW + 2 * G] * p_ref[:, RW + 2 * G:RW + 3 * G]
    u1 = pltpu.roll(u, 1, 0)
    u2 = pltpu.roll(u, 2, 0)
    rid = row_ids(G)
    if sample:
        s1 = per_sequence(up1_ref[...])
        um1 = jnp.where(rid == 0, s1, u1)
        um2 = jnp.where(rid == 0, per_sequence(up2_ref[...]), jnp.where(rid == 1, s1, u2))
        u_ref[...] = u
        plast_ref[...] = p_ref[:, 0:RW].reshape(rows // 8, 8, RW)[:, 7, :]
    else:
        um1 = jnp.where(rid == 0, carry_u[7:8, :], u1)
        um2 = jnp.where(rid == 0, carry_u[6:7, :], jnp.where(rid == 1, carry_u[7:8, :], u2))
        u_ref[0] = u[rows - 8:, :]
        ptail_ref[0] = p_ref[rows - 8:rows, 0:RW]
    conv = cw_ref[0:1, :] * um2 + cw_ref[1:2, :] * um1 + cw_ref[2:3, :] * u
    conv_ref[...] = (bg * conv).astype(BF16)

    if not sample:
        carry_p[...] = p_ref[rows - 8:rows, 0:RW]
        carry_u[...] = u[rows - 8:, :]


def _prep(proj, weights, tri, *, n_seq, seq_len, rows, chunk, row_block_offset, extra, sample, name):
    t = n_seq * seq_len
    n_chunks = t // chunk
    cpt = rows // chunk
    if sample:
        grid = (t // rows,)
        rmap = lambda i: (i + row_block_offset, 0)
        omap = lambda i: (i, 0)
        omap3 = lambda i: (i, 0, 0)
        cmap = lambda i: (0, 0)
        sem = ("parallel",)
    else:
        tiles = seq_len // rows
        grid = (n_seq, tiles)
        rmap = lambda b, j: (b * tiles + j + row_block_offset, 0)
        omap = lambda b, j: (b * tiles + j, 0)
        omap3 = lambda b, j: (b * tiles + j, 0, 0)
        cmap = lambda b, j: (0, 0)
        sem = ("parallel", "arbitrary")

    const = lambda arr: pl.BlockSpec(arr.shape, cmap)
    in_specs = [pl.BlockSpec((rows, PW), rmap)]
    if sample:
        in_specs += [pl.BlockSpec((rows // 8, RW), omap), pl.BlockSpec((rows // 8, G), omap),
                     pl.BlockSpec((rows // 8, G), omap)]
    else:
        in_specs += [const(e) for e in extra]
    in_specs += [const(w) for w in weights] + [const(tri)]
    args = [proj] + list(extra) + list(weights) + [tri]

    row_out = jax.ShapeDtypeStruct((t, G), F32)
    row_spec = pl.BlockSpec((rows, G), omap)
    row_bf16 = jax.ShapeDtypeStruct((t, G), BF16)
    out_shape = [row_bf16] * 5 + [jax.ShapeDtypeStruct((n_chunks, 1, G), F32), row_bf16, row_out,
                                  row_bf16]
    out_specs = [row_spec] * 5 + [pl.BlockSpec((cpt, 1, G), omap3), row_spec, row_spec, row_spec]
    if sample:
        out_shape += [row_out, jax.ShapeDtypeStruct((t // 8, RW), F32)]
        out_specs += [row_spec, pl.BlockSpec((rows // 8, RW), omap)]
        scratch = []
    else:
        out_shape += [jax.ShapeDtypeStruct((t // rows, 8, G), F32),
                      jax.ShapeDtypeStruct((t // rows, 8, RW), F32)]
        out_specs += [pl.BlockSpec((1, 8, G), omap3), pl.BlockSpec((1, 8, RW), omap3)]
        scratch = [pltpu.VMEM((8, RW), F32), pltpu.VMEM((8, G), F32)]
    return pl.pallas_call(
        functools.partial(_prep_kernel, rows=rows, chunk=chunk, sample=sample),
        grid=grid, in_specs=in_specs, out_specs=out_specs, out_shape=out_shape,
        scratch_shapes=scratch,
        compiler_params=pltpu.CompilerParams(dimension_semantics=sem, vmem_limit_bytes=VMEM_LIMIT),
        name=name,
    )(*args)


def _scan_kernel(*refs, n_par, rows, chunk, per_chunk_state, group, n_cast):
    ah_ref, rh_ref, bh_ref, kh_ref, v_ref, wc_ref, s0_ref = refs[:7]
    y_ref, s_ref = refs[7 + n_cast:9 + n_cast]
    for w_ref, o_ref in zip(refs[7:7 + n_cast], refs[9 + n_cast:]):
        o_ref[...] = w_ref[...].astype(BF16)

    if per_chunk_state:
        s_in = s0_ref
    else:
        s_in = s_ref

        @pl.when(pl.program_id(1) == 0)
        def _():
            for q in range(n_par):
                s_ref[q] = s0_ref[0]

    n_blk = rows // chunk
    lanes = [slice(h * HEAD_DIM, (h + 1) * HEAD_DIM) for h in range(N_HEADS)]
    bf = lambda x: x.astype(BF16)

    shift = chunk.bit_length() - 1
    rs = lax.broadcasted_iota(jnp.int32, (rows, rows), 0)
    cs = lax.broadcasted_iota(jnp.int32, (rows, rows), 1)
    same = (rs >> shift) == (cs >> shift)
    mask_strict = same & (rs > cs)
    ri = lax.broadcasted_iota(jnp.int32, (rows, 2 * rows), 0)
    ci = lax.broadcasted_iota(jnp.int32, (rows, 2 * rows), 1)
    cj = jnp.where(ci >= rows, ci - rows, ci)
    mask_r = ((ri >> shift) == (cj >> shift)) & (ri >= cj)
    eye = (rs == cs).astype(F32)

    def chunk_rows(x, c):
        return x[c * chunk:(c + 1) * chunk, :]

    pair_lane = lax.broadcasted_iota(jnp.int32, (rows, 2 * HEAD_DIM), 1) >> (HEAD_DIM.bit_length() - 1)

    def pair_masked(ref, q, h):
        hp, hh = divmod(h, 2)
        x2 = ref[q, :, 2 * HEAD_DIM * hp:2 * HEAD_DIM * (hp + 1)].astype(F32)
        return jnp.where(pair_lane == hh, x2, 0.0)

    def unit_group(units):
        per_unit = lambda f: {u: f(*u) for u in units}
        v = per_unit(lambda q, h: v_ref[q, :, lanes[h]])
        if n_blk == 1:
            a = per_unit(lambda q, h: ah_ref[q, :, lanes[h]])
            r = per_unit(lambda q, h: rh_ref[q, :, lanes[h]])
            b = per_unit(lambda q, h: bh_ref[q, :, lanes[h]])
            k = per_unit(lambda q, h: kh_ref[q, :, lanes[h]])
        else:
            a = per_unit(lambda q, h: pair_masked(ah_ref, q, h))
            r = per_unit(lambda q, h: pair_masked(rh_ref, q, h))
            b = per_unit(lambda q, h: pair_masked(bh_ref, q, h))
            k = per_unit(lambda q, h: pair_masked(kh_ref, q, h))
            v32 = {u: v[u].astype(F32) for u in units}
        ar = {u: bf(jnp.concatenate([a[u], r[u]], axis=0)) for u in units}
        bk = {u: bf(jnp.concatenate([b[u], k[u]], axis=0)) for u in units}

        gram = {u: _dot(ar[u], jnp.transpose(bk[u])) for u in units}
        l_ab = {u: jnp.where(mask_strict, gram[u][:rows, :rows], 0.0) for u in units}
        l_ak = {u: bf(jnp.where(mask_strict, gram[u][:rows, rows:], 0.0)) for u in units}
        l_r = {u: bf(jnp.where(mask_r, gram[u][rows:, :], 0.0)) for u in units}

        xs, ys = {}, {}
        for u in units:
            q, h = u
            if n_blk == 1:
                st = _dot_nt(ar[u], bf(s_in[q, h]))
                xs[u], ys[u] = st[:rows], st[rows:]
            else:
                parts = [_dot_nt(jnp.concatenate([chunk_rows(a[u], c), chunk_rows(r[u], c)], axis=0),
                                 s_in[c, h // 2]) for c in range(n_blk)]
                xs[u] = jnp.concatenate([p[:chunk] for p in parts], axis=0)
                ys[u] = jnp.concatenate([p[chunk:] for p in parts], axis=0)

        x = {u: xs[u] + _dot(l_ak[u], v[u]) for u in units}

        t_inv = {u: eye + l_ab[u] for u in units}
        n = 2
        if n < chunk:
            pw = {u: bf(l_ab[u]) for u in units}
            pw = {u: bf(_dot(pw[u], pw[u])) for u in units}
        while n < chunk:
            if 2 * n < chunk:
                both = {u: _dot(jnp.concatenate([bf(t_inv[u]), pw[u]], axis=0), pw[u]) for u in units}
                t_inv = {u: t_inv[u] + both[u][:rows] for u in units}
                pw = {u: bf(both[u][rows:]) for u in units}
            else:
                t_inv = {u: t_inv[u] + _dot(bf(t_inv[u]), pw[u]) for u in units}
            n *= 2

        uu = {u: _dot(bf(t_inv[u]), bf(x[u])) for u in units}
        uv = {u: jnp.concatenate([bf(uu[u]), v[u]], axis=0) for u in units}
        for u in units:
            q, h = u
            y_ref[q, :, lanes[h]] = ys[u] + _dot(l_r[u], uv[u])

        if n_blk == 1:
            for u in units:
                q, h = u
                upd = _dot_tn(uv[u], bk[u])
                s_ref[q, h] = (s_in[q, h] + upd) * wc_ref[q, 0][:, lanes[h]]
        else:
            def update(u, c):
                uv_c = jnp.concatenate([chunk_rows(uu[u], c), chunk_rows(v32[u], c)], axis=0)
                bk_c = jnp.concatenate([chunk_rows(b[u], c), chunk_rows(k[u], c)], axis=0)
                return _dot_tn(uv_c, bk_c)

            for q, h in units:
                if h % 2 == 0:
                    assert (q, h + 1) in units
                    hp = h // 2
                    for c in range(n_blk):
                        s_new = s_in[c, hp] + update((q, h), c) + update((q, h + 1), c)
                        s_ref[c, hp] = s_new * wc_ref[q, c][:, 2 * HEAD_DIM * hp:2 * HEAD_DIM * (hp + 1)]

    all_units = [(q, h) for h in range(N_HEADS) for q in range(n_par)]
    for g0 in range(0, len(all_units), group):
        unit_group(all_units[g0:g0 + group])


def _scan(ah, rh, bh, kh, v, wc, s0, *, n_seq, seq_len, n_par, rows, chunk, per_chunk_state, group,
          name, cast=()):
    t = n_seq * seq_len
    n_blk = rows // chunk
    if per_chunk_state:
        assert seq_len == chunk and n_par == 1
        lead = 1
        grid = (t // rows,)
        rmap = lambda i: (0, i, 0)
        cmap = lambda i: (0, i, 0, 0)
        smap = lambda i: (i, 0, 0, 0)
        s0map = smap
        state_block = (n_blk, N_HEADS // 2, HEAD_DIM, 2 * HEAD_DIM)
        s0_block = state_block
        wc_block = (1, n_blk, 1, G)
        sem = ("parallel",)
    else:
        assert rows == chunk and s0.shape[0] == 1 and n_seq % n_par == 0
        lead = n_seq
        grid = (n_seq // n_par, seq_len // chunk)
        rmap = lambda s, c: (s, c, 0)
        cmap = lambda s, c: (s, c, 0, 0)
        smap = lambda s, c: (s, 0, 0, 0)
        s0map = lambda s, c: (0, 0, 0, 0)
        state_block = (n_par, N_HEADS, HEAD_DIM, HEAD_DIM)
        s0_block = (1, N_HEADS, HEAD_DIM, HEAD_DIM)
        wc_block = (n_par, 1, 1, G)
        sem = ("parallel", "arbitrary")
    rows3 = lambda x: x.reshape(lead, t // lead, G)
    row_spec = pl.BlockSpec((n_par, rows, G), rmap)
    n_steps = 1
    for extent in grid:
        n_steps *= extent
    step = (lambda i: i) if len(grid) == 1 else (lambda s, c: s * grid[1] + c)
    cast_specs = []
    for w in cast:
        assert w.shape[0] % (16 * n_steps) == 0
        cast_specs.append(pl.BlockSpec((w.shape[0] // n_steps, w.shape[1]), lambda *g: (step(*g), 0)))
    y, s_out, *narrowed = pl.pallas_call(
        functools.partial(_scan_kernel, n_par=n_par, rows=rows, chunk=chunk,
                          per_chunk_state=per_chunk_state, group=group, n_cast=len(cast)),
        grid=grid,
        in_specs=([row_spec] * 5 + [pl.BlockSpec(wc_block, cmap), pl.BlockSpec(s0_block, s0map)]
                  + cast_specs),
        out_specs=[row_spec, pl.BlockSpec(state_block, smap)] + cast_specs,
        out_shape=[jax.ShapeDtypeStruct((lead, t // lead, G), F32),
                   jax.ShapeDtypeStruct((n_seq,) + state_block[1:], F32)]
                  + [jax.ShapeDtypeStruct(w.shape, BF16) for w in cast],
        compiler_params=pltpu.CompilerParams(dimension_semantics=sem, vmem_limit_bytes=VMEM_LIMIT),
        name=name,
    )(rows3(ah), rows3(rh), rows3(bh), rows3(kh), rows3(v), wc.reshape(lead, -1, 1, G), s0, *cast)
    return (y.reshape(t, G), s_out, *narrowed)


def _mix_kernel(*refs, segs):
    n = len(segs)
    row_refs = [refs[4 * s:4 * s + 4] for s in range(n)]
    x_ref, lnw_ref, lnb_ref, bd_ref, wo_ref, o_ref = refs[4 * n:]
    i = pl.program_id(0)

    def body(y_ref, bonus_ref, g_ref, conv_ref):
        bd = bd_ref[...]
        y = y_ref[...]
        mu = _head_sum(y, bd) * (1.0 / HEAD_DIM)
        d = y - mu
        var = _head_sum(d * d, bd) * (1.0 / HEAD_DIM)
        yn = d * lax.rsqrt(var + GN_EPS) * lnw_ref[...] + lnb_ref[...]
        rw = ((yn + bonus_ref[...]) * g_ref[...]).astype(BF16)
        mix = _dot(rw, wo_ref[0:G, :]) + _dot(conv_ref[...], wo_ref[G:2 * G, :])
        o_ref[...] = x_ref[...] + mix

    for rr, (off, cnt) in zip(row_refs, segs):
        @pl.when((i >= off) & (i < off + cnt))
        def _(rr=rr):
            body(*rr)


def _mix(row_groups, x1, lnw, lnb, bd, wo):
    segs, n_tiles = _segments([grp[0] for grp in row_groups], TM_MIX)
    cmap = lambda i: (0, 0)
    in_specs, args = [], []
    for grp, (off, cnt) in zip(row_groups, segs):
        smap = lambda i, off=off, cnt=cnt: (jnp.clip(i - off, 0, cnt - 1), 0)
        in_specs += [pl.BlockSpec((TM_MIX, G), smap)] * 4
        args += list(grp)
    in_specs += [pl.BlockSpec((TM_MIX, D_MODEL), lambda i: (i, 0)),
                 pl.BlockSpec((1, G), cmap), pl.BlockSpec((1, G), cmap),
                 pl.BlockSpec((G, G), cmap), pl.BlockSpec((D_MODEL, D_MODEL), cmap)]
    return pl.pallas_call(
        functools.partial(_mix_kernel, segs=segs),
        grid=(n_tiles,), in_specs=in_specs,
        out_specs=pl.BlockSpec((TM_MIX, D_MODEL), lambda i: (i, 0)),
        out_shape=jax.ShapeDtypeStruct((n_tiles * TM_MIX, D_MODEL), F32),
        compiler_params=pltpu.CompilerParams(
            dimension_semantics=("parallel",), vmem_limit_bytes=VMEM_LIMIT),
        name="mix",
    )(*args, x1, lnw, lnb, bd, wo)


def _pack_rwkv(a):
    r, wd, k, v, ad, gd = jnp.split(a, RWKV_SPLITS, axis=-1)
    zw = jnp.zeros(a.shape[:-1] + (LORA_PAD - W_LORA,), a.dtype)
    za = jnp.zeros(a.shape[:-1] + (LORA_PAD - A_LORA,), a.dtype)
    return jnp.concatenate([r, k, v, wd, zw, ad, za, gd], axis=-1)


def _unpack_rwkv(p):
    return jnp.concatenate([
        p[..., 0:G], p[..., OFF_WD:OFF_WD + W_LORA], p[..., G:2 * G], p[..., 2 * G:3 * G],
        p[..., OFF_AD:OFF_AD + A_LORA], p[..., OFF_GD:OFF_GD + G_LORA]], axis=-1)


def _block_tri(n, chunk):
    i = jnp.arange(n)
    return ((i[:, None] // chunk == i[None, :] // chunk) & (i[:, None] >= i[None, :])).astype(BF16)


def kernel(x_prompt, x_sample, state_wkv, state_shift, state_conv, meta_tokens, g_ffn1, ffn1_gate, ffn1_up, ffn1_down, g_mix, w_in, mu_shift, w0, w_lora_w, a0, w_lora_a, w_lora_g, k_k, k_a, r_k, ln_x_w, ln_x_b, conv_w, w_out, g_ffn2, ffn2_gate, ffn2_up, ffn2_down, g_final):
    assert g_ffn1.shape[0] == 1, "single layer"
    nb, seq, _ = x_prompt.shape
    db, dseq, _ = x_sample.shape
    assert dseq == C_SAMPLE and N_META <= C_PROMPT and seq % ROWS_PREP == 0
    tp, ts = nb * seq, db * dseq
    assert tp % TM == 0 and ts % TM == 0

    row = lambda a: a.reshape(1, -1).astype(F32)
    w_in_p = _pack_w_in(jnp.transpose(w_in[0]))
    pad_rows = lambda w, n: jnp.concatenate([w, jnp.zeros((n - w.shape[0], w.shape[1]), w.dtype)], axis=0)
    hid = jnp.arange(G) // HEAD_DIM
    bd = (hid[:, None] == hid[None, :]).astype(BF16)
    prep_w = (_pack_rwkv(mu_shift[0])[None], row(w0[0]), pad_rows(w_lora_w[0], LORA_PAD).astype(BF16),
              row(a0[0]), pad_rows(w_lora_a[0], LORA_PAD).astype(BF16), w_lora_g[0].astype(BF16),
              row(k_k[0]), row(k_a[0]), row(r_k[0]), conv_w[0].astype(F32), bd)

    x_meta = jnp.concatenate([jnp.zeros((C_PROMPT - N_META, D_MODEL), F32), meta_tokens.astype(F32)], axis=0)
    x1_meta, wg1, wu1, wd1 = _ffn([x_meta], row(g_ffn1[0]), ffn1_gate[0], ffn1_up[0], ffn1_down[0], None,
                                  [C_PROMPT], name="ffn1_meta", tm=C_PROMPT)
    proj_meta = _proj(x1_meta, row(g_mix[0]), w_in_p, name="proj_meta", tm=C_PROMPT)
    x1 = _ffn([x_prompt.reshape(tp, D_MODEL), x_sample.reshape(ts, D_MODEL)], row(g_ffn1[0]), wg1, wu1, wd1,
              None, [tp + ts], name="ffn1")[0]
    proj = _proj(x1, row(g_mix[0]), w_in_p, name="proj")

    tri_p = _block_tri(ROWS_PREP, C_PROMPT)
    zeros_state = jnp.zeros((1, N_HEADS, HEAD_DIM, HEAD_DIM), F32)
    (ah, rh, bh, kh, vv, wc, _, _, _, utail_m, ptail_m) = _prep(
        proj_meta, prep_w, tri_p[:C_PROMPT, :C_PROMPT], n_seq=1, seq_len=C_PROMPT, rows=C_PROMPT,
        chunk=C_PROMPT, row_block_offset=0, extra=(jnp.zeros((8, RW), F32), jnp.zeros((8, G), F32)),
        sample=False, name="prep_meta")
    _, wkv_m = _scan(ah, rh, bh, kh, vv, wc, zeros_state, n_seq=1, seq_len=C_PROMPT, n_par=1,
                     rows=C_PROMPT, chunk=C_PROMPT, per_chunk_state=False, group=N_HEADS,
                     name="scan_meta")

    (ah, rh, bh, kh, vv, wc, g_p, bonus_p, conv_p, utail_p, ptail_p) = _prep(
        proj, prep_w, tri_p, n_seq=nb, seq_len=seq, rows=ROWS_PREP, chunk=C_PROMPT,
        row_block_offset=0, extra=(ptail_m[0], utail_m[0]), sample=False, name="prep_prompt")
    y_p, wkv_p, wg2, wu2, wd2, wo = _scan(
        ah, rh, bh, kh, vv, wc, wkv_m, n_seq=nb, seq_len=seq, n_par=SCAN_PAR, rows=C_PROMPT,
        chunk=C_PROMPT, per_chunk_state=False, group=SCAN_PAR * N_HEADS // 2, name="scan_prompt",
        cast=(ffn2_gate[0], ffn2_up[0], ffn2_down[0], w_out[0]))

    prev = _pack_rwkv(state_shift[0])
    up1 = state_conv[0][:, 1]
    up2 = state_conv[0][:, 0]
    (ah, rh, bh, kh, vv, wc, g_s, bonus_s, conv_s, u_s, plast_s) = _prep(
        proj, prep_w, _block_tri(ROWS_PREP, C_SAMPLE), n_seq=db, seq_len=dseq, rows=ROWS_PREP,
        chunk=C_SAMPLE, row_block_offset=tp // ROWS_PREP, extra=(prev, up1, up2), sample=True,
        name="prep_sample")
    pair = (db, N_HEADS // 2, 2, HEAD_DIM, HEAD_DIM)
    s0_pairs = jnp.transpose(state_wkv[0].reshape(pair), (0, 1, 3, 2, 4)).reshape(
        db, N_HEADS // 2, HEAD_DIM, 2 * HEAD_DIM)
    y_s, wkv_s = _scan(ah, rh, bh, kh, vv, wc, s0_pairs, n_seq=db, seq_len=dseq, n_par=1,
                       rows=ROWS_SCAN_SAMPLE, chunk=C_SAMPLE, per_chunk_state=True, group=4,
                       name="scan_sample")

    x2 = _mix([(y_p, bonus_p, g_p, conv_p), (y_s, bonus_s, g_s, conv_s)], x1,
              row(ln_x_w[0]), row(ln_x_b[0]), bd, wo)
    y_prompt, y_sample = _ffn([x2], row(g_ffn2[0]), wg2, wu2, wd2, row(g_final), [tp, ts], name="ffn2")

    shift_p = _unpack_rwkv(ptail_p.reshape(nb, seq // ROWS_PREP, 8, RW)[:, -1, 7, :])
    conv_state_p = utail_p.reshape(nb, seq // ROWS_PREP, 8, G)[:, -1, 6:, :]
    shift_s = _unpack_rwkv(plast_s)
    conv_state_s = u_s.reshape(db, dseq, G)[:, -2:, :]
    wkv_s_out = jnp.transpose(wkv_s.reshape(db, N_HEADS // 2, HEAD_DIM, 2, HEAD_DIM),
                              (0, 1, 3, 2, 4)).reshape(db, N_HEADS, HEAD_DIM, HEAD_DIM)
    return (y_prompt.reshape(nb, seq, D_MODEL), y_sample.reshape(db, dseq, D_MODEL),
            wkv_p[None].astype(state_wkv.dtype), shift_p[None].astype(state_shift.dtype),
            conv_state_p[None].astype(state_conv.dtype),
            wkv_s_out[None].astype(state_wkv.dtype), shift_s[None].astype(state_shift.dtype),
            conv_state_s[None].astype(state_conv.dtype))
```

```python
import functools

import jax
import jax.numpy as jnp
from jax import lax
from jax.experimental import pallas as pl
from jax.experimental.pallas import tpu as pltpu

F32 = jnp.float32
BF16 = jnp.bfloat16

D_MODEL = 2048
D_FF = 5632
N_META = 16
G = 1024
HEAD_DIM = 64
N_HEADS = G // HEAD_DIM
W_LORA = 96
A_LORA = 96
G_LORA = 256
LORA_PAD = 128
RWKV_PROJ = 3 * G + W_LORA + A_LORA + G_LORA
RWKV_SPLITS = (G, G + W_LORA, 2 * G + W_LORA, 3 * G + W_LORA, 3 * G + W_LORA + A_LORA)
RW = 3 * G + 2 * LORA_PAD + G_LORA
PW = RW + 3 * G
OFF_WD, OFF_AD, OFF_GD = 3 * G, 3 * G + LORA_PAD, 3 * G + 2 * LORA_PAD
RMS_EPS = 1e-6
GN_EPS = 64e-5
EXP_M05_LOG2E = 0.6065306597126334 * 1.4426950408889634

C_PROMPT = 64
C_SAMPLE = 8
TM = 512
TF = 512
TN = 3328
TM_FFN2 = 256
TM_MIX = 256
ROWS_PREP = 256
ROWS_SCAN_SAMPLE = 128
SCAN_PAR = 4
VMEM_LIMIT = 56 * 1024 * 1024


def _dot(a, b):
    return jnp.dot(a, b, preferred_element_type=F32)


def _dot_nt(a, b):
    return lax.dot_general(a, b, (((1,), (1,)), ((), ())), preferred_element_type=F32)


def _dot_tn(a, b):
    return lax.dot_general(a, b, (((0,), (0,)), ((), ())), preferred_element_type=F32)


def _split(a):
    hi = a.astype(BF16)
    lo = (a - hi.astype(F32)).astype(BF16)
    return hi, lo


def _head_sum(x, bd):
    return _dot(x.astype(BF16), bd)


def _dot_split_rhs(a_bf16, b):
    hi, lo = _split(b)
    return _dot(a_bf16, hi) + _dot(a_bf16, lo)


def _rms(x, g):
    return x * lax.rsqrt(jnp.mean(x * x, axis=-1, keepdims=True) + RMS_EPS) * g


def _segments(arrays, tile):
    segs, off = [], 0
    for arr in arrays:
        n = arr.shape[0] // tile
        assert n * tile == arr.shape[0]
        segs.append((off, n))
        off += n
    return segs, off


def _seg_spec(tile, width, seg):
    off, n = seg
    return pl.BlockSpec((tile, width), lambda i, j: (jnp.clip(i - off, 0, n - 1), 0))


def _overlaps(in_segs, out_segs):
    for a, (ao, an) in enumerate(in_segs):
        for b, (bo, bn) in enumerate(out_segs):
            lo, hi = max(ao, bo), min(ao + an, bo + bn)
            if lo < hi:
                yield a, b, lo, hi


def _ffn_kernel(*refs, in_segs, out_segs, final_norm, narrow):
    n_in, n_out = len(in_segs), len(out_segs)
    x_refs = refs[:n_in]
    g_ref, wg_ref, wu_ref, wd_ref = refs[n_in:n_in + 4]
    rest = refs[n_in + 4:]
    if final_norm:
        gf_ref, rest = rest[0], rest[1:]
    o_refs = rest[:n_out]
    rest = rest[n_out:]
    if narrow:
        w_out_refs, rest = rest[:3], rest[3:]
    h_ref, acc_ref = rest
    i = pl.program_id(0)
    j = pl.program_id(1)

    for x_ref, (off, n) in zip(x_refs, in_segs):
        @pl.when((j == 0) & (i >= off) & (i < off + n))
        def _(x_ref=x_ref):
            h_ref[...] = _rms(x_ref[...], g_ref[...]).astype(BF16)
            acc_ref[...] = jnp.zeros_like(acc_ref)

    wg, wu, wd = wg_ref[...], wu_ref[...], wd_ref[...]
    if narrow:
        wg, wu, wd = wg.astype(BF16), wu.astype(BF16), wd.astype(BF16)
        for o_ref, w in zip(w_out_refs, (wg, wu, wd)):
            o_ref[...] = w
    h = h_ref[...]
    gate = _dot(h, wg)
    up = _dot(h, wu)
    act = (gate * jax.nn.sigmoid(gate) * up).astype(BF16)
    acc_ref[...] += _dot(act, wd)

    for a, b, lo, hi in _overlaps(in_segs, out_segs):
        @pl.when((j == pl.num_programs(1) - 1) & (i >= lo) & (i < hi))
        def _(x_ref=x_refs[a], o_ref=o_refs[b]):
            xo = x_ref[...] + 0.5 * acc_ref[...]
            if final_norm:
                xo = _rms(xo, gf_ref[...])
            o_ref[...] = xo


def _ffn(xs, g, wg, wu, wd, g_final, out_rows, *, name, tm=TM, tf=TF):
    final_norm = g_final is not None
    narrow = wg.dtype == F32
    in_segs, n_tiles_in = _segments(xs, tm)
    out_shape = [jax.ShapeDtypeStruct((n, D_MODEL), F32) for n in out_rows]
    out_segs, n_tiles = _segments(out_shape, tm)
    assert n_tiles <= n_tiles_in and (n_tiles == 1 or not narrow)
    cmap = lambda i, j: (0, 0)
    w_specs = [pl.BlockSpec((D_MODEL, tf), lambda i, j: (0, j)),
               pl.BlockSpec((D_MODEL, tf), lambda i, j: (0, j)),
               pl.BlockSpec((tf, D_MODEL), lambda i, j: (j, 0))]
    in_specs = [_seg_spec(tm, D_MODEL, s) for s in in_segs] + [pl.BlockSpec((1, D_MODEL), cmap)] + w_specs
    args = list(xs) + [g, wg, wu, wd]
    if final_norm:
        in_specs.append(pl.BlockSpec((1, D_MODEL), cmap))
        args.append(g_final)
    out_specs = [_seg_spec(tm, D_MODEL, s) for s in out_segs]
    if narrow:
        out_specs += w_specs
        out_shape = out_shape + [jax.ShapeDtypeStruct(w.shape, BF16) for w in (wg, wu, wd)]
    return pl.pallas_call(
        functools.partial(_ffn_kernel, in_segs=in_segs, out_segs=out_segs, final_norm=final_norm,
                          narrow=narrow),
        grid=(n_tiles, D_FF // tf),
        in_specs=in_specs,
        out_specs=out_specs,
        out_shape=out_shape,
        scratch_shapes=[pltpu.VMEM((tm, D_MODEL), BF16), pltpu.VMEM((tm, D_MODEL), F32)],
        compiler_params=pltpu.CompilerParams(
            dimension_semantics=("arbitrary", "arbitrary"), vmem_limit_bytes=VMEM_LIMIT),
        name=name,
    )(*args)


def _ffn_up_kernel(*refs, in_segs):
    n_in = len(in_segs)
    x_refs = refs[:n_in]
    g_ref, wg_ref, wu_ref, a_ref = refs[n_in:]
    i = pl.program_id(1)
    for x_ref, (off, n) in zip(x_refs, in_segs):
        @pl.when((i >= off) & (i < off + n))
        def _(x_ref=x_ref):
            h = _rms(x_ref[...], g_ref[...]).astype(BF16)
            gate = _dot(h, wg_ref[...])
            up = _dot(h, wu_ref[...])
            a_ref[...] = (gate * jax.nn.sigmoid(gate) * up).astype(BF16)


def _ffn_down_kernel(*refs, in_segs, out_segs, final_norm):
    n_in, n_out = len(in_segs), len(out_segs)
    a_ref = refs[0]
    x_refs = refs[1:1 + n_in]
    wd_ref = refs[1 + n_in]
    rest = refs[2 + n_in:]
    if final_norm:
        gf_ref, rest = rest[0], rest[1:]
    o_refs = rest[:n_out]
    i = pl.program_id(0)
    for a, b, lo, hi in _overlaps(in_segs, out_segs):
        @pl.when((i >= lo) & (i < hi))
        def _(x_ref=x_refs[a], o_ref=o_refs[b]):
            xo = x_ref[...] + 0.5 * _dot(a_ref[...], wd_ref[...])
            if final_norm:
                xo = _rms(xo, gf_ref[...])
            o_ref[...] = xo


def _ffn_two_stage(xs, g, wg, wu, wd, g_final, out_rows, *, name):
    final_norm = g_final is not None
    tm, tn = TM_FFN2, D_FF // 2
    in_segs, n_tiles = _segments(xs, tm)
    out_shape = [jax.ShapeDtypeStruct((n, D_MODEL), F32) for n in out_rows]
    out_segs, n_out_tiles = _segments(out_shape, tm)
    assert n_out_tiles == n_tiles
    seg_ji = lambda s: pl.BlockSpec((tm, D_MODEL), lambda j, i, s=s: (jnp.clip(i - s[0], 0, s[1] - 1), 0))
    half = pl.BlockSpec((D_MODEL, tn), lambda j, i: (0, j), pipeline_mode=pl.Buffered(1))
    act = pl.pallas_call(
        functools.partial(_ffn_up_kernel, in_segs=in_segs),
        grid=(D_FF // tn, n_tiles),
        in_specs=[seg_ji(s) for s in in_segs] + [pl.BlockSpec((1, D_MODEL), lambda j, i: (0, 0)), half, half],
        out_specs=pl.BlockSpec((tm, tn), lambda j, i: (i, j)),
        out_shape=jax.ShapeDtypeStruct((n_tiles * tm, D_FF), BF16),
        compiler_params=pltpu.CompilerParams(
            dimension_semantics=("arbitrary", "arbitrary"), vmem_limit_bytes=VMEM_LIMIT),
        name=name + "_up",
    )(*xs, g, wg, wu)

    seg_i = lambda s: pl.BlockSpec((tm, D_MODEL), lambda i, s=s: (jnp.clip(i - s[0], 0, s[1] - 1), 0))
    in_specs = ([pl.BlockSpec((tm, D_FF), lambda i: (i, 0))] + [seg_i(s) for s in in_segs]
                + [pl.BlockSpec((D_FF, D_MODEL), lambda i: (0, 0), pipeline_mode=pl.Buffered(1))])
    args = [act] + list(xs) + [wd]
    if final_norm:
        in_specs.append(pl.BlockSpec((1, D_MODEL), lambda i: (0, 0)))
        args.append(g_final)
    return pl.pallas_call(
        functools.partial(_ffn_down_kernel, in_segs=in_segs, out_segs=out_segs, final_norm=final_norm),
        grid=(n_tiles,),
        in_specs=in_specs,
        out_specs=[seg_i(s) for s in out_segs],
        out_shape=out_shape,
        compiler_params=pltpu.CompilerParams(
            dimension_semantics=("arbitrary",), vmem_limit_bytes=VMEM_LIMIT),
        name=name + "_down",
    )(*args)


_PACK_MOVES = (
    (0, 0, G),
    (G, RWKV_SPLITS[1], G),
    (2 * G, RWKV_SPLITS[2], G),
    (OFF_WD, RWKV_SPLITS[0], W_LORA),
    (OFF_AD, RWKV_SPLITS[3], A_LORA),
    (OFF_GD, RWKV_SPLITS[4], G_LORA),
    (RW, RWKV_PROJ, 3 * G),
)


def _pack_w_in_kernel(w_ref, o_ref):
    for dst, src, n in _PACK_MOVES:
        o_ref[dst:dst + n, :] = w_ref[src:src + n, :].astype(BF16)
    for lo, hi in ((OFF_WD + W_LORA, OFF_AD), (OFF_AD + A_LORA, OFF_GD)):
        o_ref[lo:hi, :] = jnp.zeros((hi - lo, o_ref.shape[1]), BF16)


def _pack_w_in(w_t):
    cols = 512
    return pl.pallas_call(
        _pack_w_in_kernel,
        grid=(D_MODEL // cols,),
        in_specs=[pl.BlockSpec((w_t.shape[0], cols), lambda i: (0, i))],
        out_specs=pl.BlockSpec((PW, cols), lambda i: (0, i)),
        out_shape=jax.ShapeDtypeStruct((PW, D_MODEL), BF16),
        compiler_params=pltpu.CompilerParams(
            dimension_semantics=("parallel",), vmem_limit_bytes=VMEM_LIMIT),
        name="pack_w_in",
    )(w_t)


def _proj_kernel(x_ref, g_ref, w_ref, o_ref):
    h = _rms(x_ref[...], g_ref[...]).astype(BF16)
    o_ref[...] = _dot_nt(h, w_ref[...])


def _proj(x, g, w, *, name, tm=TM):
    t = x.shape[0]
    return pl.pallas_call(
        _proj_kernel,
        grid=(PW // TN, t // tm),
        in_specs=[
            pl.BlockSpec((tm, D_MODEL), lambda j, i: (i, 0)),
            pl.BlockSpec((1, D_MODEL), lambda j, i: (0, 0)),
            pl.BlockSpec((TN, D_MODEL), lambda j, i: (j, 0)),
        ],
        out_specs=pl.BlockSpec((tm, TN), lambda j, i: (i, j)),
        out_shape=jax.ShapeDtypeStruct((t, PW), F32),
        compiler_params=pltpu.CompilerParams(
            dimension_semantics=("parallel", "parallel"), vmem_limit_bytes=VMEM_LIMIT),
        name=name,
    )(x, g, w)


def _prep_kernel(*refs, rows, chunk, sample):
    if sample:
        (p_ref, prev_ref, up1_ref, up2_ref, mu_ref, w0_ref, ww_ref, a0_ref, wa_ref, wgl_ref,
         kk_ref, ka_ref, rk_ref, cw_ref, bd_ref, tri_ref,
         ah_ref, rh_ref, bh_ref, kh_ref, v_ref, wc_ref, g_ref, bonus_ref, conv_ref, u_ref,
         plast_ref) = refs
    else:
        (p_ref, cp0_ref, cu0_ref, mu_ref, w0_ref, ww_ref, a0_ref, wa_ref, wgl_ref,
         kk_ref, ka_ref, rk_ref, cw_ref, bd_ref, tri_ref,
         ah_ref, rh_ref, bh_ref, kh_ref, v_ref, wc_ref, g_ref, bonus_ref, conv_ref, u_ref,
         ptail_ref, carry_p, carry_u) = refs

        @pl.when(pl.program_id(1) == 0)
        def _():
            carry_p[...] = cp0_ref[...]
            carry_u[...] = cu0_ref[...]

    def row_ids(width):
        r = lax.broadcasted_iota(jnp.int32, (rows, width), 0)
        return (r & 7) if sample else r

    def per_sequence(x):
        n, w = x.shape
        return jnp.broadcast_to(x[:, None, :], (n, 8, w)).reshape(n * 8, w)

    def shifted_mix(lo, hi):
        p = p_ref[:, lo:hi]
        rolled = pltpu.roll(p, 1, 0)
        if sample:
            sh = jnp.where(row_ids(hi - lo) == 0, per_sequence(prev_ref[:, lo:hi]), rolled)
        else:
            sh = jnp.where(row_ids(hi - lo) == 0, carry_p[7:8, lo:hi], rolled)
        return p + (sh - p) * mu_ref[:, lo:hi]

    r = shifted_mix(0, G)
    k = shifted_mix(G, 2 * G)
    v = shifted_mix(2 * G, 3 * G)
    wd = shifted_mix(OFF_WD, OFF_WD + LORA_PAD)
    ad = shifted_mix(OFF_AD, OFF_AD + LORA_PAD)
    gd = shifted_mix(OFF_GD, OFF_GD + G_LORA)

    z = w0_ref[...] + _dot(jnp.tanh(wd).astype(BF16), ww_ref[...])
    lw = -EXP_M05_LOG2E * jax.nn.sigmoid(z)
    a = jax.nn.sigmoid(a0_ref[...] + _dot(ad.astype(BF16), wa_ref[...]))
    g_ref[...] = _dot(jax.nn.sigmoid(gd).astype(BF16), wgl_ref[...]).astype(BF16)

    bd = bd_ref[...]
    kk = k * kk_ref[...]
    norm = jnp.sqrt(_head_sum(kk * kk, bd))
    kk = kk / jnp.maximum(norm, 1e-12)
    km = k * (1.0 + (a - 1.0) * ka_ref[...])
    bonus_ref[...] = _head_sum(r * km * rk_ref[...], bd) * v

    cum = _dot_split_rhs(tri_ref[...], lw)
    e_cum = jnp.exp2(cum)
    e_inv = jnp.exp2(-cum)
    ah_ref[...] = (-kk * jnp.exp2(cum - lw)).astype(BF16)
    rh_ref[...] = (r * e_cum).astype(BF16)
    bh_ref[...] = (kk * a * e_inv).astype(BF16)
    kh_ref[...] = (km * e_inv).astype(BF16)
    v_ref[...] = v.astype(BF16)
    for c in range(rows // chunk):
        wc_ref[c] = e_cum[(c + 1) * chunk - 1:(c + 1) * chunk, :]

    bg = p_ref[:, RW:RW + G]
    u = p_ref[:, RW + G:RW + 2 * G] * p_ref[:, RW + 2 * G:RW + 3 * G]
    u1 = pltpu.roll(u, 1, 0)
    u2 = pltpu.roll(u, 2, 0)
    rid = row_ids(G)
    if sample:
        s1 = per_sequence(up1_ref[...])
        um1 = jnp.where(rid == 0, s1, u1)
        um2 = jnp.where(rid == 0, per_sequence(up2_ref[...]), jnp.where(rid == 1, s1, u2))
        u_ref[...] = u
        plast_ref[...] = p_ref[:, 0:RW].reshape(rows // 8, 8, RW)[:, 7, :]
    else:
        um1 = jnp.where(rid == 0, carry_u[7:8, :], u1)
        um2 = jnp.where(rid == 0, carry_u[6:7, :], jnp.where(rid == 1, carry_u[7:8, :], u2))
        u_ref[0] = u[rows - 8:, :]
        ptail_ref[0] = p_ref[rows - 8:rows, 0:RW]
    conv = cw_ref[0:1, :] * um2 + cw_ref[1:2, :] * um1 + cw_ref[2:3, :] * u
    conv_ref[...] = (bg * conv).astype(BF16)

    if not sample:
        carry_p[...] = p_ref[rows - 8:rows, 0:RW]
        carry_u[...] = u[rows - 8:, :]


def _prep(proj, weights, tri, *, n_seq, seq_len, rows, chunk, row_block_offset, extra, sample, name):
    t = n_seq * seq_len
    n_chunks = t // chunk
    cpt = rows // chunk
    if sample:
        grid = (t // rows,)
        rmap = lambda i: (i + row_block_offset, 0)
        omap = lambda i: (i, 0)
        omap3 = lambda i: (i, 0, 0)
        cmap = lambda i: (0, 0)
        sem = ("parallel",)
    else:
        tiles = seq_len // rows
        grid = (n_seq, tiles)
        rmap = lambda b, j: (b * tiles + j + row_block_offset, 0)
        omap = lambda b, j: (b * tiles + j, 0)
        omap3 = lambda b, j: (b * tiles + j, 0, 0)
        cmap = lambda b, j: (0, 0)
        sem = ("parallel", "arbitrary")

    const = lambda arr: pl.BlockSpec(arr.shape, cmap)
    in_specs = [pl.BlockSpec((rows, PW), rmap)]
    if sample:
        in_specs += [pl.BlockSpec((rows // 8, RW), omap), pl.BlockSpec((rows // 8, G), omap),
                     pl.BlockSpec((rows // 8, G), omap)]
    else:
        in_specs += [const(e) for e in extra]
    in_specs += [const(w) for w in weights] + [const(tri)]
    args = [proj] + list(extra) + list(weights) + [tri]

    row_out = jax.ShapeDtypeStruct((t, G), F32)
    row_spec = pl.BlockSpec((rows, G), omap)
    row_bf16 = jax.ShapeDtypeStruct((t, G), BF16)
    out_shape = [row_bf16] * 5 + [jax.ShapeDtypeStruct((n_chunks, 1, G), F32), row_bf16, row_out,
                                  row_bf16]
    out_specs = [row_spec] * 5 + [pl.BlockSpec((cpt, 1, G), omap3), row_spec, row_spec, row_spec]
    if sample:
        out_shape += [row_out, jax.ShapeDtypeStruct((t // 8, RW), F32)]
        out_specs += [row_spec, pl.BlockSpec((rows // 8, RW), omap)]
        scratch = []
    else:
        out_shape += [jax.ShapeDtypeStruct((t // rows, 8, G), F32),
                      jax.ShapeDtypeStruct((t // rows, 8, RW), F32)]
        out_specs += [pl.BlockSpec((1, 8, G), omap3), pl.BlockSpec((1, 8, RW), omap3)]
        scratch = [pltpu.VMEM((8, RW), F32), pltpu.VMEM((8, G), F32)]
    return pl.pallas_call(
        functools.partial(_prep_kernel, rows=rows, chunk=chunk, sample=sample),
        grid=grid, in_specs=in_specs, out_specs=out_specs, out_shape=out_shape,
        scratch_shapes=scratch,
        compiler_params=pltpu.CompilerParams(dimension_semantics=sem, vmem_limit_bytes=VMEM_LIMIT),
        name=name,
    )(*args)


def _scan_kernel(*refs, n_par, rows, chunk, per_chunk_state, group, n_cast):
    ah_ref, rh_ref, bh_ref, kh_ref, v_ref, wc_ref, s0_ref = refs[:7]
    y_ref, s_ref = refs[7 + n_cast:9 + n_cast]
    for w_ref, o_ref in zip(refs[7:7 + n_cast], refs[9 + n_cast:]):
        o_ref[...] = w_ref[...].astype(BF16)

    if per_chunk_state:
        s_in = s0_ref
    else:
        s_in = s_ref

        @pl.when(pl.program_id(1) == 0)
        def _():
            for q in range(n_par):
                s_ref[q] = s0_ref[0]

    n_blk = rows // chunk
    lanes = [slice(h * HEAD_DIM, (h + 1) * HEAD_DIM) for h in range(N_HEADS)]
    bf = lambda x: x.astype(BF16)

    shift = chunk.bit_length() - 1
    rs = lax.broadcasted_iota(jnp.int32, (rows, rows), 0)
    cs = lax.broadcasted_iota(jnp.int32, (rows, rows), 1)
    same = (rs >> shift) == (cs >> shift)
    mask_strict = same & (rs > cs)
    ri = lax.broadcasted_iota(jnp.int32, (rows, 2 * rows), 0)
    ci = lax.broadcasted_iota(jnp.int32, (rows, 2 * rows), 1)
    cj = jnp.where(ci >= rows, ci - rows, ci)
    mask_r = ((ri >> shift) == (cj >> shift)) & (ri >= cj)
    eye = (rs == cs).astype(F32)

    def chunk_rows(x, c):
        return x[c * chunk:(c + 1) * chunk, :]

    pair_lane = lax.broadcasted_iota(jnp.int32, (rows, 2 * HEAD_DIM), 1) >> (HEAD_DIM.bit_length() - 1)

    def pair_masked(ref, q, h):
        hp, hh = divmod(h, 2)
        x2 = ref[q, :, 2 * HEAD_DIM * hp:2 * HEAD_DIM * (hp + 1)].astype(F32)
        return jnp.where(pair_lane == hh, x2, 0.0)

    def unit_group(units):
        per_unit = lambda f: {u: f(*u) for u in units}
        v = per_unit(lambda q, h: v_ref[q, :, lanes[h]])
        if n_blk == 1:
            a = per_unit(lambda q, h: ah_ref[q, :, lanes[h]])
            r = per_unit(lambda q, h: rh_ref[q, :, lanes[h]])
            b = per_unit(lambda q, h: bh_ref[q, :, lanes[h]])
            k = per_unit(lambda q, h: kh_ref[q, :, lanes[h]])
        else:
            a = per_unit(lambda q, h: pair_masked(ah_ref, q, h))
            r = per_unit(lambda q, h: pair_masked(rh_ref, q, h))
            b = per_unit(lambda q, h: pair_masked(bh_ref, q, h))
            k = per_unit(lambda q, h: pair_masked(kh_ref, q, h))
            v32 = {u: v[u].astype(F32) for u in units}
        ar = {u: bf(jnp.concatenate([a[u], r[u]], axis=0)) for u in units}
        bk = {u: bf(jnp.concatenate([b[u], k[u]], axis=0)) for u in units}

        gram = {u: _dot(ar[u], jnp.transpose(bk[u])) for u in units}
        l_ab = {u: jnp.where(mask_strict, gram[u][:rows, :rows], 0.0) for u in units}
        l_ak = {u: bf(jnp.where(mask_strict, gram[u][:rows, rows:], 0.0)) for u in units}
        l_r = {u: bf(jnp.where(mask_r, gram[u][rows:, :], 0.0)) for u in units}

        xs, ys = {}, {}
        for u in units:
            q, h = u
            if n_blk == 1:
                st = _dot_nt(ar[u], bf(s_in[q, h]))
                xs[u], ys[u] = st[:rows], st[rows:]
            else:
                parts = [_dot_nt(jnp.concatenate([chunk_rows(a[u], c), chunk_rows(r[u], c)], axis=0),
                                 s_in[c, h // 2]) for c in range(n_blk)]
                xs[u] = jnp.concatenate([p[:chunk] for p in parts], axis=0)
                ys[u] = jnp.concatenate([p[chunk:] for p in parts], axis=0)

        x = {u: xs[u] + _dot(l_ak[u], v[u]) for u in units}

        t_inv = {u: eye + l_ab[u] for u in units}
        n = 2
        if n < chunk:
            pw = {u: bf(l_ab[u]) for u in units}
            pw = {u: bf(_dot(pw[u], pw[u])) for u in units}
        while n < chunk:
            if 2 * n < chunk:
                both = {u: _dot(jnp.concatenate([bf(t_inv[u]), pw[u]], axis=0), pw[u]) for u in units}
                t_inv = {u: t_inv[u] + both[u][:rows] for u in units}
                pw = {u: bf(both[u][rows:]) for u in units}
            else:
                t_inv = {u: t_inv[u] + _dot(bf(t_inv[u]), pw[u]) for u in units}
            n *= 2

        uu = {u: _dot(bf(t_inv[u]), bf(x[u])) for u in units}
        uv = {u: jnp.concatenate([bf(uu[u]), v[u]], axis=0) for u in units}
        for u in units:
            q, h = u
            y_ref[q, :, lanes[h]] = ys[u] + _dot(l_r[u], uv[u])

        if n_blk == 1:
            for u in units:
                q, h = u
                upd = _dot_tn(uv[u], bk[u])
                s_ref[q, h] = (s_in[q, h] + upd) * wc_ref[q, 0][:, lanes[h]]
        else:
            def update(u, c):
                uv_c = jnp.concatenate([chunk_rows(uu[u], c), chunk_rows(v32[u], c)], axis=0)
                bk_c = jnp.concatenate([chunk_rows(b[u], c), chunk_rows(k[u], c)], axis=0)
                return _dot_tn(uv_c, bk_c)

            for q, h in units:
                if h % 2 == 0:
                    assert (q, h + 1) in units
                    hp = h // 2
                    for c in range(n_blk):
                        s_new = s_in[c, hp] + update((q, h), c) + update((q, h + 1), c)
                        s_ref[c, hp] = s_new * wc_ref[q, c][:, 2 * HEAD_DIM * hp:2 * HEAD_DIM * (hp + 1)]

    all_units = [(q, h) for h in range(N_HEADS) for q in range(n_par)]
    for g0 in range(0, len(all_units), group):
        unit_group(all_units[g0:g0 + group])


def _scan(ah, rh, bh, kh, v, wc, s0, *, n_seq, seq_len, n_par, rows, chunk, per_chunk_state, group,
          name, cast=()):
    t = n_seq * seq_len
    n_blk = rows // chunk
    if per_chunk_state:
        assert seq_len == chunk and n_par == 1
        lead = 1
        grid = (t // rows,)
        rmap = lambda i: (0, i, 0)
        cmap = lambda i: (0, i, 0, 0)
        smap = lambda i: (i, 0, 0, 0)
        s0map = smap
        state_block = (n_blk, N_HEADS // 2, HEAD_DIM, 2 * HEAD_DIM)
        s0_block = state_block
        wc_block = (1, n_blk, 1, G)
        sem = ("parallel",)
    else:
        assert rows == chunk and s0.shape[0] == 1 and n_seq % n_par == 0
        lead = n_seq
        grid = (n_seq // n_par, seq_len // chunk)
        rmap = lambda s, c: (s, c, 0)
        cmap = lambda s, c: (s, c, 0, 0)
        smap = lambda s, c: (s, 0, 0, 0)
        s0map = lambda s, c: (0, 0, 0, 0)
        state_block = (n_par, N_HEADS, HEAD_DIM, HEAD_DIM)
        s0_block = (1, N_HEADS, HEAD_DIM, HEAD_DIM)
        wc_block = (n_par, 1, 1, G)
        sem = ("parallel", "arbitrary")
    rows3 = lambda x: x.reshape(lead, t // lead, G)
    row_spec = pl.BlockSpec((n_par, rows, G), rmap)
    n_steps = 1
    for extent in grid:
        n_steps *= extent
    step = (lambda i: i) if len(grid) == 1 else (lambda s, c: s * grid[1] + c)
    cast_specs = []
    for w in cast:
        assert w.shape[0] % (16 * n_steps) == 0
        cast_specs.append(pl.BlockSpec((w.shape[0] // n_steps, w.shape[1]), lambda *g: (step(*g), 0)))
    y, s_out, *narrowed = pl.pallas_call(
        functools.partial(_scan_kernel, n_par=n_par, rows=rows, chunk=chunk,
                          per_chunk_state=per_chunk_state, group=group, n_cast=len(cast)),
        grid=grid,
        in_specs=([row_spec] * 5 + [pl.BlockSpec(wc_block, cmap), pl.BlockSpec(s0_block, s0map)]
                  + cast_specs),
        out_specs=[row_spec, pl.BlockSpec(state_block, smap)] + cast_specs,
        out_shape=[jax.ShapeDtypeStruct((lead, t // lead, G), F32),
                   jax.ShapeDtypeStruct((n_seq,) + state_block[1:], F32)]
                  + [jax.ShapeDtypeStruct(w.shape, BF16) for w in cast],
        compiler_params=pltpu.CompilerParams(dimension_semantics=sem, vmem_limit_bytes=VMEM_LIMIT),
        name=name,
    )(rows3(ah), rows3(rh), rows3(bh), rows3(kh), rows3(v), wc.reshape(lead, -1, 1, G), s0, *cast)
    return (y.reshape(t, G), s_out, *narrowed)


def _mix_kernel(*refs, segs):
    n = len(segs)
    row_refs = [refs[4 * s:4 * s + 4] for s in range(n)]
    x_ref, lnw_ref, lnb_ref, bd_ref, wo_ref, o_ref = refs[4 * n:]
    i = pl.program_id(0)

    def body(y_ref, bonus_ref, g_ref, conv_ref):
        bd = bd_ref[...]
        y = y_ref[...]
        mu = _head_sum(y, bd) * (1.0 / HEAD_DIM)
        d = y - mu
        var = _head_sum(d * d, bd) * (1.0 / HEAD_DIM)
        yn = d * lax.rsqrt(var + GN_EPS) * lnw_ref[...] + lnb_ref[...]
        rw = ((yn + bonus_ref[...]) * g_ref[...]).astype(BF16)
        mix = _dot(rw, wo_ref[0:G, :]) + _dot(conv_ref[...], wo_ref[G:2 * G, :])
        o_ref[...] = x_ref[...] + mix

    for rr, (off, cnt) in zip(row_refs, segs):
        @pl.when((i >= off) & (i < off + cnt))
        def _(rr=rr):
            body(*rr)


def _mix(row_groups, x1, lnw, lnb, bd, wo):
    segs, n_tiles = _segments([grp[0] for grp in row_groups], TM_MIX)
    cmap = lambda i: (0, 0)
    in_specs, args = [], []
    for grp, (off, cnt) in zip(row_groups, segs):
        smap = lambda i, off=off, cnt=cnt: (jnp.clip(i - off, 0, cnt - 1), 0)
        in_specs += [pl.BlockSpec((TM_MIX, G), smap)] * 4
        args += list(grp)
    in_specs += [pl.BlockSpec((TM_MIX, D_MODEL), lambda i: (i, 0)),
                 pl.BlockSpec((1, G), cmap), pl.BlockSpec((1, G), cmap),
                 pl.BlockSpec((G, G), cmap), pl.BlockSpec((D_MODEL, D_MODEL), cmap)]
    return pl.pallas_call(
        functools.partial(_mix_kernel, segs=segs),
        grid=(n_tiles,), in_specs=in_specs,
        out_specs=pl.BlockSpec((TM_MIX, D_MODEL), lambda i: (i, 0)),
        out_shape=jax.ShapeDtypeStruct((n_tiles * TM_MIX, D_MODEL), F32),
        compiler_params=pltpu.CompilerParams(
            dimension_semantics=("parallel",), vmem_limit_bytes=VMEM_LIMIT),
        name="mix",
    )(*args, x1, lnw, lnb, bd, wo)


def _pack_rwkv(a):
    r, wd, k, v, ad, gd = jnp.split(a, RWKV_SPLITS, axis=-1)
    zw = jnp.zeros(a.shape[:-1] + (LORA_PAD - W_LORA,), a.dtype)
    za = jnp.zeros(a.shape[:-1] + (LORA_PAD - A_LORA,), a.dtype)
    return jnp.concatenate([r, k, v, wd, zw, ad, za, gd], axis=-1)


def _unpack_rwkv(p):
    return jnp.concatenate([
        p[..., 0:G], p[..., OFF_WD:OFF_WD + W_LORA], p[..., G:2 * G], p[..., 2 * G:3 * G],
        p[..., OFF_AD:OFF_AD + A_LORA], p[..., OFF_GD:OFF_GD + G_LORA]], axis=-1)


def _block_tri(n, chunk):
    i = jnp.arange(n)
    return ((i[:, None] // chunk == i[None, :] // chunk) & (i[:, None] >= i[None, :])).astype(BF16)


def kernel(x_prompt, x_sample, state_wkv, state_shift, state_conv, meta_tokens, g_ffn1, ffn1_gate, ffn1_up, ffn1_down, g_mix, w_in, mu_shift, w0, w_lora_w, a0, w_lora_a, w_lora_g, k_k, k_a, r_k, ln_x_w, ln_x_b, conv_w, w_out, g_ffn2, ffn2_gate, ffn2_up, ffn2_down, g_final):
    assert g_ffn1.shape[0] == 1, "single layer"
    nb, seq, _ = x_prompt.shape
    db, dseq, _ = x_sample.shape
    assert dseq == C_SAMPLE and N_META <= C_PROMPT and seq % ROWS_PREP == 0
    tp, ts = nb * seq, db * dseq
    assert tp % TM == 0 and ts % TM == 0

    row = lambda a: a.reshape(1, -1).astype(F32)
    w_in_p = _pack_w_in(jnp.transpose(w_in[0]))
    pad_rows = lambda w, n: jnp.concatenate([w, jnp.zeros((n - w.shape[0], w.shape[1]), w.dtype)], axis=0)
    hid = jnp.arange(G) // HEAD_DIM
    bd = (hid[:, None] == hid[None, :]).astype(BF16)
    prep_w = (_pack_rwkv(mu_shift[0])[None], row(w0[0]), pad_rows(w_lora_w[0], LORA_PAD).astype(BF16),
              row(a0[0]), pad_rows(w_lora_a[0], LORA_PAD).astype(BF16), w_lora_g[0].astype(BF16),
              row(k_k[0]), row(k_a[0]), row(r_k[0]), conv_w[0].astype(F32), bd)

    x_meta = jnp.concatenate([jnp.zeros((C_PROMPT - N_META, D_MODEL), F32), meta_tokens.astype(F32)], axis=0)
    x1_meta, wg1, wu1, wd1 = _ffn([x_meta], row(g_ffn1[0]), ffn1_gate[0], ffn1_up[0], ffn1_down[0], None,
                                  [C_PROMPT], name="ffn1_meta", tm=C_PROMPT)
    proj_meta = _proj(x1_meta, row(g_mix[0]), w_in_p, name="proj_meta", tm=C_PROMPT)
    x1 = _ffn_two_stage([x_prompt.reshape(tp, D_MODEL), x_sample.reshape(ts, D_MODEL)], row(g_ffn1[0]),
                        wg1, wu1, wd1, None, [tp + ts], name="ffn1")[0]
    proj = _proj(x1, row(g_mix[0]), w_in_p, name="proj")

    tri_p = _block_tri(ROWS_PREP, C_PROMPT)
    zeros_state = jnp.zeros((1, N_HEADS, HEAD_DIM, HEAD_DIM), F32)
    (ah, rh, bh, kh, vv, wc, _, _, _, utail_m, ptail_m) = _prep(
        proj_meta, prep_w, tri_p[:C_PROMPT, :C_PROMPT], n_seq=1, seq_len=C_PROMPT, rows=C_PROMPT,
        chunk=C_PROMPT, row_block_offset=0, extra=(jnp.zeros((8, RW), F32), jnp.zeros((8, G), F32)),
        sample=False, name="prep_meta")
    _, wkv_m = _scan(ah, rh, bh, kh, vv, wc, zeros_state, n_seq=1, seq_len=C_PROMPT, n_par=1,
                     rows=C_PROMPT, chunk=C_PROMPT, per_chunk_state=False, group=N_HEADS,
                     name="scan_meta")

    (ah, rh, bh, kh, vv, wc, g_p, bonus_p, conv_p, utail_p, ptail_p) = _prep(
        proj, prep_w, tri_p, n_seq=nb, seq_len=seq, rows=ROWS_PREP, chunk=C_PROMPT,
        row_block_offset=0, extra=(ptail_m[0], utail_m[0]), sample=False, name="prep_prompt")
    y_p, wkv_p, wg2, wu2, wd2, wo = _scan(
        ah, rh, bh, kh, vv, wc, wkv_m, n_seq=nb, seq_len=seq, n_par=SCAN_PAR, rows=C_PROMPT,
        chunk=C_PROMPT, per_chunk_state=False, group=SCAN_PAR * N_HEADS // 2, name="scan_prompt",
        cast=(ffn2_gate[0], ffn2_up[0], ffn2_down[0], w_out[0]))

    prev = _pack_rwkv(state_shift[0])
    up1 = state_conv[0][:, 1]
    up2 = state_conv[0][:, 0]
    (ah, rh, bh, kh, vv, wc, g_s, bonus_s, conv_s, u_s, plast_s) = _prep(
        proj, prep_w, _block_tri(ROWS_PREP, C_SAMPLE), n_seq=db, seq_len=dseq, rows=ROWS_PREP,
        chunk=C_SAMPLE, row_block_offset=tp // ROWS_PREP, extra=(prev, up1, up2), sample=True,
        name="prep_sample")
    pair = (db, N_HEADS // 2, 2, HEAD_DIM, HEAD_DIM)
    s0_pairs = jnp.transpose(state_wkv[0].reshape(pair), (0, 1, 3, 2, 4)).reshape(
        db, N_HEADS // 2, HEAD_DIM, 2 * HEAD_DIM)
    y_s, wkv_s = _scan(ah, rh, bh, kh, vv, wc, s0_pairs, n_seq=db, seq_len=dseq, n_par=1,
                       rows=ROWS_SCAN_SAMPLE, chunk=C_SAMPLE, per_chunk_state=True, group=4,
                       name="scan_sample")

    x2 = _mix([(y_p, bonus_p, g_p, conv_p), (y_s, bonus_s, g_s, conv_s)], x1,
              row(ln_x_w[0]), row(ln_x_b[0]), bd, wo)
    y_prompt, y_sample = _ffn_two_stage([x2], row(g_ffn2[0]), wg2, wu2, wd2, row(g_final), [tp, ts],
                                        name="ffn2")

    shift_p = _unpack_rwkv(ptail_p.reshape(nb, seq // ROWS_PREP, 8, RW)[:, -1, 7, :])
    conv_state_p = utail_p.reshape(nb, seq // ROWS_PREP, 8, G)[:, -1, 6:, :]
    shift_s = _unpack_rwkv(plast_s)
    conv_state_s = u_s.reshape(db, dseq, G)[:, -2:, :]
    wkv_s_out = jnp.transpose(wkv_s.reshape(db, N_HEADS // 2, HEAD_DIM, 2, HEAD_DIM),
                              (0, 1, 3, 2, 4)).reshape(db, N_HEADS, HEAD_DIM, HEAD_DIM)
    return (y_prompt.reshape(nb, seq, D_MODEL), y_sample.reshape(db, dseq, D_MODEL),
            wkv_p[None].astype(state_wkv.dtype), shift_p[None].astype(state_shift.dtype),
            conv_state_p[None].astype(state_conv.dtype),
            wkv_s_out[None].astype(state_wkv.dtype), shift_s[None].astype(state_shift.dtype),
            conv_state_s[None].astype(state_conv.dtype))
```

```python
import functools

import jax
import jax.numpy as jnp
from jax import lax
from jax.experimental import pallas as pl
from jax.experimental.pallas import tpu as pltpu

F32 = jnp.float32
BF16 = jnp.bfloat16

D_MODEL = 2048
D_FF = 5632
N_META = 16
G = 1024
HEAD_DIM = 64
N_HEADS = G // HEAD_DIM
W_LORA = 96
A_LORA = 96
G_LORA = 256
LORA_PAD = 128
RWKV_PROJ = 3 * G + W_LORA + A_LORA + G_LORA
RWKV_SPLITS = (G, G + W_LORA, 2 * G + W_LORA, 3 * G + W_LORA, 3 * G + W_LORA + A_LORA)
RW = 3 * G + 2 * LORA_PAD + G_LORA
PW = RW + 3 * G
OFF_WD, OFF_AD, OFF_GD = 3 * G, 3 * G + LORA_PAD, 3 * G + 2 * LORA_PAD
RMS_EPS = 1e-6
GN_EPS = 64e-5
EXP_M05_LOG2E = 0.6065306597126334 * 1.4426950408889634

C_PROMPT = 64
C_SAMPLE = 8
TM = 512
TF = 512
TN = 3328
TM_FFN2 = 256
TM_MIX = 256
ROWS_PREP = 256
ROWS_SCAN_SAMPLE = 128
SCAN_PAR = 4
VMEM_LIMIT = 56 * 1024 * 1024


def _dot(a, b):
    return jnp.dot(a, b, preferred_element_type=F32)


def _dot_nt(a, b):
    return lax.dot_general(a, b, (((1,), (1,)), ((), ())), preferred_element_type=F32)


def _dot_tn(a, b):
    return lax.dot_general(a, b, (((0,), (0,)), ((), ())), preferred_element_type=F32)


def _split(a):
    hi = a.astype(BF16)
    lo = (a - hi.astype(F32)).astype(BF16)
    return hi, lo


def _head_sum(x, bd):
    return _dot(x.astype(BF16), bd)


def _dot_split_rhs(a_bf16, b):
    hi, lo = _split(b)
    return _dot(a_bf16, hi) + _dot(a_bf16, lo)


def _rms(x, g):
    return x * lax.rsqrt(jnp.mean(x * x, axis=-1, keepdims=True) + RMS_EPS) * g


def _segments(arrays, tile):
    segs, off = [], 0
    for arr in arrays:
        n = arr.shape[0] // tile
        assert n * tile == arr.shape[0]
        segs.append((off, n))
        off += n
    return segs, off


def _seg_spec(tile, width, seg):
    off, n = seg
    return pl.BlockSpec((tile, width), lambda i, j: (jnp.clip(i - off, 0, n - 1), 0))


def _overlaps(in_segs, out_segs):
    for a, (ao, an) in enumerate(in_segs):
        for b, (bo, bn) in enumerate(out_segs):
            lo, hi = max(ao, bo), min(ao + an, bo + bn)
            if lo < hi:
                yield a, b, lo, hi


def _ffn_kernel(*refs, in_segs, out_segs, final_norm, narrow):
    n_in, n_out = len(in_segs), len(out_segs)
    x_refs = refs[:n_in]
    g_ref, wg_ref, wu_ref, wd_ref = refs[n_in:n_in + 4]
    rest = refs[n_in + 4:]
    if final_norm:
        gf_ref, rest = rest[0], rest[1:]
    o_refs = rest[:n_out]
    rest = rest[n_out:]
    if narrow:
        w_out_refs, rest = rest[:3], rest[3:]
    h_ref, acc_ref = rest
    i = pl.program_id(0)
    j = pl.program_id(1)

    for x_ref, (off, n) in zip(x_refs, in_segs):
        @pl.when((j == 0) & (i >= off) & (i < off + n))
        def _(x_ref=x_ref):
            h_ref[...] = _rms(x_ref[...], g_ref[...]).astype(BF16)
            acc_ref[...] = jnp.zeros_like(acc_ref)

    wg, wu, wd = wg_ref[...], wu_ref[...], wd_ref[...]
    if narrow:
        wg, wu, wd = wg.astype(BF16), wu.astype(BF16), wd.astype(BF16)
        for o_ref, w in zip(w_out_refs, (wg, wu, wd)):
            o_ref[...] = w
    h = h_ref[...]
    gate = _dot(h, wg)
    up = _dot(h, wu)
    act = (gate * jax.nn.sigmoid(gate) * up).astype(BF16)
    acc_ref[...] += _dot(act, wd)

    for a, b, lo, hi in _overlaps(in_segs, out_segs):
        @pl.when((j == pl.num_programs(1) - 1) & (i >= lo) & (i < hi))
        def _(x_ref=x_refs[a], o_ref=o_refs[b]):
            xo = x_ref[...] + 0.5 * acc_ref[...]
            if final_norm:
                xo = _rms(xo, gf_ref[...])
            o_ref[...] = xo


def _ffn(xs, g, wg, wu, wd, g_final, out_rows, *, name, tm=TM, tf=TF):
    final_norm = g_final is not None
    narrow = wg.dtype == F32
    in_segs, n_tiles_in = _segments(xs, tm)
    out_shape = [jax.ShapeDtypeStruct((n, D_MODEL), F32) for n in out_rows]
    out_segs, n_tiles = _segments(out_shape, tm)
    assert n_tiles <= n_tiles_in and (n_tiles == 1 or not narrow)
    cmap = lambda i, j: (0, 0)
    w_specs = [pl.BlockSpec((D_MODEL, tf), lambda i, j: (0, j)),
               pl.BlockSpec((D_MODEL, tf), lambda i, j: (0, j)),
               pl.BlockSpec((tf, D_MODEL), lambda i, j: (j, 0))]
    in_specs = [_seg_spec(tm, D_MODEL, s) for s in in_segs] + [pl.BlockSpec((1, D_MODEL), cmap)] + w_specs
    args = list(xs) + [g, wg, wu, wd]
    if final_norm:
        in_specs.append(pl.BlockSpec((1, D_MODEL), cmap))
        args.append(g_final)
    out_specs = [_seg_spec(tm, D_MODEL, s) for s in out_segs]
    if narrow:
        out_specs += w_specs
        out_shape = out_shape + [jax.ShapeDtypeStruct(w.shape, BF16) for w in (wg, wu, wd)]
    return pl.pallas_call(
        functools.partial(_ffn_kernel, in_segs=in_segs, out_segs=out_segs, final_norm=final_norm,
                          narrow=narrow),
        grid=(n_tiles, D_FF // tf),
        in_specs=in_specs,
        out_specs=out_specs,
        out_shape=out_shape,
        scratch_shapes=[pltpu.VMEM((tm, D_MODEL), BF16), pltpu.VMEM((tm, D_MODEL), F32)],
        compiler_params=pltpu.CompilerParams(
            dimension_semantics=("arbitrary", "arbitrary"), vmem_limit_bytes=VMEM_LIMIT),
        name=name,
    )(*args)


def _ffn_up_kernel(*refs, in_segs):
    n_in = len(in_segs)
    x_refs = refs[:n_in]
    g_ref, wg_ref, wu_ref, a_ref = refs[n_in:]
    i = pl.program_id(1)
    for x_ref, (off, n) in zip(x_refs, in_segs):
        @pl.when((i >= off) & (i < off + n))
        def _(x_ref=x_ref):
            h = _rms(x_ref[...], g_ref[...]).astype(BF16)
            gate = _dot(h, wg_ref[...])
            up = _dot(h, wu_ref[...])
            a_ref[...] = (gate * jax.nn.sigmoid(gate) * up).astype(BF16)


def _ffn_down_kernel(*refs, in_segs, out_segs, final_norm, pack_steps):
    n_in, n_out = len(in_segs), len(out_segs)
    a_ref = refs[0]
    x_refs = refs[1:1 + n_in]
    wd_ref = refs[1 + n_in]
    rest = refs[2 + n_in:]
    if final_norm:
        gf_ref, rest = rest[0], rest[1:]
    if pack_steps:
        wt_ref, rest = rest[0], rest[1:]
    o_refs = rest[:n_out]
    i = pl.program_id(0)
    if pack_steps:
        @pl.when(i < pack_steps)
        def _():
            _pack_w_in_kernel(wt_ref, rest[n_out])

    for a, b, lo, hi in _overlaps(in_segs, out_segs):
        @pl.when((i >= lo) & (i < hi))
        def _(x_ref=x_refs[a], o_ref=o_refs[b]):
            xo = x_ref[...] + 0.5 * _dot(a_ref[...], wd_ref[...])
            if final_norm:
                xo = _rms(xo, gf_ref[...])
            o_ref[...] = xo


def _ffn_two_stage(xs, g, wg, wu, wd, g_final, out_rows, *, name, pack_src=None):
    final_norm = g_final is not None
    tm, tn = TM_FFN2, D_FF // 2
    in_segs, n_tiles = _segments(xs, tm)
    out_shape = [jax.ShapeDtypeStruct((n, D_MODEL), F32) for n in out_rows]
    out_segs, n_out_tiles = _segments(out_shape, tm)
    assert n_out_tiles == n_tiles
    seg_ji = lambda s: pl.BlockSpec((tm, D_MODEL), lambda j, i, s=s: (jnp.clip(i - s[0], 0, s[1] - 1), 0))
    half = pl.BlockSpec((D_MODEL, tn), lambda j, i: (0, j), pipeline_mode=pl.Buffered(1))
    act = pl.pallas_call(
        functools.partial(_ffn_up_kernel, in_segs=in_segs),
        grid=(D_FF // tn, n_tiles),
        in_specs=[seg_ji(s) for s in in_segs] + [pl.BlockSpec((1, D_MODEL), lambda j, i: (0, 0)), half, half],
        out_specs=pl.BlockSpec((tm, tn), lambda j, i: (i, j)),
        out_shape=jax.ShapeDtypeStruct((n_tiles * tm, D_FF), BF16),
        compiler_params=pltpu.CompilerParams(
            dimension_semantics=("arbitrary", "arbitrary"), vmem_limit_bytes=VMEM_LIMIT),
        name=name + "_up",
    )(*xs, g, wg, wu)

    seg_i = lambda s: pl.BlockSpec((tm, D_MODEL), lambda i, s=s: (jnp.clip(i - s[0], 0, s[1] - 1), 0))
    in_specs = ([pl.BlockSpec((tm, D_FF), lambda i: (i, 0))] + [seg_i(s) for s in in_segs]
                + [pl.BlockSpec((D_FF, D_MODEL), lambda i: (0, 0), pipeline_mode=pl.Buffered(1))])
    args = [act] + list(xs) + [wd]
    if final_norm:
        in_specs.append(pl.BlockSpec((1, D_MODEL), lambda i: (0, 0)))
        args.append(g_final)
    out_specs = [seg_i(s) for s in out_segs]
    pack_steps = 0
    if pack_src is not None:
        lane_blk = 128
        pack_steps = D_MODEL // lane_blk
        assert pack_steps <= n_tiles
        blk = lambda i: (0, jnp.minimum(i, pack_steps - 1))
        in_specs.append(pl.BlockSpec((pack_src.shape[0], lane_blk), blk))
        args.append(pack_src)
        out_specs.append(pl.BlockSpec((PW, lane_blk), blk))
        out_shape = out_shape + [jax.ShapeDtypeStruct((PW, D_MODEL), BF16)]
    return pl.pallas_call(
        functools.partial(_ffn_down_kernel, in_segs=in_segs, out_segs=out_segs, final_norm=final_norm,
                          pack_steps=pack_steps),
        grid=(n_tiles,),
        in_specs=in_specs,
        out_specs=out_specs,
        out_shape=out_shape,
        compiler_params=pltpu.CompilerParams(
            dimension_semantics=("arbitrary",), vmem_limit_bytes=VMEM_LIMIT),
        name=name + "_down",
    )(*args)


_PACK_MOVES = (
    (0, 0, G),
    (G, RWKV_SPLITS[1], G),
    (2 * G, RWKV_SPLITS[2], G),
    (OFF_WD, RWKV_SPLITS[0], W_LORA),
    (OFF_AD, RWKV_SPLITS[3], A_LORA),
    (OFF_GD, RWKV_SPLITS[4], G_LORA),
    (RW, RWKV_PROJ, 3 * G),
)


def _pack_w_in_kernel(w_ref, o_ref):
    for dst, src, n in _PACK_MOVES:
        o_ref[dst:dst + n, :] = w_ref[src:src + n, :].astype(BF16)
    for lo, hi in ((OFF_WD + W_LORA, OFF_AD), (OFF_AD + A_LORA, OFF_GD)):
        o_ref[lo:hi, :] = jnp.zeros((hi - lo, o_ref.shape[1]), BF16)


def _proj_kernel(*refs, in_segs):
    n_in = len(in_segs)
    x_refs = refs[:n_in]
    g_ref, w_ref, o_ref = refs[n_in:]
    i = pl.program_id(1)
    for x_ref, (off, n) in zip(x_refs, in_segs):
        @pl.when((i >= off) & (i < off + n))
        def _(x_ref=x_ref):
            h = _rms(x_ref[...], g_ref[...]).astype(BF16)
            o_ref[...] = _dot_nt(h, w_ref[...])


def _proj(xs, g, w, *, name):
    in_segs, n_tiles = _segments(xs, TM)
    seg_ji = lambda s: pl.BlockSpec((TM, D_MODEL), lambda j, i, s=s: (jnp.clip(i - s[0], 0, s[1] - 1), 0),
                                    pipeline_mode=pl.Buffered(1 if s[1] == 1 else 2))
    return pl.pallas_call(
        functools.partial(_proj_kernel, in_segs=in_segs),
        grid=(PW // TN, n_tiles),
        in_specs=[seg_ji(s) for s in in_segs] + [
            pl.BlockSpec((1, D_MODEL), lambda j, i: (0, 0)),
            pl.BlockSpec((TN, D_MODEL), lambda j, i: (j, 0)),
        ],
        out_specs=pl.BlockSpec((TM, TN), lambda j, i: (i, j)),
        out_shape=jax.ShapeDtypeStruct((n_tiles * TM, PW), F32),
        compiler_params=pltpu.CompilerParams(
            dimension_semantics=("parallel", "parallel"), vmem_limit_bytes=VMEM_LIMIT),
        name=name,
    )(*xs, g, w)


def _prep_kernel(*refs, rows, chunk, sample):
    if sample:
        (p_ref, prev_ref, up1_ref, up2_ref, mu_ref, w0_ref, ww_ref, a0_ref, wa_ref, wgl_ref,
         kk_ref, ka_ref, rk_ref, cw_ref, bd_ref, tri_ref,
         ah_ref, rh_ref, bh_ref, kh_ref, v_ref, wc_ref, g_ref, bonus_ref, conv_ref, u_ref,
         plast_ref) = refs
    else:
        (p_ref, cp0_ref, cu0_ref, mu_ref, w0_ref, ww_ref, a0_ref, wa_ref, wgl_ref,
         kk_ref, ka_ref, rk_ref, cw_ref, bd_ref, tri_ref,
         ah_ref, rh_ref, bh_ref, kh_ref, v_ref, wc_ref, g_ref, bonus_ref, conv_ref, u_ref,
         ptail_ref, carry_p, carry_u) = refs

        @pl.when(pl.program_id(1) == 0)
        def _():
            carry_p[...] = cp0_ref[...]
            carry_u[...] = cu0_ref[...]

    def row_ids(width):
        r = lax.broadcasted_iota(jnp.int32, (rows, width), 0)
        return (r & 7) if sample else r

    def per_sequence(x):
        n, w = x.shape
        return jnp.broadcast_to(x[:, None, :], (n, 8, w)).reshape(n * 8, w)

    def shifted_mix(lo, hi):
        p = p_ref[:, lo:hi]
        rolled = pltpu.roll(p, 1, 0)
        if sample:
            sh = jnp.where(row_ids(hi - lo) == 0, per_sequence(prev_ref[:, lo:hi]), rolled)
        else:
            sh = jnp.where(row_ids(hi - lo) == 0, carry_p[7:8, lo:hi], rolled)
        return p + (sh - p) * mu_ref[:, lo:hi]

    r = shifted_mix(0, G)
    k = shifted_mix(G, 2 * G)
    v = shifted_mix(2 * G, 3 * G)
    wd = shifted_mix(OFF_WD, OFF_WD + LORA_PAD)
    ad = shifted_mix(OFF_AD, OFF_AD + LORA_PAD)
    gd = shifted_mix(OFF_GD, OFF_GD + G_LORA)

    z = w0_ref[...] + _dot(jnp.tanh(wd).astype(BF16), ww_ref[...])
    lw = -EXP_M05_LOG2E * jax.nn.sigmoid(z)
    a = jax.nn.sigmoid(a0_ref[...] + _dot(ad.astype(BF16), wa_ref[...]))
    g_ref[...] = _dot(jax.nn.sigmoid(gd).astype(BF16), wgl_ref[...]).astype(BF16)

    bd = bd_ref[...]
    kk = k * kk_ref[...]
    norm = jnp.sqrt(_head_sum(kk * kk, bd))
    kk = kk / jnp.maximum(norm, 1e-12)
    km = k * (1.0 + (a - 1.0) * ka_ref[...])
    bonus_ref[...] = _head_sum(r * km * rk_ref[...], bd) * v

    cum = _dot_split_rhs(tri_ref[...], lw)
    e_cum = jnp.exp2(cum)
    e_inv = jnp.exp2(-cum)
    ah_ref[...] = (-kk * jnp.exp2(cum - lw)).astype(BF16)
    rh_ref[...] = (r * e_cum).astype(BF16)
    bh_ref[...] = (kk * a * e_inv).astype(BF16)
    kh_ref[...] = (km * e_inv).astype(BF16)
    v_ref[...] = v.astype(BF16)
    for c in range(rows // chunk):
        wc_ref[c] = e_cum[(c + 1) * chunk - 1:(c + 1) * chunk, :]

    bg = p_ref[:, RW:RW + G]
    u = p_ref[:, RW + G:RW + 2 * G] * p_ref[:, RW + 2 * G:RW + 3 * G]
    u1 = pltpu.roll(u, 1, 0)
    u2 = pltpu.roll(u, 2, 0)
    rid = row_ids(G)
    if sample:
        s1 = per_sequence(up1_ref[...])
        um1 = jnp.where(rid == 0, s1, u1)
        um2 = jnp.where(rid == 0, per_sequence(up2_ref[...]), jnp.where(rid == 1, s1, u2))
        u_ref[...] = u
        plast_ref[...] = p_ref[:, 0:RW].reshape(rows // 8, 8, RW)[:, 7, :]
    else:
        um1 = jnp.where(rid == 0, carry_u[7:8, :], u1)
        um2 = jnp.where(rid == 0, carry_u[6:7, :], jnp.where(rid == 1, carry_u[7:8, :], u2))
        u_ref[0] = u[rows - 8:, :]
        ptail_ref[0] = p_ref[rows - 8:rows, 0:RW]
    conv = cw_ref[0:1, :] * um2 + cw_ref[1:2, :] * um1 + cw_ref[2:3, :] * u
    conv_ref[...] = (bg * conv).astype(BF16)

    if not sample:
        carry_p[...] = p_ref[rows - 8:rows, 0:RW]
        carry_u[...] = u[rows - 8:, :]


def _prep(proj, weights, tri, *, n_seq, seq_len, rows, chunk, row_block_offset, extra, sample, name):
    t = n_seq * seq_len
    n_chunks = t // chunk
    cpt = rows // chunk
    if sample:
        grid = (t // rows,)
        rmap = lambda i: (i + row_block_offset, 0)
        omap = lambda i: (i, 0)
        omap3 = lambda i: (i, 0, 0)
        cmap = lambda i: (0, 0)
        sem = ("parallel",)
    else:
        tiles = seq_len // rows
        grid = (n_seq, tiles)
        rmap = lambda b, j: (b * tiles + j + row_block_offset, 0)
        omap = lambda b, j: (b * tiles + j, 0)
        omap3 = lambda b, j: (b * tiles + j, 0, 0)
        cmap = lambda b, j: (0, 0)
        sem = ("parallel", "arbitrary")

    const = lambda arr: pl.BlockSpec(arr.shape, cmap)
    in_specs = [pl.BlockSpec((rows, PW), rmap)]
    if sample:
        in_specs += [pl.BlockSpec((rows // 8, RW), omap), pl.BlockSpec((rows // 8, G), omap),
                     pl.BlockSpec((rows // 8, G), omap)]
    else:
        in_specs += [const(e) for e in extra]
    in_specs += [const(w) for w in weights] + [const(tri)]
    args = [proj] + list(extra) + list(weights) + [tri]

    row_out = jax.ShapeDtypeStruct((t, G), F32)
    row_spec = pl.BlockSpec((rows, G), omap)
    row_bf16 = jax.ShapeDtypeStruct((t, G), BF16)
    out_shape = [row_bf16] * 5 + [jax.ShapeDtypeStruct((n_chunks, 1, G), F32), row_bf16, row_out,
                                  row_bf16]
    out_specs = [row_spec] * 5 + [pl.BlockSpec((cpt, 1, G), omap3), row_spec, row_spec, row_spec]
    if sample:
        out_shape += [row_out, jax.ShapeDtypeStruct((t // 8, RW), F32)]
        out_specs += [row_spec, pl.BlockSpec((rows // 8, RW), omap)]
        scratch = []
    else:
        out_shape += [jax.ShapeDtypeStruct((t // rows, 8, G), F32),
                      jax.ShapeDtypeStruct((t // rows, 8, RW), F32)]
        out_specs += [pl.BlockSpec((1, 8, G), omap3), pl.BlockSpec((1, 8, RW), omap3)]
        scratch = [pltpu.VMEM((8, RW), F32), pltpu.VMEM((8, G), F32)]
    return pl.pallas_call(
        functools.partial(_prep_kernel, rows=rows, chunk=chunk, sample=sample),
        grid=grid, in_specs=in_specs, out_specs=out_specs, out_shape=out_shape,
        scratch_shapes=scratch,
        compiler_params=pltpu.CompilerParams(dimension_semantics=sem, vmem_limit_bytes=VMEM_LIMIT),
        name=name,
    )(*args)


def _scan_kernel(*refs, n_par, rows, chunk, per_chunk_state, group, n_cast):
    ah_ref, rh_ref, bh_ref, kh_ref, v_ref, wc_ref, s0_ref = refs[:7]
    y_ref, s_ref = refs[7 + n_cast:9 + n_cast]
    for w_ref, o_ref in zip(refs[7:7 + n_cast], refs[9 + n_cast:]):
        o_ref[...] = w_ref[...].astype(BF16)

    if per_chunk_state:
        s_in = s0_ref
    else:
        s_in = s_ref

        @pl.when(pl.program_id(1) == 0)
        def _():
            for q in range(n_par):
                s_ref[q] = s0_ref[0]

    n_blk = rows // chunk
    lanes = [slice(h * HEAD_DIM, (h + 1) * HEAD_DIM) for h in range(N_HEADS)]
    bf = lambda x: x.astype(BF16)

    shift = chunk.bit_length() - 1
    rs = lax.broadcasted_iota(jnp.int32, (rows, rows), 0)
    cs = lax.broadcasted_iota(jnp.int32, (rows, rows), 1)
    same = (rs >> shift) == (cs >> shift)
    mask_strict = same & (rs > cs)
    ri = lax.broadcasted_iota(jnp.int32, (rows, 2 * rows), 0)
    ci = lax.broadcasted_iota(jnp.int32, (rows, 2 * rows), 1)
    cj = jnp.where(ci >= rows, ci - rows, ci)
    mask_r = ((ri >> shift) == (cj >> shift)) & (ri >= cj)
    eye = (rs == cs).astype(F32)

    def chunk_rows(x, c):
        return x[c * chunk:(c + 1) * chunk, :]

    pair_lane = lax.broadcasted_iota(jnp.int32, (rows, 2 * HEAD_DIM), 1) >> (HEAD_DIM.bit_length() - 1)

    def pair_masked(ref, q, h):
        hp, hh = divmod(h, 2)
        x2 = ref[q, :, 2 * HEAD_DIM * hp:2 * HEAD_DIM * (hp + 1)].astype(F32)
        return jnp.where(pair_lane == hh, x2, 0.0)

    def unit_group(units):
        per_unit = lambda f: {u: f(*u) for u in units}
        v = per_unit(lambda q, h: v_ref[q, :, lanes[h]])
        if n_blk == 1:
            a = per_unit(lambda q, h: ah_ref[q, :, lanes[h]])
            r = per_unit(lambda q, h: rh_ref[q, :, lanes[h]])
            b = per_unit(lambda q, h: bh_ref[q, :, lanes[h]])
            k = per_unit(lambda q, h: kh_ref[q, :, lanes[h]])
        else:
            a = per_unit(lambda q, h: pair_masked(ah_ref, q, h))
            r = per_unit(lambda q, h: pair_masked(rh_ref, q, h))
            b = per_unit(lambda q, h: pair_masked(bh_ref, q, h))
            k = per_unit(lambda q, h: pair_masked(kh_ref, q, h))
            v32 = {u: v[u].astype(F32) for u in units}
        ar = {u: bf(jnp.concatenate([a[u], r[u]], axis=0)) for u in units}
        bk = {u: bf(jnp.concatenate([b[u], k[u]], axis=0)) for u in units}

        gram = {u: _dot(ar[u], jnp.transpose(bk[u])) for u in units}
        l_ab = {u: jnp.where(mask_strict, gram[u][:rows, :rows], 0.0) for u in units}
        l_ak = {u: bf(jnp.where(mask_strict, gram[u][:rows, rows:], 0.0)) for u in units}
        l_r = {u: bf(jnp.where(mask_r, gram[u][rows:, :], 0.0)) for u in units}

        xs, ys = {}, {}
        for u in units:
            q, h = u
            if n_blk == 1:
                st = _dot_nt(ar[u], bf(s_in[q, h]))
                xs[u], ys[u] = st[:rows], st[rows:]
            else:
                parts = [_dot_nt(jnp.concatenate([chunk_rows(a[u], c), chunk_rows(r[u], c)], axis=0),
                                 s_in[c, h // 2]) for c in range(n_blk)]
                xs[u] = jnp.concatenate([p[:chunk] for p in parts], axis=0)
                ys[u] = jnp.concatenate([p[chunk:] for p in parts], axis=0)

        x = {u: xs[u] + _dot(l_ak[u], v[u]) for u in units}

        t_inv = {u: eye + l_ab[u] for u in units}
        n = 2
        if n < chunk:
            pw = {u: bf(l_ab[u]) for u in units}
            pw = {u: bf(_dot(pw[u], pw[u])) for u in units}
        while n < chunk:
            if 2 * n < chunk:
                both = {u: _dot(jnp.concatenate([bf(t_inv[u]), pw[u]], axis=0), pw[u]) for u in units}
                t_inv = {u: t_inv[u] + both[u][:rows] for u in units}
                pw = {u: bf(both[u][rows:]) for u in units}
            else:
                t_inv = {u: t_inv[u] + _dot(bf(t_inv[u]), pw[u]) for u in units}
            n *= 2

        uu = {u: _dot(bf(t_inv[u]), bf(x[u])) for u in units}
        uv = {u: jnp.concatenate([bf(uu[u]), v[u]], axis=0) for u in units}
        for u in units:
            q, h = u
            y_ref[q, :, lanes[h]] = ys[u] + _dot(l_r[u], uv[u])

        if n_blk == 1:
            for u in units:
                q, h = u
                upd = _dot_tn(uv[u], bk[u])
                s_ref[q, h] = (s_in[q, h] + upd) * wc_ref[q, 0][:, lanes[h]]
        else:
            def update(u, c):
                uv_c = jnp.concatenate([chunk_rows(uu[u], c), chunk_rows(v32[u], c)], axis=0)
                bk_c = jnp.concatenate([chunk_rows(b[u], c), chunk_rows(k[u], c)], axis=0)
                return _dot_tn(uv_c, bk_c)

            for q, h in units:
                if h % 2 == 0:
                    assert (q, h + 1) in units
                    hp = h // 2
                    for c in range(n_blk):
                        s_new = s_in[c, hp] + update((q, h), c) + update((q, h + 1), c)
                        s_ref[c, hp] = s_new * wc_ref[q, c][:, 2 * HEAD_DIM * hp:2 * HEAD_DIM * (hp + 1)]

    all_units = [(q, h) for h in range(N_HEADS) for q in range(n_par)]
    for g0 in range(0, len(all_units), group):
        unit_group(all_units[g0:g0 + group])


def _scan(ah, rh, bh, kh, v, wc, s0, *, n_seq, seq_len, n_par, rows, chunk, per_chunk_state, group,
          name, cast=()):
    t = n_seq * seq_len
    n_blk = rows // chunk
    if per_chunk_state:
        assert seq_len == chunk and n_par == 1
        lead = 1
        grid = (t // rows,)
        rmap = lambda i: (0, i, 0)
        cmap = lambda i: (0, i, 0, 0)
        smap = lambda i: (i, 0, 0, 0)
        s0map = smap
        state_block = (n_blk, N_HEADS // 2, HEAD_DIM, 2 * HEAD_DIM)
        s0_block = state_block
        wc_block = (1, n_blk, 1, G)
        sem = ("parallel",)
    else:
        assert rows == chunk and s0.shape[0] == 1 and n_seq % n_par == 0
        lead = n_seq
        grid = (n_seq // n_par, seq_len // chunk)
        rmap = lambda s, c: (s, c, 0)
        cmap = lambda s, c: (s, c, 0, 0)
        smap = lambda s, c: (s, 0, 0, 0)
        s0map = lambda s, c: (0, 0, 0, 0)
        state_block = (n_par, N_HEADS, HEAD_DIM, HEAD_DIM)
        s0_block = (1, N_HEADS, HEAD_DIM, HEAD_DIM)
        wc_block = (n_par, 1, 1, G)
        sem = ("parallel", "arbitrary")
    rows3 = lambda x: x.reshape(lead, t // lead, G)
    row_spec = pl.BlockSpec((n_par, rows, G), rmap)
    n_steps = 1
    for extent in grid:
        n_steps *= extent
    step = (lambda i: i) if len(grid) == 1 else (lambda s, c: s * grid[1] + c)
    cast_specs = []
    for w in cast:
        assert w.shape[0] % (16 * n_steps) == 0
        cast_specs.append(pl.BlockSpec((w.shape[0] // n_steps, w.shape[1]), lambda *g: (step(*g), 0)))
    y, s_out, *narrowed = pl.pallas_call(
        functools.partial(_scan_kernel, n_par=n_par, rows=rows, chunk=chunk,
                          per_chunk_state=per_chunk_state, group=group, n_cast=len(cast)),
        grid=grid,
        in_specs=([row_spec] * 5 + [pl.BlockSpec(wc_block, cmap), pl.BlockSpec(s0_block, s0map)]
                  + cast_specs),
        out_specs=[row_spec, pl.BlockSpec(state_block, smap)] + cast_specs,
        out_shape=[jax.ShapeDtypeStruct((lead, t // lead, G), F32),
                   jax.ShapeDtypeStruct((n_seq,) + state_block[1:], F32)]
                  + [jax.ShapeDtypeStruct(w.shape, BF16) for w in cast],
        compiler_params=pltpu.CompilerParams(dimension_semantics=sem, vmem_limit_bytes=VMEM_LIMIT),
        name=name,
    )(rows3(ah), rows3(rh), rows3(bh), rows3(kh), rows3(v), wc.reshape(lead, -1, 1, G), s0, *cast)
    return (y.reshape(t, G), s_out, *narrowed)


def _mix_kernel(*refs, segs):
    n = len(segs)
    row_refs = [refs[4 * s:4 * s + 4] for s in range(n)]
    x_ref, lnw_ref, lnb_ref, bd_ref, wo_ref, o_ref = refs[4 * n:]
    i = pl.program_id(0)

    def body(y_ref, bonus_ref, g_ref, conv_ref):
        bd = bd_ref[...]
        y = y_ref[...]
        mu = _head_sum(y, bd) * (1.0 / HEAD_DIM)
        d = y - mu
        var = _head_sum(d * d, bd) * (1.0 / HEAD_DIM)
        yn = d * lax.rsqrt(var + GN_EPS) * lnw_ref[...] + lnb_ref[...]
        rw = ((yn + bonus_ref[...]) * g_ref[...]).astype(BF16)
        mix = _dot(rw, wo_ref[0:G, :]) + _dot(conv_ref[...], wo_ref[G:2 * G, :])
        o_ref[...] = x_ref[...] + mix

    for rr, (off, cnt) in zip(row_refs, segs):
        @pl.when((i >= off) & (i < off + cnt))
        def _(rr=rr):
            body(*rr)


def _mix(row_groups, x1, lnw, lnb, bd, wo):
    segs, n_tiles = _segments([grp[0] for grp in row_groups], TM_MIX)
    cmap = lambda i: (0, 0)
    in_specs, args = [], []
    for grp, (off, cnt) in zip(row_groups, segs):
        smap = lambda i, off=off, cnt=cnt: (jnp.clip(i - off, 0, cnt - 1), 0)
        in_specs += [pl.BlockSpec((TM_MIX, G), smap)] * 4
        args += list(grp)
    in_specs += [pl.BlockSpec((TM_MIX, D_MODEL), lambda i: (i, 0)),
                 pl.BlockSpec((1, G), cmap), pl.BlockSpec((1, G), cmap),
                 pl.BlockSpec((G, G), cmap), pl.BlockSpec((D_MODEL, D_MODEL), cmap)]
    return pl.pallas_call(
        functools.partial(_mix_kernel, segs=segs),
        grid=(n_tiles,), in_specs=in_specs,
        out_specs=pl.BlockSpec((TM_MIX, D_MODEL), lambda i: (i, 0)),
        out_shape=jax.ShapeDtypeStruct((n_tiles * TM_MIX, D_MODEL), F32),
        compiler_params=pltpu.CompilerParams(
            dimension_semantics=("parallel",), vmem_limit_bytes=VMEM_LIMIT),
        name="mix",
    )(*args, x1, lnw, lnb, bd, wo)


def _pack_rwkv(a):
    r, wd, k, v, ad, gd = jnp.split(a, RWKV_SPLITS, axis=-1)
    zw = jnp.zeros(a.shape[:-1] + (LORA_PAD - W_LORA,), a.dtype)
    za = jnp.zeros(a.shape[:-1] + (LORA_PAD - A_LORA,), a.dtype)
    return jnp.concatenate([r, k, v, wd, zw, ad, za, gd], axis=-1)


def _unpack_rwkv(p):
    return jnp.concatenate([
        p[..., 0:G], p[..., OFF_WD:OFF_WD + W_LORA], p[..., G:2 * G], p[..., 2 * G:3 * G],
        p[..., OFF_AD:OFF_AD + A_LORA], p[..., OFF_GD:OFF_GD + G_LORA]], axis=-1)


def _block_tri(n, chunk):
    i = jnp.arange(n)
    return ((i[:, None] // chunk == i[None, :] // chunk) & (i[:, None] >= i[None, :])).astype(BF16)


def kernel(x_prompt, x_sample, state_wkv, state_shift, state_conv, meta_tokens, g_ffn1, ffn1_gate, ffn1_up, ffn1_down, g_mix, w_in, mu_shift, w0, w_lora_w, a0, w_lora_a, w_lora_g, k_k, k_a, r_k, ln_x_w, ln_x_b, conv_w, w_out, g_ffn2, ffn2_gate, ffn2_up, ffn2_down, g_final):
    assert g_ffn1.shape[0] == 1, "single layer"
    nb, seq, _ = x_prompt.shape
    db, dseq, _ = x_sample.shape
    assert dseq == C_SAMPLE and N_META <= C_PROMPT and seq % ROWS_PREP == 0
    tp, ts = nb * seq, db * dseq
    assert tp % TM == 0 and ts % TM == 0

    row = lambda a: a.reshape(1, -1).astype(F32)
    pad_rows = lambda w, n: jnp.concatenate([w, jnp.zeros((n - w.shape[0], w.shape[1]), w.dtype)], axis=0)
    hid = jnp.arange(G) // HEAD_DIM
    bd = (hid[:, None] == hid[None, :]).astype(BF16)
    prep_w = (_pack_rwkv(mu_shift[0])[None], row(w0[0]), pad_rows(w_lora_w[0], LORA_PAD).astype(BF16),
              row(a0[0]), pad_rows(w_lora_a[0], LORA_PAD).astype(BF16), w_lora_g[0].astype(BF16),
              row(k_k[0]), row(k_a[0]), row(r_k[0]), conv_w[0].astype(F32), bd)

    x_meta = jnp.concatenate([jnp.zeros((C_PROMPT - N_META, D_MODEL), F32), meta_tokens.astype(F32)], axis=0)
    x1_meta, wg1, wu1, wd1 = _ffn([x_meta], row(g_ffn1[0]), ffn1_gate[0], ffn1_up[0], ffn1_down[0], None,
                                  [C_PROMPT], name="ffn1_meta", tm=C_PROMPT)
    x1, w_in_p = _ffn_two_stage([x_prompt.reshape(tp, D_MODEL), x_sample.reshape(ts, D_MODEL)],
                                row(g_ffn1[0]), wg1, wu1, wd1, None, [tp + ts], name="ffn1",
                                pack_src=jnp.transpose(w_in[0]))
    x1_meta = jnp.concatenate([x1_meta, jnp.zeros((TM - C_PROMPT, D_MODEL), F32)], axis=0)
    proj = _proj([x1, x1_meta], row(g_mix[0]), w_in_p, name="proj")
    meta_row0 = tp + ts

    tri_p = _block_tri(ROWS_PREP, C_PROMPT)
    zeros_state = jnp.zeros((1, N_HEADS, HEAD_DIM, HEAD_DIM), F32)
    (ah, rh, bh, kh, vv, wc, _, _, _, utail_m, ptail_m) = _prep(
        proj, prep_w, tri_p[:C_PROMPT, :C_PROMPT], n_seq=1, seq_len=C_PROMPT, rows=C_PROMPT,
        chunk=C_PROMPT, row_block_offset=meta_row0 // C_PROMPT,
        extra=(jnp.zeros((8, RW), F32), jnp.zeros((8, G), F32)),
        sample=False, name="prep_meta")
    _, wkv_m = _scan(ah, rh, bh, kh, vv, wc, zeros_state, n_seq=1, seq_len=C_PROMPT, n_par=1,
                     rows=C_PROMPT, chunk=C_PROMPT, per_chunk_state=False, group=N_HEADS,
                     name="scan_meta")

    (ah, rh, bh, kh, vv, wc, g_p, bonus_p, conv_p, utail_p, ptail_p) = _prep(
        proj, prep_w, tri_p, n_seq=nb, seq_len=seq, rows=ROWS_PREP, chunk=C_PROMPT,
        row_block_offset=0, extra=(ptail_m[0], utail_m[0]), sample=False, name="prep_prompt")
    y_p, wkv_p, wg2, wu2, wd2, wo = _scan(
        ah, rh, bh, kh, vv, wc, wkv_m, n_seq=nb, seq_len=seq, n_par=SCAN_PAR, rows=C_PROMPT,
        chunk=C_PROMPT, per_chunk_state=False, group=SCAN_PAR * N_HEADS // 2, name="scan_prompt",
        cast=(ffn2_gate[0], ffn2_up[0], ffn2_down[0], w_out[0]))

    prev = _pack_rwkv(state_shift[0])
    up1 = state_conv[0][:, 1]
    up2 = state_conv[0][:, 0]
    (ah, rh, bh, kh, vv, wc, g_s, bonus_s, conv_s, u_s, plast_s) = _prep(
        proj, prep_w, _block_tri(ROWS_PREP, C_SAMPLE), n_seq=db, seq_len=dseq, rows=ROWS_PREP,
        chunk=C_SAMPLE, row_block_offset=tp // ROWS_PREP, extra=(prev, up1, up2), sample=True,
        name="prep_sample")
    pair = (db, N_HEADS // 2, 2, HEAD_DIM, HEAD_DIM)
    s0_pairs = jnp.transpose(state_wkv[0].reshape(pair), (0, 1, 3, 2, 4)).reshape(
        db, N_HEADS // 2, HEAD_DIM, 2 * HEAD_DIM)
    y_s, wkv_s = _scan(ah, rh, bh, kh, vv, wc, s0_pairs, n_seq=db, seq_len=dseq, n_par=1,
                       rows=ROWS_SCAN_SAMPLE, chunk=C_SAMPLE, per_chunk_state=True, group=4,
                       name="scan_sample")

    x2 = _mix([(y_p, bonus_p, g_p, conv_p), (y_s, bonus_s, g_s, conv_s)], x1,
              row(ln_x_w[0]), row(ln_x_b[0]), bd, wo)
    y_prompt, y_sample = _ffn_two_stage([x2], row(g_ffn2[0]), wg2, wu2, wd2, row(g_final), [tp, ts],
                                        name="ffn2")

    shift_p = _unpack_rwkv(ptail_p.reshape(nb, seq // ROWS_PREP, 8, RW)[:, -1, 7, :])
    conv_state_p = utail_p.reshape(nb, seq // ROWS_PREP, 8, G)[:, -1, 6:, :]
    shift_s = _unpack_rwkv(plast_s)
    conv_state_s = u_s.reshape(db, dseq, G)[:, -2:, :]
    wkv_s_out = jnp.transpose(wkv_s.reshape(db, N_HEADS // 2, HEAD_DIM, 2, HEAD_DIM),
                              (0, 1, 3, 2, 4)).reshape(db, N_HEADS, HEAD_DIM, HEAD_DIM)
    return (y_prompt.reshape(nb, seq, D_MODEL), y_sample.reshape(db, dseq, D_MODEL),
            wkv_p[None].astype(state_wkv.dtype), shift_p[None].astype(state_shift.dtype),
            conv_state_p[None].astype(state_conv.dtype),
            wkv_s_out[None].astype(state_wkv.dtype), shift_s[None].astype(state_shift.dtype),
            conv_state_s[None].astype(state_conv.dtype))
```

```python
import functools

import jax
import jax.numpy as jnp
from jax import lax
from jax.experimental import pallas as pl
from jax.experimental.pallas import tpu as pltpu

F32 = jnp.float32
BF16 = jnp.bfloat16

D_MODEL = 2048
D_FF = 5632
N_META = 16
G = 1024
HEAD_DIM = 64
N_HEADS = G // HEAD_DIM
W_LORA = 96
A_LORA = 96
G_LORA = 256
LORA_PAD = 128
RWKV_PROJ = 3 * G + W_LORA + A_LORA + G_LORA
RWKV_SPLITS = (G, G + W_LORA, 2 * G + W_LORA, 3 * G + W_LORA, 3 * G + W_LORA + A_LORA)
RW = 3 * G + 2 * LORA_PAD + G_LORA
PW = RW + 3 * G
OFF_WD, OFF_AD, OFF_GD = 3 * G, 3 * G + LORA_PAD, 3 * G + 2 * LORA_PAD
RMS_EPS = 1e-6
GN_EPS = 64e-5
EXP_M05_LOG2E = 0.6065306597126334 * 1.4426950408889634

C_PROMPT = 64
C_SAMPLE = 8
TM = 512
TF = 512
TN = 3328
TM_FFN2 = 256
TM_MIX = 256
ROWS_PREP = 256
ROWS_SCAN_SAMPLE = 128
SCAN_PAR = 4
VMEM_LIMIT = 56 * 1024 * 1024


def _dot(a, b):
    return jnp.dot(a, b, preferred_element_type=F32)


def _dot_nt(a, b):
    return lax.dot_general(a, b, (((1,), (1,)), ((), ())), preferred_element_type=F32)


def _dot_tn(a, b):
    return lax.dot_general(a, b, (((0,), (0,)), ((), ())), preferred_element_type=F32)


def _split(a):
    hi = a.astype(BF16)
    lo = (a - hi.astype(F32)).astype(BF16)
    return hi, lo


def _head_sum(x, bd):
    return _dot(x.astype(BF16), bd)


def _dot_split_rhs(a_bf16, b):
    hi, lo = _split(b)
    return _dot(a_bf16, hi) + _dot(a_bf16, lo)


def _rms(x, g):
    return x * lax.rsqrt(jnp.mean(x * x, axis=-1, keepdims=True) + RMS_EPS) * g


def _segments(arrays, tile):
    segs, off = [], 0
    for arr in arrays:
        n = arr.shape[0] // tile
        assert n * tile == arr.shape[0]
        segs.append((off, n))
        off += n
    return segs, off


def _overlaps(in_segs, out_segs):
    for a, (ao, an) in enumerate(in_segs):
        for b, (bo, bn) in enumerate(out_segs):
            lo, hi = max(ao, bo), min(ao + an, bo + bn)
            if lo < hi:
                yield a, b, lo, hi


def _ffn_narrow_kernel(x_ref, g_ref, wg_ref, wu_ref, wd_ref, o_ref, wg_o, wu_o, wd_o, h_ref, acc_ref):
    j = pl.program_id(0)

    @pl.when(j == 0)
    def _():
        h_ref[...] = _rms(x_ref[...], g_ref[...]).astype(BF16)
        acc_ref[...] = jnp.zeros_like(acc_ref)

    wg, wu, wd = wg_ref[...].astype(BF16), wu_ref[...].astype(BF16), wd_ref[...].astype(BF16)
    wg_o[...], wu_o[...], wd_o[...] = wg, wu, wd
    h = h_ref[...]
    gate = _dot(h, wg)
    up = _dot(h, wu)
    act = (gate * jax.nn.sigmoid(gate) * up).astype(BF16)
    acc_ref[...] += _dot(act, wd)

    @pl.when(j == pl.num_programs(0) - 1)
    def _():
        o_ref[...] = x_ref[...] + 0.5 * acc_ref[...]


def _ffn_narrow(x, g, wg, wu, wd, *, name):
    rows = x.shape[0]
    whole = pl.BlockSpec((rows, D_MODEL), lambda j: (0, 0))
    w_specs = [pl.BlockSpec((D_MODEL, TF), lambda j: (0, j)),
               pl.BlockSpec((D_MODEL, TF), lambda j: (0, j)),
               pl.BlockSpec((TF, D_MODEL), lambda j: (j, 0))]
    return pl.pallas_call(
        _ffn_narrow_kernel,
        grid=(D_FF // TF,),
        in_specs=[whole, pl.BlockSpec((1, D_MODEL), lambda j: (0, 0))] + w_specs,
        out_specs=[whole] + w_specs,
        out_shape=[jax.ShapeDtypeStruct((rows, D_MODEL), F32)]
                  + [jax.ShapeDtypeStruct(w.shape, BF16) for w in (wg, wu, wd)],
        scratch_shapes=[pltpu.VMEM((rows, D_MODEL), BF16), pltpu.VMEM((rows, D_MODEL), F32)],
        compiler_params=pltpu.CompilerParams(
            dimension_semantics=("arbitrary",), vmem_limit_bytes=VMEM_LIMIT),
        name=name,
    )(x, g, wg, wu, wd)


def _ffn_up_kernel(*refs, in_segs):
    n_in = len(in_segs)
    x_refs = refs[:n_in]
    g_ref, wg_ref, wu_ref, a_ref = refs[n_in:]
    i = pl.program_id(1)
    for x_ref, (off, n) in zip(x_refs, in_segs):
        @pl.when((i >= off) & (i < off + n))
        def _(x_ref=x_ref):
            h = _rms(x_ref[...], g_ref[...]).astype(BF16)
            gate = _dot(h, wg_ref[...])
            up = _dot(h, wu_ref[...])
            a_ref[...] = (gate * jax.nn.sigmoid(gate) * up).astype(BF16)


def _ffn_down_kernel(*refs, in_segs, out_segs, final_norm, pack_steps):
    n_in, n_out = len(in_segs), len(out_segs)
    a_ref = refs[0]
    x_refs = refs[1:1 + n_in]
    wd_ref = refs[1 + n_in]
    rest = refs[2 + n_in:]
    if final_norm:
        gf_ref, rest = rest[0], rest[1:]
    if pack_steps:
        wt_ref, rest = rest[0], rest[1:]
    o_refs = rest[:n_out]
    i = pl.program_id(0)
    if pack_steps:
        @pl.when(i < pack_steps)
        def _():
            _pack_w_in_kernel(wt_ref, rest[n_out])

    for a, b, lo, hi in _overlaps(in_segs, out_segs):
        @pl.when((i >= lo) & (i < hi))
        def _(x_ref=x_refs[a], o_ref=o_refs[b]):
            xo = x_ref[...] + 0.5 * _dot(a_ref[...], wd_ref[...])
            if final_norm:
                xo = _rms(xo, gf_ref[...])
            o_ref[...] = xo


def _ffn_two_stage(xs, g, wg, wu, wd, g_final, out_rows, *, name, pack_src=None):
    final_norm = g_final is not None
    tm, tn = TM_FFN2, D_FF // 2
    in_segs, n_tiles = _segments(xs, tm)
    out_shape = [jax.ShapeDtypeStruct((n, D_MODEL), F32) for n in out_rows]
    out_segs, n_out_tiles = _segments(out_shape, tm)
    assert n_out_tiles == n_tiles
    seg_ji = lambda s: pl.BlockSpec((tm, D_MODEL), lambda j, i, s=s: (jnp.clip(i - s[0], 0, s[1] - 1), 0))
    half = pl.BlockSpec((D_MODEL, tn), lambda j, i: (0, j), pipeline_mode=pl.Buffered(1))
    act = pl.pallas_call(
        functools.partial(_ffn_up_kernel, in_segs=in_segs),
        grid=(D_FF // tn, n_tiles),
        in_specs=[seg_ji(s) for s in in_segs] + [pl.BlockSpec((1, D_MODEL), lambda j, i: (0, 0)), half, half],
        out_specs=pl.BlockSpec((tm, tn), lambda j, i: (i, j)),
        out_shape=jax.ShapeDtypeStruct((n_tiles * tm, D_FF), BF16),
        compiler_params=pltpu.CompilerParams(
            dimension_semantics=("arbitrary", "arbitrary"), vmem_limit_bytes=VMEM_LIMIT),
        name=name + "_up",
    )(*xs, g, wg, wu)

    seg_i = lambda s: pl.BlockSpec((tm, D_MODEL), lambda i, s=s: (jnp.clip(i - s[0], 0, s[1] - 1), 0))
    in_specs = ([pl.BlockSpec((tm, D_FF), lambda i: (i, 0))] + [seg_i(s) for s in in_segs]
                + [pl.BlockSpec((D_FF, D_MODEL), lambda i: (0, 0), pipeline_mode=pl.Buffered(1))])
    args = [act] + list(xs) + [wd]
    if final_norm:
        in_specs.append(pl.BlockSpec((1, D_MODEL), lambda i: (0, 0)))
        args.append(g_final)
    out_specs = [seg_i(s) for s in out_segs]
    pack_steps = 0
    if pack_src is not None:
        lane_blk = 128
        pack_steps = D_MODEL // lane_blk
        assert pack_steps <= n_tiles
        blk = lambda i: (0, jnp.minimum(i, pack_steps - 1))
        in_specs.append(pl.BlockSpec((pack_src.shape[0], lane_blk), blk))
        args.append(pack_src)
        out_specs.append(pl.BlockSpec((PW, lane_blk), blk))
        out_shape = out_shape + [jax.ShapeDtypeStruct((PW, D_MODEL), BF16)]
    return pl.pallas_call(
        functools.partial(_ffn_down_kernel, in_segs=in_segs, out_segs=out_segs, final_norm=final_norm,
                          pack_steps=pack_steps),
        grid=(n_tiles,),
        in_specs=in_specs,
        out_specs=out_specs,
        out_shape=out_shape,
        compiler_params=pltpu.CompilerParams(
            dimension_semantics=("arbitrary",), vmem_limit_bytes=VMEM_LIMIT),
        name=name + "_down",
    )(*args)


_PACK_MOVES = (
    (0, 0, G),
    (G, RWKV_SPLITS[1], G),
    (2 * G, RWKV_SPLITS[2], G),
    (OFF_WD, RWKV_SPLITS[0], W_LORA),
    (OFF_AD, RWKV_SPLITS[3], A_LORA),
    (OFF_GD, RWKV_SPLITS[4], G_LORA),
    (RW, RWKV_PROJ, 3 * G),
)


def _pack_w_in_kernel(w_ref, o_ref):
    for dst, src, n in _PACK_MOVES:
        o_ref[dst:dst + n, :] = w_ref[src:src + n, :].astype(BF16)
    for lo, hi in ((OFF_WD + W_LORA, OFF_AD), (OFF_AD + A_LORA, OFF_GD)):
        o_ref[lo:hi, :] = jnp.zeros((hi - lo, o_ref.shape[1]), BF16)


def _proj_kernel(*refs, in_segs):
    n_in = len(in_segs)
    x_refs = refs[:n_in]
    g_ref, w_ref, o_ref = refs[n_in:]
    i = pl.program_id(1)
    for x_ref, (off, n) in zip(x_refs, in_segs):
        @pl.when((i >= off) & (i < off + n))
        def _(x_ref=x_ref):
            h = _rms(x_ref[...], g_ref[...]).astype(BF16)
            o_ref[...] = _dot_nt(h, w_ref[...])


def _proj(xs, g, w, *, name):
    in_segs, n_tiles = _segments(xs, TM)
    seg_ji = lambda s: pl.BlockSpec((TM, D_MODEL), lambda j, i, s=s: (jnp.clip(i - s[0], 0, s[1] - 1), 0),
                                    pipeline_mode=pl.Buffered(1 if s[1] == 1 else 2))
    return pl.pallas_call(
        functools.partial(_proj_kernel, in_segs=in_segs),
        grid=(PW // TN, n_tiles),
        in_specs=[seg_ji(s) for s in in_segs] + [
            pl.BlockSpec((1, D_MODEL), lambda j, i: (0, 0)),
            pl.BlockSpec((TN, D_MODEL), lambda j, i: (j, 0)),
        ],
        out_specs=pl.BlockSpec((TM, TN), lambda j, i: (i, j)),
        out_shape=jax.ShapeDtypeStruct((n_tiles * TM, PW), F32),
        compiler_params=pltpu.CompilerParams(
            dimension_semantics=("parallel", "parallel"), vmem_limit_bytes=VMEM_LIMIT),
        name=name,
    )(*xs, g, w)


def _prep_kernel(*refs, rows, chunk, sample):
    if sample:
        (p_ref, prev_ref, up1_ref, up2_ref, mu_ref, w0_ref, ww_ref, a0_ref, wa_ref, wgl_ref,
         kk_ref, ka_ref, rk_ref, cw_ref, bd_ref, tri_ref,
         ah_ref, rh_ref, bh_ref, kh_ref, v_ref, wc_ref, g_ref, bonus_ref, conv_ref, u_ref,
         plast_ref) = refs
    else:
        (p_ref, cp0_ref, cu0_ref, mu_ref, w0_ref, ww_ref, a0_ref, wa_ref, wgl_ref,
         kk_ref, ka_ref, rk_ref, cw_ref, bd_ref, tri_ref,
         ah_ref, rh_ref, bh_ref, kh_ref, v_ref, wc_ref, g_ref, bonus_ref, conv_ref, u_ref,
         ptail_ref, carry_p, carry_u) = refs

        @pl.when(pl.program_id(1) == 0)
        def _():
            carry_p[...] = cp0_ref[...]
            carry_u[...] = cu0_ref[...]

    def row_ids(width):
        r = lax.broadcasted_iota(jnp.int32, (rows, width), 0)
        return (r & 7) if sample else r

    def per_sequence(x):
        n, w = x.shape
        return jnp.broadcast_to(x[:, None, :], (n, 8, w)).reshape(n * 8, w)

    def shifted_mix(lo, hi):
        p = p_ref[:, lo:hi]
        rolled = pltpu.roll(p, 1, 0)
        if sample:
            sh = jnp.where(row_ids(hi - lo) == 0, per_sequence(prev_ref[:, lo:hi]), rolled)
        else:
            sh = jnp.where(row_ids(hi - lo) == 0, carry_p[7:8, lo:hi], rolled)
        return p + (sh - p) * mu_ref[:, lo:hi]

    r = shifted_mix(0, G)
    k = shifted_mix(G, 2 * G)
    v = shifted_mix(2 * G, 3 * G)
    wd = shifted_mix(OFF_WD, OFF_WD + LORA_PAD)
    ad = shifted_mix(OFF_AD, OFF_AD + LORA_PAD)
    gd = shifted_mix(OFF_GD, OFF_GD + G_LORA)

    z = w0_ref[...] + _dot(jnp.tanh(wd).astype(BF16), ww_ref[...])
    lw = -EXP_M05_LOG2E * jax.nn.sigmoid(z)
    a = jax.nn.sigmoid(a0_ref[...] + _dot(ad.astype(BF16), wa_ref[...]))
    g_ref[...] = _dot(jax.nn.sigmoid(gd).astype(BF16), wgl_ref[...]).astype(BF16)

    bd = bd_ref[...]
    kk = k * kk_ref[...]
    norm = jnp.sqrt(_head_sum(kk * kk, bd))
    kk = kk / jnp.maximum(norm, 1e-12)
    km = k * (1.0 + (a - 1.0) * ka_ref[...])
    bonus_ref[...] = _head_sum(r * km * rk_ref[...], bd) * v

    cum = _dot_split_rhs(tri_ref[...], lw)
    e_cum = jnp.exp2(cum)
    e_inv = jnp.exp2(-cum)
    ah_ref[...] = (-kk * jnp.exp2(cum - lw)).astype(BF16)
    rh_ref[...] = (r * e_cum).astype(BF16)
    bh_ref[...] = (kk * a * e_inv).astype(BF16)
    kh_ref[...] = (km * e_inv).astype(BF16)
    v_ref[...] = v.astype(BF16)
    for c in range(rows // chunk):
        wc_ref[c] = e_cum[(c + 1) * chunk - 1:(c + 1) * chunk, :]

    bg = p_ref[:, RW:RW + G]
    u = p_ref[:, RW + G:RW + 2 * G] * p_ref[:, RW + 2 * G:RW + 3 * G]
    u1 = pltpu.roll(u, 1, 0)
    u2 = pltpu.roll(u, 2, 0)
    rid = row_ids(G)
    if sample:
        s1 = per_sequence(up1_ref[...])
        um1 = jnp.where(rid == 0, s1, u1)
        um2 = jnp.where(rid == 0, per_sequence(up2_ref[...]), jnp.where(rid == 1, s1, u2))
        u_ref[...] = u
        plast_ref[...] = p_ref[:, 0:RW].reshape(rows // 8, 8, RW)[:, 7, :]
    else:
        um1 = jnp.where(rid == 0, carry_u[7:8, :], u1)
        um2 = jnp.where(rid == 0, carry_u[6:7, :], jnp.where(rid == 1, carry_u[7:8, :], u2))
        u_ref[0] = u[rows - 8:, :]
        ptail_ref[0] = p_ref[rows - 8:rows, 0:RW]
    conv = cw_ref[0:1, :] * um2 + cw_ref[1:2, :] * um1 + cw_ref[2:3, :] * u
    conv_ref[...] = (bg * conv).astype(BF16)

    if not sample:
        carry_p[...] = p_ref[rows - 8:rows, 0:RW]
        carry_u[...] = u[rows - 8:, :]


def _prep(proj, weights, tri, *, n_seq, seq_len, rows, chunk, row_block_offset, extra, sample, name):
    t = n_seq * seq_len
    n_chunks = t // chunk
    cpt = rows // chunk
    if sample:
        grid = (t // rows,)
        rmap = lambda i: (i + row_block_offset, 0)
        omap = lambda i: (i, 0)
        omap3 = lambda i: (i, 0, 0)
        cmap = lambda i: (0, 0)
        sem = ("parallel",)
    else:
        tiles = seq_len // rows
        grid = (n_seq, tiles)
        rmap = lambda b, j: (b * tiles + j + row_block_offset, 0)
        omap = lambda b, j: (b * tiles + j, 0)
        omap3 = lambda b, j: (b * tiles + j, 0, 0)
        cmap = lambda b, j: (0, 0)
        sem = ("parallel", "arbitrary")

    const = lambda arr: pl.BlockSpec(arr.shape, cmap)
    in_specs = [pl.BlockSpec((rows, PW), rmap)]
    if sample:
        in_specs += [pl.BlockSpec((rows // 8, RW), omap), pl.BlockSpec((rows // 8, G), omap),
                     pl.BlockSpec((rows // 8, G), omap)]
    else:
        in_specs += [const(e) for e in extra]
    in_specs += [const(w) for w in weights] + [const(tri)]
    args = [proj] + list(extra) + list(weights) + [tri]

    row_out = jax.ShapeDtypeStruct((t, G), F32)
    row_spec = pl.BlockSpec((rows, G), omap)
    row_bf16 = jax.ShapeDtypeStruct((t, G), BF16)
    out_shape = [row_bf16] * 5 + [jax.ShapeDtypeStruct((n_chunks, 1, G), F32), row_bf16, row_out,
                                  row_bf16]
    out_specs = [row_spec] * 5 + [pl.BlockSpec((cpt, 1, G), omap3), row_spec, row_spec, row_spec]
    if sample:
        out_shape += [row_out, jax.ShapeDtypeStruct((t // 8, RW), F32)]
        out_specs += [row_spec, pl.BlockSpec((rows // 8, RW), omap)]
        scratch = []
    else:
        out_shape += [jax.ShapeDtypeStruct((t // rows, 8, G), F32),
                      jax.ShapeDtypeStruct((t // rows, 8, RW), F32)]
        out_specs += [pl.BlockSpec((1, 8, G), omap3), pl.BlockSpec((1, 8, RW), omap3)]
        scratch = [pltpu.VMEM((8, RW), F32), pltpu.VMEM((8, G), F32)]
    return pl.pallas_call(
        functools.partial(_prep_kernel, rows=rows, chunk=chunk, sample=sample),
        grid=grid, in_specs=in_specs, out_specs=out_specs, out_shape=out_shape,
        scratch_shapes=scratch,
        compiler_params=pltpu.CompilerParams(dimension_semantics=sem, vmem_limit_bytes=VMEM_LIMIT),
        name=name,
    )(*args)


def _scan_kernel(*refs, n_par, rows, chunk, per_chunk_state, group, n_cast):
    ah_ref, rh_ref, bh_ref, kh_ref, v_ref, wc_ref, s0_ref = refs[:7]
    y_ref, s_ref = refs[7 + n_cast:9 + n_cast]
    for w_ref, o_ref in zip(refs[7:7 + n_cast], refs[9 + n_cast:]):
        o_ref[...] = w_ref[...].astype(BF16)

    if per_chunk_state:
        s_in = s0_ref
    else:
        s_in = s_ref

        @pl.when(pl.program_id(1) == 0)
        def _():
            for q in range(n_par):
                s_ref[q] = s0_ref[0]

    n_blk = rows // chunk
    lanes = [slice(h * HEAD_DIM, (h + 1) * HEAD_DIM) for h in range(N_HEADS)]
    bf = lambda x: x.astype(BF16)

    shift = chunk.bit_length() - 1
    rs = lax.broadcasted_iota(jnp.int32, (rows, rows), 0)
    cs = lax.broadcasted_iota(jnp.int32, (rows, rows), 1)
    same = (rs >> shift) == (cs >> shift)
    mask_strict = same & (rs > cs)
    ri = lax.broadcasted_iota(jnp.int32, (rows, 2 * rows), 0)
    ci = lax.broadcasted_iota(jnp.int32, (rows, 2 * rows), 1)
    cj = jnp.where(ci >= rows, ci - rows, ci)
    mask_r = ((ri >> shift) == (cj >> shift)) & (ri >= cj)
    eye = (rs == cs).astype(F32)

    def chunk_rows(x, c):
        return x[c * chunk:(c + 1) * chunk, :]

    pair_lane = lax.broadcasted_iota(jnp.int32, (rows, 2 * HEAD_DIM), 1) >> (HEAD_DIM.bit_length() - 1)

    def pair_masked(ref, q, h):
        hp, hh = divmod(h, 2)
        x2 = ref[q, :, 2 * HEAD_DIM * hp:2 * HEAD_DIM * (hp + 1)].astype(F32)
        return jnp.where(pair_lane == hh, x2, 0.0)

    def unit_group(units):
        per_unit = lambda f: {u: f(*u) for u in units}
        v = per_unit(lambda q, h: v_ref[q, :, lanes[h]])
        if n_blk == 1:
            a = per_unit(lambda q, h: ah_ref[q, :, lanes[h]])
            r = per_unit(lambda q, h: rh_ref[q, :, lanes[h]])
            b = per_unit(lambda q, h: bh_ref[q, :, lanes[h]])
            k = per_unit(lambda q, h: kh_ref[q, :, lanes[h]])
        else:
            a = per_unit(lambda q, h: pair_masked(ah_ref, q, h))
            r = per_unit(lambda q, h: pair_masked(rh_ref, q, h))
            b = per_unit(lambda q, h: pair_masked(bh_ref, q, h))
            k = per_unit(lambda q, h: pair_masked(kh_ref, q, h))
            v32 = {u: v[u].astype(F32) for u in units}
        ar = {u: bf(jnp.concatenate([a[u], r[u]], axis=0)) for u in units}
        bk = {u: bf(jnp.concatenate([b[u], k[u]], axis=0)) for u in units}

        gram = {u: _dot(ar[u], jnp.transpose(bk[u])) for u in units}
        l_ab = {u: jnp.where(mask_strict, gram[u][:rows, :rows], 0.0) for u in units}
        l_ak = {u: bf(jnp.where(mask_strict, gram[u][:rows, rows:], 0.0)) for u in units}
        l_r = {u: bf(jnp.where(mask_r, gram[u][rows:, :], 0.0)) for u in units}

        xs, ys = {}, {}
        for u in units:
            q, h = u
            if n_blk == 1:
                st = _dot_nt(ar[u], bf(s_in[q, h]))
                xs[u], ys[u] = st[:rows], st[rows:]
            else:
                parts = [_dot_nt(jnp.concatenate([chunk_rows(a[u], c), chunk_rows(r[u], c)], axis=0),
                                 s_in[c, h // 2]) for c in range(n_blk)]
                xs[u] = jnp.concatenate([p[:chunk] for p in parts], axis=0)
                ys[u] = jnp.concatenate([p[chunk:] for p in parts], axis=0)

        x = {u: xs[u] + _dot(l_ak[u], v[u]) for u in units}

        t_inv = {u: eye + l_ab[u] for u in units}
        n = 2
        if n < chunk:
            pw = {u: bf(l_ab[u]) for u in units}
            pw = {u: bf(_dot(pw[u], pw[u])) for u in units}
        while n < chunk:
            if 2 * n < chunk:
                both = {u: _dot(jnp.concatenate([bf(t_inv[u]), pw[u]], axis=0), pw[u]) for u in units}
                t_inv = {u: t_inv[u] + both[u][:rows] for u in units}
                pw = {u: bf(both[u][rows:]) for u in units}
            else:
                t_inv = {u: t_inv[u] + _dot(bf(t_inv[u]), pw[u]) for u in units}
            n *= 2

        uu = {u: _dot(bf(t_inv[u]), bf(x[u])) for u in units}
        uv = {u: jnp.concatenate([bf(uu[u]), v[u]], axis=0) for u in units}
        for u in units:
            q, h = u
            y_ref[q, :, lanes[h]] = ys[u] + _dot(l_r[u], uv[u])

        if n_blk == 1:
            for u in units:
                q, h = u
                upd = _dot_tn(uv[u], bk[u])
                s_ref[q, h] = (s_in[q, h] + upd) * wc_ref[q, 0][:, lanes[h]]
        else:
            def update(u, c):
                uv_c = jnp.concatenate([chunk_rows(uu[u], c), chunk_rows(v32[u], c)], axis=0)
                bk_c = jnp.concatenate([chunk_rows(b[u], c), chunk_rows(k[u], c)], axis=0)
                return _dot_tn(uv_c, bk_c)

            for q, h in units:
                if h % 2 == 0:
                    assert (q, h + 1) in units
                    hp = h // 2
                    for c in range(n_blk):
                        s_new = s_in[c, hp] + update((q, h), c) + update((q, h + 1), c)
                        s_ref[c, hp] = s_new * wc_ref[q, c][:, 2 * HEAD_DIM * hp:2 * HEAD_DIM * (hp + 1)]

    all_units = [(q, h) for h in range(N_HEADS) for q in range(n_par)]
    for g0 in range(0, len(all_units), group):
        unit_group(all_units[g0:g0 + group])


def _scan(ah, rh, bh, kh, v, wc, s0, *, n_seq, seq_len, n_par, rows, chunk, per_chunk_state, group,
          name, cast=()):
    t = n_seq * seq_len
    n_blk = rows // chunk
    if per_chunk_state:
        assert seq_len == chunk and n_par == 1
        lead = 1
        grid = (t // rows,)
        rmap = lambda i: (0, i, 0)
        cmap = lambda i: (0, i, 0, 0)
        smap = lambda i: (i, 0, 0, 0)
        s0map = smap
        state_block = (n_blk, N_HEADS // 2, HEAD_DIM, 2 * HEAD_DIM)
        s0_block = state_block
        wc_block = (1, n_blk, 1, G)
        sem = ("parallel",)
    else:
        assert rows == chunk and s0.shape[0] == 1 and n_seq % n_par == 0
        lead = n_seq
        grid = (n_seq // n_par, seq_len // chunk)
        rmap = lambda s, c: (s, c, 0)
        cmap = lambda s, c: (s, c, 0, 0)
        smap = lambda s, c: (s, 0, 0, 0)
        s0map = lambda s, c: (0, 0, 0, 0)
        state_block = (n_par, N_HEADS, HEAD_DIM, HEAD_DIM)
        s0_block = (1, N_HEADS, HEAD_DIM, HEAD_DIM)
        wc_block = (n_par, 1, 1, G)
        sem = ("parallel", "arbitrary")
    rows3 = lambda x: x.reshape(lead, t // lead, G)
    row_spec = pl.BlockSpec((n_par, rows, G), rmap)
    n_steps = 1
    for extent in grid:
        n_steps *= extent
    step = (lambda i: i) if len(grid) == 1 else (lambda s, c: s * grid[1] + c)
    cast_specs = []
    for w in cast:
        assert w.shape[0] % (16 * n_steps) == 0
        cast_specs.append(pl.BlockSpec((w.shape[0] // n_steps, w.shape[1]), lambda *g: (step(*g), 0)))
    y, s_out, *narrowed = pl.pallas_call(
        functools.partial(_scan_kernel, n_par=n_par, rows=rows, chunk=chunk,
                          per_chunk_state=per_chunk_state, group=group, n_cast=len(cast)),
        grid=grid,
        in_specs=([row_spec] * 5 + [pl.BlockSpec(wc_block, cmap), pl.BlockSpec(s0_block, s0map)]
                  + cast_specs),
        out_specs=[row_spec, pl.BlockSpec(state_block, smap)] + cast_specs,
        out_shape=[jax.ShapeDtypeStruct((lead, t // lead, G), F32),
                   jax.ShapeDtypeStruct((n_seq,) + state_block[1:], F32)]
                  + [jax.ShapeDtypeStruct(w.shape, BF16) for w in cast],
        compiler_params=pltpu.CompilerParams(dimension_semantics=sem, vmem_limit_bytes=VMEM_LIMIT),
        name=name,
    )(rows3(ah), rows3(rh), rows3(bh), rows3(kh), rows3(v), wc.reshape(lead, -1, 1, G), s0, *cast)
    return (y.reshape(t, G), s_out, *narrowed)


def _mix_kernel(*refs, segs):
    n = len(segs)
    row_refs = [refs[4 * s:4 * s + 4] for s in range(n)]
    x_ref, lnw_ref, lnb_ref, bd_ref, wo_ref, o_ref = refs[4 * n:]
    i = pl.program_id(0)

    def body(y_ref, bonus_ref, g_ref, conv_ref):
        bd = bd_ref[...]
        y = y_ref[...]
        mu = _head_sum(y, bd) * (1.0 / HEAD_DIM)
        d = y - mu
        var = _head_sum(d * d, bd) * (1.0 / HEAD_DIM)
        yn = d * lax.rsqrt(var + GN_EPS) * lnw_ref[...] + lnb_ref[...]
        rw = ((yn + bonus_ref[...]) * g_ref[...]).astype(BF16)
        mix = _dot(rw, wo_ref[0:G, :]) + _dot(conv_ref[...], wo_ref[G:2 * G, :])
        o_ref[...] = x_ref[...] + mix

    for rr, (off, cnt) in zip(row_refs, segs):
        @pl.when((i >= off) & (i < off + cnt))
        def _(rr=rr):
            body(*rr)


def _mix(row_groups, x1, lnw, lnb, bd, wo):
    segs, n_tiles = _segments([grp[0] for grp in row_groups], TM_MIX)
    cmap = lambda i: (0, 0)
    in_specs, args = [], []
    for grp, (off, cnt) in zip(row_groups, segs):
        smap = lambda i, off=off, cnt=cnt: (jnp.clip(i - off, 0, cnt - 1), 0)
        in_specs += [pl.BlockSpec((TM_MIX, G), smap)] * 4
        args += list(grp)
    in_specs += [pl.BlockSpec((TM_MIX, D_MODEL), lambda i: (i, 0)),
                 pl.BlockSpec((1, G), cmap), pl.BlockSpec((1, G), cmap),
                 pl.BlockSpec((G, G), cmap), pl.BlockSpec((D_MODEL, D_MODEL), cmap)]
    return pl.pallas_call(
        functools.partial(_mix_kernel, segs=segs),
        grid=(n_tiles,), in_specs=in_specs,
        out_specs=pl.BlockSpec((TM_MIX, D_MODEL), lambda i: (i, 0)),
        out_shape=jax.ShapeDtypeStruct((n_tiles * TM_MIX, D_MODEL), F32),
        compiler_params=pltpu.CompilerParams(
            dimension_semantics=("parallel",), vmem_limit_bytes=VMEM_LIMIT),
        name="mix",
    )(*args, x1, lnw, lnb, bd, wo)


def _pack_rwkv(a):
    r, wd, k, v, ad, gd = jnp.split(a, RWKV_SPLITS, axis=-1)
    zw = jnp.zeros(a.shape[:-1] + (LORA_PAD - W_LORA,), a.dtype)
    za = jnp.zeros(a.shape[:-1] + (LORA_PAD - A_LORA,), a.dtype)
    return jnp.concatenate([r, k, v, wd, zw, ad, za, gd], axis=-1)


def _unpack_rwkv(p):
    return jnp.concatenate([
        p[..., 0:G], p[..., OFF_WD:OFF_WD + W_LORA], p[..., G:2 * G], p[..., 2 * G:3 * G],
        p[..., OFF_AD:OFF_AD + A_LORA], p[..., OFF_GD:OFF_GD + G_LORA]], axis=-1)


def _block_tri(n, chunk):
    i = jnp.arange(n)
    return ((i[:, None] // chunk == i[None, :] // chunk) & (i[:, None] >= i[None, :])).astype(BF16)


def kernel(x_prompt, x_sample, state_wkv, state_shift, state_conv, meta_tokens, g_ffn1, ffn1_gate, ffn1_up, ffn1_down, g_mix, w_in, mu_shift, w0, w_lora_w, a0, w_lora_a, w_lora_g, k_k, k_a, r_k, ln_x_w, ln_x_b, conv_w, w_out, g_ffn2, ffn2_gate, ffn2_up, ffn2_down, g_final):
    assert g_ffn1.shape[0] == 1, "single layer"
    nb, seq, _ = x_prompt.shape
    db, dseq, _ = x_sample.shape
    assert dseq == C_SAMPLE and N_META <= C_PROMPT and seq % ROWS_PREP == 0
    tp, ts = nb * seq, db * dseq
    assert tp % TM == 0 and ts % TM == 0

    row = lambda a: a.reshape(1, -1).astype(F32)
    pad_rows = lambda w, n: jnp.concatenate([w, jnp.zeros((n - w.shape[0], w.shape[1]), w.dtype)], axis=0)
    hid = jnp.arange(G) // HEAD_DIM
    bd = (hid[:, None] == hid[None, :]).astype(BF16)
    prep_w = (_pack_rwkv(mu_shift[0])[None], row(w0[0]), pad_rows(w_lora_w[0], LORA_PAD).astype(BF16),
              row(a0[0]), pad_rows(w_lora_a[0], LORA_PAD).astype(BF16), w_lora_g[0].astype(BF16),
              row(k_k[0]), row(k_a[0]), row(r_k[0]), conv_w[0].astype(F32), bd)

    x_meta = jnp.concatenate([jnp.zeros((C_PROMPT - N_META, D_MODEL), F32), meta_tokens.astype(F32)], axis=0)
    x1_meta, wg1, wu1, wd1 = _ffn_narrow(x_meta, row(g_ffn1[0]), ffn1_gate[0], ffn1_up[0], ffn1_down[0],
                                         name="ffn1_meta")
    x1, w_in_p = _ffn_two_stage([x_prompt.reshape(tp, D_MODEL), x_sample.reshape(ts, D_MODEL)],
                                row(g_ffn1[0]), wg1, wu1, wd1, None, [tp + ts], name="ffn1",
                                pack_src=jnp.transpose(w_in[0]))
    x1_meta = jnp.concatenate([x1_meta, jnp.zeros((TM - C_PROMPT, D_MODEL), F32)], axis=0)
    proj = _proj([x1, x1_meta], row(g_mix[0]), w_in_p, name="proj")
    meta_row0 = tp + ts

    tri_p = _block_tri(ROWS_PREP, C_PROMPT)
    zeros_state = jnp.zeros((1, N_HEADS, HEAD_DIM, HEAD_DIM), F32)
    (ah, rh, bh, kh, vv, wc, _, _, _, utail_m, ptail_m) = _prep(
        proj, prep_w, tri_p[:C_PROMPT, :C_PROMPT], n_seq=1, seq_len=C_PROMPT, rows=C_PROMPT,
        chunk=C_PROMPT, row_block_offset=meta_row0 // C_PROMPT,
        extra=(jnp.zeros((8, RW), F32), jnp.zeros((8, G), F32)),
        sample=False, name="prep_meta")
    _, wkv_m = _scan(ah, rh, bh, kh, vv, wc, zeros_state, n_seq=1, seq_len=C_PROMPT, n_par=1,
                     rows=C_PROMPT, chunk=C_PROMPT, per_chunk_state=False, group=N_HEADS,
                     name="scan_meta")

    (ah, rh, bh, kh, vv, wc, g_p, bonus_p, conv_p, utail_p, ptail_p) = _prep(
        proj, prep_w, tri_p, n_seq=nb, seq_len=seq, rows=ROWS_PREP, chunk=C_PROMPT,
        row_block_offset=0, extra=(ptail_m[0], utail_m[0]), sample=False, name="prep_prompt")
    y_p, wkv_p, wg2, wu2, wd2, wo = _scan(
        ah, rh, bh, kh, vv, wc, wkv_m, n_seq=nb, seq_len=seq, n_par=SCAN_PAR, rows=C_PROMPT,
        chunk=C_PROMPT, per_chunk_state=False, group=SCAN_PAR * N_HEADS // 2, name="scan_prompt",
        cast=(ffn2_gate[0], ffn2_up[0], ffn2_down[0], w_out[0]))

    prev = _pack_rwkv(state_shift[0])
    up1 = state_conv[0][:, 1]
    up2 = state_conv[0][:, 0]
    (ah, rh, bh, kh, vv, wc, g_s, bonus_s, conv_s, u_s, plast_s) = _prep(
        proj, prep_w, _block_tri(ROWS_PREP, C_SAMPLE), n_seq=db, seq_len=dseq, rows=ROWS_PREP,
        chunk=C_SAMPLE, row_block_offset=tp // ROWS_PREP, extra=(prev, up1, up2), sample=True,
        name="prep_sample")
    pair = (db, N_HEADS // 2, 2, HEAD_DIM, HEAD_DIM)
    s0_pairs = jnp.transpose(state_wkv[0].reshape(pair), (0, 1, 3, 2, 4)).reshape(
        db, N_HEADS // 2, HEAD_DIM, 2 * HEAD_DIM)
    y_s, wkv_s = _scan(ah, rh, bh, kh, vv, wc, s0_pairs, n_seq=db, seq_len=dseq, n_par=1,
                       rows=ROWS_SCAN_SAMPLE, chunk=C_SAMPLE, per_chunk_state=True, group=4,
                       name="scan_sample")

    x2 = _mix([(y_p, bonus_p, g_p, conv_p), (y_s, bonus_s, g_s, conv_s)], x1,
              row(ln_x_w[0]), row(ln_x_b[0]), bd, wo)
    y_prompt, y_sample = _ffn_two_stage([x2], row(g_ffn2[0]), wg2, wu2, wd2, row(g_final), [tp, ts],
                                        name="ffn2")

    shift_p = _unpack_rwkv(ptail_p.reshape(nb, seq // ROWS_PREP, 8, RW)[:, -1, 7, :])
    conv_state_p = utail_p.reshape(nb, seq // ROWS_PREP, 8, G)[:, -1, 6:, :]
    shift_s = _unpack_rwkv(plast_s)
    conv_state_s = u_s.reshape(db, dseq, G)[:, -2:, :]
    wkv_s_out = jnp.transpose(wkv_s.reshape(db, N_HEADS // 2, HEAD_DIM, 2, HEAD_DIM),
                              (0, 1, 3, 2, 4)).reshape(db, N_HEADS, HEAD_DIM, HEAD_DIM)
    return (y_prompt.reshape(nb, seq, D_MODEL), y_sample.reshape(db, dseq, D_MODEL),
            wkv_p[None].astype(state_wkv.dtype), shift_p[None].astype(state_shift.dtype),
            conv_state_p[None].astype(state_conv.dtype),
            wkv_s_out[None].astype(state_wkv.dtype), shift_s[None].astype(state_shift.dtype),
            conv_state_s[None].astype(state_conv.dtype))
```

```python
import functools

import jax
import jax.numpy as jnp
from jax import lax
from jax.experimental import pallas as pl
from jax.experimental.pallas import tpu as pltpu

F32 = jnp.float32
BF16 = jnp.bfloat16

D_MODEL = 2048
D_FF = 5632
N_META = 16
G = 1024
HEAD_DIM = 64
N_HEADS = G // HEAD_DIM
W_LORA = 96
A_LORA = 96
G_LORA = 256
LORA_PAD = 128
RWKV_PROJ = 3 * G + W_LORA + A_LORA + G_LORA
RWKV_SPLITS = (G, G + W_LORA, 2 * G + W_LORA, 3 * G + W_LORA, 3 * G + W_LORA + A_LORA)
RW = 3 * G + 2 * LORA_PAD + G_LORA
PW = RW + 3 * G
OFF_WD, OFF_AD, OFF_GD = 3 * G, 3 * G + LORA_PAD, 3 * G + 2 * LORA_PAD
RMS_EPS = 1e-6
GN_EPS = 64e-5
EXP_M05_LOG2E = 0.6065306597126334 * 1.4426950408889634

C_PROMPT = 64
C_SAMPLE = 8
TM = 512
TF = 512
TN = 3328
TM_FFN2 = 256
TM_MIX = 256
ROWS_PREP = 256
ROWS_SCAN_SAMPLE = 128
SCAN_PAR = 4
VMEM_LIMIT = 56 * 1024 * 1024


def _dot(a, b):
    return jnp.dot(a, b, preferred_element_type=F32)


def _dot_nt(a, b):
    return lax.dot_general(a, b, (((1,), (1,)), ((), ())), preferred_element_type=F32)


def _dot_tn(a, b):
    return lax.dot_general(a, b, (((0,), (0,)), ((), ())), preferred_element_type=F32)


def _split(a):
    hi = a.astype(BF16)
    lo = (a - hi.astype(F32)).astype(BF16)
    return hi, lo


def _head_sum(x, bd):
    return _dot(x.astype(BF16), bd)


def _dot_split_rhs(a_bf16, b):
    hi, lo = _split(b)
    return _dot(a_bf16, hi) + _dot(a_bf16, lo)


def _rms(x, g):
    return x * lax.rsqrt(jnp.mean(x * x, axis=-1, keepdims=True) + RMS_EPS) * g


def _segments(arrays, tile):
    segs, off = [], 0
    for arr in arrays:
        n = arr.shape[0] // tile
        assert n * tile == arr.shape[0]
        segs.append((off, n))
        off += n
    return segs, off


def _overlaps(in_segs, out_segs):
    for a, (ao, an) in enumerate(in_segs):
        for b, (bo, bn) in enumerate(out_segs):
            lo, hi = max(ao, bo), min(ao + an, bo + bn)
            if lo < hi:
                yield a, b, lo, hi


def _ffn_narrow_kernel(x_ref, g_ref, wg_ref, wu_ref, wd_ref, o_ref, wg_o, wu_o, wd_o, h_ref, acc_ref):
    j = pl.program_id(0)

    @pl.when(j == 0)
    def _():
        h_ref[...] = _rms(x_ref[...], g_ref[...]).astype(BF16)
        acc_ref[...] = jnp.zeros_like(acc_ref)

    wg, wu, wd = wg_ref[...].astype(BF16), wu_ref[...].astype(BF16), wd_ref[...].astype(BF16)
    wg_o[...], wu_o[...], wd_o[...] = wg, wu, wd
    h = h_ref[...]
    gate = _dot(h, wg)
    up = _dot(h, wu)
    act = (gate * jax.nn.sigmoid(gate) * up).astype(BF16)
    acc_ref[...] += _dot(act, wd)

    @pl.when(j == pl.num_programs(0) - 1)
    def _():
        o_ref[...] = x_ref[...] + 0.5 * acc_ref[...]


def _ffn_narrow(x, g, wg, wu, wd, *, name):
    rows = x.shape[0]
    whole = pl.BlockSpec((rows, D_MODEL), lambda j: (0, 0))
    w_specs = [pl.BlockSpec((D_MODEL, TF), lambda j: (0, j)),
               pl.BlockSpec((D_MODEL, TF), lambda j: (0, j)),
               pl.BlockSpec((TF, D_MODEL), lambda j: (j, 0))]
    return pl.pallas_call(
        _ffn_narrow_kernel,
        grid=(D_FF // TF,),
        in_specs=[whole, pl.BlockSpec((1, D_MODEL), lambda j: (0, 0))] + w_specs,
        out_specs=[whole] + w_specs,
        out_shape=[jax.ShapeDtypeStruct((rows, D_MODEL), F32)]
                  + [jax.ShapeDtypeStruct(w.shape, BF16) for w in (wg, wu, wd)],
        scratch_shapes=[pltpu.VMEM((rows, D_MODEL), BF16), pltpu.VMEM((rows, D_MODEL), F32)],
        compiler_params=pltpu.CompilerParams(
            dimension_semantics=("arbitrary",), vmem_limit_bytes=VMEM_LIMIT),
        name=name,
    )(x, g, wg, wu, wd)


def _ffn_up_kernel(*refs, in_segs):
    n_in = len(in_segs)
    x_refs = refs[:n_in]
    g_ref, wg_ref, wu_ref, a_ref = refs[n_in:]
    i = pl.program_id(1)
    for x_ref, (off, n) in zip(x_refs, in_segs):
        @pl.when((i >= off) & (i < off + n))
        def _(x_ref=x_ref):
            h = _rms(x_ref[...], g_ref[...]).astype(BF16)
            gate = _dot(h, wg_ref[...])
            up = _dot(h, wu_ref[...])
            a_ref[...] = (gate * jax.nn.sigmoid(gate) * up).astype(BF16)


def _ffn_down_kernel(*refs, in_segs, out_segs, final_norm, pack_steps):
    n_in, n_out = len(in_segs), len(out_segs)
    a_ref = refs[0]
    x_refs = refs[1:1 + n_in]
    wd_ref = refs[1 + n_in]
    rest = refs[2 + n_in:]
    if final_norm:
        gf_ref, rest = rest[0], rest[1:]
    if pack_steps:
        wt_ref, rest = rest[0], rest[1:]
    o_refs = rest[:n_out]
    i = pl.program_id(0)
    if pack_steps:
        @pl.when(i < pack_steps)
        def _():
            _pack_w_in_kernel(wt_ref, rest[n_out])

    for a, b, lo, hi in _overlaps(in_segs, out_segs):
        @pl.when((i >= lo) & (i < hi))
        def _(x_ref=x_refs[a], o_ref=o_refs[b]):
            xo = x_ref[...] + 0.5 * _dot(a_ref[...], wd_ref[...])
            if final_norm:
                xo = _rms(xo, gf_ref[...])
            o_ref[...] = xo


def _ffn_two_stage(xs, g, wg, wu, wd, g_final, out_rows, *, name, pack_src=None):
    final_norm = g_final is not None
    tm, tn = TM_FFN2, D_FF // 2
    in_segs, n_tiles = _segments(xs, tm)
    out_shape = [jax.ShapeDtypeStruct((n, D_MODEL), F32) for n in out_rows]
    out_segs, n_out_tiles = _segments(out_shape, tm)
    assert n_out_tiles == n_tiles
    seg_ji = lambda s: pl.BlockSpec((tm, D_MODEL), lambda j, i, s=s: (jnp.clip(i - s[0], 0, s[1] - 1), 0))
    half = pl.BlockSpec((D_MODEL, tn), lambda j, i: (0, j), pipeline_mode=pl.Buffered(1))
    act = pl.pallas_call(
        functools.partial(_ffn_up_kernel, in_segs=in_segs),
        grid=(D_FF // tn, n_tiles),
        in_specs=[seg_ji(s) for s in in_segs] + [pl.BlockSpec((1, D_MODEL), lambda j, i: (0, 0)), half, half],
        out_specs=pl.BlockSpec((tm, tn), lambda j, i: (i, j)),
        out_shape=jax.ShapeDtypeStruct((n_tiles * tm, D_FF), BF16),
        compiler_params=pltpu.CompilerParams(
            dimension_semantics=("arbitrary", "arbitrary"), vmem_limit_bytes=VMEM_LIMIT),
        name=name + "_up",
    )(*xs, g, wg, wu)

    seg_i = lambda s: pl.BlockSpec((tm, D_MODEL), lambda i, s=s: (jnp.clip(i - s[0], 0, s[1] - 1), 0))
    in_specs = ([pl.BlockSpec((tm, D_FF), lambda i: (i, 0))] + [seg_i(s) for s in in_segs]
                + [pl.BlockSpec((D_FF, D_MODEL), lambda i: (0, 0), pipeline_mode=pl.Buffered(1))])
    args = [act] + list(xs) + [wd]
    if final_norm:
        in_specs.append(pl.BlockSpec((1, D_MODEL), lambda i: (0, 0)))
        args.append(g_final)
    out_specs = [seg_i(s) for s in out_segs]
    pack_steps = 0
    if pack_src is not None:
        lane_blk = 128
        pack_steps = D_MODEL // lane_blk
        assert pack_steps <= n_tiles
        blk = lambda i: (0, jnp.minimum(i, pack_steps - 1))
        in_specs.append(pl.BlockSpec((pack_src.shape[0], lane_blk), blk))
        args.append(pack_src)
        out_specs.append(pl.BlockSpec((PW, lane_blk), blk))
        out_shape = out_shape + [jax.ShapeDtypeStruct((PW, D_MODEL), BF16)]
    return pl.pallas_call(
        functools.partial(_ffn_down_kernel, in_segs=in_segs, out_segs=out_segs, final_norm=final_norm,
                          pack_steps=pack_steps),
        grid=(n_tiles,),
        in_specs=in_specs,
        out_specs=out_specs,
        out_shape=out_shape,
        compiler_params=pltpu.CompilerParams(
            dimension_semantics=("arbitrary",), vmem_limit_bytes=VMEM_LIMIT),
        name=name + "_down",
    )(*args)


_PACK_MOVES = (
    (0, 0, G),
    (G, RWKV_SPLITS[1], G),
    (2 * G, RWKV_SPLITS[2], G),
    (OFF_WD, RWKV_SPLITS[0], W_LORA),
    (OFF_AD, RWKV_SPLITS[3], A_LORA),
    (OFF_GD, RWKV_SPLITS[4], G_LORA),
    (RW, RWKV_PROJ, 3 * G),
)


def _pack_w_in_kernel(w_ref, o_ref):
    for dst, src, n in _PACK_MOVES:
        o_ref[dst:dst + n, :] = w_ref[src:src + n, :].astype(BF16)
    for lo, hi in ((OFF_WD + W_LORA, OFF_AD), (OFF_AD + A_LORA, OFF_GD)):
        o_ref[lo:hi, :] = jnp.zeros((hi - lo, o_ref.shape[1]), BF16)


def _proj_kernel(*refs, in_segs):
    n_in = len(in_segs)
    x_refs = refs[:n_in]
    g_ref, w_ref, o_ref = refs[n_in:]
    i = pl.program_id(1)
    for x_ref, (off, n) in zip(x_refs, in_segs):
        @pl.when((i >= off) & (i < off + n))
        def _(x_ref=x_ref):
            h = _rms(x_ref[...], g_ref[...]).astype(BF16)
            o_ref[...] = _dot_nt(h, w_ref[...])


def _proj(xs, g, w, *, name):
    in_segs, n_tiles = _segments(xs, TM)
    seg_ji = lambda s: pl.BlockSpec((TM, D_MODEL), lambda j, i, s=s: (jnp.clip(i - s[0], 0, s[1] - 1), 0),
                                    pipeline_mode=pl.Buffered(1 if s[1] == 1 else 2))
    return pl.pallas_call(
        functools.partial(_proj_kernel, in_segs=in_segs),
        grid=(PW // TN, n_tiles),
        in_specs=[seg_ji(s) for s in in_segs] + [
            pl.BlockSpec((1, D_MODEL), lambda j, i: (0, 0)),
            pl.BlockSpec((TN, D_MODEL), lambda j, i: (j, 0)),
        ],
        out_specs=pl.BlockSpec((TM, TN), lambda j, i: (i, j)),
        out_shape=jax.ShapeDtypeStruct((n_tiles * TM, PW), F32),
        compiler_params=pltpu.CompilerParams(
            dimension_semantics=("parallel", "parallel"), vmem_limit_bytes=VMEM_LIMIT),
        name=name,
    )(*xs, g, w)


def _prep_kernel(*refs, rows, chunk, sample):
    if sample:
        (p_ref, prev_ref, up1_ref, up2_ref, mu_ref, w0_ref, ww_ref, a0_ref, wa_ref, wgl_ref,
         kk_ref, ka_ref, rk_ref, cw_ref, bd_ref, tri_ref,
         ah_ref, rh_ref, bh_ref, kh_ref, v_ref, wc_ref, g_ref, bonus_ref, conv_ref, u_ref,
         plast_ref) = refs
    else:
        (p_ref, cp0_ref, cu0_ref, mu_ref, w0_ref, ww_ref, a0_ref, wa_ref, wgl_ref,
         kk_ref, ka_ref, rk_ref, cw_ref, bd_ref, tri_ref,
         ah_ref, rh_ref, bh_ref, kh_ref, v_ref, wc_ref, g_ref, bonus_ref, conv_ref, u_ref,
         ptail_ref, carry_p, carry_u) = refs

        @pl.when(pl.program_id(1) == 0)
        def _():
            carry_p[...] = cp0_ref[...]
            carry_u[...] = cu0_ref[...]

    def row_ids(width):
        r = lax.broadcasted_iota(jnp.int32, (rows, width), 0)
        return (r & 7) if sample else r

    def per_sequence(x):
        n, w = x.shape
        return jnp.broadcast_to(x[:, None, :], (n, 8, w)).reshape(n * 8, w)

    def shifted_mix(lo, hi):
        p = p_ref[:, lo:hi]
        rolled = pltpu.roll(p, 1, 0)
        if sample:
            sh = jnp.where(row_ids(hi - lo) == 0, per_sequence(prev_ref[:, lo:hi]), rolled)
        else:
            sh = jnp.where(row_ids(hi - lo) == 0, carry_p[7:8, lo:hi], rolled)
        return p + (sh - p) * mu_ref[:, lo:hi]

    r = shifted_mix(0, G)
    k = shifted_mix(G, 2 * G)
    v = shifted_mix(2 * G, 3 * G)
    wd = shifted_mix(OFF_WD, OFF_WD + LORA_PAD)
    ad = shifted_mix(OFF_AD, OFF_AD + LORA_PAD)
    gd = shifted_mix(OFF_GD, OFF_GD + G_LORA)

    z = w0_ref[...] + _dot(jnp.tanh(wd).astype(BF16), ww_ref[...])
    lw = -EXP_M05_LOG2E * jax.nn.sigmoid(z)
    a = jax.nn.sigmoid(a0_ref[...] + _dot(ad.astype(BF16), wa_ref[...]))
    g_ref[...] = _dot(jax.nn.sigmoid(gd).astype(BF16), wgl_ref[...]).astype(BF16)

    bd = bd_ref[...]
    kk = k * kk_ref[...]
    norm = jnp.sqrt(_head_sum(kk * kk, bd))
    kk = kk / jnp.maximum(norm, 1e-12)
    km = k * (1.0 + (a - 1.0) * ka_ref[...])
    bonus_ref[...] = _head_sum(r * km * rk_ref[...], bd) * v

    cum = _dot_split_rhs(tri_ref[...], lw)
    e_cum = jnp.exp2(cum)
    e_inv = jnp.exp2(-cum)
    ah_ref[...] = (-kk * jnp.exp2(cum - lw)).astype(BF16)
    rh_ref[...] = (r * e_cum).astype(BF16)
    bh_ref[...] = (kk * a * e_inv).astype(BF16)
    kh_ref[...] = (km * e_inv).astype(BF16)
    v_ref[...] = v.astype(BF16)
    for c in range(rows // chunk):
        wc_ref[c] = e_cum[(c + 1) * chunk - 1:(c + 1) * chunk, :]

    bg = p_ref[:, RW:RW + G]
    u = p_ref[:, RW + G:RW + 2 * G] * p_ref[:, RW + 2 * G:RW + 3 * G]
    u1 = pltpu.roll(u, 1, 0)
    u2 = pltpu.roll(u, 2, 0)
    rid = row_ids(G)
    if sample:
        s1 = per_sequence(up1_ref[...])
        um1 = jnp.where(rid == 0, s1, u1)
        um2 = jnp.where(rid == 0, per_sequence(up2_ref[...]), jnp.where(rid == 1, s1, u2))
        u_ref[...] = u
        plast_ref[...] = p_ref[:, 0:RW].reshape(rows // 8, 8, RW)[:, 7, :]
    else:
        um1 = jnp.where(rid == 0, carry_u[7:8, :], u1)
        um2 = jnp.where(rid == 0, carry_u[6:7, :], jnp.where(rid == 1, carry_u[7:8, :], u2))
        u_ref[0] = u[rows - 8:, :]
        ptail_ref[0] = p_ref[rows - 8:rows, 0:RW]
    conv = cw_ref[0:1, :] * um2 + cw_ref[1:2, :] * um1 + cw_ref[2:3, :] * u
    conv_ref[...] = (bg * conv).astype(BF16)

    if not sample:
        carry_p[...] = p_ref[rows - 8:rows, 0:RW]
        carry_u[...] = u[rows - 8:, :]


def _prep(proj, weights, tri, *, n_seq, seq_len, rows, chunk, row_block_offset, extra, sample, name):
    t = n_seq * seq_len
    n_chunks = t // chunk
    cpt = rows // chunk
    if sample:
        grid = (t // rows,)
        rmap = lambda i: (i + row_block_offset, 0)
        omap = lambda i: (i, 0)
        omap3 = lambda i: (i, 0, 0)
        cmap = lambda i: (0, 0)
        sem = ("parallel",)
    else:
        tiles = seq_len // rows
        grid = (n_seq, tiles)
        rmap = lambda b, j: (b * tiles + j + row_block_offset, 0)
        omap = lambda b, j: (b * tiles + j, 0)
        omap3 = lambda b, j: (b * tiles + j, 0, 0)
        cmap = lambda b, j: (0, 0)
        sem = ("parallel", "arbitrary")

    const = lambda arr: pl.BlockSpec(arr.shape, cmap)
    in_specs = [pl.BlockSpec((rows, PW), rmap)]
    if sample:
        in_specs += [pl.BlockSpec((rows // 8, RW), omap), pl.BlockSpec((rows // 8, G), omap),
                     pl.BlockSpec((rows // 8, G), omap)]
    else:
        in_specs += [const(e) for e in extra]
    in_specs += [const(w) for w in weights] + [const(tri)]
    args = [proj] + list(extra) + list(weights) + [tri]

    row_out = jax.ShapeDtypeStruct((t, G), F32)
    row_spec = pl.BlockSpec((rows, G), omap)
    row_bf16 = jax.ShapeDtypeStruct((t, G), BF16)
    out_shape = [row_bf16] * 5 + [jax.ShapeDtypeStruct((n_chunks, 1, G), F32), row_bf16, row_out,
                                  row_bf16]
    out_specs = [row_spec] * 5 + [pl.BlockSpec((cpt, 1, G), omap3), row_spec, row_spec, row_spec]
    if sample:
        out_shape += [row_out, jax.ShapeDtypeStruct((t // 8, RW), F32)]
        out_specs += [row_spec, pl.BlockSpec((rows // 8, RW), omap)]
        scratch = []
    else:
        out_shape += [jax.ShapeDtypeStruct((t // rows, 8, G), F32),
                      jax.ShapeDtypeStruct((t // rows, 8, RW), F32)]
        out_specs += [pl.BlockSpec((1, 8, G), omap3), pl.BlockSpec((1, 8, RW), omap3)]
        scratch = [pltpu.VMEM((8, RW), F32), pltpu.VMEM((8, G), F32)]
    return pl.pallas_call(
        functools.partial(_prep_kernel, rows=rows, chunk=chunk, sample=sample),
        grid=grid, in_specs=in_specs, out_specs=out_specs, out_shape=out_shape,
        scratch_shapes=scratch,
        compiler_params=pltpu.CompilerParams(dimension_semantics=sem, vmem_limit_bytes=VMEM_LIMIT),
        name=name,
    )(*args)


def _scan_kernel(*refs, n_par, rows, chunk, per_chunk_state, group, n_cast):
    ah_ref, rh_ref, bh_ref, kh_ref, v_ref, wc_ref, s0_ref = refs[:7]
    y_ref, s_ref = refs[7 + n_cast:9 + n_cast]
    for w_ref, o_ref in zip(refs[7:7 + n_cast], refs[9 + n_cast:]):
        o_ref[...] = w_ref[...].astype(BF16)

    if per_chunk_state:
        s_in = s0_ref
    else:
        s_in = s_ref

        @pl.when(pl.program_id(1) == 0)
        def _():
            for q in range(n_par):
                s_ref[q] = s0_ref[0]

    n_blk = rows // chunk
    lanes = [slice(h * HEAD_DIM, (h + 1) * HEAD_DIM) for h in range(N_HEADS)]
    bf = lambda x: x.astype(BF16)

    shift = chunk.bit_length() - 1
    rs = lax.broadcasted_iota(jnp.int32, (rows, rows), 0)
    cs = lax.broadcasted_iota(jnp.int32, (rows, rows), 1)
    same = (rs >> shift) == (cs >> shift)
    mask_strict = same & (rs > cs)
    ri = lax.broadcasted_iota(jnp.int32, (rows, 2 * rows), 0)
    ci = lax.broadcasted_iota(jnp.int32, (rows, 2 * rows), 1)
    cj = jnp.where(ci >= rows, ci - rows, ci)
    mask_r = ((ri >> shift) == (cj >> shift)) & (ri >= cj)
    eye = (rs == cs).astype(F32)

    def chunk_rows(x, c):
        return x[c * chunk:(c + 1) * chunk, :]

    pair_lane = lax.broadcasted_iota(jnp.int32, (rows, 2 * HEAD_DIM), 1) >> (HEAD_DIM.bit_length() - 1)

    def pair_masked(ref, q, h):
        hp, hh = divmod(h, 2)
        x2 = ref[q, :, 2 * HEAD_DIM * hp:2 * HEAD_DIM * (hp + 1)].astype(F32)
        return jnp.where(pair_lane == hh, x2, 0.0)

    def unit_group(units):
        per_unit = lambda f: {u: f(*u) for u in units}
        v = per_unit(lambda q, h: v_ref[q, :, lanes[h]])
        if n_blk == 1:
            a = per_unit(lambda q, h: ah_ref[q, :, lanes[h]])
            r = per_unit(lambda q, h: rh_ref[q, :, lanes[h]])
            b = per_unit(lambda q, h: bh_ref[q, :, lanes[h]])
            k = per_unit(lambda q, h: kh_ref[q, :, lanes[h]])
        else:
            a = per_unit(lambda q, h: pair_masked(ah_ref, q, h))
            r = per_unit(lambda q, h: pair_masked(rh_ref, q, h))
            b = per_unit(lambda q, h: pair_masked(bh_ref, q, h))
            k = per_unit(lambda q, h: pair_masked(kh_ref, q, h))
            v32 = {u: v[u].astype(F32) for u in units}
        ar = {u: bf(jnp.concatenate([a[u], r[u]], axis=0)) for u in units}
        bk = {u: bf(jnp.concatenate([b[u], k[u]], axis=0)) for u in units}

        gram = {u: _dot(ar[u], jnp.transpose(bk[u])) for u in units}
        l_ab = {u: jnp.where(mask_strict, gram[u][:rows, :rows], 0.0) for u in units}
        l_ak = {u: bf(jnp.where(mask_strict, gram[u][:rows, rows:], 0.0)) for u in units}
        l_r = {u: bf(jnp.where(mask_r, gram[u][rows:, :], 0.0)) for u in units}

        xs, ys = {}, {}
        for u in units:
            q, h = u
            if n_blk == 1:
                st = _dot_nt(ar[u], bf(s_in[q, h]))
                xs[u], ys[u] = st[:rows], st[rows:]
            else:
                parts = [_dot_nt(jnp.concatenate([chunk_rows(a[u], c), chunk_rows(r[u], c)], axis=0),
                                 s_in[c, h // 2]) for c in range(n_blk)]
                xs[u] = jnp.concatenate([p[:chunk] for p in parts], axis=0)
                ys[u] = jnp.concatenate([p[chunk:] for p in parts], axis=0)

        x = {u: xs[u] + _dot(l_ak[u], v[u]) for u in units}

        t_inv = {u: eye + l_ab[u] for u in units}
        n = 2
        if n < chunk:
            pw = {u: bf(l_ab[u]) for u in units}
            pw = {u: bf(_dot(pw[u], pw[u])) for u in units}
        while n < chunk:
            if 2 * n < chunk:
                both = {u: _dot(jnp.concatenate([bf(t_inv[u]), pw[u]], axis=0), pw[u]) for u in units}
                t_inv = {u: t_inv[u] + both[u][:rows] for u in units}
                pw = {u: bf(both[u][rows:]) for u in units}
            else:
                t_inv = {u: t_inv[u] + _dot(bf(t_inv[u]), pw[u]) for u in units}
            n *= 2

        uu = {u: _dot(bf(t_inv[u]), bf(x[u])) for u in units}
        uv = {u: jnp.concatenate([bf(uu[u]), v[u]], axis=0) for u in units}
        for u in units:
            q, h = u
            y_ref[q, :, lanes[h]] = ys[u] + _dot(l_r[u], uv[u])

        if n_blk == 1:
            for u in units:
                q, h = u
                upd = _dot_tn(uv[u], bk[u])
                s_ref[q, h] = (s_in[q, h] + upd) * wc_ref[q, 0][:, lanes[h]]
        else:
            def update(u, c):
                uv_c = jnp.concatenate([chunk_rows(uu[u], c), chunk_rows(v32[u], c)], axis=0)
                bk_c = jnp.concatenate([chunk_rows(b[u], c), chunk_rows(k[u], c)], axis=0)
                return _dot_tn(uv_c, bk_c)

            for q, h in units:
                if h % 2 == 0:
                    assert (q, h + 1) in units
                    hp = h // 2
                    for c in range(n_blk):
                        s_new = s_in[c, hp] + update((q, h), c) + update((q, h + 1), c)
                        s_ref[c, hp] = s_new * wc_ref[q, c][:, 2 * HEAD_DIM * hp:2 * HEAD_DIM * (hp + 1)]

    all_units = [(q, h) for h in range(N_HEADS) for q in range(n_par)]
    for g0 in range(0, len(all_units), group):
        unit_group(all_units[g0:g0 + group])


def _scan(ah, rh, bh, kh, v, wc, s0, *, n_seq, seq_len, n_par, rows, chunk, per_chunk_state, group,
          name, cast=()):
    t = n_seq * seq_len
    n_blk = rows // chunk
    if per_chunk_state:
        assert seq_len == chunk and n_par == 1
        lead = 1
        grid = (t // rows,)
        rmap = lambda i: (0, i, 0)
        cmap = lambda i: (0, i, 0, 0)
        smap = lambda i: (i, 0, 0, 0)
        s0map = smap
        state_block = (n_blk, N_HEADS // 2, HEAD_DIM, 2 * HEAD_DIM)
        s0_block = state_block
        wc_block = (1, n_blk, 1, G)
        sem = ("parallel",)
    else:
        assert rows == chunk and s0.shape[0] == 1 and n_seq % n_par == 0
        lead = n_seq
        grid = (n_seq // n_par, seq_len // chunk)
        rmap = lambda s, c: (s, c, 0)
        cmap = lambda s, c: (s, c, 0, 0)
        smap = lambda s, c: (s, 0, 0, 0)
        s0map = lambda s, c: (0, 0, 0, 0)
        state_block = (n_par, N_HEADS, HEAD_DIM, HEAD_DIM)
        s0_block = (1, N_HEADS, HEAD_DIM, HEAD_DIM)
        wc_block = (n_par, 1, 1, G)
        sem = ("parallel", "arbitrary")
    rows3 = lambda x: x.reshape(lead, t // lead, G)
    row_spec = pl.BlockSpec((n_par, rows, G), rmap)
    n_steps = 1
    for extent in grid:
        n_steps *= extent
    step = (lambda i: i) if len(grid) == 1 else (lambda s, c: s * grid[1] + c)
    cast_specs = []
    for w in cast:
        assert w.shape[0] % (16 * n_steps) == 0
        cast_specs.append(pl.BlockSpec((w.shape[0] // n_steps, w.shape[1]), lambda *g: (step(*g), 0)))
    y, s_out, *narrowed = pl.pallas_call(
        functools.partial(_scan_kernel, n_par=n_par, rows=rows, chunk=chunk,
                          per_chunk_state=per_chunk_state, group=group, n_cast=len(cast)),
        grid=grid,
        in_specs=([row_spec] * 5 + [pl.BlockSpec(wc_block, cmap), pl.BlockSpec(s0_block, s0map)]
                  + cast_specs),
        out_specs=[row_spec, pl.BlockSpec(state_block, smap)] + cast_specs,
        out_shape=[jax.ShapeDtypeStruct((lead, t // lead, G), F32),
                   jax.ShapeDtypeStruct((n_seq,) + state_block[1:], F32)]
                  + [jax.ShapeDtypeStruct(w.shape, BF16) for w in cast],
        compiler_params=pltpu.CompilerParams(dimension_semantics=sem, vmem_limit_bytes=VMEM_LIMIT),
        name=name,
    )(rows3(ah), rows3(rh), rows3(bh), rows3(kh), rows3(v), wc.reshape(lead, -1, 1, G), s0, *cast)
    return (y.reshape(t, G), s_out, *narrowed)


def _mix_kernel(*refs, segs):
    n = len(segs)
    row_refs = [refs[4 * s:4 * s + 4] for s in range(n)]
    x_ref, lnw_ref, lnb_ref, bd_ref, wo_ref, o_ref = refs[4 * n:]
    i = pl.program_id(0)

    def body(y_ref, bonus_ref, g_ref, conv_ref):
        bd = bd_ref[...]
        y = y_ref[...]
        mu = _head_sum(y, bd) * (1.0 / HEAD_DIM)
        d = y - mu
        var = _head_sum(d * d, bd) * (1.0 / HEAD_DIM)
        yn = d * lax.rsqrt(var + GN_EPS) * lnw_ref[...] + lnb_ref[...]
        rw = ((yn + bonus_ref[...]) * g_ref[...]).astype(BF16)
        mix = _dot(rw, wo_ref[0:G, :]) + _dot(conv_ref[...], wo_ref[G:2 * G, :])
        o_ref[...] = x_ref[...] + mix

    for rr, (off, cnt) in zip(row_refs, segs):
        @pl.when((i >= off) & (i < off + cnt))
        def _(rr=rr):
            body(*rr)


def _mix(row_groups, x1, lnw, lnb, bd, wo):
    segs, n_tiles = _segments([grp[0] for grp in row_groups], TM_MIX)
    cmap = lambda i: (0, 0)
    in_specs, args = [], []
    for grp, (off, cnt) in zip(row_groups, segs):
        smap = lambda i, off=off, cnt=cnt: (jnp.clip(i - off, 0, cnt - 1), 0)
        in_specs += [pl.BlockSpec((TM_MIX, G), smap)] * 4
        args += list(grp)
    in_specs += [pl.BlockSpec((TM_MIX, D_MODEL), lambda i: (i, 0)),
                 pl.BlockSpec((1, G), cmap), pl.BlockSpec((1, G), cmap),
                 pl.BlockSpec((G, G), cmap), pl.BlockSpec((D_MODEL, D_MODEL), cmap)]
    return pl.pallas_call(
        functools.partial(_mix_kernel, segs=segs),
        grid=(n_tiles,), in_specs=in_specs,
        out_specs=pl.BlockSpec((TM_MIX, D_MODEL), lambda i: (i, 0)),
        out_shape=jax.ShapeDtypeStruct((n_tiles * TM_MIX, D_MODEL), F32),
        compiler_params=pltpu.CompilerParams(
            dimension_semantics=("parallel",), vmem_limit_bytes=VMEM_LIMIT),
        name="mix",
    )(*args, x1, lnw, lnb, bd, wo)


def _pack_rwkv(a):
    r, wd, k, v, ad, gd = jnp.split(a, RWKV_SPLITS, axis=-1)
    zw = jnp.zeros(a.shape[:-1] + (LORA_PAD - W_LORA,), a.dtype)
    za = jnp.zeros(a.shape[:-1] + (LORA_PAD - A_LORA,), a.dtype)
    return jnp.concatenate([r, k, v, wd, zw, ad, za, gd], axis=-1)


def _unpack_rwkv(p):
    return jnp.concatenate([
        p[..., 0:G], p[..., OFF_WD:OFF_WD + W_LORA], p[..., G:2 * G], p[..., 2 * G:3 * G],
        p[..., OFF_AD:OFF_AD + A_LORA], p[..., OFF_GD:OFF_GD + G_LORA]], axis=-1)


def _block_tri(n, chunk):
    i = jnp.arange(n)
    return ((i[:, None] // chunk == i[None, :] // chunk) & (i[:, None] >= i[None, :])).astype(BF16)


def kernel(x_prompt, x_sample, state_wkv, state_shift, state_conv, meta_tokens, g_ffn1, ffn1_gate, ffn1_up, ffn1_down, g_mix, w_in, mu_shift, w0, w_lora_w, a0, w_lora_a, w_lora_g, k_k, k_a, r_k, ln_x_w, ln_x_b, conv_w, w_out, g_ffn2, ffn2_gate, ffn2_up, ffn2_down, g_final):
    assert g_ffn1.shape[0] == 1, "single layer"
    nb, seq, _ = x_prompt.shape
    db, dseq, _ = x_sample.shape
    assert dseq == C_SAMPLE and N_META <= C_PROMPT and seq % ROWS_PREP == 0
    tp, ts = nb * seq, db * dseq
    assert tp % TM == 0 and ts % TM == 0

    row = lambda a: a.reshape(1, -1).astype(F32)
    pad_rows = lambda w, n: jnp.concatenate([w, jnp.zeros((n - w.shape[0], w.shape[1]), w.dtype)], axis=0)
    hid = jnp.arange(G) // HEAD_DIM
    bd = (hid[:, None] == hid[None, :]).astype(BF16)
    prep_w = (_pack_rwkv(mu_shift[0])[None], row(w0[0]), pad_rows(w_lora_w[0], LORA_PAD).astype(BF16),
              row(a0[0]), pad_rows(w_lora_a[0], LORA_PAD).astype(BF16), w_lora_g[0].astype(BF16),
              row(k_k[0]), row(k_a[0]), row(r_k[0]), conv_w[0].astype(F32), bd)

    x_meta = jnp.concatenate([jnp.zeros((C_PROMPT - N_META, D_MODEL), F32), meta_tokens.astype(F32)], axis=0)
    x1_meta, wg1, wu1, wd1 = _ffn_narrow(x_meta, row(g_ffn1[0]), ffn1_gate[0], ffn1_up[0], ffn1_down[0],
                                         name="ffn1_meta")
    x1, w_in_p = _ffn_two_stage([x_prompt.reshape(tp, D_MODEL), x_sample.reshape(ts, D_MODEL)],
                                row(g_ffn1[0]), wg1, wu1, wd1, None, [tp + ts], name="ffn1",
                                pack_src=jnp.transpose(w_in[0]))
    x1_meta = jnp.concatenate([x1_meta, jnp.zeros((TM - C_PROMPT, D_MODEL), F32)], axis=0)
    proj = _proj([x1, x1_meta], row(g_mix[0]), w_in_p, name="proj")
    meta_row0 = tp + ts

    tri_p = _block_tri(ROWS_PREP, C_PROMPT)
    zeros_state = jnp.zeros((1, N_HEADS, HEAD_DIM, HEAD_DIM), F32)
    (ah, rh, bh, kh, vv, wc, _, _, _, utail_m, ptail_m) = _prep(
        proj, prep_w, tri_p[:C_PROMPT, :C_PROMPT], n_seq=1, seq_len=C_PROMPT, rows=C_PROMPT,
        chunk=C_PROMPT, row_block_offset=meta_row0 // C_PROMPT,
        extra=(jnp.zeros((8, RW), F32), jnp.zeros((8, G), F32)),
        sample=False, name="prep_meta")
    _, wkv_m = _scan(ah, rh, bh, kh, vv, wc, zeros_state, n_seq=1, seq_len=C_PROMPT, n_par=1,
                     rows=C_PROMPT, chunk=C_PROMPT, per_chunk_state=False, group=N_HEADS,
                     name="scan_meta")

    (ah, rh, bh, kh, vv, wc, g_p, bonus_p, conv_p, utail_p, ptail_p) = _prep(
        proj, prep_w, tri_p, n_seq=nb, seq_len=seq, rows=ROWS_PREP, chunk=C_PROMPT,
        row_block_offset=0, extra=(ptail_m[0], utail_m[0]), sample=False, name="prep_prompt")
    y_p, wkv_p, wg2, wu2, wd2, wo = _scan(
        ah, rh, bh, kh, vv, wc, wkv_m, n_seq=nb, seq_len=seq, n_par=SCAN_PAR, rows=C_PROMPT,
        chunk=C_PROMPT, per_chunk_state=False, group=SCAN_PAR * N_HEADS // 2, name="scan_prompt",
        cast=(ffn2_gate[0], ffn2_up[0], ffn2_down[0], w_out[0]))

    prev = _pack_rwkv(state_shift[0])
    up1 = state_conv[0][:, 1]
    up2 = state_conv[0][:, 0]
    (ah, rh, bh, kh, vv, wc, g_s, bonus_s, conv_s, u_s, plast_s) = _prep(
        proj, prep_w, _block_tri(ROWS_PREP, C_SAMPLE), n_seq=db, seq_len=dseq, rows=ROWS_PREP,
        chunk=C_SAMPLE, row_block_offset=tp // ROWS_PREP, extra=(prev, up1, up2), sample=True,
        name="prep_sample")
    pair = (db, N_HEADS // 2, 2, HEAD_DIM, HEAD_DIM)
    s0_pairs = jnp.transpose(state_wkv[0].reshape(pair), (0, 1, 3, 2, 4)).reshape(
        db, N_HEADS // 2, HEAD_DIM, 2 * HEAD_DIM)
    y_s, wkv_s = _scan(ah, rh, bh, kh, vv, wc, s0_pairs, n_seq=db, seq_len=dseq, n_par=1,
                       rows=ROWS_SCAN_SAMPLE, chunk=C_SAMPLE, per_chunk_state=True, group=16,
                       name="scan_sample")

    x2 = _mix([(y_p, bonus_p, g_p, conv_p), (y_s, bonus_s, g_s, conv_s)], x1,
              row(ln_x_w[0]), row(ln_x_b[0]), bd, wo)
    y_prompt, y_sample = _ffn_two_stage([x2], row(g_ffn2[0]), wg2, wu2, wd2, row(g_final), [tp, ts],
                                        name="ffn2")

    shift_p = _unpack_rwkv(ptail_p.reshape(nb, seq // ROWS_PREP, 8, RW)[:, -1, 7, :])
    conv_state_p = utail_p.reshape(nb, seq // ROWS_PREP, 8, G)[:, -1, 6:, :]
    shift_s = _unpack_rwkv(plast_s)
    conv_state_s = u_s.reshape(db, dseq, G)[:, -2:, :]
    wkv_s_out = jnp.transpose(wkv_s.reshape(db, N_HEADS // 2, HEAD_DIM, 2, HEAD_DIM),
                              (0, 1, 3, 2, 4)).reshape(db, N_HEADS, HEAD_DIM, HEAD_DIM)
    return (y_prompt.reshape(nb, seq, D_MODEL), y_sample.reshape(db, dseq, D_MODEL),
            wkv_p[None].astype(state_wkv.dtype), shift_p[None].astype(state_shift.dtype),
            conv_state_p[None].astype(state_conv.dtype),
            wkv_s_out[None].astype(state_wkv.dtype), shift_s[None].astype(state_shift.dtype),
            conv_state_s[None].astype(state_conv.dtype))
```

```python
import functools

import jax
import jax.numpy as jnp
from jax import lax
from jax.experimental import pallas as pl
from jax.experimental.pallas import tpu as pltpu

F32 = jnp.float32
BF16 = jnp.bfloat16

D_MODEL = 2048
D_FF = 5632
N_META = 16
G = 1024
HEAD_DIM = 64
N_HEADS = G // HEAD_DIM
HEAD_BLOCK = 256
W_LORA = 96
A_LORA = 96
G_LORA = 256
LORA_PAD = 128
RWKV_PROJ = 3 * G + W_LORA + A_LORA + G_LORA
RWKV_SPLITS = (G, G + W_LORA, 2 * G + W_LORA, 3 * G + W_LORA, 3 * G + W_LORA + A_LORA)
RW = 3 * G + 2 * LORA_PAD + G_LORA
PW = RW + 3 * G
OFF_WD, OFF_AD, OFF_GD = 3 * G, 3 * G + LORA_PAD, 3 * G + 2 * LORA_PAD
RMS_EPS = 1e-6
GN_EPS = 64e-5
EXP_M05_LOG2E = 0.6065306597126334 * 1.4426950408889634

C_PROMPT = 64
C_SAMPLE = 8
TM = 512
TF = 512
TN = 3328
TM_FFN2 = 256
TM_MIX = 256
ROWS_PREP = 256
ROWS_SCAN_SAMPLE = 128
SCAN_PAR = 4
VMEM_LIMIT = 56 * 1024 * 1024


def _dot(a, b):
    return jnp.dot(a, b, preferred_element_type=F32)


def _dot_nt(a, b):
    return lax.dot_general(a, b, (((1,), (1,)), ((), ())), preferred_element_type=F32)


def _dot_tn(a, b):
    return lax.dot_general(a, b, (((0,), (0,)), ((), ())), preferred_element_type=F32)


def _split(a):
    hi = a.astype(BF16)
    lo = (a - hi.astype(F32)).astype(BF16)
    return hi, lo


def _head_sum(x, bd):
    xb = x.astype(BF16)
    wb = bd.shape[0]
    return jnp.concatenate([_dot(xb[:, c:c + wb], bd) for c in range(0, x.shape[1], wb)], axis=1)


def _dot_split_rhs(a_bf16, b):
    hi, lo = _split(b)
    return _dot(a_bf16, hi) + _dot(a_bf16, lo)


def _rms(x, g):
    return x * lax.rsqrt(jnp.mean(x * x, axis=-1, keepdims=True) + RMS_EPS) * g


def _segments(arrays, tile):
    segs, off = [], 0
    for arr in arrays:
        n = arr.shape[0] // tile
        assert n * tile == arr.shape[0]
        segs.append((off, n))
        off += n
    return segs, off


def _overlaps(in_segs, out_segs):
    for a, (ao, an) in enumerate(in_segs):
        for b, (bo, bn) in enumerate(out_segs):
            lo, hi = max(ao, bo), min(ao + an, bo + bn)
            if lo < hi:
                yield a, b, lo, hi


def _ffn_narrow_kernel(x_ref, g_ref, wg_ref, wu_ref, wd_ref, o_ref, wg_o, wu_o, wd_o, h_ref, acc_ref):
    j = pl.program_id(0)

    @pl.when(j == 0)
    def _():
        h_ref[...] = _rms(x_ref[...], g_ref[...]).astype(BF16)
        acc_ref[...] = jnp.zeros_like(acc_ref)

    wg, wu, wd = wg_ref[...].astype(BF16), wu_ref[...].astype(BF16), wd_ref[...].astype(BF16)
    wg_o[...], wu_o[...], wd_o[...] = wg, wu, wd
    h = h_ref[...]
    gate = _dot(h, wg)
    up = _dot(h, wu)
    act = (gate * jax.nn.sigmoid(gate) * up).astype(BF16)
    acc_ref[...] += _dot(act, wd)

    @pl.when(j == pl.num_programs(0) - 1)
    def _():
        o_ref[...] = x_ref[...] + 0.5 * acc_ref[...]


def _ffn_narrow(x, g, wg, wu, wd, *, name):
    rows = x.shape[0]
    whole = pl.BlockSpec((rows, D_MODEL), lambda j: (0, 0))
    w_specs = [pl.BlockSpec((D_MODEL, TF), lambda j: (0, j)),
               pl.BlockSpec((D_MODEL, TF), lambda j: (0, j)),
               pl.BlockSpec((TF, D_MODEL), lambda j: (j, 0))]
    return pl.pallas_call(
        _ffn_narrow_kernel,
        grid=(D_FF // TF,),
        in_specs=[whole, pl.BlockSpec((1, D_MODEL), lambda j: (0, 0))] + w_specs,
        out_specs=[whole] + w_specs,
        out_shape=[jax.ShapeDtypeStruct((rows, D_MODEL), F32)]
                  + [jax.ShapeDtypeStruct(w.shape, BF16) for w in (wg, wu, wd)],
        scratch_shapes=[pltpu.VMEM((rows, D_MODEL), BF16), pltpu.VMEM((rows, D_MODEL), F32)],
        compiler_params=pltpu.CompilerParams(
            dimension_semantics=("arbitrary",), vmem_limit_bytes=VMEM_LIMIT),
        name=name,
    )(x, g, wg, wu, wd)


def _ffn_up_kernel(*refs, in_segs):
    n_in = len(in_segs)
    x_refs = refs[:n_in]
    g_ref, wg_ref, wu_ref, a_ref = refs[n_in:]
    i = pl.program_id(1)
    for x_ref, (off, n) in zip(x_refs, in_segs):
        @pl.when((i >= off) & (i < off + n))
        def _(x_ref=x_ref):
            h = _rms(x_ref[...], g_ref[...]).astype(BF16)
            gate = _dot(h, wg_ref[...])
            up = _dot(h, wu_ref[...])
            a_ref[...] = (gate * jax.nn.sigmoid(gate) * up).astype(BF16)


def _ffn_down_kernel(*refs, in_segs, out_segs, final_norm, pack_steps):
    n_in, n_out = len(in_segs), len(out_segs)
    a_ref = refs[0]
    x_refs = refs[1:1 + n_in]
    wd_ref = refs[1 + n_in]
    rest = refs[2 + n_in:]
    if final_norm:
        gf_ref, rest = rest[0], rest[1:]
    if pack_steps:
        wt_ref, rest = rest[0], rest[1:]
    o_refs = rest[:n_out]
    i = pl.program_id(0)
    if pack_steps:
        @pl.when(i < pack_steps)
        def _():
            _pack_w_in_kernel(wt_ref, rest[n_out])

    for a, b, lo, hi in _overlaps(in_segs, out_segs):
        @pl.when((i >= lo) & (i < hi))
        def _(x_ref=x_refs[a], o_ref=o_refs[b]):
            xo = x_ref[...] + 0.5 * _dot(a_ref[...], wd_ref[...])
            if final_norm:
                xo = _rms(xo, gf_ref[...])
            o_ref[...] = xo


def _ffn_two_stage(xs, g, wg, wu, wd, g_final, out_rows, *, name, pack_src=None):
    final_norm = g_final is not None
    tm, tn = TM_FFN2, D_FF // 2
    in_segs, n_tiles = _segments(xs, tm)
    out_shape = [jax.ShapeDtypeStruct((n, D_MODEL), F32) for n in out_rows]
    out_segs, n_out_tiles = _segments(out_shape, tm)
    assert n_out_tiles == n_tiles
    seg_ji = lambda s: pl.BlockSpec((tm, D_MODEL), lambda j, i, s=s: (jnp.clip(i - s[0], 0, s[1] - 1), 0))
    half = pl.BlockSpec((D_MODEL, tn), lambda j, i: (0, j), pipeline_mode=pl.Buffered(1))
    act = pl.pallas_call(
        functools.partial(_ffn_up_kernel, in_segs=in_segs),
        grid=(D_FF // tn, n_tiles),
        in_specs=[seg_ji(s) for s in in_segs] + [pl.BlockSpec((1, D_MODEL), lambda j, i: (0, 0)), half, half],
        out_specs=pl.BlockSpec((tm, tn), lambda j, i: (i, j)),
        out_shape=jax.ShapeDtypeStruct((n_tiles * tm, D_FF), BF16),
        compiler_params=pltpu.CompilerParams(
            dimension_semantics=("arbitrary", "arbitrary"), vmem_limit_bytes=VMEM_LIMIT),
        name=name + "_up",
    )(*xs, g, wg, wu)

    seg_i = lambda s: pl.BlockSpec((tm, D_MODEL), lambda i, s=s: (jnp.clip(i - s[0], 0, s[1] - 1), 0))
    in_specs = ([pl.BlockSpec((tm, D_FF), lambda i: (i, 0))] + [seg_i(s) for s in in_segs]
                + [pl.BlockSpec((D_FF, D_MODEL), lambda i: (0, 0), pipeline_mode=pl.Buffered(1))])
    args = [act] + list(xs) + [wd]
    if final_norm:
        in_specs.append(pl.BlockSpec((1, D_MODEL), lambda i: (0, 0)))
        args.append(g_final)
    out_specs = [seg_i(s) for s in out_segs]
    pack_steps = 0
    if pack_src is not None:
        lane_blk = 128
        pack_steps = D_MODEL // lane_blk
        assert pack_steps <= n_tiles
        blk = lambda i: (0, jnp.minimum(i, pack_steps - 1))
        in_specs.append(pl.BlockSpec((pack_src.shape[0], lane_blk), blk))
        args.append(pack_src)
        out_specs.append(pl.BlockSpec((PW, lane_blk), blk))
        out_shape = out_shape + [jax.ShapeDtypeStruct((PW, D_MODEL), BF16)]
    return pl.pallas_call(
        functools.partial(_ffn_down_kernel, in_segs=in_segs, out_segs=out_segs, final_norm=final_norm,
                          pack_steps=pack_steps),
        grid=(n_tiles,),
        in_specs=in_specs,
        out_specs=out_specs,
        out_shape=out_shape,
        compiler_params=pltpu.CompilerParams(
            dimension_semantics=("arbitrary",), vmem_limit_bytes=VMEM_LIMIT),
        name=name + "_down",
    )(*args)


_PACK_MOVES = (
    (0, 0, G),
    (G, RWKV_SPLITS[1], G),
    (2 * G, RWKV_SPLITS[2], G),
    (OFF_WD, RWKV_SPLITS[0], W_LORA),
    (OFF_AD, RWKV_SPLITS[3], A_LORA),
    (OFF_GD, RWKV_SPLITS[4], G_LORA),
    (RW, RWKV_PROJ, 3 * G),
)


def _pack_w_in_kernel(w_ref, o_ref):
    for dst, src, n in _PACK_MOVES:
        o_ref[dst:dst + n, :] = w_ref[src:src + n, :].astype(BF16)
    for lo, hi in ((OFF_WD + W_LORA, OFF_AD), (OFF_AD + A_LORA, OFF_GD)):
        o_ref[lo:hi, :] = jnp.zeros((hi - lo, o_ref.shape[1]), BF16)


def _proj_kernel(*refs, in_segs):
    n_in = len(in_segs)
    x_refs = refs[:n_in]
    g_ref, w_ref, o_ref = refs[n_in:]
    i = pl.program_id(1)
    for x_ref, (off, n) in zip(x_refs, in_segs):
        @pl.when((i >= off) & (i < off + n))
        def _(x_ref=x_ref):
            h = _rms(x_ref[...], g_ref[...]).astype(BF16)
            o_ref[...] = _dot_nt(h, w_ref[...])


def _proj(xs, g, w, *, name):
    in_segs, n_tiles = _segments(xs, TM)
    seg_ji = lambda s: pl.BlockSpec((TM, D_MODEL), lambda j, i, s=s: (jnp.clip(i - s[0], 0, s[1] - 1), 0),
                                    pipeline_mode=pl.Buffered(1 if s[1] == 1 else 2))
    return pl.pallas_call(
        functools.partial(_proj_kernel, in_segs=in_segs),
        grid=(PW // TN, n_tiles),
        in_specs=[seg_ji(s) for s in in_segs] + [
            pl.BlockSpec((1, D_MODEL), lambda j, i: (0, 0)),
            pl.BlockSpec((TN, D_MODEL), lambda j, i: (j, 0)),
        ],
        out_specs=pl.BlockSpec((TM, TN), lambda j, i: (i, j)),
        out_shape=jax.ShapeDtypeStruct((n_tiles * TM, PW), F32),
        compiler_params=pltpu.CompilerParams(
            dimension_semantics=("parallel", "parallel"), vmem_limit_bytes=VMEM_LIMIT),
        name=name,
    )(*xs, g, w)


def _prep_kernel(*refs, rows, chunk, sample):
    if sample:
        (p_ref, prev_ref, up1_ref, up2_ref, mu_ref, w0_ref, ww_ref, a0_ref, wa_ref, wgl_ref,
         kk_ref, ka_ref, rk_ref, cw_ref, bd_ref, tri_ref,
         ah_ref, rh_ref, bh_ref, kh_ref, v_ref, wc_ref, g_ref, bonus_ref, conv_ref, u_ref,
         plast_ref) = refs
    else:
        (p_ref, cp0_ref, cu0_ref, mu_ref, w0_ref, ww_ref, a0_ref, wa_ref, wgl_ref,
         kk_ref, ka_ref, rk_ref, cw_ref, bd_ref, tri_ref,
         ah_ref, rh_ref, bh_ref, kh_ref, v_ref, wc_ref, g_ref, bonus_ref, conv_ref, u_ref,
         ptail_ref, carry_p, carry_u) = refs

        @pl.when(pl.program_id(1) == 0)
        def _():
            carry_p[...] = cp0_ref[...]
            carry_u[...] = cu0_ref[...]

    def row_ids(width):
        r = lax.broadcasted_iota(jnp.int32, (rows, width), 0)
        return (r & 7) if sample else r

    def per_sequence(x):
        n, w = x.shape
        return jnp.broadcast_to(x[:, None, :], (n, 8, w)).reshape(n * 8, w)

    def shifted_mix(lo, hi):
        p = p_ref[:, lo:hi]
        rolled = pltpu.roll(p, 1, 0)
        if sample:
            sh = jnp.where(row_ids(hi - lo) == 0, per_sequence(prev_ref[:, lo:hi]), rolled)
        else:
            sh = jnp.where(row_ids(hi - lo) == 0, carry_p[7:8, lo:hi], rolled)
        return p + (sh - p) * mu_ref[:, lo:hi]

    r = shifted_mix(0, G)
    k = shifted_mix(G, 2 * G)
    v = shifted_mix(2 * G, 3 * G)
    wd = shifted_mix(OFF_WD, OFF_WD + LORA_PAD)
    ad = shifted_mix(OFF_AD, OFF_AD + LORA_PAD)
    gd = shifted_mix(OFF_GD, OFF_GD + G_LORA)

    z = w0_ref[...] + _dot(jnp.tanh(wd).astype(BF16), ww_ref[...])
    lw = -EXP_M05_LOG2E * jax.nn.sigmoid(z)
    a = jax.nn.sigmoid(a0_ref[...] + _dot(ad.astype(BF16), wa_ref[...]))
    g_ref[...] = _dot(jax.nn.sigmoid(gd).astype(BF16), wgl_ref[...]).astype(BF16)

    bd = bd_ref[...]
    kk = k * kk_ref[...]
    norm = jnp.sqrt(_head_sum(kk * kk, bd))
    kk = kk / jnp.maximum(norm, 1e-12)
    km = k * (1.0 + (a - 1.0) * ka_ref[...])
    bonus_ref[...] = _head_sum(r * km * rk_ref[...], bd) * v

    cum = _dot_split_rhs(tri_ref[...], lw)
    e_cum = jnp.exp2(cum)
    e_inv = jnp.exp2(-cum)
    ah_ref[...] = (-kk * jnp.exp2(cum - lw)).astype(BF16)
    rh_ref[...] = (r * e_cum).astype(BF16)
    bh_ref[...] = (kk * a * e_inv).astype(BF16)
    kh_ref[...] = (km * e_inv).astype(BF16)
    v_ref[...] = v.astype(BF16)
    for c in range(rows // chunk):
        wc_ref[c] = e_cum[(c + 1) * chunk - 1:(c + 1) * chunk, :]

    bg = p_ref[:, RW:RW + G]
    u = p_ref[:, RW + G:RW + 2 * G] * p_ref[:, RW + 2 * G:RW + 3 * G]
    u1 = pltpu.roll(u, 1, 0)
    u2 = pltpu.roll(u, 2, 0)
    rid = row_ids(G)
    if sample:
        s1 = per_sequence(up1_ref[...])
        um1 = jnp.where(rid == 0, s1, u1)
        um2 = jnp.where(rid == 0, per_sequence(up2_ref[...]), jnp.where(rid == 1, s1, u2))
        u_ref[...] = u
        plast_ref[...] = p_ref[:, 0:RW].reshape(rows // 8, 8, RW)[:, 7, :]
    else:
        um1 = jnp.where(rid == 0, carry_u[7:8, :], u1)
        um2 = jnp.where(rid == 0, carry_u[6:7, :], jnp.where(rid == 1, carry_u[7:8, :], u2))
        u_ref[0] = u[rows - 8:, :]
        ptail_ref[0] = p_ref[rows - 8:rows, 0:RW]
    conv = cw_ref[0:1, :] * um2 + cw_ref[1:2, :] * um1 + cw_ref[2:3, :] * u
    conv_ref[...] = (bg * conv).astype(BF16)

    if not sample:
        carry_p[...] = p_ref[rows - 8:rows, 0:RW]
        carry_u[...] = u[rows - 8:, :]


def _prep(proj, weights, tri, *, n_seq, seq_len, rows, chunk, row_block_offset, extra, sample, name):
    t = n_seq * seq_len
    n_chunks = t // chunk
    cpt = rows // chunk
    if sample:
        grid = (t // rows,)
        rmap = lambda i: (i + row_block_offset, 0)
        omap = lambda i: (i, 0)
        omap3 = lambda i: (i, 0, 0)
        cmap = lambda i: (0, 0)
        sem = ("parallel",)
    else:
        tiles = seq_len // rows
        grid = (n_seq, tiles)
        rmap = lambda b, j: (b * tiles + j + row_block_offset, 0)
        omap = lambda b, j: (b * tiles + j, 0)
        omap3 = lambda b, j: (b * tiles + j, 0, 0)
        cmap = lambda b, j: (0, 0)
        sem = ("parallel", "arbitrary")

    const = lambda arr: pl.BlockSpec(arr.shape, cmap)
    in_specs = [pl.BlockSpec((rows, PW), rmap)]
    if sample:
        in_specs += [pl.BlockSpec((rows // 8, RW), omap), pl.BlockSpec((rows // 8, G), omap),
                     pl.BlockSpec((rows // 8, G), omap)]
    else:
        in_specs += [const(e) for e in extra]
    in_specs += [const(w) for w in weights] + [const(tri)]
    args = [proj] + list(extra) + list(weights) + [tri]

    row_out = jax.ShapeDtypeStruct((t, G), F32)
    row_spec = pl.BlockSpec((rows, G), omap)
    row_bf16 = jax.ShapeDtypeStruct((t, G), BF16)
    out_shape = [row_bf16] * 5 + [jax.ShapeDtypeStruct((n_chunks, 1, G), F32), row_bf16, row_out,
                                  row_bf16]
    out_specs = [row_spec] * 5 + [pl.BlockSpec((cpt, 1, G), omap3), row_spec, row_spec, row_spec]
    if sample:
        out_shape += [row_out, jax.ShapeDtypeStruct((t // 8, RW), F32)]
        out_specs += [row_spec, pl.BlockSpec((rows // 8, RW), omap)]
        scratch = []
    else:
        out_shape += [jax.ShapeDtypeStruct((t // rows, 8, G), F32),
                      jax.ShapeDtypeStruct((t // rows, 8, RW), F32)]
        out_specs += [pl.BlockSpec((1, 8, G), omap3), pl.BlockSpec((1, 8, RW), omap3)]
        scratch = [pltpu.VMEM((8, RW), F32), pltpu.VMEM((8, G), F32)]
    return pl.pallas_call(
        functools.partial(_prep_kernel, rows=rows, chunk=chunk, sample=sample),
        grid=grid, in_specs=in_specs, out_specs=out_specs, out_shape=out_shape,
        scratch_shapes=scratch,
        compiler_params=pltpu.CompilerParams(dimension_semantics=sem, vmem_limit_bytes=VMEM_LIMIT),
        name=name,
    )(*args)


def _scan_kernel(*refs, n_par, rows, chunk, per_chunk_state, group, n_cast):
    ah_ref, rh_ref, bh_ref, kh_ref, v_ref, wc_ref, s0_ref = refs[:7]
    y_ref, s_ref = refs[7 + n_cast:9 + n_cast]
    for w_ref, o_ref in zip(refs[7:7 + n_cast], refs[9 + n_cast:]):
        o_ref[...] = w_ref[...].astype(BF16)

    if per_chunk_state:
        s_in = s0_ref
    else:
        s_in = s_ref

        @pl.when(pl.program_id(1) == 0)
        def _():
            for q in range(n_par):
                s_ref[q] = s0_ref[0]

    n_blk = rows // chunk
    lanes = [slice(h * HEAD_DIM, (h + 1) * HEAD_DIM) for h in range(N_HEADS)]
    bf = lambda x: x.astype(BF16)

    shift = chunk.bit_length() - 1
    rs = lax.broadcasted_iota(jnp.int32, (rows, rows), 0)
    cs = lax.broadcasted_iota(jnp.int32, (rows, rows), 1)
    same = (rs >> shift) == (cs >> shift)
    mask_strict = same & (rs > cs)
    ri = lax.broadcasted_iota(jnp.int32, (rows, 2 * rows), 0)
    ci = lax.broadcasted_iota(jnp.int32, (rows, 2 * rows), 1)
    cj = jnp.where(ci >= rows, ci - rows, ci)
    mask_r = ((ri >> shift) == (cj >> shift)) & (ri >= cj)
    eye = (rs == cs).astype(F32)

    def chunk_rows(x, c):
        return x[c * chunk:(c + 1) * chunk, :]

    pair_lane = lax.broadcasted_iota(jnp.int32, (rows, 2 * HEAD_DIM), 1) >> (HEAD_DIM.bit_length() - 1)

    def pair_masked(ref, q, h):
        hp, hh = divmod(h, 2)
        x2 = ref[q, :, 2 * HEAD_DIM * hp:2 * HEAD_DIM * (hp + 1)].astype(F32)
        return jnp.where(pair_lane == hh, x2, 0.0)

    def unit_group(units):
        per_unit = lambda f: {u: f(*u) for u in units}
        v = per_unit(lambda q, h: v_ref[q, :, lanes[h]])
        if n_blk == 1:
            a = per_unit(lambda q, h: ah_ref[q, :, lanes[h]])
            r = per_unit(lambda q, h: rh_ref[q, :, lanes[h]])
            b = per_unit(lambda q, h: bh_ref[q, :, lanes[h]])
            k = per_unit(lambda q, h: kh_ref[q, :, lanes[h]])
        else:
            a = per_unit(lambda q, h: pair_masked(ah_ref, q, h))
            r = per_unit(lambda q, h: pair_masked(rh_ref, q, h))
            b = per_unit(lambda q, h: pair_masked(bh_ref, q, h))
            k = per_unit(lambda q, h: pair_masked(kh_ref, q, h))
            v32 = {u: v[u].astype(F32) for u in units}
        ar = {u: bf(jnp.concatenate([a[u], r[u]], axis=0)) for u in units}
        bk = {u: bf(jnp.concatenate([b[u], k[u]], axis=0)) for u in units}

        gram = {u: _dot(ar[u], jnp.transpose(bk[u])) for u in units}
        l_ab = {u: jnp.where(mask_strict, gram[u][:rows, :rows], 0.0) for u in units}
        l_ak = {u: bf(jnp.where(mask_strict, gram[u][:rows, rows:], 0.0)) for u in units}
        l_r = {u: bf(jnp.where(mask_r, gram[u][rows:, :], 0.0)) for u in units}

        xs, ys = {}, {}
        for u in units:
            q, h = u
            if n_blk == 1:
                st = _dot_nt(ar[u], bf(s_in[q, h]))
                xs[u], ys[u] = st[:rows], st[rows:]
            else:
                parts = [_dot_nt(jnp.concatenate([chunk_rows(a[u], c), chunk_rows(r[u], c)], axis=0),
                                 s_in[c, h // 2]) for c in range(n_blk)]
                xs[u] = jnp.concatenate([p[:chunk] for p in parts], axis=0)
                ys[u] = jnp.concatenate([p[chunk:] for p in parts], axis=0)

        x = {u: xs[u] + _dot(l_ak[u], v[u]) for u in units}

        t_inv = {u: eye + l_ab[u] for u in units}
        n = 2
        if n < chunk:
            pw = {u: bf(l_ab[u]) for u in units}
            pw = {u: bf(_dot(pw[u], pw[u])) for u in units}
        while n < chunk:
            if 2 * n < chunk:
                both = {u: _dot(jnp.concatenate([bf(t_inv[u]), pw[u]], axis=0), pw[u]) for u in units}
                t_inv = {u: t_inv[u] + both[u][:rows] for u in units}
                pw = {u: bf(both[u][rows:]) for u in units}
            else:
                t_inv = {u: t_inv[u] + _dot(bf(t_inv[u]), pw[u]) for u in units}
            n *= 2

        uu = {u: _dot(bf(t_inv[u]), bf(x[u])) for u in units}
        uv = {u: jnp.concatenate([bf(uu[u]), v[u]], axis=0) for u in units}
        for u in units:
            q, h = u
            y_ref[q, :, lanes[h]] = ys[u] + _dot(l_r[u], uv[u])

        if n_blk == 1:
            for u in units:
                q, h = u
                upd = _dot_tn(uv[u], bk[u])
                s_ref[q, h] = (s_in[q, h] + upd) * wc_ref[q, 0][:, lanes[h]]
        else:
            def update(u, c):
                uv_c = jnp.concatenate([chunk_rows(uu[u], c), chunk_rows(v32[u], c)], axis=0)
                bk_c = jnp.concatenate([chunk_rows(b[u], c), chunk_rows(k[u], c)], axis=0)
                return _dot_tn(uv_c, bk_c)

            for q, h in units:
                if h % 2 == 0:
                    assert (q, h + 1) in units
                    hp = h // 2
                    for c in range(n_blk):
                        s_new = s_in[c, hp] + update((q, h), c) + update((q, h + 1), c)
                        s_ref[c, hp] = s_new * wc_ref[q, c][:, 2 * HEAD_DIM * hp:2 * HEAD_DIM * (hp + 1)]

    all_units = [(q, h) for h in range(N_HEADS) for q in range(n_par)]
    for g0 in range(0, len(all_units), group):
        unit_group(all_units[g0:g0 + group])


def _scan(ah, rh, bh, kh, v, wc, s0, *, n_seq, seq_len, n_par, rows, chunk, per_chunk_state, group,
          name, cast=()):
    t = n_seq * seq_len
    n_blk = rows // chunk
    if per_chunk_state:
        assert seq_len == chunk and n_par == 1
        lead = 1
        grid = (t // rows,)
        rmap = lambda i: (0, i, 0)
        cmap = lambda i: (0, i, 0, 0)
        smap = lambda i: (i, 0, 0, 0)
        s0map = smap
        state_block = (n_blk, N_HEADS // 2, HEAD_DIM, 2 * HEAD_DIM)
        s0_block = state_block
        wc_block = (1, n_blk, 1, G)
        sem = ("parallel",)
    else:
        assert rows == chunk and s0.shape[0] == 1 and n_seq % n_par == 0
        lead = n_seq
        grid = (n_seq // n_par, seq_len // chunk)
        rmap = lambda s, c: (s, c, 0)
        cmap = lambda s, c: (s, c, 0, 0)
        smap = lambda s, c: (s, 0, 0, 0)
        s0map = lambda s, c: (0, 0, 0, 0)
        state_block = (n_par, N_HEADS, HEAD_DIM, HEAD_DIM)
        s0_block = (1, N_HEADS, HEAD_DIM, HEAD_DIM)
        wc_block = (n_par, 1, 1, G)
        sem = ("parallel", "arbitrary")
    rows3 = lambda x: x.reshape(lead, t // lead, G)
    row_spec = pl.BlockSpec((n_par, rows, G), rmap)
    n_steps = 1
    for extent in grid:
        n_steps *= extent
    step = (lambda i: i) if len(grid) == 1 else (lambda s, c: s * grid[1] + c)
    cast_specs = []
    for w in cast:
        assert w.shape[0] % (16 * n_steps) == 0
        cast_specs.append(pl.BlockSpec((w.shape[0] // n_steps, w.shape[1]), lambda *g: (step(*g), 0)))
    y, s_out, *narrowed = pl.pallas_call(
        functools.partial(_scan_kernel, n_par=n_par, rows=rows, chunk=chunk,
                          per_chunk_state=per_chunk_state, group=group, n_cast=len(cast)),
        grid=grid,
        in_specs=([row_spec] * 5 + [pl.BlockSpec(wc_block, cmap), pl.BlockSpec(s0_block, s0map)]
                  + cast_specs),
        out_specs=[row_spec, pl.BlockSpec(state_block, smap)] + cast_specs,
        out_shape=[jax.ShapeDtypeStruct((lead, t // lead, G), F32),
                   jax.ShapeDtypeStruct((n_seq,) + state_block[1:], F32)]
                  + [jax.ShapeDtypeStruct(w.shape, BF16) for w in cast],
        compiler_params=pltpu.CompilerParams(dimension_semantics=sem, vmem_limit_bytes=VMEM_LIMIT),
        name=name,
    )(rows3(ah), rows3(rh), rows3(bh), rows3(kh), rows3(v), wc.reshape(lead, -1, 1, G), s0, *cast)
    return (y.reshape(t, G), s_out, *narrowed)


def _mix_kernel(*refs, segs):
    n = len(segs)
    row_refs = [refs[4 * s:4 * s + 4] for s in range(n)]
    x_ref, lnw_ref, lnb_ref, bd_ref, wo_ref, o_ref = refs[4 * n:]
    i = pl.program_id(0)

    def body(y_ref, bonus_ref, g_ref, conv_ref):
        bd = bd_ref[...]
        y = y_ref[...]
        mu = _head_sum(y, bd) * (1.0 / HEAD_DIM)
        d = y - mu
        var = _head_sum(d * d, bd) * (1.0 / HEAD_DIM)
        yn = d * lax.rsqrt(var + GN_EPS) * lnw_ref[...] + lnb_ref[...]
        rw = ((yn + bonus_ref[...]) * g_ref[...]).astype(BF16)
        mix = _dot(rw, wo_ref[0:G, :]) + _dot(conv_ref[...], wo_ref[G:2 * G, :])
        o_ref[...] = x_ref[...] + mix

    for rr, (off, cnt) in zip(row_refs, segs):
        @pl.when((i >= off) & (i < off + cnt))
        def _(rr=rr):
            body(*rr)


def _mix(row_groups, x1, lnw, lnb, bd, wo):
    segs, n_tiles = _segments([grp[0] for grp in row_groups], TM_MIX)
    cmap = lambda i: (0, 0)
    in_specs, args = [], []
    for grp, (off, cnt) in zip(row_groups, segs):
        smap = lambda i, off=off, cnt=cnt: (jnp.clip(i - off, 0, cnt - 1), 0)
        in_specs += [pl.BlockSpec((TM_MIX, G), smap)] * 4
        args += list(grp)
    in_specs += [pl.BlockSpec((TM_MIX, D_MODEL), lambda i: (i, 0)),
                 pl.BlockSpec((1, G), cmap), pl.BlockSpec((1, G), cmap),
                 pl.BlockSpec(bd.shape, cmap), pl.BlockSpec((D_MODEL, D_MODEL), cmap)]
    return pl.pallas_call(
        functools.partial(_mix_kernel, segs=segs),
        grid=(n_tiles,), in_specs=in_specs,
        out_specs=pl.BlockSpec((TM_MIX, D_MODEL), lambda i: (i, 0)),
        out_shape=jax.ShapeDtypeStruct((n_tiles * TM_MIX, D_MODEL), F32),
        compiler_params=pltpu.CompilerParams(
            dimension_semantics=("parallel",), vmem_limit_bytes=VMEM_LIMIT),
        name="mix",
    )(*args, x1, lnw, lnb, bd, wo)


def _pack_rwkv(a):
    r, wd, k, v, ad, gd = jnp.split(a, RWKV_SPLITS, axis=-1)
    zw = jnp.zeros(a.shape[:-1] + (LORA_PAD - W_LORA,), a.dtype)
    za = jnp.zeros(a.shape[:-1] + (LORA_PAD - A_LORA,), a.dtype)
    return jnp.concatenate([r, k, v, wd, zw, ad, za, gd], axis=-1)


def _unpack_rwkv(p):
    return jnp.concatenate([
        p[..., 0:G], p[..., OFF_WD:OFF_WD + W_LORA], p[..., G:2 * G], p[..., 2 * G:3 * G],
        p[..., OFF_AD:OFF_AD + A_LORA], p[..., OFF_GD:OFF_GD + G_LORA]], axis=-1)


def _block_tri(n, chunk):
    i = jnp.arange(n)
    return ((i[:, None] // chunk == i[None, :] // chunk) & (i[:, None] >= i[None, :])).astype(BF16)


def kernel(x_prompt, x_sample, state_wkv, state_shift, state_conv, meta_tokens, g_ffn1, ffn1_gate, ffn1_up, ffn1_down, g_mix, w_in, mu_shift, w0, w_lora_w, a0, w_lora_a, w_lora_g, k_k, k_a, r_k, ln_x_w, ln_x_b, conv_w, w_out, g_ffn2, ffn2_gate, ffn2_up, ffn2_down, g_final):
    assert g_ffn1.shape[0] == 1, "single layer"
    nb, seq, _ = x_prompt.shape
    db, dseq, _ = x_sample.shape
    assert dseq == C_SAMPLE and N_META <= C_PROMPT and seq % ROWS_PREP == 0
    tp, ts = nb * seq, db * dseq
    assert tp % TM == 0 and ts % TM == 0

    row = lambda a: a.reshape(1, -1).astype(F32)
    pad_rows = lambda w, n: jnp.concatenate([w, jnp.zeros((n - w.shape[0], w.shape[1]), w.dtype)], axis=0)
    hid = jnp.arange(HEAD_BLOCK) // HEAD_DIM
    bd = (hid[:, None] == hid[None, :]).astype(BF16)
    prep_w = (_pack_rwkv(mu_shift[0])[None], row(w0[0]), pad_rows(w_lora_w[0], LORA_PAD).astype(BF16),
              row(a0[0]), pad_rows(w_lora_a[0], LORA_PAD).astype(BF16), w_lora_g[0].astype(BF16),
              row(k_k[0]), row(k_a[0]), row(r_k[0]), conv_w[0].astype(F32), bd)

    x_meta = jnp.concatenate([jnp.zeros((C_PROMPT - N_META, D_MODEL), F32), meta_tokens.astype(F32)], axis=0)
    x1_meta, wg1, wu1, wd1 = _ffn_narrow(x_meta, row(g_ffn1[0]), ffn1_gate[0], ffn1_up[0], ffn1_down[0],
                                         name="ffn1_meta")
    x1, w_in_p = _ffn_two_stage([x_prompt.reshape(tp, D_MODEL), x_sample.reshape(ts, D_MODEL)],
                                row(g_ffn1[0]), wg1, wu1, wd1, None, [tp + ts], name="ffn1",
                                pack_src=jnp.transpose(w_in[0]))
    x1_meta = jnp.concatenate([x1_meta, jnp.zeros((TM - C_PROMPT, D_MODEL), F32)], axis=0)
    proj = _proj([x1, x1_meta], row(g_mix[0]), w_in_p, name="proj")
    meta_row0 = tp + ts

    tri_p = _block_tri(ROWS_PREP, C_PROMPT)
    zeros_state = jnp.zeros((1, N_HEADS, HEAD_DIM, HEAD_DIM), F32)
    (ah, rh, bh, kh, vv, wc, _, _, _, utail_m, ptail_m) = _prep(
        proj, prep_w, tri_p[:C_PROMPT, :C_PROMPT], n_seq=1, seq_len=C_PROMPT, rows=C_PROMPT,
        chunk=C_PROMPT, row_block_offset=meta_row0 // C_PROMPT,
        extra=(jnp.zeros((8, RW), F32), jnp.zeros((8, G), F32)),
        sample=False, name="prep_meta")
    _, wkv_m = _scan(ah, rh, bh, kh, vv, wc, zeros_state, n_seq=1, seq_len=C_PROMPT, n_par=1,
                     rows=C_PROMPT, chunk=C_PROMPT, per_chunk_state=False, group=N_HEADS,
                     name="scan_meta")

    (ah, rh, bh, kh, vv, wc, g_p, bonus_p, conv_p, utail_p, ptail_p) = _prep(
        proj, prep_w, tri_p, n_seq=nb, seq_len=seq, rows=ROWS_PREP, chunk=C_PROMPT,
        row_block_offset=0, extra=(ptail_m[0], utail_m[0]), sample=False, name="prep_prompt")
    y_p, wkv_p, wg2, wu2, wd2, wo = _scan(
        ah, rh, bh, kh, vv, wc, wkv_m, n_seq=nb, seq_len=seq, n_par=SCAN_PAR, rows=C_PROMPT,
        chunk=C_PROMPT, per_chunk_state=False, group=SCAN_PAR * N_HEADS // 2, name="scan_prompt",
        cast=(ffn2_gate[0], ffn2_up[0], ffn2_down[0], w_out[0]))

    prev = _pack_rwkv(state_shift[0])
    up1 = state_conv[0][:, 1]
    up2 = state_conv[0][:, 0]
    (ah, rh, bh, kh, vv, wc, g_s, bonus_s, conv_s, u_s, plast_s) = _prep(
        proj, prep_w, _block_tri(ROWS_PREP, C_SAMPLE), n_seq=db, seq_len=dseq, rows=ROWS_PREP,
        chunk=C_SAMPLE, row_block_offset=tp // ROWS_PREP, extra=(prev, up1, up2), sample=True,
        name="prep_sample")
    pair = (db, N_HEADS // 2, 2, HEAD_DIM, HEAD_DIM)
    s0_pairs = jnp.transpose(state_wkv[0].reshape(pair), (0, 1, 3, 2, 4)).reshape(
        db, N_HEADS // 2, HEAD_DIM, 2 * HEAD_DIM)
    y_s, wkv_s = _scan(ah, rh, bh, kh, vv, wc, s0_pairs, n_seq=db, seq_len=dseq, n_par=1,
                       rows=ROWS_SCAN_SAMPLE, chunk=C_SAMPLE, per_chunk_state=True, group=16,
                       name="scan_sample")

    x2 = _mix([(y_p, bonus_p, g_p, conv_p), (y_s, bonus_s, g_s, conv_s)], x1,
              row(ln_x_w[0]), row(ln_x_b[0]), bd, wo)
    y_prompt, y_sample = _ffn_two_stage([x2], row(g_ffn2[0]), wg2, wu2, wd2, row(g_final), [tp, ts],
                                        name="ffn2")

    shift_p = _unpack_rwkv(ptail_p.reshape(nb, seq // ROWS_PREP, 8, RW)[:, -1, 7, :])
    conv_state_p = utail_p.reshape(nb, seq // ROWS_PREP, 8, G)[:, -1, 6:, :]
    shift_s = _unpack_rwkv(plast_s)
    conv_state_s = u_s.reshape(db, dseq, G)[:, -2:, :]
    wkv_s_out = jnp.transpose(wkv_s.reshape(db, N_HEADS // 2, HEAD_DIM, 2, HEAD_DIM),
                              (0, 1, 3, 2, 4)).reshape(db, N_HEADS, HEAD_DIM, HEAD_DIM)
    return (y_prompt.reshape(nb, seq, D_MODEL), y_sample.reshape(db, dseq, D_MODEL),
            wkv_p[None].astype(state_wkv.dtype), shift_p[None].astype(state_shift.dtype),
            conv_state_p[None].astype(state_conv.dtype),
            wkv_s_out[None].astype(state_wkv.dtype), shift_s[None].astype(state_shift.dtype),
            conv_state_s[None].astype(state_conv.dtype))
```

```python
import functools

import jax
import jax.numpy as jnp
from jax import lax
from jax.experimental import pallas as pl
from jax.experimental.pallas import tpu as pltpu

F32 = jnp.float32
BF16 = jnp.bfloat16

D_MODEL = 2048
D_FF = 5632
N_META = 16
G = 1024
HEAD_DIM = 64
N_HEADS = G // HEAD_DIM
HEAD_BLOCK = 256
W_LORA = 96
A_LORA = 96
G_LORA = 256
LORA_PAD = 128
RWKV_PROJ = 3 * G + W_LORA + A_LORA + G_LORA
RWKV_SPLITS = (G, G + W_LORA, 2 * G + W_LORA, 3 * G + W_LORA, 3 * G + W_LORA + A_LORA)
RW = 3 * G + 2 * LORA_PAD + G_LORA
PW = RW + 3 * G
OFF_WD, OFF_AD, OFF_GD = 3 * G, 3 * G + LORA_PAD, 3 * G + 2 * LORA_PAD
RMS_EPS = 1e-6
GN_EPS = 64e-5
EXP_M05_LOG2E = 0.6065306597126334 * 1.4426950408889634

C_PROMPT = 64
C_SAMPLE = 8
TM = 512
TF = 512
TN = 3328
TM_FFN2 = 256
TM_MIX = 256
ROWS_PREP = 256
ROWS_SCAN_SAMPLE = 128
SCAN_PAR = 4
VMEM_LIMIT = 56 * 1024 * 1024


def _dot(a, b):
    return jnp.dot(a, b, preferred_element_type=F32)


def _dot_nt(a, b):
    return lax.dot_general(a, b, (((1,), (1,)), ((), ())), preferred_element_type=F32)


def _dot_tn(a, b):
    return lax.dot_general(a, b, (((0,), (0,)), ((), ())), preferred_element_type=F32)


def _split(a):
    hi = a.astype(BF16)
    lo = (a - hi.astype(F32)).astype(BF16)
    return hi, lo


def _head_sum(x, bd):
    xb = x.astype(BF16)
    wb = bd.shape[0]
    return jnp.concatenate([_dot(xb[:, c:c + wb], bd) for c in range(0, x.shape[1], wb)], axis=1)


def _dot_split_rhs(a_bf16, b):
    hi, lo = _split(b)
    return _dot(a_bf16, hi) + _dot(a_bf16, lo)


def _rms(x, g):
    return x * lax.rsqrt(jnp.mean(x * x, axis=-1, keepdims=True) + RMS_EPS) * g


def _segments(arrays, tile):
    segs, off = [], 0
    for arr in arrays:
        n = arr.shape[0] // tile
        assert n * tile == arr.shape[0]
        segs.append((off, n))
        off += n
    return segs, off


def _overlaps(in_segs, out_segs):
    for a, (ao, an) in enumerate(in_segs):
        for b, (bo, bn) in enumerate(out_segs):
            lo, hi = max(ao, bo), min(ao + an, bo + bn)
            if lo < hi:
                yield a, b, lo, hi


def _ffn_narrow_kernel(x_ref, g_ref, wg_ref, wu_ref, wd_ref, o_ref, wg_o, wu_o, wd_o, h_ref, acc_ref):
    j = pl.program_id(0)

    @pl.when(j == 0)
    def _():
        h_ref[...] = _rms(x_ref[...], g_ref[...]).astype(BF16)
        acc_ref[...] = jnp.zeros_like(acc_ref)

    wg, wu, wd = wg_ref[...].astype(BF16), wu_ref[...].astype(BF16), wd_ref[...].astype(BF16)
    wg_o[...], wu_o[...], wd_o[...] = wg, wu, wd
    h = h_ref[...]
    gate = _dot(h, wg)
    up = _dot(h, wu)
    act = (gate * jax.nn.sigmoid(gate) * up).astype(BF16)
    acc_ref[...] += _dot(act, wd)

    @pl.when(j == pl.num_programs(0) - 1)
    def _():
        o_ref[...] = x_ref[...] + 0.5 * acc_ref[...]


def _ffn_narrow(x, g, wg, wu, wd, *, name):
    rows = x.shape[0]
    whole = pl.BlockSpec((rows, D_MODEL), lambda j: (0, 0))
    w_specs = [pl.BlockSpec((D_MODEL, TF), lambda j: (0, j)),
               pl.BlockSpec((D_MODEL, TF), lambda j: (0, j)),
               pl.BlockSpec((TF, D_MODEL), lambda j: (j, 0))]
    return pl.pallas_call(
        _ffn_narrow_kernel,
        grid=(D_FF // TF,),
        in_specs=[whole, pl.BlockSpec((1, D_MODEL), lambda j: (0, 0))] + w_specs,
        out_specs=[whole] + w_specs,
        out_shape=[jax.ShapeDtypeStruct((rows, D_MODEL), F32)]
                  + [jax.ShapeDtypeStruct(w.shape, BF16) for w in (wg, wu, wd)],
        scratch_shapes=[pltpu.VMEM((rows, D_MODEL), BF16), pltpu.VMEM((rows, D_MODEL), F32)],
        compiler_params=pltpu.CompilerParams(
            dimension_semantics=("arbitrary",), vmem_limit_bytes=VMEM_LIMIT),
        name=name,
    )(x, g, wg, wu, wd)


def _ffn_up_kernel(*refs, in_segs):
    n_in = len(in_segs)
    x_refs = refs[:n_in]
    g_ref, wg_ref, wu_ref, a_ref = refs[n_in:]
    i = pl.program_id(1)
    for x_ref, (off, n) in zip(x_refs, in_segs):
        @pl.when((i >= off) & (i < off + n))
        def _(x_ref=x_ref):
            h = _rms(x_ref[...], g_ref[...]).astype(BF16)
            gate = _dot(h, wg_ref[...])
            up = _dot(h, wu_ref[...])
            a_ref[...] = (gate * jax.nn.sigmoid(gate) * up).astype(BF16)


def _ffn_down_kernel(*refs, in_segs, out_segs, final_norm, pack_steps):
    n_in, n_out = len(in_segs), len(out_segs)
    a_ref = refs[0]
    x_refs = refs[1:1 + n_in]
    wd_ref = refs[1 + n_in]
    rest = refs[2 + n_in:]
    if final_norm:
        gf_ref, rest = rest[0], rest[1:]
    if pack_steps:
        wt_ref, rest = rest[0], rest[1:]
    o_refs = rest[:n_out]
    i = pl.program_id(0)
    if pack_steps:
        @pl.when(i < pack_steps)
        def _():
            _pack_w_in_kernel(wt_ref, rest[n_out])

    for a, b, lo, hi in _overlaps(in_segs, out_segs):
        @pl.when((i >= lo) & (i < hi))
        def _(x_ref=x_refs[a], o_ref=o_refs[b]):
            xo = x_ref[...] + 0.5 * _dot(a_ref[...], wd_ref[...])
            if final_norm:
                xo = _rms(xo, gf_ref[...])
            o_ref[...] = xo


def _ffn_two_stage(xs, g, wg, wu, wd, g_final, out_rows, *, name, pack_src=None):
    final_norm = g_final is not None
    tm, tn = TM_FFN2, D_FF // 2
    in_segs, n_tiles = _segments(xs, tm)
    out_shape = [jax.ShapeDtypeStruct((n, D_MODEL), F32) for n in out_rows]
    out_segs, n_out_tiles = _segments(out_shape, tm)
    assert n_out_tiles == n_tiles
    seg_ji = lambda s: pl.BlockSpec((tm, D_MODEL), lambda j, i, s=s: (jnp.clip(i - s[0], 0, s[1] - 1), 0))
    half = pl.BlockSpec((D_MODEL, tn), lambda j, i: (0, j), pipeline_mode=pl.Buffered(1))
    act = pl.pallas_call(
        functools.partial(_ffn_up_kernel, in_segs=in_segs),
        grid=(D_FF // tn, n_tiles),
        in_specs=[seg_ji(s) for s in in_segs] + [pl.BlockSpec((1, D_MODEL), lambda j, i: (0, 0)), half, half],
        out_specs=pl.BlockSpec((tm, tn), lambda j, i: (i, j)),
        out_shape=jax.ShapeDtypeStruct((n_tiles * tm, D_FF), BF16),
        compiler_params=pltpu.CompilerParams(
            dimension_semantics=("arbitrary", "arbitrary"), vmem_limit_bytes=VMEM_LIMIT),
        name=name + "_up",
    )(*xs, g, wg, wu)

    seg_i = lambda s: pl.BlockSpec((tm, D_MODEL), lambda i, s=s: (jnp.clip(i - s[0], 0, s[1] - 1), 0))
    in_specs = ([pl.BlockSpec((tm, D_FF), lambda i: (i, 0))] + [seg_i(s) for s in in_segs]
                + [pl.BlockSpec((D_FF, D_MODEL), lambda i: (0, 0), pipeline_mode=pl.Buffered(1))])
    args = [act] + list(xs) + [wd]
    if final_norm:
        in_specs.append(pl.BlockSpec((1, D_MODEL), lambda i: (0, 0)))
        args.append(g_final)
    out_specs = [seg_i(s) for s in out_segs]
    pack_steps = 0
    if pack_src is not None:
        lane_blk = 128
        pack_steps = D_MODEL // lane_blk
        assert pack_steps <= n_tiles
        blk = lambda i: (0, jnp.minimum(i, pack_steps - 1))
        in_specs.append(pl.BlockSpec((pack_src.shape[0], lane_blk), blk))
        args.append(pack_src)
        out_specs.append(pl.BlockSpec((PW, lane_blk), blk))
        out_shape = out_shape + [jax.ShapeDtypeStruct((PW, D_MODEL), BF16)]
    return pl.pallas_call(
        functools.partial(_ffn_down_kernel, in_segs=in_segs, out_segs=out_segs, final_norm=final_norm,
                          pack_steps=pack_steps),
        grid=(n_tiles,),
        in_specs=in_specs,
        out_specs=out_specs,
        out_shape=out_shape,
        compiler_params=pltpu.CompilerParams(
            dimension_semantics=("arbitrary",), vmem_limit_bytes=VMEM_LIMIT),
        name=name + "_down",
    )(*args)


_PACK_MOVES = (
    (0, 0, G),
    (G, RWKV_SPLITS[1], G),
    (2 * G, RWKV_SPLITS[2], G),
    (OFF_WD, RWKV_SPLITS[0], W_LORA),
    (OFF_AD, RWKV_SPLITS[3], A_LORA),
    (OFF_GD, RWKV_SPLITS[4], G_LORA),
    (RW, RWKV_PROJ, 3 * G),
)


def _pack_w_in_kernel(w_ref, o_ref):
    for dst, src, n in _PACK_MOVES:
        o_ref[dst:dst + n, :] = w_ref[src:src + n, :].astype(BF16)
    for lo, hi in ((OFF_WD + W_LORA, OFF_AD), (OFF_AD + A_LORA, OFF_GD)):
        o_ref[lo:hi, :] = jnp.zeros((hi - lo, o_ref.shape[1]), BF16)


def _proj_kernel(*refs, in_segs):
    n_in = len(in_segs)
    x_refs = refs[:n_in]
    g_ref, w_ref, o_ref = refs[n_in:]
    i = pl.program_id(1)
    for x_ref, (off, n) in zip(x_refs, in_segs):
        @pl.when((i >= off) & (i < off + n))
        def _(x_ref=x_ref):
            h = _rms(x_ref[...], g_ref[...]).astype(BF16)
            o_ref[...] = _dot_nt(h, w_ref[...])


def _proj(xs, g, w, *, name):
    in_segs, n_tiles = _segments(xs, TM)
    seg_ji = lambda s: pl.BlockSpec((TM, D_MODEL), lambda j, i, s=s: (jnp.clip(i - s[0], 0, s[1] - 1), 0),
                                    pipeline_mode=pl.Buffered(1 if s[1] == 1 else 2))
    return pl.pallas_call(
        functools.partial(_proj_kernel, in_segs=in_segs),
        grid=(PW // TN, n_tiles),
        in_specs=[seg_ji(s) for s in in_segs] + [
            pl.BlockSpec((1, D_MODEL), lambda j, i: (0, 0)),
            pl.BlockSpec((TN, D_MODEL), lambda j, i: (j, 0)),
        ],
        out_specs=pl.BlockSpec((TM, TN), lambda j, i: (i, j)),
        out_shape=jax.ShapeDtypeStruct((n_tiles * TM, PW), F32),
        compiler_params=pltpu.CompilerParams(
            dimension_semantics=("parallel", "parallel"), vmem_limit_bytes=VMEM_LIMIT),
        name=name,
    )(*xs, g, w)


def _prep_kernel(*refs, rows, chunk, sample):
    if sample:
        (p_ref, prev_ref, up1_ref, up2_ref, mu_ref, w0_ref, ww_ref, a0_ref, wa_ref, wgl_ref,
         kk_ref, ka_ref, rk_ref, cw_ref, bd_ref, tri_ref,
         ah_ref, rh_ref, bh_ref, kh_ref, v_ref, wc_ref, g_ref, bonus_ref, conv_ref, u_ref,
         plast_ref) = refs
    else:
        (p_ref, cp0_ref, cu0_ref, mu_ref, w0_ref, ww_ref, a0_ref, wa_ref, wgl_ref,
         kk_ref, ka_ref, rk_ref, cw_ref, bd_ref, tri_ref,
         ah_ref, rh_ref, bh_ref, kh_ref, v_ref, wc_ref, g_ref, bonus_ref, conv_ref, u_ref,
         ptail_ref, carry_p, carry_u) = refs

        @pl.when(pl.program_id(1) == 0)
        def _():
            carry_p[...] = cp0_ref[...]
            carry_u[...] = cu0_ref[...]

    def row_ids(width):
        r = lax.broadcasted_iota(jnp.int32, (rows, width), 0)
        return (r & 7) if sample else r

    def per_sequence(x):
        n, w = x.shape
        return jnp.broadcast_to(x[:, None, :], (n, 8, w)).reshape(n * 8, w)

    def shifted_mix(lo, hi):
        p = p_ref[:, lo:hi]
        rolled = pltpu.roll(p, 1, 0)
        if sample:
            sh = jnp.where(row_ids(hi - lo) == 0, per_sequence(prev_ref[:, lo:hi]), rolled)
        else:
            sh = jnp.where(row_ids(hi - lo) == 0, carry_p[7:8, lo:hi], rolled)
        return p + (sh - p) * mu_ref[:, lo:hi]

    tanh_wd = jnp.tanh(shifted_mix(OFF_WD, OFF_WD + LORA_PAD)).astype(BF16)
    ad = shifted_mix(OFF_AD, OFF_AD + LORA_PAD).astype(BF16)
    gd = shifted_mix(OFF_GD, OFF_GD + G_LORA)
    g_ref[...] = _dot(jax.nn.sigmoid(gd).astype(BF16), wgl_ref[...]).astype(BF16)
    bd = bd_ref[...]
    tri = tri_ref[...]

    for c0 in range(0, G, HEAD_BLOCK):
        cs = slice(c0, c0 + HEAD_BLOCK)
        r = shifted_mix(c0, c0 + HEAD_BLOCK)
        k = shifted_mix(G + c0, G + c0 + HEAD_BLOCK)
        v = shifted_mix(2 * G + c0, 2 * G + c0 + HEAD_BLOCK)

        z = w0_ref[:, cs] + _dot(tanh_wd, ww_ref[:, cs])
        lw = -EXP_M05_LOG2E * jax.nn.sigmoid(z)
        a = jax.nn.sigmoid(a0_ref[:, cs] + _dot(ad, wa_ref[:, cs]))

        kk = k * kk_ref[:, cs]
        norm = jnp.sqrt(_head_sum(kk * kk, bd))
        kk = kk / jnp.maximum(norm, 1e-12)
        km = k * (1.0 + (a - 1.0) * ka_ref[:, cs])
        bonus_ref[:, cs] = _head_sum(r * km * rk_ref[:, cs], bd) * v

        cum = _dot_split_rhs(tri, lw)
        e_cum = jnp.exp2(cum)
        e_inv = jnp.exp2(-cum)
        ah_ref[:, cs] = (-kk * jnp.exp2(cum - lw)).astype(BF16)
        rh_ref[:, cs] = (r * e_cum).astype(BF16)
        bh_ref[:, cs] = (kk * a * e_inv).astype(BF16)
        kh_ref[:, cs] = (km * e_inv).astype(BF16)
        v_ref[:, cs] = v.astype(BF16)
        for c in range(rows // chunk):
            wc_ref[c, :, cs] = e_cum[(c + 1) * chunk - 1:(c + 1) * chunk, :]

        bg = p_ref[:, RW + c0:RW + c0 + HEAD_BLOCK]
        u = (p_ref[:, RW + G + c0:RW + G + c0 + HEAD_BLOCK]
             * p_ref[:, RW + 2 * G + c0:RW + 2 * G + c0 + HEAD_BLOCK])
        u1 = pltpu.roll(u, 1, 0)
        u2 = pltpu.roll(u, 2, 0)
        rid = row_ids(HEAD_BLOCK)
        if sample:
            s1 = per_sequence(up1_ref[:, cs])
            um1 = jnp.where(rid == 0, s1, u1)
            um2 = jnp.where(rid == 0, per_sequence(up2_ref[:, cs]), jnp.where(rid == 1, s1, u2))
            u_ref[:, cs] = u
        else:
            um1 = jnp.where(rid == 0, carry_u[7:8, cs], u1)
            um2 = jnp.where(rid == 0, carry_u[6:7, cs], jnp.where(rid == 1, carry_u[7:8, cs], u2))
            u_ref[0, :, cs] = u[rows - 8:, :]
            carry_u[:, cs] = u[rows - 8:, :]
        conv = cw_ref[0:1, cs] * um2 + cw_ref[1:2, cs] * um1 + cw_ref[2:3, cs] * u
        conv_ref[:, cs] = (bg * conv).astype(BF16)

    if sample:
        plast_ref[...] = p_ref[:, 0:RW].reshape(rows // 8, 8, RW)[:, 7, :]
    else:
        ptail_ref[0] = p_ref[rows - 8:rows, 0:RW]
        carry_p[...] = p_ref[rows - 8:rows, 0:RW]


def _prep(proj, weights, tri, *, n_seq, seq_len, rows, chunk, row_block_offset, extra, sample, name):
    t = n_seq * seq_len
    n_chunks = t // chunk
    cpt = rows // chunk
    if sample:
        grid = (t // rows,)
        rmap = lambda i: (i + row_block_offset, 0)
        omap = lambda i: (i, 0)
        omap3 = lambda i: (i, 0, 0)
        cmap = lambda i: (0, 0)
        sem = ("parallel",)
    else:
        tiles = seq_len // rows
        grid = (n_seq, tiles)
        rmap = lambda b, j: (b * tiles + j + row_block_offset, 0)
        omap = lambda b, j: (b * tiles + j, 0)
        omap3 = lambda b, j: (b * tiles + j, 0, 0)
        cmap = lambda b, j: (0, 0)
        sem = ("parallel", "arbitrary")

    const = lambda arr: pl.BlockSpec(arr.shape, cmap)
    in_specs = [pl.BlockSpec((rows, PW), rmap)]
    if sample:
        in_specs += [pl.BlockSpec((rows // 8, RW), omap), pl.BlockSpec((rows // 8, G), omap),
                     pl.BlockSpec((rows // 8, G), omap)]
    else:
        in_specs += [const(e) for e in extra]
    in_specs += [const(w) for w in weights] + [const(tri)]
    args = [proj] + list(extra) + list(weights) + [tri]

    row_out = jax.ShapeDtypeStruct((t, G), F32)
    row_spec = pl.BlockSpec((rows, G), omap)
    row_bf16 = jax.ShapeDtypeStruct((t, G), BF16)
    out_shape = [row_bf16] * 5 + [jax.ShapeDtypeStruct((n_chunks, 1, G), F32), row_bf16, row_out,
                                  row_bf16]
    out_specs = [row_spec] * 5 + [pl.BlockSpec((cpt, 1, G), omap3), row_spec, row_spec, row_spec]
    if sample:
        out_shape += [row_out, jax.ShapeDtypeStruct((t // 8, RW), F32)]
        out_specs += [row_spec, pl.BlockSpec((rows // 8, RW), omap)]
        scratch = []
    else:
        out_shape += [jax.ShapeDtypeStruct((t // rows, 8, G), F32),
                      jax.ShapeDtypeStruct((t // rows, 8, RW), F32)]
        out_specs += [pl.BlockSpec((1, 8, G), omap3), pl.BlockSpec((1, 8, RW), omap3)]
        scratch = [pltpu.VMEM((8, RW), F32), pltpu.VMEM((8, G), F32)]
    return pl.pallas_call(
        functools.partial(_prep_kernel, rows=rows, chunk=chunk, sample=sample),
        grid=grid, in_specs=in_specs, out_specs=out_specs, out_shape=out_shape,
        scratch_shapes=scratch,
        compiler_params=pltpu.CompilerParams(dimension_semantics=sem, vmem_limit_bytes=VMEM_LIMIT),
        name=name,
    )(*args)


def _scan_kernel(*refs, n_par, rows, chunk, per_chunk_state, group, n_cast):
    ah_ref, rh_ref, bh_ref, kh_ref, v_ref, wc_ref, s0_ref = refs[:7]
    y_ref, s_ref = refs[7 + n_cast:9 + n_cast]
    for w_ref, o_ref in zip(refs[7:7 + n_cast], refs[9 + n_cast:]):
        o_ref[...] = w_ref[...].astype(BF16)

    if per_chunk_state:
        s_in = s0_ref
    else:
        s_in = s_ref

        @pl.when(pl.program_id(1) == 0)
        def _():
            for q in range(n_par):
                s_ref[q] = s0_ref[0]

    n_blk = rows // chunk
    lanes = [slice(h * HEAD_DIM, (h + 1) * HEAD_DIM) for h in range(N_HEADS)]
    bf = lambda x: x.astype(BF16)

    shift = chunk.bit_length() - 1
    rs = lax.broadcasted_iota(jnp.int32, (rows, rows), 0)
    cs = lax.broadcasted_iota(jnp.int32, (rows, rows), 1)
    same = (rs >> shift) == (cs >> shift)
    mask_strict = same & (rs > cs)
    ri = lax.broadcasted_iota(jnp.int32, (rows, 2 * rows), 0)
    ci = lax.broadcasted_iota(jnp.int32, (rows, 2 * rows), 1)
    cj = jnp.where(ci >= rows, ci - rows, ci)
    mask_r = ((ri >> shift) == (cj >> shift)) & (ri >= cj)
    eye = (rs == cs).astype(F32)

    def chunk_rows(x, c):
        return x[c * chunk:(c + 1) * chunk, :]

    pair_lane = lax.broadcasted_iota(jnp.int32, (rows, 2 * HEAD_DIM), 1) >> (HEAD_DIM.bit_length() - 1)

    def pair_masked(ref, q, h):
        hp, hh = divmod(h, 2)
        x2 = ref[q, :, 2 * HEAD_DIM * hp:2 * HEAD_DIM * (hp + 1)].astype(F32)
        return jnp.where(pair_lane == hh, x2, 0.0)

    def unit_group(units):
        per_unit = lambda f: {u: f(*u) for u in units}
        v = per_unit(lambda q, h: v_ref[q, :, lanes[h]])
        if n_blk == 1:
            a = per_unit(lambda q, h: ah_ref[q, :, lanes[h]])
            r = per_unit(lambda q, h: rh_ref[q, :, lanes[h]])
            b = per_unit(lambda q, h: bh_ref[q, :, lanes[h]])
            k = per_unit(lambda q, h: kh_ref[q, :, lanes[h]])
        else:
            a = per_unit(lambda q, h: pair_masked(ah_ref, q, h))
            r = per_unit(lambda q, h: pair_masked(rh_ref, q, h))
            b = per_unit(lambda q, h: pair_masked(bh_ref, q, h))
            k = per_unit(lambda q, h: pair_masked(kh_ref, q, h))
            v32 = {u: v[u].astype(F32) for u in units}
        ar = {u: bf(jnp.concatenate([a[u], r[u]], axis=0)) for u in units}
        bk = {u: bf(jnp.concatenate([b[u], k[u]], axis=0)) for u in units}

        gram = {u: _dot(ar[u], jnp.transpose(bk[u])) for u in units}
        l_ab = {u: jnp.where(mask_strict, gram[u][:rows, :rows], 0.0) for u in units}
        l_ak = {u: bf(jnp.where(mask_strict, gram[u][:rows, rows:], 0.0)) for u in units}
        l_r = {u: bf(jnp.where(mask_r, gram[u][rows:, :], 0.0)) for u in units}

        xs, ys = {}, {}
        for u in units:
            q, h = u
            if n_blk == 1:
                st = _dot_nt(ar[u], bf(s_in[q, h]))
                xs[u], ys[u] = st[:rows], st[rows:]
            else:
                parts = [_dot_nt(jnp.concatenate([chunk_rows(a[u], c), chunk_rows(r[u], c)], axis=0),
                                 s_in[c, h // 2]) for c in range(n_blk)]
                xs[u] = jnp.concatenate([p[:chunk] for p in parts], axis=0)
                ys[u] = jnp.concatenate([p[chunk:] for p in parts], axis=0)

        x = {u: xs[u] + _dot(l_ak[u], v[u]) for u in units}

        t_inv = {u: eye + l_ab[u] for u in units}
        n = 2
        if n < chunk:
            pw = {u: bf(l_ab[u]) for u in units}
            pw = {u: bf(_dot(pw[u], pw[u])) for u in units}
        while n < chunk:
            if 2 * n < chunk:
                both = {u: _dot(jnp.concatenate([bf(t_inv[u]), pw[u]], axis=0), pw[u]) for u in units}
                t_inv = {u: t_inv[u] + both[u][:rows] for u in units}
                pw = {u: bf(both[u][rows:]) for u in units}
            else:
                t_inv = {u: t_inv[u] + _dot(bf(t_inv[u]), pw[u]) for u in units}
            n *= 2

        uu = {u: _dot(bf(t_inv[u]), bf(x[u])) for u in units}
        uv = {u: jnp.concatenate([bf(uu[u]), v[u]], axis=0) for u in units}
        for u in units:
            q, h = u
            y_ref[q, :, lanes[h]] = ys[u] + _dot(l_r[u], uv[u])

        if n_blk == 1:
            for u in units:
                q, h = u
                upd = _dot_tn(uv[u], bk[u])
                s_ref[q, h] = (s_in[q, h] + upd) * wc_ref[q, 0][:, lanes[h]]
        else:
            def update(u, c):
                uv_c = jnp.concatenate([chunk_rows(uu[u], c), chunk_rows(v32[u], c)], axis=0)
                bk_c = jnp.concatenate([chunk_rows(b[u], c), chunk_rows(k[u], c)], axis=0)
                return _dot_tn(uv_c, bk_c)

            for q, h in units:
                if h % 2 == 0:
                    assert (q, h + 1) in units
                    hp = h // 2
                    for c in range(n_blk):
                        s_new = s_in[c, hp] + update((q, h), c) + update((q, h + 1), c)
                        s_ref[c, hp] = s_new * wc_ref[q, c][:, 2 * HEAD_DIM * hp:2 * HEAD_DIM * (hp + 1)]

    all_units = [(q, h) for h in range(N_HEADS) for q in range(n_par)]
    for g0 in range(0, len(all_units), group):
        unit_group(all_units[g0:g0 + group])


def _scan(ah, rh, bh, kh, v, wc, s0, *, n_seq, seq_len, n_par, rows, chunk, per_chunk_state, group,
          name, cast=()):
    t = n_seq * seq_len
    n_blk = rows // chunk
    if per_chunk_state:
        assert seq_len == chunk and n_par == 1
        lead = 1
        grid = (t // rows,)
        rmap = lambda i: (0, i, 0)
        cmap = lambda i: (0, i, 0, 0)
        smap = lambda i: (i, 0, 0, 0)
        s0map = smap
        state_block = (n_blk, N_HEADS // 2, HEAD_DIM, 2 * HEAD_DIM)
        s0_block = state_block
        wc_block = (1, n_blk, 1, G)
        sem = ("parallel",)
    else:
        assert rows == chunk and s0.shape[0] == 1 and n_seq % n_par == 0
        lead = n_seq
        grid = (n_seq // n_par, seq_len // chunk)
        rmap = lambda s, c: (s, c, 0)
        cmap = lambda s, c: (s, c, 0, 0)
        smap = lambda s, c: (s, 0, 0, 0)
        s0map = lambda s, c: (0, 0, 0, 0)
        state_block = (n_par, N_HEADS, HEAD_DIM, HEAD_DIM)
        s0_block = (1, N_HEADS, HEAD_DIM, HEAD_DIM)
        wc_block = (n_par, 1, 1, G)
        sem = ("parallel", "arbitrary")
    rows3 = lambda x: x.reshape(lead, t // lead, G)
    row_spec = pl.BlockSpec((n_par, rows, G), rmap)
    n_steps = 1
    for extent in grid:
        n_steps *= extent
    step = (lambda i: i) if len(grid) == 1 else (lambda s, c: s * grid[1] + c)
    cast_specs = []
    for w in cast:
        assert w.shape[0] % (16 * n_steps) == 0
        cast_specs.append(pl.BlockSpec((w.shape[0] // n_steps, w.shape[1]), lambda *g: (step(*g), 0)))
    y, s_out, *narrowed = pl.pallas_call(
        functools.partial(_scan_kernel, n_par=n_par, rows=rows, chunk=chunk,
                          per_chunk_state=per_chunk_state, group=group, n_cast=len(cast)),
        grid=grid,
        in_specs=([row_spec] * 5 + [pl.BlockSpec(wc_block, cmap), pl.BlockSpec(s0_block, s0map)]
                  + cast_specs),
        out_specs=[row_spec, pl.BlockSpec(state_block, smap)] + cast_specs,
        out_shape=[jax.ShapeDtypeStruct((lead, t // lead, G), F32),
                   jax.ShapeDtypeStruct((n_seq,) + state_block[1:], F32)]
                  + [jax.ShapeDtypeStruct(w.shape, BF16) for w in cast],
        compiler_params=pltpu.CompilerParams(dimension_semantics=sem, vmem_limit_bytes=VMEM_LIMIT),
        name=name,
    )(rows3(ah), rows3(rh), rows3(bh), rows3(kh), rows3(v), wc.reshape(lead, -1, 1, G), s0, *cast)
    return (y.reshape(t, G), s_out, *narrowed)


def _mix_kernel(*refs, segs):
    n = len(segs)
    row_refs = [refs[4 * s:4 * s + 4] for s in range(n)]
    x_ref, lnw_ref, lnb_ref, bd_ref, wo_ref, o_ref = refs[4 * n:]
    i = pl.program_id(0)

    def body(y_ref, bonus_ref, g_ref, conv_ref):
        bd = bd_ref[...]
        y = y_ref[...]
        mu = _head_sum(y, bd) * (1.0 / HEAD_DIM)
        d = y - mu
        var = _head_sum(d * d, bd) * (1.0 / HEAD_DIM)
        yn = d * lax.rsqrt(var + GN_EPS) * lnw_ref[...] + lnb_ref[...]
        rw = ((yn + bonus_ref[...]) * g_ref[...]).astype(BF16)
        mix = _dot(rw, wo_ref[0:G, :]) + _dot(conv_ref[...], wo_ref[G:2 * G, :])
        o_ref[...] = x_ref[...] + mix

    for rr, (off, cnt) in zip(row_refs, segs):
        @pl.when((i >= off) & (i < off + cnt))
        def _(rr=rr):
            body(*rr)


def _mix(row_groups, x1, lnw, lnb, bd, wo):
    segs, n_tiles = _segments([grp[0] for grp in row_groups], TM_MIX)
    cmap = lambda i: (0, 0)
    in_specs, args = [], []
    for grp, (off, cnt) in zip(row_groups, segs):
        smap = lambda i, off=off, cnt=cnt: (jnp.clip(i - off, 0, cnt - 1), 0)
        in_specs += [pl.BlockSpec((TM_MIX, G), smap)] * 4
        args += list(grp)
    in_specs += [pl.BlockSpec((TM_MIX, D_MODEL), lambda i: (i, 0)),
                 pl.BlockSpec((1, G), cmap), pl.BlockSpec((1, G), cmap),
                 pl.BlockSpec(bd.shape, cmap), pl.BlockSpec((D_MODEL, D_MODEL), cmap)]
    return pl.pallas_call(
        functools.partial(_mix_kernel, segs=segs),
        grid=(n_tiles,), in_specs=in_specs,
        out_specs=pl.BlockSpec((TM_MIX, D_MODEL), lambda i: (i, 0)),
        out_shape=jax.ShapeDtypeStruct((n_tiles * TM_MIX, D_MODEL), F32),
        compiler_params=pltpu.CompilerParams(
            dimension_semantics=("parallel",), vmem_limit_bytes=VMEM_LIMIT),
        name="mix",
    )(*args, x1, lnw, lnb, bd, wo)


def _pack_rwkv(a):
    r, wd, k, v, ad, gd = jnp.split(a, RWKV_SPLITS, axis=-1)
    zw = jnp.zeros(a.shape[:-1] + (LORA_PAD - W_LORA,), a.dtype)
    za = jnp.zeros(a.shape[:-1] + (LORA_PAD - A_LORA,), a.dtype)
    return jnp.concatenate([r, k, v, wd, zw, ad, za, gd], axis=-1)


def _unpack_rwkv(p):
    return jnp.concatenate([
        p[..., 0:G], p[..., OFF_WD:OFF_WD + W_LORA], p[..., G:2 * G], p[..., 2 * G:3 * G],
        p[..., OFF_AD:OFF_AD + A_LORA], p[..., OFF_GD:OFF_GD + G_LORA]], axis=-1)


def _block_tri(n, chunk):
    i = jnp.arange(n)
    return ((i[:, None] // chunk == i[None, :] // chunk) & (i[:, None] >= i[None, :])).astype(BF16)


def kernel(x_prompt, x_sample, state_wkv, state_shift, state_conv, meta_tokens, g_ffn1, ffn1_gate, ffn1_up, ffn1_down, g_mix, w_in, mu_shift, w0, w_lora_w, a0, w_lora_a, w_lora_g, k_k, k_a, r_k, ln_x_w, ln_x_b, conv_w, w_out, g_ffn2, ffn2_gate, ffn2_up, ffn2_down, g_final):
    assert g_ffn1.shape[0] == 1, "single layer"
    nb, seq, _ = x_prompt.shape
    db, dseq, _ = x_sample.shape
    assert dseq == C_SAMPLE and N_META <= C_PROMPT and seq % ROWS_PREP == 0
    tp, ts = nb * seq, db * dseq
    assert tp % TM == 0 and ts % TM == 0

    row = lambda a: a.reshape(1, -1).astype(F32)
    pad_rows = lambda w, n: jnp.concatenate([w, jnp.zeros((n - w.shape[0], w.shape[1]), w.dtype)], axis=0)
    hid = jnp.arange(HEAD_BLOCK) // HEAD_DIM
    bd = (hid[:, None] == hid[None, :]).astype(BF16)
    prep_w = (_pack_rwkv(mu_shift[0])[None], row(w0[0]), pad_rows(w_lora_w[0], LORA_PAD).astype(BF16),
              row(a0[0]), pad_rows(w_lora_a[0], LORA_PAD).astype(BF16), w_lora_g[0].astype(BF16),
              row(k_k[0]), row(k_a[0]), row(r_k[0]), conv_w[0].astype(F32), bd)

    x_meta = jnp.concatenate([jnp.zeros((C_PROMPT - N_META, D_MODEL), F32), meta_tokens.astype(F32)], axis=0)
    x1_meta, wg1, wu1, wd1 = _ffn_narrow(x_meta, row(g_ffn1[0]), ffn1_gate[0], ffn1_up[0], ffn1_down[0],
                                         name="ffn1_meta")
    x1, w_in_p = _ffn_two_stage([x_prompt.reshape(tp, D_MODEL), x_sample.reshape(ts, D_MODEL)],
                                row(g_ffn1[0]), wg1, wu1, wd1, None, [tp + ts], name="ffn1",
                                pack_src=jnp.transpose(w_in[0]))
    x1_meta = jnp.concatenate([x1_meta, jnp.zeros((TM - C_PROMPT, D_MODEL), F32)], axis=0)
    proj = _proj([x1, x1_meta], row(g_mix[0]), w_in_p, name="proj")
    meta_row0 = tp + ts

    tri_p = _block_tri(ROWS_PREP, C_PROMPT)
    zeros_state = jnp.zeros((1, N_HEADS, HEAD_DIM, HEAD_DIM), F32)
    (ah, rh, bh, kh, vv, wc, _, _, _, utail_m, ptail_m) = _prep(
        proj, prep_w, tri_p[:C_PROMPT, :C_PROMPT], n_seq=1, seq_len=C_PROMPT, rows=C_PROMPT,
        chunk=C_PROMPT, row_block_offset=meta_row0 // C_PROMPT,
        extra=(jnp.zeros((8, RW), F32), jnp.zeros((8, G), F32)),
        sample=False, name="prep_meta")
    _, wkv_m = _scan(ah, rh, bh, kh, vv, wc, zeros_state, n_seq=1, seq_len=C_PROMPT, n_par=1,
                     rows=C_PROMPT, chunk=C_PROMPT, per_chunk_state=False, group=N_HEADS,
                     name="scan_meta")

    (ah, rh, bh, kh, vv, wc, g_p, bonus_p, conv_p, utail_p, ptail_p) = _prep(
        proj, prep_w, tri_p, n_seq=nb, seq_len=seq, rows=ROWS_PREP, chunk=C_PROMPT,
        row_block_offset=0, extra=(ptail_m[0], utail_m[0]), sample=False, name="prep_prompt")
    y_p, wkv_p, wg2, wu2, wd2, wo = _scan(
        ah, rh, bh, kh, vv, wc, wkv_m, n_seq=nb, seq_len=seq, n_par=SCAN_PAR, rows=C_PROMPT,
        chunk=C_PROMPT, per_chunk_state=False, group=SCAN_PAR * N_HEADS // 2, name="scan_prompt",
        cast=(ffn2_gate[0], ffn2_up[0], ffn2_down[0], w_out[0]))

    prev = _pack_rwkv(state_shift[0])
    up1 = state_conv[0][:, 1]
    up2 = state_conv[0][:, 0]
    (ah, rh, bh, kh, vv, wc, g_s, bonus_s, conv_s, u_s, plast_s) = _prep(
        proj, prep_w, _block_tri(ROWS_PREP, C_SAMPLE), n_seq=db, seq_len=dseq, rows=ROWS_PREP,
        chunk=C_SAMPLE, row_block_offset=tp // ROWS_PREP, extra=(prev, up1, up2), sample=True,
        name="prep_sample")
    pair = (db, N_HEADS // 2, 2, HEAD_DIM, HEAD_DIM)
    s0_pairs = jnp.transpose(state_wkv[0].reshape(pair), (0, 1, 3, 2, 4)).reshape(
        db, N_HEADS // 2, HEAD_DIM, 2 * HEAD_DIM)
    y_s, wkv_s = _scan(ah, rh, bh, kh, vv, wc, s0_pairs, n_seq=db, seq_len=dseq, n_par=1,
                       rows=ROWS_SCAN_SAMPLE, chunk=C_SAMPLE, per_chunk_state=True, group=16,
                       name="scan_sample")

    x2 = _mix([(y_p, bonus_p, g_p, conv_p), (y_s, bonus_s, g_s, conv_s)], x1,
              row(ln_x_w[0]), row(ln_x_b[0]), bd, wo)
    y_prompt, y_sample = _ffn_two_stage([x2], row(g_ffn2[0]), wg2, wu2, wd2, row(g_final), [tp, ts],
                                        name="ffn2")

    shift_p = _unpack_rwkv(ptail_p.reshape(nb, seq // ROWS_PREP, 8, RW)[:, -1, 7, :])
    conv_state_p = utail_p.reshape(nb, seq // ROWS_PREP, 8, G)[:, -1, 6:, :]
    shift_s = _unpack_rwkv(plast_s)
    conv_state_s = u_s.reshape(db, dseq, G)[:, -2:, :]
    wkv_s_out = jnp.transpose(wkv_s.reshape(db, N_HEADS // 2, HEAD_DIM, 2, HEAD_DIM),
                              (0, 1, 3, 2, 4)).reshape(db, N_HEADS, HEAD_DIM, HEAD_DIM)
    return (y_prompt.reshape(nb, seq, D_MODEL), y_sample.reshape(db, dseq, D_MODEL),
            wkv_p[None].astype(state_wkv.dtype), shift_p[None].astype(state_shift.dtype),
            conv_state_p[None].astype(state_conv.dtype),
            wkv_s_out[None].astype(state_wkv.dtype), shift_s[None].astype(state_shift.dtype),
            conv_state_s[None].astype(state_conv.dtype))
```

```python
import functools

import jax
import jax.numpy as jnp
from jax import lax
from jax.experimental import pallas as pl
from jax.experimental.pallas import tpu as pltpu

F32 = jnp.float32
BF16 = jnp.bfloat16

D_MODEL = 2048
D_FF = 5632
N_META = 16
G = 1024
HEAD_DIM = 64
N_HEADS = G // HEAD_DIM
HEAD_BLOCK = 256
W_LORA = 96
A_LORA = 96
G_LORA = 256
LORA_PAD = 128
RWKV_PROJ = 3 * G + W_LORA + A_LORA + G_LORA
RWKV_SPLITS = (G, G + W_LORA, 2 * G + W_LORA, 3 * G + W_LORA, 3 * G + W_LORA + A_LORA)
RW = 3 * G + 2 * LORA_PAD + G_LORA
PW = RW + 3 * G
OFF_WD, OFF_AD, OFF_GD = 3 * G, 3 * G + LORA_PAD, 3 * G + 2 * LORA_PAD
RMS_EPS = 1e-6
GN_EPS = 64e-5
EXP_M05_LOG2E = 0.6065306597126334 * 1.4426950408889634

C_PROMPT = 64
C_SAMPLE = 8
TM = 512
TF = 512
TN = 3328
TM_FFN_UP = 512
TM_FFN_DOWN = 256
MXU_N = 256
TM_MIX = 256
ROWS_PREP = 256
ROWS_SCAN_SAMPLE = 128
SCAN_PAR = 4
VMEM_LIMIT = 56 * 1024 * 1024


def _dot(a, b):
    return jnp.dot(a, b, preferred_element_type=F32)


def _dot_nt(a, b):
    return lax.dot_general(a, b, (((1,), (1,)), ((), ())), preferred_element_type=F32)


def _dot_tn(a, b):
    return lax.dot_general(a, b, (((0,), (0,)), ((), ())), preferred_element_type=F32)


def _split(a):
    hi = a.astype(BF16)
    lo = (a - hi.astype(F32)).astype(BF16)
    return hi, lo


def _head_sum(x, bd):
    xb = x.astype(BF16)
    wb = bd.shape[0]
    return jnp.concatenate([_dot(xb[:, c:c + wb], bd) for c in range(0, x.shape[1], wb)], axis=1)


def _dot_split_rhs(a_bf16, b):
    hi, lo = _split(b)
    return _dot(a_bf16, hi) + _dot(a_bf16, lo)


def _rms(x, g):
    return x * lax.rsqrt(jnp.mean(x * x, axis=-1, keepdims=True) + RMS_EPS) * g


def _segments(arrays, tile):
    segs, off = [], 0
    for arr in arrays:
        n = arr.shape[0] // tile
        assert n * tile == arr.shape[0]
        segs.append((off, n))
        off += n
    return segs, off


def _overlaps(in_segs, out_segs):
    for a, (ao, an) in enumerate(in_segs):
        for b, (bo, bn) in enumerate(out_segs):
            lo, hi = max(ao, bo), min(ao + an, bo + bn)
            if lo < hi:
                yield a, b, lo, hi


def _ffn_narrow_kernel(x_ref, g_ref, wg_ref, wu_ref, wd_ref, o_ref, wg_o, wu_o, wd_o, h_ref, acc_ref):
    j = pl.program_id(0)

    @pl.when(j == 0)
    def _():
        h_ref[...] = _rms(x_ref[...], g_ref[...]).astype(BF16)
        acc_ref[...] = jnp.zeros_like(acc_ref)

    wg, wu, wd = wg_ref[...].astype(BF16), wu_ref[...].astype(BF16), wd_ref[...].astype(BF16)
    wg_o[...], wu_o[...], wd_o[...] = wg, wu, wd
    h = h_ref[...]
    gate = _dot(h, wg)
    up = _dot(h, wu)
    act = (gate * jax.nn.sigmoid(gate) * up).astype(BF16)
    acc_ref[...] += _dot(act, wd)

    @pl.when(j == pl.num_programs(0) - 1)
    def _():
        o_ref[...] = x_ref[...] + 0.5 * acc_ref[...]


def _ffn_narrow(x, g, wg, wu, wd, *, name):
    rows = x.shape[0]
    whole = pl.BlockSpec((rows, D_MODEL), lambda j: (0, 0))
    w_specs = [pl.BlockSpec((D_MODEL, TF), lambda j: (0, j)),
               pl.BlockSpec((D_MODEL, TF), lambda j: (0, j)),
               pl.BlockSpec((TF, D_MODEL), lambda j: (j, 0))]
    return pl.pallas_call(
        _ffn_narrow_kernel,
        grid=(D_FF // TF,),
        in_specs=[whole, pl.BlockSpec((1, D_MODEL), lambda j: (0, 0))] + w_specs,
        out_specs=[whole] + w_specs,
        out_shape=[jax.ShapeDtypeStruct((rows, D_MODEL), F32)]
                  + [jax.ShapeDtypeStruct(w.shape, BF16) for w in (wg, wu, wd)],
        scratch_shapes=[pltpu.VMEM((rows, D_MODEL), BF16), pltpu.VMEM((rows, D_MODEL), F32)],
        compiler_params=pltpu.CompilerParams(
            dimension_semantics=("arbitrary",), vmem_limit_bytes=VMEM_LIMIT),
        name=name,
    )(x, g, wg, wu, wd)


def _ffn_up_kernel(*refs, in_segs):
    n_in = len(in_segs)
    x_refs = refs[:n_in]
    g_ref, wg_ref, wu_ref, a_ref = refs[n_in:]
    i = pl.program_id(1)
    for x_ref, (off, n) in zip(x_refs, in_segs):
        @pl.when((i >= off) & (i < off + n))
        def _(x_ref=x_ref):
            h = _rms(x_ref[...], g_ref[...]).astype(BF16)
            for c in range(0, a_ref.shape[1], MXU_N):
                gate = _dot(h, wg_ref[:, c:c + MXU_N])
                up = _dot(h, wu_ref[:, c:c + MXU_N])
                a_ref[:, c:c + MXU_N] = (gate * jax.nn.sigmoid(gate) * up).astype(BF16)


def _ffn_down_kernel(*refs, in_segs, out_segs, final_norm, pack_steps):
    n_in, n_out = len(in_segs), len(out_segs)
    a_ref = refs[0]
    x_refs = refs[1:1 + n_in]
    wd_ref = refs[1 + n_in]
    rest = refs[2 + n_in:]
    if final_norm:
        gf_ref, rest = rest[0], rest[1:]
    if pack_steps:
        wt_ref, rest = rest[0], rest[1:]
    o_refs = rest[:n_out]
    i = pl.program_id(0)
    if pack_steps:
        @pl.when(i < pack_steps)
        def _():
            _pack_w_in_kernel(wt_ref, rest[n_out])

    for a, b, lo, hi in _overlaps(in_segs, out_segs):
        @pl.when((i >= lo) & (i < hi))
        def _(x_ref=x_refs[a], o_ref=o_refs[b]):
            xo = x_ref[...] + 0.5 * _dot(a_ref[...], wd_ref[...])
            if final_norm:
                xo = _rms(xo, gf_ref[...])
            o_ref[...] = xo


def _ffn_two_stage(xs, g, wg, wu, wd, g_final, out_rows, *, name, pack_src=None):
    final_norm = g_final is not None
    tm_up, tn = TM_FFN_UP, D_FF // 2
    up_segs, up_tiles = _segments(xs, tm_up)
    seg_ji = lambda s: pl.BlockSpec((tm_up, D_MODEL), lambda j, i, s=s: (jnp.clip(i - s[0], 0, s[1] - 1), 0))
    half = pl.BlockSpec((D_MODEL, tn), lambda j, i: (0, j), pipeline_mode=pl.Buffered(1))
    act = pl.pallas_call(
        functools.partial(_ffn_up_kernel, in_segs=up_segs),
        grid=(D_FF // tn, up_tiles),
        in_specs=[seg_ji(s) for s in up_segs] + [pl.BlockSpec((1, D_MODEL), lambda j, i: (0, 0)), half, half],
        out_specs=pl.BlockSpec((tm_up, tn), lambda j, i: (i, j)),
        out_shape=jax.ShapeDtypeStruct((up_tiles * tm_up, D_FF), BF16),
        compiler_params=pltpu.CompilerParams(
            dimension_semantics=("arbitrary", "arbitrary"), vmem_limit_bytes=VMEM_LIMIT),
        name=name + "_up",
    )(*xs, g, wg, wu)

    tm = TM_FFN_DOWN
    in_segs, n_tiles = _segments(xs, tm)
    out_shape = [jax.ShapeDtypeStruct((n, D_MODEL), F32) for n in out_rows]
    out_segs, n_out_tiles = _segments(out_shape, tm)
    assert n_out_tiles == n_tiles and n_tiles * tm == up_tiles * tm_up
    seg_i = lambda s: pl.BlockSpec((tm, D_MODEL), lambda i, s=s: (jnp.clip(i - s[0], 0, s[1] - 1), 0))
    in_specs = ([pl.BlockSpec((tm, D_FF), lambda i: (i, 0))] + [seg_i(s) for s in in_segs]
                + [pl.BlockSpec((D_FF, D_MODEL), lambda i: (0, 0), pipeline_mode=pl.Buffered(1))])
    args = [act] + list(xs) + [wd]
    if final_norm:
        in_specs.append(pl.BlockSpec((1, D_MODEL), lambda i: (0, 0)))
        args.append(g_final)
    out_specs = [seg_i(s) for s in out_segs]
    pack_steps = 0
    if pack_src is not None:
        lane_blk = 128
        pack_steps = D_MODEL // lane_blk
        assert pack_steps <= n_tiles
        blk = lambda i: (0, jnp.minimum(i, pack_steps - 1))
        in_specs.append(pl.BlockSpec((pack_src.shape[0], lane_blk), blk))
        args.append(pack_src)
        out_specs.append(pl.BlockSpec((PW, lane_blk), blk))
        out_shape = out_shape + [jax.ShapeDtypeStruct((PW, D_MODEL), BF16)]
    return pl.pallas_call(
        functools.partial(_ffn_down_kernel, in_segs=in_segs, out_segs=out_segs, final_norm=final_norm,
                          pack_steps=pack_steps),
        grid=(n_tiles,),
        in_specs=in_specs,
        out_specs=out_specs,
        out_shape=out_shape,
        compiler_params=pltpu.CompilerParams(
            dimension_semantics=("arbitrary",), vmem_limit_bytes=VMEM_LIMIT),
        name=name + "_down",
    )(*args)


_PACK_MOVES = (
    (0, 0, G),
    (G, RWKV_SPLITS[1], G),
    (2 * G, RWKV_SPLITS[2], G),
    (OFF_WD, RWKV_SPLITS[0], W_LORA),
    (OFF_AD, RWKV_SPLITS[3], A_LORA),
    (OFF_GD, RWKV_SPLITS[4], G_LORA),
    (RW, RWKV_PROJ, 3 * G),
)


def _pack_w_in_kernel(w_ref, o_ref):
    for dst, src, n in _PACK_MOVES:
        o_ref[dst:dst + n, :] = w_ref[src:src + n, :].astype(BF16)
    for lo, hi in ((OFF_WD + W_LORA, OFF_AD), (OFF_AD + A_LORA, OFF_GD)):
        o_ref[lo:hi, :] = jnp.zeros((hi - lo, o_ref.shape[1]), BF16)


def _proj_kernel(*refs, in_segs):
    n_in = len(in_segs)
    x_refs = refs[:n_in]
    g_ref, w_ref, o_ref = refs[n_in:]
    i = pl.program_id(1)
    for x_ref, (off, n) in zip(x_refs, in_segs):
        @pl.when((i >= off) & (i < off + n))
        def _(x_ref=x_ref):
            h = _rms(x_ref[...], g_ref[...]).astype(BF16)
            o_ref[...] = _dot_nt(h, w_ref[...])


def _proj(xs, g, w, *, name):
    in_segs, n_tiles = _segments(xs, TM)
    seg_ji = lambda s: pl.BlockSpec((TM, D_MODEL), lambda j, i, s=s: (jnp.clip(i - s[0], 0, s[1] - 1), 0),
                                    pipeline_mode=pl.Buffered(1 if s[1] == 1 else 2))
    return pl.pallas_call(
        functools.partial(_proj_kernel, in_segs=in_segs),
        grid=(PW // TN, n_tiles),
        in_specs=[seg_ji(s) for s in in_segs] + [
            pl.BlockSpec((1, D_MODEL), lambda j, i: (0, 0)),
            pl.BlockSpec((TN, D_MODEL), lambda j, i: (j, 0)),
        ],
        out_specs=pl.BlockSpec((TM, TN), lambda j, i: (i, j)),
        out_shape=jax.ShapeDtypeStruct((n_tiles * TM, PW), F32),
        compiler_params=pltpu.CompilerParams(
            dimension_semantics=("parallel", "parallel"), vmem_limit_bytes=VMEM_LIMIT),
        name=name,
    )(*xs, g, w)


def _prep_kernel(*refs, rows, chunk, sample):
    if sample:
        (p_ref, prev_ref, up1_ref, up2_ref, mu_ref, w0_ref, ww_ref, a0_ref, wa_ref, wgl_ref,
         kk_ref, ka_ref, rk_ref, cw_ref, bd_ref, tri_ref,
         ah_ref, rh_ref, bh_ref, kh_ref, v_ref, wc_ref, g_ref, bonus_ref, conv_ref, u_ref,
         plast_ref) = refs
    else:
        (p_ref, cp0_ref, cu0_ref, mu_ref, w0_ref, ww_ref, a0_ref, wa_ref, wgl_ref,
         kk_ref, ka_ref, rk_ref, cw_ref, bd_ref, tri_ref,
         ah_ref, rh_ref, bh_ref, kh_ref, v_ref, wc_ref, g_ref, bonus_ref, conv_ref, u_ref,
         ptail_ref, carry_p, carry_u) = refs

        @pl.when(pl.program_id(1) == 0)
        def _():
            carry_p[...] = cp0_ref[...]
            carry_u[...] = cu0_ref[...]

    def row_ids(width):
        r = lax.broadcasted_iota(jnp.int32, (rows, width), 0)
        return (r & 7) if sample else r

    def per_sequence(x):
        n, w = x.shape
        return jnp.broadcast_to(x[:, None, :], (n, 8, w)).reshape(n * 8, w)

    def shifted_mix(lo, hi):
        p = p_ref[:, lo:hi]
        rolled = pltpu.roll(p, 1, 0)
        if sample:
            sh = jnp.where(row_ids(hi - lo) == 0, per_sequence(prev_ref[:, lo:hi]), rolled)
        else:
            sh = jnp.where(row_ids(hi - lo) == 0, carry_p[7:8, lo:hi], rolled)
        return p + (sh - p) * mu_ref[:, lo:hi]

    r = shifted_mix(0, G)
    k = shifted_mix(G, 2 * G)
    v = shifted_mix(2 * G, 3 * G)
    wd = shifted_mix(OFF_WD, OFF_WD + LORA_PAD)
    ad = shifted_mix(OFF_AD, OFF_AD + LORA_PAD)
    gd = shifted_mix(OFF_GD, OFF_GD + G_LORA)

    z = w0_ref[...] + _dot(jnp.tanh(wd).astype(BF16), ww_ref[...])
    lw = -EXP_M05_LOG2E * jax.nn.sigmoid(z)
    a = jax.nn.sigmoid(a0_ref[...] + _dot(ad.astype(BF16), wa_ref[...]))
    g_ref[...] = _dot(jax.nn.sigmoid(gd).astype(BF16), wgl_ref[...]).astype(BF16)

    bd = bd_ref[...]
    kk = k * kk_ref[...]
    norm = jnp.sqrt(_head_sum(kk * kk, bd))
    kk = kk / jnp.maximum(norm, 1e-12)
    km = k * (1.0 + (a - 1.0) * ka_ref[...])
    bonus_ref[...] = _head_sum(r * km * rk_ref[...], bd) * v

    cum = _dot_split_rhs(tri_ref[...], lw)
    e_cum = jnp.exp2(cum)
    e_inv = jnp.exp2(-cum)
    ah_ref[...] = (-kk * jnp.exp2(cum - lw)).astype(BF16)
    rh_ref[...] = (r * e_cum).astype(BF16)
    bh_ref[...] = (kk * a * e_inv).astype(BF16)
    kh_ref[...] = (km * e_inv).astype(BF16)
    v_ref[...] = v.astype(BF16)
    for c in range(rows // chunk):
        wc_ref[c] = e_cum[(c + 1) * chunk - 1:(c + 1) * chunk, :]

    bg = p_ref[:, RW:RW + G]
    u = p_ref[:, RW + G:RW + 2 * G] * p_ref[:, RW + 2 * G:RW + 3 * G]
    u1 = pltpu.roll(u, 1, 0)
    u2 = pltpu.roll(u, 2, 0)
    rid = row_ids(G)
    if sample:
        s1 = per_sequence(up1_ref[...])
        um1 = jnp.where(rid == 0, s1, u1)
        um2 = jnp.where(rid == 0, per_sequence(up2_ref[...]), jnp.where(rid == 1, s1, u2))
        u_ref[...] = u
        plast_ref[...] = p_ref[:, 0:RW].reshape(rows // 8, 8, RW)[:, 7, :]
    else:
        um1 = jnp.where(rid == 0, carry_u[7:8, :], u1)
        um2 = jnp.where(rid == 0, carry_u[6:7, :], jnp.where(rid == 1, carry_u[7:8, :], u2))
        u_ref[0] = u[rows - 8:, :]
        ptail_ref[0] = p_ref[rows - 8:rows, 0:RW]
    conv = cw_ref[0:1, :] * um2 + cw_ref[1:2, :] * um1 + cw_ref[2:3, :] * u
    conv_ref[...] = (bg * conv).astype(BF16)

    if not sample:
        carry_p[...] = p_ref[rows - 8:rows, 0:RW]
        carry_u[...] = u[rows - 8:, :]


def _prep(proj, weights, tri, *, n_seq, seq_len, rows, chunk, row_block_offset, extra, sample, name):
    t = n_seq * seq_len
    n_chunks = t // chunk
    cpt = rows // chunk
    if sample:
        grid = (t // rows,)
        rmap = lambda i: (i + row_block_offset, 0)
        omap = lambda i: (i, 0)
        omap3 = lambda i: (i, 0, 0)
        cmap = lambda i: (0, 0)
        sem = ("parallel",)
    else:
        tiles = seq_len // rows
        grid = (n_seq, tiles)
        rmap = lambda b, j: (b * tiles + j + row_block_offset, 0)
        omap = lambda b, j: (b * tiles + j, 0)
        omap3 = lambda b, j: (b * tiles + j, 0, 0)
        cmap = lambda b, j: (0, 0)
        sem = ("parallel", "arbitrary")

    const = lambda arr: pl.BlockSpec(arr.shape, cmap)
    in_specs = [pl.BlockSpec((rows, PW), rmap)]
    if sample:
        in_specs += [pl.BlockSpec((rows // 8, RW), omap), pl.BlockSpec((rows // 8, G), omap),
                     pl.BlockSpec((rows // 8, G), omap)]
    else:
        in_specs += [const(e) for e in extra]
    in_specs += [const(w) for w in weights] + [const(tri)]
    args = [proj] + list(extra) + list(weights) + [tri]

    row_out = jax.ShapeDtypeStruct((t, G), F32)
    row_spec = pl.BlockSpec((rows, G), omap)
    row_bf16 = jax.ShapeDtypeStruct((t, G), BF16)
    out_shape = [row_bf16] * 5 + [jax.ShapeDtypeStruct((n_chunks, 1, G), F32), row_bf16, row_out,
                                  row_bf16]
    out_specs = [row_spec] * 5 + [pl.BlockSpec((cpt, 1, G), omap3), row_spec, row_spec, row_spec]
    if sample:
        out_shape += [row_out, jax.ShapeDtypeStruct((t // 8, RW), F32)]
        out_specs += [row_spec, pl.BlockSpec((rows // 8, RW), omap)]
        scratch = []
    else:
        out_shape += [jax.ShapeDtypeStruct((t // rows, 8, G), F32),
                      jax.ShapeDtypeStruct((t // rows, 8, RW), F32)]
        out_specs += [pl.BlockSpec((1, 8, G), omap3), pl.BlockSpec((1, 8, RW), omap3)]
        scratch = [pltpu.VMEM((8, RW), F32), pltpu.VMEM((8, G), F32)]
    return pl.pallas_call(
        functools.partial(_prep_kernel, rows=rows, chunk=chunk, sample=sample),
        grid=grid, in_specs=in_specs, out_specs=out_specs, out_shape=out_shape,
        scratch_shapes=scratch,
        compiler_params=pltpu.CompilerParams(dimension_semantics=sem, vmem_limit_bytes=VMEM_LIMIT),
        name=name,
    )(*args)


def _scan_kernel(*refs, n_par, rows, chunk, per_chunk_state, group, n_cast):
    ah_ref, rh_ref, bh_ref, kh_ref, v_ref, wc_ref, s0_ref = refs[:7]
    y_ref, s_ref = refs[7 + n_cast:9 + n_cast]
    for w_ref, o_ref in zip(refs[7:7 + n_cast], refs[9 + n_cast:]):
        o_ref[...] = w_ref[...].astype(BF16)

    if per_chunk_state:
        s_in = s0_ref
    else:
        s_in = s_ref

        @pl.when(pl.program_id(1) == 0)
        def _():
            for q in range(n_par):
                s_ref[q] = s0_ref[0]

    n_blk = rows // chunk
    lanes = [slice(h * HEAD_DIM, (h + 1) * HEAD_DIM) for h in range(N_HEADS)]
    bf = lambda x: x.astype(BF16)

    shift = chunk.bit_length() - 1
    rs = lax.broadcasted_iota(jnp.int32, (rows, rows), 0)
    cs = lax.broadcasted_iota(jnp.int32, (rows, rows), 1)
    same = (rs >> shift) == (cs >> shift)
    mask_strict = same & (rs > cs)
    ri = lax.broadcasted_iota(jnp.int32, (rows, 2 * rows), 0)
    ci = lax.broadcasted_iota(jnp.int32, (rows, 2 * rows), 1)
    cj = jnp.where(ci >= rows, ci - rows, ci)
    mask_r = ((ri >> shift) == (cj >> shift)) & (ri >= cj)
    eye = (rs == cs).astype(F32)

    def chunk_rows(x, c):
        return x[c * chunk:(c + 1) * chunk, :]

    pair_lane = lax.broadcasted_iota(jnp.int32, (rows, 2 * HEAD_DIM), 1) >> (HEAD_DIM.bit_length() - 1)

    def pair_masked(ref, q, h):
        hp, hh = divmod(h, 2)
        x2 = ref[q, :, 2 * HEAD_DIM * hp:2 * HEAD_DIM * (hp + 1)].astype(F32)
        return jnp.where(pair_lane == hh, x2, 0.0)

    def unit_group(units):
        per_unit = lambda f: {u: f(*u) for u in units}
        v = per_unit(lambda q, h: v_ref[q, :, lanes[h]])
        if n_blk == 1:
            a = per_unit(lambda q, h: ah_ref[q, :, lanes[h]])
            r = per_unit(lambda q, h: rh_ref[q, :, lanes[h]])
            b = per_unit(lambda q, h: bh_ref[q, :, lanes[h]])
            k = per_unit(lambda q, h: kh_ref[q, :, lanes[h]])
        else:
            a = per_unit(lambda q, h: pair_masked(ah_ref, q, h))
            r = per_unit(lambda q, h: pair_masked(rh_ref, q, h))
            b = per_unit(lambda q, h: pair_masked(bh_ref, q, h))
            k = per_unit(lambda q, h: pair_masked(kh_ref, q, h))
            v32 = {u: v[u].astype(F32) for u in units}
        ar = {u: bf(jnp.concatenate([a[u], r[u]], axis=0)) for u in units}
        bk = {u: bf(jnp.concatenate([b[u], k[u]], axis=0)) for u in units}

        gram = {u: _dot(ar[u], jnp.transpose(bk[u])) for u in units}
        l_ab = {u: jnp.where(mask_strict, gram[u][:rows, :rows], 0.0) for u in units}
        l_ak = {u: bf(jnp.where(mask_strict, gram[u][:rows, rows:], 0.0)) for u in units}
        l_r = {u: bf(jnp.where(mask_r, gram[u][rows:, :], 0.0)) for u in units}

        xs, ys = {}, {}
        for u in units:
            q, h = u
            if n_blk == 1:
                st = _dot_nt(ar[u], bf(s_in[q, h]))
                xs[u], ys[u] = st[:rows], st[rows:]
            else:
                parts = [_dot_nt(jnp.concatenate([chunk_rows(a[u], c), chunk_rows(r[u], c)], axis=0),
                                 s_in[c, h // 2]) for c in range(n_blk)]
                xs[u] = jnp.concatenate([p[:chunk] for p in parts], axis=0)
                ys[u] = jnp.concatenate([p[chunk:] for p in parts], axis=0)

        x = {u: xs[u] + _dot(l_ak[u], v[u]) for u in units}

        t_inv = {u: eye + l_ab[u] for u in units}
        n = 2
        if n < chunk:
            pw = {u: bf(l_ab[u]) for u in units}
            pw = {u: bf(_dot(pw[u], pw[u])) for u in units}
        while n < chunk:
            if 2 * n < chunk:
                both = {u: _dot(jnp.concatenate([bf(t_inv[u]), pw[u]], axis=0), pw[u]) for u in units}
                t_inv = {u: t_inv[u] + both[u][:rows] for u in units}
                pw = {u: bf(both[u][rows:]) for u in units}
            else:
                t_inv = {u: t_inv[u] + _dot(bf(t_inv[u]), pw[u]) for u in units}
            n *= 2

        uu = {u: _dot(bf(t_inv[u]), bf(x[u])) for u in units}
        uv = {u: jnp.concatenate([bf(uu[u]), v[u]], axis=0) for u in units}
        for u in units:
            q, h = u
            y_ref[q, :, lanes[h]] = ys[u] + _dot(l_r[u], uv[u])

        if n_blk == 1:
            for u in units:
                q, h = u
                upd = _dot_tn(uv[u], bk[u])
                s_ref[q, h] = (s_in[q, h] + upd) * wc_ref[q, 0][:, lanes[h]]
        else:
            def update(u, c):
                uv_c = jnp.concatenate([chunk_rows(uu[u], c), chunk_rows(v32[u], c)], axis=0)
                bk_c = jnp.concatenate([chunk_rows(b[u], c), chunk_rows(k[u], c)], axis=0)
                return _dot_tn(uv_c, bk_c)

            for q, h in units:
                if h % 2 == 0:
                    assert (q, h + 1) in units
                    hp = h // 2
                    for c in range(n_blk):
                        s_new = s_in[c, hp] + update((q, h), c) + update((q, h + 1), c)
                        s_ref[c, hp] = s_new * wc_ref[q, c][:, 2 * HEAD_DIM * hp:2 * HEAD_DIM * (hp + 1)]

    all_units = [(q, h) for h in range(N_HEADS) for q in range(n_par)]
    for g0 in range(0, len(all_units), group):
        unit_group(all_units[g0:g0 + group])


def _scan(ah, rh, bh, kh, v, wc, s0, *, n_seq, seq_len, n_par, rows, chunk, per_chunk_state, group,
          name, cast=()):
    t = n_seq * seq_len
    n_blk = rows // chunk
    if per_chunk_state:
        assert seq_len == chunk and n_par == 1
        lead = 1
        grid = (t // rows,)
        rmap = lambda i: (0, i, 0)
        cmap = lambda i: (0, i, 0, 0)
        smap = lambda i: (i, 0, 0, 0)
        s0map = smap
        state_block = (n_blk, N_HEADS // 2, HEAD_DIM, 2 * HEAD_DIM)
        s0_block = state_block
        wc_block = (1, n_blk, 1, G)
        sem = ("parallel",)
    else:
        assert rows == chunk and s0.shape[0] == 1 and n_seq % n_par == 0
        lead = n_seq
        grid = (n_seq // n_par, seq_len // chunk)
        rmap = lambda s, c: (s, c, 0)
        cmap = lambda s, c: (s, c, 0, 0)
        smap = lambda s, c: (s, 0, 0, 0)
        s0map = lambda s, c: (0, 0, 0, 0)
        state_block = (n_par, N_HEADS, HEAD_DIM, HEAD_DIM)
        s0_block = (1, N_HEADS, HEAD_DIM, HEAD_DIM)
        wc_block = (n_par, 1, 1, G)
        sem = ("parallel", "arbitrary")
    rows3 = lambda x: x.reshape(lead, t // lead, G)
    row_spec = pl.BlockSpec((n_par, rows, G), rmap)
    n_steps = 1
    for extent in grid:
        n_steps *= extent
    step = (lambda i: i) if len(grid) == 1 else (lambda s, c: s * grid[1] + c)
    cast_specs = []
    for w in cast:
        assert w.shape[0] % (16 * n_steps) == 0
        cast_specs.append(pl.BlockSpec((w.shape[0] // n_steps, w.shape[1]), lambda *g: (step(*g), 0)))
    y, s_out, *narrowed = pl.pallas_call(
        functools.partial(_scan_kernel, n_par=n_par, rows=rows, chunk=chunk,
                          per_chunk_state=per_chunk_state, group=group, n_cast=len(cast)),
        grid=grid,
        in_specs=([row_spec] * 5 + [pl.BlockSpec(wc_block, cmap), pl.BlockSpec(s0_block, s0map)]
                  + cast_specs),
        out_specs=[row_spec, pl.BlockSpec(state_block, smap)] + cast_specs,
        out_shape=[jax.ShapeDtypeStruct((lead, t // lead, G), F32),
                   jax.ShapeDtypeStruct((n_seq,) + state_block[1:], F32)]
                  + [jax.ShapeDtypeStruct(w.shape, BF16) for w in cast],
        compiler_params=pltpu.CompilerParams(dimension_semantics=sem, vmem_limit_bytes=VMEM_LIMIT),
        name=name,
    )(rows3(ah), rows3(rh), rows3(bh), rows3(kh), rows3(v), wc.reshape(lead, -1, 1, G), s0, *cast)
    return (y.reshape(t, G), s_out, *narrowed)


def _mix_kernel(*refs, segs):
    n = len(segs)
    row_refs = [refs[4 * s:4 * s + 4] for s in range(n)]
    x_ref, lnw_ref, lnb_ref, bd_ref, wo_ref, o_ref = refs[4 * n:]
    i = pl.program_id(0)

    def body(y_ref, bonus_ref, g_ref, conv_ref):
        bd = bd_ref[...]
        y = y_ref[...]
        mu = _head_sum(y, bd) * (1.0 / HEAD_DIM)
        d = y - mu
        var = _head_sum(d * d, bd) * (1.0 / HEAD_DIM)
        yn = d * lax.rsqrt(var + GN_EPS) * lnw_ref[...] + lnb_ref[...]
        rw = ((yn + bonus_ref[...]) * g_ref[...]).astype(BF16)
        mix = _dot(rw, wo_ref[0:G, :]) + _dot(conv_ref[...], wo_ref[G:2 * G, :])
        o_ref[...] = x_ref[...] + mix

    for rr, (off, cnt) in zip(row_refs, segs):
        @pl.when((i >= off) & (i < off + cnt))
        def _(rr=rr):
            body(*rr)


def _mix(row_groups, x1, lnw, lnb, bd, wo):
    segs, n_tiles = _segments([grp[0] for grp in row_groups], TM_MIX)
    cmap = lambda i: (0, 0)
    in_specs, args = [], []
    for grp, (off, cnt) in zip(row_groups, segs):
        smap = lambda i, off=off, cnt=cnt: (jnp.clip(i - off, 0, cnt - 1), 0)
        in_specs += [pl.BlockSpec((TM_MIX, G), smap)] * 4
        args += list(grp)
    in_specs += [pl.BlockSpec((TM_MIX, D_MODEL), lambda i: (i, 0)),
                 pl.BlockSpec((1, G), cmap), pl.BlockSpec((1, G), cmap),
                 pl.BlockSpec(bd.shape, cmap), pl.BlockSpec((D_MODEL, D_MODEL), cmap)]
    return pl.pallas_call(
        functools.partial(_mix_kernel, segs=segs),
        grid=(n_tiles,), in_specs=in_specs,
        out_specs=pl.BlockSpec((TM_MIX, D_MODEL), lambda i: (i, 0)),
        out_shape=jax.ShapeDtypeStruct((n_tiles * TM_MIX, D_MODEL), F32),
        compiler_params=pltpu.CompilerParams(
            dimension_semantics=("parallel",), vmem_limit_bytes=VMEM_LIMIT),
        name="mix",
    )(*args, x1, lnw, lnb, bd, wo)


def _pack_rwkv(a):
    r, wd, k, v, ad, gd = jnp.split(a, RWKV_SPLITS, axis=-1)
    zw = jnp.zeros(a.shape[:-1] + (LORA_PAD - W_LORA,), a.dtype)
    za = jnp.zeros(a.shape[:-1] + (LORA_PAD - A_LORA,), a.dtype)
    return jnp.concatenate([r, k, v, wd, zw, ad, za, gd], axis=-1)


def _unpack_rwkv(p):
    return jnp.concatenate([
        p[..., 0:G], p[..., OFF_WD:OFF_WD + W_LORA], p[..., G:2 * G], p[..., 2 * G:3 * G],
        p[..., OFF_AD:OFF_AD + A_LORA], p[..., OFF_GD:OFF_GD + G_LORA]], axis=-1)


def _block_tri(n, chunk):
    i = jnp.arange(n)
    return ((i[:, None] // chunk == i[None, :] // chunk) & (i[:, None] >= i[None, :])).astype(BF16)


def kernel(x_prompt, x_sample, state_wkv, state_shift, state_conv, meta_tokens, g_ffn1, ffn1_gate, ffn1_up, ffn1_down, g_mix, w_in, mu_shift, w0, w_lora_w, a0, w_lora_a, w_lora_g, k_k, k_a, r_k, ln_x_w, ln_x_b, conv_w, w_out, g_ffn2, ffn2_gate, ffn2_up, ffn2_down, g_final):
    assert g_ffn1.shape[0] == 1, "single layer"
    nb, seq, _ = x_prompt.shape
    db, dseq, _ = x_sample.shape
    assert dseq == C_SAMPLE and N_META <= C_PROMPT and seq % ROWS_PREP == 0
    tp, ts = nb * seq, db * dseq
    assert tp % TM == 0 and ts % TM == 0

    row = lambda a: a.reshape(1, -1).astype(F32)
    pad_rows = lambda w, n: jnp.concatenate([w, jnp.zeros((n - w.shape[0], w.shape[1]), w.dtype)], axis=0)
    hid = jnp.arange(HEAD_BLOCK) // HEAD_DIM
    bd = (hid[:, None] == hid[None, :]).astype(BF16)
    prep_w = (_pack_rwkv(mu_shift[0])[None], row(w0[0]), pad_rows(w_lora_w[0], LORA_PAD).astype(BF16),
              row(a0[0]), pad_rows(w_lora_a[0], LORA_PAD).astype(BF16), w_lora_g[0].astype(BF16),
              row(k_k[0]), row(k_a[0]), row(r_k[0]), conv_w[0].astype(F32), bd)

    x_meta = jnp.concatenate([jnp.zeros((C_PROMPT - N_META, D_MODEL), F32), meta_tokens.astype(F32)], axis=0)
    x1_meta, wg1, wu1, wd1 = _ffn_narrow(x_meta, row(g_ffn1[0]), ffn1_gate[0], ffn1_up[0], ffn1_down[0],
                                         name="ffn1_meta")
    x1, w_in_p = _ffn_two_stage([x_prompt.reshape(tp, D_MODEL), x_sample.reshape(ts, D_MODEL)],
                                row(g_ffn1[0]), wg1, wu1, wd1, None, [tp + ts], name="ffn1",
                                pack_src=jnp.transpose(w_in[0]))
    x1_meta = jnp.concatenate([x1_meta, jnp.zeros((TM - C_PROMPT, D_MODEL), F32)], axis=0)
    proj = _proj([x1, x1_meta], row(g_mix[0]), w_in_p, name="proj")
    meta_row0 = tp + ts

    tri_p = _block_tri(ROWS_PREP, C_PROMPT)
    zeros_state = jnp.zeros((1, N_HEADS, HEAD_DIM, HEAD_DIM), F32)
    (ah, rh, bh, kh, vv, wc, _, _, _, utail_m, ptail_m) = _prep(
        proj, prep_w, tri_p[:C_PROMPT, :C_PROMPT], n_seq=1, seq_len=C_PROMPT, rows=C_PROMPT,
        chunk=C_PROMPT, row_block_offset=meta_row0 // C_PROMPT,
        extra=(jnp.zeros((8, RW), F32), jnp.zeros((8, G), F32)),
        sample=False, name="prep_meta")
    _, wkv_m = _scan(ah, rh, bh, kh, vv, wc, zeros_state, n_seq=1, seq_len=C_PROMPT, n_par=1,
                     rows=C_PROMPT, chunk=C_PROMPT, per_chunk_state=False, group=N_HEADS,
                     name="scan_meta")

    (ah, rh, bh, kh, vv, wc, g_p, bonus_p, conv_p, utail_p, ptail_p) = _prep(
        proj, prep_w, tri_p, n_seq=nb, seq_len=seq, rows=ROWS_PREP, chunk=C_PROMPT,
        row_block_offset=0, extra=(ptail_m[0], utail_m[0]), sample=False, name="prep_prompt")
    y_p, wkv_p, wg2, wu2, wd2, wo = _scan(
        ah, rh, bh, kh, vv, wc, wkv_m, n_seq=nb, seq_len=seq, n_par=SCAN_PAR, rows=C_PROMPT,
        chunk=C_PROMPT, per_chunk_state=False, group=SCAN_PAR * N_HEADS // 2, name="scan_prompt",
        cast=(ffn2_gate[0], ffn2_up[0], ffn2_down[0], w_out[0]))

    prev = _pack_rwkv(state_shift[0])
    up1 = state_conv[0][:, 1]
    up2 = state_conv[0][:, 0]
    (ah, rh, bh, kh, vv, wc, g_s, bonus_s, conv_s, u_s, plast_s) = _prep(
        proj, prep_w, _block_tri(ROWS_PREP, C_SAMPLE), n_seq=db, seq_len=dseq, rows=ROWS_PREP,
        chunk=C_SAMPLE, row_block_offset=tp // ROWS_PREP, extra=(prev, up1, up2), sample=True,
        name="prep_sample")
    pair = (db, N_HEADS // 2, 2, HEAD_DIM, HEAD_DIM)
    s0_pairs = jnp.transpose(state_wkv[0].reshape(pair), (0, 1, 3, 2, 4)).reshape(
        db, N_HEADS // 2, HEAD_DIM, 2 * HEAD_DIM)
    y_s, wkv_s = _scan(ah, rh, bh, kh, vv, wc, s0_pairs, n_seq=db, seq_len=dseq, n_par=1,
                       rows=ROWS_SCAN_SAMPLE, chunk=C_SAMPLE, per_chunk_state=True, group=16,
                       name="scan_sample")

    x2 = _mix([(y_p, bonus_p, g_p, conv_p), (y_s, bonus_s, g_s, conv_s)], x1,
              row(ln_x_w[0]), row(ln_x_b[0]), bd, wo)
    y_prompt, y_sample = _ffn_two_stage([x2], row(g_ffn2[0]), wg2, wu2, wd2, row(g_final), [tp, ts],
                                        name="ffn2")

    shift_p = _unpack_rwkv(ptail_p.reshape(nb, seq // ROWS_PREP, 8, RW)[:, -1, 7, :])
    conv_state_p = utail_p.reshape(nb, seq // ROWS_PREP, 8, G)[:, -1, 6:, :]
    shift_s = _unpack_rwkv(plast_s)
    conv_state_s = u_s.reshape(db, dseq, G)[:, -2:, :]
    wkv_s_out = jnp.transpose(wkv_s.reshape(db, N_HEADS // 2, HEAD_DIM, 2, HEAD_DIM),
                              (0, 1, 3, 2, 4)).reshape(db, N_HEADS, HEAD_DIM, HEAD_DIM)
    return (y_prompt.reshape(nb, seq, D_MODEL), y_sample.reshape(db, dseq, D_MODEL),
            wkv_p[None].astype(state_wkv.dtype), shift_p[None].astype(state_shift.dtype),
            conv_state_p[None].astype(state_conv.dtype),
            wkv_s_out[None].astype(state_wkv.dtype), shift_s[None].astype(state_shift.dtype),
            conv_state_s[None].astype(state_conv.dtype))
```

```python
import functools

import jax
import jax.numpy as jnp
from jax import lax
from jax.experimental import pallas as pl
from jax.experimental.pallas import tpu as pltpu

F32 = jnp.float32
BF16 = jnp.bfloat16

D_MODEL = 2048
D_FF = 5632
N_META = 16
G = 1024
HEAD_DIM = 64
N_HEADS = G // HEAD_DIM
MXU_N = 256
HEAD_BLOCK = MXU_N
W_LORA = 96
A_LORA = 96
G_LORA = 256
LORA_PAD = 128
RWKV_PROJ = 3 * G + W_LORA + A_LORA + G_LORA
RWKV_SPLITS = (G, G + W_LORA, 2 * G + W_LORA, 3 * G + W_LORA, 3 * G + W_LORA + A_LORA)
RW = 3 * G + 2 * LORA_PAD + G_LORA
PW = RW + 3 * G
OFF_WD, OFF_AD, OFF_GD = 3 * G, 3 * G + LORA_PAD, 3 * G + 2 * LORA_PAD
RMS_EPS = 1e-6
GN_EPS = 64e-5
EXP_M05_LOG2E = 0.6065306597126334 * 1.4426950408889634

C_PROMPT = 64
C_SAMPLE = 8
TM = 512
TF = 512
TN = PW // 2
assert TN % MXU_N == 0
TM_FFN_UP = 512
TM_FFN_DOWN = 256
TM_MIX = 256
ROWS_PREP = 256
ROWS_SCAN_SAMPLE = 128
SCAN_PAR = 4
VMEM_LIMIT = 56 * 1024 * 1024


def _dot(a, b):
    return jnp.dot(a, b, preferred_element_type=F32)


def _dot_nt(a, b):
    return lax.dot_general(a, b, (((1,), (1,)), ((), ())), preferred_element_type=F32)


def _dot_tn(a, b):
    return lax.dot_general(a, b, (((0,), (0,)), ((), ())), preferred_element_type=F32)


def _split(a):
    hi = a.astype(BF16)
    lo = (a - hi.astype(F32)).astype(BF16)
    return hi, lo


def _head_sum(x, bd):
    xb = x.astype(BF16)
    wb = bd.shape[0]
    return jnp.concatenate([_dot(xb[:, c:c + wb], bd) for c in range(0, x.shape[1], wb)], axis=1)


def _dot_split_rhs(a_bf16, b):
    hi, lo = _split(b)
    return _dot(a_bf16, hi) + _dot(a_bf16, lo)


def _rms(x, g):
    return x * lax.rsqrt(jnp.mean(x * x, axis=-1, keepdims=True) + RMS_EPS) * g


def _segments(arrays, tile):
    segs, off = [], 0
    for arr in arrays:
        n = arr.shape[0] // tile
        assert n * tile == arr.shape[0]
        segs.append((off, n))
        off += n
    return segs, off


def _overlaps(in_segs, out_segs):
    for a, (ao, an) in enumerate(in_segs):
        for b, (bo, bn) in enumerate(out_segs):
            lo, hi = max(ao, bo), min(ao + an, bo + bn)
            if lo < hi:
                yield a, b, lo, hi


def _ffn_narrow_kernel(x_ref, g_ref, wg_ref, wu_ref, wd_ref, o_ref, wg_o, wu_o, wd_o, h_ref, acc_ref):
    j = pl.program_id(0)

    @pl.when(j == 0)
    def _():
        h_ref[...] = _rms(x_ref[...], g_ref[...]).astype(BF16)
        acc_ref[...] = jnp.zeros_like(acc_ref)

    wg, wu, wd = wg_ref[...].astype(BF16), wu_ref[...].astype(BF16), wd_ref[...].astype(BF16)
    wg_o[...], wu_o[...], wd_o[...] = wg, wu, wd
    h = h_ref[...]
    gate = _dot(h, wg)
    up = _dot(h, wu)
    act = (gate * jax.nn.sigmoid(gate) * up).astype(BF16)
    acc_ref[...] += _dot(act, wd)

    @pl.when(j == pl.num_programs(0) - 1)
    def _():
        o_ref[...] = x_ref[...] + 0.5 * acc_ref[...]


def _ffn_narrow(x, g, wg, wu, wd, *, name):
    rows = x.shape[0]
    whole = pl.BlockSpec((rows, D_MODEL), lambda j: (0, 0))
    w_specs = [pl.BlockSpec((D_MODEL, TF), lambda j: (0, j)),
               pl.BlockSpec((D_MODEL, TF), lambda j: (0, j)),
               pl.BlockSpec((TF, D_MODEL), lambda j: (j, 0))]
    return pl.pallas_call(
        _ffn_narrow_kernel,
        grid=(D_FF // TF,),
        in_specs=[whole, pl.BlockSpec((1, D_MODEL), lambda j: (0, 0))] + w_specs,
        out_specs=[whole] + w_specs,
        out_shape=[jax.ShapeDtypeStruct((rows, D_MODEL), F32)]
                  + [jax.ShapeDtypeStruct(w.shape, BF16) for w in (wg, wu, wd)],
        scratch_shapes=[pltpu.VMEM((rows, D_MODEL), BF16), pltpu.VMEM((rows, D_MODEL), F32)],
        compiler_params=pltpu.CompilerParams(
            dimension_semantics=("arbitrary",), vmem_limit_bytes=VMEM_LIMIT),
        name=name,
    )(x, g, wg, wu, wd)


def _ffn_up_kernel(*refs, in_segs):
    n_in = len(in_segs)
    x_refs = refs[:n_in]
    g_ref, wg_ref, wu_ref, a_ref = refs[n_in:]
    i = pl.program_id(1)
    for x_ref, (off, n) in zip(x_refs, in_segs):
        @pl.when((i >= off) & (i < off + n))
        def _(x_ref=x_ref):
            h = _rms(x_ref[...], g_ref[...]).astype(BF16)
            for c in range(0, a_ref.shape[1], MXU_N):
                gate = _dot(h, wg_ref[:, c:c + MXU_N])
                up = _dot(h, wu_ref[:, c:c + MXU_N])
                a_ref[:, c:c + MXU_N] = (gate * jax.nn.sigmoid(gate) * up).astype(BF16)


def _ffn_down_kernel(*refs, in_segs, out_segs, final_norm, pack_steps):
    n_in, n_out = len(in_segs), len(out_segs)
    a_ref = refs[0]
    x_refs = refs[1:1 + n_in]
    wd_ref = refs[1 + n_in]
    rest = refs[2 + n_in:]
    if final_norm:
        gf_ref, rest = rest[0], rest[1:]
    if pack_steps:
        wt_ref, rest = rest[0], rest[1:]
    o_refs = rest[:n_out]
    i = pl.program_id(0)
    if pack_steps:
        @pl.when(i < pack_steps)
        def _():
            _pack_w_in_kernel(wt_ref, rest[n_out])

    for a, b, lo, hi in _overlaps(in_segs, out_segs):
        @pl.when((i >= lo) & (i < hi))
        def _(x_ref=x_refs[a], o_ref=o_refs[b]):
            xo = x_ref[...] + 0.5 * _dot(a_ref[...], wd_ref[...])
            if final_norm:
                xo = _rms(xo, gf_ref[...])
            o_ref[...] = xo


def _ffn_two_stage(xs, g, wg, wu, wd, g_final, out_rows, *, name, pack_src=None):
    final_norm = g_final is not None
    tm_up, tn = TM_FFN_UP, D_FF // 2
    up_segs, up_tiles = _segments(xs, tm_up)
    seg_ji = lambda s: pl.BlockSpec((tm_up, D_MODEL), lambda j, i, s=s: (jnp.clip(i - s[0], 0, s[1] - 1), 0))
    half = pl.BlockSpec((D_MODEL, tn), lambda j, i: (0, j), pipeline_mode=pl.Buffered(1))
    act = pl.pallas_call(
        functools.partial(_ffn_up_kernel, in_segs=up_segs),
        grid=(D_FF // tn, up_tiles),
        in_specs=[seg_ji(s) for s in up_segs] + [pl.BlockSpec((1, D_MODEL), lambda j, i: (0, 0)), half, half],
        out_specs=pl.BlockSpec((tm_up, tn), lambda j, i: (i, j)),
        out_shape=jax.ShapeDtypeStruct((up_tiles * tm_up, D_FF), BF16),
        compiler_params=pltpu.CompilerParams(
            dimension_semantics=("arbitrary", "arbitrary"), vmem_limit_bytes=VMEM_LIMIT),
        name=name + "_up",
    )(*xs, g, wg, wu)

    tm = TM_FFN_DOWN
    in_segs, n_tiles = _segments(xs, tm)
    out_shape = [jax.ShapeDtypeStruct((n, D_MODEL), F32) for n in out_rows]
    out_segs, n_out_tiles = _segments(out_shape, tm)
    assert n_out_tiles == n_tiles and n_tiles * tm == up_tiles * tm_up
    seg_i = lambda s: pl.BlockSpec((tm, D_MODEL), lambda i, s=s: (jnp.clip(i - s[0], 0, s[1] - 1), 0))
    in_specs = ([pl.BlockSpec((tm, D_FF), lambda i: (i, 0))] + [seg_i(s) for s in in_segs]
                + [pl.BlockSpec((D_FF, D_MODEL), lambda i: (0, 0), pipeline_mode=pl.Buffered(1))])
    args = [act] + list(xs) + [wd]
    if final_norm:
        in_specs.append(pl.BlockSpec((1, D_MODEL), lambda i: (0, 0)))
        args.append(g_final)
    out_specs = [seg_i(s) for s in out_segs]
    pack_steps = 0
    if pack_src is not None:
        lane_blk = 128
        pack_steps = D_MODEL // lane_blk
        assert pack_steps <= n_tiles
        blk = lambda i: (0, jnp.minimum(i, pack_steps - 1))
        in_specs.append(pl.BlockSpec((pack_src.shape[0], lane_blk), blk))
        args.append(pack_src)
        out_specs.append(pl.BlockSpec((PW, lane_blk), blk))
        out_shape = out_shape + [jax.ShapeDtypeStruct((PW, D_MODEL), BF16)]
    return pl.pallas_call(
        functools.partial(_ffn_down_kernel, in_segs=in_segs, out_segs=out_segs, final_norm=final_norm,
                          pack_steps=pack_steps),
        grid=(n_tiles,),
        in_specs=in_specs,
        out_specs=out_specs,
        out_shape=out_shape,
        compiler_params=pltpu.CompilerParams(
            dimension_semantics=("arbitrary",), vmem_limit_bytes=VMEM_LIMIT),
        name=name + "_down",
    )(*args)


_PACK_MOVES = (
    (0, 0, G),
    (G, RWKV_SPLITS[1], G),
    (2 * G, RWKV_SPLITS[2], G),
    (OFF_WD, RWKV_SPLITS[0], W_LORA),
    (OFF_AD, RWKV_SPLITS[3], A_LORA),
    (OFF_GD, RWKV_SPLITS[4], G_LORA),
    (RW, RWKV_PROJ, 3 * G),
)


def _pack_w_in_kernel(w_ref, o_ref):
    for dst, src, n in _PACK_MOVES:
        o_ref[dst:dst + n, :] = w_ref[src:src + n, :].astype(BF16)
    for lo, hi in ((OFF_WD + W_LORA, OFF_AD), (OFF_AD + A_LORA, OFF_GD)):
        o_ref[lo:hi, :] = jnp.zeros((hi - lo, o_ref.shape[1]), BF16)


def _proj_kernel(*refs, in_segs):
    n_in = len(in_segs)
    x_refs = refs[:n_in]
    g_ref, w_ref, o_ref = refs[n_in:]
    i = pl.program_id(1)
    for x_ref, (off, n) in zip(x_refs, in_segs):
        @pl.when((i >= off) & (i < off + n))
        def _(x_ref=x_ref):
            h = _rms(x_ref[...], g_ref[...]).astype(BF16)
            o_ref[...] = _dot_nt(h, w_ref[...])


def _proj(xs, g, w, *, name):
    in_segs, n_tiles = _segments(xs, TM)
    seg_ji = lambda s: pl.BlockSpec((TM, D_MODEL), lambda j, i, s=s: (jnp.clip(i - s[0], 0, s[1] - 1), 0),
                                    pipeline_mode=pl.Buffered(1 if s[1] == 1 else 2))
    return pl.pallas_call(
        functools.partial(_proj_kernel, in_segs=in_segs),
        grid=(PW // TN, n_tiles),
        in_specs=[seg_ji(s) for s in in_segs] + [
            pl.BlockSpec((1, D_MODEL), lambda j, i: (0, 0)),
            pl.BlockSpec((TN, D_MODEL), lambda j, i: (j, 0)),
        ],
        out_specs=pl.BlockSpec((TM, TN), lambda j, i: (i, j)),
        out_shape=jax.ShapeDtypeStruct((n_tiles * TM, PW), F32),
        compiler_params=pltpu.CompilerParams(
            dimension_semantics=("parallel", "parallel"), vmem_limit_bytes=VMEM_LIMIT),
        name=name,
    )(*xs, g, w)


def _prep_kernel(*refs, rows, chunk, sample):
    if sample:
        (p_ref, prev_ref, up1_ref, up2_ref, mu_ref, w0_ref, ww_ref, a0_ref, wa_ref, wgl_ref,
         kk_ref, ka_ref, rk_ref, cw_ref, bd_ref, tri_ref,
         ah_ref, rh_ref, bh_ref, kh_ref, v_ref, wc_ref, g_ref, bonus_ref, conv_ref, u_ref,
         plast_ref) = refs
    else:
        (p_ref, cp0_ref, cu0_ref, mu_ref, w0_ref, ww_ref, a0_ref, wa_ref, wgl_ref,
         kk_ref, ka_ref, rk_ref, cw_ref, bd_ref, tri_ref,
         ah_ref, rh_ref, bh_ref, kh_ref, v_ref, wc_ref, g_ref, bonus_ref, conv_ref, u_ref,
         ptail_ref, carry_p, carry_u) = refs

        @pl.when(pl.program_id(1) == 0)
        def _():
            carry_p[...] = cp0_ref[...]
            carry_u[...] = cu0_ref[...]

    def row_ids(width):
        r = lax.broadcasted_iota(jnp.int32, (rows, width), 0)
        return (r & 7) if sample else r

    def per_sequence(x):
        n, w = x.shape
        return jnp.broadcast_to(x[:, None, :], (n, 8, w)).reshape(n * 8, w)

    def shifted_mix(lo, hi):
        p = p_ref[:, lo:hi]
        rolled = pltpu.roll(p, 1, 0)
        if sample:
            sh = jnp.where(row_ids(hi - lo) == 0, per_sequence(prev_ref[:, lo:hi]), rolled)
        else:
            sh = jnp.where(row_ids(hi - lo) == 0, carry_p[7:8, lo:hi], rolled)
        return p + (sh - p) * mu_ref[:, lo:hi]

    r = shifted_mix(0, G)
    k = shifted_mix(G, 2 * G)
    v = shifted_mix(2 * G, 3 * G)
    wd = shifted_mix(OFF_WD, OFF_WD + LORA_PAD)
    ad = shifted_mix(OFF_AD, OFF_AD + LORA_PAD)
    gd = shifted_mix(OFF_GD, OFF_GD + G_LORA)

    z = w0_ref[...] + _dot(jnp.tanh(wd).astype(BF16), ww_ref[...])
    lw = -EXP_M05_LOG2E * jax.nn.sigmoid(z)
    a = jax.nn.sigmoid(a0_ref[...] + _dot(ad.astype(BF16), wa_ref[...]))
    g_ref[...] = _dot(jax.nn.sigmoid(gd).astype(BF16), wgl_ref[...]).astype(BF16)

    bd = bd_ref[...]
    kk = k * kk_ref[...]
    norm = jnp.sqrt(_head_sum(kk * kk, bd))
    kk = kk / jnp.maximum(norm, 1e-12)
    km = k * (1.0 + (a - 1.0) * ka_ref[...])
    bonus_ref[...] = _head_sum(r * km * rk_ref[...], bd) * v

    cum = _dot_split_rhs(tri_ref[...], lw)
    e_cum = jnp.exp2(cum)
    e_inv = jnp.exp2(-cum)
    ah_ref[...] = (-kk * jnp.exp2(cum - lw)).astype(BF16)
    rh_ref[...] = (r * e_cum).astype(BF16)
    bh_ref[...] = (kk * a * e_inv).astype(BF16)
    kh_ref[...] = (km * e_inv).astype(BF16)
    v_ref[...] = v.astype(BF16)
    for c in range(rows // chunk):
        wc_ref[c] = e_cum[(c + 1) * chunk - 1:(c + 1) * chunk, :]

    bg = p_ref[:, RW:RW + G]
    u = p_ref[:, RW + G:RW + 2 * G] * p_ref[:, RW + 2 * G:RW + 3 * G]
    u1 = pltpu.roll(u, 1, 0)
    u2 = pltpu.roll(u, 2, 0)
    rid = row_ids(G)
    if sample:
        s1 = per_sequence(up1_ref[...])
        um1 = jnp.where(rid == 0, s1, u1)
        um2 = jnp.where(rid == 0, per_sequence(up2_ref[...]), jnp.where(rid == 1, s1, u2))
        u_ref[...] = u
        plast_ref[...] = p_ref[:, 0:RW].reshape(rows // 8, 8, RW)[:, 7, :]
    else:
        um1 = jnp.where(rid == 0, carry_u[7:8, :], u1)
        um2 = jnp.where(rid == 0, carry_u[6:7, :], jnp.where(rid == 1, carry_u[7:8, :], u2))
        u_ref[0] = u[rows - 8:, :]
        ptail_ref[0] = p_ref[rows - 8:rows, 0:RW]
    conv = cw_ref[0:1, :] * um2 + cw_ref[1:2, :] * um1 + cw_ref[2:3, :] * u
    conv_ref[...] = (bg * conv).astype(BF16)

    if not sample:
        carry_p[...] = p_ref[rows - 8:rows, 0:RW]
        carry_u[...] = u[rows - 8:, :]


def _prep(proj, weights, tri, *, n_seq, seq_len, rows, chunk, row_block_offset, extra, sample, name):
    t = n_seq * seq_len
    n_chunks = t // chunk
    cpt = rows // chunk
    if sample:
        grid = (t // rows,)
        rmap = lambda i: (i + row_block_offset, 0)
        omap = lambda i: (i, 0)
        omap3 = lambda i: (i, 0, 0)
        cmap = lambda i: (0, 0)
        sem = ("parallel",)
    else:
        tiles = seq_len // rows
        grid = (n_seq, tiles)
        rmap = lambda b, j: (b * tiles + j + row_block_offset, 0)
        omap = lambda b, j: (b * tiles + j, 0)
        omap3 = lambda b, j: (b * tiles + j, 0, 0)
        cmap = lambda b, j: (0, 0)
        sem = ("parallel", "arbitrary")

    const = lambda arr: pl.BlockSpec(arr.shape, cmap)
    in_specs = [pl.BlockSpec((rows, PW), rmap)]
    if sample:
        in_specs += [pl.BlockSpec((rows // 8, RW), omap), pl.BlockSpec((rows // 8, G), omap),
                     pl.BlockSpec((rows // 8, G), omap)]
    else:
        in_specs += [const(e) for e in extra]
    in_specs += [const(w) for w in weights] + [const(tri)]
    args = [proj] + list(extra) + list(weights) + [tri]

    row_out = jax.ShapeDtypeStruct((t, G), F32)
    row_spec = pl.BlockSpec((rows, G), omap)
    row_bf16 = jax.ShapeDtypeStruct((t, G), BF16)
    out_shape = [row_bf16] * 5 + [jax.ShapeDtypeStruct((n_chunks, 1, G), F32), row_bf16, row_out,
                                  row_bf16]
    out_specs = [row_spec] * 5 + [pl.BlockSpec((cpt, 1, G), omap3), row_spec, row_spec, row_spec]
    if sample:
        out_shape += [row_out, jax.ShapeDtypeStruct((t // 8, RW), F32)]
        out_specs += [row_spec, pl.BlockSpec((rows // 8, RW), omap)]
        scratch = []
    else:
        out_shape += [jax.ShapeDtypeStruct((t // rows, 8, G), F32),
                      jax.ShapeDtypeStruct((t // rows, 8, RW), F32)]
        out_specs += [pl.BlockSpec((1, 8, G), omap3), pl.BlockSpec((1, 8, RW), omap3)]
        scratch = [pltpu.VMEM((8, RW), F32), pltpu.VMEM((8, G), F32)]
    return pl.pallas_call(
        functools.partial(_prep_kernel, rows=rows, chunk=chunk, sample=sample),
        grid=grid, in_specs=in_specs, out_specs=out_specs, out_shape=out_shape,
        scratch_shapes=scratch,
        compiler_params=pltpu.CompilerParams(dimension_semantics=sem, vmem_limit_bytes=VMEM_LIMIT),
        name=name,
    )(*args)


def _scan_kernel(*refs, n_par, rows, chunk, per_chunk_state, group, n_cast):
    ah_ref, rh_ref, bh_ref, kh_ref, v_ref, wc_ref, s0_ref = refs[:7]
    y_ref, s_ref = refs[7 + n_cast:9 + n_cast]
    for w_ref, o_ref in zip(refs[7:7 + n_cast], refs[9 + n_cast:]):
        o_ref[...] = w_ref[...].astype(BF16)

    if per_chunk_state:
        s_in = s0_ref
    else:
        s_in = s_ref

        @pl.when(pl.program_id(1) == 0)
        def _():
            for q in range(n_par):
                s_ref[q] = s0_ref[0]

    n_blk = rows // chunk
    lanes = [slice(h * HEAD_DIM, (h + 1) * HEAD_DIM) for h in range(N_HEADS)]
    bf = lambda x: x.astype(BF16)

    shift = chunk.bit_length() - 1
    rs = lax.broadcasted_iota(jnp.int32, (rows, rows), 0)
    cs = lax.broadcasted_iota(jnp.int32, (rows, rows), 1)
    same = (rs >> shift) == (cs >> shift)
    mask_strict = same & (rs > cs)
    ri = lax.broadcasted_iota(jnp.int32, (rows, 2 * rows), 0)
    ci = lax.broadcasted_iota(jnp.int32, (rows, 2 * rows), 1)
    cj = jnp.where(ci >= rows, ci - rows, ci)
    mask_r = ((ri >> shift) == (cj >> shift)) & (ri >= cj)
    eye = (rs == cs).astype(F32)

    def chunk_rows(x, c):
        return x[c * chunk:(c + 1) * chunk, :]

    pair_lane = lax.broadcasted_iota(jnp.int32, (rows, 2 * HEAD_DIM), 1) >> (HEAD_DIM.bit_length() - 1)

    def pair_masked(ref, q, h):
        hp, hh = divmod(h, 2)
        x2 = ref[q, :, 2 * HEAD_DIM * hp:2 * HEAD_DIM * (hp + 1)].astype(F32)
        return jnp.where(pair_lane == hh, x2, 0.0)

    def unit_group(units):
        per_unit = lambda f: {u: f(*u) for u in units}
        v = per_unit(lambda q, h: v_ref[q, :, lanes[h]])
        if n_blk == 1:
            a = per_unit(lambda q, h: ah_ref[q, :, lanes[h]])
            r = per_unit(lambda q, h: rh_ref[q, :, lanes[h]])
            b = per_unit(lambda q, h: bh_ref[q, :, lanes[h]])
            k = per_unit(lambda q, h: kh_ref[q, :, lanes[h]])
        else:
            a = per_unit(lambda q, h: pair_masked(ah_ref, q, h))
            r = per_unit(lambda q, h: pair_masked(rh_ref, q, h))
            b = per_unit(lambda q, h: pair_masked(bh_ref, q, h))
            k = per_unit(lambda q, h: pair_masked(kh_ref, q, h))
            v32 = {u: v[u].astype(F32) for u in units}
        ar = {u: bf(jnp.concatenate([a[u], r[u]], axis=0)) for u in units}
        bk = {u: bf(jnp.concatenate([b[u], k[u]], axis=0)) for u in units}

        gram = {u: _dot(ar[u], jnp.transpose(bk[u])) for u in units}
        l_ab = {u: jnp.where(mask_strict, gram[u][:rows, :rows], 0.0) for u in units}
        l_ak = {u: bf(jnp.where(mask_strict, gram[u][:rows, rows:], 0.0)) for u in units}
        l_r = {u: bf(jnp.where(mask_r, gram[u][rows:, :], 0.0)) for u in units}

        xs, ys = {}, {}
        for u in units:
            q, h = u
            if n_blk == 1:
                st = _dot_nt(ar[u], bf(s_in[q, h]))
                xs[u], ys[u] = st[:rows], st[rows:]
            else:
                parts = [_dot_nt(jnp.concatenate([chunk_rows(a[u], c), chunk_rows(r[u], c)], axis=0),
                                 s_in[c, h // 2]) for c in range(n_blk)]
                xs[u] = jnp.concatenate([p[:chunk] for p in parts], axis=0)
                ys[u] = jnp.concatenate([p[chunk:] for p in parts], axis=0)

        x = {u: xs[u] + _dot(l_ak[u], v[u]) for u in units}

        t_inv = {u: eye + l_ab[u] for u in units}
        n = 2
        if n < chunk:
            pw = {u: bf(l_ab[u]) for u in units}
            pw = {u: bf(_dot(pw[u], pw[u])) for u in units}
        while n < chunk:
            if 2 * n < chunk:
                both = {u: _dot(jnp.concatenate([bf(t_inv[u]), pw[u]], axis=0), pw[u]) for u in units}
                t_inv = {u: t_inv[u] + both[u][:rows] for u in units}
                pw = {u: bf(both[u][rows:]) for u in units}
            else:
                t_inv = {u: t_inv[u] + _dot(bf(t_inv[u]), pw[u]) for u in units}
            n *= 2

        uu = {u: _dot(bf(t_inv[u]), bf(x[u])) for u in units}
        uv = {u: jnp.concatenate([bf(uu[u]), v[u]], axis=0) for u in units}
        for u in units:
            q, h = u
            y_ref[q, :, lanes[h]] = ys[u] + _dot(l_r[u], uv[u])

        if n_blk == 1:
            for u in units:
                q, h = u
                upd = _dot_tn(uv[u], bk[u])
                s_ref[q, h] = (s_in[q, h] + upd) * wc_ref[q, 0][:, lanes[h]]
        else:
            def update(u, c):
                uv_c = jnp.concatenate([chunk_rows(uu[u], c), chunk_rows(v32[u], c)], axis=0)
                bk_c = jnp.concatenate([chunk_rows(b[u], c), chunk_rows(k[u], c)], axis=0)
                return _dot_tn(uv_c, bk_c)

            for q, h in units:
                if h % 2 == 0:
                    assert (q, h + 1) in units
                    hp = h // 2
                    for c in range(n_blk):
                        s_new = s_in[c, hp] + update((q, h), c) + update((q, h + 1), c)
                        s_ref[c, hp] = s_new * wc_ref[q, c][:, 2 * HEAD_DIM * hp:2 * HEAD_DIM * (hp + 1)]

    all_units = [(q, h) for h in range(N_HEADS) for q in range(n_par)]
    for g0 in range(0, len(all_units), group):
        unit_group(all_units[g0:g0 + group])


def _scan(ah, rh, bh, kh, v, wc, s0, *, n_seq, seq_len, n_par, rows, chunk, per_chunk_state, group,
          name, cast=()):
    t = n_seq * seq_len
    n_blk = rows // chunk
    if per_chunk_state:
        assert seq_len == chunk and n_par == 1
        lead = 1
        grid = (t // rows,)
        rmap = lambda i: (0, i, 0)
        cmap = lambda i: (0, i, 0, 0)
        smap = lambda i: (i, 0, 0, 0)
        s0map = smap
        state_block = (n_blk, N_HEADS // 2, HEAD_DIM, 2 * HEAD_DIM)
        s0_block = state_block
        wc_block = (1, n_blk, 1, G)
        sem = ("parallel",)
    else:
        assert rows == chunk and s0.shape[0] == 1 and n_seq % n_par == 0
        lead = n_seq
        grid = (n_seq // n_par, seq_len // chunk)
        rmap = lambda s, c: (s, c, 0)
        cmap = lambda s, c: (s, c, 0, 0)
        smap = lambda s, c: (s, 0, 0, 0)
        s0map = lambda s, c: (0, 0, 0, 0)
        state_block = (n_par, N_HEADS, HEAD_DIM, HEAD_DIM)
        s0_block = (1, N_HEADS, HEAD_DIM, HEAD_DIM)
        wc_block = (n_par, 1, 1, G)
        sem = ("parallel", "arbitrary")
    rows3 = lambda x: x.reshape(lead, t // lead, G)
    row_spec = pl.BlockSpec((n_par, rows, G), rmap)
    n_steps = 1
    for extent in grid:
        n_steps *= extent
    step = (lambda i: i) if len(grid) == 1 else (lambda s, c: s * grid[1] + c)
    cast_specs = []
    for w in cast:
        assert w.shape[0] % (16 * n_steps) == 0
        cast_specs.append(pl.BlockSpec((w.shape[0] // n_steps, w.shape[1]), lambda *g: (step(*g), 0)))
    y, s_out, *narrowed = pl.pallas_call(
        functools.partial(_scan_kernel, n_par=n_par, rows=rows, chunk=chunk,
                          per_chunk_state=per_chunk_state, group=group, n_cast=len(cast)),
        grid=grid,
        in_specs=([row_spec] * 5 + [pl.BlockSpec(wc_block, cmap), pl.BlockSpec(s0_block, s0map)]
                  + cast_specs),
        out_specs=[row_spec, pl.BlockSpec(state_block, smap)] + cast_specs,
        out_shape=[jax.ShapeDtypeStruct((lead, t // lead, G), F32),
                   jax.ShapeDtypeStruct((n_seq,) + state_block[1:], F32)]
                  + [jax.ShapeDtypeStruct(w.shape, BF16) for w in cast],
        compiler_params=pltpu.CompilerParams(dimension_semantics=sem, vmem_limit_bytes=VMEM_LIMIT),
        name=name,
    )(rows3(ah), rows3(rh), rows3(bh), rows3(kh), rows3(v), wc.reshape(lead, -1, 1, G), s0, *cast)
    return (y.reshape(t, G), s_out, *narrowed)


def _mix_kernel(*refs, segs):
    n = len(segs)
    row_refs = [refs[4 * s:4 * s + 4] for s in range(n)]
    x_ref, lnw_ref, lnb_ref, bd_ref, wo_ref, o_ref = refs[4 * n:]
    i = pl.program_id(0)

    def body(y_ref, bonus_ref, g_ref, conv_ref):
        bd = bd_ref[...]
        y = y_ref[...]
        mu = _head_sum(y, bd) * (1.0 / HEAD_DIM)
        d = y - mu
        var = _head_sum(d * d, bd) * (1.0 / HEAD_DIM)
        yn = d * lax.rsqrt(var + GN_EPS) * lnw_ref[...] + lnb_ref[...]
        rw = ((yn + bonus_ref[...]) * g_ref[...]).astype(BF16)
        mix = _dot(rw, wo_ref[0:G, :]) + _dot(conv_ref[...], wo_ref[G:2 * G, :])
        o_ref[...] = x_ref[...] + mix

    for rr, (off, cnt) in zip(row_refs, segs):
        @pl.when((i >= off) & (i < off + cnt))
        def _(rr=rr):
            body(*rr)


def _mix(row_groups, x1, lnw, lnb, bd, wo):
    segs, n_tiles = _segments([grp[0] for grp in row_groups], TM_MIX)
    cmap = lambda i: (0, 0)
    in_specs, args = [], []
    for grp, (off, cnt) in zip(row_groups, segs):
        smap = lambda i, off=off, cnt=cnt: (jnp.clip(i - off, 0, cnt - 1), 0)
        in_specs += [pl.BlockSpec((TM_MIX, G), smap)] * 4
        args += list(grp)
    in_specs += [pl.BlockSpec((TM_MIX, D_MODEL), lambda i: (i, 0)),
                 pl.BlockSpec((1, G), cmap), pl.BlockSpec((1, G), cmap),
                 pl.BlockSpec(bd.shape, cmap), pl.BlockSpec((D_MODEL, D_MODEL), cmap)]
    return pl.pallas_call(
        functools.partial(_mix_kernel, segs=segs),
        grid=(n_tiles,), in_specs=in_specs,
        out_specs=pl.BlockSpec((TM_MIX, D_MODEL), lambda i: (i, 0)),
        out_shape=jax.ShapeDtypeStruct((n_tiles * TM_MIX, D_MODEL), F32),
        compiler_params=pltpu.CompilerParams(
            dimension_semantics=("parallel",), vmem_limit_bytes=VMEM_LIMIT),
        name="mix",
    )(*args, x1, lnw, lnb, bd, wo)


def _pack_rwkv(a):
    r, wd, k, v, ad, gd = jnp.split(a, RWKV_SPLITS, axis=-1)
    zw = jnp.zeros(a.shape[:-1] + (LORA_PAD - W_LORA,), a.dtype)
    za = jnp.zeros(a.shape[:-1] + (LORA_PAD - A_LORA,), a.dtype)
    return jnp.concatenate([r, k, v, wd, zw, ad, za, gd], axis=-1)


def _unpack_rwkv(p):
    return jnp.concatenate([
        p[..., 0:G], p[..., OFF_WD:OFF_WD + W_LORA], p[..., G:2 * G], p[..., 2 * G:3 * G],
        p[..., OFF_AD:OFF_AD + A_LORA], p[..., OFF_GD:OFF_GD + G_LORA]], axis=-1)


def _block_tri(n, chunk):
    i = jnp.arange(n)
    return ((i[:, None] // chunk == i[None, :] // chunk) & (i[:, None] >= i[None, :])).astype(BF16)


def kernel(x_prompt, x_sample, state_wkv, state_shift, state_conv, meta_tokens, g_ffn1, ffn1_gate, ffn1_up, ffn1_down, g_mix, w_in, mu_shift, w0, w_lora_w, a0, w_lora_a, w_lora_g, k_k, k_a, r_k, ln_x_w, ln_x_b, conv_w, w_out, g_ffn2, ffn2_gate, ffn2_up, ffn2_down, g_final):
    assert g_ffn1.shape[0] == 1, "single layer"
    nb, seq, _ = x_prompt.shape
    db, dseq, _ = x_sample.shape
    assert dseq == C_SAMPLE and N_META <= C_PROMPT and seq % ROWS_PREP == 0
    tp, ts = nb * seq, db * dseq
    assert tp % TM == 0 and ts % TM == 0

    row = lambda a: a.reshape(1, -1).astype(F32)
    pad_rows = lambda w, n: jnp.concatenate([w, jnp.zeros((n - w.shape[0], w.shape[1]), w.dtype)], axis=0)
    hid = jnp.arange(HEAD_BLOCK) // HEAD_DIM
    bd = (hid[:, None] == hid[None, :]).astype(BF16)
    prep_w = (_pack_rwkv(mu_shift[0])[None], row(w0[0]), pad_rows(w_lora_w[0], LORA_PAD).astype(BF16),
              row(a0[0]), pad_rows(w_lora_a[0], LORA_PAD).astype(BF16), w_lora_g[0].astype(BF16),
              row(k_k[0]), row(k_a[0]), row(r_k[0]), conv_w[0].astype(F32), bd)

    x_meta = jnp.concatenate([jnp.zeros((C_PROMPT - N_META, D_MODEL), F32), meta_tokens.astype(F32)], axis=0)
    x1_meta, wg1, wu1, wd1 = _ffn_narrow(x_meta, row(g_ffn1[0]), ffn1_gate[0], ffn1_up[0], ffn1_down[0],
                                         name="ffn1_meta")
    x1, w_in_p = _ffn_two_stage([x_prompt.reshape(tp, D_MODEL), x_sample.reshape(ts, D_MODEL)],
                                row(g_ffn1[0]), wg1, wu1, wd1, None, [tp + ts], name="ffn1",
                                pack_src=jnp.transpose(w_in[0]))
    x1_meta = jnp.concatenate([x1_meta, jnp.zeros((TM - C_PROMPT, D_MODEL), F32)], axis=0)
    proj = _proj([x1, x1_meta], row(g_mix[0]), w_in_p, name="proj")
    meta_row0 = tp + ts

    tri_p = _block_tri(ROWS_PREP, C_PROMPT)
    zeros_state = jnp.zeros((1, N_HEADS, HEAD_DIM, HEAD_DIM), F32)
    (ah, rh, bh, kh, vv, wc, _, _, _, utail_m, ptail_m) = _prep(
        proj, prep_w, tri_p[:C_PROMPT, :C_PROMPT], n_seq=1, seq_len=C_PROMPT, rows=C_PROMPT,
        chunk=C_PROMPT, row_block_offset=meta_row0 // C_PROMPT,
        extra=(jnp.zeros((8, RW), F32), jnp.zeros((8, G), F32)),
        sample=False, name="prep_meta")
    _, wkv_m = _scan(ah, rh, bh, kh, vv, wc, zeros_state, n_seq=1, seq_len=C_PROMPT, n_par=1,
                     rows=C_PROMPT, chunk=C_PROMPT, per_chunk_state=False, group=N_HEADS,
                     name="scan_meta")

    (ah, rh, bh, kh, vv, wc, g_p, bonus_p, conv_p, utail_p, ptail_p) = _prep(
        proj, prep_w, tri_p, n_seq=nb, seq_len=seq, rows=ROWS_PREP, chunk=C_PROMPT,
        row_block_offset=0, extra=(ptail_m[0], utail_m[0]), sample=False, name="prep_prompt")
    y_p, wkv_p, wg2, wu2, wd2, wo = _scan(
        ah, rh, bh, kh, vv, wc, wkv_m, n_seq=nb, seq_len=seq, n_par=SCAN_PAR, rows=C_PROMPT,
        chunk=C_PROMPT, per_chunk_state=False, group=SCAN_PAR * N_HEADS // 2, name="scan_prompt",
        cast=(ffn2_gate[0], ffn2_up[0], ffn2_down[0], w_out[0]))

    prev = _pack_rwkv(state_shift[0])
    up1 = state_conv[0][:, 1]
    up2 = state_conv[0][:, 0]
    (ah, rh, bh, kh, vv, wc, g_s, bonus_s, conv_s, u_s, plast_s) = _prep(
        proj, prep_w, _block_tri(ROWS_PREP, C_SAMPLE), n_seq=db, seq_len=dseq, rows=ROWS_PREP,
        chunk=C_SAMPLE, row_block_offset=tp // ROWS_PREP, extra=(prev, up1, up2), sample=True,
        name="prep_sample")
    pair = (db, N_HEADS // 2, 2, HEAD_DIM, HEAD_DIM)
    s0_pairs = jnp.transpose(state_wkv[0].reshape(pair), (0, 1, 3, 2, 4)).reshape(
        db, N_HEADS // 2, HEAD_DIM, 2 * HEAD_DIM)
    y_s, wkv_s = _scan(ah, rh, bh, kh, vv, wc, s0_pairs, n_seq=db, seq_len=dseq, n_par=1,
                       rows=ROWS_SCAN_SAMPLE, chunk=C_SAMPLE, per_chunk_state=True, group=16,
                       name="scan_sample")

    x2 = _mix([(y_p, bonus_p, g_p, conv_p), (y_s, bonus_s, g_s, conv_s)], x1,
              row(ln_x_w[0]), row(ln_x_b[0]), bd, wo)
    y_prompt, y_sample = _ffn_two_stage([x2], row(g_ffn2[0]), wg2, wu2, wd2, row(g_final), [tp, ts],
                                        name="ffn2")

    shift_p = _unpack_rwkv(ptail_p.reshape(nb, seq // ROWS_PREP, 8, RW)[:, -1, 7, :])
    conv_state_p = utail_p.reshape(nb, seq // ROWS_PREP, 8, G)[:, -1, 6:, :]
    shift_s = _unpack_rwkv(plast_s)
    conv_state_s = u_s.reshape(db, dseq, G)[:, -2:, :]
    wkv_s_out = jnp.transpose(wkv_s.reshape(db, N_HEADS // 2, HEAD_DIM, 2, HEAD_DIM),
                              (0, 1, 3, 2, 4)).reshape(db, N_HEADS, HEAD_DIM, HEAD_DIM)
    return (y_prompt.reshape(nb, seq, D_MODEL), y_sample.reshape(db, dseq, D_MODEL),
            wkv_p[None].astype(state_wkv.dtype), shift_p[None].astype(state_shift.dtype),
            conv_state_p[None].astype(state_conv.dtype),
            wkv_s_out[None].astype(state_wkv.dtype), shift_s[None].astype(state_shift.dtype),
            conv_state_s[None].astype(state_conv.dtype))
```

```python
import functools

import jax
import jax.numpy as jnp
from jax import lax
from jax.experimental import pallas as pl
from jax.experimental.pallas import tpu as pltpu

F32 = jnp.float32
BF16 = jnp.bfloat16

D_MODEL = 2048
D_FF = 5632
N_META = 16
G = 1024
HEAD_DIM = 64
N_HEADS = G // HEAD_DIM
MXU_N = 256
HEAD_BLOCK = MXU_N
W_LORA = 96
A_LORA = 96
G_LORA = 256
LORA_PAD = 128
RWKV_PROJ = 3 * G + W_LORA + A_LORA + G_LORA
RWKV_SPLITS = (G, G + W_LORA, 2 * G + W_LORA, 3 * G + W_LORA, 3 * G + W_LORA + A_LORA)
RW = 3 * G + 2 * LORA_PAD + G_LORA
PW = RW + 3 * G
OFF_WD, OFF_AD, OFF_GD = 3 * G, 3 * G + LORA_PAD, 3 * G + 2 * LORA_PAD
RMS_EPS = 1e-6
GN_EPS = 64e-5
EXP_M05_LOG2E = 0.6065306597126334 * 1.4426950408889634

C_PROMPT = 64
C_SAMPLE = 8
TM = 512
TF = 512
TN = PW // 2
assert TN % MXU_N == 0
TM_FFN_UP = 512
TM_FFN_DOWN = 256
TM_MIX = 256
ROWS_PREP = 256
ROWS_SCAN_SAMPLE = 128
SCAN_PAR = 4
VMEM_LIMIT = 56 * 1024 * 1024


def _dot(a, b):
    return jnp.dot(a, b, preferred_element_type=F32)


def _dot_nt(a, b):
    return lax.dot_general(a, b, (((1,), (1,)), ((), ())), preferred_element_type=F32)


def _dot_tn(a, b):
    return lax.dot_general(a, b, (((0,), (0,)), ((), ())), preferred_element_type=F32)


def _split(a):
    hi = a.astype(BF16)
    lo = (a - hi.astype(F32)).astype(BF16)
    return hi, lo


def _head_sum(x, bd):
    xb = x.astype(BF16)
    wb = bd.shape[0]
    return jnp.concatenate([_dot(xb[:, c:c + wb], bd) for c in range(0, x.shape[1], wb)], axis=1)


def _dot_split_rhs(a_bf16, b):
    hi, lo = _split(b)
    return _dot(a_bf16, hi) + _dot(a_bf16, lo)


def _rms(x, g):
    return x * lax.rsqrt(jnp.mean(x * x, axis=-1, keepdims=True) + RMS_EPS) * g


def _segments(arrays, tile):
    segs, off = [], 0
    for arr in arrays:
        n = arr.shape[0] // tile
        assert n * tile == arr.shape[0]
        segs.append((off, n))
        off += n
    return segs, off


def _overlaps(in_segs, out_segs):
    for a, (ao, an) in enumerate(in_segs):
        for b, (bo, bn) in enumerate(out_segs):
            lo, hi = max(ao, bo), min(ao + an, bo + bn)
            if lo < hi:
                yield a, b, lo, hi


def _ffn_narrow_kernel(x_ref, g_ref, wg_ref, wu_ref, wd_ref, o_ref, wg_o, wu_o, wd_o, h_ref, acc_ref):
    j = pl.program_id(0)

    @pl.when(j == 0)
    def _():
        h_ref[...] = _rms(x_ref[...], g_ref[...]).astype(BF16)
        acc_ref[...] = jnp.zeros_like(acc_ref)

    wg, wu, wd = wg_ref[...].astype(BF16), wu_ref[...].astype(BF16), wd_ref[...].astype(BF16)
    wg_o[...], wu_o[...], wd_o[...] = wg, wu, wd
    h = h_ref[...]
    gate = _dot(h, wg)
    up = _dot(h, wu)
    act = (gate * jax.nn.sigmoid(gate) * up).astype(BF16)
    acc_ref[...] += _dot(act, wd)

    @pl.when(j == pl.num_programs(0) - 1)
    def _():
        o_ref[...] = x_ref[...] + 0.5 * acc_ref[...]


def _ffn_narrow(x, g, wg, wu, wd, *, name):
    rows = x.shape[0]
    whole = pl.BlockSpec((rows, D_MODEL), lambda j: (0, 0))
    w_specs = [pl.BlockSpec((D_MODEL, TF), lambda j: (0, j)),
               pl.BlockSpec((D_MODEL, TF), lambda j: (0, j)),
               pl.BlockSpec((TF, D_MODEL), lambda j: (j, 0))]
    return pl.pallas_call(
        _ffn_narrow_kernel,
        grid=(D_FF // TF,),
        in_specs=[whole, pl.BlockSpec((1, D_MODEL), lambda j: (0, 0))] + w_specs,
        out_specs=[whole] + w_specs,
        out_shape=[jax.ShapeDtypeStruct((rows, D_MODEL), F32)]
                  + [jax.ShapeDtypeStruct(w.shape, BF16) for w in (wg, wu, wd)],
        scratch_shapes=[pltpu.VMEM((rows, D_MODEL), BF16), pltpu.VMEM((rows, D_MODEL), F32)],
        compiler_params=pltpu.CompilerParams(
            dimension_semantics=("arbitrary",), vmem_limit_bytes=VMEM_LIMIT),
        name=name,
    )(x, g, wg, wu, wd)


def _ffn_up_kernel(*refs, in_segs):
    n_in = len(in_segs)
    x_refs = refs[:n_in]
    g_ref, wg_ref, wu_ref, a_ref = refs[n_in:]
    i = pl.program_id(1)
    for x_ref, (off, n) in zip(x_refs, in_segs):
        @pl.when((i >= off) & (i < off + n))
        def _(x_ref=x_ref):
            h = _rms(x_ref[...], g_ref[...]).astype(BF16)
            for c in range(0, a_ref.shape[1], MXU_N):
                gate = _dot(h, wg_ref[:, c:c + MXU_N])
                up = _dot(h, wu_ref[:, c:c + MXU_N])
                a_ref[:, c:c + MXU_N] = (gate * jax.nn.sigmoid(gate) * up).astype(BF16)


def _ffn_down_kernel(*refs, in_segs, out_segs, final_norm, pack_steps):
    n_in, n_out = len(in_segs), len(out_segs)
    a_ref = refs[0]
    x_refs = refs[1:1 + n_in]
    wd_ref = refs[1 + n_in]
    rest = refs[2 + n_in:]
    if final_norm:
        gf_ref, rest = rest[0], rest[1:]
    if pack_steps:
        wt_ref, rest = rest[0], rest[1:]
    o_refs = rest[:n_out]
    i = pl.program_id(0)
    if pack_steps:
        @pl.when(i < pack_steps)
        def _():
            _pack_w_in_kernel(wt_ref, rest[n_out])

    for a, b, lo, hi in _overlaps(in_segs, out_segs):
        @pl.when((i >= lo) & (i < hi))
        def _(x_ref=x_refs[a], o_ref=o_refs[b]):
            xo = x_ref[...] + 0.5 * _dot(a_ref[...], wd_ref[...])
            if final_norm:
                xo = _rms(xo, gf_ref[...])
            o_ref[...] = xo


def _ffn_two_stage(xs, g, wg, wu, wd, g_final, out_rows, *, name, pack_src=None):
    final_norm = g_final is not None
    tm_up, tn = TM_FFN_UP, D_FF // 2
    up_segs, up_tiles = _segments(xs, tm_up)
    seg_ji = lambda s: pl.BlockSpec((tm_up, D_MODEL), lambda j, i, s=s: (jnp.clip(i - s[0], 0, s[1] - 1), 0))
    half = pl.BlockSpec((D_MODEL, tn), lambda j, i: (0, j), pipeline_mode=pl.Buffered(1))
    act = pl.pallas_call(
        functools.partial(_ffn_up_kernel, in_segs=up_segs),
        grid=(D_FF // tn, up_tiles),
        in_specs=[seg_ji(s) for s in up_segs] + [pl.BlockSpec((1, D_MODEL), lambda j, i: (0, 0)), half, half],
        out_specs=pl.BlockSpec((tm_up, tn), lambda j, i: (i, j)),
        out_shape=jax.ShapeDtypeStruct((up_tiles * tm_up, D_FF), BF16),
        compiler_params=pltpu.CompilerParams(
            dimension_semantics=("arbitrary", "arbitrary"), vmem_limit_bytes=VMEM_LIMIT),
        name=name + "_up",
    )(*xs, g, wg, wu)

    tm = TM_FFN_DOWN
    in_segs, n_tiles = _segments(xs, tm)
    out_shape = [jax.ShapeDtypeStruct((n, D_MODEL), F32) for n in out_rows]
    out_segs, n_out_tiles = _segments(out_shape, tm)
    assert n_out_tiles == n_tiles and n_tiles * tm == up_tiles * tm_up
    seg_i = lambda s: pl.BlockSpec((tm, D_MODEL), lambda i, s=s: (jnp.clip(i - s[0], 0, s[1] - 1), 0))
    in_specs = ([pl.BlockSpec((tm, D_FF), lambda i: (i, 0))] + [seg_i(s) for s in in_segs]
                + [pl.BlockSpec((D_FF, D_MODEL), lambda i: (0, 0), pipeline_mode=pl.Buffered(1))])
    args = [act] + list(xs) + [wd]
    if final_norm:
        in_specs.append(pl.BlockSpec((1, D_MODEL), lambda i: (0, 0)))
        args.append(g_final)
    out_specs = [seg_i(s) for s in out_segs]
    pack_steps = 0
    if pack_src is not None:
        lane_blk = 128
        pack_steps = D_MODEL // lane_blk
        assert pack_steps <= n_tiles
        blk = lambda i: (0, jnp.minimum(i, pack_steps - 1))
        in_specs.append(pl.BlockSpec((pack_src.shape[0], lane_blk), blk))
        args.append(pack_src)
        out_specs.append(pl.BlockSpec((PW, lane_blk), blk))
        out_shape = out_shape + [jax.ShapeDtypeStruct((PW, D_MODEL), BF16)]
    return pl.pallas_call(
        functools.partial(_ffn_down_kernel, in_segs=in_segs, out_segs=out_segs, final_norm=final_norm,
                          pack_steps=pack_steps),
        grid=(n_tiles,),
        in_specs=in_specs,
        out_specs=out_specs,
        out_shape=out_shape,
        compiler_params=pltpu.CompilerParams(
            dimension_semantics=("arbitrary",), vmem_limit_bytes=VMEM_LIMIT),
        name=name + "_down",
    )(*args)


_PACK_MOVES = (
    (0, 0, G),
    (G, RWKV_SPLITS[1], G),
    (2 * G, RWKV_SPLITS[2], G),
    (OFF_WD, RWKV_SPLITS[0], W_LORA),
    (OFF_AD, RWKV_SPLITS[3], A_LORA),
    (OFF_GD, RWKV_SPLITS[4], G_LORA),
    (RW, RWKV_PROJ, 3 * G),
)


def _pack_w_in_kernel(w_ref, o_ref):
    for dst, src, n in _PACK_MOVES:
        o_ref[dst:dst + n, :] = w_ref[src:src + n, :].astype(BF16)
    for lo, hi in ((OFF_WD + W_LORA, OFF_AD), (OFF_AD + A_LORA, OFF_GD)):
        o_ref[lo:hi, :] = jnp.zeros((hi - lo, o_ref.shape[1]), BF16)


def _proj_kernel(*refs, in_segs):
    n_in = len(in_segs)
    x_refs = refs[:n_in]
    g_ref, w_ref, o_ref = refs[n_in:]
    i = pl.program_id(1)
    for x_ref, (off, n) in zip(x_refs, in_segs):
        @pl.when((i >= off) & (i < off + n))
        def _(x_ref=x_ref):
            h = _rms(x_ref[...], g_ref[...]).astype(BF16)
            o_ref[...] = _dot_nt(h, w_ref[...])


def _proj(xs, g, w, *, name):
    in_segs, n_tiles = _segments(xs, TM)
    seg_ji = lambda s: pl.BlockSpec((TM, D_MODEL), lambda j, i, s=s: (jnp.clip(i - s[0], 0, s[1] - 1), 0),
                                    pipeline_mode=pl.Buffered(1 if s[1] == 1 else 2))
    return pl.pallas_call(
        functools.partial(_proj_kernel, in_segs=in_segs),
        grid=(PW // TN, n_tiles),
        in_specs=[seg_ji(s) for s in in_segs] + [
            pl.BlockSpec((1, D_MODEL), lambda j, i: (0, 0)),
            pl.BlockSpec((TN, D_MODEL), lambda j, i: (j, 0)),
        ],
        out_specs=pl.BlockSpec((TM, TN), lambda j, i: (i, j)),
        out_shape=jax.ShapeDtypeStruct((n_tiles * TM, PW), F32),
        compiler_params=pltpu.CompilerParams(
            dimension_semantics=("parallel", "parallel"), vmem_limit_bytes=VMEM_LIMIT),
        name=name,
    )(*xs, g, w)


def _prep_kernel(*refs, rows, chunk, sample):
    if sample:
        (p_ref, prev_ref, up1_ref, up2_ref, mu_ref, w0_ref, ww_ref, a0_ref, wa_ref, wgl_ref,
         kk_ref, ka_ref, rk_ref, cw_ref, bd_ref, tri_ref,
         ah_ref, rh_ref, bh_ref, kh_ref, v_ref, wc_ref, g_ref, bonus_ref, conv_ref, u_ref,
         plast_ref) = refs
    else:
        (p_ref, cp0_ref, cu0_ref, mu_ref, w0_ref, ww_ref, a0_ref, wa_ref, wgl_ref,
         kk_ref, ka_ref, rk_ref, cw_ref, bd_ref, tri_ref,
         ah_ref, rh_ref, bh_ref, kh_ref, v_ref, wc_ref, g_ref, bonus_ref, conv_ref, u_ref,
         ptail_ref, carry_p, carry_u) = refs

        @pl.when(pl.program_id(1) == 0)
        def _():
            carry_p[...] = cp0_ref[...]
            carry_u[...] = cu0_ref[...]

    def row_ids(width):
        r = lax.broadcasted_iota(jnp.int32, (rows, width), 0)
        return (r & 7) if sample else r

    def per_sequence(x):
        n, w = x.shape
        return jnp.broadcast_to(x[:, None, :], (n, 8, w)).reshape(n * 8, w)

    def shifted_mix(lo, hi):
        p = p_ref[:, lo:hi]
        rolled = pltpu.roll(p, 1, 0)
        if sample:
            sh = jnp.where(row_ids(hi - lo) == 0, per_sequence(prev_ref[:, lo:hi]), rolled)
        else:
            sh = jnp.where(row_ids(hi - lo) == 0, carry_p[7:8, lo:hi], rolled)
        return p + (sh - p) * mu_ref[:, lo:hi]

    r = shifted_mix(0, G)
    k = shifted_mix(G, 2 * G)
    v = shifted_mix(2 * G, 3 * G)
    wd = shifted_mix(OFF_WD, OFF_WD + LORA_PAD)
    ad = shifted_mix(OFF_AD, OFF_AD + LORA_PAD)
    gd = shifted_mix(OFF_GD, OFF_GD + G_LORA)

    z = w0_ref[...] + _dot(jnp.tanh(wd).astype(BF16), ww_ref[...])
    lw = -EXP_M05_LOG2E * jax.nn.sigmoid(z)
    a = jax.nn.sigmoid(a0_ref[...] + _dot(ad.astype(BF16), wa_ref[...]))
    g_ref[...] = _dot(jax.nn.sigmoid(gd).astype(BF16), wgl_ref[...]).astype(BF16)

    bd = bd_ref[...]
    kk = k * kk_ref[...]
    norm = jnp.sqrt(_head_sum(kk * kk, bd))
    kk = kk / jnp.maximum(norm, 1e-12)
    km = k * (1.0 + (a - 1.0) * ka_ref[...])
    bonus_ref[...] = _head_sum(r * km * rk_ref[...], bd) * v

    cum = _dot_split_rhs(tri_ref[...], lw)
    e_cum = jnp.exp2(cum)
    e_inv = jnp.exp2(-cum)
    ah_ref[...] = (-kk * jnp.exp2(cum - lw)).astype(BF16)
    rh_ref[...] = (r * e_cum).astype(BF16)
    bh_ref[...] = (kk * a * e_inv).astype(BF16)
    kh_ref[...] = (km * e_inv).astype(BF16)
    v_ref[...] = v.astype(BF16)
    for c in range(rows // chunk):
        wc_ref[c] = e_cum[(c + 1) * chunk - 1:(c + 1) * chunk, :]

    bg = p_ref[:, RW:RW + G]
    u = p_ref[:, RW + G:RW + 2 * G] * p_ref[:, RW + 2 * G:RW + 3 * G]
    u1 = pltpu.roll(u, 1, 0)
    u2 = pltpu.roll(u, 2, 0)
    rid = row_ids(G)
    if sample:
        s1 = per_sequence(up1_ref[...])
        um1 = jnp.where(rid == 0, s1, u1)
        um2 = jnp.where(rid == 0, per_sequence(up2_ref[...]), jnp.where(rid == 1, s1, u2))
        u_ref[...] = u
        plast_ref[...] = p_ref[:, 0:RW].reshape(rows // 8, 8, RW)[:, 7, :]
    else:
        um1 = jnp.where(rid == 0, carry_u[7:8, :], u1)
        um2 = jnp.where(rid == 0, carry_u[6:7, :], jnp.where(rid == 1, carry_u[7:8, :], u2))
        u_ref[0] = u[rows - 8:, :]
        ptail_ref[0] = p_ref[rows - 8:rows, 0:RW]
    conv = cw_ref[0:1, :] * um2 + cw_ref[1:2, :] * um1 + cw_ref[2:3, :] * u
    conv_ref[...] = (bg * conv).astype(BF16)

    if not sample:
        carry_p[...] = p_ref[rows - 8:rows, 0:RW]
        carry_u[...] = u[rows - 8:, :]


def _prep(proj, weights, tri, *, n_seq, seq_len, rows, chunk, row_block_offset, extra, sample, name):
    t = n_seq * seq_len
    n_chunks = t // chunk
    cpt = rows // chunk
    if sample:
        grid = (t // rows,)
        rmap = lambda i: (i + row_block_offset, 0)
        omap = lambda i: (i, 0)
        omap3 = lambda i: (i, 0, 0)
        cmap = lambda i: (0, 0)
        sem = ("parallel",)
    else:
        tiles = seq_len // rows
        grid = (n_seq, tiles)
        rmap = lambda b, j: (b * tiles + j + row_block_offset, 0)
        omap = lambda b, j: (b * tiles + j, 0)
        omap3 = lambda b, j: (b * tiles + j, 0, 0)
        cmap = lambda b, j: (0, 0)
        sem = ("parallel", "arbitrary")

    const = lambda arr: pl.BlockSpec(arr.shape, cmap)
    in_specs = [pl.BlockSpec((rows, PW), rmap)]
    if sample:
        in_specs += [pl.BlockSpec((rows // 8, RW), omap), pl.BlockSpec((rows // 8, G), omap),
                     pl.BlockSpec((rows // 8, G), omap)]
    else:
        in_specs += [const(e) for e in extra]
    in_specs += [const(w) for w in weights] + [const(tri)]
    args = [proj] + list(extra) + list(weights) + [tri]

    row_out = jax.ShapeDtypeStruct((t, G), F32)
    row_spec = pl.BlockSpec((rows, G), omap)
    row_bf16 = jax.ShapeDtypeStruct((t, G), BF16)
    out_shape = [row_bf16] * 5 + [jax.ShapeDtypeStruct((n_chunks, 1, G), F32), row_bf16, row_out,
                                  row_bf16]
    out_specs = [row_spec] * 5 + [pl.BlockSpec((cpt, 1, G), omap3), row_spec, row_spec, row_spec]
    if sample:
        out_shape += [row_out, jax.ShapeDtypeStruct((t // 8, RW), F32)]
        out_specs += [row_spec, pl.BlockSpec((rows // 8, RW), omap)]
        scratch = []
    else:
        out_shape += [jax.ShapeDtypeStruct((t // rows, 8, G), F32),
                      jax.ShapeDtypeStruct((t // rows, 8, RW), F32)]
        out_specs += [pl.BlockSpec((1, 8, G), omap3), pl.BlockSpec((1, 8, RW), omap3)]
        scratch = [pltpu.VMEM((8, RW), F32), pltpu.VMEM((8, G), F32)]
    return pl.pallas_call(
        functools.partial(_prep_kernel, rows=rows, chunk=chunk, sample=sample),
        grid=grid, in_specs=in_specs, out_specs=out_specs, out_shape=out_shape,
        scratch_shapes=scratch,
        compiler_params=pltpu.CompilerParams(dimension_semantics=sem, vmem_limit_bytes=VMEM_LIMIT),
        name=name,
    )(*args)


def _scan_kernel(*refs, n_par, rows, chunk, per_chunk_state, group, n_cast):
    ah_ref, rh_ref, bh_ref, kh_ref, v_ref, wc_ref, s0_ref = refs[:7]
    y_ref, s_ref = refs[7 + n_cast:9 + n_cast]
    for w_ref, o_ref in zip(refs[7:7 + n_cast], refs[9 + n_cast:]):
        o_ref[...] = w_ref[...].astype(BF16)

    if per_chunk_state:
        s_in = s0_ref
    else:
        s_in = s_ref

        @pl.when(pl.program_id(1) == 0)
        def _():
            for q in range(n_par):
                s_ref[q] = s0_ref[0]

    n_blk = rows // chunk
    lanes = [slice(h * HEAD_DIM, (h + 1) * HEAD_DIM) for h in range(N_HEADS)]
    bf = lambda x: x.astype(BF16)

    shift = chunk.bit_length() - 1
    rs = lax.broadcasted_iota(jnp.int32, (rows, rows), 0)
    cs = lax.broadcasted_iota(jnp.int32, (rows, rows), 1)
    same = (rs >> shift) == (cs >> shift)
    mask_strict = same & (rs > cs)
    ri = lax.broadcasted_iota(jnp.int32, (rows, 2 * rows), 0)
    ci = lax.broadcasted_iota(jnp.int32, (rows, 2 * rows), 1)
    cj = jnp.where(ci >= rows, ci - rows, ci)
    mask_r = ((ri >> shift) == (cj >> shift)) & (ri >= cj)
    eye = (rs == cs).astype(F32)

    def chunk_rows(x, c):
        return x[c * chunk:(c + 1) * chunk, :]

    pair_lane = lax.broadcasted_iota(jnp.int32, (rows, 2 * HEAD_DIM), 1) >> (HEAD_DIM.bit_length() - 1)

    def pair_masked(ref, q, h):
        hp, hh = divmod(h, 2)
        x2 = ref[q, :, 2 * HEAD_DIM * hp:2 * HEAD_DIM * (hp + 1)].astype(F32)
        return jnp.where(pair_lane == hh, x2, 0.0)

    def unit_group(units):
        per_unit = lambda f: {u: f(*u) for u in units}
        v = per_unit(lambda q, h: v_ref[q, :, lanes[h]])
        if n_blk == 1:
            a = per_unit(lambda q, h: ah_ref[q, :, lanes[h]])
            r = per_unit(lambda q, h: rh_ref[q, :, lanes[h]])
            b = per_unit(lambda q, h: bh_ref[q, :, lanes[h]])
            k = per_unit(lambda q, h: kh_ref[q, :, lanes[h]])
        else:
            a = per_unit(lambda q, h: pair_masked(ah_ref, q, h))
            r = per_unit(lambda q, h: pair_masked(rh_ref, q, h))
            b = per_unit(lambda q, h: pair_masked(bh_ref, q, h))
            k = per_unit(lambda q, h: pair_masked(kh_ref, q, h))
            v32 = {u: v[u].astype(F32) for u in units}
        ar = {u: bf(jnp.concatenate([a[u], r[u]], axis=0)) for u in units}
        bk = {u: bf(jnp.concatenate([b[u], k[u]], axis=0)) for u in units}

        bk_t = {u: jnp.transpose(bk[u]) for u in units}
        gram = {u: _dot(ar[u], bk_t[u]) for u in units}
        l_ab = {u: jnp.where(mask_strict, gram[u][:rows, :rows], 0.0) for u in units}
        l_ak = {u: bf(jnp.where(mask_strict, gram[u][:rows, rows:], 0.0)) for u in units}
        l_r = {u: bf(jnp.where(mask_r, gram[u][rows:, :], 0.0)) for u in units}

        xs, ys = {}, {}
        for u in units:
            q, h = u
            if n_blk == 1:
                st = _dot(ar[u], bf(s_in[q, h]))
                xs[u], ys[u] = st[:rows], st[rows:]
            else:
                parts = [_dot_nt(jnp.concatenate([chunk_rows(a[u], c), chunk_rows(r[u], c)], axis=0),
                                 s_in[c, h // 2]) for c in range(n_blk)]
                xs[u] = jnp.concatenate([p[:chunk] for p in parts], axis=0)
                ys[u] = jnp.concatenate([p[chunk:] for p in parts], axis=0)

        x = {u: xs[u] + _dot(l_ak[u], v[u]) for u in units}

        t_inv = {u: eye + l_ab[u] for u in units}
        n = 2
        if n < chunk:
            pw = {u: bf(l_ab[u]) for u in units}
            pw = {u: bf(_dot(pw[u], pw[u])) for u in units}
        while n < chunk:
            if 2 * n < chunk:
                both = {u: _dot(jnp.concatenate([bf(t_inv[u]), pw[u]], axis=0), pw[u]) for u in units}
                t_inv = {u: t_inv[u] + both[u][:rows] for u in units}
                pw = {u: bf(both[u][rows:]) for u in units}
            else:
                t_inv = {u: t_inv[u] + _dot(bf(t_inv[u]), pw[u]) for u in units}
            n *= 2

        uu = {u: _dot(bf(t_inv[u]), bf(x[u])) for u in units}
        uv = {u: jnp.concatenate([bf(uu[u]), v[u]], axis=0) for u in units}
        if n_blk == 1:
            both = {u: _dot(jnp.concatenate([l_r[u], bk_t[u]], axis=0), uv[u]) for u in units}
            for u in units:
                q, h = u
                y_ref[q, :, lanes[h]] = ys[u] + both[u][:rows]
            for u in units:
                q, h = u
                w_c = jnp.broadcast_to(wc_ref[q, 0][:, lanes[h]], (HEAD_DIM, HEAD_DIM))
                s_ref[q, h] = (s_in[q, h] + both[u][rows:]) * jnp.transpose(w_c)
        else:
            for u in units:
                q, h = u
                y_ref[q, :, lanes[h]] = ys[u] + _dot(l_r[u], uv[u])

            def update(u, c):
                uv_c = jnp.concatenate([chunk_rows(uu[u], c), chunk_rows(v32[u], c)], axis=0)
                bk_c = jnp.concatenate([chunk_rows(b[u], c), chunk_rows(k[u], c)], axis=0)
                return _dot_tn(uv_c, bk_c)

            for q, h in units:
                if h % 2 == 0:
                    assert (q, h + 1) in units
                    hp = h // 2
                    for c in range(n_blk):
                        s_new = s_in[c, hp] + update((q, h), c) + update((q, h + 1), c)
                        s_ref[c, hp] = s_new * wc_ref[q, c][:, 2 * HEAD_DIM * hp:2 * HEAD_DIM * (hp + 1)]

    all_units = [(q, h) for h in range(N_HEADS) for q in range(n_par)]
    for g0 in range(0, len(all_units), group):
        unit_group(all_units[g0:g0 + group])


def _scan(ah, rh, bh, kh, v, wc, s0, *, n_seq, seq_len, n_par, rows, chunk, per_chunk_state, group,
          name, cast=()):
    t = n_seq * seq_len
    n_blk = rows // chunk
    if per_chunk_state:
        assert seq_len == chunk and n_par == 1
        lead = 1
        grid = (t // rows,)
        rmap = lambda i: (0, i, 0)
        cmap = lambda i: (0, i, 0, 0)
        smap = lambda i: (i, 0, 0, 0)
        s0map = smap
        state_block = (n_blk, N_HEADS // 2, HEAD_DIM, 2 * HEAD_DIM)
        s0_block = state_block
        wc_block = (1, n_blk, 1, G)
        sem = ("parallel",)
    else:
        assert rows == chunk and s0.shape[0] == 1 and n_seq % n_par == 0
        lead = n_seq
        grid = (n_seq // n_par, seq_len // chunk)
        rmap = lambda s, c: (s, c, 0)
        cmap = lambda s, c: (s, c, 0, 0)
        smap = lambda s, c: (s, 0, 0, 0)
        s0map = lambda s, c: (0, 0, 0, 0)
        state_block = (n_par, N_HEADS, HEAD_DIM, HEAD_DIM)
        s0_block = (1, N_HEADS, HEAD_DIM, HEAD_DIM)
        wc_block = (n_par, 1, 1, G)
        sem = ("parallel", "arbitrary")
    rows3 = lambda x: x.reshape(lead, t // lead, G)
    row_spec = pl.BlockSpec((n_par, rows, G), rmap)
    n_steps = 1
    for extent in grid:
        n_steps *= extent
    step = (lambda i: i) if len(grid) == 1 else (lambda s, c: s * grid[1] + c)
    cast_specs = []
    for w in cast:
        assert w.shape[0] % (16 * n_steps) == 0
        cast_specs.append(pl.BlockSpec((w.shape[0] // n_steps, w.shape[1]), lambda *g: (step(*g), 0)))
    y, s_out, *narrowed = pl.pallas_call(
        functools.partial(_scan_kernel, n_par=n_par, rows=rows, chunk=chunk,
                          per_chunk_state=per_chunk_state, group=group, n_cast=len(cast)),
        grid=grid,
        in_specs=([row_spec] * 5 + [pl.BlockSpec(wc_block, cmap), pl.BlockSpec(s0_block, s0map)]
                  + cast_specs),
        out_specs=[row_spec, pl.BlockSpec(state_block, smap)] + cast_specs,
        out_shape=[jax.ShapeDtypeStruct((lead, t // lead, G), F32),
                   jax.ShapeDtypeStruct((n_seq,) + state_block[1:], F32)]
                  + [jax.ShapeDtypeStruct(w.shape, BF16) for w in cast],
        compiler_params=pltpu.CompilerParams(dimension_semantics=sem, vmem_limit_bytes=VMEM_LIMIT),
        name=name,
    )(rows3(ah), rows3(rh), rows3(bh), rows3(kh), rows3(v), wc.reshape(lead, -1, 1, G), s0, *cast)
    return (y.reshape(t, G), s_out, *narrowed)


def _mix_kernel(*refs, segs):
    n = len(segs)
    row_refs = [refs[4 * s:4 * s + 4] for s in range(n)]
    x_ref, lnw_ref, lnb_ref, bd_ref, wo_ref, o_ref = refs[4 * n:]
    i = pl.program_id(0)

    def body(y_ref, bonus_ref, g_ref, conv_ref):
        bd = bd_ref[...]
        y = y_ref[...]
        mu = _head_sum(y, bd) * (1.0 / HEAD_DIM)
        d = y - mu
        var = _head_sum(d * d, bd) * (1.0 / HEAD_DIM)
        yn = d * lax.rsqrt(var + GN_EPS) * lnw_ref[...] + lnb_ref[...]
        rw = ((yn + bonus_ref[...]) * g_ref[...]).astype(BF16)
        mix = _dot(rw, wo_ref[0:G, :]) + _dot(conv_ref[...], wo_ref[G:2 * G, :])
        o_ref[...] = x_ref[...] + mix

    for rr, (off, cnt) in zip(row_refs, segs):
        @pl.when((i >= off) & (i < off + cnt))
        def _(rr=rr):
            body(*rr)


def _mix(row_groups, x1, lnw, lnb, bd, wo):
    segs, n_tiles = _segments([grp[0] for grp in row_groups], TM_MIX)
    cmap = lambda i: (0, 0)
    in_specs, args = [], []
    for grp, (off, cnt) in zip(row_groups, segs):
        smap = lambda i, off=off, cnt=cnt: (jnp.clip(i - off, 0, cnt - 1), 0)
        in_specs += [pl.BlockSpec((TM_MIX, G), smap)] * 4
        args += list(grp)
    in_specs += [pl.BlockSpec((TM_MIX, D_MODEL), lambda i: (i, 0)),
                 pl.BlockSpec((1, G), cmap), pl.BlockSpec((1, G), cmap),
                 pl.BlockSpec(bd.shape, cmap), pl.BlockSpec((D_MODEL, D_MODEL), cmap)]
    return pl.pallas_call(
        functools.partial(_mix_kernel, segs=segs),
        grid=(n_tiles,), in_specs=in_specs,
        out_specs=pl.BlockSpec((TM_MIX, D_MODEL), lambda i: (i, 0)),
        out_shape=jax.ShapeDtypeStruct((n_tiles * TM_MIX, D_MODEL), F32),
        compiler_params=pltpu.CompilerParams(
            dimension_semantics=("parallel",), vmem_limit_bytes=VMEM_LIMIT),
        name="mix",
    )(*args, x1, lnw, lnb, bd, wo)


def _pack_rwkv(a):
    r, wd, k, v, ad, gd = jnp.split(a, RWKV_SPLITS, axis=-1)
    zw = jnp.zeros(a.shape[:-1] + (LORA_PAD - W_LORA,), a.dtype)
    za = jnp.zeros(a.shape[:-1] + (LORA_PAD - A_LORA,), a.dtype)
    return jnp.concatenate([r, k, v, wd, zw, ad, za, gd], axis=-1)


def _unpack_rwkv(p):
    return jnp.concatenate([
        p[..., 0:G], p[..., OFF_WD:OFF_WD + W_LORA], p[..., G:2 * G], p[..., 2 * G:3 * G],
        p[..., OFF_AD:OFF_AD + A_LORA], p[..., OFF_GD:OFF_GD + G_LORA]], axis=-1)


def _block_tri(n, chunk):
    i = jnp.arange(n)
    return ((i[:, None] // chunk == i[None, :] // chunk) & (i[:, None] >= i[None, :])).astype(BF16)


def kernel(x_prompt, x_sample, state_wkv, state_shift, state_conv, meta_tokens, g_ffn1, ffn1_gate, ffn1_up, ffn1_down, g_mix, w_in, mu_shift, w0, w_lora_w, a0, w_lora_a, w_lora_g, k_k, k_a, r_k, ln_x_w, ln_x_b, conv_w, w_out, g_ffn2, ffn2_gate, ffn2_up, ffn2_down, g_final):
    assert g_ffn1.shape[0] == 1, "single layer"
    nb, seq, _ = x_prompt.shape
    db, dseq, _ = x_sample.shape
    assert dseq == C_SAMPLE and N_META <= C_PROMPT and seq % ROWS_PREP == 0
    tp, ts = nb * seq, db * dseq
    assert tp % TM == 0 and ts % TM == 0

    row = lambda a: a.reshape(1, -1).astype(F32)
    pad_rows = lambda w, n: jnp.concatenate([w, jnp.zeros((n - w.shape[0], w.shape[1]), w.dtype)], axis=0)
    hid = jnp.arange(HEAD_BLOCK) // HEAD_DIM
    bd = (hid[:, None] == hid[None, :]).astype(BF16)
    prep_w = (_pack_rwkv(mu_shift[0])[None], row(w0[0]), pad_rows(w_lora_w[0], LORA_PAD).astype(BF16),
              row(a0[0]), pad_rows(w_lora_a[0], LORA_PAD).astype(BF16), w_lora_g[0].astype(BF16),
              row(k_k[0]), row(k_a[0]), row(r_k[0]), conv_w[0].astype(F32), bd)

    x_meta = jnp.concatenate([jnp.zeros((C_PROMPT - N_META, D_MODEL), F32), meta_tokens.astype(F32)], axis=0)
    x1_meta, wg1, wu1, wd1 = _ffn_narrow(x_meta, row(g_ffn1[0]), ffn1_gate[0], ffn1_up[0], ffn1_down[0],
                                         name="ffn1_meta")
    x1, w_in_p = _ffn_two_stage([x_prompt.reshape(tp, D_MODEL), x_sample.reshape(ts, D_MODEL)],
                                row(g_ffn1[0]), wg1, wu1, wd1, None, [tp + ts], name="ffn1",
                                pack_src=jnp.transpose(w_in[0]))
    x1_meta = jnp.concatenate([x1_meta, jnp.zeros((TM - C_PROMPT, D_MODEL), F32)], axis=0)
    proj = _proj([x1, x1_meta], row(g_mix[0]), w_in_p, name="proj")
    meta_row0 = tp + ts

    tri_p = _block_tri(ROWS_PREP, C_PROMPT)
    zeros_state = jnp.zeros((1, N_HEADS, HEAD_DIM, HEAD_DIM), F32)
    (ah, rh, bh, kh, vv, wc, _, _, _, utail_m, ptail_m) = _prep(
        proj, prep_w, tri_p[:C_PROMPT, :C_PROMPT], n_seq=1, seq_len=C_PROMPT, rows=C_PROMPT,
        chunk=C_PROMPT, row_block_offset=meta_row0 // C_PROMPT,
        extra=(jnp.zeros((8, RW), F32), jnp.zeros((8, G), F32)),
        sample=False, name="prep_meta")
    _, wkv_m = _scan(ah, rh, bh, kh, vv, wc, zeros_state, n_seq=1, seq_len=C_PROMPT, n_par=1,
                     rows=C_PROMPT, chunk=C_PROMPT, per_chunk_state=False, group=N_HEADS,
                     name="scan_meta")

    (ah, rh, bh, kh, vv, wc, g_p, bonus_p, conv_p, utail_p, ptail_p) = _prep(
        proj, prep_w, tri_p, n_seq=nb, seq_len=seq, rows=ROWS_PREP, chunk=C_PROMPT,
        row_block_offset=0, extra=(ptail_m[0], utail_m[0]), sample=False, name="prep_prompt")
    y_p, wkv_p, wg2, wu2, wd2, wo = _scan(
        ah, rh, bh, kh, vv, wc, wkv_m, n_seq=nb, seq_len=seq, n_par=SCAN_PAR, rows=C_PROMPT,
        chunk=C_PROMPT, per_chunk_state=False, group=SCAN_PAR * N_HEADS // 2, name="scan_prompt",
        cast=(ffn2_gate[0], ffn2_up[0], ffn2_down[0], w_out[0]))

    prev = _pack_rwkv(state_shift[0])
    up1 = state_conv[0][:, 1]
    up2 = state_conv[0][:, 0]
    (ah, rh, bh, kh, vv, wc, g_s, bonus_s, conv_s, u_s, plast_s) = _prep(
        proj, prep_w, _block_tri(ROWS_PREP, C_SAMPLE), n_seq=db, seq_len=dseq, rows=ROWS_PREP,
        chunk=C_SAMPLE, row_block_offset=tp // ROWS_PREP, extra=(prev, up1, up2), sample=True,
        name="prep_sample")
    pair = (db, N_HEADS // 2, 2, HEAD_DIM, HEAD_DIM)
    s0_pairs = jnp.transpose(state_wkv[0].reshape(pair), (0, 1, 3, 2, 4)).reshape(
        db, N_HEADS // 2, HEAD_DIM, 2 * HEAD_DIM)
    y_s, wkv_s = _scan(ah, rh, bh, kh, vv, wc, s0_pairs, n_seq=db, seq_len=dseq, n_par=1,
                       rows=ROWS_SCAN_SAMPLE, chunk=C_SAMPLE, per_chunk_state=True, group=16,
                       name="scan_sample")

    x2 = _mix([(y_p, bonus_p, g_p, conv_p), (y_s, bonus_s, g_s, conv_s)], x1,
              row(ln_x_w[0]), row(ln_x_b[0]), bd, wo)
    y_prompt, y_sample = _ffn_two_stage([x2], row(g_ffn2[0]), wg2, wu2, wd2, row(g_final), [tp, ts],
                                        name="ffn2")

    shift_p = _unpack_rwkv(ptail_p.reshape(nb, seq // ROWS_PREP, 8, RW)[:, -1, 7, :])
    conv_state_p = utail_p.reshape(nb, seq // ROWS_PREP, 8, G)[:, -1, 6:, :]
    shift_s = _unpack_rwkv(plast_s)
    conv_state_s = u_s.reshape(db, dseq, G)[:, -2:, :]
    wkv_s_out = jnp.transpose(wkv_s.reshape(db, N_HEADS // 2, HEAD_DIM, 2, HEAD_DIM),
                              (0, 1, 3, 2, 4)).reshape(db, N_HEADS, HEAD_DIM, HEAD_DIM)
    return (y_prompt.reshape(nb, seq, D_MODEL), y_sample.reshape(db, dseq, D_MODEL),
            jnp.swapaxes(wkv_p, -1, -2)[None].astype(state_wkv.dtype),
            shift_p[None].astype(state_shift.dtype),
            conv_state_p[None].astype(state_conv.dtype),
            wkv_s_out[None].astype(state_wkv.dtype), shift_s[None].astype(state_shift.dtype),
            conv_state_s[None].astype(state_conv.dtype))
```

```python
import functools

import jax
import jax.numpy as jnp
from jax import lax
from jax.experimental import pallas as pl
from jax.experimental.pallas import tpu as pltpu

F32 = jnp.float32
BF16 = jnp.bfloat16

D_MODEL = 2048
D_FF = 5632
N_META = 16
G = 1024
HEAD_DIM = 64
N_HEADS = G // HEAD_DIM
MXU_N = 256
HEAD_BLOCK = MXU_N
W_LORA = 96
A_LORA = 96
G_LORA = 256
LORA_PAD = 128
RWKV_PROJ = 3 * G + W_LORA + A_LORA + G_LORA
RWKV_SPLITS = (G, G + W_LORA, 2 * G + W_LORA, 3 * G + W_LORA, 3 * G + W_LORA + A_LORA)
RW = 3 * G + 2 * LORA_PAD + G_LORA
PW = RW + 3 * G
OFF_WD, OFF_AD, OFF_GD = 3 * G, 3 * G + LORA_PAD, 3 * G + 2 * LORA_PAD
RMS_EPS = 1e-6
GN_EPS = 64e-5
EXP_M05_LOG2E = 0.6065306597126334 * 1.4426950408889634

C_PROMPT = 64
C_SAMPLE = 8
TM = 512
TF = 512
TN = PW // 2
assert TN % MXU_N == 0
TM_FFN_UP = 512
TM_FFN_DOWN = 256
TM_MIX = 256
ROWS_PREP = 256
ROWS_SCAN_SAMPLE = 128
SCAN_PAR = 4
VMEM_LIMIT = 56 * 1024 * 1024


def _dot(a, b):
    return jnp.dot(a, b, preferred_element_type=F32)


def _dot_nt(a, b):
    return lax.dot_general(a, b, (((1,), (1,)), ((), ())), preferred_element_type=F32)


def _dot_tn(a, b):
    return lax.dot_general(a, b, (((0,), (0,)), ((), ())), preferred_element_type=F32)


def _split(a):
    hi = a.astype(BF16)
    lo = (a - hi.astype(F32)).astype(BF16)
    return hi, lo


def _head_sum(x, bd):
    xb = x.astype(BF16)
    wb = bd.shape[0]
    return jnp.concatenate([_dot(xb[:, c:c + wb], bd) for c in range(0, x.shape[1], wb)], axis=1)


def _dot_split_rhs(a_bf16, b):
    hi, lo = _split(b)
    return _dot(a_bf16, hi) + _dot(a_bf16, lo)


def _rms(x, g):
    return x * lax.rsqrt(jnp.mean(x * x, axis=-1, keepdims=True) + RMS_EPS) * g


def _segments(arrays, tile):
    segs, off = [], 0
    for arr in arrays:
        n = arr.shape[0] // tile
        assert n * tile == arr.shape[0]
        segs.append((off, n))
        off += n
    return segs, off


def _overlaps(in_segs, out_segs):
    for a, (ao, an) in enumerate(in_segs):
        for b, (bo, bn) in enumerate(out_segs):
            lo, hi = max(ao, bo), min(ao + an, bo + bn)
            if lo < hi:
                yield a, b, lo, hi


def _ffn_narrow_kernel(x_ref, g_ref, wg_ref, wu_ref, wd_ref, o_ref, wg_o, wu_o, wd_o, h_ref, acc_ref):
    j = pl.program_id(0)

    @pl.when(j == 0)
    def _():
        h_ref[...] = _rms(x_ref[...], g_ref[...]).astype(BF16)
        acc_ref[...] = jnp.zeros_like(acc_ref)

    wg, wu, wd = wg_ref[...].astype(BF16), wu_ref[...].astype(BF16), wd_ref[...].astype(BF16)
    wg_o[...], wu_o[...], wd_o[...] = wg, wu, wd
    h = h_ref[...]
    gate = _dot(h, wg)
    up = _dot(h, wu)
    act = (gate * jax.nn.sigmoid(gate) * up).astype(BF16)
    acc_ref[...] += _dot(act, wd)

    @pl.when(j == pl.num_programs(0) - 1)
    def _():
        o_ref[...] = x_ref[...] + 0.5 * acc_ref[...]


def _ffn_narrow(x, g, wg, wu, wd, *, name):
    rows = x.shape[0]
    whole = pl.BlockSpec((rows, D_MODEL), lambda j: (0, 0))
    w_specs = [pl.BlockSpec((D_MODEL, TF), lambda j: (0, j)),
               pl.BlockSpec((D_MODEL, TF), lambda j: (0, j)),
               pl.BlockSpec((TF, D_MODEL), lambda j: (j, 0))]
    return pl.pallas_call(
        _ffn_narrow_kernel,
        grid=(D_FF // TF,),
        in_specs=[whole, pl.BlockSpec((1, D_MODEL), lambda j: (0, 0))] + w_specs,
        out_specs=[whole] + w_specs,
        out_shape=[jax.ShapeDtypeStruct((rows, D_MODEL), F32)]
                  + [jax.ShapeDtypeStruct(w.shape, BF16) for w in (wg, wu, wd)],
        scratch_shapes=[pltpu.VMEM((rows, D_MODEL), BF16), pltpu.VMEM((rows, D_MODEL), F32)],
        compiler_params=pltpu.CompilerParams(
            dimension_semantics=("arbitrary",), vmem_limit_bytes=VMEM_LIMIT),
        name=name,
    )(x, g, wg, wu, wd)


def _ffn_up_kernel(*refs, in_segs):
    n_in = len(in_segs)
    x_refs = refs[:n_in]
    g_ref, wg_ref, wu_ref, a_ref = refs[n_in:]
    i = pl.program_id(1)
    for x_ref, (off, n) in zip(x_refs, in_segs):
        @pl.when((i >= off) & (i < off + n))
        def _(x_ref=x_ref):
            h = _rms(x_ref[...], g_ref[...]).astype(BF16)
            for c in range(0, a_ref.shape[1], MXU_N):
                gate = _dot(h, wg_ref[:, c:c + MXU_N])
                up = _dot(h, wu_ref[:, c:c + MXU_N])
                a_ref[:, c:c + MXU_N] = (gate * jax.nn.sigmoid(gate) * up).astype(BF16)


def _ffn_down_kernel(*refs, in_segs, out_segs, final_norm, pack_steps):
    n_in, n_out = len(in_segs), len(out_segs)
    a_ref = refs[0]
    x_refs = refs[1:1 + n_in]
    wd_ref = refs[1 + n_in]
    rest = refs[2 + n_in:]
    if final_norm:
        gf_ref, rest = rest[0], rest[1:]
    if pack_steps:
        wt_ref, rest = rest[0], rest[1:]
    o_refs = rest[:n_out]
    i = pl.program_id(0)
    if pack_steps:
        @pl.when(i < pack_steps)
        def _():
            _pack_w_in_kernel(wt_ref, rest[n_out])

    for a, b, lo, hi in _overlaps(in_segs, out_segs):
        @pl.when((i >= lo) & (i < hi))
        def _(x_ref=x_refs[a], o_ref=o_refs[b]):
            xo = x_ref[...] + 0.5 * _dot(a_ref[...], wd_ref[...])
            if final_norm:
                xo = _rms(xo, gf_ref[...])
            o_ref[...] = xo


def _ffn_two_stage(xs, g, wg, wu, wd, g_final, out_rows, *, name, pack_src=None):
    final_norm = g_final is not None
    tm_up, tn = TM_FFN_UP, D_FF // 2
    up_segs, up_tiles = _segments(xs, tm_up)
    seg_ji = lambda s: pl.BlockSpec((tm_up, D_MODEL), lambda j, i, s=s: (jnp.clip(i - s[0], 0, s[1] - 1), 0))
    half = pl.BlockSpec((D_MODEL, tn), lambda j, i: (0, j), pipeline_mode=pl.Buffered(1))
    act = pl.pallas_call(
        functools.partial(_ffn_up_kernel, in_segs=up_segs),
        grid=(D_FF // tn, up_tiles),
        in_specs=[seg_ji(s) for s in up_segs] + [pl.BlockSpec((1, D_MODEL), lambda j, i: (0, 0)), half, half],
        out_specs=pl.BlockSpec((tm_up, tn), lambda j, i: (i, j)),
        out_shape=jax.ShapeDtypeStruct((up_tiles * tm_up, D_FF), BF16),
        compiler_params=pltpu.CompilerParams(
            dimension_semantics=("arbitrary", "arbitrary"), vmem_limit_bytes=VMEM_LIMIT),
        name=name + "_up",
    )(*xs, g, wg, wu)

    tm = TM_FFN_DOWN
    in_segs, n_tiles = _segments(xs, tm)
    out_shape = [jax.ShapeDtypeStruct((n, D_MODEL), F32) for n in out_rows]
    out_segs, n_out_tiles = _segments(out_shape, tm)
    assert n_out_tiles == n_tiles and n_tiles * tm == up_tiles * tm_up
    seg_i = lambda s: pl.BlockSpec((tm, D_MODEL), lambda i, s=s: (jnp.clip(i - s[0], 0, s[1] - 1), 0))
    in_specs = ([pl.BlockSpec((tm, D_FF), lambda i: (i, 0))] + [seg_i(s) for s in in_segs]
                + [pl.BlockSpec((D_FF, D_MODEL), lambda i: (0, 0), pipeline_mode=pl.Buffered(1))])
    args = [act] + list(xs) + [wd]
    if final_norm:
        in_specs.append(pl.BlockSpec((1, D_MODEL), lambda i: (0, 0)))
        args.append(g_final)
    out_specs = [seg_i(s) for s in out_segs]
    pack_steps = 0
    if pack_src is not None:
        lane_blk = 128
        pack_steps = D_MODEL // lane_blk
        assert pack_steps <= n_tiles
        blk = lambda i: (0, jnp.minimum(i, pack_steps - 1))
        in_specs.append(pl.BlockSpec((pack_src.shape[0], lane_blk), blk))
        args.append(pack_src)
        out_specs.append(pl.BlockSpec((PW, lane_blk), blk))
        out_shape = out_shape + [jax.ShapeDtypeStruct((PW, D_MODEL), BF16)]
    return pl.pallas_call(
        functools.partial(_ffn_down_kernel, in_segs=in_segs, out_segs=out_segs, final_norm=final_norm,
                          pack_steps=pack_steps),
        grid=(n_tiles,),
        in_specs=in_specs,
        out_specs=out_specs,
        out_shape=out_shape,
        compiler_params=pltpu.CompilerParams(
            dimension_semantics=("arbitrary",), vmem_limit_bytes=VMEM_LIMIT),
        name=name + "_down",
    )(*args)


_PACK_MOVES = (
    (0, 0, G),
    (G, RWKV_SPLITS[1], G),
    (2 * G, RWKV_SPLITS[2], G),
    (OFF_WD, RWKV_SPLITS[0], W_LORA),
    (OFF_AD, RWKV_SPLITS[3], A_LORA),
    (OFF_GD, RWKV_SPLITS[4], G_LORA),
    (RW, RWKV_PROJ, 3 * G),
)


def _pack_w_in_kernel(w_ref, o_ref):
    for dst, src, n in _PACK_MOVES:
        o_ref[dst:dst + n, :] = w_ref[src:src + n, :].astype(BF16)
    for lo, hi in ((OFF_WD + W_LORA, OFF_AD), (OFF_AD + A_LORA, OFF_GD)):
        o_ref[lo:hi, :] = jnp.zeros((hi - lo, o_ref.shape[1]), BF16)


def _proj_kernel(*refs, in_segs):
    n_in = len(in_segs)
    x_refs = refs[:n_in]
    g_ref, w_ref, o_ref = refs[n_in:]
    i = pl.program_id(1)
    for x_ref, (off, n) in zip(x_refs, in_segs):
        @pl.when((i >= off) & (i < off + n))
        def _(x_ref=x_ref):
            h = _rms(x_ref[...], g_ref[...]).astype(BF16)
            o_ref[...] = _dot_nt(h, w_ref[...])


def _proj(xs, g, w, *, name):
    in_segs, n_tiles = _segments(xs, TM)
    seg_ji = lambda s: pl.BlockSpec((TM, D_MODEL), lambda j, i, s=s: (jnp.clip(i - s[0], 0, s[1] - 1), 0),
                                    pipeline_mode=pl.Buffered(1 if s[1] == 1 else 2))
    return pl.pallas_call(
        functools.partial(_proj_kernel, in_segs=in_segs),
        grid=(PW // TN, n_tiles),
        in_specs=[seg_ji(s) for s in in_segs] + [
            pl.BlockSpec((1, D_MODEL), lambda j, i: (0, 0)),
            pl.BlockSpec((TN, D_MODEL), lambda j, i: (j, 0)),
        ],
        out_specs=pl.BlockSpec((TM, TN), lambda j, i: (i, j)),
        out_shape=jax.ShapeDtypeStruct((n_tiles * TM, PW), F32),
        compiler_params=pltpu.CompilerParams(
            dimension_semantics=("parallel", "parallel"), vmem_limit_bytes=VMEM_LIMIT),
        name=name,
    )(*xs, g, w)


def _prep_kernel(*refs, rows, chunk, sample):
    if sample:
        (p_ref, prev_ref, up1_ref, up2_ref, mu_ref, w0_ref, ww_ref, a0_ref, wa_ref, wgl_ref,
         kk_ref, ka_ref, rk_ref, cw_ref, bd_ref, tri_ref,
         ah_ref, rh_ref, bh_ref, kh_ref, v_ref, wc_ref, g_ref, bonus_ref, conv_ref, u_ref,
         plast_ref) = refs
    else:
        (p_ref, cp0_ref, cu0_ref, mu_ref, w0_ref, ww_ref, a0_ref, wa_ref, wgl_ref,
         kk_ref, ka_ref, rk_ref, cw_ref, bd_ref, tri_ref,
         ah_ref, rh_ref, bh_ref, kh_ref, v_ref, wc_ref, g_ref, bonus_ref, conv_ref, u_ref,
         ptail_ref, carry_p, carry_u) = refs

        @pl.when(pl.program_id(1) == 0)
        def _():
            carry_p[...] = cp0_ref[...]
            carry_u[...] = cu0_ref[...]

    def row_ids(width):
        r = lax.broadcasted_iota(jnp.int32, (rows, width), 0)
        return (r & 7) if sample else r

    def per_sequence(x):
        n, w = x.shape
        return jnp.broadcast_to(x[:, None, :], (n, 8, w)).reshape(n * 8, w)

    def shifted_mix(lo, hi):
        p = p_ref[:, lo:hi]
        rolled = pltpu.roll(p, 1, 0)
        if sample:
            sh = jnp.where(row_ids(hi - lo) == 0, per_sequence(prev_ref[:, lo:hi]), rolled)
        else:
            sh = jnp.where(row_ids(hi - lo) == 0, carry_p[7:8, lo:hi], rolled)
        return p + (sh - p) * mu_ref[:, lo:hi]

    r = shifted_mix(0, G)
    k = shifted_mix(G, 2 * G)
    v = shifted_mix(2 * G, 3 * G)
    wd = shifted_mix(OFF_WD, OFF_WD + LORA_PAD)
    ad = shifted_mix(OFF_AD, OFF_AD + LORA_PAD)
    gd = shifted_mix(OFF_GD, OFF_GD + G_LORA)

    z = w0_ref[...] + _dot(jnp.tanh(wd).astype(BF16), ww_ref[...])
    lw = -EXP_M05_LOG2E * jax.nn.sigmoid(z)
    a = jax.nn.sigmoid(a0_ref[...] + _dot(ad.astype(BF16), wa_ref[...]))
    g_ref[...] = _dot(jax.nn.sigmoid(gd).astype(BF16), wgl_ref[...]).astype(BF16)

    bd = bd_ref[...]
    kk = k * kk_ref[...]
    norm = jnp.sqrt(_head_sum(kk * kk, bd))
    kk = kk / jnp.maximum(norm, 1e-12)
    km = k * (1.0 + (a - 1.0) * ka_ref[...])
    bonus_ref[...] = _head_sum(r * km * rk_ref[...], bd) * v

    cum = _dot_split_rhs(tri_ref[...], lw)
    e_cum = jnp.exp2(cum)
    e_inv = jnp.exp2(-cum)
    ah_ref[...] = (-kk * jnp.exp2(cum - lw)).astype(BF16)
    rh_ref[...] = (r * e_cum).astype(BF16)
    bh_ref[...] = (kk * a * e_inv).astype(BF16)
    kh_ref[...] = (km * e_inv).astype(BF16)
    v_ref[...] = v.astype(BF16)
    for c in range(rows // chunk):
        wc_ref[c] = e_cum[(c + 1) * chunk - 1:(c + 1) * chunk, :]

    bg = p_ref[:, RW:RW + G]
    u = p_ref[:, RW + G:RW + 2 * G] * p_ref[:, RW + 2 * G:RW + 3 * G]
    u1 = pltpu.roll(u, 1, 0)
    u2 = pltpu.roll(u, 2, 0)
    rid = row_ids(G)
    if sample:
        s1 = per_sequence(up1_ref[...])
        um1 = jnp.where(rid == 0, s1, u1)
        um2 = jnp.where(rid == 0, per_sequence(up2_ref[...]), jnp.where(rid == 1, s1, u2))
        u_ref[...] = u
        plast_ref[...] = p_ref[:, 0:RW].reshape(rows // 8, 8, RW)[:, 7, :]
    else:
        um1 = jnp.where(rid == 0, carry_u[7:8, :], u1)
        um2 = jnp.where(rid == 0, carry_u[6:7, :], jnp.where(rid == 1, carry_u[7:8, :], u2))
        u_ref[0] = u[rows - 8:, :]
        ptail_ref[0] = p_ref[rows - 8:rows, 0:RW]
    conv = cw_ref[0:1, :] * um2 + cw_ref[1:2, :] * um1 + cw_ref[2:3, :] * u
    conv_ref[...] = (bg * conv).astype(BF16)

    if not sample:
        carry_p[...] = p_ref[rows - 8:rows, 0:RW]
        carry_u[...] = u[rows - 8:, :]


def _prep(proj, weights, tri, *, n_seq, seq_len, rows, chunk, row_block_offset, extra, sample, name):
    t = n_seq * seq_len
    n_chunks = t // chunk
    cpt = rows // chunk
    if sample:
        grid = (t // rows,)
        rmap = lambda i: (i + row_block_offset, 0)
        omap = lambda i: (i, 0)
        omap3 = lambda i: (i, 0, 0)
        cmap = lambda i: (0, 0)
        sem = ("parallel",)
    else:
        tiles = seq_len // rows
        grid = (n_seq, tiles)
        rmap = lambda b, j: (b * tiles + j + row_block_offset, 0)
        omap = lambda b, j: (b * tiles + j, 0)
        omap3 = lambda b, j: (b * tiles + j, 0, 0)
        cmap = lambda b, j: (0, 0)
        sem = ("parallel", "arbitrary")

    const = lambda arr: pl.BlockSpec(arr.shape, cmap)
    in_specs = [pl.BlockSpec((rows, PW), rmap)]
    if sample:
        in_specs += [pl.BlockSpec((rows // 8, RW), omap), pl.BlockSpec((rows // 8, G), omap),
                     pl.BlockSpec((rows // 8, G), omap)]
    else:
        in_specs += [const(e) for e in extra]
    in_specs += [const(w) for w in weights] + [const(tri)]
    args = [proj] + list(extra) + list(weights) + [tri]

    row_out = jax.ShapeDtypeStruct((t, G), F32)
    row_spec = pl.BlockSpec((rows, G), omap)
    row_bf16 = jax.ShapeDtypeStruct((t, G), BF16)
    out_shape = [row_bf16] * 5 + [jax.ShapeDtypeStruct((n_chunks, 1, G), F32), row_bf16, row_out,
                                  row_bf16]
    out_specs = [row_spec] * 5 + [pl.BlockSpec((cpt, 1, G), omap3), row_spec, row_spec, row_spec]
    if sample:
        out_shape += [row_out, jax.ShapeDtypeStruct((t // 8, RW), F32)]
        out_specs += [row_spec, pl.BlockSpec((rows // 8, RW), omap)]
        scratch = []
    else:
        out_shape += [jax.ShapeDtypeStruct((t // rows, 8, G), F32),
                      jax.ShapeDtypeStruct((t // rows, 8, RW), F32)]
        out_specs += [pl.BlockSpec((1, 8, G), omap3), pl.BlockSpec((1, 8, RW), omap3)]
        scratch = [pltpu.VMEM((8, RW), F32), pltpu.VMEM((8, G), F32)]
    return pl.pallas_call(
        functools.partial(_prep_kernel, rows=rows, chunk=chunk, sample=sample),
        grid=grid, in_specs=in_specs, out_specs=out_specs, out_shape=out_shape,
        scratch_shapes=scratch,
        compiler_params=pltpu.CompilerParams(dimension_semantics=sem, vmem_limit_bytes=VMEM_LIMIT),
        name=name,
    )(*args)


def _scan_kernel(*refs, n_par, rows, chunk, per_chunk_state, group, n_cast):
    ah_ref, rh_ref, bh_ref, kh_ref, v_ref, wc_ref, s0_ref = refs[:7]
    y_ref, s_ref = refs[7 + n_cast:9 + n_cast]
    for w_ref, o_ref in zip(refs[7:7 + n_cast], refs[9 + n_cast:]):
        o_ref[...] = w_ref[...].astype(BF16)

    if per_chunk_state:
        s_in = s0_ref
    else:
        s_in = s_ref

        @pl.when(pl.program_id(1) == 0)
        def _():
            for q in range(n_par):
                s_ref[q] = s0_ref[0]

    n_blk = rows // chunk
    lanes = [slice(h * HEAD_DIM, (h + 1) * HEAD_DIM) for h in range(N_HEADS)]
    bf = lambda x: x.astype(BF16)

    shift = chunk.bit_length() - 1
    rs = lax.broadcasted_iota(jnp.int32, (rows, rows), 0)
    cs = lax.broadcasted_iota(jnp.int32, (rows, rows), 1)
    same = (rs >> shift) == (cs >> shift)
    mask_strict = same & (rs > cs)
    ri = lax.broadcasted_iota(jnp.int32, (rows, 2 * rows), 0)
    ci = lax.broadcasted_iota(jnp.int32, (rows, 2 * rows), 1)
    cj = jnp.where(ci >= rows, ci - rows, ci)
    mask_r = ((ri >> shift) == (cj >> shift)) & (ri >= cj)
    eye = (rs == cs).astype(F32)

    def chunk_rows(x, c):
        return x[c * chunk:(c + 1) * chunk, :]

    pair_lane = lax.broadcasted_iota(jnp.int32, (rows, 2 * HEAD_DIM), 1) >> (HEAD_DIM.bit_length() - 1)

    def pair_masked(ref, q, h):
        hp, hh = divmod(h, 2)
        x2 = ref[q, :, 2 * HEAD_DIM * hp:2 * HEAD_DIM * (hp + 1)].astype(F32)
        return jnp.where(pair_lane == hh, x2, 0.0)

    def unit_group(units):
        per_unit = lambda f: {u: f(*u) for u in units}
        v = per_unit(lambda q, h: v_ref[q, :, lanes[h]])
        if n_blk == 1:
            a = per_unit(lambda q, h: ah_ref[q, :, lanes[h]])
            r = per_unit(lambda q, h: rh_ref[q, :, lanes[h]])
            b = per_unit(lambda q, h: bh_ref[q, :, lanes[h]])
            k = per_unit(lambda q, h: kh_ref[q, :, lanes[h]])
        else:
            a = per_unit(lambda q, h: pair_masked(ah_ref, q, h))
            r = per_unit(lambda q, h: pair_masked(rh_ref, q, h))
            b = per_unit(lambda q, h: pair_masked(bh_ref, q, h))
            k = per_unit(lambda q, h: pair_masked(kh_ref, q, h))
            v32 = {u: v[u].astype(F32) for u in units}
        ar = {u: bf(jnp.concatenate([a[u], r[u]], axis=0)) for u in units}
        bk = {u: bf(jnp.concatenate([b[u], k[u]], axis=0)) for u in units}

        bk_t = {u: jnp.transpose(bk[u]) for u in units}
        gram = {u: _dot(ar[u], bk_t[u]) for u in units}
        l_ab = {u: jnp.where(mask_strict, gram[u][:rows, :rows], 0.0) for u in units}
        l_ak = {u: bf(jnp.where(mask_strict, gram[u][:rows, rows:], 0.0)) for u in units}
        l_r = {u: bf(jnp.where(mask_r, gram[u][rows:, :], 0.0)) for u in units}

        xs, ys = {}, {}
        for u in units:
            q, h = u
            if n_blk == 1:
                st = _dot(ar[u], bf(s_in[q, h]))
                xs[u], ys[u] = st[:rows], st[rows:]
            elif h % 2 == 0:
                u1 = (q, h + 1)
                assert u1 in units
                parts = [_dot_nt(jnp.concatenate([chunk_rows(t, c) for t in (a[u], r[u], a[u1], r[u1])],
                                                 axis=0), s_in[c, h // 2]) for c in range(n_blk)]
                pick = lambda n: jnp.concatenate([p[n * chunk:(n + 1) * chunk] for p in parts], axis=0)
                xs[u], ys[u], xs[u1], ys[u1] = pick(0), pick(1), pick(2), pick(3)

        x = {u: xs[u] + _dot(l_ak[u], v[u]) for u in units}

        t_inv = {u: eye + l_ab[u] for u in units}
        n = 2
        if n < chunk:
            pw = {u: bf(l_ab[u]) for u in units}
            pw = {u: bf(_dot(pw[u], pw[u])) for u in units}
        while n < chunk:
            if 2 * n < chunk:
                both = {u: _dot(jnp.concatenate([bf(t_inv[u]), pw[u]], axis=0), pw[u]) for u in units}
                t_inv = {u: t_inv[u] + both[u][:rows] for u in units}
                pw = {u: bf(both[u][rows:]) for u in units}
            else:
                t_inv = {u: t_inv[u] + _dot(bf(t_inv[u]), pw[u]) for u in units}
            n *= 2

        uu = {u: _dot(bf(t_inv[u]), bf(x[u])) for u in units}
        uv = {u: jnp.concatenate([bf(uu[u]), v[u]], axis=0) for u in units}
        if n_blk == 1:
            both = {u: _dot(jnp.concatenate([l_r[u], bk_t[u]], axis=0), uv[u]) for u in units}
            for u in units:
                q, h = u
                y_ref[q, :, lanes[h]] = ys[u] + both[u][:rows]
            for u in units:
                q, h = u
                w_c = jnp.broadcast_to(wc_ref[q, 0][:, lanes[h]], (HEAD_DIM, HEAD_DIM))
                s_ref[q, h] = (s_in[q, h] + both[u][rows:]) * jnp.transpose(w_c)
        else:
            for u in units:
                q, h = u
                y_ref[q, :, lanes[h]] = ys[u] + _dot(l_r[u], uv[u])

            for q, h in units:
                if h % 2 == 0:
                    u0, u1, hp = (q, h), (q, h + 1), h // 2
                    for c in range(n_blk):
                        uv_c = jnp.concatenate(
                            [chunk_rows(t, c) for t in (uu[u0], v32[u0], uu[u1], v32[u1])], axis=0)
                        bk_c = jnp.concatenate(
                            [chunk_rows(t, c) for t in (b[u0], k[u0], b[u1], k[u1])], axis=0)
                        s_new = s_in[c, hp] + _dot_tn(uv_c, bk_c)
                        s_ref[c, hp] = s_new * wc_ref[q, c][:, 2 * HEAD_DIM * hp:2 * HEAD_DIM * (hp + 1)]

    all_units = [(q, h) for h in range(N_HEADS) for q in range(n_par)]
    for g0 in range(0, len(all_units), group):
        unit_group(all_units[g0:g0 + group])


def _scan(ah, rh, bh, kh, v, wc, s0, *, n_seq, seq_len, n_par, rows, chunk, per_chunk_state, group,
          name, cast=()):
    t = n_seq * seq_len
    n_blk = rows // chunk
    if per_chunk_state:
        assert seq_len == chunk and n_par == 1
        lead = 1
        grid = (t // rows,)
        rmap = lambda i: (0, i, 0)
        cmap = lambda i: (0, i, 0, 0)
        smap = lambda i: (i, 0, 0, 0)
        s0map = smap
        state_block = (n_blk, N_HEADS // 2, HEAD_DIM, 2 * HEAD_DIM)
        s0_block = state_block
        wc_block = (1, n_blk, 1, G)
        sem = ("parallel",)
    else:
        assert rows == chunk and s0.shape[0] == 1 and n_seq % n_par == 0
        lead = n_seq
        grid = (n_seq // n_par, seq_len // chunk)
        rmap = lambda s, c: (s, c, 0)
        cmap = lambda s, c: (s, c, 0, 0)
        smap = lambda s, c: (s, 0, 0, 0)
        s0map = lambda s, c: (0, 0, 0, 0)
        state_block = (n_par, N_HEADS, HEAD_DIM, HEAD_DIM)
        s0_block = (1, N_HEADS, HEAD_DIM, HEAD_DIM)
        wc_block = (n_par, 1, 1, G)
        sem = ("parallel", "arbitrary")
    rows3 = lambda x: x.reshape(lead, t // lead, G)
    row_spec = pl.BlockSpec((n_par, rows, G), rmap)
    n_steps = 1
    for extent in grid:
        n_steps *= extent
    step = (lambda i: i) if len(grid) == 1 else (lambda s, c: s * grid[1] + c)
    cast_specs = []
    for w in cast:
        assert w.shape[0] % (16 * n_steps) == 0
        cast_specs.append(pl.BlockSpec((w.shape[0] // n_steps, w.shape[1]), lambda *g: (step(*g), 0)))
    y, s_out, *narrowed = pl.pallas_call(
        functools.partial(_scan_kernel, n_par=n_par, rows=rows, chunk=chunk,
                          per_chunk_state=per_chunk_state, group=group, n_cast=len(cast)),
        grid=grid,
        in_specs=([row_spec] * 5 + [pl.BlockSpec(wc_block, cmap), pl.BlockSpec(s0_block, s0map)]
                  + cast_specs),
        out_specs=[row_spec, pl.BlockSpec(state_block, smap)] + cast_specs,
        out_shape=[jax.ShapeDtypeStruct((lead, t // lead, G), F32),
                   jax.ShapeDtypeStruct((n_seq,) + state_block[1:], F32)]
                  + [jax.ShapeDtypeStruct(w.shape, BF16) for w in cast],
        compiler_params=pltpu.CompilerParams(dimension_semantics=sem, vmem_limit_bytes=VMEM_LIMIT),
        name=name,
    )(rows3(ah), rows3(rh), rows3(bh), rows3(kh), rows3(v), wc.reshape(lead, -1, 1, G), s0, *cast)
    return (y.reshape(t, G), s_out, *narrowed)


def _mix_kernel(*refs, segs):
    n = len(segs)
    row_refs = [refs[4 * s:4 * s + 4] for s in range(n)]
    x_ref, lnw_ref, lnb_ref, bd_ref, wo_ref, o_ref = refs[4 * n:]
    i = pl.program_id(0)

    def body(y_ref, bonus_ref, g_ref, conv_ref):
        bd = bd_ref[...]
        y = y_ref[...]
        mu = _head_sum(y, bd) * (1.0 / HEAD_DIM)
        d = y - mu
        var = _head_sum(d * d, bd) * (1.0 / HEAD_DIM)
        yn = d * lax.rsqrt(var + GN_EPS) * lnw_ref[...] + lnb_ref[...]
        rw = ((yn + bonus_ref[...]) * g_ref[...]).astype(BF16)
        mix = _dot(rw, wo_ref[0:G, :]) + _dot(conv_ref[...], wo_ref[G:2 * G, :])
        o_ref[...] = x_ref[...] + mix

    for rr, (off, cnt) in zip(row_refs, segs):
        @pl.when((i >= off) & (i < off + cnt))
        def _(rr=rr):
            body(*rr)


def _mix(row_groups, x1, lnw, lnb, bd, wo):
    segs, n_tiles = _segments([grp[0] for grp in row_groups], TM_MIX)
    cmap = lambda i: (0, 0)
    in_specs, args = [], []
    for grp, (off, cnt) in zip(row_groups, segs):
        smap = lambda i, off=off, cnt=cnt: (jnp.clip(i - off, 0, cnt - 1), 0)
        in_specs += [pl.BlockSpec((TM_MIX, G), smap)] * 4
        args += list(grp)
    in_specs += [pl.BlockSpec((TM_MIX, D_MODEL), lambda i: (i, 0)),
                 pl.BlockSpec((1, G), cmap), pl.BlockSpec((1, G), cmap),
                 pl.BlockSpec(bd.shape, cmap), pl.BlockSpec((D_MODEL, D_MODEL), cmap)]
    return pl.pallas_call(
        functools.partial(_mix_kernel, segs=segs),
        grid=(n_tiles,), in_specs=in_specs,
        out_specs=pl.BlockSpec((TM_MIX, D_MODEL), lambda i: (i, 0)),
        out_shape=jax.ShapeDtypeStruct((n_tiles * TM_MIX, D_MODEL), F32),
        compiler_params=pltpu.CompilerParams(
            dimension_semantics=("parallel",), vmem_limit_bytes=VMEM_LIMIT),
        name="mix",
    )(*args, x1, lnw, lnb, bd, wo)


def _pack_rwkv(a):
    r, wd, k, v, ad, gd = jnp.split(a, RWKV_SPLITS, axis=-1)
    zw = jnp.zeros(a.shape[:-1] + (LORA_PAD - W_LORA,), a.dtype)
    za = jnp.zeros(a.shape[:-1] + (LORA_PAD - A_LORA,), a.dtype)
    return jnp.concatenate([r, k, v, wd, zw, ad, za, gd], axis=-1)


def _unpack_rwkv(p):
    return jnp.concatenate([
        p[..., 0:G], p[..., OFF_WD:OFF_WD + W_LORA], p[..., G:2 * G], p[..., 2 * G:3 * G],
        p[..., OFF_AD:OFF_AD + A_LORA], p[..., OFF_GD:OFF_GD + G_LORA]], axis=-1)


def _block_tri(n, chunk):
    i = jnp.arange(n)
    return ((i[:, None] // chunk == i[None, :] // chunk) & (i[:, None] >= i[None, :])).astype(BF16)


def kernel(x_prompt, x_sample, state_wkv, state_shift, state_conv, meta_tokens, g_ffn1, ffn1_gate, ffn1_up, ffn1_down, g_mix, w_in, mu_shift, w0, w_lora_w, a0, w_lora_a, w_lora_g, k_k, k_a, r_k, ln_x_w, ln_x_b, conv_w, w_out, g_ffn2, ffn2_gate, ffn2_up, ffn2_down, g_final):
    assert g_ffn1.shape[0] == 1, "single layer"
    nb, seq, _ = x_prompt.shape
    db, dseq, _ = x_sample.shape
    assert dseq == C_SAMPLE and N_META <= C_PROMPT and seq % ROWS_PREP == 0
    tp, ts = nb * seq, db * dseq
    assert tp % TM == 0 and ts % TM == 0

    row = lambda a: a.reshape(1, -1).astype(F32)
    pad_rows = lambda w, n: jnp.concatenate([w, jnp.zeros((n - w.shape[0], w.shape[1]), w.dtype)], axis=0)
    hid = jnp.arange(HEAD_BLOCK) // HEAD_DIM
    bd = (hid[:, None] == hid[None, :]).astype(BF16)
    prep_w = (_pack_rwkv(mu_shift[0])[None], row(w0[0]), pad_rows(w_lora_w[0], LORA_PAD).astype(BF16),
              row(a0[0]), pad_rows(w_lora_a[0], LORA_PAD).astype(BF16), w_lora_g[0].astype(BF16),
              row(k_k[0]), row(k_a[0]), row(r_k[0]), conv_w[0].astype(F32), bd)

    x_meta = jnp.concatenate([jnp.zeros((C_PROMPT - N_META, D_MODEL), F32), meta_tokens.astype(F32)], axis=0)
    x1_meta, wg1, wu1, wd1 = _ffn_narrow(x_meta, row(g_ffn1[0]), ffn1_gate[0], ffn1_up[0], ffn1_down[0],
                                         name="ffn1_meta")
    x1, w_in_p = _ffn_two_stage([x_prompt.reshape(tp, D_MODEL), x_sample.reshape(ts, D_MODEL)],
                                row(g_ffn1[0]), wg1, wu1, wd1, None, [tp + ts], name="ffn1",
                                pack_src=jnp.transpose(w_in[0]))
    x1_meta = jnp.concatenate([x1_meta, jnp.zeros((TM - C_PROMPT, D_MODEL), F32)], axis=0)
    proj = _proj([x1, x1_meta], row(g_mix[0]), w_in_p, name="proj")
    meta_row0 = tp + ts

    tri_p = _block_tri(ROWS_PREP, C_PROMPT)
    zeros_state = jnp.zeros((1, N_HEADS, HEAD_DIM, HEAD_DIM), F32)
    (ah, rh, bh, kh, vv, wc, _, _, _, utail_m, ptail_m) = _prep(
        proj, prep_w, tri_p[:C_PROMPT, :C_PROMPT], n_seq=1, seq_len=C_PROMPT, rows=C_PROMPT,
        chunk=C_PROMPT, row_block_offset=meta_row0 // C_PROMPT,
        extra=(jnp.zeros((8, RW), F32), jnp.zeros((8, G), F32)),
        sample=False, name="prep_meta")
    _, wkv_m = _scan(ah, rh, bh, kh, vv, wc, zeros_state, n_seq=1, seq_len=C_PROMPT, n_par=1,
                     rows=C_PROMPT, chunk=C_PROMPT, per_chunk_state=False, group=N_HEADS,
                     name="scan_meta")

    (ah, rh, bh, kh, vv, wc, g_p, bonus_p, conv_p, utail_p, ptail_p) = _prep(
        proj, prep_w, tri_p, n_seq=nb, seq_len=seq, rows=ROWS_PREP, chunk=C_PROMPT,
        row_block_offset=0, extra=(ptail_m[0], utail_m[0]), sample=False, name="prep_prompt")
    y_p, wkv_p, wg2, wu2, wd2, wo = _scan(
        ah, rh, bh, kh, vv, wc, wkv_m, n_seq=nb, seq_len=seq, n_par=SCAN_PAR, rows=C_PROMPT,
        chunk=C_PROMPT, per_chunk_state=False, group=SCAN_PAR * N_HEADS // 2, name="scan_prompt",
        cast=(ffn2_gate[0], ffn2_up[0], ffn2_down[0], w_out[0]))

    prev = _pack_rwkv(state_shift[0])
    up1 = state_conv[0][:, 1]
    up2 = state_conv[0][:, 0]
    (ah, rh, bh, kh, vv, wc, g_s, bonus_s, conv_s, u_s, plast_s) = _prep(
        proj, prep_w, _block_tri(ROWS_PREP, C_SAMPLE), n_seq=db, seq_len=dseq, rows=ROWS_PREP,
        chunk=C_SAMPLE, row_block_offset=tp // ROWS_PREP, extra=(prev, up1, up2), sample=True,
        name="prep_sample")
    pair = (db, N_HEADS // 2, 2, HEAD_DIM, HEAD_DIM)
    s0_pairs = jnp.transpose(state_wkv[0].reshape(pair), (0, 1, 3, 2, 4)).reshape(
        db, N_HEADS // 2, HEAD_DIM, 2 * HEAD_DIM)
    y_s, wkv_s = _scan(ah, rh, bh, kh, vv, wc, s0_pairs, n_seq=db, seq_len=dseq, n_par=1,
                       rows=ROWS_SCAN_SAMPLE, chunk=C_SAMPLE, per_chunk_state=True, group=16,
                       name="scan_sample")

    x2 = _mix([(y_p, bonus_p, g_p, conv_p), (y_s, bonus_s, g_s, conv_s)], x1,
              row(ln_x_w[0]), row(ln_x_b[0]), bd, wo)
    y_prompt, y_sample = _ffn_two_stage([x2], row(g_ffn2[0]), wg2, wu2, wd2, row(g_final), [tp, ts],
                                        name="ffn2")

    shift_p = _unpack_rwkv(ptail_p.reshape(nb, seq // ROWS_PREP, 8, RW)[:, -1, 7, :])
    conv_state_p = utail_p.reshape(nb, seq // ROWS_PREP, 8, G)[:, -1, 6:, :]
    shift_s = _unpack_rwkv(plast_s)
    conv_state_s = u_s.reshape(db, dseq, G)[:, -2:, :]
    wkv_s_out = jnp.transpose(wkv_s.reshape(db, N_HEADS // 2, HEAD_DIM, 2, HEAD_DIM),
                              (0, 1, 3, 2, 4)).reshape(db, N_HEADS, HEAD_DIM, HEAD_DIM)
    return (y_prompt.reshape(nb, seq, D_MODEL), y_sample.reshape(db, dseq, D_MODEL),
            jnp.swapaxes(wkv_p, -1, -2)[None].astype(state_wkv.dtype),
            shift_p[None].astype(state_shift.dtype),
            conv_state_p[None].astype(state_conv.dtype),
            wkv_s_out[None].astype(state_wkv.dtype), shift_s[None].astype(state_shift.dtype),
            conv_state_s[None].astype(state_conv.dtype))
```

```python
import functools

import jax
import jax.numpy as jnp
from jax import lax
from jax.experimental import pallas as pl
from jax.experimental.pallas import tpu as pltpu

F32 = jnp.float32
BF16 = jnp.bfloat16

D_MODEL = 2048
D_FF = 5632
N_META = 16
G = 1024
HEAD_DIM = 64
N_HEADS = G // HEAD_DIM
MXU_N = 256
HEAD_BLOCK = MXU_N
W_LORA = 96
A_LORA = 96
G_LORA = 256
LORA_PAD = 128
RWKV_PROJ = 3 * G + W_LORA + A_LORA + G_LORA
RWKV_SPLITS = (G, G + W_LORA, 2 * G + W_LORA, 3 * G + W_LORA, 3 * G + W_LORA + A_LORA)
RW = 3 * G + 2 * LORA_PAD + G_LORA
PW = RW + 3 * G
OFF_WD, OFF_AD, OFF_GD = 3 * G, 3 * G + LORA_PAD, 3 * G + 2 * LORA_PAD
RMS_EPS = 1e-6
GN_EPS = 64e-5
EXP_M05_LOG2E = 0.6065306597126334 * 1.4426950408889634

C_PROMPT = 64
C_SAMPLE = 8
TM = 512
TF = 512
TN = PW // 2
assert TN % MXU_N == 0
TM_FFN_UP = 512
TM_FFN_DOWN = 256
TM_MIX = 256
ROWS_PREP = 256
ROWS_SCAN_SAMPLE = 128
SCAN_PAR = 4
VMEM_LIMIT = 56 * 1024 * 1024


def _dot(a, b):
    return jnp.dot(a, b, preferred_element_type=F32)


def _dot_nt(a, b):
    return lax.dot_general(a, b, (((1,), (1,)), ((), ())), preferred_element_type=F32)


def _dot_tn(a, b):
    return lax.dot_general(a, b, (((0,), (0,)), ((), ())), preferred_element_type=F32)


def _split(a):
    hi = a.astype(BF16)
    lo = (a - hi.astype(F32)).astype(BF16)
    return hi, lo


def _head_sum(x, bd):
    xb = x.astype(BF16)
    wb = bd.shape[0]
    return jnp.concatenate([_dot(xb[:, c:c + wb], bd) for c in range(0, x.shape[1], wb)], axis=1)


def _dot_split_rhs(a_bf16, b):
    hi, lo = _split(b)
    return _dot(a_bf16, hi) + _dot(a_bf16, lo)


def _rms(x, g):
    return x * lax.rsqrt(jnp.mean(x * x, axis=-1, keepdims=True) + RMS_EPS) * g


def _segments(arrays, tile):
    segs, off = [], 0
    for arr in arrays:
        n = arr.shape[0] // tile
        assert n * tile == arr.shape[0]
        segs.append((off, n))
        off += n
    return segs, off


def _overlaps(in_segs, out_segs):
    for a, (ao, an) in enumerate(in_segs):
        for b, (bo, bn) in enumerate(out_segs):
            lo, hi = max(ao, bo), min(ao + an, bo + bn)
            if lo < hi:
                yield a, b, lo, hi


def _ffn_narrow_kernel(x_ref, g_ref, wg_ref, wu_ref, wd_ref, o_ref, wg_o, wu_o, wd_o, h_ref, acc_ref):
    j = pl.program_id(0)

    @pl.when(j == 0)
    def _():
        h_ref[...] = _rms(x_ref[...], g_ref[...]).astype(BF16)
        acc_ref[...] = jnp.zeros_like(acc_ref)

    wg, wu, wd = wg_ref[...].astype(BF16), wu_ref[...].astype(BF16), wd_ref[...].astype(BF16)
    wg_o[...], wu_o[...], wd_o[...] = wg, wu, wd
    h = h_ref[...]
    gate = _dot(h, wg)
    up = _dot(h, wu)
    act = (gate * jax.nn.sigmoid(gate) * up).astype(BF16)
    acc_ref[...] += _dot(act, wd)

    @pl.when(j == pl.num_programs(0) - 1)
    def _():
        o_ref[...] = x_ref[...] + 0.5 * acc_ref[...]


def _ffn_narrow(x, g, wg, wu, wd, *, name):
    rows = x.shape[0]
    whole = pl.BlockSpec((rows, D_MODEL), lambda j: (0, 0))
    w_specs = [pl.BlockSpec((D_MODEL, TF), lambda j: (0, j)),
               pl.BlockSpec((D_MODEL, TF), lambda j: (0, j)),
               pl.BlockSpec((TF, D_MODEL), lambda j: (j, 0))]
    return pl.pallas_call(
        _ffn_narrow_kernel,
        grid=(D_FF // TF,),
        in_specs=[whole, pl.BlockSpec((1, D_MODEL), lambda j: (0, 0))] + w_specs,
        out_specs=[whole] + w_specs,
        out_shape=[jax.ShapeDtypeStruct((rows, D_MODEL), F32)]
                  + [jax.ShapeDtypeStruct(w.shape, BF16) for w in (wg, wu, wd)],
        scratch_shapes=[pltpu.VMEM((rows, D_MODEL), BF16), pltpu.VMEM((rows, D_MODEL), F32)],
        compiler_params=pltpu.CompilerParams(
            dimension_semantics=("arbitrary",), vmem_limit_bytes=VMEM_LIMIT),
        name=name,
    )(x, g, wg, wu, wd)


def _ffn_up_kernel(*refs, in_segs):
    n_in = len(in_segs)
    x_refs = refs[:n_in]
    g_ref, wg_ref, wu_ref, a_ref = refs[n_in:]
    i = pl.program_id(1)
    for x_ref, (off, n) in zip(x_refs, in_segs):
        @pl.when((i >= off) & (i < off + n))
        def _(x_ref=x_ref):
            h = _rms(x_ref[...], g_ref[...]).astype(BF16)
            for c in range(0, a_ref.shape[1], MXU_N):
                gate = _dot(h, wg_ref[:, c:c + MXU_N])
                up = _dot(h, wu_ref[:, c:c + MXU_N])
                a_ref[:, c:c + MXU_N] = (gate * jax.nn.sigmoid(gate) * up).astype(BF16)


def _ffn_down_kernel(*refs, in_segs, out_segs, final_norm, pack_steps):
    n_in, n_out = len(in_segs), len(out_segs)
    a_ref = refs[0]
    x_refs = refs[1:1 + n_in]
    wd_ref = refs[1 + n_in]
    rest = refs[2 + n_in:]
    if final_norm:
        gf_ref, rest = rest[0], rest[1:]
    if pack_steps:
        wt_ref, rest = rest[0], rest[1:]
    o_refs = rest[:n_out]
    i = pl.program_id(0)
    if pack_steps:
        @pl.when(i < pack_steps)
        def _():
            _pack_w_in_kernel(wt_ref, rest[n_out])

    for a, b, lo, hi in _overlaps(in_segs, out_segs):
        @pl.when((i >= lo) & (i < hi))
        def _(x_ref=x_refs[a], o_ref=o_refs[b]):
            xo = x_ref[...] + 0.5 * _dot(a_ref[...], wd_ref[...])
            if final_norm:
                xo = _rms(xo, gf_ref[...])
            o_ref[...] = xo


def _ffn_two_stage(xs, g, wg, wu, wd, g_final, out_rows, *, name, pack_src=None):
    final_norm = g_final is not None
    tm_up, tn = TM_FFN_UP, D_FF // 2
    up_segs, up_tiles = _segments(xs, tm_up)
    seg_ji = lambda s: pl.BlockSpec((tm_up, D_MODEL), lambda j, i, s=s: (jnp.clip(i - s[0], 0, s[1] - 1), 0))
    half = pl.BlockSpec((D_MODEL, tn), lambda j, i: (0, j), pipeline_mode=pl.Buffered(1))
    act = pl.pallas_call(
        functools.partial(_ffn_up_kernel, in_segs=up_segs),
        grid=(D_FF // tn, up_tiles),
        in_specs=[seg_ji(s) for s in up_segs] + [pl.BlockSpec((1, D_MODEL), lambda j, i: (0, 0)), half, half],
        out_specs=pl.BlockSpec((tm_up, tn), lambda j, i: (i, j)),
        out_shape=jax.ShapeDtypeStruct((up_tiles * tm_up, D_FF), BF16),
        compiler_params=pltpu.CompilerParams(
            dimension_semantics=("arbitrary", "arbitrary"), vmem_limit_bytes=VMEM_LIMIT),
        name=name + "_up",
    )(*xs, g, wg, wu)

    tm = TM_FFN_DOWN
    in_segs, n_tiles = _segments(xs, tm)
    out_shape = [jax.ShapeDtypeStruct((n, D_MODEL), F32) for n in out_rows]
    out_segs, n_out_tiles = _segments(out_shape, tm)
    assert n_out_tiles == n_tiles and n_tiles * tm == up_tiles * tm_up
    seg_i = lambda s: pl.BlockSpec((tm, D_MODEL), lambda i, s=s: (jnp.clip(i - s[0], 0, s[1] - 1), 0))
    in_specs = ([pl.BlockSpec((tm, D_FF), lambda i: (i, 0))] + [seg_i(s) for s in in_segs]
                + [pl.BlockSpec((D_FF, D_MODEL), lambda i: (0, 0), pipeline_mode=pl.Buffered(1))])
    args = [act] + list(xs) + [wd]
    if final_norm:
        in_specs.append(pl.BlockSpec((1, D_MODEL), lambda i: (0, 0)))
        args.append(g_final)
    out_specs = [seg_i(s) for s in out_segs]
    pack_steps = 0
    if pack_src is not None:
        lane_blk = 128
        pack_steps = D_MODEL // lane_blk
        assert pack_steps <= n_tiles
        blk = lambda i: (0, jnp.minimum(i, pack_steps - 1))
        in_specs.append(pl.BlockSpec((pack_src.shape[0], lane_blk), blk))
        args.append(pack_src)
        out_specs.append(pl.BlockSpec((PW, lane_blk), blk))
        out_shape = out_shape + [jax.ShapeDtypeStruct((PW, D_MODEL), BF16)]
    return pl.pallas_call(
        functools.partial(_ffn_down_kernel, in_segs=in_segs, out_segs=out_segs, final_norm=final_norm,
                          pack_steps=pack_steps),
        grid=(n_tiles,),
        in_specs=in_specs,
        out_specs=out_specs,
        out_shape=out_shape,
        compiler_params=pltpu.CompilerParams(
            dimension_semantics=("arbitrary",), vmem_limit_bytes=VMEM_LIMIT),
        name=name + "_down",
    )(*args)


_PACK_MOVES = (
    (0, 0, G),
    (G, RWKV_SPLITS[1], G),
    (2 * G, RWKV_SPLITS[2], G),
    (OFF_WD, RWKV_SPLITS[0], W_LORA),
    (OFF_AD, RWKV_SPLITS[3], A_LORA),
    (OFF_GD, RWKV_SPLITS[4], G_LORA),
    (RW, RWKV_PROJ, 3 * G),
)


def _pack_w_in_kernel(w_ref, o_ref):
    for dst, src, n in _PACK_MOVES:
        o_ref[dst:dst + n, :] = w_ref[src:src + n, :].astype(BF16)
    for lo, hi in ((OFF_WD + W_LORA, OFF_AD), (OFF_AD + A_LORA, OFF_GD)):
        o_ref[lo:hi, :] = jnp.zeros((hi - lo, o_ref.shape[1]), BF16)


def _proj_kernel(*refs, in_segs):
    n_in = len(in_segs)
    x_refs = refs[:n_in]
    g_ref, w_ref, o_ref = refs[n_in:]
    i = pl.program_id(1)
    for x_ref, (off, n) in zip(x_refs, in_segs):
        @pl.when((i >= off) & (i < off + n))
        def _(x_ref=x_ref):
            h = _rms(x_ref[...], g_ref[...]).astype(BF16)
            o_ref[...] = _dot_nt(h, w_ref[...])


def _proj(xs, g, w, *, name):
    in_segs, n_tiles = _segments(xs, TM)
    seg_ji = lambda s: pl.BlockSpec((TM, D_MODEL), lambda j, i, s=s: (jnp.clip(i - s[0], 0, s[1] - 1), 0),
                                    pipeline_mode=pl.Buffered(1 if s[1] == 1 else 2))
    return pl.pallas_call(
        functools.partial(_proj_kernel, in_segs=in_segs),
        grid=(PW // TN, n_tiles),
        in_specs=[seg_ji(s) for s in in_segs] + [
            pl.BlockSpec((1, D_MODEL), lambda j, i: (0, 0)),
            pl.BlockSpec((TN, D_MODEL), lambda j, i: (j, 0)),
        ],
        out_specs=pl.BlockSpec((TM, TN), lambda j, i: (i, j)),
        out_shape=jax.ShapeDtypeStruct((n_tiles * TM, PW), F32),
        compiler_params=pltpu.CompilerParams(
            dimension_semantics=("parallel", "parallel"), vmem_limit_bytes=VMEM_LIMIT),
        name=name,
    )(*xs, g, w)


def _prep_kernel(*refs, rows, chunk, sample):
    if sample:
        (p_ref, prev_ref, up1_ref, up2_ref, mu_ref, w0_ref, ww_ref, a0_ref, wa_ref, wgl_ref,
         kk_ref, ka_ref, rk_ref, cw_ref, bd_ref, tri_ref,
         ops_ref, wc_ref, g_ref, bonus_ref, conv_ref, u_ref,
         plast_ref) = refs
    else:
        (p_ref, cp0_ref, cu0_ref, mu_ref, w0_ref, ww_ref, a0_ref, wa_ref, wgl_ref,
         kk_ref, ka_ref, rk_ref, cw_ref, bd_ref, tri_ref,
         ops_ref, wc_ref, g_ref, bonus_ref, conv_ref, u_ref,
         ptail_ref, carry_p, carry_u) = refs

        @pl.when(pl.program_id(1) == 0)
        def _():
            carry_p[...] = cp0_ref[...]
            carry_u[...] = cu0_ref[...]

    def row_ids(width):
        r = lax.broadcasted_iota(jnp.int32, (rows, width), 0)
        return (r & 7) if sample else r

    def per_sequence(x):
        n, w = x.shape
        return jnp.broadcast_to(x[:, None, :], (n, 8, w)).reshape(n * 8, w)

    def shifted_mix(lo, hi):
        p = p_ref[:, lo:hi]
        rolled = pltpu.roll(p, 1, 0)
        if sample:
            sh = jnp.where(row_ids(hi - lo) == 0, per_sequence(prev_ref[:, lo:hi]), rolled)
        else:
            sh = jnp.where(row_ids(hi - lo) == 0, carry_p[7:8, lo:hi], rolled)
        return p + (sh - p) * mu_ref[:, lo:hi]

    r = shifted_mix(0, G)
    k = shifted_mix(G, 2 * G)
    v = shifted_mix(2 * G, 3 * G)
    wd = shifted_mix(OFF_WD, OFF_WD + LORA_PAD)
    ad = shifted_mix(OFF_AD, OFF_AD + LORA_PAD)
    gd = shifted_mix(OFF_GD, OFF_GD + G_LORA)

    z = w0_ref[...] + _dot(jnp.tanh(wd).astype(BF16), ww_ref[...])
    lw = -EXP_M05_LOG2E * jax.nn.sigmoid(z)
    a = jax.nn.sigmoid(a0_ref[...] + _dot(ad.astype(BF16), wa_ref[...]))
    g_ref[...] = _dot(jax.nn.sigmoid(gd).astype(BF16), wgl_ref[...]).astype(BF16)

    bd = bd_ref[...]
    kk = k * kk_ref[...]
    norm = jnp.sqrt(_head_sum(kk * kk, bd))
    kk = kk / jnp.maximum(norm, 1e-12)
    km = k * (1.0 + (a - 1.0) * ka_ref[...])
    bonus_ref[...] = _head_sum(r * km * rk_ref[...], bd) * v

    cum = _dot_split_rhs(tri_ref[...], lw)
    e_cum = jnp.exp2(cum)
    e_inv = jnp.exp2(-cum)
    ops_ref[:, 0:G] = (-kk * jnp.exp2(cum - lw)).astype(BF16)
    ops_ref[:, G:2 * G] = (r * e_cum).astype(BF16)
    ops_ref[:, 2 * G:3 * G] = (kk * a * e_inv).astype(BF16)
    ops_ref[:, 3 * G:4 * G] = (km * e_inv).astype(BF16)
    ops_ref[:, 4 * G:5 * G] = v.astype(BF16)
    for c in range(rows // chunk):
        wc_ref[c] = e_cum[(c + 1) * chunk - 1:(c + 1) * chunk, :]

    bg = p_ref[:, RW:RW + G]
    u = p_ref[:, RW + G:RW + 2 * G] * p_ref[:, RW + 2 * G:RW + 3 * G]
    u1 = pltpu.roll(u, 1, 0)
    u2 = pltpu.roll(u, 2, 0)
    rid = row_ids(G)
    if sample:
        s1 = per_sequence(up1_ref[...])
        um1 = jnp.where(rid == 0, s1, u1)
        um2 = jnp.where(rid == 0, per_sequence(up2_ref[...]), jnp.where(rid == 1, s1, u2))
        u_ref[...] = u
        plast_ref[...] = p_ref[:, 0:RW].reshape(rows // 8, 8, RW)[:, 7, :]
    else:
        um1 = jnp.where(rid == 0, carry_u[7:8, :], u1)
        um2 = jnp.where(rid == 0, carry_u[6:7, :], jnp.where(rid == 1, carry_u[7:8, :], u2))
        u_ref[0] = u[rows - 8:, :]
        ptail_ref[0] = p_ref[rows - 8:rows, 0:RW]
    conv = cw_ref[0:1, :] * um2 + cw_ref[1:2, :] * um1 + cw_ref[2:3, :] * u
    conv_ref[...] = (bg * conv).astype(BF16)

    if not sample:
        carry_p[...] = p_ref[rows - 8:rows, 0:RW]
        carry_u[...] = u[rows - 8:, :]


def _prep(proj, weights, tri, *, n_seq, seq_len, rows, chunk, row_block_offset, extra, sample, name):
    t = n_seq * seq_len
    n_chunks = t // chunk
    cpt = rows // chunk
    if sample:
        grid = (t // rows,)
        rmap = lambda i: (i + row_block_offset, 0)
        omap = lambda i: (i, 0)
        omap3 = lambda i: (i, 0, 0)
        cmap = lambda i: (0, 0)
        sem = ("parallel",)
    else:
        tiles = seq_len // rows
        grid = (n_seq, tiles)
        rmap = lambda b, j: (b * tiles + j + row_block_offset, 0)
        omap = lambda b, j: (b * tiles + j, 0)
        omap3 = lambda b, j: (b * tiles + j, 0, 0)
        cmap = lambda b, j: (0, 0)
        sem = ("parallel", "arbitrary")

    const = lambda arr: pl.BlockSpec(arr.shape, cmap)
    in_specs = [pl.BlockSpec((rows, PW), rmap)]
    if sample:
        in_specs += [pl.BlockSpec((rows // 8, RW), omap), pl.BlockSpec((rows // 8, G), omap),
                     pl.BlockSpec((rows // 8, G), omap)]
    else:
        in_specs += [const(e) for e in extra]
    in_specs += [const(w) for w in weights] + [const(tri)]
    args = [proj] + list(extra) + list(weights) + [tri]

    row_out = jax.ShapeDtypeStruct((t, G), F32)
    row_spec = pl.BlockSpec((rows, G), omap)
    row_bf16 = jax.ShapeDtypeStruct((t, G), BF16)
    out_shape = [jax.ShapeDtypeStruct((t, 5 * G), BF16)] + [jax.ShapeDtypeStruct((n_chunks, 1, G), F32), row_bf16, row_out,
                                  row_bf16]
    out_specs = [pl.BlockSpec((rows, 5 * G), omap)] + [pl.BlockSpec((cpt, 1, G), omap3), row_spec, row_spec, row_spec]
    if sample:
        out_shape += [row_out, jax.ShapeDtypeStruct((t // 8, RW), F32)]
        out_specs += [row_spec, pl.BlockSpec((rows // 8, RW), omap)]
        scratch = []
    else:
        out_shape += [jax.ShapeDtypeStruct((t // rows, 8, G), F32),
                      jax.ShapeDtypeStruct((t // rows, 8, RW), F32)]
        out_specs += [pl.BlockSpec((1, 8, G), omap3), pl.BlockSpec((1, 8, RW), omap3)]
        scratch = [pltpu.VMEM((8, RW), F32), pltpu.VMEM((8, G), F32)]
    return pl.pallas_call(
        functools.partial(_prep_kernel, rows=rows, chunk=chunk, sample=sample),
        grid=grid, in_specs=in_specs, out_specs=out_specs, out_shape=out_shape,
        scratch_shapes=scratch,
        compiler_params=pltpu.CompilerParams(dimension_semantics=sem, vmem_limit_bytes=VMEM_LIMIT),
        name=name,
    )(*args)


def _scan_kernel(*refs, n_par, rows, chunk, per_chunk_state, group, n_cast):
    ops_ref, wc_ref, s0_ref = refs[:3]
    ah_ref, rh_ref, bh_ref, kh_ref, v_ref = (ops_ref.at[:, :, n * G:(n + 1) * G] for n in range(5))
    y_ref, s_ref = refs[3 + n_cast:5 + n_cast]
    for w_ref, o_ref in zip(refs[3:3 + n_cast], refs[5 + n_cast:]):
        o_ref[...] = w_ref[...].astype(BF16)

    if per_chunk_state:
        s_in = s0_ref
    else:
        s_in = s_ref

        @pl.when(pl.program_id(1) == 0)
        def _():
            for q in range(n_par):
                s_ref[q] = s0_ref[0]

    n_blk = rows // chunk
    lanes = [slice(h * HEAD_DIM, (h + 1) * HEAD_DIM) for h in range(N_HEADS)]
    bf = lambda x: x.astype(BF16)

    shift = chunk.bit_length() - 1
    rs = lax.broadcasted_iota(jnp.int32, (rows, rows), 0)
    cs = lax.broadcasted_iota(jnp.int32, (rows, rows), 1)
    same = (rs >> shift) == (cs >> shift)
    mask_strict = same & (rs > cs)
    ri = lax.broadcasted_iota(jnp.int32, (rows, 2 * rows), 0)
    ci = lax.broadcasted_iota(jnp.int32, (rows, 2 * rows), 1)
    cj = jnp.where(ci >= rows, ci - rows, ci)
    mask_r = ((ri >> shift) == (cj >> shift)) & (ri >= cj)
    eye = (rs == cs).astype(F32)

    def chunk_rows(x, c):
        return x[c * chunk:(c + 1) * chunk, :]

    pair_lane = lax.broadcasted_iota(jnp.int32, (rows, 2 * HEAD_DIM), 1) >> (HEAD_DIM.bit_length() - 1)

    def pair_masked(ref, q, h):
        hp, hh = divmod(h, 2)
        x2 = ref[q, :, 2 * HEAD_DIM * hp:2 * HEAD_DIM * (hp + 1)].astype(F32)
        return jnp.where(pair_lane == hh, x2, 0.0)

    def unit_group(units):
        per_unit = lambda f: {u: f(*u) for u in units}
        v = per_unit(lambda q, h: v_ref[q, :, lanes[h]])
        if n_blk == 1:
            a = per_unit(lambda q, h: ah_ref[q, :, lanes[h]])
            r = per_unit(lambda q, h: rh_ref[q, :, lanes[h]])
            b = per_unit(lambda q, h: bh_ref[q, :, lanes[h]])
            k = per_unit(lambda q, h: kh_ref[q, :, lanes[h]])
        else:
            a = per_unit(lambda q, h: pair_masked(ah_ref, q, h))
            r = per_unit(lambda q, h: pair_masked(rh_ref, q, h))
            b = per_unit(lambda q, h: pair_masked(bh_ref, q, h))
            k = per_unit(lambda q, h: pair_masked(kh_ref, q, h))
            v32 = {u: v[u].astype(F32) for u in units}
        ar = {u: bf(jnp.concatenate([a[u], r[u]], axis=0)) for u in units}
        bk = {u: bf(jnp.concatenate([b[u], k[u]], axis=0)) for u in units}

        bk_t = {u: jnp.transpose(bk[u]) for u in units}
        gram = {u: _dot(ar[u], bk_t[u]) for u in units}
        l_ab = {u: jnp.where(mask_strict, gram[u][:rows, :rows], 0.0) for u in units}
        l_ak = {u: bf(jnp.where(mask_strict, gram[u][:rows, rows:], 0.0)) for u in units}
        l_r = {u: bf(jnp.where(mask_r, gram[u][rows:, :], 0.0)) for u in units}

        xs, ys = {}, {}
        for u in units:
            q, h = u
            if n_blk == 1:
                st = _dot(ar[u], bf(s_in[q, h]))
                xs[u], ys[u] = st[:rows], st[rows:]
            elif h % 2 == 0:
                u1 = (q, h + 1)
                assert u1 in units
                parts = [_dot_nt(jnp.concatenate([chunk_rows(t, c) for t in (a[u], r[u], a[u1], r[u1])],
                                                 axis=0), s_in[c, h // 2]) for c in range(n_blk)]
                pick = lambda n: jnp.concatenate([p[n * chunk:(n + 1) * chunk] for p in parts], axis=0)
                xs[u], ys[u], xs[u1], ys[u1] = pick(0), pick(1), pick(2), pick(3)

        x = {u: xs[u] + _dot(l_ak[u], v[u]) for u in units}

        t_inv = {u: eye + l_ab[u] for u in units}
        n = 2
        if n < chunk:
            pw = {u: bf(l_ab[u]) for u in units}
            pw = {u: bf(_dot(pw[u], pw[u])) for u in units}
        while n < chunk:
            if 2 * n < chunk:
                both = {u: _dot(jnp.concatenate([bf(t_inv[u]), pw[u]], axis=0), pw[u]) for u in units}
                t_inv = {u: t_inv[u] + both[u][:rows] for u in units}
                pw = {u: bf(both[u][rows:]) for u in units}
            else:
                t_inv = {u: t_inv[u] + _dot(bf(t_inv[u]), pw[u]) for u in units}
            n *= 2

        uu = {u: _dot(bf(t_inv[u]), bf(x[u])) for u in units}
        uv = {u: jnp.concatenate([bf(uu[u]), v[u]], axis=0) for u in units}
        if n_blk == 1:
            both = {u: _dot(jnp.concatenate([l_r[u], bk_t[u]], axis=0), uv[u]) for u in units}
            for u in units:
                q, h = u
                y_ref[q, :, lanes[h]] = ys[u] + both[u][:rows]
            for u in units:
                q, h = u
                w_c = jnp.broadcast_to(wc_ref[q, 0][:, lanes[h]], (HEAD_DIM, HEAD_DIM))
                s_ref[q, h] = (s_in[q, h] + both[u][rows:]) * jnp.transpose(w_c)
        else:
            for u in units:
                q, h = u
                y_ref[q, :, lanes[h]] = ys[u] + _dot(l_r[u], uv[u])

            for q, h in units:
                if h % 2 == 0:
                    u0, u1, hp = (q, h), (q, h + 1), h // 2
                    for c in range(n_blk):
                        uv_c = jnp.concatenate(
                            [chunk_rows(t, c) for t in (uu[u0], v32[u0], uu[u1], v32[u1])], axis=0)
                        bk_c = jnp.concatenate(
                            [chunk_rows(t, c) for t in (b[u0], k[u0], b[u1], k[u1])], axis=0)
                        s_new = s_in[c, hp] + _dot_tn(uv_c, bk_c)
                        s_ref[c, hp] = s_new * wc_ref[q, c][:, 2 * HEAD_DIM * hp:2 * HEAD_DIM * (hp + 1)]

    all_units = [(q, h) for h in range(N_HEADS) for q in range(n_par)]
    for g0 in range(0, len(all_units), group):
        unit_group(all_units[g0:g0 + group])


def _scan(ops, wc, s0, *, n_seq, seq_len, n_par, rows, chunk, per_chunk_state, group,
          name, cast=()):
    t = n_seq * seq_len
    n_blk = rows // chunk
    if per_chunk_state:
        assert seq_len == chunk and n_par == 1
        lead = 1
        grid = (t // rows,)
        rmap = lambda i: (0, i, 0)
        cmap = lambda i: (0, i, 0, 0)
        smap = lambda i: (i, 0, 0, 0)
        s0map = smap
        state_block = (n_blk, N_HEADS // 2, HEAD_DIM, 2 * HEAD_DIM)
        s0_block = state_block
        wc_block = (1, n_blk, 1, G)
        sem = ("parallel",)
    else:
        assert rows == chunk and s0.shape[0] == 1 and n_seq % n_par == 0
        lead = n_seq
        grid = (n_seq // n_par, seq_len // chunk)
        rmap = lambda s, c: (s, c, 0)
        cmap = lambda s, c: (s, c, 0, 0)
        smap = lambda s, c: (s, 0, 0, 0)
        s0map = lambda s, c: (0, 0, 0, 0)
        state_block = (n_par, N_HEADS, HEAD_DIM, HEAD_DIM)
        s0_block = (1, N_HEADS, HEAD_DIM, HEAD_DIM)
        wc_block = (n_par, 1, 1, G)
        sem = ("parallel", "arbitrary")
    row_spec = pl.BlockSpec((n_par, rows, G), rmap)
    ops_spec = pl.BlockSpec((n_par, rows, 5 * G), rmap)
    n_steps = 1
    for extent in grid:
        n_steps *= extent
    step = (lambda i: i) if len(grid) == 1 else (lambda s, c: s * grid[1] + c)
    cast_specs = []
    for w in cast:
        assert w.shape[0] % (16 * n_steps) == 0
        cast_specs.append(pl.BlockSpec((w.shape[0] // n_steps, w.shape[1]), lambda *g: (step(*g), 0)))
    y, s_out, *narrowed = pl.pallas_call(
        functools.partial(_scan_kernel, n_par=n_par, rows=rows, chunk=chunk,
                          per_chunk_state=per_chunk_state, group=group, n_cast=len(cast)),
        grid=grid,
        in_specs=([ops_spec, pl.BlockSpec(wc_block, cmap), pl.BlockSpec(s0_block, s0map)]
                  + cast_specs),
        out_specs=[row_spec, pl.BlockSpec(state_block, smap)] + cast_specs,
        out_shape=[jax.ShapeDtypeStruct((lead, t // lead, G), F32),
                   jax.ShapeDtypeStruct((n_seq,) + state_block[1:], F32)]
                  + [jax.ShapeDtypeStruct(w.shape, BF16) for w in cast],
        compiler_params=pltpu.CompilerParams(dimension_semantics=sem, vmem_limit_bytes=VMEM_LIMIT),
        name=name,
    )(ops.reshape(lead, t // lead, 5 * G), wc.reshape(lead, -1, 1, G), s0, *cast)
    return (y.reshape(t, G), s_out, *narrowed)


def _mix_kernel(*refs, segs):
    n = len(segs)
    row_refs = [refs[4 * s:4 * s + 4] for s in range(n)]
    x_ref, lnw_ref, lnb_ref, bd_ref, wo_ref, o_ref = refs[4 * n:]
    i = pl.program_id(0)

    def body(y_ref, bonus_ref, g_ref, conv_ref):
        bd = bd_ref[...]
        y = y_ref[...]
        mu = _head_sum(y, bd) * (1.0 / HEAD_DIM)
        d = y - mu
        var = _head_sum(d * d, bd) * (1.0 / HEAD_DIM)
        yn = d * lax.rsqrt(var + GN_EPS) * lnw_ref[...] + lnb_ref[...]
        rw = ((yn + bonus_ref[...]) * g_ref[...]).astype(BF16)
        mix = _dot(rw, wo_ref[0:G, :]) + _dot(conv_ref[...], wo_ref[G:2 * G, :])
        o_ref[...] = x_ref[...] + mix

    for rr, (off, cnt) in zip(row_refs, segs):
        @pl.when((i >= off) & (i < off + cnt))
        def _(rr=rr):
            body(*rr)


def _mix(row_groups, x1, lnw, lnb, bd, wo):
    segs, n_tiles = _segments([grp[0] for grp in row_groups], TM_MIX)
    cmap = lambda i: (0, 0)
    in_specs, args = [], []
    for grp, (off, cnt) in zip(row_groups, segs):
        smap = lambda i, off=off, cnt=cnt: (jnp.clip(i - off, 0, cnt - 1), 0)
        in_specs += [pl.BlockSpec((TM_MIX, G), smap)] * 4
        args += list(grp)
    in_specs += [pl.BlockSpec((TM_MIX, D_MODEL), lambda i: (i, 0)),
                 pl.BlockSpec((1, G), cmap), pl.BlockSpec((1, G), cmap),
                 pl.BlockSpec(bd.shape, cmap), pl.BlockSpec((D_MODEL, D_MODEL), cmap)]
    return pl.pallas_call(
        functools.partial(_mix_kernel, segs=segs),
        grid=(n_tiles,), in_specs=in_specs,
        out_specs=pl.BlockSpec((TM_MIX, D_MODEL), lambda i: (i, 0)),
        out_shape=jax.ShapeDtypeStruct((n_tiles * TM_MIX, D_MODEL), F32),
        compiler_params=pltpu.CompilerParams(
            dimension_semantics=("parallel",), vmem_limit_bytes=VMEM_LIMIT),
        name="mix",
    )(*args, x1, lnw, lnb, bd, wo)


def _pack_rwkv(a):
    r, wd, k, v, ad, gd = jnp.split(a, RWKV_SPLITS, axis=-1)
    zw = jnp.zeros(a.shape[:-1] + (LORA_PAD - W_LORA,), a.dtype)
    za = jnp.zeros(a.shape[:-1] + (LORA_PAD - A_LORA,), a.dtype)
    return jnp.concatenate([r, k, v, wd, zw, ad, za, gd], axis=-1)


def _unpack_rwkv(p):
    return jnp.concatenate([
        p[..., 0:G], p[..., OFF_WD:OFF_WD + W_LORA], p[..., G:2 * G], p[..., 2 * G:3 * G],
        p[..., OFF_AD:OFF_AD + A_LORA], p[..., OFF_GD:OFF_GD + G_LORA]], axis=-1)


def _block_tri(n, chunk):
    i = jnp.arange(n)
    return ((i[:, None] // chunk == i[None, :] // chunk) & (i[:, None] >= i[None, :])).astype(BF16)


def kernel(x_prompt, x_sample, state_wkv, state_shift, state_conv, meta_tokens, g_ffn1, ffn1_gate, ffn1_up, ffn1_down, g_mix, w_in, mu_shift, w0, w_lora_w, a0, w_lora_a, w_lora_g, k_k, k_a, r_k, ln_x_w, ln_x_b, conv_w, w_out, g_ffn2, ffn2_gate, ffn2_up, ffn2_down, g_final):
    assert g_ffn1.shape[0] == 1, "single layer"
    nb, seq, _ = x_prompt.shape
    db, dseq, _ = x_sample.shape
    assert dseq == C_SAMPLE and N_META <= C_PROMPT and seq % ROWS_PREP == 0
    tp, ts = nb * seq, db * dseq
    assert tp % TM == 0 and ts % TM == 0

    row = lambda a: a.reshape(1, -1).astype(F32)
    pad_rows = lambda w, n: jnp.concatenate([w, jnp.zeros((n - w.shape[0], w.shape[1]), w.dtype)], axis=0)
    hid = jnp.arange(HEAD_BLOCK) // HEAD_DIM
    bd = (hid[:, None] == hid[None, :]).astype(BF16)
    prep_w = (_pack_rwkv(mu_shift[0])[None], row(w0[0]), pad_rows(w_lora_w[0], LORA_PAD).astype(BF16),
              row(a0[0]), pad_rows(w_lora_a[0], LORA_PAD).astype(BF16), w_lora_g[0].astype(BF16),
              row(k_k[0]), row(k_a[0]), row(r_k[0]), conv_w[0].astype(F32), bd)

    x_meta = jnp.concatenate([jnp.zeros((C_PROMPT - N_META, D_MODEL), F32), meta_tokens.astype(F32)], axis=0)
    x1_meta, wg1, wu1, wd1 = _ffn_narrow(x_meta, row(g_ffn1[0]), ffn1_gate[0], ffn1_up[0], ffn1_down[0],
                                         name="ffn1_meta")
    x1, w_in_p = _ffn_two_stage([x_prompt.reshape(tp, D_MODEL), x_sample.reshape(ts, D_MODEL)],
                                row(g_ffn1[0]), wg1, wu1, wd1, None, [tp + ts], name="ffn1",
                                pack_src=jnp.transpose(w_in[0]))
    x1_meta = jnp.concatenate([x1_meta, jnp.zeros((TM - C_PROMPT, D_MODEL), F32)], axis=0)
    proj = _proj([x1, x1_meta], row(g_mix[0]), w_in_p, name="proj")
    meta_row0 = tp + ts

    tri_p = _block_tri(ROWS_PREP, C_PROMPT)
    zeros_state = jnp.zeros((1, N_HEADS, HEAD_DIM, HEAD_DIM), F32)
    (ops, wc,_, _, _, utail_m, ptail_m) = _prep(
        proj, prep_w, tri_p[:C_PROMPT, :C_PROMPT], n_seq=1, seq_len=C_PROMPT, rows=C_PROMPT,
        chunk=C_PROMPT, row_block_offset=meta_row0 // C_PROMPT,
        extra=(jnp.zeros((8, RW), F32), jnp.zeros((8, G), F32)),
        sample=False, name="prep_meta")
    _, wkv_m = _scan(ops, wc,zeros_state, n_seq=1, seq_len=C_PROMPT, n_par=1,
                     rows=C_PROMPT, chunk=C_PROMPT, per_chunk_state=False, group=N_HEADS,
                     name="scan_meta")

    (ops, wc,g_p, bonus_p, conv_p, utail_p, ptail_p) = _prep(
        proj, prep_w, tri_p, n_seq=nb, seq_len=seq, rows=ROWS_PREP, chunk=C_PROMPT,
        row_block_offset=0, extra=(ptail_m[0], utail_m[0]), sample=False, name="prep_prompt")
    y_p, wkv_p, wg2, wu2, wd2, wo = _scan(
        ops, wc, wkv_m, n_seq=nb, seq_len=seq, n_par=SCAN_PAR, rows=C_PROMPT,
        chunk=C_PROMPT, per_chunk_state=False, group=SCAN_PAR * N_HEADS // 2, name="scan_prompt",
        cast=(ffn2_gate[0], ffn2_up[0], ffn2_down[0], w_out[0]))

    prev = _pack_rwkv(state_shift[0])
    up1 = state_conv[0][:, 1]
    up2 = state_conv[0][:, 0]
    (ops, wc,g_s, bonus_s, conv_s, u_s, plast_s) = _prep(
        proj, prep_w, _block_tri(ROWS_PREP, C_SAMPLE), n_seq=db, seq_len=dseq, rows=ROWS_PREP,
        chunk=C_SAMPLE, row_block_offset=tp // ROWS_PREP, extra=(prev, up1, up2), sample=True,
        name="prep_sample")
    pair = (db, N_HEADS // 2, 2, HEAD_DIM, HEAD_DIM)
    s0_pairs = jnp.transpose(state_wkv[0].reshape(pair), (0, 1, 3, 2, 4)).reshape(
        db, N_HEADS // 2, HEAD_DIM, 2 * HEAD_DIM)
    y_s, wkv_s = _scan(ops, wc,s0_pairs, n_seq=db, seq_len=dseq, n_par=1,
                       rows=ROWS_SCAN_SAMPLE, chunk=C_SAMPLE, per_chunk_state=True, group=16,
                       name="scan_sample")

    x2 = _mix([(y_p, bonus_p, g_p, conv_p), (y_s, bonus_s, g_s, conv_s)], x1,
              row(ln_x_w[0]), row(ln_x_b[0]), bd, wo)
    y_prompt, y_sample = _ffn_two_stage([x2], row(g_ffn2[0]), wg2, wu2, wd2, row(g_final), [tp, ts],
                                        name="ffn2")

    shift_p = _unpack_rwkv(ptail_p.reshape(nb, seq // ROWS_PREP, 8, RW)[:, -1, 7, :])
    conv_state_p = utail_p.reshape(nb, seq // ROWS_PREP, 8, G)[:, -1, 6:, :]
    shift_s = _unpack_rwkv(plast_s)
    conv_state_s = u_s.reshape(db, dseq, G)[:, -2:, :]
    wkv_s_out = jnp.transpose(wkv_s.reshape(db, N_HEADS // 2, HEAD_DIM, 2, HEAD_DIM),
                              (0, 1, 3, 2, 4)).reshape(db, N_HEADS, HEAD_DIM, HEAD_DIM)
    return (y_prompt.reshape(nb, seq, D_MODEL), y_sample.reshape(db, dseq, D_MODEL),
            jnp.swapaxes(wkv_p, -1, -2)[None].astype(state_wkv.dtype),
            shift_p[None].astype(state_shift.dtype),
            conv_state_p[None].astype(state_conv.dtype),
            wkv_s_out[None].astype(state_wkv.dtype), shift_s[None].astype(state_shift.dtype),
            conv_state_s[None].astype(state_conv.dtype))
```

```python
import functools

import jax
import jax.numpy as jnp
from jax import lax
from jax.experimental import pallas as pl
from jax.experimental.pallas import tpu as pltpu

F32 = jnp.float32
BF16 = jnp.bfloat16

D_MODEL = 2048
D_FF = 5632
N_META = 16
G = 1024
HEAD_DIM = 64
N_HEADS = G // HEAD_DIM
MXU_N = 256
HEAD_BLOCK = MXU_N
W_LORA = 96
A_LORA = 96
G_LORA = 256
LORA_PAD = 128
RWKV_PROJ = 3 * G + W_LORA + A_LORA + G_LORA
RWKV_SPLITS = (G, G + W_LORA, 2 * G + W_LORA, 3 * G + W_LORA, 3 * G + W_LORA + A_LORA)
RW = 3 * G + 2 * LORA_PAD + G_LORA
PW = RW + 3 * G
OFF_WD, OFF_AD, OFF_GD = 3 * G, 3 * G + LORA_PAD, 3 * G + 2 * LORA_PAD
RMS_EPS = 1e-6
GN_EPS = 64e-5
EXP_M05_LOG2E = 0.6065306597126334 * 1.4426950408889634

C_PROMPT = 64
C_SAMPLE = 8
TM = 512
TF = 512
TN = PW // 2
assert TN % MXU_N == 0
TM_FFN_UP = 512
TM_FFN_DOWN = 256
TM_MIX = 256
ROWS_PREP = 128
ROWS_SCAN_SAMPLE = 128
SCAN_PAR = 4
VMEM_LIMIT = 56 * 1024 * 1024


def _dot(a, b):
    return jnp.dot(a, b, preferred_element_type=F32)


def _dot_nt(a, b):
    return lax.dot_general(a, b, (((1,), (1,)), ((), ())), preferred_element_type=F32)


def _dot_tn(a, b):
    return lax.dot_general(a, b, (((0,), (0,)), ((), ())), preferred_element_type=F32)


def _split(a):
    hi = a.astype(BF16)
    lo = (a - hi.astype(F32)).astype(BF16)
    return hi, lo


def _head_sum(x, bd):
    xb = x.astype(BF16)
    wb = bd.shape[0]
    return jnp.concatenate([_dot(xb[:, c:c + wb], bd) for c in range(0, x.shape[1], wb)], axis=1)


def _dot_split_rhs(a_bf16, b):
    hi, lo = _split(b)
    return _dot(a_bf16, hi) + _dot(a_bf16, lo)


def _rms(x, g):
    return x * lax.rsqrt(jnp.mean(x * x, axis=-1, keepdims=True) + RMS_EPS) * g


def _segments(arrays, tile):
    segs, off = [], 0
    for arr in arrays:
        n = arr.shape[0] // tile
        assert n * tile == arr.shape[0]
        segs.append((off, n))
        off += n
    return segs, off


def _overlaps(in_segs, out_segs):
    for a, (ao, an) in enumerate(in_segs):
        for b, (bo, bn) in enumerate(out_segs):
            lo, hi = max(ao, bo), min(ao + an, bo + bn)
            if lo < hi:
                yield a, b, lo, hi


def _ffn_narrow_kernel(x_ref, g_ref, wg_ref, wu_ref, wd_ref, o_ref, wg_o, wu_o, wd_o, h_ref, acc_ref):
    j = pl.program_id(0)

    @pl.when(j == 0)
    def _():
        h_ref[...] = _rms(x_ref[...], g_ref[...]).astype(BF16)
        acc_ref[...] = jnp.zeros_like(acc_ref)

    wg, wu, wd = wg_ref[...].astype(BF16), wu_ref[...].astype(BF16), wd_ref[...].astype(BF16)
    wg_o[...], wu_o[...], wd_o[...] = wg, wu, wd
    h = h_ref[...]
    gate = _dot(h, wg)
    up = _dot(h, wu)
    act = (gate * jax.nn.sigmoid(gate) * up).astype(BF16)
    acc_ref[...] += _dot(act, wd)

    @pl.when(j == pl.num_programs(0) - 1)
    def _():
        o_ref[...] = x_ref[...] + 0.5 * acc_ref[...]


def _ffn_narrow(x, g, wg, wu, wd, *, name):
    rows = x.shape[0]
    whole = pl.BlockSpec((rows, D_MODEL), lambda j: (0, 0))
    w_specs = [pl.BlockSpec((D_MODEL, TF), lambda j: (0, j)),
               pl.BlockSpec((D_MODEL, TF), lambda j: (0, j)),
               pl.BlockSpec((TF, D_MODEL), lambda j: (j, 0))]
    return pl.pallas_call(
        _ffn_narrow_kernel,
        grid=(D_FF // TF,),
        in_specs=[whole, pl.BlockSpec((1, D_MODEL), lambda j: (0, 0))] + w_specs,
        out_specs=[whole] + w_specs,
        out_shape=[jax.ShapeDtypeStruct((rows, D_MODEL), F32)]
                  + [jax.ShapeDtypeStruct(w.shape, BF16) for w in (wg, wu, wd)],
        scratch_shapes=[pltpu.VMEM((rows, D_MODEL), BF16), pltpu.VMEM((rows, D_MODEL), F32)],
        compiler_params=pltpu.CompilerParams(
            dimension_semantics=("arbitrary",), vmem_limit_bytes=VMEM_LIMIT),
        name=name,
    )(x, g, wg, wu, wd)


def _ffn_up_kernel(*refs, in_segs):
    n_in = len(in_segs)
    x_refs = refs[:n_in]
    g_ref, wg_ref, wu_ref, a_ref = refs[n_in:]
    i = pl.program_id(1)
    for x_ref, (off, n) in zip(x_refs, in_segs):
        @pl.when((i >= off) & (i < off + n))
        def _(x_ref=x_ref):
            h = _rms(x_ref[...], g_ref[...]).astype(BF16)
            for c in range(0, a_ref.shape[1], MXU_N):
                gate = _dot(h, wg_ref[:, c:c + MXU_N])
                up = _dot(h, wu_ref[:, c:c + MXU_N])
                a_ref[:, c:c + MXU_N] = (gate * jax.nn.sigmoid(gate) * up).astype(BF16)


def _ffn_down_kernel(*refs, in_segs, out_segs, final_norm, pack_steps):
    n_in, n_out = len(in_segs), len(out_segs)
    a_ref = refs[0]
    x_refs = refs[1:1 + n_in]
    wd_ref = refs[1 + n_in]
    rest = refs[2 + n_in:]
    if final_norm:
        gf_ref, rest = rest[0], rest[1:]
    if pack_steps:
        wt_ref, rest = rest[0], rest[1:]
    o_refs = rest[:n_out]
    i = pl.program_id(0)
    if pack_steps:
        @pl.when(i < pack_steps)
        def _():
            _pack_w_in_kernel(wt_ref, rest[n_out])

    for a, b, lo, hi in _overlaps(in_segs, out_segs):
        @pl.when((i >= lo) & (i < hi))
        def _(x_ref=x_refs[a], o_ref=o_refs[b]):
            xo = x_ref[...] + 0.5 * _dot(a_ref[...], wd_ref[...])
            if final_norm:
                xo = _rms(xo, gf_ref[...])
            o_ref[...] = xo


def _ffn_two_stage(xs, g, wg, wu, wd, g_final, out_rows, *, name, pack_src=None):
    final_norm = g_final is not None
    tm_up, tn = TM_FFN_UP, D_FF // 2
    up_segs, up_tiles = _segments(xs, tm_up)
    seg_ji = lambda s: pl.BlockSpec((tm_up, D_MODEL), lambda j, i, s=s: (jnp.clip(i - s[0], 0, s[1] - 1), 0))
    half = pl.BlockSpec((D_MODEL, tn), lambda j, i: (0, j), pipeline_mode=pl.Buffered(1))
    act = pl.pallas_call(
        functools.partial(_ffn_up_kernel, in_segs=up_segs),
        grid=(D_FF // tn, up_tiles),
        in_specs=[seg_ji(s) for s in up_segs] + [pl.BlockSpec((1, D_MODEL), lambda j, i: (0, 0)), half, half],
        out_specs=pl.BlockSpec((tm_up, tn), lambda j, i: (i, j)),
        out_shape=jax.ShapeDtypeStruct((up_tiles * tm_up, D_FF), BF16),
        compiler_params=pltpu.CompilerParams(
            dimension_semantics=("arbitrary", "arbitrary"), vmem_limit_bytes=VMEM_LIMIT),
        name=name + "_up",
    )(*xs, g, wg, wu)

    tm = TM_FFN_DOWN
    in_segs, n_tiles = _segments(xs, tm)
    out_shape = [jax.ShapeDtypeStruct((n, D_MODEL), F32) for n in out_rows]
    out_segs, n_out_tiles = _segments(out_shape, tm)
    assert n_out_tiles == n_tiles and n_tiles * tm == up_tiles * tm_up
    seg_i = lambda s: pl.BlockSpec((tm, D_MODEL), lambda i, s=s: (jnp.clip(i - s[0], 0, s[1] - 1), 0))
    in_specs = ([pl.BlockSpec((tm, D_FF), lambda i: (i, 0))] + [seg_i(s) for s in in_segs]
                + [pl.BlockSpec((D_FF, D_MODEL), lambda i: (0, 0), pipeline_mode=pl.Buffered(1))])
    args = [act] + list(xs) + [wd]
    if final_norm:
        in_specs.append(pl.BlockSpec((1, D_MODEL), lambda i: (0, 0)))
        args.append(g_final)
    out_specs = [seg_i(s) for s in out_segs]
    pack_steps = 0
    if pack_src is not None:
        lane_blk = 128
        pack_steps = D_MODEL // lane_blk
        assert pack_steps <= n_tiles
        blk = lambda i: (0, jnp.minimum(i, pack_steps - 1))
        in_specs.append(pl.BlockSpec((pack_src.shape[0], lane_blk), blk))
        args.append(pack_src)
        out_specs.append(pl.BlockSpec((PW, lane_blk), blk))
        out_shape = out_shape + [jax.ShapeDtypeStruct((PW, D_MODEL), BF16)]
    return pl.pallas_call(
        functools.partial(_ffn_down_kernel, in_segs=in_segs, out_segs=out_segs, final_norm=final_norm,
                          pack_steps=pack_steps),
        grid=(n_tiles,),
        in_specs=in_specs,
        out_specs=out_specs,
        out_shape=out_shape,
        compiler_params=pltpu.CompilerParams(
            dimension_semantics=("arbitrary",), vmem_limit_bytes=VMEM_LIMIT),
        name=name + "_down",
    )(*args)


_PACK_MOVES = (
    (0, 0, G),
    (G, RWKV_SPLITS[1], G),
    (2 * G, RWKV_SPLITS[2], G),
    (OFF_WD, RWKV_SPLITS[0], W_LORA),
    (OFF_AD, RWKV_SPLITS[3], A_LORA),
    (OFF_GD, RWKV_SPLITS[4], G_LORA),
    (RW, RWKV_PROJ, 3 * G),
)


def _pack_w_in_kernel(w_ref, o_ref):
    for dst, src, n in _PACK_MOVES:
        o_ref[dst:dst + n, :] = w_ref[src:src + n, :].astype(BF16)
    for lo, hi in ((OFF_WD + W_LORA, OFF_AD), (OFF_AD + A_LORA, OFF_GD)):
        o_ref[lo:hi, :] = jnp.zeros((hi - lo, o_ref.shape[1]), BF16)


def _proj_kernel(*refs, in_segs):
    n_in = len(in_segs)
    x_refs = refs[:n_in]
    g_ref, w_ref, o_ref = refs[n_in:]
    i = pl.program_id(1)
    for x_ref, (off, n) in zip(x_refs, in_segs):
        @pl.when((i >= off) & (i < off + n))
        def _(x_ref=x_ref):
            h = _rms(x_ref[...], g_ref[...]).astype(BF16)
            o_ref[...] = _dot_nt(h, w_ref[...])


def _proj(xs, g, w, *, name):
    in_segs, n_tiles = _segments(xs, TM)
    seg_ji = lambda s: pl.BlockSpec((TM, D_MODEL), lambda j, i, s=s: (jnp.clip(i - s[0], 0, s[1] - 1), 0),
                                    pipeline_mode=pl.Buffered(1 if s[1] == 1 else 2))
    return pl.pallas_call(
        functools.partial(_proj_kernel, in_segs=in_segs),
        grid=(PW // TN, n_tiles),
        in_specs=[seg_ji(s) for s in in_segs] + [
            pl.BlockSpec((1, D_MODEL), lambda j, i: (0, 0)),
            pl.BlockSpec((TN, D_MODEL), lambda j, i: (j, 0)),
        ],
        out_specs=pl.BlockSpec((TM, TN), lambda j, i: (i, j)),
        out_shape=jax.ShapeDtypeStruct((n_tiles * TM, PW), F32),
        compiler_params=pltpu.CompilerParams(
            dimension_semantics=("parallel", "parallel"), vmem_limit_bytes=VMEM_LIMIT),
        name=name,
    )(*xs, g, w)


def _prep_kernel(*refs, rows, chunk, sample):
    if sample:
        (p_ref, prev_ref, up1_ref, up2_ref, mu_ref, w0_ref, ww_ref, a0_ref, wa_ref, wgl_ref,
         kk_ref, ka_ref, rk_ref, cw_ref, bd_ref, tri_ref,
         ah_ref, rh_ref, bh_ref, kh_ref, v_ref, wc_ref, g_ref, bonus_ref, conv_ref, u_ref,
         plast_ref) = refs
    else:
        (p_ref, cp0_ref, cu0_ref, mu_ref, w0_ref, ww_ref, a0_ref, wa_ref, wgl_ref,
         kk_ref, ka_ref, rk_ref, cw_ref, bd_ref, tri_ref,
         ah_ref, rh_ref, bh_ref, kh_ref, v_ref, wc_ref, g_ref, bonus_ref, conv_ref, u_ref,
         ptail_ref, carry_p, carry_u) = refs

        @pl.when(pl.program_id(1) == 0)
        def _():
            carry_p[...] = cp0_ref[...]
            carry_u[...] = cu0_ref[...]

    def row_ids(width):
        r = lax.broadcasted_iota(jnp.int32, (rows, width), 0)
        return (r & 7) if sample else r

    def per_sequence(x):
        n, w = x.shape
        return jnp.broadcast_to(x[:, None, :], (n, 8, w)).reshape(n * 8, w)

    def shifted_mix(lo, hi):
        p = p_ref[:, lo:hi]
        rolled = pltpu.roll(p, 1, 0)
        if sample:
            sh = jnp.where(row_ids(hi - lo) == 0, per_sequence(prev_ref[:, lo:hi]), rolled)
        else:
            sh = jnp.where(row_ids(hi - lo) == 0, carry_p[7:8, lo:hi], rolled)
        return p + (sh - p) * mu_ref[:, lo:hi]

    r = shifted_mix(0, G)
    k = shifted_mix(G, 2 * G)
    v = shifted_mix(2 * G, 3 * G)
    wd = shifted_mix(OFF_WD, OFF_WD + LORA_PAD)
    ad = shifted_mix(OFF_AD, OFF_AD + LORA_PAD)
    gd = shifted_mix(OFF_GD, OFF_GD + G_LORA)

    z = w0_ref[...] + _dot(jnp.tanh(wd).astype(BF16), ww_ref[...])
    lw = -EXP_M05_LOG2E * jax.nn.sigmoid(z)
    a = jax.nn.sigmoid(a0_ref[...] + _dot(ad.astype(BF16), wa_ref[...]))
    g_ref[...] = _dot(jax.nn.sigmoid(gd).astype(BF16), wgl_ref[...]).astype(BF16)

    bd = bd_ref[...]
    kk = k * kk_ref[...]
    norm = jnp.sqrt(_head_sum(kk * kk, bd))
    kk = kk / jnp.maximum(norm, 1e-12)
    km = k * (1.0 + (a - 1.0) * ka_ref[...])
    bonus_ref[...] = _head_sum(r * km * rk_ref[...], bd) * v

    cum = _dot_split_rhs(tri_ref[...], lw)
    e_cum = jnp.exp2(cum)
    e_inv = jnp.exp2(-cum)
    ah_ref[...] = (-kk * jnp.exp2(cum - lw)).astype(BF16)
    rh_ref[...] = (r * e_cum).astype(BF16)
    bh_ref[...] = (kk * a * e_inv).astype(BF16)
    kh_ref[...] = (km * e_inv).astype(BF16)
    v_ref[...] = v.astype(BF16)
    for c in range(rows // chunk):
        wc_ref[c] = e_cum[(c + 1) * chunk - 1:(c + 1) * chunk, :]

    bg = p_ref[:, RW:RW + G]
    u = p_ref[:, RW + G:RW + 2 * G] * p_ref[:, RW + 2 * G:RW + 3 * G]
    u1 = pltpu.roll(u, 1, 0)
    u2 = pltpu.roll(u, 2, 0)
    rid = row_ids(G)
    if sample:
        s1 = per_sequence(up1_ref[...])
        um1 = jnp.where(rid == 0, s1, u1)
        um2 = jnp.where(rid == 0, per_sequence(up2_ref[...]), jnp.where(rid == 1, s1, u2))
        u_ref[...] = u
        plast_ref[...] = p_ref[:, 0:RW].reshape(rows // 8, 8, RW)[:, 7, :]
    else:
        um1 = jnp.where(rid == 0, carry_u[7:8, :], u1)
        um2 = jnp.where(rid == 0, carry_u[6:7, :], jnp.where(rid == 1, carry_u[7:8, :], u2))
        u_ref[0] = u[rows - 8:, :]
        ptail_ref[0] = p_ref[rows - 8:rows, 0:RW]
    conv = cw_ref[0:1, :] * um2 + cw_ref[1:2, :] * um1 + cw_ref[2:3, :] * u
    conv_ref[...] = (bg * conv).astype(BF16)

    if not sample:
        carry_p[...] = p_ref[rows - 8:rows, 0:RW]
        carry_u[...] = u[rows - 8:, :]


def _prep(proj, weights, tri, *, n_seq, seq_len, rows, chunk, row_block_offset, extra, sample, name):
    t = n_seq * seq_len
    n_chunks = t // chunk
    cpt = rows // chunk
    if sample:
        grid = (t // rows,)
        rmap = lambda i: (i + row_block_offset, 0)
        omap = lambda i: (i, 0)
        omap3 = lambda i: (i, 0, 0)
        cmap = lambda i: (0, 0)
        sem = ("parallel",)
    else:
        tiles = seq_len // rows
        grid = (n_seq, tiles)
        rmap = lambda b, j: (b * tiles + j + row_block_offset, 0)
        omap = lambda b, j: (b * tiles + j, 0)
        omap3 = lambda b, j: (b * tiles + j, 0, 0)
        cmap = lambda b, j: (0, 0)
        sem = ("parallel", "arbitrary")

    const = lambda arr: pl.BlockSpec(arr.shape, cmap)
    in_specs = [pl.BlockSpec((rows, PW), rmap)]
    if sample:
        in_specs += [pl.BlockSpec((rows // 8, RW), omap), pl.BlockSpec((rows // 8, G), omap),
                     pl.BlockSpec((rows // 8, G), omap)]
    else:
        in_specs += [const(e) for e in extra]
    in_specs += [const(w) for w in weights] + [const(tri)]
    args = [proj] + list(extra) + list(weights) + [tri]

    row_out = jax.ShapeDtypeStruct((t, G), F32)
    row_spec = pl.BlockSpec((rows, G), omap)
    row_bf16 = jax.ShapeDtypeStruct((t, G), BF16)
    out_shape = [row_bf16] * 5 + [jax.ShapeDtypeStruct((n_chunks, 1, G), F32), row_bf16, row_out,
                                  row_bf16]
    out_specs = [row_spec] * 5 + [pl.BlockSpec((cpt, 1, G), omap3), row_spec, row_spec, row_spec]
    if sample:
        out_shape += [row_out, jax.ShapeDtypeStruct((t // 8, RW), F32)]
        out_specs += [row_spec, pl.BlockSpec((rows // 8, RW), omap)]
        scratch = []
    else:
        out_shape += [jax.ShapeDtypeStruct((t // rows, 8, G), F32),
                      jax.ShapeDtypeStruct((t // rows, 8, RW), F32)]
        out_specs += [pl.BlockSpec((1, 8, G), omap3), pl.BlockSpec((1, 8, RW), omap3)]
        scratch = [pltpu.VMEM((8, RW), F32), pltpu.VMEM((8, G), F32)]
    return pl.pallas_call(
        functools.partial(_prep_kernel, rows=rows, chunk=chunk, sample=sample),
        grid=grid, in_specs=in_specs, out_specs=out_specs, out_shape=out_shape,
        scratch_shapes=scratch,
        compiler_params=pltpu.CompilerParams(dimension_semantics=sem, vmem_limit_bytes=VMEM_LIMIT),
        name=name,
    )(*args)


def _scan_kernel(*refs, n_par, rows, chunk, per_chunk_state, group, n_cast):
    ah_ref, rh_ref, bh_ref, kh_ref, v_ref, wc_ref, s0_ref = refs[:7]
    y_ref, s_ref = refs[7 + n_cast:9 + n_cast]
    for w_ref, o_ref in zip(refs[7:7 + n_cast], refs[9 + n_cast:]):
        o_ref[...] = w_ref[...].astype(BF16)

    if per_chunk_state:
        s_in = s0_ref
    else:
        s_in = s_ref

        @pl.when(pl.program_id(1) == 0)
        def _():
            for q in range(n_par):
                s_ref[q] = s0_ref[0]

    n_blk = rows // chunk
    lanes = [slice(h * HEAD_DIM, (h + 1) * HEAD_DIM) for h in range(N_HEADS)]
    bf = lambda x: x.astype(BF16)

    shift = chunk.bit_length() - 1
    rs = lax.broadcasted_iota(jnp.int32, (rows, rows), 0)
    cs = lax.broadcasted_iota(jnp.int32, (rows, rows), 1)
    same = (rs >> shift) == (cs >> shift)
    mask_strict = same & (rs > cs)
    ri = lax.broadcasted_iota(jnp.int32, (rows, 2 * rows), 0)
    ci = lax.broadcasted_iota(jnp.int32, (rows, 2 * rows), 1)
    cj = jnp.where(ci >= rows, ci - rows, ci)
    mask_r = ((ri >> shift) == (cj >> shift)) & (ri >= cj)
    eye = (rs == cs).astype(F32)

    def chunk_rows(x, c):
        return x[c * chunk:(c + 1) * chunk, :]

    pair_lane = lax.broadcasted_iota(jnp.int32, (rows, 2 * HEAD_DIM), 1) >> (HEAD_DIM.bit_length() - 1)

    def pair_masked(ref, q, h):
        hp, hh = divmod(h, 2)
        x2 = ref[q, :, 2 * HEAD_DIM * hp:2 * HEAD_DIM * (hp + 1)].astype(F32)
        return jnp.where(pair_lane == hh, x2, 0.0)

    def unit_group(units):
        per_unit = lambda f: {u: f(*u) for u in units}
        v = per_unit(lambda q, h: v_ref[q, :, lanes[h]])
        if n_blk == 1:
            a = per_unit(lambda q, h: ah_ref[q, :, lanes[h]])
            r = per_unit(lambda q, h: rh_ref[q, :, lanes[h]])
            b = per_unit(lambda q, h: bh_ref[q, :, lanes[h]])
            k = per_unit(lambda q, h: kh_ref[q, :, lanes[h]])
        else:
            a = per_unit(lambda q, h: pair_masked(ah_ref, q, h))
            r = per_unit(lambda q, h: pair_masked(rh_ref, q, h))
            b = per_unit(lambda q, h: pair_masked(bh_ref, q, h))
            k = per_unit(lambda q, h: pair_masked(kh_ref, q, h))
            v32 = {u: v[u].astype(F32) for u in units}
        ar = {u: bf(jnp.concatenate([a[u], r[u]], axis=0)) for u in units}
        bk = {u: bf(jnp.concatenate([b[u], k[u]], axis=0)) for u in units}

        bk_t = {u: jnp.transpose(bk[u]) for u in units}
        gram = {u: _dot(ar[u], bk_t[u]) for u in units}
        l_ab = {u: jnp.where(mask_strict, gram[u][:rows, :rows], 0.0) for u in units}
        l_ak = {u: bf(jnp.where(mask_strict, gram[u][:rows, rows:], 0.0)) for u in units}
        l_r = {u: bf(jnp.where(mask_r, gram[u][rows:, :], 0.0)) for u in units}

        xs, ys = {}, {}
        for u in units:
            q, h = u
            if n_blk == 1:
                st = _dot(ar[u], bf(s_in[q, h]))
                xs[u], ys[u] = st[:rows], st[rows:]
            elif h % 2 == 0:
                u1 = (q, h + 1)
                assert u1 in units
                parts = [_dot_nt(jnp.concatenate([chunk_rows(t, c) for t in (a[u], r[u], a[u1], r[u1])],
                                                 axis=0), s_in[c, h // 2]) for c in range(n_blk)]
                pick = lambda n: jnp.concatenate([p[n * chunk:(n + 1) * chunk] for p in parts], axis=0)
                xs[u], ys[u], xs[u1], ys[u1] = pick(0), pick(1), pick(2), pick(3)

        x = {u: xs[u] + _dot(l_ak[u], v[u]) for u in units}

        t_inv = {u: eye + l_ab[u] for u in units}
        n = 2
        if n < chunk:
            pw = {u: bf(l_ab[u]) for u in units}
            pw = {u: bf(_dot(pw[u], pw[u])) for u in units}
        while n < chunk:
            if 2 * n < chunk:
                both = {u: _dot(jnp.concatenate([bf(t_inv[u]), pw[u]], axis=0), pw[u]) for u in units}
                t_inv = {u: t_inv[u] + both[u][:rows] for u in units}
                pw = {u: bf(both[u][rows:]) for u in units}
            else:
                t_inv = {u: t_inv[u] + _dot(bf(t_inv[u]), pw[u]) for u in units}
            n *= 2

        uu = {u: _dot(bf(t_inv[u]), bf(x[u])) for u in units}
        uv = {u: jnp.concatenate([bf(uu[u]), v[u]], axis=0) for u in units}
        if n_blk == 1:
            both = {u: _dot(jnp.concatenate([l_r[u], bk_t[u]], axis=0), uv[u]) for u in units}
            for u in units:
                q, h = u
                y_ref[q, :, lanes[h]] = ys[u] + both[u][:rows]
            for u in units:
                q, h = u
                w_c = jnp.broadcast_to(wc_ref[q, 0][:, lanes[h]], (HEAD_DIM, HEAD_DIM))
                s_ref[q, h] = (s_in[q, h] + both[u][rows:]) * jnp.transpose(w_c)
        else:
            for u in units:
                q, h = u
                y_ref[q, :, lanes[h]] = ys[u] + _dot(l_r[u], uv[u])

            for q, h in units:
                if h % 2 == 0:
                    u0, u1, hp = (q, h), (q, h + 1), h // 2
                    for c in range(n_blk):
                        uv_c = jnp.concatenate(
                            [chunk_rows(t, c) for t in (uu[u0], v32[u0], uu[u1], v32[u1])], axis=0)
                        bk_c = jnp.concatenate(
                            [chunk_rows(t, c) for t in (b[u0], k[u0], b[u1], k[u1])], axis=0)
                        s_new = s_in[c, hp] + _dot_tn(uv_c, bk_c)
                        s_ref[c, hp] = s_new * wc_ref[q, c][:, 2 * HEAD_DIM * hp:2 * HEAD_DIM * (hp + 1)]

    all_units = [(q, h) for h in range(N_HEADS) for q in range(n_par)]
    for g0 in range(0, len(all_units), group):
        unit_group(all_units[g0:g0 + group])


def _scan(ah, rh, bh, kh, v, wc, s0, *, n_seq, seq_len, n_par, rows, chunk, per_chunk_state, group,
          name, cast=()):
    t = n_seq * seq_len
    n_blk = rows // chunk
    if per_chunk_state:
        assert seq_len == chunk and n_par == 1
        lead = 1
        grid = (t // rows,)
        rmap = lambda i: (0, i, 0)
        cmap = lambda i: (0, i, 0, 0)
        smap = lambda i: (i, 0, 0, 0)
        s0map = smap
        state_block = (n_blk, N_HEADS // 2, HEAD_DIM, 2 * HEAD_DIM)
        s0_block = state_block
        wc_block = (1, n_blk, 1, G)
        sem = ("parallel",)
    else:
        assert rows == chunk and s0.shape[0] == 1 and n_seq % n_par == 0
        lead = n_seq
        grid = (n_seq // n_par, seq_len // chunk)
        rmap = lambda s, c: (s, c, 0)
        cmap = lambda s, c: (s, c, 0, 0)
        smap = lambda s, c: (s, 0, 0, 0)
        s0map = lambda s, c: (0, 0, 0, 0)
        state_block = (n_par, N_HEADS, HEAD_DIM, HEAD_DIM)
        s0_block = (1, N_HEADS, HEAD_DIM, HEAD_DIM)
        wc_block = (n_par, 1, 1, G)
        sem = ("parallel", "arbitrary")
    rows3 = lambda x: x.reshape(lead, t // lead, G)
    row_spec = pl.BlockSpec((n_par, rows, G), rmap)
    n_steps = 1
    for extent in grid:
        n_steps *= extent
    step = (lambda i: i) if len(grid) == 1 else (lambda s, c: s * grid[1] + c)
    cast_specs = []
    for w in cast:
        assert w.shape[0] % (16 * n_steps) == 0
        cast_specs.append(pl.BlockSpec((w.shape[0] // n_steps, w.shape[1]), lambda *g: (step(*g), 0)))
    y, s_out, *narrowed = pl.pallas_call(
        functools.partial(_scan_kernel, n_par=n_par, rows=rows, chunk=chunk,
                          per_chunk_state=per_chunk_state, group=group, n_cast=len(cast)),
        grid=grid,
        in_specs=([row_spec] * 5 + [pl.BlockSpec(wc_block, cmap), pl.BlockSpec(s0_block, s0map)]
                  + cast_specs),
        out_specs=[row_spec, pl.BlockSpec(state_block, smap)] + cast_specs,
        out_shape=[jax.ShapeDtypeStruct((lead, t // lead, G), F32),
                   jax.ShapeDtypeStruct((n_seq,) + state_block[1:], F32)]
                  + [jax.ShapeDtypeStruct(w.shape, BF16) for w in cast],
        compiler_params=pltpu.CompilerParams(dimension_semantics=sem, vmem_limit_bytes=VMEM_LIMIT),
        name=name,
    )(rows3(ah), rows3(rh), rows3(bh), rows3(kh), rows3(v), wc.reshape(lead, -1, 1, G), s0, *cast)
    return (y.reshape(t, G), s_out, *narrowed)


def _mix_kernel(*refs, segs):
    n = len(segs)
    row_refs = [refs[4 * s:4 * s + 4] for s in range(n)]
    x_ref, lnw_ref, lnb_ref, bd_ref, wo_ref, o_ref = refs[4 * n:]
    i = pl.program_id(0)

    def body(y_ref, bonus_ref, g_ref, conv_ref):
        bd = bd_ref[...]
        y = y_ref[...]
        mu = _head_sum(y, bd) * (1.0 / HEAD_DIM)
        d = y - mu
        var = _head_sum(d * d, bd) * (1.0 / HEAD_DIM)
        yn = d * lax.rsqrt(var + GN_EPS) * lnw_ref[...] + lnb_ref[...]
        rw = ((yn + bonus_ref[...]) * g_ref[...]).astype(BF16)
        mix = _dot(rw, wo_ref[0:G, :]) + _dot(conv_ref[...], wo_ref[G:2 * G, :])
        o_ref[...] = x_ref[...] + mix

    for rr, (off, cnt) in zip(row_refs, segs):
        @pl.when((i >= off) & (i < off + cnt))
        def _(rr=rr):
            body(*rr)


def _mix(row_groups, x1, lnw, lnb, bd, wo):
    segs, n_tiles = _segments([grp[0] for grp in row_groups], TM_MIX)
    cmap = lambda i: (0, 0)
    in_specs, args = [], []
    for grp, (off, cnt) in zip(row_groups, segs):
        smap = lambda i, off=off, cnt=cnt: (jnp.clip(i - off, 0, cnt - 1), 0)
        in_specs += [pl.BlockSpec((TM_MIX, G), smap)] * 4
        args += list(grp)
    in_specs += [pl.BlockSpec((TM_MIX, D_MODEL), lambda i: (i, 0)),
                 pl.BlockSpec((1, G), cmap), pl.BlockSpec((1, G), cmap),
                 pl.BlockSpec(bd.shape, cmap), pl.BlockSpec((D_MODEL, D_MODEL), cmap)]
    return pl.pallas_call(
        functools.partial(_mix_kernel, segs=segs),
        grid=(n_tiles,), in_specs=in_specs,
        out_specs=pl.BlockSpec((TM_MIX, D_MODEL), lambda i: (i, 0)),
        out_shape=jax.ShapeDtypeStruct((n_tiles * TM_MIX, D_MODEL), F32),
        compiler_params=pltpu.CompilerParams(
            dimension_semantics=("parallel",), vmem_limit_bytes=VMEM_LIMIT),
        name="mix",
    )(*args, x1, lnw, lnb, bd, wo)


def _pack_rwkv(a):
    r, wd, k, v, ad, gd = jnp.split(a, RWKV_SPLITS, axis=-1)
    zw = jnp.zeros(a.shape[:-1] + (LORA_PAD - W_LORA,), a.dtype)
    za = jnp.zeros(a.shape[:-1] + (LORA_PAD - A_LORA,), a.dtype)
    return jnp.concatenate([r, k, v, wd, zw, ad, za, gd], axis=-1)


def _unpack_rwkv(p):
    return jnp.concatenate([
        p[..., 0:G], p[..., OFF_WD:OFF_WD + W_LORA], p[..., G:2 * G], p[..., 2 * G:3 * G],
        p[..., OFF_AD:OFF_AD + A_LORA], p[..., OFF_GD:OFF_GD + G_LORA]], axis=-1)


def _block_tri(n, chunk):
    i = jnp.arange(n)
    return ((i[:, None] // chunk == i[None, :] // chunk) & (i[:, None] >= i[None, :])).astype(BF16)


def kernel(x_prompt, x_sample, state_wkv, state_shift, state_conv, meta_tokens, g_ffn1, ffn1_gate, ffn1_up, ffn1_down, g_mix, w_in, mu_shift, w0, w_lora_w, a0, w_lora_a, w_lora_g, k_k, k_a, r_k, ln_x_w, ln_x_b, conv_w, w_out, g_ffn2, ffn2_gate, ffn2_up, ffn2_down, g_final):
    assert g_ffn1.shape[0] == 1, "single layer"
    nb, seq, _ = x_prompt.shape
    db, dseq, _ = x_sample.shape
    assert dseq == C_SAMPLE and N_META <= C_PROMPT and seq % ROWS_PREP == 0
    tp, ts = nb * seq, db * dseq
    assert tp % TM == 0 and ts % TM == 0

    row = lambda a: a.reshape(1, -1).astype(F32)
    pad_rows = lambda w, n: jnp.concatenate([w, jnp.zeros((n - w.shape[0], w.shape[1]), w.dtype)], axis=0)
    hid = jnp.arange(HEAD_BLOCK) // HEAD_DIM
    bd = (hid[:, None] == hid[None, :]).astype(BF16)
    prep_w = (_pack_rwkv(mu_shift[0])[None], row(w0[0]), pad_rows(w_lora_w[0], LORA_PAD).astype(BF16),
              row(a0[0]), pad_rows(w_lora_a[0], LORA_PAD).astype(BF16), w_lora_g[0].astype(BF16),
              row(k_k[0]), row(k_a[0]), row(r_k[0]), conv_w[0].astype(F32), bd)

    x_meta = jnp.concatenate([jnp.zeros((C_PROMPT - N_META, D_MODEL), F32), meta_tokens.astype(F32)], axis=0)
    x1_meta, wg1, wu1, wd1 = _ffn_narrow(x_meta, row(g_ffn1[0]), ffn1_gate[0], ffn1_up[0], ffn1_down[0],
                                         name="ffn1_meta")
    x1, w_in_p = _ffn_two_stage([x_prompt.reshape(tp, D_MODEL), x_sample.reshape(ts, D_MODEL)],
                                row(g_ffn1[0]), wg1, wu1, wd1, None, [tp + ts], name="ffn1",
                                pack_src=jnp.transpose(w_in[0]))
    x1_meta = jnp.concatenate([x1_meta, jnp.zeros((TM - C_PROMPT, D_MODEL), F32)], axis=0)
    proj = _proj([x1, x1_meta], row(g_mix[0]), w_in_p, name="proj")
    meta_row0 = tp + ts

    tri_p = _block_tri(ROWS_PREP, C_PROMPT)
    zeros_state = jnp.zeros((1, N_HEADS, HEAD_DIM, HEAD_DIM), F32)
    (ah, rh, bh, kh, vv, wc, _, _, _, utail_m, ptail_m) = _prep(
        proj, prep_w, tri_p[:C_PROMPT, :C_PROMPT], n_seq=1, seq_len=C_PROMPT, rows=C_PROMPT,
        chunk=C_PROMPT, row_block_offset=meta_row0 // C_PROMPT,
        extra=(jnp.zeros((8, RW), F32), jnp.zeros((8, G), F32)),
        sample=False, name="prep_meta")
    _, wkv_m = _scan(ah, rh, bh, kh, vv, wc, zeros_state, n_seq=1, seq_len=C_PROMPT, n_par=1,
                     rows=C_PROMPT, chunk=C_PROMPT, per_chunk_state=False, group=N_HEADS,
                     name="scan_meta")

    (ah, rh, bh, kh, vv, wc, g_p, bonus_p, conv_p, utail_p, ptail_p) = _prep(
        proj, prep_w, tri_p, n_seq=nb, seq_len=seq, rows=ROWS_PREP, chunk=C_PROMPT,
        row_block_offset=0, extra=(ptail_m[0], utail_m[0]), sample=False, name="prep_prompt")
    y_p, wkv_p, wg2, wu2, wd2, wo = _scan(
        ah, rh, bh, kh, vv, wc, wkv_m, n_seq=nb, seq_len=seq, n_par=SCAN_PAR, rows=C_PROMPT,
        chunk=C_PROMPT, per_chunk_state=False, group=SCAN_PAR * N_HEADS // 2, name="scan_prompt",
        cast=(ffn2_gate[0], ffn2_up[0], ffn2_down[0], w_out[0]))

    prev = _pack_rwkv(state_shift[0])
    up1 = state_conv[0][:, 1]
    up2 = state_conv[0][:, 0]
    (ah, rh, bh, kh, vv, wc, g_s, bonus_s, conv_s, u_s, plast_s) = _prep(
        proj, prep_w, _block_tri(ROWS_PREP, C_SAMPLE), n_seq=db, seq_len=dseq, rows=ROWS_PREP,
        chunk=C_SAMPLE, row_block_offset=tp // ROWS_PREP, extra=(prev, up1, up2), sample=True,
        name="prep_sample")
    pair = (db, N_HEADS // 2, 2, HEAD_DIM, HEAD_DIM)
    s0_pairs = jnp.transpose(state_wkv[0].reshape(pair), (0, 1, 3, 2, 4)).reshape(
        db, N_HEADS // 2, HEAD_DIM, 2 * HEAD_DIM)
    y_s, wkv_s = _scan(ah, rh, bh, kh, vv, wc, s0_pairs, n_seq=db, seq_len=dseq, n_par=1,
                       rows=ROWS_SCAN_SAMPLE, chunk=C_SAMPLE, per_chunk_state=True, group=16,
                       name="scan_sample")

    x2 = _mix([(y_p, bonus_p, g_p, conv_p), (y_s, bonus_s, g_s, conv_s)], x1,
              row(ln_x_w[0]), row(ln_x_b[0]), bd, wo)
    y_prompt, y_sample = _ffn_two_stage([x2], row(g_ffn2[0]), wg2, wu2, wd2, row(g_final), [tp, ts],
                                        name="ffn2")

    shift_p = _unpack_rwkv(ptail_p.reshape(nb, seq // ROWS_PREP, 8, RW)[:, -1, 7, :])
    conv_state_p = utail_p.reshape(nb, seq // ROWS_PREP, 8, G)[:, -1, 6:, :]
    shift_s = _unpack_rwkv(plast_s)
    conv_state_s = u_s.reshape(db, dseq, G)[:, -2:, :]
    wkv_s_out = jnp.transpose(wkv_s.reshape(db, N_HEADS // 2, HEAD_DIM, 2, HEAD_DIM),
                              (0, 1, 3, 2, 4)).reshape(db, N_HEADS, HEAD_DIM, HEAD_DIM)
    return (y_prompt.reshape(nb, seq, D_MODEL), y_sample.reshape(db, dseq, D_MODEL),
            jnp.swapaxes(wkv_p, -1, -2)[None].astype(state_wkv.dtype),
            shift_p[None].astype(state_shift.dtype),
            conv_state_p[None].astype(state_conv.dtype),
            wkv_s_out[None].astype(state_wkv.dtype), shift_s[None].astype(state_shift.dtype),
            conv_state_s[None].astype(state_conv.dtype))
```
